```python
import jax, jax.numpy as jnp
from jax import lax
import numpy as np

D_MODEL = 1024
BATCH = 8
SEQ = 8192
DEPTH = 2

SSD_HEADS = 16
SSD_HEAD_DIM = 64
SSD_INNER = SSD_HEADS * SSD_HEAD_DIM
SSD_GROUPS = 2
SSD_STATE = 64
SSD_CONV = 4
SSD_CHUNK = 128
SSD_XBC = SSD_INNER + 2 * SSD_GROUPS * SSD_STATE
CONF_WIDTH = 512
CONF_KERNEL = 31
SC_WIDTH = 512
SC_KERNEL = 3
N_BRANCH = 3
D_FF = 2816
FFN_KERNEL = 3
EPS = 1e-6

OFF_Z = SSD_INNER
OFF_XBC = OFF_Z + SSD_XBC
OFF_DT = OFF_XBC + SSD_HEADS
OFF_CONF = OFF_DT + 2 * CONF_WIDTH
OFF_SC = OFF_CONF + 3 * SC_WIDTH
N_IN = OFF_SC + N_BRANCH * D_MODEL
IN_SPLITS = (OFF_Z, OFF_XBC, OFF_DT, OFF_CONF, OFF_SC)

kernel_name = "hybrid_ssd_conformer_shortconv_adaln"


def rms_norm(x, g):
    xf = x.astype(jnp.float32)
    y = xf * lax.rsqrt(jnp.mean(xf * xf, axis=-1, keepdims=True) + EPS)
    return (y * g).astype(x.dtype)


def layer_norm(x, g, b):
    xf = x.astype(jnp.float32)
    mu = jnp.mean(xf, axis=-1, keepdims=True)
    xc = xf - mu
    y = xc * lax.rsqrt(jnp.mean(xc * xc, axis=-1, keepdims=True) + EPS)
    return (y * g + b).astype(x.dtype)


def gated_group_rmsnorm(y, z, g):
    v = (y * jax.nn.silu(z)).astype(jnp.float32)
    v = v.reshape(*v.shape[:-1], SSD_GROUPS, -1)
    v = v * lax.rsqrt(jnp.mean(v * v, axis=-1, keepdims=True) + EPS)
    return (v.reshape(y.shape) * g).astype(z.dtype)


def causal_dwconv(x, w, b=None):
    k = w.shape[0]
    y = lax.conv_general_dilated(
        x, w[:, None, :].astype(x.dtype), window_strides=(1,), padding=[(k - 1, 0)],
        dimension_numbers=('NWC', 'WIO', 'NWC'), feature_group_count=x.shape[-1])
    return y if b is None else y + b


def adaln(c, w, b):
    mod = jax.nn.silu(c) @ w + b
    shift, scale, gate = jnp.split(mod[:, None, :], 3, axis=-1)
    return shift, scale, gate


def ssd_chunked(x, dt, a, b_mat, c_mat):
    bsz, l, h, p = x.shape
    g, n = b_mat.shape[-2:]
    r = h // g
    q = SSD_CHUNK
    nc = l // q
    xc = (x * dt[..., None]).reshape(bsz, nc, q, g, r, p)
    a_dt = (dt * a).astype(jnp.float32).reshape(bsz, nc, q, g, r)
    a_cum = jnp.cumsum(jnp.moveaxis(a_dt, 2, -1), axis=-1)
    bc = b_mat.reshape(bsz, nc, q, g, n)
    cc = c_mat.reshape(bsz, nc, q, g, n)
    causal = jnp.tril(jnp.ones((q, q), dtype=bool))
    seg = a_cum[..., :, None] - a_cum[..., None, :]
    decay = jnp.exp(jnp.where(causal, seg, -jnp.inf))
    cb = jnp.einsum('bclgn,bcsgn->bcgls', cc, bc)
    wts = cb[:, :, :, None] * decay
    y_diag = jnp.einsum('bcgrls,bcsgrp->bclgrp', wts, xc)
    decay_states = jnp.exp(a_cum[..., -1:] - a_cum)
    xd = xc * jnp.moveaxis(decay_states, -1, 2)[..., None]
    states = jnp.einsum('bcsgn,bcsgrp->bcgrpn', bc, xd)
    chunk_decay = jnp.exp(a_cum[..., -1])

    def step(hstate, inp):
        s, d = inp
        return hstate * d[..., None, None] + s, hstate

    init = jnp.zeros((bsz, g, r, p, n), dtype=states.dtype)
    _, prev = lax.scan(step, init, (jnp.moveaxis(states, 1, 0), jnp.moveaxis(chunk_decay, 1, 0)))
    prev = jnp.moveaxis(prev, 0, 1)
    decay_out = jnp.exp(jnp.moveaxis(a_cum, -1, 2))[..., None]
    y_off = jnp.einsum('bclgn,bcgrpn->bclgrp', cc, prev) * decay_out
    return (y_diag + y_off).reshape(bsz, l, h, p)


def token_mixers(h, w_in, b_gate, ssd_conv_w, ssd_conv_b, ssd_dt_bias, ssd_a_log, ssd_d,
                 ssd_norm_g, w_ssd_out, conf_conv_w, conf_conv_b, conf_ln_g, conf_ln_b,
                 w_conf_out, sc_conv_w, w_sc_out, w_o):
    bsz, l, _ = h.shape
    proj = h @ w_in
    z, xbc, dt, conf_in, sc_in, gates = jnp.split(proj, IN_SPLITS, axis=-1)
    xbc = jax.nn.silu(causal_dwconv(xbc, ssd_conv_w, ssd_conv_b))
    xs, bm, cm = jnp.split(xbc, [SSD_INNER, SSD_INNER + SSD_GROUPS * SSD_STATE], axis=-1)
    dt = jax.nn.softplus((dt + ssd_dt_bias).astype(jnp.float32))
    a = -jnp.exp(ssd_a_log.astype(jnp.float32))
    xs = xs.reshape(bsz, l, SSD_HEADS, SSD_HEAD_DIM)
    y = ssd_chunked(xs, dt, a,
                    bm.reshape(bsz, l, SSD_GROUPS, SSD_STATE),
                    cm.reshape(bsz, l, SSD_GROUPS, SSD_STATE))
    y = (y + xs * ssd_d[:, None]).reshape(bsz, l, SSD_INNER)
    y_a = gated_group_rmsnorm(y, z, ssd_norm_g) @ w_ssd_out
    u_val, u_gate = jnp.split(conf_in, 2, axis=-1)
    u = u_val * jax.nn.sigmoid(u_gate)
    u = causal_dwconv(u, conf_conv_w, conf_conv_b)
    u = layer_norm(u, conf_ln_g, conf_ln_b)
    y_b = jax.nn.silu(u) @ w_conf_out
    gb, gc, xv = jnp.split(sc_in, 3, axis=-1)
    y_c = (gb * causal_dwconv(gc * xv, sc_conv_w)) @ w_sc_out
    g_a, g_b, g_c = jnp.split(jax.nn.sigmoid(gates + b_gate), 3, axis=-1)
    merged = g_a * y_a + g_b * y_b + g_c * y_c
    return (merged @ w_o).astype(h.dtype)


def conv_ffn(h, w_up, conv_w, conv_b, w_down):
    u = causal_dwconv(h @ w_up, conv_w, conv_b)
    gate, val = jnp.split(u, 2, axis=-1)
    return (jax.nn.silu(gate) * val) @ w_down


def _fwd_setup_inputs(seed: int = 0) -> dict:
    key = jax.random.key(seed)
    ks = iter(jax.random.split(key, 40))

    def nrm(shape, scale):
        return jax.random.normal(next(ks), shape, jnp.float32) * scale

    def gain(shape):
        return 1.0 + nrm(shape, 0.02)

    dt0 = jnp.exp(jax.random.uniform(next(ks), (DEPTH, SSD_HEADS), jnp.float32,
                                     np.float32(np.log(1e-3)), np.float32(np.log(1e-1))))
    return {
        "x": nrm((BATCH, SEQ, D_MODEL), 1.0),
        "c": nrm((BATCH, D_MODEL), 1.0),
        "ada_mix_w": nrm((DEPTH, D_MODEL, 3 * D_MODEL), D_MODEL ** -0.5),
        "ada_mix_b": nrm((DEPTH, 3 * D_MODEL), 0.02),
        "norm_mix_g": gain((DEPTH, D_MODEL)),
        "w_in": nrm((DEPTH, D_MODEL, N_IN), D_MODEL ** -0.5),
        "b_gate": nrm((DEPTH, N_BRANCH * D_MODEL), 0.02),
        "ssd_conv_w": nrm((DEPTH, SSD_CONV, SSD_XBC), SSD_CONV ** -0.5),
        "ssd_conv_b": nrm((DEPTH, SSD_XBC), 0.02),
        "ssd_dt_bias": dt0 + jnp.log(-jnp.expm1(-dt0)),
        "ssd_a_log": jnp.log(jax.random.uniform(next(ks), (DEPTH, SSD_HEADS), jnp.float32, 1.0, 16.0)),
        "ssd_d": gain((DEPTH, SSD_HEADS)),
        "ssd_norm_g": gain((DEPTH, SSD_INNER)),
        "w_ssd_out": nrm((DEPTH, SSD_INNER, D_MODEL), SSD_INNER ** -0.5),
        "conf_conv_w": nrm((DEPTH, CONF_KERNEL, CONF_WIDTH), CONF_KERNEL ** -0.5),
        "conf_conv_b": nrm((DEPTH, CONF_WIDTH), 0.02),
        "conf_ln_g": gain((DEPTH, CONF_WIDTH)),
        "conf_ln_b": nrm((DEPTH, CONF_WIDTH), 0.02),
        "w_conf_out": nrm((DEPTH, CONF_WIDTH, D_MODEL), CONF_WIDTH ** -0.5),
        "sc_conv_w": nrm((DEPTH, SC_KERNEL, SC_WIDTH), SC_KERNEL ** -0.5),
        "w_sc_out": nrm((DEPTH, SC_WIDTH, D_MODEL), SC_WIDTH ** -0.5),
        "w_o": nrm((DEPTH, D_MODEL, D_MODEL), D_MODEL ** -0.5),
        "ada_ffn_w": nrm((DEPTH, D_MODEL, 3 * D_MODEL), D_MODEL ** -0.5),
        "ada_ffn_b": nrm((DEPTH, 3 * D_MODEL), 0.02),
        "norm_ffn_g": gain((DEPTH, D_MODEL)),
        "w_up": nrm((DEPTH, D_MODEL, 2 * D_FF), D_MODEL ** -0.5),
        "ffn_conv_w": nrm((DEPTH, FFN_KERNEL, 2 * D_FF), FFN_KERNEL ** -0.5),
        "ffn_conv_b": nrm((DEPTH, 2 * D_FF), 0.02),
        "w_down": nrm((DEPTH, D_FF, D_MODEL), D_FF ** -0.5),
        "final_norm_g": gain((D_MODEL,)),
    }


def _fwd_reference(x, c, ada_mix_w, ada_mix_b, norm_mix_g, w_in, b_gate, ssd_conv_w, ssd_conv_b,
              ssd_dt_bias, ssd_a_log, ssd_d, ssd_norm_g, w_ssd_out, conf_conv_w, conf_conv_b,
              conf_ln_g, conf_ln_b, w_conf_out, sc_conv_w, w_sc_out, w_o, ada_ffn_w, ada_ffn_b,
              norm_ffn_g, w_up, ffn_conv_w, ffn_conv_b, w_down, final_norm_g):
    for i in range(DEPTH):
        shift, scale, gate = adaln(c, ada_mix_w[i], ada_mix_b[i])
        h = rms_norm(x, norm_mix_g[i]) * (1 + scale) + shift
        mix = token_mixers(h, w_in[i], b_gate[i], ssd_conv_w[i], ssd_conv_b[i], ssd_dt_bias[i],
                           ssd_a_log[i], ssd_d[i], ssd_norm_g[i], w_ssd_out[i], conf_conv_w[i],
                           conf_conv_b[i], conf_ln_g[i], conf_ln_b[i], w_conf_out[i],
                           sc_conv_w[i], w_sc_out[i], w_o[i])
        x = x + (gate * mix).astype(x.dtype)
        shift, scale, gate = adaln(c, ada_ffn_w[i], ada_ffn_b[i])
        h = rms_norm(x, norm_ffn_g[i]) * (1 + scale) + shift
        x = x + (gate * conv_ffn(h, w_up[i], ffn_conv_w[i], ffn_conv_b[i], w_down[i])).astype(x.dtype)
    return rms_norm(x, final_norm_g)


import jax as _jax
import jax.numpy as _jnp

TWIN_FORMAT = 'train_step'
FWD_PARAMS = ['x', 'c', 'ada_mix_w', 'ada_mix_b', 'norm_mix_g', 'w_in', 'b_gate', 'ssd_conv_w', 'ssd_conv_b', 'ssd_dt_bias', 'ssd_a_log', 'ssd_d', 'ssd_norm_g', 'w_ssd_out', 'conf_conv_w', 'conf_conv_b', 'conf_ln_g', 'conf_ln_b', 'w_conf_out', 'sc_conv_w', 'w_sc_out', 'w_o', 'ada_ffn_w', 'ada_ffn_b', 'norm_ffn_g', 'w_up', 'ffn_conv_w', 'ffn_conv_b', 'w_down', 'final_norm_g']
TWIN_WEIGHTS = ['ada_mix_w', 'ada_mix_b', 'norm_mix_g', 'w_in', 'b_gate', 'ssd_conv_w', 'ssd_conv_b', 'ssd_dt_bias', 'ssd_a_log', 'ssd_d', 'ssd_norm_g', 'w_ssd_out', 'conf_conv_w', 'conf_conv_b', 'conf_ln_g', 'conf_ln_b', 'w_conf_out', 'sc_conv_w', 'w_sc_out', 'w_o', 'ada_ffn_w', 'ada_ffn_b', 'norm_ffn_g', 'w_up', 'ffn_conv_w', 'ffn_conv_b', 'w_down', 'final_norm_g']
TWIN_DIFF_INPUT = 'x'
TWIN_INPUTS = ['x', 'c', 'ada_mix_w', 'ada_mix_b', 'norm_mix_g', 'w_in', 'b_gate', 'ssd_conv_w', 'ssd_conv_b', 'ssd_dt_bias', 'ssd_a_log', 'ssd_d', 'ssd_norm_g', 'w_ssd_out', 'conf_conv_w', 'conf_conv_b', 'conf_ln_g', 'conf_ln_b', 'w_conf_out', 'sc_conv_w', 'w_sc_out', 'w_o', 'ada_ffn_w', 'ada_ffn_b', 'norm_ffn_g', 'w_up', 'ffn_conv_w', 'ffn_conv_b', 'w_down', 'final_norm_g', 'loss_target', 'm_ada_mix_w', 'm_ada_mix_b', 'm_norm_mix_g', 'm_w_in', 'm_b_gate', 'm_ssd_conv_w', 'm_ssd_conv_b', 'm_ssd_dt_bias', 'm_ssd_a_log', 'm_ssd_d', 'm_ssd_norm_g', 'm_w_ssd_out', 'm_conf_conv_w', 'm_conf_conv_b', 'm_conf_ln_g', 'm_conf_ln_b', 'm_w_conf_out', 'm_sc_conv_w', 'm_w_sc_out', 'm_w_o', 'm_ada_ffn_w', 'm_ada_ffn_b', 'm_norm_ffn_g', 'm_w_up', 'm_ffn_conv_w', 'm_ffn_conv_b', 'm_w_down', 'm_final_norm_g', 'v_ada_mix_w', 'v_ada_mix_b', 'v_norm_mix_g', 'v_w_in', 'v_b_gate', 'v_ssd_conv_w', 'v_ssd_conv_b', 'v_ssd_dt_bias', 'v_ssd_a_log', 'v_ssd_d', 'v_ssd_norm_g', 'v_w_ssd_out', 'v_conf_conv_w', 'v_conf_conv_b', 'v_conf_ln_g', 'v_conf_ln_b', 'v_w_conf_out', 'v_sc_conv_w', 'v_w_sc_out', 'v_w_o', 'v_ada_ffn_w', 'v_ada_ffn_b', 'v_norm_ffn_g', 'v_w_up', 'v_ffn_conv_w', 'v_ffn_conv_b', 'v_w_down', 'v_final_norm_g']
TWIN_OUTPUTS = ['loss', 'grad_x', 'grad_ada_mix_w', 'grad_ada_mix_b', 'grad_norm_mix_g', 'grad_w_in', 'grad_b_gate', 'grad_ssd_conv_w', 'grad_ssd_conv_b', 'grad_ssd_dt_bias', 'grad_ssd_a_log', 'grad_ssd_d', 'grad_ssd_norm_g', 'grad_w_ssd_out', 'grad_conf_conv_w', 'grad_conf_conv_b', 'grad_conf_ln_g', 'grad_conf_ln_b', 'grad_w_conf_out', 'grad_sc_conv_w', 'grad_w_sc_out', 'grad_w_o', 'grad_ada_ffn_w', 'grad_ada_ffn_b', 'grad_norm_ffn_g', 'grad_w_up', 'grad_ffn_conv_w', 'grad_ffn_conv_b', 'grad_w_down', 'grad_final_norm_g', 'delta_ada_mix_w', 'delta_ada_mix_b', 'delta_norm_mix_g', 'delta_w_in', 'delta_b_gate', 'delta_ssd_conv_w', 'delta_ssd_conv_b', 'delta_ssd_dt_bias', 'delta_ssd_a_log', 'delta_ssd_d', 'delta_ssd_norm_g', 'delta_w_ssd_out', 'delta_conf_conv_w', 'delta_conf_conv_b', 'delta_conf_ln_g', 'delta_conf_ln_b', 'delta_w_conf_out', 'delta_sc_conv_w', 'delta_w_sc_out', 'delta_w_o', 'delta_ada_ffn_w', 'delta_ada_ffn_b', 'delta_norm_ffn_g', 'delta_w_up', 'delta_ffn_conv_w', 'delta_ffn_conv_b', 'delta_w_down', 'delta_final_norm_g', 'new_m_ada_mix_w', 'new_m_ada_mix_b', 'new_m_norm_mix_g', 'new_m_w_in', 'new_m_b_gate', 'new_m_ssd_conv_w', 'new_m_ssd_conv_b', 'new_m_ssd_dt_bias', 'new_m_ssd_a_log', 'new_m_ssd_d', 'new_m_ssd_norm_g', 'new_m_w_ssd_out', 'new_m_conf_conv_w', 'new_m_conf_conv_b', 'new_m_conf_ln_g', 'new_m_conf_ln_b', 'new_m_w_conf_out', 'new_m_sc_conv_w', 'new_m_w_sc_out', 'new_m_w_o', 'new_m_ada_ffn_w', 'new_m_ada_ffn_b', 'new_m_norm_ffn_g', 'new_m_w_up', 'new_m_ffn_conv_w', 'new_m_ffn_conv_b', 'new_m_w_down', 'new_m_final_norm_g', 'new_v_ada_mix_w', 'new_v_ada_mix_b', 'new_v_norm_mix_g', 'new_v_w_in', 'new_v_b_gate', 'new_v_ssd_conv_w', 'new_v_ssd_conv_b', 'new_v_ssd_dt_bias', 'new_v_ssd_a_log', 'new_v_ssd_d', 'new_v_ssd_norm_g', 'new_v_w_ssd_out', 'new_v_conf_conv_w', 'new_v_conf_conv_b', 'new_v_conf_ln_g', 'new_v_conf_ln_b', 'new_v_w_conf_out', 'new_v_sc_conv_w', 'new_v_w_sc_out', 'new_v_w_o', 'new_v_ada_ffn_w', 'new_v_ada_ffn_b', 'new_v_norm_ffn_g', 'new_v_w_up', 'new_v_ffn_conv_w', 'new_v_ffn_conv_b', 'new_v_w_down', 'new_v_final_norm_g']
TWIN_LEAF_KINDS = {'loss': 'loss', 'grad_x': 'grad_x', 'grad_ada_mix_w': 'grad_w', 'grad_ada_mix_b': 'grad_w', 'grad_norm_mix_g': 'grad_w', 'grad_w_in': 'grad_w', 'grad_b_gate': 'grad_w', 'grad_ssd_conv_w': 'grad_w', 'grad_ssd_conv_b': 'grad_w', 'grad_ssd_dt_bias': 'grad_w', 'grad_ssd_a_log': 'grad_w', 'grad_ssd_d': 'grad_w', 'grad_ssd_norm_g': 'grad_w', 'grad_w_ssd_out': 'grad_w', 'grad_conf_conv_w': 'grad_w', 'grad_conf_conv_b': 'grad_w', 'grad_conf_ln_g': 'grad_w', 'grad_conf_ln_b': 'grad_w', 'grad_w_conf_out': 'grad_w', 'grad_sc_conv_w': 'grad_w', 'grad_w_sc_out': 'grad_w', 'grad_w_o': 'grad_w', 'grad_ada_ffn_w': 'grad_w', 'grad_ada_ffn_b': 'grad_w', 'grad_norm_ffn_g': 'grad_w', 'grad_w_up': 'grad_w', 'grad_ffn_conv_w': 'grad_w', 'grad_ffn_conv_b': 'grad_w', 'grad_w_down': 'grad_w', 'grad_final_norm_g': 'grad_w', 'delta_ada_mix_w': 'delta_w', 'delta_ada_mix_b': 'delta_w', 'delta_norm_mix_g': 'delta_w', 'delta_w_in': 'delta_w', 'delta_b_gate': 'delta_w', 'delta_ssd_conv_w': 'delta_w', 'delta_ssd_conv_b': 'delta_w', 'delta_ssd_dt_bias': 'delta_w', 'delta_ssd_a_log': 'delta_w', 'delta_ssd_d': 'delta_w', 'delta_ssd_norm_g': 'delta_w', 'delta_w_ssd_out': 'delta_w', 'delta_conf_conv_w': 'delta_w', 'delta_conf_conv_b': 'delta_w', 'delta_conf_ln_g': 'delta_w', 'delta_conf_ln_b': 'delta_w', 'delta_w_conf_out': 'delta_w', 'delta_sc_conv_w': 'delta_w', 'delta_w_sc_out': 'delta_w', 'delta_w_o': 'delta_w', 'delta_ada_ffn_w': 'delta_w', 'delta_ada_ffn_b': 'delta_w', 'delta_norm_ffn_g': 'delta_w', 'delta_w_up': 'delta_w', 'delta_ffn_conv_w': 'delta_w', 'delta_ffn_conv_b': 'delta_w', 'delta_w_down': 'delta_w', 'delta_final_norm_g': 'delta_w', 'new_m_ada_mix_w': 'new_m', 'new_m_ada_mix_b': 'new_m', 'new_m_norm_mix_g': 'new_m', 'new_m_w_in': 'new_m', 'new_m_b_gate': 'new_m', 'new_m_ssd_conv_w': 'new_m', 'new_m_ssd_conv_b': 'new_m', 'new_m_ssd_dt_bias': 'new_m', 'new_m_ssd_a_log': 'new_m', 'new_m_ssd_d': 'new_m', 'new_m_ssd_norm_g': 'new_m', 'new_m_w_ssd_out': 'new_m', 'new_m_conf_conv_w': 'new_m', 'new_m_conf_conv_b': 'new_m', 'new_m_conf_ln_g': 'new_m', 'new_m_conf_ln_b': 'new_m', 'new_m_w_conf_out': 'new_m', 'new_m_sc_conv_w': 'new_m', 'new_m_w_sc_out': 'new_m', 'new_m_w_o': 'new_m', 'new_m_ada_ffn_w': 'new_m', 'new_m_ada_ffn_b': 'new_m', 'new_m_norm_ffn_g': 'new_m', 'new_m_w_up': 'new_m', 'new_m_ffn_conv_w': 'new_m', 'new_m_ffn_conv_b': 'new_m', 'new_m_w_down': 'new_m', 'new_m_final_norm_g': 'new_m', 'new_v_ada_mix_w': 'new_v', 'new_v_ada_mix_b': 'new_v', 'new_v_norm_mix_g': 'new_v', 'new_v_w_in': 'new_v', 'new_v_b_gate': 'new_v', 'new_v_ssd_conv_w': 'new_v', 'new_v_ssd_conv_b': 'new_v', 'new_v_ssd_dt_bias': 'new_v', 'new_v_ssd_a_log': 'new_v', 'new_v_ssd_d': 'new_v', 'new_v_ssd_norm_g': 'new_v', 'new_v_w_ssd_out': 'new_v', 'new_v_conf_conv_w': 'new_v', 'new_v_conf_conv_b': 'new_v', 'new_v_conf_ln_g': 'new_v', 'new_v_conf_ln_b': 'new_v', 'new_v_w_conf_out': 'new_v', 'new_v_sc_conv_w': 'new_v', 'new_v_w_sc_out': 'new_v', 'new_v_w_o': 'new_v', 'new_v_ada_ffn_w': 'new_v', 'new_v_ada_ffn_b': 'new_v', 'new_v_norm_ffn_g': 'new_v', 'new_v_w_up': 'new_v', 'new_v_ffn_conv_w': 'new_v', 'new_v_ffn_conv_b': 'new_v', 'new_v_w_down': 'new_v', 'new_v_final_norm_g': 'new_v'}


def _forward(args):
    return _fwd_reference(*[args[k] for k in FWD_PARAMS])


def _output_shape():
    def fwd():
        inp = _fwd_setup_inputs(0)
        return _fwd_reference(*[inp[k] for k in FWD_PARAMS])
    out = _jax.eval_shape(fwd)
    return out.shape, out.dtype

N_MICROBATCH = 1
ADAM_LR = 0.001
ADAM_B1 = 0.9
ADAM_B2 = 0.999
ADAM_EPS = 1e-08
ADAM_WD = 0.01
ADAM_STEP = 10
PER_EXAMPLE_BATCH_AXIS = {'x': 0, 'c': 0, 'loss_target': 0}
SHARED_INPUTS = []
_WEIGHT_DTYPES = {'ada_mix_w': _jnp.float32, 'ada_mix_b': _jnp.float32, 'norm_mix_g': _jnp.float32, 'w_in': _jnp.float32, 'b_gate': _jnp.float32, 'ssd_conv_w': _jnp.float32, 'ssd_conv_b': _jnp.float32, 'ssd_dt_bias': _jnp.float32, 'ssd_a_log': _jnp.float32, 'ssd_d': _jnp.float32, 'ssd_norm_g': _jnp.float32, 'w_ssd_out': _jnp.float32, 'conf_conv_w': _jnp.float32, 'conf_conv_b': _jnp.float32, 'conf_ln_g': _jnp.float32, 'conf_ln_b': _jnp.float32, 'w_conf_out': _jnp.float32, 'sc_conv_w': _jnp.float32, 'w_sc_out': _jnp.float32, 'w_o': _jnp.float32, 'ada_ffn_w': _jnp.float32, 'ada_ffn_b': _jnp.float32, 'norm_ffn_g': _jnp.float32, 'w_up': _jnp.float32, 'ffn_conv_w': _jnp.float32, 'ffn_conv_b': _jnp.float32, 'w_down': _jnp.float32, 'final_norm_g': _jnp.float32}
MOMENT_SCALE = {'ada_mix_w': 1.585667e-01, 'ada_mix_b': 2.722073e-01, 'norm_mix_g': 2.633412e-01, 'w_in': 1.049665e-01, 'b_gate': 3.725908e-02, 'ssd_conv_w': 6.974768e-02, 'ssd_conv_b': 6.391299e-02, 'ssd_dt_bias': 1.907954e-01, 'ssd_a_log': 1.753134e-01, 'ssd_d': 4.162827e-01, 'ssd_norm_g': 6.955904e-02, 'w_ssd_out': 6.887899e-02, 'conf_conv_w': 6.133101e-02, 'conf_conv_b': 9.508544e-02, 'conf_ln_g': 8.552308e-02, 'conf_ln_b': 6.713892e-02, 'w_conf_out': 4.182936e-02, 'sc_conv_w': 2.164283e-01, 'w_sc_out': 1.484365e-01, 'w_o': 1.695113e-01, 'ada_ffn_w': 9.987602e-02, 'ada_ffn_b': 1.729586e-01, 'norm_ffn_g': 1.535771e-01, 'w_up': 7.222420e-02, 'ffn_conv_w': 7.199047e-02, 'ffn_conv_b': 5.397615e-02, 'w_down': 1.166505e-01, 'final_norm_g': 6.516113e+01}


def _to_microbatches(a, axis):
    t = _jnp.moveaxis(a, axis, 0)
    t = t.reshape((N_MICROBATCH, t.shape[0] // N_MICROBATCH) + t.shape[1:])
    return _jnp.moveaxis(t, 1, axis + 1)


def setup_inputs(seed: int = 0) -> dict:
    inp = _fwd_setup_inputs(seed)
    key = _jax.random.fold_in(_jax.random.key(seed), 7919)
    shape, _ = _output_shape()
    out = dict(inp)
    out["loss_target"] = _jax.random.normal(_jax.random.fold_in(key, 0), shape, _jnp.float32)
    for i, name in enumerate(TWIN_WEIGHTS):
        w = inp[name].astype(_jnp.float32)
        if MOMENT_SCALE is None:
            s = _jnp.sqrt(_jnp.mean(_jnp.square(w)) + 1e-30)
        else:
            s = MOMENT_SCALE[name]
        km, kv = _jax.random.split(_jax.random.fold_in(key, i + 1))
        out[name] = w
        out["m_" + name] = s * _jax.random.normal(km, w.shape, _jnp.float32)
        out["v_" + name] = (s * s) * _jax.random.uniform(kv, w.shape, _jnp.float32, 0.5, 1.5)
    if N_MICROBATCH > 1:
        for name, axis in PER_EXAMPLE_BATCH_AXIS.items():
            out[name] = _to_microbatches(out[name], axis)
    return {'x': out['x'], 'c': out['c'], 'ada_mix_w': out['ada_mix_w'], 'ada_mix_b': out['ada_mix_b'], 'norm_mix_g': out['norm_mix_g'], 'w_in': out['w_in'], 'b_gate': out['b_gate'], 'ssd_conv_w': out['ssd_conv_w'], 'ssd_conv_b': out['ssd_conv_b'], 'ssd_dt_bias': out['ssd_dt_bias'], 'ssd_a_log': out['ssd_a_log'], 'ssd_d': out['ssd_d'], 'ssd_norm_g': out['ssd_norm_g'], 'w_ssd_out': out['w_ssd_out'], 'conf_conv_w': out['conf_conv_w'], 'conf_conv_b': out['conf_conv_b'], 'conf_ln_g': out['conf_ln_g'], 'conf_ln_b': out['conf_ln_b'], 'w_conf_out': out['w_conf_out'], 'sc_conv_w': out['sc_conv_w'], 'w_sc_out': out['w_sc_out'], 'w_o': out['w_o'], 'ada_ffn_w': out['ada_ffn_w'], 'ada_ffn_b': out['ada_ffn_b'], 'norm_ffn_g': out['norm_ffn_g'], 'w_up': out['w_up'], 'ffn_conv_w': out['ffn_conv_w'], 'ffn_conv_b': out['ffn_conv_b'], 'w_down': out['w_down'], 'final_norm_g': out['final_norm_g'], 'loss_target': out['loss_target'], 'm_ada_mix_w': out['m_ada_mix_w'], 'm_ada_mix_b': out['m_ada_mix_b'], 'm_norm_mix_g': out['m_norm_mix_g'], 'm_w_in': out['m_w_in'], 'm_b_gate': out['m_b_gate'], 'm_ssd_conv_w': out['m_ssd_conv_w'], 'm_ssd_conv_b': out['m_ssd_conv_b'], 'm_ssd_dt_bias': out['m_ssd_dt_bias'], 'm_ssd_a_log': out['m_ssd_a_log'], 'm_ssd_d': out['m_ssd_d'], 'm_ssd_norm_g': out['m_ssd_norm_g'], 'm_w_ssd_out': out['m_w_ssd_out'], 'm_conf_conv_w': out['m_conf_conv_w'], 'm_conf_conv_b': out['m_conf_conv_b'], 'm_conf_ln_g': out['m_conf_ln_g'], 'm_conf_ln_b': out['m_conf_ln_b'], 'm_w_conf_out': out['m_w_conf_out'], 'm_sc_conv_w': out['m_sc_conv_w'], 'm_w_sc_out': out['m_w_sc_out'], 'm_w_o': out['m_w_o'], 'm_ada_ffn_w': out['m_ada_ffn_w'], 'm_ada_ffn_b': out['m_ada_ffn_b'], 'm_norm_ffn_g': out['m_norm_ffn_g'], 'm_w_up': out['m_w_up'], 'm_ffn_conv_w': out['m_ffn_conv_w'], 'm_ffn_conv_b': out['m_ffn_conv_b'], 'm_w_down': out['m_w_down'], 'm_final_norm_g': out['m_final_norm_g'], 'v_ada_mix_w': out['v_ada_mix_w'], 'v_ada_mix_b': out['v_ada_mix_b'], 'v_norm_mix_g': out['v_norm_mix_g'], 'v_w_in': out['v_w_in'], 'v_b_gate': out['v_b_gate'], 'v_ssd_conv_w': out['v_ssd_conv_w'], 'v_ssd_conv_b': out['v_ssd_conv_b'], 'v_ssd_dt_bias': out['v_ssd_dt_bias'], 'v_ssd_a_log': out['v_ssd_a_log'], 'v_ssd_d': out['v_ssd_d'], 'v_ssd_norm_g': out['v_ssd_norm_g'], 'v_w_ssd_out': out['v_w_ssd_out'], 'v_conf_conv_w': out['v_conf_conv_w'], 'v_conf_conv_b': out['v_conf_conv_b'], 'v_conf_ln_g': out['v_conf_ln_g'], 'v_conf_ln_b': out['v_conf_ln_b'], 'v_w_conf_out': out['v_w_conf_out'], 'v_sc_conv_w': out['v_sc_conv_w'], 'v_w_sc_out': out['v_w_sc_out'], 'v_w_o': out['v_w_o'], 'v_ada_ffn_w': out['v_ada_ffn_w'], 'v_ada_ffn_b': out['v_ada_ffn_b'], 'v_norm_ffn_g': out['v_norm_ffn_g'], 'v_w_up': out['v_w_up'], 'v_ffn_conv_w': out['v_ffn_conv_w'], 'v_ffn_conv_b': out['v_ffn_conv_b'], 'v_w_down': out['v_w_down'], 'v_final_norm_g': out['v_final_norm_g']}


def _loss(weights, diff, rest, loss_target):
    with _jax.named_scope("forward"):
        args = {**rest, TWIN_DIFF_INPUT: diff, **{k: w.astype(_WEIGHT_DTYPES[k]) for k, w in weights.items()}}
        y = _forward(args)
    with _jax.named_scope("loss_head"):
        err = _jnp.square(y.astype(_jnp.float32) - loss_target)
        return 0.5 * _jnp.sum(_jnp.mean(err, axis=-1)) if err.ndim else 0.5 * err


def _adamw(w, g, m, v):
    m = ADAM_B1 * m + (1.0 - ADAM_B1) * g
    v = ADAM_B2 * v + (1.0 - ADAM_B2) * _jnp.square(g)
    m_hat = m / (1.0 - ADAM_B1 ** ADAM_STEP)
    v_hat = v / (1.0 - ADAM_B2 ** ADAM_STEP)
    delta = -ADAM_LR * (m_hat / (_jnp.sqrt(v_hat) + ADAM_EPS) + ADAM_WD * w)
    return delta, m, v


def reference(x, c, ada_mix_w, ada_mix_b, norm_mix_g, w_in, b_gate, ssd_conv_w, ssd_conv_b, ssd_dt_bias, ssd_a_log, ssd_d, ssd_norm_g, w_ssd_out, conf_conv_w, conf_conv_b, conf_ln_g, conf_ln_b, w_conf_out, sc_conv_w, w_sc_out, w_o, ada_ffn_w, ada_ffn_b, norm_ffn_g, w_up, ffn_conv_w, ffn_conv_b, w_down, final_norm_g, loss_target, m_ada_mix_w, m_ada_mix_b, m_norm_mix_g, m_w_in, m_b_gate, m_ssd_conv_w, m_ssd_conv_b, m_ssd_dt_bias, m_ssd_a_log, m_ssd_d, m_ssd_norm_g, m_w_ssd_out, m_conf_conv_w, m_conf_conv_b, m_conf_ln_g, m_conf_ln_b, m_w_conf_out, m_sc_conv_w, m_w_sc_out, m_w_o, m_ada_ffn_w, m_ada_ffn_b, m_norm_ffn_g, m_w_up, m_ffn_conv_w, m_ffn_conv_b, m_w_down, m_final_norm_g, v_ada_mix_w, v_ada_mix_b, v_norm_mix_g, v_w_in, v_b_gate, v_ssd_conv_w, v_ssd_conv_b, v_ssd_dt_bias, v_ssd_a_log, v_ssd_d, v_ssd_norm_g, v_w_ssd_out, v_conf_conv_w, v_conf_conv_b, v_conf_ln_g, v_conf_ln_b, v_w_conf_out, v_sc_conv_w, v_w_sc_out, v_w_o, v_ada_ffn_w, v_ada_ffn_b, v_norm_ffn_g, v_w_up, v_ffn_conv_w, v_ffn_conv_b, v_w_down, v_final_norm_g):
    given = dict(x=x, c=c, ada_mix_w=ada_mix_w, ada_mix_b=ada_mix_b, norm_mix_g=norm_mix_g, w_in=w_in, b_gate=b_gate, ssd_conv_w=ssd_conv_w, ssd_conv_b=ssd_conv_b, ssd_dt_bias=ssd_dt_bias, ssd_a_log=ssd_a_log, ssd_d=ssd_d, ssd_norm_g=ssd_norm_g, w_ssd_out=w_ssd_out, conf_conv_w=conf_conv_w, conf_conv_b=conf_conv_b, conf_ln_g=conf_ln_g, conf_ln_b=conf_ln_b, w_conf_out=w_conf_out, sc_conv_w=sc_conv_w, w_sc_out=w_sc_out, w_o=w_o, ada_ffn_w=ada_ffn_w, ada_ffn_b=ada_ffn_b, norm_ffn_g=norm_ffn_g, w_up=w_up, ffn_conv_w=ffn_conv_w, ffn_conv_b=ffn_conv_b, w_down=w_down, final_norm_g=final_norm_g, loss_target=loss_target, m_ada_mix_w=m_ada_mix_w, m_ada_mix_b=m_ada_mix_b, m_norm_mix_g=m_norm_mix_g, m_w_in=m_w_in, m_b_gate=m_b_gate, m_ssd_conv_w=m_ssd_conv_w, m_ssd_conv_b=m_ssd_conv_b, m_ssd_dt_bias=m_ssd_dt_bias, m_ssd_a_log=m_ssd_a_log, m_ssd_d=m_ssd_d, m_ssd_norm_g=m_ssd_norm_g, m_w_ssd_out=m_w_ssd_out, m_conf_conv_w=m_conf_conv_w, m_conf_conv_b=m_conf_conv_b, m_conf_ln_g=m_conf_ln_g, m_conf_ln_b=m_conf_ln_b, m_w_conf_out=m_w_conf_out, m_sc_conv_w=m_sc_conv_w, m_w_sc_out=m_w_sc_out, m_w_o=m_w_o, m_ada_ffn_w=m_ada_ffn_w, m_ada_ffn_b=m_ada_ffn_b, m_norm_ffn_g=m_norm_ffn_g, m_w_up=m_w_up, m_ffn_conv_w=m_ffn_conv_w, m_ffn_conv_b=m_ffn_conv_b, m_w_down=m_w_down, m_final_norm_g=m_final_norm_g, v_ada_mix_w=v_ada_mix_w, v_ada_mix_b=v_ada_mix_b, v_norm_mix_g=v_norm_mix_g, v_w_in=v_w_in, v_b_gate=v_b_gate, v_ssd_conv_w=v_ssd_conv_w, v_ssd_conv_b=v_ssd_conv_b, v_ssd_dt_bias=v_ssd_dt_bias, v_ssd_a_log=v_ssd_a_log, v_ssd_d=v_ssd_d, v_ssd_norm_g=v_ssd_norm_g, v_w_ssd_out=v_w_ssd_out, v_conf_conv_w=v_conf_conv_w, v_conf_conv_b=v_conf_conv_b, v_conf_ln_g=v_conf_ln_g, v_conf_ln_b=v_conf_ln_b, v_w_conf_out=v_w_conf_out, v_sc_conv_w=v_sc_conv_w, v_w_sc_out=v_w_sc_out, v_w_o=v_w_o, v_ada_ffn_w=v_ada_ffn_w, v_ada_ffn_b=v_ada_ffn_b, v_norm_ffn_g=v_norm_ffn_g, v_w_up=v_w_up, v_ffn_conv_w=v_ffn_conv_w, v_ffn_conv_b=v_ffn_conv_b, v_w_down=v_w_down, v_final_norm_g=v_final_norm_g)
    weights = {n: given[n] for n in TWIN_WEIGHTS}
    shared = {n: given[n] for n in SHARED_INPUTS}
    per_example = {n: given[n] for n in ['x', 'c']}
    grad_fn = _jax.value_and_grad(_loss, argnums=(0, 1))

    def one_microbatch(ex, loss_target):
        ex = dict(ex)
        diff = ex.pop(TWIN_DIFF_INPUT)
        return grad_fn(weights, diff, {**shared, **ex}, loss_target)

    if N_MICROBATCH == 1:
        loss, (grad_w, grad_x) = one_microbatch(per_example, given["loss_target"])
    else:
        def body(carry, xs):
            loss_sum, grad_sum = carry
            l_k, (gw_k, gx_k) = one_microbatch(xs[0], xs[1])
            with _jax.named_scope("update"):
                return (loss_sum + l_k, _jax.tree.map(_jnp.add, grad_sum, gw_k)), gx_k

        init = (_jnp.zeros((), _jnp.float32), _jax.tree.map(_jnp.zeros_like, weights))
        (loss, grad_w), grad_x = _jax.lax.scan(body, init, (per_example, given["loss_target"]))
    with _jax.named_scope("update"):
        delta_w, new_m, new_v = {}, {}, {}
        for n in TWIN_WEIGHTS:
            delta_w[n], new_m[n], new_v[n] = _adamw(weights[n], grad_w[n], given["m_" + n], given["v_" + n])
    return (loss, grad_x, *[grad_w[n] for n in TWIN_WEIGHTS], *[delta_w[n] for n in TWIN_WEIGHTS],
            *[new_m[n] for n in TWIN_WEIGHTS], *[new_v[n] for n in TWIN_WEIGHTS])
```

```python
import functools

import jax
import jax.numpy as jnp
from jax import lax
from jax.experimental import pallas as pl
from jax.experimental.pallas import tpu as pltpu

F32 = jnp.float32
BF16 = jnp.bfloat16
HI = lax.Precision.HIGHEST
MESH = pl.DeviceIdType.MESH

N_DEV = 8
DEPTH = 2
D_MODEL = 1024
SSD_HEADS = 16
SSD_HEAD_DIM = 64
SSD_INNER = 1024
SSD_STATE = 64
SSD_CHUNK = 128
SSD_XBC = 1280
CONF_WIDTH = 512
CONF_KERNEL = 31
SC_WIDTH = 512
D_FF = 2816
EPS = 1e-6
OFF_Z, OFF_XBC, OFF_DT, OFF_CONF, OFF_SC, N_IN = 1024, 2304, 2320, 3344, 4880, 7952

ADAM_LR, ADAM_B1, ADAM_B2, ADAM_EPS, ADAM_WD, ADAM_STEP = 0.001, 0.9, 0.999, 1e-08, 0.01, 10

LANES = 128
SUBLANES = 8
VMEM_LIMIT = 56 * 1024 * 1024
ROW_TILE = 256
PACK_COLS = 1024

NN = (((1,), (0,)), ((), ()))
NT = (((1,), (1,)), ((), ()))
TN = (((0,), (0,)), ((), ()))


def _params(n_axes):
    return pltpu.CompilerParams(dimension_semantics=("arbitrary",) * n_axes, vmem_limit_bytes=VMEM_LIMIT)


def _pc(body, **kw):
    return pl.pallas_call(body, **kw)


def _dot(a, b, dn=NN, precision=None):
    return lax.dot_general(a, b, dn, precision=precision, preferred_element_type=F32)


def _sig(x):
    return 1.0 / (1.0 + jnp.exp(-x))


def _fold(v):
    r, c = v.shape
    return v.reshape(r // SUBLANES, SUBLANES, c).sum(axis=0)


def _tile(n_rows):
    return min(ROW_TILE, n_rows // 2)


def _row(tl, c, col=0):
    return pl.BlockSpec((tl, c), lambda i, col=col: (i, col))


def _prev(tl, hb, c, col=0):
    r = tl // hb
    return pl.BlockSpec((hb, c), lambda i, col=col: (jnp.maximum(i * r - 1, 0), col))


def _next(tl, hb, c, n_rows, col=0):
    r = tl // hb
    last = n_rows // hb - 1
    return pl.BlockSpec((hb, c), lambda i, col=col: (jnp.minimum((i + 1) * r, last), col))


def _const(shape):
    return pl.BlockSpec(shape, lambda i: (0,) * len(shape))


def _sds(shape, dtype=F32):
    return jax.ShapeDtypeStruct(shape, dtype)


def _pick(dim, target):
    if dim <= target:
        return dim
    best = None
    for t in range(LANES, target + 1, LANES):
        if dim % t == 0:
            best = t
    assert best is not None, (dim, target)
    return best


def _matmul(a, b, mode, out_dtype, name, add=None):
    if mode == "nn":
        (m, k), (k2, n) = a.shape, b.shape
    elif mode == "nt":
        (m, k), (n, k2) = a.shape, b.shape
    else:
        (k, m), (k2, n) = a.shape, b.shape
    assert k == k2, (a.shape, b.shape, mode)
    tm, tn, tk = _pick(m, 1024), _pick(n, 1536), _pick(k, 1024)
    nk = k // tk
    dn = {"nn": NN, "nt": NT, "tn": TN}[mode]

    def body(*refs):
        if add is None:
            a_ref, b_ref, o_ref, acc = refs
        else:
            a_ref, b_ref, c_ref, o_ref, acc = refs
        kk = pl.program_id(2)

        @pl.when(kk == 0)
        def _():
            acc[...] = jnp.zeros_like(acc)

        acc[...] += _dot(a_ref[...].astype(BF16), b_ref[...].astype(BF16), dn)

        @pl.when(kk == nk - 1)
        def _():
            r = acc[...]
            if add is not None:
                r = r + c_ref[...]
            o_ref[...] = r.astype(out_dtype)

    a_spec = {"nn": pl.BlockSpec((tm, tk), lambda i, j, kk: (i, kk)),
              "nt": pl.BlockSpec((tm, tk), lambda i, j, kk: (i, kk)),
              "tn": pl.BlockSpec((tk, tm), lambda i, j, kk: (kk, i))}[mode]
    b_spec = {"nn": pl.BlockSpec((tk, tn), lambda i, j, kk: (kk, j)),
              "nt": pl.BlockSpec((tn, tk), lambda i, j, kk: (j, kk)),
              "tn": pl.BlockSpec((tk, tn), lambda i, j, kk: (kk, j))}[mode]
    o_spec = pl.BlockSpec((tm, tn), lambda i, j, kk: (i, j))
    in_specs = [a_spec, b_spec] + ([o_spec] if add is not None else [])
    args = (a, b) + ((add,) if add is not None else ())
    return _pc(body, name=name, out_shape=_sds((m, n), out_dtype), grid=(m // tm, n // tn, nk),
               in_specs=in_specs, out_specs=o_spec, scratch_shapes=[pltpu.VMEM((tm, tn), F32)],
               compiler_params=_params(3))(*args)


def _norm_mod(x, g, scale, shift):
    r = lax.rsqrt(jnp.mean(x * x, axis=-1, keepdims=True) + EPS)
    return ((x * r) * g) * (1.0 + scale) + shift


def _prenorm_first(x, g, scale, shift, name):
    n, d = x.shape
    tl = _tile(n)

    def body(x_ref, g_ref, sc_ref, sh_ref, h_ref):
        h_ref[...] = _norm_mod(x_ref[...], g_ref[...], sc_ref[...], sh_ref[...]).astype(BF16)

    return _pc(body, name=name, out_shape=_sds((n, d), BF16), grid=(n // tl,),
               in_specs=[_row(tl, d)] + [_const((1, d))] * 3, out_specs=_row(tl, d),
               compiler_params=_params(1))(x, g, scale, shift)


def _prenorm_res(x, y, gate, g, scale, shift, name):
    n, d = x.shape
    tl = _tile(n)

    def body(x_ref, y_ref, gate_ref, g_ref, sc_ref, sh_ref, xo_ref, h_ref):
        xn = x_ref[...] + gate_ref[...] * y_ref[...]
        xo_ref[...] = xn
        h_ref[...] = _norm_mod(xn, g_ref[...], sc_ref[...], sh_ref[...]).astype(BF16)

    return _pc(body, name=name, out_shape=(_sds((n, d)), _sds((n, d), BF16)), grid=(n // tl,),
               in_specs=[_row(tl, d), _row(tl, d)] + [_const((1, d))] * 4,
               out_specs=(_row(tl, d), _row(tl, d)), compiler_params=_params(1))(x, y, gate, g, scale, shift)


def _final_loss(x, y, gate, gf, target, name):
    n, d = x.shape
    tl = _tile(n)
    nb = n // tl

    def body(x_ref, y_ref, gate_ref, gf_ref, t_ref, loss_ref, dx_ref, dys_ref, dgate_ref, dgf_ref,
             acc_l, acc_gate, acc_gf):
        i = pl.program_id(0)

        @pl.when(i == 0)
        def _():
            acc_l[...] = jnp.zeros_like(acc_l)
            acc_gate[...] = jnp.zeros_like(acc_gate)
            acc_gf[...] = jnp.zeros_like(acc_gf)

        yv = y_ref[...]
        gate = gate_ref[...]
        gf = gf_ref[...]
        x2 = x_ref[...] + gate * yv
        r = lax.rsqrt(jnp.mean(x2 * x2, axis=-1, keepdims=True) + EPS)
        xn = x2 * r
        e = xn * gf - t_ref[...]
        acc_l[...] += _fold(e * e)
        dy = e * (1.0 / d)
        acc_gf[...] += _fold(dy * xn)
        dxn = dy * gf
        dx = r * (dxn - xn * jnp.mean(dxn * xn, axis=-1, keepdims=True))
        dx_ref[...] = dx
        dys_ref[...] = (dx * gate).astype(BF16)
        acc_gate[...] += _fold(dx * yv)

        @pl.when(i == nb - 1)
        def _():
            loss_ref[...] = jnp.full((SUBLANES, LANES), 0.5 / d, F32) * jnp.sum(acc_l[...])
            dgate_ref[...] = jnp.sum(acc_gate[...], axis=0, keepdims=True)
            dgf_ref[...] = jnp.sum(acc_gf[...], axis=0, keepdims=True)

    return _pc(body, name=name,
               out_shape=(_sds((SUBLANES, LANES)), _sds((n, d)), _sds((n, d), BF16), _sds((1, d)), _sds((1, d))),
               grid=(nb,),
               in_specs=[_row(tl, d), _row(tl, d), _const((1, d)), _const((1, d)), _row(tl, d)],
               out_specs=(_const((SUBLANES, LANES)), _row(tl, d), _row(tl, d), _const((1, d)), _const((1, d))),
               scratch_shapes=[pltpu.VMEM((SUBLANES, d), F32)] * 3,
               compiler_params=_params(1))(x, y, gate, gf, target)


def _norm_bwd(dh, x, dxo, g, scale, name, y_prev=None, gate_prev=None):
    n, d = x.shape
    tl = _tile(n)
    nb = n // tl
    has_prev = y_prev is not None

    def body(*refs):
        if has_prev:
            (dh_ref, x_ref, dxo_ref, g_ref, sc_ref, yp_ref, gp_ref,
             dx_ref, dsh_ref, dsc_ref, dg_ref, dys_ref, dgp_ref, acc_sh, acc_s, acc_gp) = refs
        else:
            (dh_ref, x_ref, dxo_ref, g_ref, sc_ref,
             dx_ref, dsh_ref, dsc_ref, dg_ref, acc_sh, acc_s) = refs
        i = pl.program_id(0)

        @pl.when(i == 0)
        def _():
            acc_sh[...] = jnp.zeros_like(acc_sh)
            acc_s[...] = jnp.zeros_like(acc_s)
            if has_prev:
                acc_gp[...] = jnp.zeros_like(acc_gp)

        x_ = x_ref[...]
        dh_ = dh_ref[...]
        g_ = g_ref[...]
        one_sc = 1.0 + sc_ref[...]
        r = lax.rsqrt(jnp.mean(x_ * x_, axis=-1, keepdims=True) + EPS)
        xn = x_ * r
        dxn = dh_ * (g_ * one_sc)
        dx = dxo_ref[...] + r * (dxn - xn * jnp.mean(dxn * xn, axis=-1, keepdims=True))
        dx_ref[...] = dx
        acc_sh[...] += _fold(dh_)
        acc_s[...] += _fold(dh_ * xn)
        if has_prev:
            dys_ref[...] = (dx * gp_ref[...]).astype(BF16)
            acc_gp[...] += _fold(dx * yp_ref[...])

        @pl.when(i == nb - 1)
        def _():
            s = jnp.sum(acc_s[...], axis=0, keepdims=True)
            dsh_ref[...] = jnp.sum(acc_sh[...], axis=0, keepdims=True)
            dsc_ref[...] = s * g_
            dg_ref[...] = s * one_sc
            if has_prev:
                dgp_ref[...] = jnp.sum(acc_gp[...], axis=0, keepdims=True)

    vec = _sds((1, d))
    in_specs = [_row(tl, d)] * 3 + [_const((1, d))] * 2
    out_shape = [_sds((n, d)), vec, vec, vec]
    out_specs = [_row(tl, d)] + [_const((1, d))] * 3
    scratch = [pltpu.VMEM((SUBLANES, d), F32)] * 2
    args = [dh, x, dxo, g, scale]
    if has_prev:
        in_specs += [_row(tl, d), _const((1, d))]
        out_shape += [_sds((n, d), BF16), vec]
        out_specs += [_row(tl, d), _const((1, d))]
        scratch += [pltpu.VMEM((SUBLANES, d), F32)]
        args += [y_prev, gate_prev]
    return _pc(body, name=name, out_shape=tuple(out_shape), grid=(nb,), in_specs=in_specs,
               out_specs=tuple(out_specs), scratch_shapes=scratch, compiler_params=_params(1))(*args)


CONV_HALO = 8
CONF_HALO = 32


def _ssd_pre(xbc, dt_raw, conv_w, conv_b, dt_bias, name):
    n, c = xbc.shape
    tl = _tile(n)
    hb = CONV_HALO
    k_taps = 4

    def body(x_ref, xp_ref, dt_ref, w_ref, b_ref, dtb_ref, pre_ref, dts_ref, buf):
        i = pl.program_id(0)
        buf[pl.ds(0, hb), :] = jnp.where(i > 0, xp_ref[...], 0.0)
        buf[pl.ds(hb, tl), :] = x_ref[...]
        acc = b_ref[...] + w_ref[0:1, :] * buf[pl.ds(hb - 3, tl), :]
        for k in range(1, k_taps):
            acc = acc + w_ref[k:k + 1, :] * buf[pl.ds(hb - 3 + k, tl), :]
        pre_ref[...] = acc
        v = dt_ref[...] + dtb_ref[...]
        dts_ref[...] = jnp.maximum(v, 0.0) + jnp.log1p(jnp.exp(-jnp.abs(v)))

    return _pc(body, name=name, out_shape=(_sds((n, c)), _sds((n, LANES))), grid=(n // tl,),
               in_specs=[_row(tl, c), _prev(tl, hb, c), _row(tl, LANES), _const((SUBLANES, c)), _const((1, c)),
                         _const((1, LANES))],
               out_specs=(_row(tl, c), _row(tl, LANES)),
               scratch_shapes=[pltpu.VMEM((tl + hb, c), F32)], compiler_params=_params(1))(
        xbc, xbc, dt_raw, conv_w, conv_b, dt_bias)


def _ssd_pre_bwd(dpre, xbc, ddt, dt_raw, conv_w, dt_bias, name):
    n, c = xbc.shape
    tl = _tile(n)
    nb = n // tl
    hb = CONV_HALO
    k_taps = 4

    def body(dp_ref, dpn_ref, x_ref, xp_ref, ddt_ref, dt_ref, w_ref, dtb_ref,
             dx_ref, ddr_ref, dw_ref, db_ref, ddtb_ref, dbuf, xbuf, acc_w, acc_b, acc_dtb):
        i = pl.program_id(0)

        @pl.when(i == 0)
        def _():
            acc_w[...] = jnp.zeros_like(acc_w)
            acc_b[...] = jnp.zeros_like(acc_b)
            acc_dtb[...] = jnp.zeros_like(acc_dtb)

        dp = dp_ref[...]
        dbuf[pl.ds(0, tl), :] = dp
        dbuf[pl.ds(tl, hb), :] = jnp.where(i < nb - 1, dpn_ref[...], 0.0)
        xbuf[pl.ds(0, hb), :] = jnp.where(i > 0, xp_ref[...], 0.0)
        xbuf[pl.ds(hb, tl), :] = x_ref[...]
        dx = w_ref[0:1, :] * dbuf[pl.ds(3, tl), :]
        for k in range(1, k_taps):
            dx = dx + w_ref[k:k + 1, :] * dbuf[pl.ds(3 - k, tl), :]
        dx_ref[...] = dx.astype(BF16)
        for k in range(k_taps):
            acc_w[k] += _fold(dp * xbuf[pl.ds(hb - 3 + k, tl), :])
        acc_b[...] += _fold(dp)
        ddr = ddt_ref[...] * _sig(dt_ref[...] + dtb_ref[...])
        ddr_ref[...] = ddr.astype(BF16)
        acc_dtb[...] += _fold(ddr)

        @pl.when(i == nb - 1)
        def _():
            dw_ref[...] = jnp.zeros_like(dw_ref)
            for k in range(k_taps):
                dw_ref[k:k + 1, :] = jnp.sum(acc_w[k], axis=0, keepdims=True)
            db_ref[...] = jnp.sum(acc_b[...], axis=0, keepdims=True)
            ddtb_ref[...] = jnp.sum(acc_dtb[...], axis=0, keepdims=True)

    return _pc(body, name=name,
               out_shape=(_sds((n, c), BF16), _sds((n, LANES), BF16), _sds((SUBLANES, c)), _sds((1, c)),
                          _sds((1, LANES))),
               grid=(nb,),
               in_specs=[_row(tl, c), _next(tl, hb, c, n), _row(tl, c), _prev(tl, hb, c), _row(tl, LANES),
                         _row(tl, LANES), _const((SUBLANES, c)), _const((1, LANES))],
               out_specs=(_row(tl, c), _row(tl, LANES), _const((SUBLANES, c)), _const((1, c)), _const((1, LANES))),
               scratch_shapes=[pltpu.VMEM((tl + hb, c), F32), pltpu.VMEM((tl + hb, c), F32),
                               pltpu.VMEM((k_taps, SUBLANES, c), F32), pltpu.VMEM((SUBLANES, c), F32),
                               pltpu.VMEM((SUBLANES, LANES), F32)],
               compiler_params=_params(1))(dpre, dpre, xbc, xbc, ddt, dt_raw, conv_w, dt_bias)


def _expand_mat():
    r = lax.broadcasted_iota(jnp.int32, (LANES, SSD_INNER), 0)
    c = lax.broadcasted_iota(jnp.int32, (LANES, SSD_INNER), 1)
    return (jnp.right_shift(c, 6) == r).astype(F32)


def _reduce_mat():
    r = lax.broadcasted_iota(jnp.int32, (SSD_INNER, LANES), 0)
    c = lax.broadcasted_iota(jnp.int32, (SSD_INNER, LANES), 1)
    return (jnp.right_shift(r, 6) == c).astype(F32)


def _ssd_common(pre, dt, alog):
    q = SSD_CHUNK
    sg = _sig(pre)
    act = pre * sg
    lane = lax.broadcasted_iota(jnp.int32, (1, LANES), 1)
    a_neg = jnp.where(lane < SSD_HEADS, -jnp.exp(alog), 0.0)
    rr = lax.broadcasted_iota(jnp.int32, (q, q), 0)
    cc = lax.broadcasted_iota(jnp.int32, (q, q), 1)
    causal = rr >= cc
    cum = _dot(causal.astype(F32), dt * a_neg, precision=HI)
    e_mat = _expand_mat()
    dtx = _dot(dt, e_mat, precision=HI)
    cumx = _dot(cum, e_mat, precision=HI)
    return sg, act, a_neg, causal, cum, e_mat, dtx, cumx


def _ssd_scan(pre, dt, alog, dvec, name):
    n = pre.shape[0]
    q = SSD_CHUNK
    nc = n // q

    def body(pre_ref, dt_ref, alog_ref, d_ref, y_ref, hp_ref, state):
        i = pl.program_id(0)

        @pl.when(i == 0)
        def _():
            state[...] = jnp.zeros_like(state)

        dt_ = dt_ref[...]
        _, act, _, causal, cum, e_mat, dtx, cumx = _ssd_common(pre_ref[...], dt_, alog_ref[...])
        xs = act[:, :SSD_INNER]
        bm = act[:, SSD_INNER:SSD_INNER + LANES]
        cm = act[:, SSD_INNER + LANES:]
        cum_t = cum.T
        clx = cumx[q - 1:q, :]
        xc = xs * dtx
        xd = xc * jnp.exp(clx - cumx)
        doutx = jnp.exp(cumx)
        edec = jnp.exp(clx)
        dx_row = _dot(jnp.broadcast_to(d_ref[...], (SUBLANES, LANES)), e_mat, precision=HI)[0:1, :]
        hp_ref[0] = state[...]
        bb = bm.astype(BF16)
        cb = cm.astype(BF16)
        lane = lax.broadcasted_iota(jnp.int32, (1, LANES), 1)
        row = lax.broadcasted_iota(jnp.int32, (LANES, 1), 0)
        cbs = []
        for g in range(2):
            cg = jnp.where(jnp.right_shift(lane, 6) == g, cm, 0.0).astype(BF16)
            cbs.append(_dot(cg, bb, NT))
        for j in range(SSD_HEADS // 2):
            sl = slice(j * LANES, (j + 1) * LANES)
            g = j // 4
            xcj = xc[:, sl].astype(BF16)
            halves = []
            for half in range(2):
                h = 2 * j + half
                seg = cum[:, h:h + 1] - cum_t[h:h + 1, :]
                w = cbs[g] * jnp.exp(jnp.where(causal, seg, -jnp.inf))
                halves.append(_dot(w.astype(BF16), xcj))
            y_diag = jnp.where(lane < SSD_HEAD_DIM, halves[0], halves[1])
            hj = state[:, sl]
            y_off = doutx[:, sl] * _dot(cb, hj.astype(BF16))
            y_ref[:, sl] = y_diag + y_off + xs[:, sl] * dx_row[:, sl]
            st = _dot(bb, xd[:, sl].astype(BF16), TN)
            state[:, sl] = hj * edec[:, sl] + jnp.where(jnp.right_shift(row, 6) == g, st, 0.0)

    return _pc(body, name=name, out_shape=(_sds((n, SSD_INNER)), _sds((nc, LANES, SSD_INNER))), grid=(nc,),
               in_specs=[_row(q, SSD_XBC), _row(q, LANES), _const((1, LANES)), _const((1, LANES))],
               out_specs=(_row(q, SSD_INNER), pl.BlockSpec((1, LANES, SSD_INNER), lambda i: (i, 0, 0))),
               scratch_shapes=[pltpu.VMEM((LANES, SSD_INNER), F32)], compiler_params=_params(1))(pre, dt, alog, dvec)


def _ssd_scan_bwd(pre, dt, hprev, dy, alog, dvec, name):
    n = pre.shape[0]
    q = SSD_CHUNK
    nc = n // q

    def body(pre_ref, dt_ref, hp_ref, dy_ref, alog_ref, d_ref, dpre_ref, ddt_ref, da_ref, dd_ref,
             d_state, dxc_s, dcx_s, dcl_s, acc_a, acc_d):
        i = pl.program_id(0)

        @pl.when(i == 0)
        def _():
            d_state[...] = jnp.zeros_like(d_state)
            acc_a[...] = jnp.zeros_like(acc_a)
            acc_d[...] = jnp.zeros_like(acc_d)

        pre_ = pre_ref[...]
        dt_ = dt_ref[...]
        sg, act, a_neg, causal, cum, e_mat, dtx, cumx = _ssd_common(pre_, dt_, alog_ref[...])
        r_mat = _reduce_mat()
        xs = act[:, :SSD_INNER]
        bm = act[:, SSD_INNER:SSD_INNER + LANES]
        cm = act[:, SSD_INNER + LANES:]
        cum_t = cum.T
        clx = cumx[q - 1:q, :]
        xc = xs * dtx
        dsx = jnp.exp(clx - cumx)
        doutx = jnp.exp(cumx)
        edec = jnp.exp(clx)
        dx_row = _dot(jnp.broadcast_to(d_ref[...], (SUBLANES, LANES)), e_mat, precision=HI)[0:1, :]
        dy_ = dy_ref[...]
        acc_d[...] += _fold(dy_ * xs)
        bb = bm.astype(BF16)
        cb = cm.astype(BF16)
        lane = lax.broadcasted_iota(jnp.int32, (1, LANES), 1)
        row = lax.broadcasted_iota(jnp.int32, (LANES, 1), 0)
        d_c = jnp.zeros((q, LANES), F32)
        d_b = jnp.zeros((q, LANES), F32)

        for j in range(SSD_HEADS // 2):
            sl = slice(j * LANES, (j + 1) * LANES)
            g = j // 4
            hj = hp_ref[0, :, sl]
            hjb = hj.astype(BF16)
            dyj = dy_[:, sl]
            tj = _dot(cb, hjb)
            dtj = (doutx[:, sl] * dyj).astype(BF16)
            dcx = dyj * tj * doutx[:, sl]
            d_c = d_c + _dot(dtj, hjb, NT)
            dhn = d_state[:, sl]
            dhp = dhn * edec[:, sl] + jnp.where(jnp.right_shift(row, 6) == g, _dot(cb, dtj, TN), 0.0)
            dcl = jnp.sum(dhn * hj, axis=0, keepdims=True) * edec[:, sl]
            dsb = dhn.astype(BF16)
            dxd = _dot(bb, dsb)
            xcj = xc[:, sl]
            dsj = dsx[:, sl]
            d_b = d_b + _dot((xcj * dsj).astype(BF16), dsb, NT)
            dds = dxd * xcj * dsj
            d_state[:, sl] = dhp
            dxc_s[:, sl] = dxd * dsj
            dcx_s[:, sl] = dcx - dds
            dcl_s[:, sl] = jnp.broadcast_to(dcl + jnp.sum(dds, axis=0, keepdims=True), (SUBLANES, LANES))

        dcum_c = jnp.zeros((q, LANES), F32)
        dcum_t = jnp.zeros((LANES, q), F32)
        for g in range(2):
            gmask = jnp.right_shift(lane, 6) == g
            cg = jnp.where(gmask, cm, 0.0).astype(BF16)
            cbg = _dot(cg, bb, NT)
            d_cb = jnp.zeros((q, q), F32)
            for hh in range(SSD_HEADS // 2):
                h = g * (SSD_HEADS // 2) + hh
                j, half = h // 2, h % 2
                sl = slice(j * LANES, (j + 1) * LANES)
                hmask = jnp.right_shift(lane, 6) == half
                seg = cum[:, h:h + 1] - cum_t[h:h + 1, :]
                lm = jnp.exp(jnp.where(causal, seg, -jnp.inf))
                w = cbg * lm
                dyj = dy_[:, sl]
                dw = _dot(jnp.where(hmask, dyj, 0.0).astype(BF16), xc[:, sl].astype(BF16), NT)
                dxch = _dot(w.astype(BF16), dyj.astype(BF16), TN)
                dxc_s[:, sl] += jnp.where(hmask, dxch, 0.0)
                d_cb = d_cb + dw * lm
                m = dw * w
                dcum_c = dcum_c + jnp.sum(m, axis=1, keepdims=True) * (lane == h).astype(F32)
                dcum_t = dcum_t + (row == h).astype(F32) * jnp.sum(m, axis=0, keepdims=True)
            d_cbb = d_cb.astype(BF16)
            d_c = d_c + jnp.where(gmask, _dot(d_cbb, bb), 0.0)
            d_b = d_b + jnp.where(gmask, _dot(d_cbb, cb, TN), 0.0)

        dcl_row = _dot(dcl_s[...], r_mat, precision=HI)[0:1, :]
        rowq = lax.broadcasted_iota(jnp.int32, (q, 1), 0)
        dcum = (dcum_c - dcum_t.T + _dot(dcx_s[...], r_mat, precision=HI)
                + jnp.where(rowq == q - 1, dcl_row, 0.0))
        rr = lax.broadcasted_iota(jnp.int32, (q, q), 0)
        cc = lax.broadcasted_iota(jnp.int32, (q, q), 1)
        dadt = _dot((rr <= cc).astype(F32), dcum, precision=HI)
        dxc = dxc_s[...]
        ddt_ref[...] = dadt * a_neg + _dot(dxc * xs, r_mat, precision=HI)
        acc_a[...] += _fold(dadt * dt_)
        dsilu = sg * (1.0 + pre_ * (1.0 - sg))
        dpre_ref[:, :SSD_INNER] = (dxc * dtx + dy_ * dx_row) * dsilu[:, :SSD_INNER]
        dpre_ref[:, SSD_INNER:SSD_INNER + LANES] = d_b * dsilu[:, SSD_INNER:SSD_INNER + LANES]
        dpre_ref[:, SSD_INNER + LANES:] = d_c * dsilu[:, SSD_INNER + LANES:]

        @pl.when(i == nc - 1)
        def _():
            da_ref[...] = jnp.sum(acc_a[...], axis=0, keepdims=True) * a_neg
            dd_ref[...] = jnp.sum(_dot(acc_d[...], r_mat, precision=HI), axis=0, keepdims=True)

    rev = lambda i: (nc - 1 - i, 0)
    return _pc(body, name=name,
               out_shape=(_sds((n, SSD_XBC)), _sds((n, LANES)), _sds((1, LANES)), _sds((1, LANES))), grid=(nc,),
               in_specs=[pl.BlockSpec((q, SSD_XBC), rev), pl.BlockSpec((q, LANES), rev),
                         pl.BlockSpec((1, LANES, SSD_INNER), lambda i: (nc - 1 - i, 0, 0)),
                         pl.BlockSpec((q, SSD_INNER), rev), _const((1, LANES)), _const((1, LANES))],
               out_specs=(pl.BlockSpec((q, SSD_XBC), rev), pl.BlockSpec((q, LANES), rev), _const((1, LANES)),
                          _const((1, LANES))),
               scratch_shapes=[pltpu.VMEM((LANES, SSD_INNER), F32), pltpu.VMEM((q, SSD_INNER), F32),
                               pltpu.VMEM((q, SSD_INNER), F32), pltpu.VMEM((SUBLANES, SSD_INNER), F32),
                               pltpu.VMEM((SUBLANES, LANES), F32), pltpu.VMEM((SUBLANES, SSD_INNER), F32)],
               compiler_params=_params(1))(pre, dt, hprev, dy, alog, dvec)


def _group_norm_parts(v):
    half = SSD_INNER // 2
    r0 = lax.rsqrt(jnp.mean(v[:, :half] * v[:, :half], axis=-1, keepdims=True) + EPS)
    r1 = lax.rsqrt(jnp.mean(v[:, half:] * v[:, half:], axis=-1, keepdims=True) + EPS)
    lane = lax.broadcasted_iota(jnp.int32, (1, SSD_INNER), 1)
    return jnp.where(lane < half, r0, r1)


def _group_mean(v):
    half = SSD_INNER // 2
    m0 = jnp.mean(v[:, :half], axis=-1, keepdims=True)
    m1 = jnp.mean(v[:, half:], axis=-1, keepdims=True)
    lane = lax.broadcasted_iota(jnp.int32, (1, SSD_INNER), 1)
    return jnp.where(lane < half, m0, m1)


def _ssd_post(y, z, g, name):
    n, d = y.shape
    tl = _tile(n)

    def body(y_ref, z_ref, g_ref, o_ref):
        z_ = z_ref[...]
        v = y_ref[...] * (z_ * _sig(z_))
        o_ref[...] = ((v * _group_norm_parts(v)) * g_ref[...]).astype(BF16)

    return _pc(body, name=name, out_shape=_sds((n, d), BF16), grid=(n // tl,),
               in_specs=[_row(tl, d), _row(tl, d), _const((1, d))], out_specs=_row(tl, d),
               compiler_params=_params(1))(y, z, g)


def _ssd_post_bwd(dout, y, z, g, name):
    n, d = y.shape
    tl = _tile(n)
    nb = n // tl

    def body(do_ref, y_ref, z_ref, g_ref, dy_ref, dz_ref, dg_ref, acc_g):
        i = pl.program_id(0)

        @pl.when(i == 0)
        def _():
            acc_g[...] = jnp.zeros_like(acc_g)

        z_ = z_ref[...]
        y_ = y_ref[...]
        sz = _sig(z_)
        silu_z = z_ * sz
        v = y_ * silu_z
        rs = _group_norm_parts(v)
        nv = v * rs
        do_ = do_ref[...]
        acc_g[...] += _fold(do_ * nv)
        dn = do_ * g_ref[...]
        dv = rs * (dn - nv * _group_mean(dn * nv))
        dy_ref[...] = dv * silu_z
        dz_ref[...] = (dv * y_ * (sz * (1.0 + z_ * (1.0 - sz)))).astype(BF16)

        @pl.when(i == nb - 1)
        def _():
            dg_ref[...] = jnp.sum(acc_g[...], axis=0, keepdims=True)

    return _pc(body, name=name, out_shape=(_sds((n, d)), _sds((n, d), BF16), _sds((1, d))), grid=(nb,),
               in_specs=[_row(tl, d), _row(tl, d), _row(tl, d), _const((1, d))],
               out_specs=(_row(tl, d), _row(tl, d), _const((1, d))),
               scratch_shapes=[pltpu.VMEM((SUBLANES, d), F32)], compiler_params=_params(1))(dout, y, z, g)


def _layer_norm_parts(uc):
    mu = jnp.mean(uc, axis=-1, keepdims=True)
    xc = uc - mu
    rstd = lax.rsqrt(jnp.mean(xc * xc, axis=-1, keepdims=True) + EPS)
    return xc * rstd, rstd


def _conf_fwd(conf_in, conv_w, conv_b, ln_g, ln_b, name):
    n = conf_in.shape[0]
    c = CONF_WIDTH
    tl = _tile(n)
    hb = CONF_HALO
    k_taps = CONF_KERNEL

    def body(x_ref, xp_ref, w_ref, b_ref, g_ref, beta_ref, o_ref, uc_ref, buf):
        i = pl.program_id(0)
        xp = xp_ref[...]
        buf[pl.ds(0, hb), :] = jnp.where(i > 0, xp[:, :c] * _sig(xp[:, c:]), 0.0)
        x_ = x_ref[...]
        buf[pl.ds(hb, tl), :] = x_[:, :c] * _sig(x_[:, c:])
        acc = b_ref[...] + w_ref[0:1, :] * buf[pl.ds(hb - (k_taps - 1), tl), :]
        for k in range(1, k_taps):
            acc = acc + w_ref[k:k + 1, :] * buf[pl.ds(hb - (k_taps - 1) + k, tl), :]
        uc_ref[...] = acc
        nv, _ = _layer_norm_parts(acc)
        v = nv * g_ref[...] + beta_ref[...]
        o_ref[...] = (v * _sig(v)).astype(BF16)

    return _pc(body, name=name, out_shape=(_sds((n, c), BF16), _sds((n, c))), grid=(n // tl,),
               in_specs=[_row(tl, 2 * c), _prev(tl, hb, 2 * c), _const((hb, c)), _const((1, c)), _const((1, c)),
                         _const((1, c))],
               out_specs=(_row(tl, c), _row(tl, c)),
               scratch_shapes=[pltpu.VMEM((tl + hb, c), F32)], compiler_params=_params(1))(
        conf_in, conf_in, conv_w, conv_b, ln_g, ln_b)


def _conf_bwd(dout, uc, conf_in, conv_w, ln_g, ln_b, name):
    n = conf_in.shape[0]
    c = CONF_WIDTH
    tl = _tile(n)
    nb = n // tl
    hb = CONF_HALO
    k_taps = CONF_KERNEL

    def body(do_ref, don_ref, uc_ref, ucn_ref, x_ref, xp_ref, w_ref, g_ref, beta_ref,
             dx_ref, dw_ref, db_ref, dg_ref, dbeta_ref, dbuf, ubuf, acc_w, acc_b, acc_g, acc_beta):
        i = pl.program_id(0)

        @pl.when(i == 0)
        def _():
            acc_w[...] = jnp.zeros_like(acc_w)
            acc_b[...] = jnp.zeros_like(acc_b)
            acc_g[...] = jnp.zeros_like(acc_g)
            acc_beta[...] = jnp.zeros_like(acc_beta)

        g_ = g_ref[...]
        beta_ = beta_ref[...]

        def d_conv_out(do_, uc_):
            nv, rstd = _layer_norm_parts(uc_)
            v = nv * g_ + beta_
            sv = _sig(v)
            dv = do_ * (sv * (1.0 + v * (1.0 - sv)))
            dn = dv * g_
            duc = rstd * (dn - jnp.mean(dn, axis=-1, keepdims=True)
                          - nv * jnp.mean(dn * nv, axis=-1, keepdims=True))
            return duc, dv, nv

        duc, dv, nv = d_conv_out(do_ref[...], uc_ref[...])
        acc_g[...] += _fold(dv * nv)
        acc_beta[...] += _fold(dv)
        acc_b[...] += _fold(duc)
        dbuf[pl.ds(0, tl), :] = duc
        ducn, _, _ = d_conv_out(don_ref[...], ucn_ref[...])
        dbuf[pl.ds(tl, hb), :] = jnp.where(i < nb - 1, ducn, 0.0)
        xp = xp_ref[...]
        ubuf[pl.ds(0, hb), :] = jnp.where(i > 0, xp[:, :c] * _sig(xp[:, c:]), 0.0)
        x_ = x_ref[...]
        val = x_[:, :c]
        sgate = _sig(x_[:, c:])
        ubuf[pl.ds(hb, tl), :] = val * sgate
        du = w_ref[0:1, :] * dbuf[pl.ds(k_taps - 1, tl), :]
        for k in range(1, k_taps):
            du = du + w_ref[k:k + 1, :] * dbuf[pl.ds(k_taps - 1 - k, tl), :]
        for k in range(k_taps):
            acc_w[k] += _fold(duc * ubuf[pl.ds(hb - (k_taps - 1) + k, tl), :])
        dx_ref[:, :c] = (du * sgate).astype(BF16)
        dx_ref[:, c:] = (du * val * sgate * (1.0 - sgate)).astype(BF16)

        @pl.when(i == nb - 1)
        def _():
            dw_ref[...] = jnp.zeros_like(dw_ref)
            for k in range(k_taps):
                dw_ref[k:k + 1, :] = jnp.sum(acc_w[k], axis=0, keepdims=True)
            db_ref[...] = jnp.sum(acc_b[...], axis=0, keepdims=True)
            dg_ref[...] = jnp.sum(acc_g[...], axis=0, keepdims=True)
            dbeta_ref[...] = jnp.sum(acc_beta[...], axis=0, keepdims=True)

    vec = _sds((1, c))
    return _pc(body, name=name, out_shape=(_sds((n, 2 * c), BF16), _sds((hb, c)), vec, vec, vec), grid=(nb,),
               in_specs=[_row(tl, c), _next(tl, hb, c, n), _row(tl, c), _next(tl, hb, c, n), _row(tl, 2 * c),
                         _prev(tl, hb, 2 * c), _const((hb, c)), _const((1, c)), _const((1, c))],
               out_specs=(_row(tl, 2 * c), _const((hb, c)), _const((1, c)), _const((1, c)), _const((1, c))),
               scratch_shapes=[pltpu.VMEM((tl + hb, c), F32), pltpu.VMEM((tl + hb, c), F32),
                               pltpu.VMEM((k_taps, SUBLANES, c), F32), pltpu.VMEM((SUBLANES, c), F32),
                               pltpu.VMEM((SUBLANES, c), F32), pltpu.VMEM((SUBLANES, c), F32)],
               compiler_params=_params(1))(dout, dout, uc, uc, conf_in, conf_in, conv_w, ln_g, ln_b)


def _sc_fwd(sc_in, conv_w, name):
    n = sc_in.shape[0]
    c = SC_WIDTH
    tl = _tile(n)
    hb = CONV_HALO

    def body(x_ref, xp_ref, w_ref, o_ref, buf):
        i = pl.program_id(0)
        xp = xp_ref[...]
        buf[pl.ds(0, hb), :] = jnp.where(i > 0, xp[:, c:2 * c] * xp[:, 2 * c:], 0.0)
        x_ = x_ref[...]
        buf[pl.ds(hb, tl), :] = x_[:, c:2 * c] * x_[:, 2 * c:]
        cv = w_ref[0:1, :] * buf[pl.ds(hb - 2, tl), :]
        for k in range(1, 3):
            cv = cv + w_ref[k:k + 1, :] * buf[pl.ds(hb - 2 + k, tl), :]
        o_ref[...] = (x_[:, :c] * cv).astype(BF16)

    return _pc(body, name=name, out_shape=_sds((n, c), BF16), grid=(n // tl,),
               in_specs=[_row(tl, 3 * c), _prev(tl, hb, 3 * c), _const((SUBLANES, c))], out_specs=_row(tl, c),
               scratch_shapes=[pltpu.VMEM((tl + hb, c), F32)], compiler_params=_params(1))(sc_in, sc_in, conv_w)


def _sc_bwd(dout, sc_in, conv_w, name):
    n = sc_in.shape[0]
    c = SC_WIDTH
    tl = _tile(n)
    nb = n // tl
    hb = CONV_HALO

    def body(do_ref, don_ref, x_ref, xp_ref, xn_ref, w_ref, dx_ref, dw_ref, dbuf, pbuf, acc_w):
        i = pl.program_id(0)

        @pl.when(i == 0)
        def _():
            acc_w[...] = jnp.zeros_like(acc_w)

        x_ = x_ref[...]
        gb, gc, xv = x_[:, :c], x_[:, c:2 * c], x_[:, 2 * c:]
        do_ = do_ref[...]
        dcv = do_ * gb
        dbuf[pl.ds(0, tl), :] = dcv
        dbuf[pl.ds(tl, hb), :] = jnp.where(i < nb - 1, don_ref[...] * xn_ref[...], 0.0)
        xp = xp_ref[...]
        pbuf[pl.ds(0, hb), :] = jnp.where(i > 0, xp[:, c:2 * c] * xp[:, 2 * c:], 0.0)
        pbuf[pl.ds(hb, tl), :] = gc * xv
        cv = w_ref[0:1, :] * pbuf[pl.ds(hb - 2, tl), :]
        dp = w_ref[0:1, :] * dbuf[pl.ds(2, tl), :]
        for k in range(1, 3):
            cv = cv + w_ref[k:k + 1, :] * pbuf[pl.ds(hb - 2 + k, tl), :]
            dp = dp + w_ref[k:k + 1, :] * dbuf[pl.ds(2 - k, tl), :]
        for k in range(3):
            acc_w[k] += _fold(dcv * pbuf[pl.ds(hb - 2 + k, tl), :])
        dx_ref[:, :c] = (do_ * cv).astype(BF16)
        dx_ref[:, c:2 * c] = (dp * xv).astype(BF16)
        dx_ref[:, 2 * c:] = (dp * gc).astype(BF16)

        @pl.when(i == nb - 1)
        def _():
            dw_ref[...] = jnp.zeros_like(dw_ref)
            for k in range(3):
                dw_ref[k:k + 1, :] = jnp.sum(acc_w[k], axis=0, keepdims=True)

    return _pc(body, name=name, out_shape=(_sds((n, 3 * c), BF16), _sds((SUBLANES, c))), grid=(nb,),
               in_specs=[_row(tl, c), _next(tl, hb, c, n), _row(tl, 3 * c), _prev(tl, hb, 3 * c),
                         _next(tl, hb, c, n), _const((SUBLANES, c))],
               out_specs=(_row(tl, 3 * c), _const((SUBLANES, c))),
               scratch_shapes=[pltpu.VMEM((tl + hb, c), F32), pltpu.VMEM((tl + hb, c), F32),
                               pltpu.VMEM((3, SUBLANES, c), F32)],
               compiler_params=_params(1))(dout, dout, sc_in, sc_in, sc_in, conv_w)


def _merge_fwd(gates, ya, yb, yc, b_gate, name):
    n, d = ya.shape
    tl = _tile(n)

    def body(gt_ref, ya_ref, yb_ref, yc_ref, b_ref, o_ref):
        gt = _sig(gt_ref[...] + b_ref[...])
        o_ref[...] = (gt[:, :d] * ya_ref[...] + gt[:, d:2 * d] * yb_ref[...] + gt[:, 2 * d:] * yc_ref[...]).astype(BF16)

    return _pc(body, name=name, out_shape=_sds((n, d), BF16), grid=(n // tl,),
               in_specs=[_row(tl, 3 * d), _row(tl, d), _row(tl, d), _row(tl, d), _const((1, 3 * d))],
               out_specs=_row(tl, d), compiler_params=_params(1))(gates, ya, yb, yc, b_gate)


def _merge_bwd(dm, gates, ya, yb, yc, b_gate, name):
    n, d = ya.shape
    tl = _tile(n)
    nb = n // tl

    def body(dm_ref, gt_ref, ya_ref, yb_ref, yc_ref, b_ref, dya_ref, dyb_ref, dyc_ref, dgt_ref, db_ref, acc):
        i = pl.program_id(0)

        @pl.when(i == 0)
        def _():
            acc[...] = jnp.zeros_like(acc)

        dm_ = dm_ref[...]
        gt = _sig(gt_ref[...] + b_ref[...])
        for idx, (y_ref, dy_ref) in enumerate(((ya_ref, dya_ref), (yb_ref, dyb_ref), (yc_ref, dyc_ref))):
            gk = gt[:, idx * d:(idx + 1) * d]
            dy_ref[...] = (dm_ * gk).astype(BF16)
            dpre = dm_ * y_ref[...] * gk * (1.0 - gk)
            dgt_ref[:, idx * d:(idx + 1) * d] = dpre.astype(BF16)
            acc[:, idx * d:(idx + 1) * d] += _fold(dpre)

        @pl.when(i == nb - 1)
        def _():
            db_ref[...] = jnp.sum(acc[...], axis=0, keepdims=True)

    bf = _sds((n, d), BF16)
    return _pc(body, name=name, out_shape=(bf, bf, bf, _sds((n, 3 * d), BF16), _sds((1, 3 * d))), grid=(nb,),
               in_specs=[_row(tl, d), _row(tl, 3 * d), _row(tl, d), _row(tl, d), _row(tl, d), _const((1, 3 * d))],
               out_specs=(_row(tl, d), _row(tl, d), _row(tl, d), _row(tl, 3 * d), _const((1, 3 * d))),
               scratch_shapes=[pltpu.VMEM((SUBLANES, 3 * d), F32)], compiler_params=_params(1))(
        dm, gates, ya, yb, yc, b_gate)


FFN_COLS = 1408


def _ffn_mid(up, conv_w, conv_b, name):
    n = up.shape[0]
    tl = _tile(n)
    hb = CONV_HALO
    tc = FFN_COLS
    ncb = D_FF // tc

    def spec(shape_rows, idx_fn, off):
        return pl.BlockSpec((shape_rows, tc), lambda j, i, off=off: (idx_fn(i), j + off))

    r = tl // hb
    cur = lambda i: i
    prv = lambda i: jnp.maximum(i * r - 1, 0)

    def conv(buf, w_ref, b_ref):
        acc = b_ref[...] + w_ref[0:1, :] * buf[pl.ds(hb - 2, tl), :]
        for k in range(1, 3):
            acc = acc + w_ref[k:k + 1, :] * buf[pl.ds(hb - 2 + k, tl), :]
        return acc

    def body(g_ref, gp_ref, v_ref, vp_ref, wg_ref, wv_ref, bg_ref, bv_ref, o_ref, gbuf, vbuf):
        i = pl.program_id(1)
        gbuf[pl.ds(0, hb), :] = jnp.where(i > 0, gp_ref[...], 0.0)
        gbuf[pl.ds(hb, tl), :] = g_ref[...]
        vbuf[pl.ds(0, hb), :] = jnp.where(i > 0, vp_ref[...], 0.0)
        vbuf[pl.ds(hb, tl), :] = v_ref[...]
        ug = conv(gbuf, wg_ref, bg_ref)
        uv = conv(vbuf, wv_ref, bv_ref)
        o_ref[...] = (ug * _sig(ug) * uv).astype(BF16)

    wspec = lambda off: pl.BlockSpec((SUBLANES, tc), lambda j, i, off=off: (0, j + off))
    bspec = lambda off: pl.BlockSpec((1, tc), lambda j, i, off=off: (0, j + off))
    return _pc(body, name=name, out_shape=_sds((n, D_FF), BF16), grid=(ncb, n // tl),
               in_specs=[spec(tl, cur, 0), spec(hb, prv, 0), spec(tl, cur, ncb), spec(hb, prv, ncb),
                         wspec(0), wspec(ncb), bspec(0), bspec(ncb)],
               out_specs=pl.BlockSpec((tl, tc), lambda j, i: (i, j)),
               scratch_shapes=[pltpu.VMEM((tl + hb, tc), F32)] * 2, compiler_params=_params(2))(
        up, up, up, up, conv_w, conv_w, conv_b, conv_b)


def _ffn_mid_bwd(da, up, conv_w, conv_b, name):
    n = up.shape[0]
    tl = _tile(n)
    nb = n // tl
    hb = CONV_HALO
    tc = FFN_COLS
    ncb = D_FF // tc
    r = tl // hb
    last = n // hb - 1
    cur = lambda i: i
    prv = lambda i: jnp.maximum(i * r - 1, 0)
    nxt = lambda i: jnp.minimum((i + 1) * r, last)

    def spec(shape_rows, idx_fn, off):
        return pl.BlockSpec((shape_rows, tc), lambda j, i, off=off: (idx_fn(i), j + off))

    def body(da_ref, dan_ref, g_ref, gp_ref, gn_ref, v_ref, vp_ref, vn_ref, wg_ref, wv_ref, bg_ref, bv_ref,
             dg_ref, dv_ref, dwg_ref, dwv_ref, dbg_ref, dbv_ref, gbuf, vbuf, dgbuf, dvbuf, acc_w, acc_b):
        i = pl.program_id(1)

        @pl.when(i == 0)
        def _():
            acc_w[...] = jnp.zeros_like(acc_w)
            acc_b[...] = jnp.zeros_like(acc_b)

        rows = tl + 2 * hb
        gbuf[pl.ds(0, hb), :] = jnp.where(i > 0, gp_ref[...], 0.0)
        gbuf[pl.ds(hb, tl), :] = g_ref[...]
        gbuf[pl.ds(hb + tl, hb), :] = gn_ref[...]
        vbuf[pl.ds(0, hb), :] = jnp.where(i > 0, vp_ref[...], 0.0)
        vbuf[pl.ds(hb, tl), :] = v_ref[...]
        vbuf[pl.ds(hb + tl, hb), :] = vn_ref[...]

        def conv(buf, w_ref, b_ref, start, cnt):
            acc = b_ref[...] + w_ref[0:1, :] * buf[pl.ds(start - 2, cnt), :]
            for k in range(1, 3):
                acc = acc + w_ref[k:k + 1, :] * buf[pl.ds(start - 2 + k, cnt), :]
            return acc

        def d_conv_out(da_, ug, uv):
            s = _sig(ug)
            return da_ * uv * (s * (1.0 + ug * (1.0 - s))), da_ * (ug * s)

        dug, duv = d_conv_out(da_ref[...], conv(gbuf, wg_ref, bg_ref, hb, tl), conv(vbuf, wv_ref, bv_ref, hb, tl))
        dgbuf[pl.ds(0, tl), :] = dug
        dvbuf[pl.ds(0, tl), :] = duv
        dugn, duvn = d_conv_out(dan_ref[...], conv(gbuf, wg_ref, bg_ref, hb + tl, hb),
                                conv(vbuf, wv_ref, bv_ref, hb + tl, hb))
        dgbuf[pl.ds(tl, hb), :] = jnp.where(i < nb - 1, dugn, 0.0)
        dvbuf[pl.ds(tl, hb), :] = jnp.where(i < nb - 1, duvn, 0.0)
        for (dbuf, w_ref, out_ref) in ((dgbuf, wg_ref, dg_ref), (dvbuf, wv_ref, dv_ref)):
            dx = w_ref[0:1, :] * dbuf[pl.ds(2, tl), :]
            for k in range(1, 3):
                dx = dx + w_ref[k:k + 1, :] * dbuf[pl.ds(2 - k, tl), :]
            out_ref[...] = dx.astype(BF16)
        for k in range(3):
            acc_w[0, k] += _fold(dug * gbuf[pl.ds(hb - 2 + k, tl), :])
            acc_w[1, k] += _fold(duv * vbuf[pl.ds(hb - 2 + k, tl), :])
        acc_b[0] += _fold(dug)
        acc_b[1] += _fold(duv)

        @pl.when(i == nb - 1)
        def _():
            for t, (dw_ref, db_ref) in enumerate(((dwg_ref, dbg_ref), (dwv_ref, dbv_ref))):
                dw_ref[...] = jnp.zeros_like(dw_ref)
                for k in range(3):
                    dw_ref[k:k + 1, :] = jnp.sum(acc_w[t, k], axis=0, keepdims=True)
                db_ref[...] = jnp.sum(acc_b[t], axis=0, keepdims=True)

    wspec = lambda off: pl.BlockSpec((SUBLANES, tc), lambda j, i, off=off: (0, j + off))
    bspec = lambda off: pl.BlockSpec((1, tc), lambda j, i, off=off: (0, j + off))
    ospec = lambda off: pl.BlockSpec((tl, tc), lambda j, i, off=off: (i, j + off))
    dg, dv, dwg, dwv, dbg, dbv = _pc(
        body, name=name,
        out_shape=(_sds((n, D_FF), BF16), _sds((n, D_FF), BF16), _sds((SUBLANES, D_FF)), _sds((SUBLANES, D_FF)),
                   _sds((1, D_FF)), _sds((1, D_FF))),
        grid=(ncb, nb),
        in_specs=[spec(tl, cur, 0), spec(hb, nxt, 0),
                  spec(tl, cur, 0), spec(hb, prv, 0), spec(hb, nxt, 0),
                  spec(tl, cur, ncb), spec(hb, prv, ncb), spec(hb, nxt, ncb),
                  wspec(0), wspec(ncb), bspec(0), bspec(ncb)],
        out_specs=(ospec(0), ospec(0), wspec(0), wspec(0), bspec(0), bspec(0)),
        scratch_shapes=[pltpu.VMEM((tl + 2 * hb, tc), F32)] * 2 + [pltpu.VMEM((tl + hb, tc), F32)] * 2
        + [pltpu.VMEM((2, 3, SUBLANES, tc), F32), pltpu.VMEM((2, SUBLANES, tc), F32)],
        compiler_params=_params(2))(da, da, up, up, up, up, up, up, conv_w, conv_w, conv_b, conv_b)
    return dg, dv, jnp.concatenate([dwg, dwv], axis=1), jnp.concatenate([dbg, dbv], axis=1)


def _position():
    return lax.axis_index("x"), lax.axis_index("y"), lax.axis_index("c")


def _all_gather(local, name):
    r, c = local.shape

    def body(x_ref, out_ref, send_sems, recv_sems, local_sem):
        x, y, cc = _position()
        me, sibling = (x, y, cc), (x, y, 1 - cc)
        chips = [(1 - x, y), (x, 1 - y), (1 - x, 1 - y)]

        def slot(px, py, pc):
            return out_ref.at[4 * px + 2 * py + pc]

        def copy(k, block, to, src=None):
            return pltpu.make_async_remote_copy(
                src_ref=slot(*block) if src is None else src, dst_ref=slot(*block),
                send_sem=send_sems.at[k], recv_sem=recv_sems.at[k], device_id=to, device_id_type=MESH)

        mine = pltpu.make_async_copy(x_ref, slot(*me), local_sem)
        mine.start()
        first = [copy(0, me, sibling, src=x_ref)]
        first += [copy(1 + j, me, (*chip, cc), src=x_ref) for j, chip in enumerate(chips)]
        for cp in first:
            cp.start()
        passed = [copy(4 + j, (*chip, cc), sibling) for j, chip in enumerate(chips)]
        for j, chip in enumerate(chips):
            copy(1 + j, (*chip, cc), me).wait_recv()
            passed[j].start()
        copy(0, sibling, me).wait_recv()
        for j, chip in enumerate(chips):
            copy(4 + j, (*chip, 1 - cc), me).wait_recv()
        for cp in first + passed:
            cp.wait_send()
        mine.wait()

    return _pc(body, name=name, out_shape=_sds((N_DEV, r, c), local.dtype),
               in_specs=[pl.BlockSpec(memory_space=pl.ANY)], out_specs=pl.BlockSpec(memory_space=pl.ANY),
               scratch_shapes=[pltpu.SemaphoreType.DMA((7,)), pltpu.SemaphoreType.DMA((7,)),
                               pltpu.SemaphoreType.DMA])(local)


def _exchange(g, name):
    _, r, c = g.shape
    flips = [(0, 0, 1), (1, 0, 0), (0, 1, 0), (1, 1, 0), (1, 0, 1), (0, 1, 1), (1, 1, 1)]

    def body(g_ref, out_ref, send_sems, recv_sems, local_sem):
        x, y, cc = _position()
        me_idx = 4 * x + 2 * y + cc
        mine = pltpu.make_async_copy(g_ref.at[me_idx], out_ref.at[me_idx], local_sem)
        mine.start()
        sends, recvs = [], []
        for k, (fx, fy, fc) in enumerate(flips):
            peer = (1 - x if fx else x, 1 - y if fy else y, 1 - cc if fc else cc)
            peer_idx = 4 * peer[0] + 2 * peer[1] + peer[2]
            sends.append(pltpu.make_async_remote_copy(
                src_ref=g_ref.at[peer_idx], dst_ref=out_ref.at[me_idx], send_sem=send_sems.at[k],
                recv_sem=recv_sems.at[k], device_id=peer, device_id_type=MESH))
            recvs.append(pltpu.make_async_remote_copy(
                src_ref=g_ref.at[me_idx], dst_ref=out_ref.at[peer_idx], send_sem=send_sems.at[k],
                recv_sem=recv_sems.at[k], device_id=peer, device_id_type=MESH))
        for cp in sends:
            cp.start()
        for cp in recvs:
            cp.wait_recv()
        for cp in sends:
            cp.wait_send()
        mine.wait()

    return _pc(body, name=name, out_shape=_sds(g.shape, g.dtype),
               in_specs=[pl.BlockSpec(memory_space=pl.ANY)], out_specs=pl.BlockSpec(memory_space=pl.ANY),
               scratch_shapes=[pltpu.SemaphoreType.DMA((7,)), pltpu.SemaphoreType.DMA((7,)),
                               pltpu.SemaphoreType.DMA])(g)


def _reduce_adamw(parts, w, m, v, rows, name):
    _, r, c = parts.shape
    assert r % rows == 0
    c1 = 1.0 - ADAM_B1 ** ADAM_STEP
    c2 = 1.0 - ADAM_B2 ** ADAM_STEP

    def body(p_ref, w_ref, m_ref, v_ref, g_out, d_out, m_out, v_out):
        g = p_ref[0].astype(F32)
        for j in range(1, N_DEV):
            g = g + p_ref[j].astype(F32)
        m_new = ADAM_B1 * m_ref[...] + (1.0 - ADAM_B1) * g
        v_new = ADAM_B2 * v_ref[...] + (1.0 - ADAM_B2) * (g * g)
        m_hat = m_new / c1
        v_hat = v_new / c2
        g_out[...] = g
        d_out[...] = -ADAM_LR * (m_hat / (jnp.sqrt(v_hat) + ADAM_EPS) + ADAM_WD * w_ref[...])
        m_out[...] = m_new
        v_out[...] = v_new

    spec = pl.BlockSpec((rows, c), lambda i: (i, 0))
    return _pc(body, name=name, out_shape=(_sds((r, c)),) * 4, grid=(r // rows,),
               in_specs=[pl.BlockSpec((N_DEV, rows, c), lambda i: (0, i, 0)), spec, spec, spec],
               out_specs=(spec,) * 4, compiler_params=_params(1))(parts, w, m, v)


MATRICES = (("ada_mix_w", 2), ("w_in", 2), ("w_ssd_out", 1), ("w_conf_out", 2), ("w_sc_out", 2), ("w_o", 1),
            ("ada_ffn_w", 2), ("w_up", 2), ("w_down", 1))
CONV_WEIGHTS = (("ssd_conv_w", 2), ("conf_conv_w", 2), ("sc_conv_w", 2), ("ffn_conv_w", 2))
SHARDED = MATRICES + CONV_WEIGHTS
REPLICATED = ("ada_mix_b", "norm_mix_g", "b_gate", "ssd_conv_b", "ssd_dt_bias", "ssd_a_log", "ssd_d", "ssd_norm_g",
              "conf_conv_b", "conf_ln_g", "conf_ln_b", "ada_ffn_b", "norm_ffn_g", "ffn_conv_b", "final_norm_g")
WEIGHT_NAMES = ("ada_mix_w", "ada_mix_b", "norm_mix_g", "w_in", "b_gate", "ssd_conv_w", "ssd_conv_b", "ssd_dt_bias",
                "ssd_a_log", "ssd_d", "ssd_norm_g", "w_ssd_out", "conf_conv_w", "conf_conv_b", "conf_ln_g",
                "conf_ln_b", "w_conf_out", "sc_conv_w", "w_sc_out", "w_o", "ada_ffn_w", "ada_ffn_b", "norm_ffn_g",
                "w_up", "ffn_conv_w", "ffn_conv_b", "w_down", "final_norm_g")
GRAD_ROWS = 96


def _pack_flat(arrays, cols, row_multiple, dtype):
    flat = jnp.concatenate([a.reshape(-1).astype(dtype) for a in arrays])
    rows = -(-flat.shape[0] // cols)
    rows = -(-rows // row_multiple) * row_multiple
    return jnp.pad(flat, (0, rows * cols - flat.shape[0])).reshape(rows, cols)


def _unpack_flat(flat2d, shapes):
    flat = flat2d.reshape(-1)
    out, off = [], 0
    for s in shapes:
        n = 1
        for d in s:
            n *= d
        out.append(flat[off:off + n].reshape(s))
        off += n
    return out


def _unshard(gathered, local_shapes, axes):
    flat = gathered.reshape(N_DEV, -1)
    out, off = [], 0
    for (two, a, b), axis in zip(local_shapes, axes):
        n = two * a * b
        seg = flat[:, off:off + n].reshape(N_DEV, two, a, b)
        off += n
        if axis == 2:
            out.append(jnp.transpose(seg, (1, 2, 0, 3)).reshape(two, a, N_DEV * b))
        else:
            out.append(jnp.transpose(seg, (1, 0, 2, 3)).reshape(two, N_DEV * a, b))
    return out


def _to_shards(full, axis):
    two, a, b = full.shape
    if axis == 2:
        t = jnp.transpose(full.reshape(two, a, N_DEV, b // N_DEV), (2, 0, 1, 3))
    else:
        t = jnp.transpose(full.reshape(two, N_DEV, a // N_DEV, b), (1, 0, 2, 3))
    return t.reshape(N_DEV, -1)


def _pad_rows(a, rows):
    return jnp.pad(a, ((0, rows - a.shape[0]), (0, 0)))


def _pad_lanes(a):
    return jnp.pad(a, ((0, 0), (0, LANES - a.shape[1])))


def _layer_weights(full, i):
    row = lambda name: full[name][i].reshape(1, -1)
    w_in = full["w_in"][i]
    wl = {
        "ada_mix_w": full["ada_mix_w"][i], "ada_mix_b": row("ada_mix_b"), "norm_mix_g": row("norm_mix_g"),
        "w_z": w_in[:, :OFF_Z], "w_xbc": w_in[:, OFF_Z:OFF_XBC], "w_dt": _pad_lanes(w_in[:, OFF_XBC:OFF_DT]),
        "w_conf": w_in[:, OFF_DT:OFF_CONF], "w_sc": w_in[:, OFF_CONF:OFF_SC], "w_gates": w_in[:, OFF_SC:],
        "b_gate": row("b_gate"),
        "ssd_conv_w": _pad_rows(full["ssd_conv_w"][i], SUBLANES), "ssd_conv_b": row("ssd_conv_b"),
        "dt_bias": _pad_lanes(row("ssd_dt_bias")), "a_log": _pad_lanes(row("ssd_a_log")),
        "ssd_d": _pad_lanes(row("ssd_d")), "ssd_norm_g": row("ssd_norm_g"), "w_ssd_out": full["w_ssd_out"][i],
        "conf_conv_w": _pad_rows(full["conf_conv_w"][i], CONF_HALO), "conf_conv_b": row("conf_conv_b"),
        "conf_ln_g": row("conf_ln_g"), "conf_ln_b": row("conf_ln_b"), "w_conf_out": full["w_conf_out"][i],
        "sc_conv_w": _pad_rows(full["sc_conv_w"][i], SUBLANES), "w_sc_out": full["w_sc_out"][i],
        "w_o": full["w_o"][i],
        "ada_ffn_w": full["ada_ffn_w"][i], "ada_ffn_b": row("ada_ffn_b"), "norm_ffn_g": row("norm_ffn_g"),
        "w_up": full["w_up"][i], "ffn_conv_w": _pad_rows(full["ffn_conv_w"][i], SUBLANES),
        "ffn_conv_b": row("ffn_conv_b"), "w_down": full["w_down"][i],
    }
    return wl


def _adaln(sc8, w, b, name):
    mod = _matmul(sc8, w, "nn", F32, name)[0:1, :] + b
    return mod[:, :D_MODEL], mod[:, D_MODEL:2 * D_MODEL], mod[:, 2 * D_MODEL:]


def _layer_fwd(i, x, prev, sc8, wl):
    t = f"l{i}_"
    s = {}
    shift, scale, gate = _adaln(sc8, wl["ada_mix_w"], wl["ada_mix_b"], t + "ada_mix")
    if prev is None:
        s["x_in"] = x
        s["h"] = _prenorm_first(x, wl["norm_mix_g"], scale, shift, t + "norm_mix")
    else:
        s["x_in"], s["h"] = _prenorm_res(x, prev[0], prev[1], wl["norm_mix_g"], scale, shift, t + "norm_mix")
    s["scale_mix"], s["gate_mix"] = scale, gate
    h = s["h"]
    s["z"] = _matmul(h, wl["w_z"], "nn", F32, t + "in_z")
    s["xbc"] = _matmul(h, wl["w_xbc"], "nn", F32, t + "in_xbc")
    s["dt_raw"] = _matmul(h, wl["w_dt"], "nn", F32, t + "in_dt")
    s["conf"] = _matmul(h, wl["w_conf"], "nn", F32, t + "in_conf")
    s["sc"] = _matmul(h, wl["w_sc"], "nn", F32, t + "in_sc")
    s["gates"] = _matmul(h, wl["w_gates"], "nn", F32, t + "in_gates")
    s["pre"], s["dt"] = _ssd_pre(s["xbc"], s["dt_raw"], wl["ssd_conv_w"], wl["ssd_conv_b"], wl["dt_bias"],
                                 t + "ssd_pre")
    s["y"], s["hprev"] = _ssd_scan(s["pre"], s["dt"], wl["a_log"], wl["ssd_d"], t + "ssd_scan")
    s["ya_in"] = _ssd_post(s["y"], s["z"], wl["ssd_norm_g"], t + "ssd_post")
    s["yb_in"], s["uc"] = _conf_fwd(s["conf"], wl["conf_conv_w"], wl["conf_conv_b"], wl["conf_ln_g"],
                                    wl["conf_ln_b"], t + "conf")
    s["yc_in"] = _sc_fwd(s["sc"], wl["sc_conv_w"], t + "sconv")
    s["ya"] = _matmul(s["ya_in"], wl["w_ssd_out"], "nn", F32, t + "ssd_out")
    s["yb"] = _matmul(s["yb_in"], wl["w_conf_out"], "nn", F32, t + "conf_out")
    s["yc"] = _matmul(s["yc_in"], wl["w_sc_out"], "nn", F32, t + "sc_out")
    s["merged"] = _merge_fwd(s["gates"], s["ya"], s["yb"], s["yc"], wl["b_gate"], t + "merge")
    s["mix"] = _matmul(s["merged"], wl["w_o"], "nn", F32, t + "w_o")
    shift2, scale2, gate2 = _adaln(sc8, wl["ada_ffn_w"], wl["ada_ffn_b"], t + "ada_ffn")
    s["x_mid"], s["h2"] = _prenorm_res(s["x_in"], s["mix"], gate, wl["norm_ffn_g"], scale2, shift2, t + "norm_ffn")
    s["scale_ffn"], s["gate_ffn"] = scale2, gate2
    s["up"] = _matmul(s["h2"], wl["w_up"], "nn", F32, t + "w_up")
    s["a"] = _ffn_mid(s["up"], wl["ffn_conv_w"], wl["ffn_conv_b"], t + "ffn_mid")
    s["out"] = _matmul(s["a"], wl["w_down"], "nn", F32, t + "w_down")
    return s


def _layer_bwd(i, s, wl, sc8, dys_ffn, dx_after, dgate_ffn, prev):
    t = f"l{i}_b_"
    g = {}
    da = _matmul(dys_ffn, wl["w_down"], "nt", F32, t + "d_a")
    g["w_down"] = _matmul(s["a"], dys_ffn, "tn", F32, t + "dw_down")
    dug, duv, dfw, g["ffn_conv_b"] = _ffn_mid_bwd(da, s["up"], wl["ffn_conv_w"], wl["ffn_conv_b"], t + "ffn_mid")
    g["ffn_conv_w"] = dfw[:3]
    dh2 = _matmul(dug, wl["w_up"][:, :D_FF], "nt", F32, t + "d_h2_g")
    dh2 = _matmul(duv, wl["w_up"][:, D_FF:], "nt", F32, t + "d_h2_v", add=dh2)
    g["w_up"] = jnp.concatenate([_matmul(s["h2"], dug, "tn", F32, t + "dw_up_g"),
                                 _matmul(s["h2"], duv, "tn", F32, t + "dw_up_v")], axis=1)
    dx_mid, dshift2, dscale2, g["norm_ffn_g"], dys_mix, dgate_mix = _norm_bwd(
        dh2, s["x_mid"], dx_after, wl["norm_ffn_g"], s["scale_ffn"], t + "norm_ffn", s["mix"], s["gate_mix"])
    dmod_ffn = jnp.concatenate([dshift2, dscale2, dgate_ffn], axis=1)
    g["ada_ffn_b"] = dmod_ffn
    g["ada_ffn_w"] = _matmul(sc8, _pad_rows(dmod_ffn, SUBLANES), "tn", F32, t + "dw_ada_ffn")
    dmerged = _matmul(dys_mix, wl["w_o"], "nt", F32, t + "d_merged")
    g["w_o"] = _matmul(s["merged"], dys_mix, "tn", F32, t + "dw_o")
    dya, dyb, dyc, dgates, g["b_gate"] = _merge_bwd(dmerged, s["gates"], s["ya"], s["yb"], s["yc"], wl["b_gate"],
                                                    t + "merge")
    dya_in = _matmul(dya, wl["w_ssd_out"], "nt", F32, t + "d_ya_in")
    g["w_ssd_out"] = _matmul(s["ya_in"], dya, "tn", F32, t + "dw_ssd_out")
    dyb_in = _matmul(dyb, wl["w_conf_out"], "nt", F32, t + "d_yb_in")
    g["w_conf_out"] = _matmul(s["yb_in"], dyb, "tn", F32, t + "dw_conf_out")
    dyc_in = _matmul(dyc, wl["w_sc_out"], "nt", F32, t + "d_yc_in")
    g["w_sc_out"] = _matmul(s["yc_in"], dyc, "tn", F32, t + "dw_sc_out")
    dy, dz, g["ssd_norm_g"] = _ssd_post_bwd(dya_in, s["y"], s["z"], wl["ssd_norm_g"], t + "ssd_post")
    dpre, ddt, da_log, dd = _ssd_scan_bwd(s["pre"], s["dt"], s["hprev"], dy, wl["a_log"], wl["ssd_d"],
                                          t + "ssd_scan")
    g["ssd_a_log"], g["ssd_d"] = da_log[:, :SSD_HEADS], dd[:, :SSD_HEADS]
    dxbc, ddt_raw, dcw, g["ssd_conv_b"], ddtb = _ssd_pre_bwd(dpre, s["xbc"], ddt, s["dt_raw"], wl["ssd_conv_w"],
                                                             wl["dt_bias"], t + "ssd_pre")
    g["ssd_conv_w"], g["ssd_dt_bias"] = dcw[:4], ddtb[:, :SSD_HEADS]
    dconf, dccw, g["conf_conv_b"], g["conf_ln_g"], g["conf_ln_b"] = _conf_bwd(
        dyb_in, s["uc"], s["conf"], wl["conf_conv_w"], wl["conf_ln_g"], wl["conf_ln_b"], t + "conf")
    g["conf_conv_w"] = dccw[:CONF_KERNEL]
    dsc, dscw = _sc_bwd(dyc_in, s["sc"], wl["sc_conv_w"], t + "sconv")
    g["sc_conv_w"] = dscw[:3]
    segs = (("z", dz, "w_z"), ("xbc", dxbc, "w_xbc"), ("dt", ddt_raw, "w_dt"), ("conf", dconf, "w_conf"),
            ("sc", dsc, "w_sc"), ("gates", dgates, "w_gates"))
    dh, dw_segs = None, []
    for nm, dseg, wname in segs:
        dh = _matmul(dseg, wl[wname], "nt", F32, t + "d_h_" + nm, add=dh)
        dw = _matmul(s["h"], dseg, "tn", F32, t + "dw_in_" + nm)
        dw_segs.append(dw[:, :SSD_HEADS] if nm == "dt" else dw)
    g["w_in"] = jnp.concatenate(dw_segs, axis=1)
    if prev is None:
        dx_in, dshift, dscale, g["norm_mix_g"] = _norm_bwd(dh, s["x_in"], dx_mid, wl["norm_mix_g"], s["scale_mix"],
                                                          t + "norm_mix")
        back = None
    else:
        dx_in, dshift, dscale, g["norm_mix_g"], dys_prev, dgate_prev = _norm_bwd(
            dh, s["x_in"], dx_mid, wl["norm_mix_g"], s["scale_mix"], t + "norm_mix", prev[0], prev[1])
        back = (dys_prev, dgate_prev)
    dmod_mix = jnp.concatenate([dshift, dscale, dgate_mix], axis=1)
    g["ada_mix_b"] = dmod_mix
    g["ada_mix_w"] = _matmul(sc8, _pad_rows(dmod_mix, SUBLANES), "tn", F32, t + "dw_ada_mix")
    return g, dx_in, back


def _device_step(x, c, target, full):
    sc8 = _pad_rows(c * (1.0 / (1.0 + jnp.exp(-c))), SUBLANES)
    wls = [_layer_weights(full, i) for i in range(DEPTH)]
    saved, prev, xcur = [], None, x
    for i in range(DEPTH):
        s = _layer_fwd(i, xcur, prev, sc8, wls[i])
        saved.append(s)
        xcur, prev = s["x_mid"], (s["out"], s["gate_ffn"])
    gf = full["final_norm_g"].reshape(1, -1)
    last = saved[-1]
    loss, dx, dys, dgate, dgf = _final_loss(last["x_mid"], last["out"], last["gate_ffn"], gf, target, "final_loss")
    grads = [None] * DEPTH
    for i in reversed(range(DEPTH)):
        prev = None if i == 0 else (saved[i - 1]["out"], saved[i - 1]["gate_ffn"])
        grads[i], dx, back = _layer_bwd(i, saved[i], wls[i], sc8, dys, dx, dgate, prev)
        if back is not None:
            dys, dgate = back
    out = {"final_norm_g": dgf.reshape(-1)}
    for name in WEIGHT_NAMES:
        if name != "final_norm_g":
            out[name] = jnp.stack([grads[i][name].reshape(full[name].shape[1:]) for i in range(DEPTH)])
    return loss[0, 0], dx, out


def _step(x, c, target, weights, moments_m, moments_v):
    mat_names = [n for n, _ in MATRICES]
    conv_names = [n for n, _ in CONV_WEIGHTS]
    sharded_names = [n for n, _ in SHARDED]
    mats = _all_gather(_pack_flat([weights[n] for n in mat_names], PACK_COLS, 16, BF16), "gather_matrices")
    convs = _all_gather(_pack_flat([weights[n] for n in conv_names], LANES, SUBLANES, F32), "gather_conv_weights")
    full = {n: weights[n] for n in REPLICATED}
    full.update(zip(mat_names, _unshard(mats, [weights[n].shape for n in mat_names], [a for _, a in MATRICES])))
    full.update(zip(conv_names, _unshard(convs, [weights[n].shape for n in conv_names],
                                         [a for _, a in CONV_WEIGHTS])))
    loss, grad_x, grads = _device_step(x[0], c, target[0], full)
    parts = jnp.concatenate([_to_shards(grads[n], a).astype(BF16) for n, a in SHARDED], axis=1)
    n_el = parts.shape[1]
    rows = -(-n_el // PACK_COLS)
    rows = -(-rows // GRAD_ROWS) * GRAD_ROWS
    parts = jnp.pad(parts, ((0, 0), (0, rows * PACK_COLS - n_el))).reshape(N_DEV, rows, PACK_COLS)
    received = _exchange(parts, "exchange_grads")
    pack = lambda d: _pack_flat([d[n] for n in sharded_names], PACK_COLS, GRAD_ROWS, F32)
    big = _reduce_adamw(received, pack(weights), pack(moments_m), pack(moments_v), GRAD_ROWS, "adamw_sharded")
    big = [_unpack_flat(b, [weights[n].shape for n in sharded_names]) for b in big]
    small_parts = _all_gather(_pack_flat([grads[n] for n in REPLICATED], LANES, SUBLANES, F32), "gather_small_grads")
    pack_s = lambda d: _pack_flat([d[n] for n in REPLICATED], LANES, SUBLANES, F32)
    small = _reduce_adamw(small_parts, pack_s(weights), pack_s(moments_m), pack_s(moments_v), small_parts.shape[1],
                          "adamw_replicated")
    small = [_unpack_flat(b, [weights[n].shape for n in REPLICATED]) for b in small]
    results = []
    for kind in range(4):
        by_name = dict(zip(sharded_names, big[kind]))
        by_name.update(zip(REPLICATED, small[kind]))
        results.append([by_name[n] for n in WEIGHT_NAMES])
    loss = lax.psum(loss, ("x", "y", "c"))
    return (loss, grad_x[None], *results[0], *results[1], *results[2], *results[3])


def kernel(x, c, ada_mix_w, ada_mix_b, norm_mix_g, w_in, b_gate, ssd_conv_w, ssd_conv_b, ssd_dt_bias, ssd_a_log, ssd_d, ssd_norm_g, w_ssd_out, conf_conv_w, conf_conv_b, conf_ln_g, conf_ln_b, w_conf_out, sc_conv_w, w_sc_out, w_o, ada_ffn_w, ada_ffn_b, norm_ffn_g, w_up, ffn_conv_w, ffn_conv_b, w_down, final_norm_g, loss_target, m_ada_mix_w, m_ada_mix_b, m_norm_mix_g, m_w_in, m_b_gate, m_ssd_conv_w, m_ssd_conv_b, m_ssd_dt_bias, m_ssd_a_log, m_ssd_d, m_ssd_norm_g, m_w_ssd_out, m_conf_conv_w, m_conf_conv_b, m_conf_ln_g, m_conf_ln_b, m_w_conf_out, m_sc_conv_w, m_w_sc_out, m_w_o, m_ada_ffn_w, m_ada_ffn_b, m_norm_ffn_g, m_w_up, m_ffn_conv_w, m_ffn_conv_b, m_w_down, m_final_norm_g, v_ada_mix_w, v_ada_mix_b, v_norm_mix_g, v_w_in, v_b_gate, v_ssd_conv_w, v_ssd_conv_b, v_ssd_dt_bias, v_ssd_a_log, v_ssd_d, v_ssd_norm_g, v_w_ssd_out, v_conf_conv_w, v_conf_conv_b, v_conf_ln_g, v_conf_ln_b, v_w_conf_out, v_sc_conv_w, v_w_sc_out, v_w_o, v_ada_ffn_w, v_ada_ffn_b, v_norm_ffn_g, v_w_up, v_ffn_conv_w, v_ffn_conv_b, v_w_down, v_final_norm_g):
    given = dict(locals())
    weights = {n: given[n] for n in WEIGHT_NAMES}
    moments_m = {n: given["m_" + n] for n in WEIGHT_NAMES}
    moments_v = {n: given["v_" + n] for n in WEIGHT_NAMES}
    return _step(x, c, loss_target, weights, moments_m, moments_v)
```

```python
import functools

import jax
import jax.numpy as jnp
from jax import lax
from jax.experimental import pallas as pl
from jax.experimental.pallas import tpu as pltpu

F32 = jnp.float32
BF16 = jnp.bfloat16
HI = lax.Precision.HIGHEST
MESH = pl.DeviceIdType.MESH

N_DEV = 8
DEPTH = 2
D_MODEL = 1024
SSD_HEADS = 16
SSD_HEAD_DIM = 64
SSD_INNER = 1024
SSD_STATE = 64
SSD_CHUNK = 128
SSD_XBC = 1280
CONF_WIDTH = 512
CONF_KERNEL = 31
SC_WIDTH = 512
D_FF = 2816
EPS = 1e-6
OFF_Z, OFF_XBC, OFF_DT, OFF_CONF, OFF_SC, N_IN = 1024, 2304, 2320, 3344, 4880, 7952

ADAM_LR, ADAM_B1, ADAM_B2, ADAM_EPS, ADAM_WD, ADAM_STEP = 0.001, 0.9, 0.999, 1e-08, 0.01, 10

LANES = 128
SUBLANES = 8
VMEM_LIMIT = 56 * 1024 * 1024
ROW_TILE = 256
PACK_COLS = 1024

NN = (((1,), (0,)), ((), ()))
NT = (((1,), (1,)), ((), ()))
TN = (((0,), (0,)), ((), ()))


def _params(n_axes):
    return pltpu.CompilerParams(dimension_semantics=("arbitrary",) * n_axes, vmem_limit_bytes=VMEM_LIMIT)


def _pc(body, **kw):
    return pl.pallas_call(body, **kw)


def _dot(a, b, dn=NN, precision=None):
    return lax.dot_general(a, b, dn, precision=precision, preferred_element_type=F32)


def _sig(x):
    return 1.0 / (1.0 + jnp.exp(-x))


def _fold(v):
    r, c = v.shape
    return v.reshape(r // SUBLANES, SUBLANES, c).sum(axis=0)


def _tile(n_rows):
    return min(ROW_TILE, n_rows // 2)


def _row(tl, c, col=0):
    return pl.BlockSpec((tl, c), lambda i, col=col: (i, col))


def _prev(tl, hb, c, col=0):
    r = tl // hb
    return pl.BlockSpec((hb, c), lambda i, col=col: (jnp.maximum(i * r - 1, 0), col))


def _next(tl, hb, c, n_rows, col=0):
    r = tl // hb
    last = n_rows // hb - 1
    return pl.BlockSpec((hb, c), lambda i, col=col: (jnp.minimum((i + 1) * r, last), col))


def _const(shape):
    return pl.BlockSpec(shape, lambda i: (0,) * len(shape))


def _sds(shape, dtype=F32):
    return jax.ShapeDtypeStruct(shape, dtype)


def _pick(dim, target):
    if dim <= target:
        return dim
    best = None
    for t in range(LANES, target + 1, LANES):
        if dim % t == 0:
            best = t
    assert best is not None, (dim, target)
    return best


def _matmul(a, b, mode, out_dtype, name, add=None):
    if mode == "nn":
        (m, k), (k2, n) = a.shape, b.shape
    elif mode == "nt":
        (m, k), (n, k2) = a.shape, b.shape
    else:
        (k, m), (k2, n) = a.shape, b.shape
    assert k == k2, (a.shape, b.shape, mode)
    tm, tn, tk = _pick(m, 1024), _pick(n, 1536), _pick(k, 1024)
    nk = k // tk
    dn = {"nn": NN, "nt": NT, "tn": TN}[mode]

    def body(*refs):
        if add is None:
            a_ref, b_ref, o_ref, acc = refs
        else:
            a_ref, b_ref, c_ref, o_ref, acc = refs
        kk = pl.program_id(2)

        @pl.when(kk == 0)
        def _():
            acc[...] = jnp.zeros_like(acc)

        acc[...] += _dot(a_ref[...].astype(BF16), b_ref[...].astype(BF16), dn)

        @pl.when(kk == nk - 1)
        def _():
            r = acc[...]
            if add is not None:
                r = r + c_ref[...]
            o_ref[...] = r.astype(out_dtype)

    a_spec = {"nn": pl.BlockSpec((tm, tk), lambda i, j, kk: (i, kk)),
              "nt": pl.BlockSpec((tm, tk), lambda i, j, kk: (i, kk)),
              "tn": pl.BlockSpec((tk, tm), lambda i, j, kk: (kk, i))}[mode]
    b_spec = {"nn": pl.BlockSpec((tk, tn), lambda i, j, kk: (kk, j)),
              "nt": pl.BlockSpec((tn, tk), lambda i, j, kk: (j, kk)),
              "tn": pl.BlockSpec((tk, tn), lambda i, j, kk: (kk, j))}[mode]
    o_spec = pl.BlockSpec((tm, tn), lambda i, j, kk: (i, j))
    in_specs = [a_spec, b_spec] + ([o_spec] if add is not None else [])
    args = (a, b) + ((add,) if add is not None else ())
    return _pc(body, name=name, out_shape=_sds((m, n), out_dtype), grid=(m // tm, n // tn, nk),
               in_specs=in_specs, out_specs=o_spec, scratch_shapes=[pltpu.VMEM((tm, tn), F32)],
               compiler_params=_params(3))(*args)


def _norm_mod(x, g, scale, shift):
    r = lax.rsqrt(jnp.mean(x * x, axis=-1, keepdims=True) + EPS)
    return ((x * r) * g) * (1.0 + scale) + shift


def _prenorm_first(x, g, scale, shift, name):
    n, d = x.shape
    tl = _tile(n)

    def body(x_ref, g_ref, sc_ref, sh_ref, h_ref):
        h_ref[...] = _norm_mod(x_ref[...], g_ref[...], sc_ref[...], sh_ref[...]).astype(BF16)

    return _pc(body, name=name, out_shape=_sds((n, d), BF16), grid=(n // tl,),
               in_specs=[_row(tl, d)] + [_const((1, d))] * 3, out_specs=_row(tl, d),
               compiler_params=_params(1))(x, g, scale, shift)


def _prenorm_res(x, y, gate, g, scale, shift, name):
    n, d = x.shape
    tl = _tile(n)

    def body(x_ref, y_ref, gate_ref, g_ref, sc_ref, sh_ref, xo_ref, h_ref):
        xn = x_ref[...] + gate_ref[...] * y_ref[...]
        xo_ref[...] = xn
        h_ref[...] = _norm_mod(xn, g_ref[...], sc_ref[...], sh_ref[...]).astype(BF16)

    return _pc(body, name=name, out_shape=(_sds((n, d)), _sds((n, d), BF16)), grid=(n // tl,),
               in_specs=[_row(tl, d), _row(tl, d)] + [_const((1, d))] * 4,
               out_specs=(_row(tl, d), _row(tl, d)), compiler_params=_params(1))(x, y, gate, g, scale, shift)


def _final_loss(x, y, gate, gf, target, name):
    n, d = x.shape
    tl = _tile(n)
    nb = n // tl

    def body(x_ref, y_ref, gate_ref, gf_ref, t_ref, loss_ref, dx_ref, dys_ref, dgate_ref, dgf_ref,
             acc_l, acc_gate, acc_gf):
        i = pl.program_id(0)

        @pl.when(i == 0)
        def _():
            acc_l[...] = jnp.zeros_like(acc_l)
            acc_gate[...] = jnp.zeros_like(acc_gate)
            acc_gf[...] = jnp.zeros_like(acc_gf)

        yv = y_ref[...]
        gate = gate_ref[...]
        gf = gf_ref[...]
        x2 = x_ref[...] + gate * yv
        r = lax.rsqrt(jnp.mean(x2 * x2, axis=-1, keepdims=True) + EPS)
        xn = x2 * r
        e = xn * gf - t_ref[...]
        acc_l[...] += _fold(e * e)
        dy = e * (1.0 / d)
        acc_gf[...] += _fold(dy * xn)
        dxn = dy * gf
        dx = r * (dxn - xn * jnp.mean(dxn * xn, axis=-1, keepdims=True))
        dx_ref[...] = dx
        dys_ref[...] = (dx * gate).astype(BF16)
        acc_gate[...] += _fold(dx * yv)

        @pl.when(i == nb - 1)
        def _():
            loss_ref[...] = jnp.full((SUBLANES, LANES), 0.5 / d, F32) * jnp.sum(acc_l[...])
            dgate_ref[...] = jnp.sum(acc_gate[...], axis=0, keepdims=True)
            dgf_ref[...] = jnp.sum(acc_gf[...], axis=0, keepdims=True)

    return _pc(body, name=name,
               out_shape=(_sds((SUBLANES, LANES)), _sds((n, d)), _sds((n, d), BF16), _sds((1, d)), _sds((1, d))),
               grid=(nb,),
               in_specs=[_row(tl, d), _row(tl, d), _const((1, d)), _const((1, d)), _row(tl, d)],
               out_specs=(_const((SUBLANES, LANES)), _row(tl, d), _row(tl, d), _const((1, d)), _const((1, d))),
               scratch_shapes=[pltpu.VMEM((SUBLANES, d), F32)] * 3,
               compiler_params=_params(1))(x, y, gate, gf, target)


def _norm_bwd(dh, x, dxo, g, scale, name, y_prev=None, gate_prev=None):
    n, d = x.shape
    tl = _tile(n)
    nb = n // tl
    has_prev = y_prev is not None

    def body(*refs):
        if has_prev:
            (dh_ref, x_ref, dxo_ref, g_ref, sc_ref, yp_ref, gp_ref,
             dx_ref, dsh_ref, dsc_ref, dg_ref, dys_ref, dgp_ref, acc_sh, acc_s, acc_gp) = refs
        else:
            (dh_ref, x_ref, dxo_ref, g_ref, sc_ref,
             dx_ref, dsh_ref, dsc_ref, dg_ref, acc_sh, acc_s) = refs
        i = pl.program_id(0)

        @pl.when(i == 0)
        def _():
            acc_sh[...] = jnp.zeros_like(acc_sh)
            acc_s[...] = jnp.zeros_like(acc_s)
            if has_prev:
                acc_gp[...] = jnp.zeros_like(acc_gp)

        x_ = x_ref[...]
        dh_ = dh_ref[...]
        g_ = g_ref[...]
        one_sc = 1.0 + sc_ref[...]
        r = lax.rsqrt(jnp.mean(x_ * x_, axis=-1, keepdims=True) + EPS)
        xn = x_ * r
        dxn = dh_ * (g_ * one_sc)
        dx = dxo_ref[...] + r * (dxn - xn * jnp.mean(dxn * xn, axis=-1, keepdims=True))
        dx_ref[...] = dx
        acc_sh[...] += _fold(dh_)
        acc_s[...] += _fold(dh_ * xn)
        if has_prev:
            dys_ref[...] = (dx * gp_ref[...]).astype(BF16)
            acc_gp[...] += _fold(dx * yp_ref[...])

        @pl.when(i == nb - 1)
        def _():
            s = jnp.sum(acc_s[...], axis=0, keepdims=True)
            dsh_ref[...] = jnp.sum(acc_sh[...], axis=0, keepdims=True)
            dsc_ref[...] = s * g_
            dg_ref[...] = s * one_sc
            if has_prev:
                dgp_ref[...] = jnp.sum(acc_gp[...], axis=0, keepdims=True)

    vec = _sds((1, d))
    in_specs = [_row(tl, d)] * 3 + [_const((1, d))] * 2
    out_shape = [_sds((n, d)), vec, vec, vec]
    out_specs = [_row(tl, d)] + [_const((1, d))] * 3
    scratch = [pltpu.VMEM((SUBLANES, d), F32)] * 2
    args = [dh, x, dxo, g, scale]
    if has_prev:
        in_specs += [_row(tl, d), _const((1, d))]
        out_shape += [_sds((n, d), BF16), vec]
        out_specs += [_row(tl, d), _const((1, d))]
        scratch += [pltpu.VMEM((SUBLANES, d), F32)]
        args += [y_prev, gate_prev]
    return _pc(body, name=name, out_shape=tuple(out_shape), grid=(nb,), in_specs=in_specs,
               out_specs=tuple(out_specs), scratch_shapes=scratch, compiler_params=_params(1))(*args)


CONV_HALO = 8
CONF_HALO = 32


def _ssd_pre(xbc, dt_raw, conv_w, conv_b, dt_bias, name):
    n, c = xbc.shape
    tl = _tile(n)
    hb = CONV_HALO
    k_taps = 4

    def body(x_ref, xp_ref, dt_ref, w_ref, b_ref, dtb_ref, pre_ref, dts_ref, buf):
        i = pl.program_id(0)
        buf[pl.ds(0, hb), :] = jnp.where(i > 0, xp_ref[...], 0.0)
        buf[pl.ds(hb, tl), :] = x_ref[...]
        acc = b_ref[...] + w_ref[0:1, :] * buf[pl.ds(hb - 3, tl), :]
        for k in range(1, k_taps):
            acc = acc + w_ref[k:k + 1, :] * buf[pl.ds(hb - 3 + k, tl), :]
        pre_ref[...] = acc
        v = dt_ref[...] + dtb_ref[...]
        dts_ref[...] = jnp.maximum(v, 0.0) + jnp.log1p(jnp.exp(-jnp.abs(v)))

    return _pc(body, name=name, out_shape=(_sds((n, c)), _sds((n, LANES))), grid=(n // tl,),
               in_specs=[_row(tl, c), _prev(tl, hb, c), _row(tl, LANES), _const((SUBLANES, c)), _const((1, c)),
                         _const((1, LANES))],
               out_specs=(_row(tl, c), _row(tl, LANES)),
               scratch_shapes=[pltpu.VMEM((tl + hb, c), F32)], compiler_params=_params(1))(
        xbc, xbc, dt_raw, conv_w, conv_b, dt_bias)


def _ssd_pre_bwd(dpre, xbc, ddt, dt_raw, conv_w, dt_bias, name):
    n, c = xbc.shape
    tl = _tile(n)
    nb = n // tl
    hb = CONV_HALO
    k_taps = 4

    def body(dp_ref, dpn_ref, x_ref, xp_ref, ddt_ref, dt_ref, w_ref, dtb_ref,
             dx_ref, ddr_ref, dw_ref, db_ref, ddtb_ref, dbuf, xbuf, acc_w, acc_b, acc_dtb):
        i = pl.program_id(0)

        @pl.when(i == 0)
        def _():
            acc_w[...] = jnp.zeros_like(acc_w)
            acc_b[...] = jnp.zeros_like(acc_b)
            acc_dtb[...] = jnp.zeros_like(acc_dtb)

        dp = dp_ref[...]
        dbuf[pl.ds(0, tl), :] = dp
        dbuf[pl.ds(tl, hb), :] = jnp.where(i < nb - 1, dpn_ref[...], 0.0)
        xbuf[pl.ds(0, hb), :] = jnp.where(i > 0, xp_ref[...], 0.0)
        xbuf[pl.ds(hb, tl), :] = x_ref[...]
        dx = w_ref[0:1, :] * dbuf[pl.ds(3, tl), :]
        for k in range(1, k_taps):
            dx = dx + w_ref[k:k + 1, :] * dbuf[pl.ds(3 - k, tl), :]
        dx_ref[...] = dx.astype(BF16)
        for k in range(k_taps):
            acc_w[k] += _fold(dp * xbuf[pl.ds(hb - 3 + k, tl), :])
        acc_b[...] += _fold(dp)
        ddr = ddt_ref[...] * _sig(dt_ref[...] + dtb_ref[...])
        ddr_ref[...] = ddr.astype(BF16)
        acc_dtb[...] += _fold(ddr)

        @pl.when(i == nb - 1)
        def _():
            dw_ref[...] = jnp.zeros_like(dw_ref)
            for k in range(k_taps):
                dw_ref[k:k + 1, :] = jnp.sum(acc_w[k], axis=0, keepdims=True)
            db_ref[...] = jnp.sum(acc_b[...], axis=0, keepdims=True)
            ddtb_ref[...] = jnp.sum(acc_dtb[...], axis=0, keepdims=True)

    return _pc(body, name=name,
               out_shape=(_sds((n, c), BF16), _sds((n, LANES), BF16), _sds((SUBLANES, c)), _sds((1, c)),
                          _sds((1, LANES))),
               grid=(nb,),
               in_specs=[_row(tl, c), _next(tl, hb, c, n), _row(tl, c), _prev(tl, hb, c), _row(tl, LANES),
                         _row(tl, LANES), _const((SUBLANES, c)), _const((1, LANES))],
               out_specs=(_row(tl, c), _row(tl, LANES), _const((SUBLANES, c)), _const((1, c)), _const((1, LANES))),
               scratch_shapes=[pltpu.VMEM((tl + hb, c), F32), pltpu.VMEM((tl + hb, c), F32),
                               pltpu.VMEM((k_taps, SUBLANES, c), F32), pltpu.VMEM((SUBLANES, c), F32),
                               pltpu.VMEM((SUBLANES, LANES), F32)],
               compiler_params=_params(1))(dpre, dpre, xbc, xbc, ddt, dt_raw, conv_w, dt_bias)


def _expand_mat():
    r = lax.broadcasted_iota(jnp.int32, (LANES, SSD_INNER), 0)
    c = lax.broadcasted_iota(jnp.int32, (LANES, SSD_INNER), 1)
    return (jnp.right_shift(c, 6) == r).astype(F32)


def _reduce_mat():
    r = lax.broadcasted_iota(jnp.int32, (SSD_INNER, LANES), 0)
    c = lax.broadcasted_iota(jnp.int32, (SSD_INNER, LANES), 1)
    return (jnp.right_shift(r, 6) == c).astype(F32)


def _ssd_common(pre, dt, alog):
    q = SSD_CHUNK
    sg = _sig(pre)
    act = pre * sg
    lane = lax.broadcasted_iota(jnp.int32, (1, LANES), 1)
    a_neg = jnp.where(lane < SSD_HEADS, -jnp.exp(alog), 0.0)
    rr = lax.broadcasted_iota(jnp.int32, (q, q), 0)
    cc = lax.broadcasted_iota(jnp.int32, (q, q), 1)
    causal = rr >= cc
    cum = _dot(causal.astype(F32), dt * a_neg, precision=HI)
    e_mat = _expand_mat()
    dtx = _dot(dt, e_mat, precision=HI)
    cumx = _dot(cum, e_mat, precision=HI)
    return sg, act, a_neg, causal, cum, e_mat, dtx, cumx


def _ssd_scan(pre, dt, alog, dvec, name):
    n = pre.shape[0]
    q = SSD_CHUNK
    nc = n // q

    def body(pre_ref, dt_ref, alog_ref, d_ref, y_ref, hp_ref, state):
        i = pl.program_id(0)

        @pl.when(i == 0)
        def _():
            state[...] = jnp.zeros_like(state)

        dt_ = dt_ref[...]
        _, act, _, causal, cum, e_mat, dtx, cumx = _ssd_common(pre_ref[...], dt_, alog_ref[...])
        xs = act[:, :SSD_INNER]
        bm = act[:, SSD_INNER:SSD_INNER + LANES]
        cm = act[:, SSD_INNER + LANES:]
        cum_t = cum.T
        clx = cumx[q - 1:q, :]
        xc = xs * dtx
        xd = xc * jnp.exp(clx - cumx)
        doutx = jnp.exp(cumx)
        edec = jnp.exp(clx)
        dx_row = _dot(jnp.broadcast_to(d_ref[...], (SUBLANES, LANES)), e_mat, precision=HI)[0:1, :]
        hp_ref[0] = state[...]
        bb = bm.astype(BF16)
        cb = cm.astype(BF16)
        lane = lax.broadcasted_iota(jnp.int32, (1, LANES), 1)
        row = lax.broadcasted_iota(jnp.int32, (LANES, 1), 0)
        cbs = []
        for g in range(2):
            cg = jnp.where(jnp.right_shift(lane, 6) == g, cm, 0.0).astype(BF16)
            cbs.append(_dot(cg, bb, NT))
        for j in range(SSD_HEADS // 2):
            sl = slice(j * LANES, (j + 1) * LANES)
            g = j // 4
            xcj = xc[:, sl].astype(BF16)
            halves = []
            for half in range(2):
                h = 2 * j + half
                seg = cum[:, h:h + 1] - cum_t[h:h + 1, :]
                w = cbs[g] * jnp.exp(jnp.where(causal, seg, -jnp.inf))
                halves.append(_dot(w.astype(BF16), xcj))
            y_diag = jnp.where(lane < SSD_HEAD_DIM, halves[0], halves[1])
            hj = state[:, sl]
            y_off = doutx[:, sl] * _dot(cb, hj.astype(BF16))
            y_ref[:, sl] = y_diag + y_off + xs[:, sl] * dx_row[:, sl]
            st = _dot(bb, xd[:, sl].astype(BF16), TN)
            state[:, sl] = hj * edec[:, sl] + jnp.where(jnp.right_shift(row, 6) == g, st, 0.0)

    return _pc(body, name=name, out_shape=(_sds((n, SSD_INNER)), _sds((nc, LANES, SSD_INNER))), grid=(nc,),
               in_specs=[_row(q, SSD_XBC), _row(q, LANES), _const((1, LANES)), _const((1, LANES))],
               out_specs=(_row(q, SSD_INNER), pl.BlockSpec((1, LANES, SSD_INNER), lambda i: (i, 0, 0))),
               scratch_shapes=[pltpu.VMEM((LANES, SSD_INNER), F32)], compiler_params=_params(1))(pre, dt, alog, dvec)


def _ssd_scan_bwd(pre, dt, hprev, dy, alog, dvec, name):
    n = pre.shape[0]
    q = SSD_CHUNK
    nc = n // q

    def body(pre_ref, dt_ref, hp_ref, dy_ref, alog_ref, d_ref, dpre_ref, ddt_ref, da_ref, dd_ref,
             d_state, dxc_s, dcx_s, dcl_s, acc_a, acc_d):
        i = pl.program_id(0)

        @pl.when(i == 0)
        def _():
            d_state[...] = jnp.zeros_like(d_state)
            acc_a[...] = jnp.zeros_like(acc_a)
            acc_d[...] = jnp.zeros_like(acc_d)

        pre_ = pre_ref[...]
        dt_ = dt_ref[...]
        sg, act, a_neg, causal, cum, e_mat, dtx, cumx = _ssd_common(pre_, dt_, alog_ref[...])
        r_mat = _reduce_mat()
        xs = act[:, :SSD_INNER]
        bm = act[:, SSD_INNER:SSD_INNER + LANES]
        cm = act[:, SSD_INNER + LANES:]
        cum_t = cum.T
        clx = cumx[q - 1:q, :]
        xc = xs * dtx
        dsx = jnp.exp(clx - cumx)
        doutx = jnp.exp(cumx)
        edec = jnp.exp(clx)
        dx_row = _dot(jnp.broadcast_to(d_ref[...], (SUBLANES, LANES)), e_mat, precision=HI)[0:1, :]
        dy_ = dy_ref[...]
        acc_d[...] += _fold(dy_ * xs)
        bb = bm.astype(BF16)
        cb = cm.astype(BF16)
        lane = lax.broadcasted_iota(jnp.int32, (1, LANES), 1)
        row = lax.broadcasted_iota(jnp.int32, (LANES, 1), 0)
        d_c = jnp.zeros((q, LANES), F32)
        d_b = jnp.zeros((q, LANES), F32)

        for j in range(SSD_HEADS // 2):
            sl = slice(j * LANES, (j + 1) * LANES)
            g = j // 4
            hj = hp_ref[0, :, sl]
            hjb = hj.astype(BF16)
            dyj = dy_[:, sl]
            tj = _dot(cb, hjb)
            dtj = (doutx[:, sl] * dyj).astype(BF16)
            dcx = dyj * tj * doutx[:, sl]
            d_c = d_c + _dot(dtj, hjb, NT)
            dhn = d_state[:, sl]
            dhp = dhn * edec[:, sl] + jnp.where(jnp.right_shift(row, 6) == g, _dot(cb, dtj, TN), 0.0)
            dcl = jnp.sum(dhn * hj, axis=0, keepdims=True) * edec[:, sl]
            dsb = dhn.astype(BF16)
            dxd = _dot(bb, dsb)
            xcj = xc[:, sl]
            dsj = dsx[:, sl]
            d_b = d_b + _dot((xcj * dsj).astype(BF16), dsb, NT)
            dds = dxd * xcj * dsj
            d_state[:, sl] = dhp
            dxc_s[:, sl] = dxd * dsj
            dcx_s[:, sl] = dcx - dds
            dcl_s[:, sl] = jnp.broadcast_to(dcl + jnp.sum(dds, axis=0, keepdims=True), (SUBLANES, LANES))

        dcum_c = jnp.zeros((q, LANES), F32)
        dcum_t = jnp.zeros((LANES, q), F32)
        for g in range(2):
            gmask = jnp.right_shift(lane, 6) == g
            cg = jnp.where(gmask, cm, 0.0).astype(BF16)
            cbg = _dot(cg, bb, NT)
            d_cb = jnp.zeros((q, q), F32)
            for hh in range(SSD_HEADS // 2):
                h = g * (SSD_HEADS // 2) + hh
                j, half = h // 2, h % 2
                sl = slice(j * LANES, (j + 1) * LANES)
                hmask = jnp.right_shift(lane, 6) == half
                seg = cum[:, h:h + 1] - cum_t[h:h + 1, :]
                lm = jnp.exp(jnp.where(causal, seg, -jnp.inf))
                w = cbg * lm
                dyj = dy_[:, sl]
                dw = _dot(jnp.where(hmask, dyj, 0.0).astype(BF16), xc[:, sl].astype(BF16), NT)
                dxch = _dot(w.astype(BF16), dyj.astype(BF16), TN)
                dxc_s[:, sl] += jnp.where(hmask, dxch, 0.0)
                d_cb = d_cb + dw * lm
                m = dw * w
                dcum_c = dcum_c + jnp.sum(m, axis=1, keepdims=True) * (lane == h).astype(F32)
                dcum_t = dcum_t + (row == h).astype(F32) * jnp.sum(m, axis=0, keepdims=True)
            d_cbb = d_cb.astype(BF16)
            d_c = d_c + jnp.where(gmask, _dot(d_cbb, bb), 0.0)
            d_b = d_b + jnp.where(gmask, _dot(d_cbb, cb, TN), 0.0)

        dcl_row = _dot(dcl_s[...], r_mat, precision=HI)[0:1, :]
        rowq = lax.broadcasted_iota(jnp.int32, (q, 1), 0)
        dcum = (dcum_c - dcum_t.T + _dot(dcx_s[...], r_mat, precision=HI)
                + jnp.where(rowq == q - 1, dcl_row, 0.0))
        rr = lax.broadcasted_iota(jnp.int32, (q, q), 0)
        cc = lax.broadcasted_iota(jnp.int32, (q, q), 1)
        dadt = _dot((rr <= cc).astype(F32), dcum, precision=HI)
        dxc = dxc_s[...]
        ddt_ref[...] = dadt * a_neg + _dot(dxc * xs, r_mat, precision=HI)
        acc_a[...] += _fold(dadt * dt_)
        dsilu = sg * (1.0 + pre_ * (1.0 - sg))
        dpre_ref[:, :SSD_INNER] = (dxc * dtx + dy_ * dx_row) * dsilu[:, :SSD_INNER]
        dpre_ref[:, SSD_INNER:SSD_INNER + LANES] = d_b * dsilu[:, SSD_INNER:SSD_INNER + LANES]
        dpre_ref[:, SSD_INNER + LANES:] = d_c * dsilu[:, SSD_INNER + LANES:]

        @pl.when(i == nc - 1)
        def _():
            da_ref[...] = jnp.sum(acc_a[...], axis=0, keepdims=True) * a_neg
            dd_ref[...] = jnp.sum(_dot(acc_d[...], r_mat, precision=HI), axis=0, keepdims=True)

    rev = lambda i: (nc - 1 - i, 0)
    return _pc(body, name=name,
               out_shape=(_sds((n, SSD_XBC)), _sds((n, LANES)), _sds((1, LANES)), _sds((1, LANES))), grid=(nc,),
               in_specs=[pl.BlockSpec((q, SSD_XBC), rev), pl.BlockSpec((q, LANES), rev),
                         pl.BlockSpec((1, LANES, SSD_INNER), lambda i: (nc - 1 - i, 0, 0)),
                         pl.BlockSpec((q, SSD_INNER), rev), _const((1, LANES)), _const((1, LANES))],
               out_specs=(pl.BlockSpec((q, SSD_XBC), rev), pl.BlockSpec((q, LANES), rev), _const((1, LANES)),
                          _const((1, LANES))),
               scratch_shapes=[pltpu.VMEM((LANES, SSD_INNER), F32), pltpu.VMEM((q, SSD_INNER), F32),
                               pltpu.VMEM((q, SSD_INNER), F32), pltpu.VMEM((SUBLANES, SSD_INNER), F32),
                               pltpu.VMEM((SUBLANES, LANES), F32), pltpu.VMEM((SUBLANES, SSD_INNER), F32)],
               compiler_params=_params(1))(pre, dt, hprev, dy, alog, dvec)


def _group_norm_parts(v):
    half = SSD_INNER // 2
    r0 = lax.rsqrt(jnp.mean(v[:, :half] * v[:, :half], axis=-1, keepdims=True) + EPS)
    r1 = lax.rsqrt(jnp.mean(v[:, half:] * v[:, half:], axis=-1, keepdims=True) + EPS)
    lane = lax.broadcasted_iota(jnp.int32, (1, SSD_INNER), 1)
    return jnp.where(lane < half, r0, r1)


def _group_mean(v):
    half = SSD_INNER // 2
    m0 = jnp.mean(v[:, :half], axis=-1, keepdims=True)
    m1 = jnp.mean(v[:, half:], axis=-1, keepdims=True)
    lane = lax.broadcasted_iota(jnp.int32, (1, SSD_INNER), 1)
    return jnp.where(lane < half, m0, m1)


def _ssd_post(y, z, g, name):
    n, d = y.shape
    tl = _tile(n)

    def body(y_ref, z_ref, g_ref, o_ref):
        z_ = z_ref[...]
        v = y_ref[...] * (z_ * _sig(z_))
        o_ref[...] = ((v * _group_norm_parts(v)) * g_ref[...]).astype(BF16)

    return _pc(body, name=name, out_shape=_sds((n, d), BF16), grid=(n // tl,),
               in_specs=[_row(tl, d), _row(tl, d), _const((1, d))], out_specs=_row(tl, d),
               compiler_params=_params(1))(y, z, g)


def _ssd_post_bwd(dout, y, z, g, name):
    n, d = y.shape
    tl = _tile(n)
    nb = n // tl

    def body(do_ref, y_ref, z_ref, g_ref, dy_ref, dz_ref, dg_ref, acc_g):
        i = pl.program_id(0)

        @pl.when(i == 0)
        def _():
            acc_g[...] = jnp.zeros_like(acc_g)

        z_ = z_ref[...]
        y_ = y_ref[...]
        sz = _sig(z_)
        silu_z = z_ * sz
        v = y_ * silu_z
        rs = _group_norm_parts(v)
        nv = v * rs
        do_ = do_ref[...]
        acc_g[...] += _fold(do_ * nv)
        dn = do_ * g_ref[...]
        dv = rs * (dn - nv * _group_mean(dn * nv))
        dy_ref[...] = dv * silu_z
        dz_ref[...] = (dv * y_ * (sz * (1.0 + z_ * (1.0 - sz)))).astype(BF16)

        @pl.when(i == nb - 1)
        def _():
            dg_ref[...] = jnp.sum(acc_g[...], axis=0, keepdims=True)

    return _pc(body, name=name, out_shape=(_sds((n, d)), _sds((n, d), BF16), _sds((1, d))), grid=(nb,),
               in_specs=[_row(tl, d), _row(tl, d), _row(tl, d), _const((1, d))],
               out_specs=(_row(tl, d), _row(tl, d), _const((1, d))),
               scratch_shapes=[pltpu.VMEM((SUBLANES, d), F32)], compiler_params=_params(1))(dout, y, z, g)


def _layer_norm_parts(uc):
    mu = jnp.mean(uc, axis=-1, keepdims=True)
    xc = uc - mu
    rstd = lax.rsqrt(jnp.mean(xc * xc, axis=-1, keepdims=True) + EPS)
    return xc * rstd, rstd


def _conf_fwd(conf_in, conv_w, conv_b, ln_g, ln_b, name):
    n = conf_in.shape[0]
    c = CONF_WIDTH
    tl = _tile(n)
    hb = CONF_HALO
    k_taps = CONF_KERNEL

    def body(x_ref, xp_ref, w_ref, b_ref, g_ref, beta_ref, o_ref, uc_ref, buf):
        i = pl.program_id(0)
        xp = xp_ref[...]
        buf[pl.ds(0, hb), :] = jnp.where(i > 0, xp[:, :c] * _sig(xp[:, c:]), 0.0)
        x_ = x_ref[...]
        buf[pl.ds(hb, tl), :] = x_[:, :c] * _sig(x_[:, c:])
        acc = b_ref[...] + w_ref[0:1, :] * buf[pl.ds(hb - (k_taps - 1), tl), :]
        for k in range(1, k_taps):
            acc = acc + w_ref[k:k + 1, :] * buf[pl.ds(hb - (k_taps - 1) + k, tl), :]
        uc_ref[...] = acc
        nv, _ = _layer_norm_parts(acc)
        v = nv * g_ref[...] + beta_ref[...]
        o_ref[...] = (v * _sig(v)).astype(BF16)

    return _pc(body, name=name, out_shape=(_sds((n, c), BF16), _sds((n, c))), grid=(n // tl,),
               in_specs=[_row(tl, 2 * c), _prev(tl, hb, 2 * c), _const((hb, c)), _const((1, c)), _const((1, c)),
                         _const((1, c))],
               out_specs=(_row(tl, c), _row(tl, c)),
               scratch_shapes=[pltpu.VMEM((tl + hb, c), F32)], compiler_params=_params(1))(
        conf_in, conf_in, conv_w, conv_b, ln_g, ln_b)


def _conf_bwd(dout, uc, conf_in, conv_w, ln_g, ln_b, name):
    n = conf_in.shape[0]
    c = CONF_WIDTH
    tl = _tile(n)
    nb = n // tl
    hb = CONF_HALO
    k_taps = CONF_KERNEL

    def body(do_ref, don_ref, uc_ref, ucn_ref, x_ref, xp_ref, w_ref, g_ref, beta_ref,
             dx_ref, dw_ref, db_ref, dg_ref, dbeta_ref, dbuf, ubuf, acc_w, acc_b, acc_g, acc_beta):
        i = pl.program_id(0)

        @pl.when(i == 0)
        def _():
            acc_w[...] = jnp.zeros_like(acc_w)
            acc_b[...] = jnp.zeros_like(acc_b)
            acc_g[...] = jnp.zeros_like(acc_g)
            acc_beta[...] = jnp.zeros_like(acc_beta)

        g_ = g_ref[...]
        beta_ = beta_ref[...]

        def d_conv_out(do_, uc_):
            nv, rstd = _layer_norm_parts(uc_)
            v = nv * g_ + beta_
            sv = _sig(v)
            dv = do_ * (sv * (1.0 + v * (1.0 - sv)))
            dn = dv * g_
            duc = rstd * (dn - jnp.mean(dn, axis=-1, keepdims=True)
                          - nv * jnp.mean(dn * nv, axis=-1, keepdims=True))
            return duc, dv, nv

        duc, dv, nv = d_conv_out(do_ref[...], uc_ref[...])
        acc_g[...] += _fold(dv * nv)
        acc_beta[...] += _fold(dv)
        acc_b[...] += _fold(duc)
        dbuf[pl.ds(0, tl), :] = duc
        ducn, _, _ = d_conv_out(don_ref[...], ucn_ref[...])
        dbuf[pl.ds(tl, hb), :] = jnp.where(i < nb - 1, ducn, 0.0)
        xp = xp_ref[...]
        ubuf[pl.ds(0, hb), :] = jnp.where(i > 0, xp[:, :c] * _sig(xp[:, c:]), 0.0)
        x_ = x_ref[...]
        val = x_[:, :c]
        sgate = _sig(x_[:, c:])
        ubuf[pl.ds(hb, tl), :] = val * sgate
        du = w_ref[0:1, :] * dbuf[pl.ds(k_taps - 1, tl), :]
        for k in range(1, k_taps):
            du = du + w_ref[k:k + 1, :] * dbuf[pl.ds(k_taps - 1 - k, tl), :]
        for k in range(k_taps):
            acc_w[k] += _fold(duc * ubuf[pl.ds(hb - (k_taps - 1) + k, tl), :])
        dx_ref[:, :c] = (du * sgate).astype(BF16)
        dx_ref[:, c:] = (du * val * sgate * (1.0 - sgate)).astype(BF16)

        @pl.when(i == nb - 1)
        def _():
            dw_ref[...] = jnp.zeros_like(dw_ref)
            for k in range(k_taps):
                dw_ref[k:k + 1, :] = jnp.sum(acc_w[k], axis=0, keepdims=True)
            db_ref[...] = jnp.sum(acc_b[...], axis=0, keepdims=True)
            dg_ref[...] = jnp.sum(acc_g[...], axis=0, keepdims=True)
            dbeta_ref[...] = jnp.sum(acc_beta[...], axis=0, keepdims=True)

    vec = _sds((1, c))
    return _pc(body, name=name, out_shape=(_sds((n, 2 * c), BF16), _sds((hb, c)), vec, vec, vec), grid=(nb,),
               in_specs=[_row(tl, c), _next(tl, hb, c, n), _row(tl, c), _next(tl, hb, c, n), _row(tl, 2 * c),
                         _prev(tl, hb, 2 * c), _const((hb, c)), _const((1, c)), _const((1, c))],
               out_specs=(_row(tl, 2 * c), _const((hb, c)), _const((1, c)), _const((1, c)), _const((1, c))),
               scratch_shapes=[pltpu.VMEM((tl + hb, c), F32), pltpu.VMEM((tl + hb, c), F32),
                               pltpu.VMEM((k_taps, SUBLANES, c), F32), pltpu.VMEM((SUBLANES, c), F32),
                               pltpu.VMEM((SUBLANES, c), F32), pltpu.VMEM((SUBLANES, c), F32)],
               compiler_params=_params(1))(dout, dout, uc, uc, conf_in, conf_in, conv_w, ln_g, ln_b)


def _sc_fwd(sc_in, conv_w, name):
    n = sc_in.shape[0]
    c = SC_WIDTH
    tl = _tile(n)
    hb = CONV_HALO

    def body(x_ref, xp_ref, w_ref, o_ref, buf):
        i = pl.program_id(0)
        xp = xp_ref[...]
        buf[pl.ds(0, hb), :] = jnp.where(i > 0, xp[:, c:2 * c] * xp[:, 2 * c:], 0.0)
        x_ = x_ref[...]
        buf[pl.ds(hb, tl), :] = x_[:, c:2 * c] * x_[:, 2 * c:]
        cv = w_ref[0:1, :] * buf[pl.ds(hb - 2, tl), :]
        for k in range(1, 3):
            cv = cv + w_ref[k:k + 1, :] * buf[pl.ds(hb - 2 + k, tl), :]
        o_ref[...] = (x_[:, :c] * cv).astype(BF16)

    return _pc(body, name=name, out_shape=_sds((n, c), BF16), grid=(n // tl,),
               in_specs=[_row(tl, 3 * c), _prev(tl, hb, 3 * c), _const((SUBLANES, c))], out_specs=_row(tl, c),
               scratch_shapes=[pltpu.VMEM((tl + hb, c), F32)], compiler_params=_params(1))(sc_in, sc_in, conv_w)


def _sc_bwd(dout, sc_in, conv_w, name):
    n = sc_in.shape[0]
    c = SC_WIDTH
    tl = _tile(n)
    nb = n // tl
    hb = CONV_HALO

    def body(do_ref, don_ref, x_ref, xp_ref, xn_ref, w_ref, dx_ref, dw_ref, dbuf, pbuf, acc_w):
        i = pl.program_id(0)

        @pl.when(i == 0)
        def _():
            acc_w[...] = jnp.zeros_like(acc_w)

        x_ = x_ref[...]
        gb, gc, xv = x_[:, :c], x_[:, c:2 * c], x_[:, 2 * c:]
        do_ = do_ref[...]
        dcv = do_ * gb
        dbuf[pl.ds(0, tl), :] = dcv
        dbuf[pl.ds(tl, hb), :] = jnp.where(i < nb - 1, don_ref[...] * xn_ref[...], 0.0)
        xp = xp_ref[...]
        pbuf[pl.ds(0, hb), :] = jnp.where(i > 0, xp[:, c:2 * c] * xp[:, 2 * c:], 0.0)
        pbuf[pl.ds(hb, tl), :] = gc * xv
        cv = w_ref[0:1, :] * pbuf[pl.ds(hb - 2, tl), :]
        dp = w_ref[0:1, :] * dbuf[pl.ds(2, tl), :]
        for k in range(1, 3):
            cv = cv + w_ref[k:k + 1, :] * pbuf[pl.ds(hb - 2 + k, tl), :]
            dp = dp + w_ref[k:k + 1, :] * dbuf[pl.ds(2 - k, tl), :]
        for k in range(3):
            acc_w[k] += _fold(dcv * pbuf[pl.ds(hb - 2 + k, tl), :])
        dx_ref[:, :c] = (do_ * cv).astype(BF16)
        dx_ref[:, c:2 * c] = (dp * xv).astype(BF16)
        dx_ref[:, 2 * c:] = (dp * gc).astype(BF16)

        @pl.when(i == nb - 1)
        def _():
            dw_ref[...] = jnp.zeros_like(dw_ref)
            for k in range(3):
                dw_ref[k:k + 1, :] = jnp.sum(acc_w[k], axis=0, keepdims=True)

    return _pc(body, name=name, out_shape=(_sds((n, 3 * c), BF16), _sds((SUBLANES, c))), grid=(nb,),
               in_specs=[_row(tl, c), _next(tl, hb, c, n), _row(tl, 3 * c), _prev(tl, hb, 3 * c),
                         _next(tl, hb, c, n), _const((SUBLANES, c))],
               out_specs=(_row(tl, 3 * c), _const((SUBLANES, c))),
               scratch_shapes=[pltpu.VMEM((tl + hb, c), F32), pltpu.VMEM((tl + hb, c), F32),
                               pltpu.VMEM((3, SUBLANES, c), F32)],
               compiler_params=_params(1))(dout, dout, sc_in, sc_in, sc_in, conv_w)


def _merge_fwd(gates, ya, yb, yc, b_gate, name):
    n, d = ya.shape
    tl = _tile(n)

    def body(gt_ref, ya_ref, yb_ref, yc_ref, b_ref, o_ref):
        gt = _sig(gt_ref[...] + b_ref[...])
        o_ref[...] = (gt[:, :d] * ya_ref[...] + gt[:, d:2 * d] * yb_ref[...] + gt[:, 2 * d:] * yc_ref[...]).astype(BF16)

    return _pc(body, name=name, out_shape=_sds((n, d), BF16), grid=(n // tl,),
               in_specs=[_row(tl, 3 * d), _row(tl, d), _row(tl, d), _row(tl, d), _const((1, 3 * d))],
               out_specs=_row(tl, d), compiler_params=_params(1))(gates, ya, yb, yc, b_gate)


def _merge_bwd(dm, gates, ya, yb, yc, b_gate, name):
    n, d = ya.shape
    tl = _tile(n)
    nb = n // tl

    def body(dm_ref, gt_ref, ya_ref, yb_ref, yc_ref, b_ref, dya_ref, dyb_ref, dyc_ref, dgt_ref, db_ref, acc):
        i = pl.program_id(0)

        @pl.when(i == 0)
        def _():
            acc[...] = jnp.zeros_like(acc)

        dm_ = dm_ref[...]
        gt = _sig(gt_ref[...] + b_ref[...])
        for idx, (y_ref, dy_ref) in enumerate(((ya_ref, dya_ref), (yb_ref, dyb_ref), (yc_ref, dyc_ref))):
            gk = gt[:, idx * d:(idx + 1) * d]
            dy_ref[...] = (dm_ * gk).astype(BF16)
            dpre = dm_ * y_ref[...] * gk * (1.0 - gk)
            dgt_ref[:, idx * d:(idx + 1) * d] = dpre.astype(BF16)
            acc[:, idx * d:(idx + 1) * d] += _fold(dpre)

        @pl.when(i == nb - 1)
        def _():
            db_ref[...] = jnp.sum(acc[...], axis=0, keepdims=True)

    bf = _sds((n, d), BF16)
    return _pc(body, name=name, out_shape=(bf, bf, bf, _sds((n, 3 * d), BF16), _sds((1, 3 * d))), grid=(nb,),
               in_specs=[_row(tl, d), _row(tl, 3 * d), _row(tl, d), _row(tl, d), _row(tl, d), _const((1, 3 * d))],
               out_specs=(_row(tl, d), _row(tl, d), _row(tl, d), _row(tl, 3 * d), _const((1, 3 * d))),
               scratch_shapes=[pltpu.VMEM((SUBLANES, 3 * d), F32)], compiler_params=_params(1))(
        dm, gates, ya, yb, yc, b_gate)


FFN_COLS = 1408


def _ffn_mid(up, conv_w, conv_b, name):
    n = up.shape[0]
    tl = _tile(n)
    hb = CONV_HALO
    tc = FFN_COLS
    ncb = D_FF // tc

    def spec(shape_rows, idx_fn, off):
        return pl.BlockSpec((shape_rows, tc), lambda j, i, off=off: (idx_fn(i), j + off))

    r = tl // hb
    cur = lambda i: i
    prv = lambda i: jnp.maximum(i * r - 1, 0)

    def conv(buf, w_ref, b_ref):
        acc = b_ref[...] + w_ref[0:1, :] * buf[pl.ds(hb - 2, tl), :]
        for k in range(1, 3):
            acc = acc + w_ref[k:k + 1, :] * buf[pl.ds(hb - 2 + k, tl), :]
        return acc

    def body(g_ref, gp_ref, v_ref, vp_ref, wg_ref, wv_ref, bg_ref, bv_ref, o_ref, gbuf, vbuf):
        i = pl.program_id(1)
        gbuf[pl.ds(0, hb), :] = jnp.where(i > 0, gp_ref[...], 0.0)
        gbuf[pl.ds(hb, tl), :] = g_ref[...]
        vbuf[pl.ds(0, hb), :] = jnp.where(i > 0, vp_ref[...], 0.0)
        vbuf[pl.ds(hb, tl), :] = v_ref[...]
        ug = conv(gbuf, wg_ref, bg_ref)
        uv = conv(vbuf, wv_ref, bv_ref)
        o_ref[...] = (ug * _sig(ug) * uv).astype(BF16)

    wspec = lambda off: pl.BlockSpec((SUBLANES, tc), lambda j, i, off=off: (0, j + off))
    bspec = lambda off: pl.BlockSpec((1, tc), lambda j, i, off=off: (0, j + off))
    return _pc(body, name=name, out_shape=_sds((n, D_FF), BF16), grid=(ncb, n // tl),
               in_specs=[spec(tl, cur, 0), spec(hb, prv, 0), spec(tl, cur, ncb), spec(hb, prv, ncb),
                         wspec(0), wspec(ncb), bspec(0), bspec(ncb)],
               out_specs=pl.BlockSpec((tl, tc), lambda j, i: (i, j)),
               scratch_shapes=[pltpu.VMEM((tl + hb, tc), F32)] * 2, compiler_params=_params(2))(
        up, up, up, up, conv_w, conv_w, conv_b, conv_b)


def _ffn_mid_bwd(da, up, conv_w, conv_b, name):
    n = up.shape[0]
    tl = _tile(n)
    nb = n // tl
    hb = CONV_HALO
    tc = FFN_COLS
    ncb = D_FF // tc
    r = tl // hb
    last = n // hb - 1
    cur = lambda i: i
    prv = lambda i: jnp.maximum(i * r - 1, 0)
    nxt = lambda i: jnp.minimum((i + 1) * r, last)

    def spec(shape_rows, idx_fn, off):
        return pl.BlockSpec((shape_rows, tc), lambda j, i, off=off: (idx_fn(i), j + off))

    def body(da_ref, dan_ref, g_ref, gp_ref, gn_ref, v_ref, vp_ref, vn_ref, wg_ref, wv_ref, bg_ref, bv_ref,
             dg_ref, dv_ref, dwg_ref, dwv_ref, dbg_ref, dbv_ref, gbuf, vbuf, dgbuf, dvbuf, acc_w, acc_b):
        i = pl.program_id(1)

        @pl.when(i == 0)
        def _():
            acc_w[...] = jnp.zeros_like(acc_w)
            acc_b[...] = jnp.zeros_like(acc_b)

        rows = tl + 2 * hb
        gbuf[pl.ds(0, hb), :] = jnp.where(i > 0, gp_ref[...], 0.0)
        gbuf[pl.ds(hb, tl), :] = g_ref[...]
        gbuf[pl.ds(hb + tl, hb), :] = gn_ref[...]
        vbuf[pl.ds(0, hb), :] = jnp.where(i > 0, vp_ref[...], 0.0)
        vbuf[pl.ds(hb, tl), :] = v_ref[...]
        vbuf[pl.ds(hb + tl, hb), :] = vn_ref[...]

        def conv(buf, w_ref, b_ref, start, cnt):
            acc = b_ref[...] + w_ref[0:1, :] * buf[pl.ds(start - 2, cnt), :]
            for k in range(1, 3):
                acc = acc + w_ref[k:k + 1, :] * buf[pl.ds(start - 2 + k, cnt), :]
            return acc

        def d_conv_out(da_, ug, uv):
            s = _sig(ug)
            return da_ * uv * (s * (1.0 + ug * (1.0 - s))), da_ * (ug * s)

        dug, duv = d_conv_out(da_ref[...], conv(gbuf, wg_ref, bg_ref, hb, tl), conv(vbuf, wv_ref, bv_ref, hb, tl))
        dgbuf[pl.ds(0, tl), :] = dug
        dvbuf[pl.ds(0, tl), :] = duv
        dugn, duvn = d_conv_out(dan_ref[...], conv(gbuf, wg_ref, bg_ref, hb + tl, hb),
                                conv(vbuf, wv_ref, bv_ref, hb + tl, hb))
        dgbuf[pl.ds(tl, hb), :] = jnp.where(i < nb - 1, dugn, 0.0)
        dvbuf[pl.ds(tl, hb), :] = jnp.where(i < nb - 1, duvn, 0.0)
        for (dbuf, w_ref, out_ref) in ((dgbuf, wg_ref, dg_ref), (dvbuf, wv_ref, dv_ref)):
            dx = w_ref[0:1, :] * dbuf[pl.ds(2, tl), :]
            for k in range(1, 3):
                dx = dx + w_ref[k:k + 1, :] * dbuf[pl.ds(2 - k, tl), :]
            out_ref[...] = dx.astype(BF16)
        for k in range(3):
            acc_w[0, k] += _fold(dug * gbuf[pl.ds(hb - 2 + k, tl), :])
            acc_w[1, k] += _fold(duv * vbuf[pl.ds(hb - 2 + k, tl), :])
        acc_b[0] += _fold(dug)
        acc_b[1] += _fold(duv)

        @pl.when(i == nb - 1)
        def _():
            for t, (dw_ref, db_ref) in enumerate(((dwg_ref, dbg_ref), (dwv_ref, dbv_ref))):
                dw_ref[...] = jnp.zeros_like(dw_ref)
                for k in range(3):
                    dw_ref[k:k + 1, :] = jnp.sum(acc_w[t, k], axis=0, keepdims=True)
                db_ref[...] = jnp.sum(acc_b[t], axis=0, keepdims=True)

    wspec = lambda off: pl.BlockSpec((SUBLANES, tc), lambda j, i, off=off: (0, j + off))
    bspec = lambda off: pl.BlockSpec((1, tc), lambda j, i, off=off: (0, j + off))
    ospec = lambda off: pl.BlockSpec((tl, tc), lambda j, i, off=off: (i, j + off))
    dg, dv, dwg, dwv, dbg, dbv = _pc(
        body, name=name,
        out_shape=(_sds((n, D_FF), BF16), _sds((n, D_FF), BF16), _sds((SUBLANES, D_FF)), _sds((SUBLANES, D_FF)),
                   _sds((1, D_FF)), _sds((1, D_FF))),
        grid=(ncb, nb),
        in_specs=[spec(tl, cur, 0), spec(hb, nxt, 0),
                  spec(tl, cur, 0), spec(hb, prv, 0), spec(hb, nxt, 0),
                  spec(tl, cur, ncb), spec(hb, prv, ncb), spec(hb, nxt, ncb),
                  wspec(0), wspec(ncb), bspec(0), bspec(ncb)],
        out_specs=(ospec(0), ospec(0), wspec(0), wspec(0), bspec(0), bspec(0)),
        scratch_shapes=[pltpu.VMEM((tl + 2 * hb, tc), F32)] * 2 + [pltpu.VMEM((tl + hb, tc), F32)] * 2
        + [pltpu.VMEM((2, 3, SUBLANES, tc), F32), pltpu.VMEM((2, SUBLANES, tc), F32)],
        compiler_params=_params(2))(da, da, up, up, up, up, up, up, conv_w, conv_w, conv_b, conv_b)
    return dg, dv, jnp.concatenate([dwg, dwv], axis=1), jnp.concatenate([dbg, dbv], axis=1)


def _position():
    return lax.axis_index("x"), lax.axis_index("y"), lax.axis_index("c")


def _all_gather(locals_, name):
    n = len(locals_)

    def body(*refs):
        x_refs, out_refs = refs[:n], refs[n:2 * n]
        send_sems, recv_sems, local_sems = refs[2 * n:]
        x, y, cc = _position()
        me, sibling = (x, y, cc), (x, y, 1 - cc)
        chips = [(1 - x, y), (x, 1 - y), (1 - x, 1 - y)]

        def slot(a, px, py, pc):
            return out_refs[a].at[4 * px + 2 * py + pc]

        def copy(k, a, block, to, own=False):
            return pltpu.make_async_remote_copy(
                src_ref=x_refs[a] if own else slot(a, *block), dst_ref=slot(a, *block),
                send_sem=send_sems.at[k, a], recv_sem=recv_sems.at[k, a], device_id=to, device_id_type=MESH)

        mine = [pltpu.make_async_copy(x_refs[a], slot(a, *me), local_sems.at[a]) for a in range(n)]
        first = [copy(1 + j, a, me, (*chip, cc), own=True) for j, chip in enumerate(chips) for a in range(n)]
        first += [copy(0, a, me, sibling, own=True) for a in range(n)]
        for cp in mine + first:
            cp.start()
        passed = []
        for j, chip in enumerate(chips):
            for a in range(n):
                copy(1 + j, a, (*chip, cc), me).wait_recv()
                cp = copy(4 + j, a, (*chip, cc), sibling)
                cp.start()
                passed.append(cp)
        for a in range(n):
            copy(0, a, sibling, me).wait_recv()
        for j, chip in enumerate(chips):
            for a in range(n):
                copy(4 + j, a, (*chip, 1 - cc), me).wait_recv()
        for cp in first + passed:
            cp.wait_send()
        for cp in mine:
            cp.wait()

    hbm = pl.BlockSpec(memory_space=pl.ANY)
    return _pc(body, name=name, out_shape=[_sds((N_DEV,) + a.shape, a.dtype) for a in locals_],
               in_specs=[hbm] * n, out_specs=[hbm] * n,
               scratch_shapes=[pltpu.SemaphoreType.DMA((7, n)), pltpu.SemaphoreType.DMA((7, n)),
                               pltpu.SemaphoreType.DMA((n,))])(*locals_)


def _exchange(parts, name):
    n = len(parts)
    flips = [(0, 0, 1), (1, 0, 0), (0, 1, 0), (1, 1, 0), (1, 0, 1), (0, 1, 1), (1, 1, 1)]

    def body(*refs):
        g_refs, out_refs = refs[:n], refs[n:2 * n]
        send_sems, recv_sems, local_sems = refs[2 * n:]
        x, y, cc = _position()
        me_idx = 4 * x + 2 * y + cc
        mine = [pltpu.make_async_copy(g_refs[a].at[me_idx], out_refs[a].at[me_idx], local_sems.at[a])
                for a in range(n)]
        sends, recvs = [], []
        for k, (fx, fy, fc) in enumerate(flips):
            peer = (1 - x if fx else x, 1 - y if fy else y, 1 - cc if fc else cc)
            peer_idx = 4 * peer[0] + 2 * peer[1] + peer[2]
            for a in range(n):
                sends.append(pltpu.make_async_remote_copy(
                    src_ref=g_refs[a].at[peer_idx], dst_ref=out_refs[a].at[me_idx], send_sem=send_sems.at[k, a],
                    recv_sem=recv_sems.at[k, a], device_id=peer, device_id_type=MESH))
                recvs.append(pltpu.make_async_remote_copy(
                    src_ref=g_refs[a].at[me_idx], dst_ref=out_refs[a].at[peer_idx], send_sem=send_sems.at[k, a],
                    recv_sem=recv_sems.at[k, a], device_id=peer, device_id_type=MESH))
        for cp in mine + sends:
            cp.start()
        for cp in recvs:
            cp.wait_recv()
        for cp in sends:
            cp.wait_send()
        for cp in mine:
            cp.wait()

    hbm = pl.BlockSpec(memory_space=pl.ANY)
    return _pc(body, name=name, out_shape=[_sds(a.shape, a.dtype) for a in parts],
               in_specs=[hbm] * n, out_specs=[hbm] * n,
               scratch_shapes=[pltpu.SemaphoreType.DMA((7, n)), pltpu.SemaphoreType.DMA((7, n)),
                               pltpu.SemaphoreType.DMA((n,))])(*parts)


def _block_rows(a, b):
    ta = a
    while ta * b > 256 * 1024 and ta % 32 == 0:
        ta //= 2
    return ta


def _reduce_adamw(parts, w, m, v, name):
    _, s, a, b = parts.shape
    ta = _block_rows(a, b)
    c1 = 1.0 - ADAM_B1 ** ADAM_STEP
    c2 = 1.0 - ADAM_B2 ** ADAM_STEP

    def body(p_ref, w_ref, m_ref, v_ref, g_out, d_out, m_out, v_out):
        g = p_ref[0].astype(F32)
        for j in range(1, N_DEV):
            g = g + p_ref[j].astype(F32)
        m_new = ADAM_B1 * m_ref[...] + (1.0 - ADAM_B1) * g
        v_new = ADAM_B2 * v_ref[...] + (1.0 - ADAM_B2) * (g * g)
        m_hat = m_new / c1
        v_hat = v_new / c2
        g_out[...] = g
        d_out[...] = -ADAM_LR * (m_hat / (jnp.sqrt(v_hat) + ADAM_EPS) + ADAM_WD * w_ref[...])
        m_out[...] = m_new
        v_out[...] = v_new

    spec = pl.BlockSpec((None, ta, b), lambda l, i: (l, i, 0))
    return _pc(body, name=name, out_shape=(_sds((s, a, b)),) * 4, grid=(s, a // ta),
               in_specs=[pl.BlockSpec((N_DEV, None, ta, b), lambda l, i: (0, l, i, 0)), spec, spec, spec],
               out_specs=(spec,) * 4, compiler_params=_params(2))(parts, w, m, v)


MATRICES = (("ada_mix_w", 2), ("w_in", 2), ("w_ssd_out", 1), ("w_conf_out", 2), ("w_sc_out", 2), ("w_o", 1),
            ("ada_ffn_w", 2), ("w_up", 2), ("w_down", 1))
CONV_WEIGHTS = (("ssd_conv_w", 2), ("conf_conv_w", 2), ("sc_conv_w", 2), ("ffn_conv_w", 2))
SHARDED = MATRICES + CONV_WEIGHTS
REPLICATED = ("ada_mix_b", "norm_mix_g", "b_gate", "ssd_conv_b", "ssd_dt_bias", "ssd_a_log", "ssd_d", "ssd_norm_g",
              "conf_conv_b", "conf_ln_g", "conf_ln_b", "ada_ffn_b", "norm_ffn_g", "ffn_conv_b", "final_norm_g")
WEIGHT_NAMES = ("ada_mix_w", "ada_mix_b", "norm_mix_g", "w_in", "b_gate", "ssd_conv_w", "ssd_conv_b", "ssd_dt_bias",
                "ssd_a_log", "ssd_d", "ssd_norm_g", "w_ssd_out", "conf_conv_w", "conf_conv_b", "conf_ln_g",
                "conf_ln_b", "w_conf_out", "sc_conv_w", "w_sc_out", "w_o", "ada_ffn_w", "ada_ffn_b", "norm_ffn_g",
                "w_up", "ffn_conv_w", "ffn_conv_b", "w_down", "final_norm_g")
GRAD_ROWS = 96


def _pack_flat(arrays, cols, row_multiple, dtype):
    flat = jnp.concatenate([a.reshape(-1).astype(dtype) for a in arrays])
    rows = -(-flat.shape[0] // cols)
    rows = -(-rows // row_multiple) * row_multiple
    return jnp.pad(flat, (0, rows * cols - flat.shape[0])).reshape(rows, cols)


def _unpack_flat(flat2d, shapes):
    flat = flat2d.reshape(-1)
    out, off = [], 0
    for s in shapes:
        n = 1
        for d in s:
            n *= d
        out.append(flat[off:off + n].reshape(s))
        off += n
    return out


def _cols(g, i, lo, hi):
    b = g.shape[-1]
    pieces = []
    for k in range(N_DEV):
        a, e = max(lo, k * b), min(hi, (k + 1) * b)
        if a < e:
            pieces.append(g[k, i, :, a - k * b:e - k * b])
    return pieces[0] if len(pieces) == 1 else jnp.concatenate(pieces, axis=1)


def _rows(g, i):
    return g[:, i].reshape(N_DEV * g.shape[2], g.shape[3])


def _col_shards(segs, b):
    shards = []
    for k in range(N_DEV):
        lo, hi = k * b, (k + 1) * b
        pieces, off = [], 0
        for seg in segs:
            n = seg.shape[1]
            a, e = max(lo, off), min(hi, off + n)
            if a < e:
                pieces.append(seg[:, a - off:e - off])
            off += n
        shards.append(pieces[0] if len(pieces) == 1 else jnp.concatenate(pieces, axis=1))
    return jnp.stack(shards)


def _row_shards(full):
    return full.reshape(N_DEV, full.shape[0] // N_DEV, full.shape[1])


def _pad_rows(a, rows):
    return jnp.pad(a, ((0, rows - a.shape[0]), (0, 0)))


def _pad_lanes(a):
    return jnp.pad(a, ((0, 0), (0, LANES - a.shape[1])))


def _layer_weights(full, i):
    row = lambda name: full[name][i].reshape(1, -1)
    whole = lambda name: _cols(full[name], i, 0, N_DEV * full[name].shape[-1])
    w_in = full["w_in"]
    wl = {
        "ada_mix_w": whole("ada_mix_w"), "ada_mix_b": row("ada_mix_b"), "norm_mix_g": row("norm_mix_g"),
        "w_z": _cols(w_in, i, 0, OFF_Z), "w_xbc": _cols(w_in, i, OFF_Z, OFF_XBC),
        "w_dt": _pad_lanes(_cols(w_in, i, OFF_XBC, OFF_DT)), "w_conf": _cols(w_in, i, OFF_DT, OFF_CONF),
        "w_sc": _cols(w_in, i, OFF_CONF, OFF_SC), "w_gates": _cols(w_in, i, OFF_SC, N_IN),
        "b_gate": row("b_gate"),
        "ssd_conv_w": _pad_rows(whole("ssd_conv_w"), SUBLANES), "ssd_conv_b": row("ssd_conv_b"),
        "dt_bias": _pad_lanes(row("ssd_dt_bias")), "a_log": _pad_lanes(row("ssd_a_log")),
        "ssd_d": _pad_lanes(row("ssd_d")), "ssd_norm_g": row("ssd_norm_g"), "w_ssd_out": _rows(full["w_ssd_out"], i),
        "conf_conv_w": _pad_rows(whole("conf_conv_w"), CONF_HALO), "conf_conv_b": row("conf_conv_b"),
        "conf_ln_g": row("conf_ln_g"), "conf_ln_b": row("conf_ln_b"), "w_conf_out": whole("w_conf_out"),
        "sc_conv_w": _pad_rows(whole("sc_conv_w"), SUBLANES), "w_sc_out": whole("w_sc_out"),
        "w_o": _rows(full["w_o"], i),
        "ada_ffn_w": whole("ada_ffn_w"), "ada_ffn_b": row("ada_ffn_b"), "norm_ffn_g": row("norm_ffn_g"),
        "w_up": whole("w_up"), "ffn_conv_w": _pad_rows(whole("ffn_conv_w"), SUBLANES),
        "ffn_conv_b": row("ffn_conv_b"), "w_down": _rows(full["w_down"], i),
    }
    return wl


def _adaln(sc8, w, b, name):
    mod = _matmul(sc8, w, "nn", F32, name)[0:1, :] + b
    return mod[:, :D_MODEL], mod[:, D_MODEL:2 * D_MODEL], mod[:, 2 * D_MODEL:]


def _layer_fwd(i, x, prev, sc8, wl):
    t = f"l{i}_"
    s = {}
    shift, scale, gate = _adaln(sc8, wl["ada_mix_w"], wl["ada_mix_b"], t + "ada_mix")
    if prev is None:
        s["x_in"] = x
        s["h"] = _prenorm_first(x, wl["norm_mix_g"], scale, shift, t + "norm_mix")
    else:
        s["x_in"], s["h"] = _prenorm_res(x, prev[0], prev[1], wl["norm_mix_g"], scale, shift, t + "norm_mix")
    s["scale_mix"], s["gate_mix"] = scale, gate
    h = s["h"]
    s["z"] = _matmul(h, wl["w_z"], "nn", F32, t + "in_z")
    s["xbc"] = _matmul(h, wl["w_xbc"], "nn", F32, t + "in_xbc")
    s["dt_raw"] = _matmul(h, wl["w_dt"], "nn", F32, t + "in_dt")
    s["conf"] = _matmul(h, wl["w_conf"], "nn", F32, t + "in_conf")
    s["sc"] = _matmul(h, wl["w_sc"], "nn", F32, t + "in_sc")
    s["gates"] = _matmul(h, wl["w_gates"], "nn", F32, t + "in_gates")
    s["pre"], s["dt"] = _ssd_pre(s["xbc"], s["dt_raw"], wl["ssd_conv_w"], wl["ssd_conv_b"], wl["dt_bias"],
                                 t + "ssd_pre")
    s["y"], s["hprev"] = _ssd_scan(s["pre"], s["dt"], wl["a_log"], wl["ssd_d"], t + "ssd_scan")
    s["ya_in"] = _ssd_post(s["y"], s["z"], wl["ssd_norm_g"], t + "ssd_post")
    s["yb_in"], s["uc"] = _conf_fwd(s["conf"], wl["conf_conv_w"], wl["conf_conv_b"], wl["conf_ln_g"],
                                    wl["conf_ln_b"], t + "conf")
    s["yc_in"] = _sc_fwd(s["sc"], wl["sc_conv_w"], t + "sconv")
    s["ya"] = _matmul(s["ya_in"], wl["w_ssd_out"], "nn", F32, t + "ssd_out")
    s["yb"] = _matmul(s["yb_in"], wl["w_conf_out"], "nn", F32, t + "conf_out")
    s["yc"] = _matmul(s["yc_in"], wl["w_sc_out"], "nn", F32, t + "sc_out")
    s["merged"] = _merge_fwd(s["gates"], s["ya"], s["yb"], s["yc"], wl["b_gate"], t + "merge")
    s["mix"] = _matmul(s["merged"], wl["w_o"], "nn", F32, t + "w_o")
    shift2, scale2, gate2 = _adaln(sc8, wl["ada_ffn_w"], wl["ada_ffn_b"], t + "ada_ffn")
    s["x_mid"], s["h2"] = _prenorm_res(s["x_in"], s["mix"], gate, wl["norm_ffn_g"], scale2, shift2, t + "norm_ffn")
    s["scale_ffn"], s["gate_ffn"] = scale2, gate2
    s["up"] = _matmul(s["h2"], wl["w_up"], "nn", F32, t + "w_up")
    s["a"] = _ffn_mid(s["up"], wl["ffn_conv_w"], wl["ffn_conv_b"], t + "ffn_mid")
    s["out"] = _matmul(s["a"], wl["w_down"], "nn", F32, t + "w_down")
    return s


def _layer_bwd(i, s, wl, sc8, dys_ffn, dx_after, dgate_ffn, prev):
    t = f"l{i}_b_"
    g = {}
    da = _matmul(dys_ffn, wl["w_down"], "nt", F32, t + "d_a")
    g["w_down"] = _matmul(s["a"], dys_ffn, "tn", BF16, t + "dw_down")
    dug, duv, dfw, g["ffn_conv_b"] = _ffn_mid_bwd(da, s["up"], wl["ffn_conv_w"], wl["ffn_conv_b"], t + "ffn_mid")
    g["ffn_conv_w"] = dfw[:3]
    dh2 = _matmul(dug, wl["w_up"][:, :D_FF], "nt", F32, t + "d_h2_g")
    dh2 = _matmul(duv, wl["w_up"][:, D_FF:], "nt", F32, t + "d_h2_v", add=dh2)
    g["w_up"] = [_matmul(s["h2"], dug, "tn", BF16, t + "dw_up_g"), _matmul(s["h2"], duv, "tn", BF16, t + "dw_up_v")]
    dx_mid, dshift2, dscale2, g["norm_ffn_g"], dys_mix, dgate_mix = _norm_bwd(
        dh2, s["x_mid"], dx_after, wl["norm_ffn_g"], s["scale_ffn"], t + "norm_ffn", s["mix"], s["gate_mix"])
    dmod_ffn = jnp.concatenate([dshift2, dscale2, dgate_ffn], axis=1)
    g["ada_ffn_b"] = dmod_ffn
    g["ada_ffn_w"] = [_matmul(sc8, _pad_rows(dmod_ffn, SUBLANES), "tn", BF16, t + "dw_ada_ffn")]
    dmerged = _matmul(dys_mix, wl["w_o"], "nt", F32, t + "d_merged")
    g["w_o"] = _matmul(s["merged"], dys_mix, "tn", BF16, t + "dw_o")
    dya, dyb, dyc, dgates, g["b_gate"] = _merge_bwd(dmerged, s["gates"], s["ya"], s["yb"], s["yc"], wl["b_gate"],
                                                    t + "merge")
    dya_in = _matmul(dya, wl["w_ssd_out"], "nt", F32, t + "d_ya_in")
    g["w_ssd_out"] = _matmul(s["ya_in"], dya, "tn", BF16, t + "dw_ssd_out")
    dyb_in = _matmul(dyb, wl["w_conf_out"], "nt", F32, t + "d_yb_in")
    g["w_conf_out"] = [_matmul(s["yb_in"], dyb, "tn", BF16, t + "dw_conf_out")]
    dyc_in = _matmul(dyc, wl["w_sc_out"], "nt", F32, t + "d_yc_in")
    g["w_sc_out"] = [_matmul(s["yc_in"], dyc, "tn", BF16, t + "dw_sc_out")]
    dy, dz, g["ssd_norm_g"] = _ssd_post_bwd(dya_in, s["y"], s["z"], wl["ssd_norm_g"], t + "ssd_post")
    dpre, ddt, da_log, dd = _ssd_scan_bwd(s["pre"], s["dt"], s["hprev"], dy, wl["a_log"], wl["ssd_d"],
                                          t + "ssd_scan")
    g["ssd_a_log"], g["ssd_d"] = da_log[:, :SSD_HEADS], dd[:, :SSD_HEADS]
    dxbc, ddt_raw, dcw, g["ssd_conv_b"], ddtb = _ssd_pre_bwd(dpre, s["xbc"], ddt, s["dt_raw"], wl["ssd_conv_w"],
                                                             wl["dt_bias"], t + "ssd_pre")
    g["ssd_conv_w"], g["ssd_dt_bias"] = dcw[:4], ddtb[:, :SSD_HEADS]
    dconf, dccw, g["conf_conv_b"], g["conf_ln_g"], g["conf_ln_b"] = _conf_bwd(
        dyb_in, s["uc"], s["conf"], wl["conf_conv_w"], wl["conf_ln_g"], wl["conf_ln_b"], t + "conf")
    g["conf_conv_w"] = dccw[:CONF_KERNEL]
    dsc, dscw = _sc_bwd(dyc_in, s["sc"], wl["sc_conv_w"], t + "sconv")
    g["sc_conv_w"] = dscw[:3]
    segs = (("z", dz, "w_z"), ("xbc", dxbc, "w_xbc"), ("dt", ddt_raw, "w_dt"), ("conf", dconf, "w_conf"),
            ("sc", dsc, "w_sc"), ("gates", dgates, "w_gates"))
    dh, dw_segs = None, []
    for nm, dseg, wname in segs:
        dh = _matmul(dseg, wl[wname], "nt", F32, t + "d_h_" + nm, add=dh)
        dw = _matmul(s["h"], dseg, "tn", BF16, t + "dw_in_" + nm)
        dw_segs.append(dw[:, :SSD_HEADS] if nm == "dt" else dw)
    g["w_in"] = dw_segs
    if prev is None:
        dx_in, dshift, dscale, g["norm_mix_g"] = _norm_bwd(dh, s["x_in"], dx_mid, wl["norm_mix_g"], s["scale_mix"],
                                                          t + "norm_mix")
        back = None
    else:
        dx_in, dshift, dscale, g["norm_mix_g"], dys_prev, dgate_prev = _norm_bwd(
            dh, s["x_in"], dx_mid, wl["norm_mix_g"], s["scale_mix"], t + "norm_mix", prev[0], prev[1])
        back = (dys_prev, dgate_prev)
    dmod_mix = jnp.concatenate([dshift, dscale, dgate_mix], axis=1)
    g["ada_mix_b"] = dmod_mix
    g["ada_mix_w"] = [_matmul(sc8, _pad_rows(dmod_mix, SUBLANES), "tn", BF16, t + "dw_ada_mix")]
    return g, dx_in, back


def _device_step(x, c, target, full):
    sc8 = _pad_rows(c * (1.0 / (1.0 + jnp.exp(-c))), SUBLANES)
    wls = [_layer_weights(full, i) for i in range(DEPTH)]
    saved, prev, xcur = [], None, x
    for i in range(DEPTH):
        s = _layer_fwd(i, xcur, prev, sc8, wls[i])
        saved.append(s)
        xcur, prev = s["x_mid"], (s["out"], s["gate_ffn"])
    gf = full["final_norm_g"].reshape(1, -1)
    last = saved[-1]
    loss, dx, dys, dgate, dgf = _final_loss(last["x_mid"], last["out"], last["gate_ffn"], gf, target, "final_loss")
    grads = [None] * DEPTH
    for i in reversed(range(DEPTH)):
        prev = None if i == 0 else (saved[i - 1]["out"], saved[i - 1]["gate_ffn"])
        grads[i], dx, back = _layer_bwd(i, saved[i], wls[i], sc8, dys, dx, dgate, prev)
        if back is not None:
            dys, dgate = back
    return loss[0, 0], dx, grads, dgf


def _step(x, c, target, weights, moments_m, moments_v):
    mat_names = [n for n, _ in MATRICES]
    sharded_names = [n for n, _ in SHARDED]
    local = [weights[n].astype(BF16) if n in mat_names else weights[n] for n in sharded_names]
    full = {n: weights[n] for n in REPLICATED}
    full.update(zip(sharded_names, _all_gather(local, "gather_weights")))
    loss, grad_x, grads, dgf = _device_step(x[0], c, target[0], full)
    parts = []
    for name, axis in SHARDED:
        per_layer = []
        for i in range(DEPTH):
            gw = grads[i][name]
            if axis == 1:
                per_layer.append(_row_shards(gw))
            else:
                per_layer.append(_col_shards(gw if isinstance(gw, list) else [gw], weights[name].shape[-1]))
        parts.append(jnp.stack(per_layer, axis=1).astype(BF16))
    received = _exchange(parts, "exchange_grads")
    big = {n: _reduce_adamw(r, weights[n], moments_m[n], moments_v[n], "adamw_" + n)
           for n, r in zip(sharded_names, received)}
    rep_grads = [dgf if n == "final_norm_g" else jnp.stack([grads[i][n].reshape(-1) for i in range(DEPTH)])
                 for n in REPLICATED]
    small_parts, = _all_gather([_pack_flat(rep_grads, LANES, SUBLANES, F32)], "gather_small_grads")
    pack_s = lambda d: _pack_flat([d[n] for n in REPLICATED], LANES, SUBLANES, F32)[None]
    small = _reduce_adamw(small_parts[:, None], pack_s(weights), pack_s(moments_m), pack_s(moments_v),
                          "adamw_replicated")
    small = [_unpack_flat(b, [weights[n].shape for n in REPLICATED]) for b in small]
    results = []
    for kind in range(4):
        by_name = {n: big[n][kind] for n in sharded_names}
        by_name.update(zip(REPLICATED, small[kind]))
        results.append([by_name[n] for n in WEIGHT_NAMES])
    loss = lax.psum(loss, ("x", "y", "c"))
    return (loss, grad_x[None], *results[0], *results[1], *results[2], *results[3])


def kernel(x, c, ada_mix_w, ada_mix_b, norm_mix_g, w_in, b_gate, ssd_conv_w, ssd_conv_b, ssd_dt_bias, ssd_a_log, ssd_d, ssd_norm_g, w_ssd_out, conf_conv_w, conf_conv_b, conf_ln_g, conf_ln_b, w_conf_out, sc_conv_w, w_sc_out, w_o, ada_ffn_w, ada_ffn_b, norm_ffn_g, w_up, ffn_conv_w, ffn_conv_b, w_down, final_norm_g, loss_target, m_ada_mix_w, m_ada_mix_b, m_norm_mix_g, m_w_in, m_b_gate, m_ssd_conv_w, m_ssd_conv_b, m_ssd_dt_bias, m_ssd_a_log, m_ssd_d, m_ssd_norm_g, m_w_ssd_out, m_conf_conv_w, m_conf_conv_b, m_conf_ln_g, m_conf_ln_b, m_w_conf_out, m_sc_conv_w, m_w_sc_out, m_w_o, m_ada_ffn_w, m_ada_ffn_b, m_norm_ffn_g, m_w_up, m_ffn_conv_w, m_ffn_conv_b, m_w_down, m_final_norm_g, v_ada_mix_w, v_ada_mix_b, v_norm_mix_g, v_w_in, v_b_gate, v_ssd_conv_w, v_ssd_conv_b, v_ssd_dt_bias, v_ssd_a_log, v_ssd_d, v_ssd_norm_g, v_w_ssd_out, v_conf_conv_w, v_conf_conv_b, v_conf_ln_g, v_conf_ln_b, v_w_conf_out, v_sc_conv_w, v_w_sc_out, v_w_o, v_ada_ffn_w, v_ada_ffn_b, v_norm_ffn_g, v_w_up, v_ffn_conv_w, v_ffn_conv_b, v_w_down, v_final_norm_g):
    given = dict(locals())
    weights = {n: given[n] for n in WEIGHT_NAMES}
    moments_m = {n: given["m_" + n] for n in WEIGHT_NAMES}
    moments_v = {n: given["v_" + n] for n in WEIGHT_NAMES}
    return _step(x, c, loss_target, weights, moments_m, moments_v)
```

```python
import functools

import jax
import jax.numpy as jnp
from jax import lax
from jax.experimental import pallas as pl
from jax.experimental.pallas import tpu as pltpu

F32 = jnp.float32
BF16 = jnp.bfloat16
HI = lax.Precision.HIGHEST
MESH = pl.DeviceIdType.MESH

N_DEV = 8
DEPTH = 2
D_MODEL = 1024
SSD_HEADS = 16
SSD_HEAD_DIM = 64
SSD_INNER = 1024
SSD_STATE = 64
SSD_CHUNK = 128
SSD_XBC = 1280
CONF_WIDTH = 512
CONF_KERNEL = 31
SC_WIDTH = 512
D_FF = 2816
EPS = 1e-6
OFF_Z, OFF_XBC, OFF_DT, OFF_CONF, OFF_SC, N_IN = 1024, 2304, 2320, 3344, 4880, 7952

ADAM_LR, ADAM_B1, ADAM_B2, ADAM_EPS, ADAM_WD, ADAM_STEP = 0.001, 0.9, 0.999, 1e-08, 0.01, 10

LANES = 128
SUBLANES = 8
VMEM_LIMIT = 56 * 1024 * 1024
ROW_TILE = 256

NN = (((1,), (0,)), ((), ()))
NT = (((1,), (1,)), ((), ()))
TN = (((0,), (0,)), ((), ()))


def _params(n_axes):
    return pltpu.CompilerParams(dimension_semantics=("arbitrary",) * n_axes, vmem_limit_bytes=VMEM_LIMIT)


def _pc(body, **kw):
    return pl.pallas_call(body, **kw)


def _dot(a, b, dn=NN, precision=None):
    return lax.dot_general(a, b, dn, precision=precision, preferred_element_type=F32)


def _sig(x):
    return 1.0 / (1.0 + jnp.exp(-x))


def _fold(v):
    r, c = v.shape
    return v.reshape(r // SUBLANES, SUBLANES, c).sum(axis=0)


def _tile(n_rows):
    return min(ROW_TILE, n_rows // 2)


def _row(tl, c, col=0):
    return pl.BlockSpec((tl, c), lambda i, col=col: (i, col))


def _prev(tl, hb, c, col=0):
    r = tl // hb
    return pl.BlockSpec((hb, c), lambda i, col=col: (jnp.maximum(i * r - 1, 0), col))


def _next(tl, hb, c, n_rows, col=0):
    r = tl // hb
    last = n_rows // hb - 1
    return pl.BlockSpec((hb, c), lambda i, col=col: (jnp.minimum((i + 1) * r, last), col))


def _const(shape):
    return pl.BlockSpec(shape, lambda i: (0,) * len(shape))


def _sds(shape, dtype=F32):
    return jax.ShapeDtypeStruct(shape, dtype)


def _pick(dim, target):
    if dim <= target:
        return dim
    best = None
    for t in range(LANES, target + 1, LANES):
        if dim % t == 0:
            best = t
    assert best is not None, (dim, target)
    return best


def _matmul(a, b, mode, out_dtype, name, add=None):
    if mode == "nn":
        (m, k), (k2, n) = a.shape, b.shape
    elif mode == "nt":
        (m, k), (n, k2) = a.shape, b.shape
    else:
        (k, m), (k2, n) = a.shape, b.shape
    assert k == k2, (a.shape, b.shape, mode)
    tm, tn, tk = _pick(m, 1024), _pick(n, 1536), _pick(k, 1024)
    nk = k // tk
    dn = {"nn": NN, "nt": NT, "tn": TN}[mode]

    def body(*refs):
        if add is None:
            a_ref, b_ref, o_ref, acc = refs
        else:
            a_ref, b_ref, c_ref, o_ref, acc = refs
        kk = pl.program_id(2)

        @pl.when(kk == 0)
        def _():
            acc[...] = jnp.zeros_like(acc)

        acc[...] += _dot(a_ref[...].astype(BF16), b_ref[...].astype(BF16), dn)

        @pl.when(kk == nk - 1)
        def _():
            r = acc[...]
            if add is not None:
                r = r + c_ref[...]
            o_ref[...] = r.astype(out_dtype)

    a_spec = {"nn": pl.BlockSpec((tm, tk), lambda i, j, kk: (i, kk)),
              "nt": pl.BlockSpec((tm, tk), lambda i, j, kk: (i, kk)),
              "tn": pl.BlockSpec((tk, tm), lambda i, j, kk: (kk, i))}[mode]
    b_spec = {"nn": pl.BlockSpec((tk, tn), lambda i, j, kk: (kk, j)),
              "nt": pl.BlockSpec((tn, tk), lambda i, j, kk: (j, kk)),
              "tn": pl.BlockSpec((tk, tn), lambda i, j, kk: (kk, j))}[mode]
    o_spec = pl.BlockSpec((tm, tn), lambda i, j, kk: (i, j))
    in_specs = [a_spec, b_spec] + ([o_spec] if add is not None else [])
    args = (a, b) + ((add,) if add is not None else ())
    return _pc(body, name=name, out_shape=_sds((m, n), out_dtype), grid=(m // tm, n // tn, nk),
               in_specs=in_specs, out_specs=o_spec, scratch_shapes=[pltpu.VMEM((tm, tn), F32)],
               compiler_params=_params(3))(*args)


def _norm_mod(x, g, scale, shift):
    r = lax.rsqrt(jnp.mean(x * x, axis=-1, keepdims=True) + EPS)
    return ((x * r) * g) * (1.0 + scale) + shift


def _prenorm_first(x, g, scale, shift, name):
    n, d = x.shape
    tl = _tile(n)

    def body(x_ref, g_ref, sc_ref, sh_ref, h_ref):
        h_ref[...] = _norm_mod(x_ref[...], g_ref[...], sc_ref[...], sh_ref[...]).astype(BF16)

    return _pc(body, name=name, out_shape=_sds((n, d), BF16), grid=(n // tl,),
               in_specs=[_row(tl, d)] + [_const((1, d))] * 3, out_specs=_row(tl, d),
               compiler_params=_params(1))(x, g, scale, shift)


def _prenorm_res(x, y, gate, g, scale, shift, name):
    n, d = x.shape
    tl = _tile(n)

    def body(x_ref, y_ref, gate_ref, g_ref, sc_ref, sh_ref, xo_ref, h_ref):
        xn = x_ref[...] + gate_ref[...] * y_ref[...]
        xo_ref[...] = xn
        h_ref[...] = _norm_mod(xn, g_ref[...], sc_ref[...], sh_ref[...]).astype(BF16)

    return _pc(body, name=name, out_shape=(_sds((n, d)), _sds((n, d), BF16)), grid=(n // tl,),
               in_specs=[_row(tl, d), _row(tl, d)] + [_const((1, d))] * 4,
               out_specs=(_row(tl, d), _row(tl, d)), compiler_params=_params(1))(x, y, gate, g, scale, shift)


def _final_loss(x, y, gate, gf, target, name):
    n, d = x.shape
    tl = _tile(n)
    nb = n // tl

    def body(x_ref, y_ref, gate_ref, gf_ref, t_ref, loss_ref, dx_ref, dys_ref, dgate_ref, dgf_ref,
             acc_l, acc_gate, acc_gf):
        i = pl.program_id(0)

        @pl.when(i == 0)
        def _():
            acc_l[...] = jnp.zeros_like(acc_l)
            acc_gate[...] = jnp.zeros_like(acc_gate)
            acc_gf[...] = jnp.zeros_like(acc_gf)

        yv = y_ref[...]
        gate = gate_ref[...]
        gf = gf_ref[...]
        x2 = x_ref[...] + gate * yv
        r = lax.rsqrt(jnp.mean(x2 * x2, axis=-1, keepdims=True) + EPS)
        xn = x2 * r
        e = xn * gf - t_ref[...]
        acc_l[...] += _fold(e * e)
        dy = e * (1.0 / d)
        acc_gf[...] += _fold(dy * xn)
        dxn = dy * gf
        dx = r * (dxn - xn * jnp.mean(dxn * xn, axis=-1, keepdims=True))
        dx_ref[...] = dx
        dys_ref[...] = (dx * gate).astype(BF16)
        acc_gate[...] += _fold(dx * yv)

        @pl.when(i == nb - 1)
        def _():
            loss_ref[...] = jnp.full((SUBLANES, LANES), 0.5 / d, F32) * jnp.sum(acc_l[...])
            dgate_ref[...] = jnp.sum(acc_gate[...], axis=0, keepdims=True)
            dgf_ref[...] = jnp.sum(acc_gf[...], axis=0, keepdims=True)

    return _pc(body, name=name,
               out_shape=(_sds((SUBLANES, LANES)), _sds((n, d)), _sds((n, d), BF16), _sds((1, d)), _sds((1, d))),
               grid=(nb,),
               in_specs=[_row(tl, d), _row(tl, d), _const((1, d)), _const((1, d)), _row(tl, d)],
               out_specs=(_const((SUBLANES, LANES)), _row(tl, d), _row(tl, d), _const((1, d)), _const((1, d))),
               scratch_shapes=[pltpu.VMEM((SUBLANES, d), F32)] * 3,
               compiler_params=_params(1))(x, y, gate, gf, target)


def _norm_bwd(dh, x, dxo, g, scale, name, y_prev=None, gate_prev=None):
    n, d = x.shape
    tl = _tile(n)
    nb = n // tl
    has_prev = y_prev is not None

    def body(*refs):
        if has_prev:
            (dh_ref, x_ref, dxo_ref, g_ref, sc_ref, yp_ref, gp_ref,
             dx_ref, dsh_ref, dsc_ref, dg_ref, dys_ref, dgp_ref, acc_sh, acc_s, acc_gp) = refs
        else:
            (dh_ref, x_ref, dxo_ref, g_ref, sc_ref,
             dx_ref, dsh_ref, dsc_ref, dg_ref, acc_sh, acc_s) = refs
        i = pl.program_id(0)

        @pl.when(i == 0)
        def _():
            acc_sh[...] = jnp.zeros_like(acc_sh)
            acc_s[...] = jnp.zeros_like(acc_s)
            if has_prev:
                acc_gp[...] = jnp.zeros_like(acc_gp)

        x_ = x_ref[...]
        dh_ = dh_ref[...]
        g_ = g_ref[...]
        one_sc = 1.0 + sc_ref[...]
        r = lax.rsqrt(jnp.mean(x_ * x_, axis=-1, keepdims=True) + EPS)
        xn = x_ * r
        dxn = dh_ * (g_ * one_sc)
        dx = dxo_ref[...] + r * (dxn - xn * jnp.mean(dxn * xn, axis=-1, keepdims=True))
        dx_ref[...] = dx
        acc_sh[...] += _fold(dh_)
        acc_s[...] += _fold(dh_ * xn)
        if has_prev:
            dys_ref[...] = (dx * gp_ref[...]).astype(BF16)
            acc_gp[...] += _fold(dx * yp_ref[...])

        @pl.when(i == nb - 1)
        def _():
            s = jnp.sum(acc_s[...], axis=0, keepdims=True)
            dsh_ref[...] = jnp.sum(acc_sh[...], axis=0, keepdims=True)
            dsc_ref[...] = s * g_
            dg_ref[...] = s * one_sc
            if has_prev:
                dgp_ref[...] = jnp.sum(acc_gp[...], axis=0, keepdims=True)

    vec = _sds((1, d))
    in_specs = [_row(tl, d)] * 3 + [_const((1, d))] * 2
    out_shape = [_sds((n, d)), vec, vec, vec]
    out_specs = [_row(tl, d)] + [_const((1, d))] * 3
    scratch = [pltpu.VMEM((SUBLANES, d), F32)] * 2
    args = [dh, x, dxo, g, scale]
    if has_prev:
        in_specs += [_row(tl, d), _const((1, d))]
        out_shape += [_sds((n, d), BF16), vec]
        out_specs += [_row(tl, d), _const((1, d))]
        scratch += [pltpu.VMEM((SUBLANES, d), F32)]
        args += [y_prev, gate_prev]
    return _pc(body, name=name, out_shape=tuple(out_shape), grid=(nb,), in_specs=in_specs,
               out_specs=tuple(out_specs), scratch_shapes=scratch, compiler_params=_params(1))(*args)


CONV_HALO = 8
CONF_HALO = 32


def _ssd_pre(xbc, dt_raw, conv_w, conv_b, dt_bias, name):
    n, c = xbc.shape
    tl = _tile(n)
    hb = CONV_HALO
    k_taps = 4

    def body(x_ref, xp_ref, dt_ref, w_ref, b_ref, dtb_ref, pre_ref, dts_ref, buf):
        i = pl.program_id(0)
        buf[pl.ds(0, hb), :] = jnp.where(i > 0, xp_ref[...], 0.0)
        buf[pl.ds(hb, tl), :] = x_ref[...]
        acc = b_ref[...] + w_ref[0:1, :] * buf[pl.ds(hb - 3, tl), :]
        for k in range(1, k_taps):
            acc = acc + w_ref[k:k + 1, :] * buf[pl.ds(hb - 3 + k, tl), :]
        pre_ref[...] = acc
        v = dt_ref[...] + dtb_ref[...]
        dts_ref[...] = jnp.maximum(v, 0.0) + jnp.log1p(jnp.exp(-jnp.abs(v)))

    return _pc(body, name=name, out_shape=(_sds((n, c)), _sds((n, LANES))), grid=(n // tl,),
               in_specs=[_row(tl, c), _prev(tl, hb, c), _row(tl, LANES), _const((SUBLANES, c)), _const((1, c)),
                         _const((1, LANES))],
               out_specs=(_row(tl, c), _row(tl, LANES)),
               scratch_shapes=[pltpu.VMEM((tl + hb, c), F32)], compiler_params=_params(1))(
        xbc, xbc, dt_raw, conv_w, conv_b, dt_bias)


def _ssd_pre_bwd(dpre, xbc, ddt, dt_raw, conv_w, dt_bias, name):
    n, c = xbc.shape
    tl = _tile(n)
    nb = n // tl
    hb = CONV_HALO
    k_taps = 4

    def body(dp_ref, dpn_ref, x_ref, xp_ref, ddt_ref, dt_ref, w_ref, dtb_ref,
             dx_ref, ddr_ref, dw_ref, db_ref, ddtb_ref, dbuf, xbuf, acc_w, acc_b, acc_dtb):
        i = pl.program_id(0)

        @pl.when(i == 0)
        def _():
            acc_w[...] = jnp.zeros_like(acc_w)
            acc_b[...] = jnp.zeros_like(acc_b)
            acc_dtb[...] = jnp.zeros_like(acc_dtb)

        dp = dp_ref[...]
        dbuf[pl.ds(0, tl), :] = dp
        dbuf[pl.ds(tl, hb), :] = jnp.where(i < nb - 1, dpn_ref[...], 0.0)
        xbuf[pl.ds(0, hb), :] = jnp.where(i > 0, xp_ref[...], 0.0)
        xbuf[pl.ds(hb, tl), :] = x_ref[...]
        dx = w_ref[0:1, :] * dbuf[pl.ds(3, tl), :]
        for k in range(1, k_taps):
            dx = dx + w_ref[k:k + 1, :] * dbuf[pl.ds(3 - k, tl), :]
        dx_ref[...] = dx.astype(BF16)
        for k in range(k_taps):
            acc_w[k] += _fold(dp * xbuf[pl.ds(hb - 3 + k, tl), :])
        acc_b[...] += _fold(dp)
        ddr = ddt_ref[...] * _sig(dt_ref[...] + dtb_ref[...])
        ddr_ref[...] = ddr.astype(BF16)
        acc_dtb[...] += _fold(ddr)

        @pl.when(i == nb - 1)
        def _():
            dw_ref[...] = jnp.zeros_like(dw_ref)
            for k in range(k_taps):
                dw_ref[k:k + 1, :] = jnp.sum(acc_w[k], axis=0, keepdims=True)
            db_ref[...] = jnp.sum(acc_b[...], axis=0, keepdims=True)
            ddtb_ref[...] = jnp.sum(acc_dtb[...], axis=0, keepdims=True)

    return _pc(body, name=name,
               out_shape=(_sds((n, c), BF16), _sds((n, LANES), BF16), _sds((SUBLANES, c)), _sds((1, c)),
                          _sds((1, LANES))),
               grid=(nb,),
               in_specs=[_row(tl, c), _next(tl, hb, c, n), _row(tl, c), _prev(tl, hb, c), _row(tl, LANES),
                         _row(tl, LANES), _const((SUBLANES, c)), _const((1, LANES))],
               out_specs=(_row(tl, c), _row(tl, LANES), _const((SUBLANES, c)), _const((1, c)), _const((1, LANES))),
               scratch_shapes=[pltpu.VMEM((tl + hb, c), F32), pltpu.VMEM((tl + hb, c), F32),
                               pltpu.VMEM((k_taps, SUBLANES, c), F32), pltpu.VMEM((SUBLANES, c), F32),
                               pltpu.VMEM((SUBLANES, LANES), F32)],
               compiler_params=_params(1))(dpre, dpre, xbc, xbc, ddt, dt_raw, conv_w, dt_bias)


def _expand_mat():
    r = lax.broadcasted_iota(jnp.int32, (LANES, SSD_INNER), 0)
    c = lax.broadcasted_iota(jnp.int32, (LANES, SSD_INNER), 1)
    return (jnp.right_shift(c, 6) == r).astype(F32)


def _reduce_mat():
    r = lax.broadcasted_iota(jnp.int32, (SSD_INNER, LANES), 0)
    c = lax.broadcasted_iota(jnp.int32, (SSD_INNER, LANES), 1)
    return (jnp.right_shift(r, 6) == c).astype(F32)


def _ssd_common(pre, dt, alog):
    q = SSD_CHUNK
    sg = _sig(pre)
    act = pre * sg
    lane = lax.broadcasted_iota(jnp.int32, (1, LANES), 1)
    a_neg = jnp.where(lane < SSD_HEADS, -jnp.exp(alog), 0.0)
    rr = lax.broadcasted_iota(jnp.int32, (q, q), 0)
    cc = lax.broadcasted_iota(jnp.int32, (q, q), 1)
    causal = rr >= cc
    cum = _dot(causal.astype(F32), dt * a_neg, precision=HI)
    e_mat = _expand_mat()
    dtx = _dot(dt, e_mat, precision=HI)
    cumx = _dot(cum, e_mat, precision=HI)
    return sg, act, a_neg, causal, cum, e_mat, dtx, cumx


def _ssd_scan(pre, dt, alog, dvec, name):
    n = pre.shape[0]
    q = SSD_CHUNK
    nc = n // q

    def body(pre_ref, dt_ref, alog_ref, d_ref, y_ref, hp_ref, state):
        i = pl.program_id(0)

        @pl.when(i == 0)
        def _():
            state[...] = jnp.zeros_like(state)

        dt_ = dt_ref[...]
        _, act, _, causal, cum, e_mat, dtx, cumx = _ssd_common(pre_ref[...], dt_, alog_ref[...])
        xs = act[:, :SSD_INNER]
        bm = act[:, SSD_INNER:SSD_INNER + LANES]
        cm = act[:, SSD_INNER + LANES:]
        cum_t = cum.T
        clx = cumx[q - 1:q, :]
        xc = xs * dtx
        xd = xc * jnp.exp(clx - cumx)
        doutx = jnp.exp(cumx)
        edec = jnp.exp(clx)
        dx_row = _dot(jnp.broadcast_to(d_ref[...], (SUBLANES, LANES)), e_mat, precision=HI)[0:1, :]
        hp_ref[0] = state[...]
        bb = bm.astype(BF16)
        cb = cm.astype(BF16)
        lane = lax.broadcasted_iota(jnp.int32, (1, LANES), 1)
        row = lax.broadcasted_iota(jnp.int32, (LANES, 1), 0)
        cbs = []
        for g in range(2):
            cg = jnp.where(jnp.right_shift(lane, 6) == g, cm, 0.0).astype(BF16)
            cbs.append(_dot(cg, bb, NT))
        for j in range(SSD_HEADS // 2):
            sl = slice(j * LANES, (j + 1) * LANES)
            g = j // 4
            xcj = xc[:, sl].astype(BF16)
            halves = []
            for half in range(2):
                h = 2 * j + half
                seg = cum[:, h:h + 1] - cum_t[h:h + 1, :]
                w = cbs[g] * jnp.exp(jnp.where(causal, seg, -jnp.inf))
                halves.append(_dot(w.astype(BF16), xcj))
            y_diag = jnp.where(lane < SSD_HEAD_DIM, halves[0], halves[1])
            hj = state[:, sl]
            y_off = doutx[:, sl] * _dot(cb, hj.astype(BF16))
            y_ref[:, sl] = y_diag + y_off + xs[:, sl] * dx_row[:, sl]
            st = _dot(bb, xd[:, sl].astype(BF16), TN)
            state[:, sl] = hj * edec[:, sl] + jnp.where(jnp.right_shift(row, 6) == g, st, 0.0)

    return _pc(body, name=name, out_shape=(_sds((n, SSD_INNER)), _sds((nc, LANES, SSD_INNER))), grid=(nc,),
               in_specs=[_row(q, SSD_XBC), _row(q, LANES), _const((1, LANES)), _const((1, LANES))],
               out_specs=(_row(q, SSD_INNER), pl.BlockSpec((1, LANES, SSD_INNER), lambda i: (i, 0, 0))),
               scratch_shapes=[pltpu.VMEM((LANES, SSD_INNER), F32)], compiler_params=_params(1))(pre, dt, alog, dvec)


def _ssd_scan_bwd(pre, dt, hprev, dy, alog, dvec, name):
    n = pre.shape[0]
    q = SSD_CHUNK
    nc = n // q

    def body(pre_ref, dt_ref, hp_ref, dy_ref, alog_ref, d_ref, dpre_ref, ddt_ref, da_ref, dd_ref,
             d_state, dxc_s, dcx_s, dcl_s, acc_a, acc_d):
        i = pl.program_id(0)

        @pl.when(i == 0)
        def _():
            d_state[...] = jnp.zeros_like(d_state)
            acc_a[...] = jnp.zeros_like(acc_a)
            acc_d[...] = jnp.zeros_like(acc_d)

        pre_ = pre_ref[...]
        dt_ = dt_ref[...]
        sg, act, a_neg, causal, cum, e_mat, dtx, cumx = _ssd_common(pre_, dt_, alog_ref[...])
        r_mat = _reduce_mat()
        xs = act[:, :SSD_INNER]
        bm = act[:, SSD_INNER:SSD_INNER + LANES]
        cm = act[:, SSD_INNER + LANES:]
        cum_t = cum.T
        clx = cumx[q - 1:q, :]
        xc = xs * dtx
        dsx = jnp.exp(clx - cumx)
        doutx = jnp.exp(cumx)
        edec = jnp.exp(clx)
        dx_row = _dot(jnp.broadcast_to(d_ref[...], (SUBLANES, LANES)), e_mat, precision=HI)[0:1, :]
        dy_ = dy_ref[...]
        acc_d[...] += _fold(dy_ * xs)
        bb = bm.astype(BF16)
        cb = cm.astype(BF16)
        lane = lax.broadcasted_iota(jnp.int32, (1, LANES), 1)
        row = lax.broadcasted_iota(jnp.int32, (LANES, 1), 0)
        d_c = jnp.zeros((q, LANES), F32)
        d_b = jnp.zeros((q, LANES), F32)

        for j in range(SSD_HEADS // 2):
            sl = slice(j * LANES, (j + 1) * LANES)
            g = j // 4
            hj = hp_ref[0, :, sl]
            hjb = hj.astype(BF16)
            dyj = dy_[:, sl]
            tj = _dot(cb, hjb)
            dtj = (doutx[:, sl] * dyj).astype(BF16)
            dcx = dyj * tj * doutx[:, sl]
            d_c = d_c + _dot(dtj, hjb, NT)
            dhn = d_state[:, sl]
            dhp = dhn * edec[:, sl] + jnp.where(jnp.right_shift(row, 6) == g, _dot(cb, dtj, TN), 0.0)
            dcl = jnp.sum(dhn * hj, axis=0, keepdims=True) * edec[:, sl]
            dsb = dhn.astype(BF16)
            dxd = _dot(bb, dsb)
            xcj = xc[:, sl]
            dsj = dsx[:, sl]
            d_b = d_b + _dot((xcj * dsj).astype(BF16), dsb, NT)
            dds = dxd * xcj * dsj
            d_state[:, sl] = dhp
            dxc_s[:, sl] = dxd * dsj
            dcx_s[:, sl] = dcx - dds
            dcl_s[:, sl] = jnp.broadcast_to(dcl + jnp.sum(dds, axis=0, keepdims=True), (SUBLANES, LANES))

        dcum_c = jnp.zeros((q, LANES), F32)
        dcum_t = jnp.zeros((LANES, q), F32)
        for g in range(2):
            gmask = jnp.right_shift(lane, 6) == g
            cg = jnp.where(gmask, cm, 0.0).astype(BF16)
            cbg = _dot(cg, bb, NT)
            d_cb = jnp.zeros((q, q), F32)
            for hh in range(SSD_HEADS // 2):
                h = g * (SSD_HEADS // 2) + hh
                j, half = h // 2, h % 2
                sl = slice(j * LANES, (j + 1) * LANES)
                hmask = jnp.right_shift(lane, 6) == half
                seg = cum[:, h:h + 1] - cum_t[h:h + 1, :]
                lm = jnp.exp(jnp.where(causal, seg, -jnp.inf))
                w = cbg * lm
                dyj = dy_[:, sl]
                dw = _dot(jnp.where(hmask, dyj, 0.0).astype(BF16), xc[:, sl].astype(BF16), NT)
                dxch = _dot(w.astype(BF16), dyj.astype(BF16), TN)
                dxc_s[:, sl] += jnp.where(hmask, dxch, 0.0)
                d_cb = d_cb + dw * lm
                m = dw * w
                dcum_c = dcum_c + jnp.sum(m, axis=1, keepdims=True) * (lane == h).astype(F32)
                dcum_t = dcum_t + (row == h).astype(F32) * jnp.sum(m, axis=0, keepdims=True)
            d_cbb = d_cb.astype(BF16)
            d_c = d_c + jnp.where(gmask, _dot(d_cbb, bb), 0.0)
            d_b = d_b + jnp.where(gmask, _dot(d_cbb, cb, TN), 0.0)

        dcl_row = _dot(dcl_s[...], r_mat, precision=HI)[0:1, :]
        rowq = lax.broadcasted_iota(jnp.int32, (q, 1), 0)
        dcum = (dcum_c - dcum_t.T + _dot(dcx_s[...], r_mat, precision=HI)
                + jnp.where(rowq == q - 1, dcl_row, 0.0))
        rr = lax.broadcasted_iota(jnp.int32, (q, q), 0)
        cc = lax.broadcasted_iota(jnp.int32, (q, q), 1)
        dadt = _dot((rr <= cc).astype(F32), dcum, precision=HI)
        dxc = dxc_s[...]
        ddt_ref[...] = dadt * a_neg + _dot(dxc * xs, r_mat, precision=HI)
        acc_a[...] += _fold(dadt * dt_)
        dsilu = sg * (1.0 + pre_ * (1.0 - sg))
        dpre_ref[:, :SSD_INNER] = (dxc * dtx + dy_ * dx_row) * dsilu[:, :SSD_INNER]
        dpre_ref[:, SSD_INNER:SSD_INNER + LANES] = d_b * dsilu[:, SSD_INNER:SSD_INNER + LANES]
        dpre_ref[:, SSD_INNER + LANES:] = d_c * dsilu[:, SSD_INNER + LANES:]

        @pl.when(i == nc - 1)
        def _():
            da_ref[...] = jnp.sum(acc_a[...], axis=0, keepdims=True) * a_neg
            dd_ref[...] = jnp.sum(_dot(acc_d[...], r_mat, precision=HI), axis=0, keepdims=True)

    rev = lambda i: (nc - 1 - i, 0)
    return _pc(body, name=name,
               out_shape=(_sds((n, SSD_XBC)), _sds((n, LANES)), _sds((1, LANES)), _sds((1, LANES))), grid=(nc,),
               in_specs=[pl.BlockSpec((q, SSD_XBC), rev), pl.BlockSpec((q, LANES), rev),
                         pl.BlockSpec((1, LANES, SSD_INNER), lambda i: (nc - 1 - i, 0, 0)),
                         pl.BlockSpec((q, SSD_INNER), rev), _const((1, LANES)), _const((1, LANES))],
               out_specs=(pl.BlockSpec((q, SSD_XBC), rev), pl.BlockSpec((q, LANES), rev), _const((1, LANES)),
                          _const((1, LANES))),
               scratch_shapes=[pltpu.VMEM((LANES, SSD_INNER), F32), pltpu.VMEM((q, SSD_INNER), F32),
                               pltpu.VMEM((q, SSD_INNER), F32), pltpu.VMEM((SUBLANES, SSD_INNER), F32),
                               pltpu.VMEM((SUBLANES, LANES), F32), pltpu.VMEM((SUBLANES, SSD_INNER), F32)],
               compiler_params=_params(1))(pre, dt, hprev, dy, alog, dvec)


def _group_norm_parts(v):
    half = SSD_INNER // 2
    r0 = lax.rsqrt(jnp.mean(v[:, :half] * v[:, :half], axis=-1, keepdims=True) + EPS)
    r1 = lax.rsqrt(jnp.mean(v[:, half:] * v[:, half:], axis=-1, keepdims=True) + EPS)
    lane = lax.broadcasted_iota(jnp.int32, (1, SSD_INNER), 1)
    return jnp.where(lane < half, r0, r1)


def _group_mean(v):
    half = SSD_INNER // 2
    m0 = jnp.mean(v[:, :half], axis=-1, keepdims=True)
    m1 = jnp.mean(v[:, half:], axis=-1, keepdims=True)
    lane = lax.broadcasted_iota(jnp.int32, (1, SSD_INNER), 1)
    return jnp.where(lane < half, m0, m1)


def _ssd_post(y, z, g, name):
    n, d = y.shape
    tl = _tile(n)

    def body(y_ref, z_ref, g_ref, o_ref):
        z_ = z_ref[...]
        v = y_ref[...] * (z_ * _sig(z_))
        o_ref[...] = ((v * _group_norm_parts(v)) * g_ref[...]).astype(BF16)

    return _pc(body, name=name, out_shape=_sds((n, d), BF16), grid=(n // tl,),
               in_specs=[_row(tl, d), _row(tl, d), _const((1, d))], out_specs=_row(tl, d),
               compiler_params=_params(1))(y, z, g)


def _ssd_post_bwd(dout, y, z, g, name):
    n, d = y.shape
    tl = _tile(n)
    nb = n // tl

    def body(do_ref, y_ref, z_ref, g_ref, dy_ref, dz_ref, dg_ref, acc_g):
        i = pl.program_id(0)

        @pl.when(i == 0)
        def _():
            acc_g[...] = jnp.zeros_like(acc_g)

        z_ = z_ref[...]
        y_ = y_ref[...]
        sz = _sig(z_)
        silu_z = z_ * sz
        v = y_ * silu_z
        rs = _group_norm_parts(v)
        nv = v * rs
        do_ = do_ref[...]
        acc_g[...] += _fold(do_ * nv)
        dn = do_ * g_ref[...]
        dv = rs * (dn - nv * _group_mean(dn * nv))
        dy_ref[...] = dv * silu_z
        dz_ref[...] = (dv * y_ * (sz * (1.0 + z_ * (1.0 - sz)))).astype(BF16)

        @pl.when(i == nb - 1)
        def _():
            dg_ref[...] = jnp.sum(acc_g[...], axis=0, keepdims=True)

    return _pc(body, name=name, out_shape=(_sds((n, d)), _sds((n, d), BF16), _sds((1, d))), grid=(nb,),
               in_specs=[_row(tl, d), _row(tl, d), _row(tl, d), _const((1, d))],
               out_specs=(_row(tl, d), _row(tl, d), _const((1, d))),
               scratch_shapes=[pltpu.VMEM((SUBLANES, d), F32)], compiler_params=_params(1))(dout, y, z, g)


def _layer_norm_parts(uc):
    mu = jnp.mean(uc, axis=-1, keepdims=True)
    xc = uc - mu
    rstd = lax.rsqrt(jnp.mean(xc * xc, axis=-1, keepdims=True) + EPS)
    return xc * rstd, rstd


def _conf_fwd(conf_in, conv_w, conv_b, ln_g, ln_b, name):
    n = conf_in.shape[0]
    c = CONF_WIDTH
    tl = _tile(n)
    hb = CONF_HALO
    k_taps = CONF_KERNEL

    def body(x_ref, xp_ref, w_ref, b_ref, g_ref, beta_ref, o_ref, uc_ref, buf):
        i = pl.program_id(0)
        xp = xp_ref[...]
        buf[pl.ds(0, hb), :] = jnp.where(i > 0, xp[:, :c] * _sig(xp[:, c:]), 0.0)
        x_ = x_ref[...]
        buf[pl.ds(hb, tl), :] = x_[:, :c] * _sig(x_[:, c:])
        acc = b_ref[...] + w_ref[0:1, :] * buf[pl.ds(hb - (k_taps - 1), tl), :]
        for k in range(1, k_taps):
            acc = acc + w_ref[k:k + 1, :] * buf[pl.ds(hb - (k_taps - 1) + k, tl), :]
        uc_ref[...] = acc
        nv, _ = _layer_norm_parts(acc)
        v = nv * g_ref[...] + beta_ref[...]
        o_ref[...] = (v * _sig(v)).astype(BF16)

    return _pc(body, name=name, out_shape=(_sds((n, c), BF16), _sds((n, c))), grid=(n // tl,),
               in_specs=[_row(tl, 2 * c), _prev(tl, hb, 2 * c), _const((hb, c)), _const((1, c)), _const((1, c)),
                         _const((1, c))],
               out_specs=(_row(tl, c), _row(tl, c)),
               scratch_shapes=[pltpu.VMEM((tl + hb, c), F32)], compiler_params=_params(1))(
        conf_in, conf_in, conv_w, conv_b, ln_g, ln_b)


def _conf_bwd(dout, uc, conf_in, conv_w, ln_g, ln_b, name):
    n = conf_in.shape[0]
    c = CONF_WIDTH
    tl = _tile(n)
    nb = n // tl
    hb = CONF_HALO
    k_taps = CONF_KERNEL

    def body(do_ref, don_ref, uc_ref, ucn_ref, x_ref, xp_ref, w_ref, g_ref, beta_ref,
             dx_ref, dw_ref, db_ref, dg_ref, dbeta_ref, dbuf, ubuf, acc_w, acc_b, acc_g, acc_beta):
        i = pl.program_id(0)

        @pl.when(i == 0)
        def _():
            acc_w[...] = jnp.zeros_like(acc_w)
            acc_b[...] = jnp.zeros_like(acc_b)
            acc_g[...] = jnp.zeros_like(acc_g)
            acc_beta[...] = jnp.zeros_like(acc_beta)

        g_ = g_ref[...]
        beta_ = beta_ref[...]

        def d_conv_out(do_, uc_):
            nv, rstd = _layer_norm_parts(uc_)
            v = nv * g_ + beta_
            sv = _sig(v)
            dv = do_ * (sv * (1.0 + v * (1.0 - sv)))
            dn = dv * g_
            duc = rstd * (dn - jnp.mean(dn, axis=-1, keepdims=True)
                          - nv * jnp.mean(dn * nv, axis=-1, keepdims=True))
            return duc, dv, nv

        duc, dv, nv = d_conv_out(do_ref[...], uc_ref[...])
        acc_g[...] += _fold(dv * nv)
        acc_beta[...] += _fold(dv)
        acc_b[...] += _fold(duc)
        dbuf[pl.ds(0, tl), :] = duc
        ducn, _, _ = d_conv_out(don_ref[...], ucn_ref[...])
        dbuf[pl.ds(tl, hb), :] = jnp.where(i < nb - 1, ducn, 0.0)
        xp = xp_ref[...]
        ubuf[pl.ds(0, hb), :] = jnp.where(i > 0, xp[:, :c] * _sig(xp[:, c:]), 0.0)
        x_ = x_ref[...]
        val = x_[:, :c]
        sgate = _sig(x_[:, c:])
        ubuf[pl.ds(hb, tl), :] = val * sgate
        du = w_ref[0:1, :] * dbuf[pl.ds(k_taps - 1, tl), :]
        for k in range(1, k_taps):
            du = du + w_ref[k:k + 1, :] * dbuf[pl.ds(k_taps - 1 - k, tl), :]
        for k in range(k_taps):
            acc_w[k] += _fold(duc * ubuf[pl.ds(hb - (k_taps - 1) + k, tl), :])
        dx_ref[:, :c] = (du * sgate).astype(BF16)
        dx_ref[:, c:] = (du * val * sgate * (1.0 - sgate)).astype(BF16)

        @pl.when(i == nb - 1)
        def _():
            dw_ref[...] = jnp.zeros_like(dw_ref)
            for k in range(k_taps):
                dw_ref[k:k + 1, :] = jnp.sum(acc_w[k], axis=0, keepdims=True)
            db_ref[...] = jnp.sum(acc_b[...], axis=0, keepdims=True)
            dg_ref[...] = jnp.sum(acc_g[...], axis=0, keepdims=True)
            dbeta_ref[...] = jnp.sum(acc_beta[...], axis=0, keepdims=True)

    vec = _sds((1, c))
    return _pc(body, name=name, out_shape=(_sds((n, 2 * c), BF16), _sds((hb, c)), vec, vec, vec), grid=(nb,),
               in_specs=[_row(tl, c), _next(tl, hb, c, n), _row(tl, c), _next(tl, hb, c, n), _row(tl, 2 * c),
                         _prev(tl, hb, 2 * c), _const((hb, c)), _const((1, c)), _const((1, c))],
               out_specs=(_row(tl, 2 * c), _const((hb, c)), _const((1, c)), _const((1, c)), _const((1, c))),
               scratch_shapes=[pltpu.VMEM((tl + hb, c), F32), pltpu.VMEM((tl + hb, c), F32),
                               pltpu.VMEM((k_taps, SUBLANES, c), F32), pltpu.VMEM((SUBLANES, c), F32),
                               pltpu.VMEM((SUBLANES, c), F32), pltpu.VMEM((SUBLANES, c), F32)],
               compiler_params=_params(1))(dout, dout, uc, uc, conf_in, conf_in, conv_w, ln_g, ln_b)


def _sc_fwd(sc_in, conv_w, name):
    n = sc_in.shape[0]
    c = SC_WIDTH
    tl = _tile(n)
    hb = CONV_HALO

    def body(x_ref, xp_ref, w_ref, o_ref, buf):
        i = pl.program_id(0)
        xp = xp_ref[...]
        buf[pl.ds(0, hb), :] = jnp.where(i > 0, xp[:, c:2 * c] * xp[:, 2 * c:], 0.0)
        x_ = x_ref[...]
        buf[pl.ds(hb, tl), :] = x_[:, c:2 * c] * x_[:, 2 * c:]
        cv = w_ref[0:1, :] * buf[pl.ds(hb - 2, tl), :]
        for k in range(1, 3):
            cv = cv + w_ref[k:k + 1, :] * buf[pl.ds(hb - 2 + k, tl), :]
        o_ref[...] = (x_[:, :c] * cv).astype(BF16)

    return _pc(body, name=name, out_shape=_sds((n, c), BF16), grid=(n // tl,),
               in_specs=[_row(tl, 3 * c), _prev(tl, hb, 3 * c), _const((SUBLANES, c))], out_specs=_row(tl, c),
               scratch_shapes=[pltpu.VMEM((tl + hb, c), F32)], compiler_params=_params(1))(sc_in, sc_in, conv_w)


def _sc_bwd(dout, sc_in, conv_w, name):
    n = sc_in.shape[0]
    c = SC_WIDTH
    tl = _tile(n)
    nb = n // tl
    hb = CONV_HALO

    def body(do_ref, don_ref, x_ref, xp_ref, xn_ref, w_ref, dx_ref, dw_ref, dbuf, pbuf, acc_w):
        i = pl.program_id(0)

        @pl.when(i == 0)
        def _():
            acc_w[...] = jnp.zeros_like(acc_w)

        x_ = x_ref[...]
        gb, gc, xv = x_[:, :c], x_[:, c:2 * c], x_[:, 2 * c:]
        do_ = do_ref[...]
        dcv = do_ * gb
        dbuf[pl.ds(0, tl), :] = dcv
        dbuf[pl.ds(tl, hb), :] = jnp.where(i < nb - 1, don_ref[...] * xn_ref[...], 0.0)
        xp = xp_ref[...]
        pbuf[pl.ds(0, hb), :] = jnp.where(i > 0, xp[:, c:2 * c] * xp[:, 2 * c:], 0.0)
        pbuf[pl.ds(hb, tl), :] = gc * xv
        cv = w_ref[0:1, :] * pbuf[pl.ds(hb - 2, tl), :]
        dp = w_ref[0:1, :] * dbuf[pl.ds(2, tl), :]
        for k in range(1, 3):
            cv = cv + w_ref[k:k + 1, :] * pbuf[pl.ds(hb - 2 + k, tl), :]
            dp = dp + w_ref[k:k + 1, :] * dbuf[pl.ds(2 - k, tl), :]
        for k in range(3):
            acc_w[k] += _fold(dcv * pbuf[pl.ds(hb - 2 + k, tl), :])
        dx_ref[:, :c] = (do_ * cv).astype(BF16)
        dx_ref[:, c:2 * c] = (dp * xv).astype(BF16)
        dx_ref[:, 2 * c:] = (dp * gc).astype(BF16)

        @pl.when(i == nb - 1)
        def _():
            dw_ref[...] = jnp.zeros_like(dw_ref)
            for k in range(3):
                dw_ref[k:k + 1, :] = jnp.sum(acc_w[k], axis=0, keepdims=True)

    return _pc(body, name=name, out_shape=(_sds((n, 3 * c), BF16), _sds((SUBLANES, c))), grid=(nb,),
               in_specs=[_row(tl, c), _next(tl, hb, c, n), _row(tl, 3 * c), _prev(tl, hb, 3 * c),
                         _next(tl, hb, c, n), _const((SUBLANES, c))],
               out_specs=(_row(tl, 3 * c), _const((SUBLANES, c))),
               scratch_shapes=[pltpu.VMEM((tl + hb, c), F32), pltpu.VMEM((tl + hb, c), F32),
                               pltpu.VMEM((3, SUBLANES, c), F32)],
               compiler_params=_params(1))(dout, dout, sc_in, sc_in, sc_in, conv_w)


def _merge_fwd(gates, ya, yb, yc, b_gate, name):
    n, d = ya.shape
    tl = _tile(n)

    def body(gt_ref, ya_ref, yb_ref, yc_ref, b_ref, o_ref):
        gt = _sig(gt_ref[...] + b_ref[...])
        o_ref[...] = (gt[:, :d] * ya_ref[...] + gt[:, d:2 * d] * yb_ref[...] + gt[:, 2 * d:] * yc_ref[...]).astype(BF16)

    return _pc(body, name=name, out_shape=_sds((n, d), BF16), grid=(n // tl,),
               in_specs=[_row(tl, 3 * d), _row(tl, d), _row(tl, d), _row(tl, d), _const((1, 3 * d))],
               out_specs=_row(tl, d), compiler_params=_params(1))(gates, ya, yb, yc, b_gate)


def _merge_bwd(dm, gates, ya, yb, yc, b_gate, name):
    n, d = ya.shape
    tl = _tile(n)
    nb = n // tl

    def body(dm_ref, gt_ref, ya_ref, yb_ref, yc_ref, b_ref, dya_ref, dyb_ref, dyc_ref, dgt_ref, db_ref, acc):
        i = pl.program_id(0)

        @pl.when(i == 0)
        def _():
            acc[...] = jnp.zeros_like(acc)

        dm_ = dm_ref[...]
        gt = _sig(gt_ref[...] + b_ref[...])
        for idx, (y_ref, dy_ref) in enumerate(((ya_ref, dya_ref), (yb_ref, dyb_ref), (yc_ref, dyc_ref))):
            gk = gt[:, idx * d:(idx + 1) * d]
            dy_ref[...] = (dm_ * gk).astype(BF16)
            dpre = dm_ * y_ref[...] * gk * (1.0 - gk)
            dgt_ref[:, idx * d:(idx + 1) * d] = dpre.astype(BF16)
            acc[:, idx * d:(idx + 1) * d] += _fold(dpre)

        @pl.when(i == nb - 1)
        def _():
            db_ref[...] = jnp.sum(acc[...], axis=0, keepdims=True)

    bf = _sds((n, d), BF16)
    return _pc(body, name=name, out_shape=(bf, bf, bf, _sds((n, 3 * d), BF16), _sds((1, 3 * d))), grid=(nb,),
               in_specs=[_row(tl, d), _row(tl, 3 * d), _row(tl, d), _row(tl, d), _row(tl, d), _const((1, 3 * d))],
               out_specs=(_row(tl, d), _row(tl, d), _row(tl, d), _row(tl, 3 * d), _const((1, 3 * d))),
               scratch_shapes=[pltpu.VMEM((SUBLANES, 3 * d), F32)], compiler_params=_params(1))(
        dm, gates, ya, yb, yc, b_gate)


FFN_COLS = 1408


def _down(prev, cur, s, row):
    return jnp.where(row < s, pltpu.roll(prev, s, 0), pltpu.roll(cur, s, 0))


def _up(cur, nxt, s, row):
    return jnp.where(row < SUBLANES - s, pltpu.roll(cur, SUBLANES - s, 0), pltpu.roll(nxt, SUBLANES - s, 0))


def _ffn_mid(up, conv_w, conv_b, name):
    n = up.shape[0]
    tl = _tile(n)
    hb = CONV_HALO
    tc = FFN_COLS
    ncb = D_FF // tc

    def spec(shape_rows, idx_fn, off):
        return pl.BlockSpec((shape_rows, tc), lambda j, i, off=off: (idx_fn(i), j + off))

    r = tl // hb
    cur = lambda i: i
    prv = lambda i: jnp.maximum(i * r - 1, 0)

    def body(g_ref, gp_ref, v_ref, vp_ref, wg_ref, wv_ref, bg_ref, bv_ref, o_ref):
        i = pl.program_id(1)
        row = lax.broadcasted_iota(jnp.int32, (SUBLANES, LANES), 0)
        full = lambda ref, k, ls: jnp.broadcast_to(ref[k:k + 1, ls], (SUBLANES, LANES))
        for lc in range(tc // LANES):
            ls = slice(lc * LANES, (lc + 1) * LANES)
            wg = [full(wg_ref, k, ls) for k in range(3)]
            wv = [full(wv_ref, k, ls) for k in range(3)]
            bg, bv = full(bg_ref, 0, ls), full(bv_ref, 0, ls)

            def conv(prev, x, w, b):
                return b + w[0] * _down(prev, x, 2, row) + w[1] * _down(prev, x, 1, row) + w[2] * x

            def strip(t, carry):
                gp, vp = carry
                r0 = pl.multiple_of(t * 16, 16)
                r1 = pl.multiple_of(t * 16 + SUBLANES, SUBLANES)
                ga, gb = g_ref[pl.ds(r0, SUBLANES), ls], g_ref[pl.ds(r1, SUBLANES), ls]
                va, vb = v_ref[pl.ds(r0, SUBLANES), ls], v_ref[pl.ds(r1, SUBLANES), ls]
                outs = []
                for g0, g1, v0, v1 in ((gp, ga, vp, va), (ga, gb, va, vb)):
                    ug = conv(g0, g1, wg, bg)
                    outs.append(ug * _sig(ug) * conv(v0, v1, wv, bv))
                o_ref[pl.ds(r0, 16), ls] = jnp.concatenate(outs, axis=0).astype(BF16)
                return gb, vb

            lax.fori_loop(0, tl // 16, strip,
                          (jnp.where(i > 0, gp_ref[:, ls], 0.0), jnp.where(i > 0, vp_ref[:, ls], 0.0)))

    wspec = lambda off: pl.BlockSpec((SUBLANES, tc), lambda j, i, off=off: (0, j + off))
    bspec = lambda off: pl.BlockSpec((1, tc), lambda j, i, off=off: (0, j + off))
    return _pc(body, name=name, out_shape=_sds((n, D_FF), BF16), grid=(ncb, n // tl),
               in_specs=[spec(tl, cur, 0), spec(hb, prv, 0), spec(tl, cur, ncb), spec(hb, prv, ncb),
                         wspec(0), wspec(ncb), bspec(0), bspec(ncb)],
               out_specs=pl.BlockSpec((tl, tc), lambda j, i: (i, j)), compiler_params=_params(2))(
        up, up, up, up, conv_w, conv_w, conv_b, conv_b)


def _ffn_mid_bwd(da, up, conv_w, conv_b, name):
    n = up.shape[0]
    tl = _tile(n)
    nb = n // tl
    nt = tl // 16
    hb = CONV_HALO
    tc = FFN_COLS
    ncb = D_FF // tc
    r = tl // hb
    last = n // hb - 1
    cur = lambda i: i
    prv = lambda i: jnp.maximum(i * r - 1, 0)
    nxt = lambda i: jnp.minimum((i + 1) * r, last)

    def spec(shape_rows, idx_fn, off):
        return pl.BlockSpec((shape_rows, tc), lambda j, i, off=off: (idx_fn(i), j + off))

    def body(da_ref, dan_ref, g_ref, gp_ref, gn_ref, v_ref, vp_ref, vn_ref, wg_ref, wv_ref, bg_ref, bv_ref,
             dg_ref, dv_ref, dwg_ref, dwv_ref, dbg_ref, dbv_ref, acc_w, acc_b):
        i = pl.program_id(1)

        @pl.when(i == 0)
        def _():
            acc_w[...] = jnp.zeros_like(acc_w)
            acc_b[...] = jnp.zeros_like(acc_b)

        row = lax.broadcasted_iota(jnp.int32, (SUBLANES, LANES), 0)
        full = lambda ref, k, ls: jnp.broadcast_to(ref[k:k + 1, ls], (SUBLANES, LANES))
        zero = jnp.zeros((SUBLANES, LANES), F32)

        def d_conv_out(da_, ug, uv):
            s = _sig(ug)
            return da_ * uv * (s * (1.0 + ug * (1.0 - s))), da_ * (ug * s)

        for lc in range(tc // LANES):
            ls = slice(lc * LANES, (lc + 1) * LANES)
            wg = [full(wg_ref, k, ls) for k in range(3)]
            wv = [full(wv_ref, k, ls) for k in range(3)]
            bg, bv = full(bg_ref, 0, ls), full(bv_ref, 0, ls)

            def unit(prev_g, g, prev_v, v, da_):
                g1, g2 = _down(prev_g, g, 1, row), _down(prev_g, g, 2, row)
                v1, v2 = _down(prev_v, v, 1, row), _down(prev_v, v, 2, row)
                ug = bg + wg[0] * g2 + wg[1] * g1 + wg[2] * g
                uv = bv + wv[0] * v2 + wv[1] * v1 + wv[2] * v
                dug, duv = d_conv_out(da_, ug, uv)
                return dug, duv, (g2, g1, g), (v2, v1, v)

            def d_in(d, d_next, w):
                return w[2] * d + w[1] * _up(d, d_next, 1, row) + w[0] * _up(d, d_next, 2, row)

            tail = pl.ds(tl - SUBLANES, SUBLANES)
            dgn, dvn, _, _ = unit(g_ref[tail, ls], gn_ref[:, ls], v_ref[tail, ls], vn_ref[:, ls], dan_ref[:, ls])
            dgn = jnp.where(i < nb - 1, dgn, 0.0)
            dvn = jnp.where(i < nb - 1, dvn, 0.0)
            gp0 = jnp.where(i > 0, gp_ref[:, ls], 0.0)
            vp0 = jnp.where(i > 0, vp_ref[:, ls], 0.0)

            def strip(tt, carry):
                dgn, dvn = carry[0], carry[1]
                aw, ab = list(carry[2:8]), list(carry[8:10])
                t = nt - 1 - tt
                r0 = pl.multiple_of(t * 16, 16)
                r1 = pl.multiple_of(t * 16 + SUBLANES, SUBLANES)
                rm = pl.multiple_of(jnp.maximum(t * 16 - SUBLANES, 0), SUBLANES)
                g_bef = jnp.where(t > 0, g_ref[pl.ds(rm, SUBLANES), ls], gp0)
                v_bef = jnp.where(t > 0, v_ref[pl.ds(rm, SUBLANES), ls], vp0)
                ga, gb = g_ref[pl.ds(r0, SUBLANES), ls], g_ref[pl.ds(r1, SUBLANES), ls]
                va, vb = v_ref[pl.ds(r0, SUBLANES), ls], v_ref[pl.ds(r1, SUBLANES), ls]
                dgb, dvb, gsb, vsb = unit(ga, gb, va, vb, da_ref[pl.ds(r1, SUBLANES), ls])
                dga, dva, gsa, vsa = unit(g_bef, ga, v_bef, va, da_ref[pl.ds(r0, SUBLANES), ls])
                dg_ref[pl.ds(r0, 16), ls] = jnp.concatenate([d_in(dga, dgb, wg), d_in(dgb, dgn, wg)],
                                                            axis=0).astype(BF16)
                dv_ref[pl.ds(r0, 16), ls] = jnp.concatenate([d_in(dva, dvb, wv), d_in(dvb, dvn, wv)],
                                                            axis=0).astype(BF16)
                for k in range(3):
                    aw[k] = aw[k] + dga * gsa[k] + dgb * gsb[k]
                    aw[3 + k] = aw[3 + k] + dva * vsa[k] + dvb * vsb[k]
                ab[0] = ab[0] + dga + dgb
                ab[1] = ab[1] + dva + dvb
                return (dga, dva, *aw, *ab)

            res = lax.fori_loop(0, nt, strip, (dgn, dvn) + (zero,) * 8)
            for k in range(3):
                acc_w[0, k, :, ls] += res[2 + k]
                acc_w[1, k, :, ls] += res[5 + k]
            acc_b[0, :, ls] += res[8]
            acc_b[1, :, ls] += res[9]

        @pl.when(i == nb - 1)
        def _():
            for t, (dw_ref, db_ref) in enumerate(((dwg_ref, dbg_ref), (dwv_ref, dbv_ref))):
                dw_ref[...] = jnp.zeros_like(dw_ref)
                for k in range(3):
                    dw_ref[k:k + 1, :] = jnp.sum(acc_w[t, k], axis=0, keepdims=True)
                db_ref[...] = jnp.sum(acc_b[t], axis=0, keepdims=True)

    wspec = lambda off: pl.BlockSpec((SUBLANES, tc), lambda j, i, off=off: (0, j + off))
    bspec = lambda off: pl.BlockSpec((1, tc), lambda j, i, off=off: (0, j + off))
    ospec = lambda off: pl.BlockSpec((tl, tc), lambda j, i, off=off: (i, j + off))
    dg, dv, dwg, dwv, dbg, dbv = _pc(
        body, name=name,
        out_shape=(_sds((n, D_FF), BF16), _sds((n, D_FF), BF16), _sds((SUBLANES, D_FF)), _sds((SUBLANES, D_FF)),
                   _sds((1, D_FF)), _sds((1, D_FF))),
        grid=(ncb, nb),
        in_specs=[spec(tl, cur, 0), spec(hb, nxt, 0),
                  spec(tl, cur, 0), spec(hb, prv, 0), spec(hb, nxt, 0),
                  spec(tl, cur, ncb), spec(hb, prv, ncb), spec(hb, nxt, ncb),
                  wspec(0), wspec(ncb), bspec(0), bspec(ncb)],
        out_specs=(ospec(0), ospec(0), wspec(0), wspec(0), bspec(0), bspec(0)),
        scratch_shapes=[pltpu.VMEM((2, 3, SUBLANES, tc), F32), pltpu.VMEM((2, SUBLANES, tc), F32)],
        compiler_params=_params(2))(da, da, up, up, up, up, up, up, conv_w, conv_w, conv_b, conv_b)
    return dg, dv, jnp.concatenate([dwg, dwv], axis=1), jnp.concatenate([dbg, dbv], axis=1)


def _position():
    return lax.axis_index("x"), lax.axis_index("y"), lax.axis_index("c")


def _all_gather(locals_, name):
    n = len(locals_)

    def body(*refs):
        x_refs, out_refs = refs[:n], refs[n:2 * n]
        send_sems, recv_sems, local_sems = refs[2 * n:]
        x, y, cc = _position()
        me, sibling = (x, y, cc), (x, y, 1 - cc)
        chips = [(1 - x, y), (x, 1 - y), (1 - x, 1 - y)]

        def slot(a, px, py, pc):
            return out_refs[a].at[4 * px + 2 * py + pc]

        def copy(k, a, block, to, own=False):
            return pltpu.make_async_remote_copy(
                src_ref=x_refs[a] if own else slot(a, *block), dst_ref=slot(a, *block),
                send_sem=send_sems.at[k, a], recv_sem=recv_sems.at[k, a], device_id=to, device_id_type=MESH)

        mine = [pltpu.make_async_copy(x_refs[a], slot(a, *me), local_sems.at[a]) for a in range(n)]
        first = [copy(1 + j, a, me, (*chip, cc), own=True) for j, chip in enumerate(chips) for a in range(n)]
        first += [copy(0, a, me, sibling, own=True) for a in range(n)]
        for cp in mine + first:
            cp.start()
        passed = []
        for j, chip in enumerate(chips):
            for a in range(n):
                copy(1 + j, a, (*chip, cc), me).wait_recv()
                cp = copy(4 + j, a, (*chip, cc), sibling)
                cp.start()
                passed.append(cp)
        for a in range(n):
            copy(0, a, sibling, me).wait_recv()
        for j, chip in enumerate(chips):
            for a in range(n):
                copy(4 + j, a, (*chip, 1 - cc), me).wait_recv()
        for cp in first + passed:
            cp.wait_send()
        for cp in mine:
            cp.wait()

    hbm = pl.BlockSpec(memory_space=pl.ANY)
    return _pc(body, name=name, out_shape=[_sds((N_DEV,) + a.shape, a.dtype) for a in locals_],
               in_specs=[hbm] * n, out_specs=[hbm] * n,
               scratch_shapes=[pltpu.SemaphoreType.DMA((7, n)), pltpu.SemaphoreType.DMA((7, n)),
                               pltpu.SemaphoreType.DMA((n,))])(*locals_)


N_CHIPS = 4


def _sibling_swap(parts, name):
    n = len(parts)

    def body(*refs):
        g_refs, kept_refs, got_refs = refs[:n], refs[n:2 * n], refs[2 * n:3 * n]
        send_sems, recv_sems, local_sems = refs[3 * n:]
        x, y, cc = _position()
        keeps, swaps = [], []
        for q in range(N_CHIPS):
            for a in range(n):
                keeps.append(pltpu.make_async_copy(g_refs[a].at[2 * q + cc], kept_refs[a].at[q], local_sems.at[q, a]))
                swaps.append(pltpu.make_async_remote_copy(
                    src_ref=g_refs[a].at[2 * q + 1 - cc], dst_ref=got_refs[a].at[q], send_sem=send_sems.at[q, a],
                    recv_sem=recv_sems.at[q, a], device_id=(x, y, 1 - cc), device_id_type=MESH))
        for cp in swaps + keeps:
            cp.start()
        for cp in swaps:
            cp.wait_recv()
        for cp in swaps:
            cp.wait_send()
        for cp in keeps:
            cp.wait()

    hbm = pl.BlockSpec(memory_space=pl.ANY)
    shapes = [_sds((N_CHIPS,) + a.shape[1:], a.dtype) for a in parts]
    out = _pc(body, name=name, out_shape=shapes + shapes, in_specs=[hbm] * n, out_specs=[hbm] * (2 * n),
              scratch_shapes=[pltpu.SemaphoreType.DMA((N_CHIPS, n)), pltpu.SemaphoreType.DMA((N_CHIPS, n)),
                              pltpu.SemaphoreType.DMA((N_CHIPS, n))])(*parts)
    return out[:n], out[n:]


def _pair_add(kept, got, name):
    q, s, a, b = kept.shape
    ta = _block_rows(a, b)

    def body(k_ref, g_ref, o_ref):
        o_ref[...] = (k_ref[...].astype(F32) + g_ref[...].astype(F32)).astype(BF16)

    spec = pl.BlockSpec((None, None, ta, b), lambda c, l, i: (c, l, i, 0))
    return _pc(body, name=name, out_shape=_sds(kept.shape, BF16), grid=(q, s, a // ta), in_specs=[spec, spec],
               out_specs=spec, compiler_params=_params(3))(kept, got)


def _chip_exchange(sums, name):
    n = len(sums)
    flips = [(1, 0), (0, 1), (1, 1)]

    def body(*refs):
        g_refs, out_refs = refs[:n], refs[n:2 * n]
        send_sems, recv_sems, local_sems = refs[2 * n:]
        x, y, cc = _position()
        me_q = 2 * x + y
        mine = [pltpu.make_async_copy(g_refs[a].at[me_q], out_refs[a].at[me_q], local_sems.at[a]) for a in range(n)]
        sends, recvs = [], []
        for k, (fx, fy) in enumerate(flips):
            px, py = (1 - x if fx else x), (1 - y if fy else y)
            peer_q = 2 * px + py
            for a in range(n):
                sends.append(pltpu.make_async_remote_copy(
                    src_ref=g_refs[a].at[peer_q], dst_ref=out_refs[a].at[me_q], send_sem=send_sems.at[k, a],
                    recv_sem=recv_sems.at[k, a], device_id=(px, py, cc), device_id_type=MESH))
                recvs.append(pltpu.make_async_remote_copy(
                    src_ref=g_refs[a].at[me_q], dst_ref=out_refs[a].at[peer_q], send_sem=send_sems.at[k, a],
                    recv_sem=recv_sems.at[k, a], device_id=(px, py, cc), device_id_type=MESH))
        for cp in sends + mine:
            cp.start()
        for cp in recvs:
            cp.wait_recv()
        for cp in sends:
            cp.wait_send()
        for cp in mine:
            cp.wait()

    hbm = pl.BlockSpec(memory_space=pl.ANY)
    return _pc(body, name=name, out_shape=[_sds(a.shape, a.dtype) for a in sums],
               in_specs=[hbm] * n, out_specs=[hbm] * n,
               scratch_shapes=[pltpu.SemaphoreType.DMA((3, n)), pltpu.SemaphoreType.DMA((3, n)),
                               pltpu.SemaphoreType.DMA((n,))])(*sums)


def _block_rows(a, b):
    ta = a
    while ta * b > 256 * 1024 and ta % 32 == 0:
        ta //= 2
    return ta


def _reduce_adamw(parts, w, m, v, name):
    n_parts, s, a, b = parts.shape
    ta = _block_rows(a, b)
    c1 = 1.0 - ADAM_B1 ** ADAM_STEP
    c2 = 1.0 - ADAM_B2 ** ADAM_STEP

    def body(p_ref, w_ref, m_ref, v_ref, g_out, d_out, m_out, v_out):
        g = p_ref[0].astype(F32)
        for j in range(1, n_parts):
            g = g + p_ref[j].astype(F32)
        m_new = ADAM_B1 * m_ref[...] + (1.0 - ADAM_B1) * g
        v_new = ADAM_B2 * v_ref[...] + (1.0 - ADAM_B2) * (g * g)
        m_hat = m_new / c1
        v_hat = v_new / c2
        g_out[...] = g
        d_out[...] = -ADAM_LR * (m_hat / (jnp.sqrt(v_hat) + ADAM_EPS) + ADAM_WD * w_ref[...])
        m_out[...] = m_new
        v_out[...] = v_new

    spec = pl.BlockSpec((None, ta, b), lambda l, i: (l, i, 0))
    return _pc(body, name=name, out_shape=(_sds((s, a, b)),) * 4, grid=(s, a // ta),
               in_specs=[pl.BlockSpec((n_parts, None, ta, b), lambda l, i: (0, l, i, 0)), spec, spec, spec],
               out_specs=(spec,) * 4, compiler_params=_params(2))(parts, w, m, v)


MATRICES = (("ada_mix_w", 2), ("w_in", 2), ("w_ssd_out", 1), ("w_conf_out", 2), ("w_sc_out", 2), ("w_o", 1),
            ("ada_ffn_w", 2), ("w_up", 2), ("w_down", 1))
CONV_WEIGHTS = (("ssd_conv_w", 2), ("conf_conv_w", 2), ("sc_conv_w", 2), ("ffn_conv_w", 2))
SHARDED = MATRICES + CONV_WEIGHTS
REPLICATED = ("ada_mix_b", "norm_mix_g", "b_gate", "ssd_conv_b", "ssd_dt_bias", "ssd_a_log", "ssd_d", "ssd_norm_g",
              "conf_conv_b", "conf_ln_g", "conf_ln_b", "ada_ffn_b", "norm_ffn_g", "ffn_conv_b", "final_norm_g")
WEIGHT_NAMES = ("ada_mix_w", "ada_mix_b", "norm_mix_g", "w_in", "b_gate", "ssd_conv_w", "ssd_conv_b", "ssd_dt_bias",
                "ssd_a_log", "ssd_d", "ssd_norm_g", "w_ssd_out", "conf_conv_w", "conf_conv_b", "conf_ln_g",
                "conf_ln_b", "w_conf_out", "sc_conv_w", "w_sc_out", "w_o", "ada_ffn_w", "ada_ffn_b", "norm_ffn_g",
                "w_up", "ffn_conv_w", "ffn_conv_b", "w_down", "final_norm_g")


def _pack_flat(arrays, cols, row_multiple, dtype):
    flat = jnp.concatenate([a.reshape(-1).astype(dtype) for a in arrays])
    rows = -(-flat.shape[0] // cols)
    rows = -(-rows // row_multiple) * row_multiple
    return jnp.pad(flat, (0, rows * cols - flat.shape[0])).reshape(rows, cols)


def _unpack_flat(flat2d, shapes):
    flat = flat2d.reshape(-1)
    out, off = [], 0
    for s in shapes:
        n = 1
        for d in s:
            n *= d
        out.append(flat[off:off + n].reshape(s))
        off += n
    return out


def _cols(g, i, lo, hi):
    b = g.shape[-1]
    pieces = []
    for k in range(N_DEV):
        a, e = max(lo, k * b), min(hi, (k + 1) * b)
        if a < e:
            pieces.append(g[k, i, :, a - k * b:e - k * b])
    return pieces[0] if len(pieces) == 1 else jnp.concatenate(pieces, axis=1)


def _rows(g, i):
    return g[:, i].reshape(N_DEV * g.shape[2], g.shape[3])


def _col_shards(segs, b):
    shards = []
    for k in range(N_DEV):
        lo, hi = k * b, (k + 1) * b
        pieces, off = [], 0
        for seg in segs:
            n = seg.shape[1]
            a, e = max(lo, off), min(hi, off + n)
            if a < e:
                pieces.append(seg[:, a - off:e - off])
            off += n
        shards.append(pieces[0] if len(pieces) == 1 else jnp.concatenate(pieces, axis=1))
    return jnp.stack(shards)


def _row_shards(full):
    return full.reshape(N_DEV, full.shape[0] // N_DEV, full.shape[1])


def _pad_rows(a, rows):
    return jnp.pad(a, ((0, rows - a.shape[0]), (0, 0)))


def _pad_lanes(a):
    return jnp.pad(a, ((0, 0), (0, LANES - a.shape[1])))


def _layer_weights(full, i):
    row = lambda name: full[name][i].reshape(1, -1)
    whole = lambda name: _cols(full[name], i, 0, N_DEV * full[name].shape[-1])
    w_in = full["w_in"]
    wl = {
        "ada_mix_w": whole("ada_mix_w"), "ada_mix_b": row("ada_mix_b"), "norm_mix_g": row("norm_mix_g"),
        "w_z": _cols(w_in, i, 0, OFF_Z), "w_xbc": _cols(w_in, i, OFF_Z, OFF_XBC),
        "w_dt": _pad_lanes(_cols(w_in, i, OFF_XBC, OFF_DT)), "w_conf": _cols(w_in, i, OFF_DT, OFF_CONF),
        "w_sc": _cols(w_in, i, OFF_CONF, OFF_SC), "w_gates": _cols(w_in, i, OFF_SC, N_IN),
        "b_gate": row("b_gate"),
        "ssd_conv_w": _pad_rows(whole("ssd_conv_w"), SUBLANES), "ssd_conv_b": row("ssd_conv_b"),
        "dt_bias": _pad_lanes(row("ssd_dt_bias")), "a_log": _pad_lanes(row("ssd_a_log")),
        "ssd_d": _pad_lanes(row("ssd_d")), "ssd_norm_g": row("ssd_norm_g"), "w_ssd_out": _rows(full["w_ssd_out"], i),
        "conf_conv_w": _pad_rows(whole("conf_conv_w"), CONF_HALO), "conf_conv_b": row("conf_conv_b"),
        "conf_ln_g": row("conf_ln_g"), "conf_ln_b": row("conf_ln_b"), "w_conf_out": whole("w_conf_out"),
        "sc_conv_w": _pad_rows(whole("sc_conv_w"), SUBLANES), "w_sc_out": whole("w_sc_out"),
        "w_o": _rows(full["w_o"], i),
        "ada_ffn_w": whole("ada_ffn_w"), "ada_ffn_b": row("ada_ffn_b"), "norm_ffn_g": row("norm_ffn_g"),
        "w_up": whole("w_up"), "ffn_conv_w": _pad_rows(whole("ffn_conv_w"), SUBLANES),
        "ffn_conv_b": row("ffn_conv_b"), "w_down": _rows(full["w_down"], i),
    }
    return wl


def _adaln(sc8, w, b, name):
    mod = _matmul(sc8, w, "nn", F32, name)[0:1, :] + b
    return mod[:, :D_MODEL], mod[:, D_MODEL:2 * D_MODEL], mod[:, 2 * D_MODEL:]


def _layer_fwd(i, x, prev, sc8, wl):
    t = f"l{i}_"
    s = {}
    shift, scale, gate = _adaln(sc8, wl["ada_mix_w"], wl["ada_mix_b"], t + "ada_mix")
    if prev is None:
        s["x_in"] = x
        s["h"] = _prenorm_first(x, wl["norm_mix_g"], scale, shift, t + "norm_mix")
    else:
        s["x_in"], s["h"] = _prenorm_res(x, prev[0], prev[1], wl["norm_mix_g"], scale, shift, t + "norm_mix")
    s["scale_mix"], s["gate_mix"] = scale, gate
    h = s["h"]
    s["z"] = _matmul(h, wl["w_z"], "nn", F32, t + "in_z")
    s["xbc"] = _matmul(h, wl["w_xbc"], "nn", F32, t + "in_xbc")
    s["dt_raw"] = _matmul(h, wl["w_dt"], "nn", F32, t + "in_dt")
    s["conf"] = _matmul(h, wl["w_conf"], "nn", F32, t + "in_conf")
    s["sc"] = _matmul(h, wl["w_sc"], "nn", F32, t + "in_sc")
    s["gates"] = _matmul(h, wl["w_gates"], "nn", F32, t + "in_gates")
    s["pre"], s["dt"] = _ssd_pre(s["xbc"], s["dt_raw"], wl["ssd_conv_w"], wl["ssd_conv_b"], wl["dt_bias"],
                                 t + "ssd_pre")
    s["y"], s["hprev"] = _ssd_scan(s["pre"], s["dt"], wl["a_log"], wl["ssd_d"], t + "ssd_scan")
    s["ya_in"] = _ssd_post(s["y"], s["z"], wl["ssd_norm_g"], t + "ssd_post")
    s["yb_in"], s["uc"] = _conf_fwd(s["conf"], wl["conf_conv_w"], wl["conf_conv_b"], wl["conf_ln_g"],
                                    wl["conf_ln_b"], t + "conf")
    s["yc_in"] = _sc_fwd(s["sc"], wl["sc_conv_w"], t + "sconv")
    s["ya"] = _matmul(s["ya_in"], wl["w_ssd_out"], "nn", F32, t + "ssd_out")
    s["yb"] = _matmul(s["yb_in"], wl["w_conf_out"], "nn", F32, t + "conf_out")
    s["yc"] = _matmul(s["yc_in"], wl["w_sc_out"], "nn", F32, t + "sc_out")
    s["merged"] = _merge_fwd(s["gates"], s["ya"], s["yb"], s["yc"], wl["b_gate"], t + "merge")
    s["mix"] = _matmul(s["merged"], wl["w_o"], "nn", F32, t + "w_o")
    shift2, scale2, gate2 = _adaln(sc8, wl["ada_ffn_w"], wl["ada_ffn_b"], t + "ada_ffn")
    s["x_mid"], s["h2"] = _prenorm_res(s["x_in"], s["mix"], gate, wl["norm_ffn_g"], scale2, shift2, t + "norm_ffn")
    s["scale_ffn"], s["gate_ffn"] = scale2, gate2
    s["up"] = _matmul(s["h2"], wl["w_up"], "nn", F32, t + "w_up")
    s["a"] = _ffn_mid(s["up"], wl["ffn_conv_w"], wl["ffn_conv_b"], t + "ffn_mid")
    s["out"] = _matmul(s["a"], wl["w_down"], "nn", F32, t + "w_down")
    return s


def _layer_bwd(i, s, wl, sc8, dys_ffn, dx_after, dgate_ffn, prev):
    t = f"l{i}_b_"
    g = {}
    da = _matmul(dys_ffn, wl["w_down"], "nt", F32, t + "d_a")
    g["w_down"] = _matmul(s["a"], dys_ffn, "tn", BF16, t + "dw_down")
    dug, duv, dfw, g["ffn_conv_b"] = _ffn_mid_bwd(da, s["up"], wl["ffn_conv_w"], wl["ffn_conv_b"], t + "ffn_mid")
    g["ffn_conv_w"] = dfw[:3]
    dh2 = _matmul(dug, wl["w_up"][:, :D_FF], "nt", F32, t + "d_h2_g")
    dh2 = _matmul(duv, wl["w_up"][:, D_FF:], "nt", F32, t + "d_h2_v", add=dh2)
    g["w_up"] = [_matmul(s["h2"], dug, "tn", BF16, t + "dw_up_g"), _matmul(s["h2"], duv, "tn", BF16, t + "dw_up_v")]
    dx_mid, dshift2, dscale2, g["norm_ffn_g"], dys_mix, dgate_mix = _norm_bwd(
        dh2, s["x_mid"], dx_after, wl["norm_ffn_g"], s["scale_ffn"], t + "norm_ffn", s["mix"], s["gate_mix"])
    dmod_ffn = jnp.concatenate([dshift2, dscale2, dgate_ffn], axis=1)
    g["ada_ffn_b"] = dmod_ffn
    g["ada_ffn_w"] = [_matmul(sc8, _pad_rows(dmod_ffn, SUBLANES), "tn", BF16, t + "dw_ada_ffn")]
    dmerged = _matmul(dys_mix, wl["w_o"], "nt", F32, t + "d_merged")
    g["w_o"] = _matmul(s["merged"], dys_mix, "tn", BF16, t + "dw_o")
    dya, dyb, dyc, dgates, g["b_gate"] = _merge_bwd(dmerged, s["gates"], s["ya"], s["yb"], s["yc"], wl["b_gate"],
                                                    t + "merge")
    dya_in = _matmul(dya, wl["w_ssd_out"], "nt", F32, t + "d_ya_in")
    g["w_ssd_out"] = _matmul(s["ya_in"], dya, "tn", BF16, t + "dw_ssd_out")
    dyb_in = _matmul(dyb, wl["w_conf_out"], "nt", F32, t + "d_yb_in")
    g["w_conf_out"] = [_matmul(s["yb_in"], dyb, "tn", BF16, t + "dw_conf_out")]
    dyc_in = _matmul(dyc, wl["w_sc_out"], "nt", F32, t + "d_yc_in")
    g["w_sc_out"] = [_matmul(s["yc_in"], dyc, "tn", BF16, t + "dw_sc_out")]
    dy, dz, g["ssd_norm_g"] = _ssd_post_bwd(dya_in, s["y"], s["z"], wl["ssd_norm_g"], t + "ssd_post")
    dpre, ddt, da_log, dd = _ssd_scan_bwd(s["pre"], s["dt"], s["hprev"], dy, wl["a_log"], wl["ssd_d"],
                                          t + "ssd_scan")
    g["ssd_a_log"], g["ssd_d"] = da_log[:, :SSD_HEADS], dd[:, :SSD_HEADS]
    dxbc, ddt_raw, dcw, g["ssd_conv_b"], ddtb = _ssd_pre_bwd(dpre, s["xbc"], ddt, s["dt_raw"], wl["ssd_conv_w"],
                                                             wl["dt_bias"], t + "ssd_pre")
    g["ssd_conv_w"], g["ssd_dt_bias"] = dcw[:4], ddtb[:, :SSD_HEADS]
    dconf, dccw, g["conf_conv_b"], g["conf_ln_g"], g["conf_ln_b"] = _conf_bwd(
        dyb_in, s["uc"], s["conf"], wl["conf_conv_w"], wl["conf_ln_g"], wl["conf_ln_b"], t + "conf")
    g["conf_conv_w"] = dccw[:CONF_KERNEL]
    dsc, dscw = _sc_bwd(dyc_in, s["sc"], wl["sc_conv_w"], t + "sconv")
    g["sc_conv_w"] = dscw[:3]
    segs = (("z", dz, "w_z"), ("xbc", dxbc, "w_xbc"), ("dt", ddt_raw, "w_dt"), ("conf", dconf, "w_conf"),
            ("sc", dsc, "w_sc"), ("gates", dgates, "w_gates"))
    dh, dw_segs = None, []
    for nm, dseg, wname in segs:
        dh = _matmul(dseg, wl[wname], "nt", F32, t + "d_h_" + nm, add=dh)
        dw = _matmul(s["h"], dseg, "tn", BF16, t + "dw_in_" + nm)
        dw_segs.append(dw[:, :SSD_HEADS] if nm == "dt" else dw)
    g["w_in"] = dw_segs
    if prev is None:
        dx_in, dshift, dscale, g["norm_mix_g"] = _norm_bwd(dh, s["x_in"], dx_mid, wl["norm_mix_g"], s["scale_mix"],
                                                          t + "norm_mix")
        back = None
    else:
        dx_in, dshift, dscale, g["norm_mix_g"], dys_prev, dgate_prev = _norm_bwd(
            dh, s["x_in"], dx_mid, wl["norm_mix_g"], s["scale_mix"], t + "norm_mix", prev[0], prev[1])
        back = (dys_prev, dgate_prev)
    dmod_mix = jnp.concatenate([dshift, dscale, dgate_mix], axis=1)
    g["ada_mix_b"] = dmod_mix
    g["ada_mix_w"] = [_matmul(sc8, _pad_rows(dmod_mix, SUBLANES), "tn", BF16, t + "dw_ada_mix")]
    return g, dx_in, back


def _device_step(x, c, target, full):
    sc8 = _pad_rows(c * (1.0 / (1.0 + jnp.exp(-c))), SUBLANES)
    wls = [_layer_weights(full, i) for i in range(DEPTH)]
    saved, prev, xcur = [], None, x
    for i in range(DEPTH):
        s = _layer_fwd(i, xcur, prev, sc8, wls[i])
        saved.append(s)
        xcur, prev = s["x_mid"], (s["out"], s["gate_ffn"])
    gf = full["final_norm_g"].reshape(1, -1)
    last = saved[-1]
    loss, dx, dys, dgate, dgf = _final_loss(last["x_mid"], last["out"], last["gate_ffn"], gf, target, "final_loss")
    grads = [None] * DEPTH
    for i in reversed(range(DEPTH)):
        prev = None if i == 0 else (saved[i - 1]["out"], saved[i - 1]["gate_ffn"])
        grads[i], dx, back = _layer_bwd(i, saved[i], wls[i], sc8, dys, dx, dgate, prev)
        if back is not None:
            dys, dgate = back
    return loss[0, 0], dx, grads, dgf


def _step(x, c, target, weights, moments_m, moments_v):
    mat_names = [n for n, _ in MATRICES]
    sharded_names = [n for n, _ in SHARDED]
    local = [weights[n].astype(BF16) if n in mat_names else weights[n] for n in sharded_names]
    full = {n: weights[n] for n in REPLICATED}
    full.update(zip(sharded_names, _all_gather(local, "gather_weights")))
    loss, grad_x, grads, dgf = _device_step(x[0], c, target[0], full)
    parts = []
    for name, axis in SHARDED:
        per_layer = []
        for i in range(DEPTH):
            gw = grads[i][name]
            if axis == 1:
                per_layer.append(_row_shards(gw))
            else:
                per_layer.append(_col_shards(gw if isinstance(gw, list) else [gw], weights[name].shape[-1]))
        parts.append(jnp.stack(per_layer, axis=1).astype(BF16))
    kept, got = _sibling_swap(parts, "swap_grads")
    sums = [_pair_add(k, g, "pair_add_" + n) for n, k, g in zip(sharded_names, kept, got)]
    received = _chip_exchange(sums, "exchange_grads")
    big = {n: _reduce_adamw(r, weights[n], moments_m[n], moments_v[n], "adamw_" + n)
           for n, r in zip(sharded_names, received)}
    rep_grads = [dgf if n == "final_norm_g" else jnp.stack([grads[i][n].reshape(-1) for i in range(DEPTH)])
                 for n in REPLICATED]
    small_parts, = _all_gather([_pack_flat(rep_grads, LANES, SUBLANES, F32)], "gather_small_grads")
    pack_s = lambda d: _pack_flat([d[n] for n in REPLICATED], LANES, SUBLANES, F32)[None]
    small = _reduce_adamw(small_parts[:, None], pack_s(weights), pack_s(moments_m), pack_s(moments_v),
                          "adamw_replicated")
    small = [_unpack_flat(b, [weights[n].shape for n in REPLICATED]) for b in small]
    results = []
    for kind in range(4):
        by_name = {n: big[n][kind] for n in sharded_names}
        by_name.update(zip(REPLICATED, small[kind]))
        results.append([by_name[n] for n in WEIGHT_NAMES])
    loss = lax.psum(loss, ("x", "y", "c"))
    return (loss, grad_x[None], *results[0], *results[1], *results[2], *results[3])


def kernel(x, c, ada_mix_w, ada_mix_b, norm_mix_g, w_in, b_gate, ssd_conv_w, ssd_conv_b, ssd_dt_bias, ssd_a_log, ssd_d, ssd_norm_g, w_ssd_out, conf_conv_w, conf_conv_b, conf_ln_g, conf_ln_b, w_conf_out, sc_conv_w, w_sc_out, w_o, ada_ffn_w, ada_ffn_b, norm_ffn_g, w_up, ffn_conv_w, ffn_conv_b, w_down, final_norm_g, loss_target, m_ada_mix_w, m_ada_mix_b, m_norm_mix_g, m_w_in, m_b_gate, m_ssd_conv_w, m_ssd_conv_b, m_ssd_dt_bias, m_ssd_a_log, m_ssd_d, m_ssd_norm_g, m_w_ssd_out, m_conf_conv_w, m_conf_conv_b, m_conf_ln_g, m_conf_ln_b, m_w_conf_out, m_sc_conv_w, m_w_sc_out, m_w_o, m_ada_ffn_w, m_ada_ffn_b, m_norm_ffn_g, m_w_up, m_ffn_conv_w, m_ffn_conv_b, m_w_down, m_final_norm_g, v_ada_mix_w, v_ada_mix_b, v_norm_mix_g, v_w_in, v_b_gate, v_ssd_conv_w, v_ssd_conv_b, v_ssd_dt_bias, v_ssd_a_log, v_ssd_d, v_ssd_norm_g, v_w_ssd_out, v_conf_conv_w, v_conf_conv_b, v_conf_ln_g, v_conf_ln_b, v_w_conf_out, v_sc_conv_w, v_w_sc_out, v_w_o, v_ada_ffn_w, v_ada_ffn_b, v_norm_ffn_g, v_w_up, v_ffn_conv_w, v_ffn_conv_b, v_w_down, v_final_norm_g):
    given = dict(locals())
    weights = {n: given[n] for n in WEIGHT_NAMES}
    moments_m = {n: given["m_" + n] for n in WEIGHT_NAMES}
    moments_v = {n: given["v_" + n] for n in WEIGHT_NAMES}
    return _step(x, c, loss_target, weights, moments_m, moments_v)
```

```python
import functools

import jax
import jax.numpy as jnp
from jax import lax
from jax.experimental import pallas as pl
from jax.experimental.pallas import tpu as pltpu

F32 = jnp.float32
BF16 = jnp.bfloat16
HI = lax.Precision.HIGHEST
MESH = pl.DeviceIdType.MESH

N_DEV = 8
DEPTH = 2
D_MODEL = 1024
SSD_HEADS = 16
SSD_HEAD_DIM = 64
SSD_INNER = 1024
SSD_STATE = 64
SSD_CHUNK = 128
SSD_XBC = 1280
CONF_WIDTH = 512
CONF_KERNEL = 31
SC_WIDTH = 512
D_FF = 2816
EPS = 1e-6
OFF_Z, OFF_XBC, OFF_DT, OFF_CONF, OFF_SC, N_IN = 1024, 2304, 2320, 3344, 4880, 7952

ADAM_LR, ADAM_B1, ADAM_B2, ADAM_EPS, ADAM_WD, ADAM_STEP = 0.001, 0.9, 0.999, 1e-08, 0.01, 10

LANES = 128
SUBLANES = 8
VMEM_LIMIT = 56 * 1024 * 1024
ROW_TILE = 256

NN = (((1,), (0,)), ((), ()))
NT = (((1,), (1,)), ((), ()))
TN = (((0,), (0,)), ((), ()))


def _params(n_axes):
    return pltpu.CompilerParams(dimension_semantics=("arbitrary",) * n_axes, vmem_limit_bytes=VMEM_LIMIT)


def _pc(body, **kw):
    return pl.pallas_call(body, **kw)


def _dot(a, b, dn=NN, precision=None):
    return lax.dot_general(a, b, dn, precision=precision, preferred_element_type=F32)


def _sig(x):
    return 1.0 / (1.0 + jnp.exp(-x))


def _fold(v):
    r, c = v.shape
    return v.reshape(r // SUBLANES, SUBLANES, c).sum(axis=0)


def _tile(n_rows):
    return min(ROW_TILE, n_rows // 2)


def _row(tl, c, col=0):
    return pl.BlockSpec((tl, c), lambda i, col=col: (i, col))


def _prev(tl, hb, c, col=0):
    r = tl // hb
    return pl.BlockSpec((hb, c), lambda i, col=col: (jnp.maximum(i * r - 1, 0), col))


def _next(tl, hb, c, n_rows, col=0):
    r = tl // hb
    last = n_rows // hb - 1
    return pl.BlockSpec((hb, c), lambda i, col=col: (jnp.minimum((i + 1) * r, last), col))


def _const(shape):
    return pl.BlockSpec(shape, lambda i: (0,) * len(shape))


def _sds(shape, dtype=F32):
    return jax.ShapeDtypeStruct(shape, dtype)


def _pick(dim, target):
    if dim <= target:
        return dim
    best = None
    for t in range(LANES, target + 1, LANES):
        if dim % t == 0:
            best = t
    assert best is not None, (dim, target)
    return best


def _matmul(a, b, mode, out_dtype, name, add=None):
    if mode == "nn":
        (m, k), (k2, n) = a.shape, b.shape
    elif mode == "nt":
        (m, k), (n, k2) = a.shape, b.shape
    else:
        (k, m), (k2, n) = a.shape, b.shape
    assert k == k2, (a.shape, b.shape, mode)
    tm, tn, tk = _pick(m, 1024), _pick(n, 1536), _pick(k, 1024)
    nk = k // tk
    dn = {"nn": NN, "nt": NT, "tn": TN}[mode]

    def body(*refs):
        if add is None:
            a_ref, b_ref, o_ref, acc = refs
        else:
            a_ref, b_ref, c_ref, o_ref, acc = refs
        kk = pl.program_id(2)

        @pl.when(kk == 0)
        def _():
            acc[...] = jnp.zeros_like(acc)

        acc[...] += _dot(a_ref[...].astype(BF16), b_ref[...].astype(BF16), dn)

        @pl.when(kk == nk - 1)
        def _():
            r = acc[...]
            if add is not None:
                r = r + c_ref[...]
            o_ref[...] = r.astype(out_dtype)

    a_spec = {"nn": pl.BlockSpec((tm, tk), lambda i, j, kk: (i, kk)),
              "nt": pl.BlockSpec((tm, tk), lambda i, j, kk: (i, kk)),
              "tn": pl.BlockSpec((tk, tm), lambda i, j, kk: (kk, i))}[mode]
    b_spec = {"nn": pl.BlockSpec((tk, tn), lambda i, j, kk: (kk, j)),
              "nt": pl.BlockSpec((tn, tk), lambda i, j, kk: (j, kk)),
              "tn": pl.BlockSpec((tk, tn), lambda i, j, kk: (kk, j))}[mode]
    o_spec = pl.BlockSpec((tm, tn), lambda i, j, kk: (i, j))
    in_specs = [a_spec, b_spec] + ([o_spec] if add is not None else [])
    args = (a, b) + ((add,) if add is not None else ())
    return _pc(body, name=name, out_shape=_sds((m, n), out_dtype), grid=(m // tm, n // tn, nk),
               in_specs=in_specs, out_specs=o_spec, scratch_shapes=[pltpu.VMEM((tm, tn), F32)],
               compiler_params=_params(3))(*args)


def _norm_mod(x, g, scale, shift):
    r = lax.rsqrt(jnp.mean(x * x, axis=-1, keepdims=True) + EPS)
    return ((x * r) * g) * (1.0 + scale) + shift


def _prenorm_first(x, g, scale, shift, name):
    n, d = x.shape
    tl = _tile(n)

    def body(x_ref, g_ref, sc_ref, sh_ref, h_ref):
        h_ref[...] = _norm_mod(x_ref[...], g_ref[...], sc_ref[...], sh_ref[...]).astype(BF16)

    return _pc(body, name=name, out_shape=_sds((n, d), BF16), grid=(n // tl,),
               in_specs=[_row(tl, d)] + [_const((1, d))] * 3, out_specs=_row(tl, d),
               compiler_params=_params(1))(x, g, scale, shift)


def _prenorm_res(x, y, gate, g, scale, shift, name):
    n, d = x.shape
    tl = _tile(n)

    def body(x_ref, y_ref, gate_ref, g_ref, sc_ref, sh_ref, xo_ref, h_ref):
        xn = x_ref[...] + gate_ref[...] * y_ref[...]
        xo_ref[...] = xn
        h_ref[...] = _norm_mod(xn, g_ref[...], sc_ref[...], sh_ref[...]).astype(BF16)

    return _pc(body, name=name, out_shape=(_sds((n, d)), _sds((n, d), BF16)), grid=(n // tl,),
               in_specs=[_row(tl, d), _row(tl, d)] + [_const((1, d))] * 4,
               out_specs=(_row(tl, d), _row(tl, d)), compiler_params=_params(1))(x, y, gate, g, scale, shift)


def _final_loss(x, y, gate, gf, target, name):
    n, d = x.shape
    tl = _tile(n)
    nb = n // tl

    def body(x_ref, y_ref, gate_ref, gf_ref, t_ref, loss_ref, dx_ref, dys_ref, dgate_ref, dgf_ref,
             acc_l, acc_gate, acc_gf):
        i = pl.program_id(0)

        @pl.when(i == 0)
        def _():
            acc_l[...] = jnp.zeros_like(acc_l)
            acc_gate[...] = jnp.zeros_like(acc_gate)
            acc_gf[...] = jnp.zeros_like(acc_gf)

        yv = y_ref[...]
        gate = gate_ref[...]
        gf = gf_ref[...]
        x2 = x_ref[...] + gate * yv
        r = lax.rsqrt(jnp.mean(x2 * x2, axis=-1, keepdims=True) + EPS)
        xn = x2 * r
        e = xn * gf - t_ref[...]
        acc_l[...] += _fold(e * e)
        dy = e * (1.0 / d)
        acc_gf[...] += _fold(dy * xn)
        dxn = dy * gf
        dx = r * (dxn - xn * jnp.mean(dxn * xn, axis=-1, keepdims=True))
        dx_ref[...] = dx
        dys_ref[...] = (dx * gate).astype(BF16)
        acc_gate[...] += _fold(dx * yv)

        @pl.when(i == nb - 1)
        def _():
            loss_ref[...] = jnp.full((SUBLANES, LANES), 0.5 / d, F32) * jnp.sum(acc_l[...])
            dgate_ref[...] = jnp.sum(acc_gate[...], axis=0, keepdims=True)
            dgf_ref[...] = jnp.sum(acc_gf[...], axis=0, keepdims=True)

    return _pc(body, name=name,
               out_shape=(_sds((SUBLANES, LANES)), _sds((n, d)), _sds((n, d), BF16), _sds((1, d)), _sds((1, d))),
               grid=(nb,),
               in_specs=[_row(tl, d), _row(tl, d), _const((1, d)), _const((1, d)), _row(tl, d)],
               out_specs=(_const((SUBLANES, LANES)), _row(tl, d), _row(tl, d), _const((1, d)), _const((1, d))),
               scratch_shapes=[pltpu.VMEM((SUBLANES, d), F32)] * 3,
               compiler_params=_params(1))(x, y, gate, gf, target)


def _norm_bwd(dh, x, dxo, g, scale, name, y_prev=None, gate_prev=None):
    n, d = x.shape
    tl = _tile(n)
    nb = n // tl
    has_prev = y_prev is not None

    def body(*refs):
        if has_prev:
            (dh_ref, x_ref, dxo_ref, g_ref, sc_ref, yp_ref, gp_ref,
             dx_ref, dsh_ref, dsc_ref, dg_ref, dys_ref, dgp_ref, acc_sh, acc_s, acc_gp) = refs
        else:
            (dh_ref, x_ref, dxo_ref, g_ref, sc_ref,
             dx_ref, dsh_ref, dsc_ref, dg_ref, acc_sh, acc_s) = refs
        i = pl.program_id(0)

        @pl.when(i == 0)
        def _():
            acc_sh[...] = jnp.zeros_like(acc_sh)
            acc_s[...] = jnp.zeros_like(acc_s)
            if has_prev:
                acc_gp[...] = jnp.zeros_like(acc_gp)

        x_ = x_ref[...]
        dh_ = dh_ref[...]
        g_ = g_ref[...]
        one_sc = 1.0 + sc_ref[...]
        r = lax.rsqrt(jnp.mean(x_ * x_, axis=-1, keepdims=True) + EPS)
        xn = x_ * r
        dxn = dh_ * (g_ * one_sc)
        dx = dxo_ref[...] + r * (dxn - xn * jnp.mean(dxn * xn, axis=-1, keepdims=True))
        dx_ref[...] = dx
        acc_sh[...] += _fold(dh_)
        acc_s[...] += _fold(dh_ * xn)
        if has_prev:
            dys_ref[...] = (dx * gp_ref[...]).astype(BF16)
            acc_gp[...] += _fold(dx * yp_ref[...])

        @pl.when(i == nb - 1)
        def _():
            s = jnp.sum(acc_s[...], axis=0, keepdims=True)
            dsh_ref[...] = jnp.sum(acc_sh[...], axis=0, keepdims=True)
            dsc_ref[...] = s * g_
            dg_ref[...] = s * one_sc
            if has_prev:
                dgp_ref[...] = jnp.sum(acc_gp[...], axis=0, keepdims=True)

    vec = _sds((1, d))
    in_specs = [_row(tl, d)] * 3 + [_const((1, d))] * 2
    out_shape = [_sds((n, d)), vec, vec, vec]
    out_specs = [_row(tl, d)] + [_const((1, d))] * 3
    scratch = [pltpu.VMEM((SUBLANES, d), F32)] * 2
    args = [dh, x, dxo, g, scale]
    if has_prev:
        in_specs += [_row(tl, d), _const((1, d))]
        out_shape += [_sds((n, d), BF16), vec]
        out_specs += [_row(tl, d), _const((1, d))]
        scratch += [pltpu.VMEM((SUBLANES, d), F32)]
        args += [y_prev, gate_prev]
    return _pc(body, name=name, out_shape=tuple(out_shape), grid=(nb,), in_specs=in_specs,
               out_specs=tuple(out_specs), scratch_shapes=scratch, compiler_params=_params(1))(*args)


CONV_HALO = 8
CONF_HALO = 32


def _ssd_pre(xbc, dt_raw, conv_w, conv_b, dt_bias, name):
    n, c = xbc.shape
    tl = _tile(n)
    hb = CONV_HALO
    k_taps = 4

    def body(x_ref, xp_ref, dt_ref, w_ref, b_ref, dtb_ref, pre_ref, dts_ref, buf):
        i = pl.program_id(0)
        buf[pl.ds(0, hb), :] = jnp.where(i > 0, xp_ref[...], 0.0)
        buf[pl.ds(hb, tl), :] = x_ref[...]
        acc = b_ref[...] + w_ref[0:1, :] * buf[pl.ds(hb - 3, tl), :]
        for k in range(1, k_taps):
            acc = acc + w_ref[k:k + 1, :] * buf[pl.ds(hb - 3 + k, tl), :]
        pre_ref[...] = acc
        v = dt_ref[...] + dtb_ref[...]
        dts_ref[...] = jnp.maximum(v, 0.0) + jnp.log1p(jnp.exp(-jnp.abs(v)))

    return _pc(body, name=name, out_shape=(_sds((n, c)), _sds((n, LANES))), grid=(n // tl,),
               in_specs=[_row(tl, c), _prev(tl, hb, c), _row(tl, LANES), _const((SUBLANES, c)), _const((1, c)),
                         _const((1, LANES))],
               out_specs=(_row(tl, c), _row(tl, LANES)),
               scratch_shapes=[pltpu.VMEM((tl + hb, c), F32)], compiler_params=_params(1))(
        xbc, xbc, dt_raw, conv_w, conv_b, dt_bias)


def _ssd_pre_bwd(dpre, xbc, ddt, dt_raw, conv_w, dt_bias, name):
    n, c = xbc.shape
    tl = _tile(n)
    nb = n // tl
    hb = CONV_HALO
    k_taps = 4

    def body(dp_ref, dpn_ref, x_ref, xp_ref, ddt_ref, dt_ref, w_ref, dtb_ref,
             dx_ref, ddr_ref, dw_ref, db_ref, ddtb_ref, dbuf, xbuf, acc_w, acc_b, acc_dtb):
        i = pl.program_id(0)

        @pl.when(i == 0)
        def _():
            acc_w[...] = jnp.zeros_like(acc_w)
            acc_b[...] = jnp.zeros_like(acc_b)
            acc_dtb[...] = jnp.zeros_like(acc_dtb)

        dp = dp_ref[...]
        dbuf[pl.ds(0, tl), :] = dp
        dbuf[pl.ds(tl, hb), :] = jnp.where(i < nb - 1, dpn_ref[...], 0.0)
        xbuf[pl.ds(0, hb), :] = jnp.where(i > 0, xp_ref[...], 0.0)
        xbuf[pl.ds(hb, tl), :] = x_ref[...]
        dx = w_ref[0:1, :] * dbuf[pl.ds(3, tl), :]
        for k in range(1, k_taps):
            dx = dx + w_ref[k:k + 1, :] * dbuf[pl.ds(3 - k, tl), :]
        dx_ref[...] = dx.astype(BF16)
        for k in range(k_taps):
            acc_w[k] += _fold(dp * xbuf[pl.ds(hb - 3 + k, tl), :])
        acc_b[...] += _fold(dp)
        ddr = ddt_ref[...] * _sig(dt_ref[...] + dtb_ref[...])
        ddr_ref[...] = ddr.astype(BF16)
        acc_dtb[...] += _fold(ddr)

        @pl.when(i == nb - 1)
        def _():
            dw_ref[...] = jnp.zeros_like(dw_ref)
            for k in range(k_taps):
                dw_ref[k:k + 1, :] = jnp.sum(acc_w[k], axis=0, keepdims=True)
            db_ref[...] = jnp.sum(acc_b[...], axis=0, keepdims=True)
            ddtb_ref[...] = jnp.sum(acc_dtb[...], axis=0, keepdims=True)

    return _pc(body, name=name,
               out_shape=(_sds((n, c), BF16), _sds((n, LANES), BF16), _sds((SUBLANES, c)), _sds((1, c)),
                          _sds((1, LANES))),
               grid=(nb,),
               in_specs=[_row(tl, c), _next(tl, hb, c, n), _row(tl, c), _prev(tl, hb, c), _row(tl, LANES),
                         _row(tl, LANES), _const((SUBLANES, c)), _const((1, LANES))],
               out_specs=(_row(tl, c), _row(tl, LANES), _const((SUBLANES, c)), _const((1, c)), _const((1, LANES))),
               scratch_shapes=[pltpu.VMEM((tl + hb, c), F32), pltpu.VMEM((tl + hb, c), F32),
                               pltpu.VMEM((k_taps, SUBLANES, c), F32), pltpu.VMEM((SUBLANES, c), F32),
                               pltpu.VMEM((SUBLANES, LANES), F32)],
               compiler_params=_params(1))(dpre, dpre, xbc, xbc, ddt, dt_raw, conv_w, dt_bias)


def _expand_mat():
    r = lax.broadcasted_iota(jnp.int32, (LANES, SSD_INNER), 0)
    c = lax.broadcasted_iota(jnp.int32, (LANES, SSD_INNER), 1)
    return (jnp.right_shift(c, 6) == r).astype(F32)


def _reduce_mat():
    r = lax.broadcasted_iota(jnp.int32, (SSD_INNER, LANES), 0)
    c = lax.broadcasted_iota(jnp.int32, (SSD_INNER, LANES), 1)
    return (jnp.right_shift(r, 6) == c).astype(F32)


def _ssd_common(pre, dt, alog):
    q = SSD_CHUNK
    sg = _sig(pre)
    act = pre * sg
    lane = lax.broadcasted_iota(jnp.int32, (1, LANES), 1)
    a_neg = jnp.where(lane < SSD_HEADS, -jnp.exp(alog), 0.0)
    rr = lax.broadcasted_iota(jnp.int32, (q, q), 0)
    cc = lax.broadcasted_iota(jnp.int32, (q, q), 1)
    causal = rr >= cc
    cum = _dot(causal.astype(F32), dt * a_neg, precision=HI)
    e_mat = _expand_mat()
    dtx = _dot(dt, e_mat, precision=HI)
    cumx = _dot(cum, e_mat, precision=HI)
    return sg, act, a_neg, causal, cum, e_mat, dtx, cumx


def _ssd_scan(pre, dt, alog, dvec, name):
    n = pre.shape[0]
    q = SSD_CHUNK
    nc = n // q

    def body(pre_ref, dt_ref, alog_ref, d_ref, y_ref, hp_ref, state):
        i = pl.program_id(0)

        @pl.when(i == 0)
        def _():
            state[...] = jnp.zeros_like(state)

        dt_ = dt_ref[...]
        _, act, _, causal, cum, e_mat, dtx, cumx = _ssd_common(pre_ref[...], dt_, alog_ref[...])
        xs = act[:, :SSD_INNER]
        bm = act[:, SSD_INNER:SSD_INNER + LANES]
        cm = act[:, SSD_INNER + LANES:]
        cum_t = cum.T
        clx = cumx[q - 1:q, :]
        xc = xs * dtx
        xd = xc * jnp.exp(clx - cumx)
        doutx = jnp.exp(cumx)
        edec = jnp.exp(clx)
        dx_row = _dot(jnp.broadcast_to(d_ref[...], (SUBLANES, LANES)), e_mat, precision=HI)[0:1, :]
        hp_ref[0] = state[...]
        bb = bm.astype(BF16)
        cb = cm.astype(BF16)
        lane = lax.broadcasted_iota(jnp.int32, (1, LANES), 1)
        row = lax.broadcasted_iota(jnp.int32, (LANES, 1), 0)
        cbs = []
        for g in range(2):
            cg = jnp.where(jnp.right_shift(lane, 6) == g, cm, 0.0).astype(BF16)
            cbs.append(_dot(cg, bb, NT))
        for j in range(SSD_HEADS // 2):
            sl = slice(j * LANES, (j + 1) * LANES)
            g = j // 4
            xcj = xc[:, sl].astype(BF16)
            halves = []
            for half in range(2):
                h = 2 * j + half
                seg = cum[:, h:h + 1] - cum_t[h:h + 1, :]
                w = cbs[g] * jnp.exp(jnp.where(causal, seg, -jnp.inf))
                halves.append(_dot(w.astype(BF16), xcj))
            y_diag = jnp.where(lane < SSD_HEAD_DIM, halves[0], halves[1])
            hj = state[:, sl]
            y_off = doutx[:, sl] * _dot(cb, hj.astype(BF16))
            y_ref[:, sl] = y_diag + y_off + xs[:, sl] * dx_row[:, sl]
            st = _dot(bb, xd[:, sl].astype(BF16), TN)
            state[:, sl] = hj * edec[:, sl] + jnp.where(jnp.right_shift(row, 6) == g, st, 0.0)

    return _pc(body, name=name, out_shape=(_sds((n, SSD_INNER)), _sds((nc, LANES, SSD_INNER))), grid=(nc,),
               in_specs=[_row(q, SSD_XBC), _row(q, LANES), _const((1, LANES)), _const((1, LANES))],
               out_specs=(_row(q, SSD_INNER), pl.BlockSpec((1, LANES, SSD_INNER), lambda i: (i, 0, 0))),
               scratch_shapes=[pltpu.VMEM((LANES, SSD_INNER), F32)], compiler_params=_params(1))(pre, dt, alog, dvec)


def _ssd_scan_bwd(pre, dt, hprev, dy, alog, dvec, name):
    n = pre.shape[0]
    q = SSD_CHUNK
    nc = n // q

    def body(pre_ref, dt_ref, hp_ref, dy_ref, alog_ref, d_ref, dpre_ref, ddt_ref, da_ref, dd_ref,
             d_state, dxc_s, dcx_s, dcl_s, acc_a, acc_d):
        i = pl.program_id(0)

        @pl.when(i == 0)
        def _():
            d_state[...] = jnp.zeros_like(d_state)
            acc_a[...] = jnp.zeros_like(acc_a)
            acc_d[...] = jnp.zeros_like(acc_d)

        pre_ = pre_ref[...]
        dt_ = dt_ref[...]
        sg, act, a_neg, causal, cum, e_mat, dtx, cumx = _ssd_common(pre_, dt_, alog_ref[...])
        r_mat = _reduce_mat()
        xs = act[:, :SSD_INNER]
        bm = act[:, SSD_INNER:SSD_INNER + LANES]
        cm = act[:, SSD_INNER + LANES:]
        cum_t = cum.T
        clx = cumx[q - 1:q, :]
        xc = xs * dtx
        dsx = jnp.exp(clx - cumx)
        doutx = jnp.exp(cumx)
        edec = jnp.exp(clx)
        dx_row = _dot(jnp.broadcast_to(d_ref[...], (SUBLANES, LANES)), e_mat, precision=HI)[0:1, :]
        dy_ = dy_ref[...]
        acc_d[...] += _fold(dy_ * xs)
        bb = bm.astype(BF16)
        cb = cm.astype(BF16)
        lane = lax.broadcasted_iota(jnp.int32, (1, LANES), 1)
        row = lax.broadcasted_iota(jnp.int32, (LANES, 1), 0)
        d_c = jnp.zeros((q, LANES), F32)
        d_b = jnp.zeros((q, LANES), F32)

        for j in range(SSD_HEADS // 2):
            sl = slice(j * LANES, (j + 1) * LANES)
            g = j // 4
            hj = hp_ref[0, :, sl]
            hjb = hj.astype(BF16)
            dyj = dy_[:, sl]
            tj = _dot(cb, hjb)
            dtj = (doutx[:, sl] * dyj).astype(BF16)
            dcx = dyj * tj * doutx[:, sl]
            d_c = d_c + _dot(dtj, hjb, NT)
            dhn = d_state[:, sl]
            dhp = dhn * edec[:, sl] + jnp.where(jnp.right_shift(row, 6) == g, _dot(cb, dtj, TN), 0.0)
            dcl = jnp.sum(dhn * hj, axis=0, keepdims=True) * edec[:, sl]
            dsb = dhn.astype(BF16)
            dxd = _dot(bb, dsb)
            xcj = xc[:, sl]
            dsj = dsx[:, sl]
            d_b = d_b + _dot((xcj * dsj).astype(BF16), dsb, NT)
            dds = dxd * xcj * dsj
            d_state[:, sl] = dhp
            dxc_s[:, sl] = dxd * dsj
            dcx_s[:, sl] = dcx - dds
            dcl_s[:, sl] = jnp.broadcast_to(dcl + jnp.sum(dds, axis=0, keepdims=True), (SUBLANES, LANES))

        dcum_c = jnp.zeros((q, LANES), F32)
        dcum_t = jnp.zeros((LANES, q), F32)
        for g in range(2):
            gmask = jnp.right_shift(lane, 6) == g
            cg = jnp.where(gmask, cm, 0.0).astype(BF16)
            cbg = _dot(cg, bb, NT)
            d_cb = jnp.zeros((q, q), F32)
            for hh in range(SSD_HEADS // 2):
                h = g * (SSD_HEADS // 2) + hh
                j, half = h // 2, h % 2
                sl = slice(j * LANES, (j + 1) * LANES)
                hmask = jnp.right_shift(lane, 6) == half
                seg = cum[:, h:h + 1] - cum_t[h:h + 1, :]
                lm = jnp.exp(jnp.where(causal, seg, -jnp.inf))
                w = cbg * lm
                dyj = dy_[:, sl]
                dw = _dot(jnp.where(hmask, dyj, 0.0).astype(BF16), xc[:, sl].astype(BF16), NT)
                dxch = _dot(w.astype(BF16), dyj.astype(BF16), TN)
                dxc_s[:, sl] += jnp.where(hmask, dxch, 0.0)
                d_cb = d_cb + dw * lm
                m = dw * w
                dcum_c = dcum_c + jnp.sum(m, axis=1, keepdims=True) * (lane == h).astype(F32)
                dcum_t = dcum_t + (row == h).astype(F32) * jnp.sum(m, axis=0, keepdims=True)
            d_cbb = d_cb.astype(BF16)
            d_c = d_c + jnp.where(gmask, _dot(d_cbb, bb), 0.0)
            d_b = d_b + jnp.where(gmask, _dot(d_cbb, cb, TN), 0.0)

        dcl_row = _dot(dcl_s[...], r_mat, precision=HI)[0:1, :]
        rowq = lax.broadcasted_iota(jnp.int32, (q, 1), 0)
        dcum = (dcum_c - dcum_t.T + _dot(dcx_s[...], r_mat, precision=HI)
                + jnp.where(rowq == q - 1, dcl_row, 0.0))
        rr = lax.broadcasted_iota(jnp.int32, (q, q), 0)
        cc = lax.broadcasted_iota(jnp.int32, (q, q), 1)
        dadt = _dot((rr <= cc).astype(F32), dcum, precision=HI)
        dxc = dxc_s[...]
        ddt_ref[...] = dadt * a_neg + _dot(dxc * xs, r_mat, precision=HI)
        acc_a[...] += _fold(dadt * dt_)
        dsilu = sg * (1.0 + pre_ * (1.0 - sg))
        dpre_ref[:, :SSD_INNER] = (dxc * dtx + dy_ * dx_row) * dsilu[:, :SSD_INNER]
        dpre_ref[:, SSD_INNER:SSD_INNER + LANES] = d_b * dsilu[:, SSD_INNER:SSD_INNER + LANES]
        dpre_ref[:, SSD_INNER + LANES:] = d_c * dsilu[:, SSD_INNER + LANES:]

        @pl.when(i == nc - 1)
        def _():
            da_ref[...] = jnp.sum(acc_a[...], axis=0, keepdims=True) * a_neg
            dd_ref[...] = jnp.sum(_dot(acc_d[...], r_mat, precision=HI), axis=0, keepdims=True)

    rev = lambda i: (nc - 1 - i, 0)
    return _pc(body, name=name,
               out_shape=(_sds((n, SSD_XBC)), _sds((n, LANES)), _sds((1, LANES)), _sds((1, LANES))), grid=(nc,),
               in_specs=[pl.BlockSpec((q, SSD_XBC), rev), pl.BlockSpec((q, LANES), rev),
                         pl.BlockSpec((1, LANES, SSD_INNER), lambda i: (nc - 1 - i, 0, 0)),
                         pl.BlockSpec((q, SSD_INNER), rev), _const((1, LANES)), _const((1, LANES))],
               out_specs=(pl.BlockSpec((q, SSD_XBC), rev), pl.BlockSpec((q, LANES), rev), _const((1, LANES)),
                          _const((1, LANES))),
               scratch_shapes=[pltpu.VMEM((LANES, SSD_INNER), F32), pltpu.VMEM((q, SSD_INNER), F32),
                               pltpu.VMEM((q, SSD_INNER), F32), pltpu.VMEM((SUBLANES, SSD_INNER), F32),
                               pltpu.VMEM((SUBLANES, LANES), F32), pltpu.VMEM((SUBLANES, SSD_INNER), F32)],
               compiler_params=_params(1))(pre, dt, hprev, dy, alog, dvec)


def _group_norm_parts(v):
    half = SSD_INNER // 2
    r0 = lax.rsqrt(jnp.mean(v[:, :half] * v[:, :half], axis=-1, keepdims=True) + EPS)
    r1 = lax.rsqrt(jnp.mean(v[:, half:] * v[:, half:], axis=-1, keepdims=True) + EPS)
    lane = lax.broadcasted_iota(jnp.int32, (1, SSD_INNER), 1)
    return jnp.where(lane < half, r0, r1)


def _group_mean(v):
    half = SSD_INNER // 2
    m0 = jnp.mean(v[:, :half], axis=-1, keepdims=True)
    m1 = jnp.mean(v[:, half:], axis=-1, keepdims=True)
    lane = lax.broadcasted_iota(jnp.int32, (1, SSD_INNER), 1)
    return jnp.where(lane < half, m0, m1)


def _ssd_post(y, z, g, name):
    n, d = y.shape
    tl = _tile(n)

    def body(y_ref, z_ref, g_ref, o_ref):
        z_ = z_ref[...]
        v = y_ref[...] * (z_ * _sig(z_))
        o_ref[...] = ((v * _group_norm_parts(v)) * g_ref[...]).astype(BF16)

    return _pc(body, name=name, out_shape=_sds((n, d), BF16), grid=(n // tl,),
               in_specs=[_row(tl, d), _row(tl, d), _const((1, d))], out_specs=_row(tl, d),
               compiler_params=_params(1))(y, z, g)


def _ssd_post_bwd(dout, y, z, g, name):
    n, d = y.shape
    tl = _tile(n)
    nb = n // tl

    def body(do_ref, y_ref, z_ref, g_ref, dy_ref, dz_ref, dg_ref, acc_g):
        i = pl.program_id(0)

        @pl.when(i == 0)
        def _():
            acc_g[...] = jnp.zeros_like(acc_g)

        z_ = z_ref[...]
        y_ = y_ref[...]
        sz = _sig(z_)
        silu_z = z_ * sz
        v = y_ * silu_z
        rs = _group_norm_parts(v)
        nv = v * rs
        do_ = do_ref[...]
        acc_g[...] += _fold(do_ * nv)
        dn = do_ * g_ref[...]
        dv = rs * (dn - nv * _group_mean(dn * nv))
        dy_ref[...] = dv * silu_z
        dz_ref[...] = (dv * y_ * (sz * (1.0 + z_ * (1.0 - sz)))).astype(BF16)

        @pl.when(i == nb - 1)
        def _():
            dg_ref[...] = jnp.sum(acc_g[...], axis=0, keepdims=True)

    return _pc(body, name=name, out_shape=(_sds((n, d)), _sds((n, d), BF16), _sds((1, d))), grid=(nb,),
               in_specs=[_row(tl, d), _row(tl, d), _row(tl, d), _const((1, d))],
               out_specs=(_row(tl, d), _row(tl, d), _const((1, d))),
               scratch_shapes=[pltpu.VMEM((SUBLANES, d), F32)], compiler_params=_params(1))(dout, y, z, g)


def _layer_norm_parts(uc):
    mu = jnp.mean(uc, axis=-1, keepdims=True)
    xc = uc - mu
    rstd = lax.rsqrt(jnp.mean(xc * xc, axis=-1, keepdims=True) + EPS)
    return xc * rstd, rstd


def _conf_fwd(conf_in, conv_w, conv_b, ln_g, ln_b, name):
    n = conf_in.shape[0]
    c = CONF_WIDTH
    tl = _tile(n)
    hb = CONF_HALO
    k_taps = CONF_KERNEL

    def body(x_ref, xp_ref, w_ref, b_ref, g_ref, beta_ref, o_ref, uc_ref, buf):
        i = pl.program_id(0)
        xp = xp_ref[...]
        buf[pl.ds(0, hb), :] = jnp.where(i > 0, xp[:, :c] * _sig(xp[:, c:]), 0.0)
        x_ = x_ref[...]
        buf[pl.ds(hb, tl), :] = x_[:, :c] * _sig(x_[:, c:])
        acc = b_ref[...] + w_ref[0:1, :] * buf[pl.ds(hb - (k_taps - 1), tl), :]
        for k in range(1, k_taps):
            acc = acc + w_ref[k:k + 1, :] * buf[pl.ds(hb - (k_taps - 1) + k, tl), :]
        uc_ref[...] = acc
        nv, _ = _layer_norm_parts(acc)
        v = nv * g_ref[...] + beta_ref[...]
        o_ref[...] = (v * _sig(v)).astype(BF16)

    return _pc(body, name=name, out_shape=(_sds((n, c), BF16), _sds((n, c))), grid=(n // tl,),
               in_specs=[_row(tl, 2 * c), _prev(tl, hb, 2 * c), _const((hb, c)), _const((1, c)), _const((1, c)),
                         _const((1, c))],
               out_specs=(_row(tl, c), _row(tl, c)),
               scratch_shapes=[pltpu.VMEM((tl + hb, c), F32)], compiler_params=_params(1))(
        conf_in, conf_in, conv_w, conv_b, ln_g, ln_b)


def _conf_bwd(dout, uc, conf_in, conv_w, ln_g, ln_b, name):
    n = conf_in.shape[0]
    c = CONF_WIDTH
    tl = _tile(n)
    nb = n // tl
    hb = CONF_HALO
    k_taps = CONF_KERNEL

    def body(do_ref, don_ref, uc_ref, ucn_ref, x_ref, xp_ref, w_ref, g_ref, beta_ref,
             dx_ref, dw_ref, db_ref, dg_ref, dbeta_ref, dbuf, ubuf, acc_w, acc_b, acc_g, acc_beta):
        i = pl.program_id(0)

        @pl.when(i == 0)
        def _():
            acc_w[...] = jnp.zeros_like(acc_w)
            acc_b[...] = jnp.zeros_like(acc_b)
            acc_g[...] = jnp.zeros_like(acc_g)
            acc_beta[...] = jnp.zeros_like(acc_beta)

        g_ = g_ref[...]
        beta_ = beta_ref[...]

        def d_conv_out(do_, uc_):
            nv, rstd = _layer_norm_parts(uc_)
            v = nv * g_ + beta_
            sv = _sig(v)
            dv = do_ * (sv * (1.0 + v * (1.0 - sv)))
            dn = dv * g_
            duc = rstd * (dn - jnp.mean(dn, axis=-1, keepdims=True)
                          - nv * jnp.mean(dn * nv, axis=-1, keepdims=True))
            return duc, dv, nv

        duc, dv, nv = d_conv_out(do_ref[...], uc_ref[...])
        acc_g[...] += _fold(dv * nv)
        acc_beta[...] += _fold(dv)
        acc_b[...] += _fold(duc)
        dbuf[pl.ds(0, tl), :] = duc
        ducn, _, _ = d_conv_out(don_ref[...], ucn_ref[...])
        dbuf[pl.ds(tl, hb), :] = jnp.where(i < nb - 1, ducn, 0.0)
        xp = xp_ref[...]
        ubuf[pl.ds(0, hb), :] = jnp.where(i > 0, xp[:, :c] * _sig(xp[:, c:]), 0.0)
        x_ = x_ref[...]
        val = x_[:, :c]
        sgate = _sig(x_[:, c:])
        ubuf[pl.ds(hb, tl), :] = val * sgate
        du = w_ref[0:1, :] * dbuf[pl.ds(k_taps - 1, tl), :]
        for k in range(1, k_taps):
            du = du + w_ref[k:k + 1, :] * dbuf[pl.ds(k_taps - 1 - k, tl), :]
        for k in range(k_taps):
            acc_w[k] += _fold(duc * ubuf[pl.ds(hb - (k_taps - 1) + k, tl), :])
        dx_ref[:, :c] = (du * sgate).astype(BF16)
        dx_ref[:, c:] = (du * val * sgate * (1.0 - sgate)).astype(BF16)

        @pl.when(i == nb - 1)
        def _():
            dw_ref[...] = jnp.zeros_like(dw_ref)
            for k in range(k_taps):
                dw_ref[k:k + 1, :] = jnp.sum(acc_w[k], axis=0, keepdims=True)
            db_ref[...] = jnp.sum(acc_b[...], axis=0, keepdims=True)
            dg_ref[...] = jnp.sum(acc_g[...], axis=0, keepdims=True)
            dbeta_ref[...] = jnp.sum(acc_beta[...], axis=0, keepdims=True)

    vec = _sds((1, c))
    return _pc(body, name=name, out_shape=(_sds((n, 2 * c), BF16), _sds((hb, c)), vec, vec, vec), grid=(nb,),
               in_specs=[_row(tl, c), _next(tl, hb, c, n), _row(tl, c), _next(tl, hb, c, n), _row(tl, 2 * c),
                         _prev(tl, hb, 2 * c), _const((hb, c)), _const((1, c)), _const((1, c))],
               out_specs=(_row(tl, 2 * c), _const((hb, c)), _const((1, c)), _const((1, c)), _const((1, c))),
               scratch_shapes=[pltpu.VMEM((tl + hb, c), F32), pltpu.VMEM((tl + hb, c), F32),
                               pltpu.VMEM((k_taps, SUBLANES, c), F32), pltpu.VMEM((SUBLANES, c), F32),
                               pltpu.VMEM((SUBLANES, c), F32), pltpu.VMEM((SUBLANES, c), F32)],
               compiler_params=_params(1))(dout, dout, uc, uc, conf_in, conf_in, conv_w, ln_g, ln_b)


def _sc_fwd(sc_in, conv_w, name):
    n = sc_in.shape[0]
    c = SC_WIDTH
    tl = _tile(n)
    hb = CONV_HALO

    def body(x_ref, xp_ref, w_ref, o_ref, buf):
        i = pl.program_id(0)
        xp = xp_ref[...]
        buf[pl.ds(0, hb), :] = jnp.where(i > 0, xp[:, c:2 * c] * xp[:, 2 * c:], 0.0)
        x_ = x_ref[...]
        buf[pl.ds(hb, tl), :] = x_[:, c:2 * c] * x_[:, 2 * c:]
        cv = w_ref[0:1, :] * buf[pl.ds(hb - 2, tl), :]
        for k in range(1, 3):
            cv = cv + w_ref[k:k + 1, :] * buf[pl.ds(hb - 2 + k, tl), :]
        o_ref[...] = (x_[:, :c] * cv).astype(BF16)

    return _pc(body, name=name, out_shape=_sds((n, c), BF16), grid=(n // tl,),
               in_specs=[_row(tl, 3 * c), _prev(tl, hb, 3 * c), _const((SUBLANES, c))], out_specs=_row(tl, c),
               scratch_shapes=[pltpu.VMEM((tl + hb, c), F32)], compiler_params=_params(1))(sc_in, sc_in, conv_w)


def _sc_bwd(dout, sc_in, conv_w, name):
    n = sc_in.shape[0]
    c = SC_WIDTH
    tl = _tile(n)
    nb = n // tl
    hb = CONV_HALO

    def body(do_ref, don_ref, x_ref, xp_ref, xn_ref, w_ref, dx_ref, dw_ref, dbuf, pbuf, acc_w):
        i = pl.program_id(0)

        @pl.when(i == 0)
        def _():
            acc_w[...] = jnp.zeros_like(acc_w)

        x_ = x_ref[...]
        gb, gc, xv = x_[:, :c], x_[:, c:2 * c], x_[:, 2 * c:]
        do_ = do_ref[...]
        dcv = do_ * gb
        dbuf[pl.ds(0, tl), :] = dcv
        dbuf[pl.ds(tl, hb), :] = jnp.where(i < nb - 1, don_ref[...] * xn_ref[...], 0.0)
        xp = xp_ref[...]
        pbuf[pl.ds(0, hb), :] = jnp.where(i > 0, xp[:, c:2 * c] * xp[:, 2 * c:], 0.0)
        pbuf[pl.ds(hb, tl), :] = gc * xv
        cv = w_ref[0:1, :] * pbuf[pl.ds(hb - 2, tl), :]
        dp = w_ref[0:1, :] * dbuf[pl.ds(2, tl), :]
        for k in range(1, 3):
            cv = cv + w_ref[k:k + 1, :] * pbuf[pl.ds(hb - 2 + k, tl), :]
            dp = dp + w_ref[k:k + 1, :] * dbuf[pl.ds(2 - k, tl), :]
        for k in range(3):
            acc_w[k] += _fold(dcv * pbuf[pl.ds(hb - 2 + k, tl), :])
        dx_ref[:, :c] = (do_ * cv).astype(BF16)
        dx_ref[:, c:2 * c] = (dp * xv).astype(BF16)
        dx_ref[:, 2 * c:] = (dp * gc).astype(BF16)

        @pl.when(i == nb - 1)
        def _():
            dw_ref[...] = jnp.zeros_like(dw_ref)
            for k in range(3):
                dw_ref[k:k + 1, :] = jnp.sum(acc_w[k], axis=0, keepdims=True)

    return _pc(body, name=name, out_shape=(_sds((n, 3 * c), BF16), _sds((SUBLANES, c))), grid=(nb,),
               in_specs=[_row(tl, c), _next(tl, hb, c, n), _row(tl, 3 * c), _prev(tl, hb, 3 * c),
                         _next(tl, hb, c, n), _const((SUBLANES, c))],
               out_specs=(_row(tl, 3 * c), _const((SUBLANES, c))),
               scratch_shapes=[pltpu.VMEM((tl + hb, c), F32), pltpu.VMEM((tl + hb, c), F32),
                               pltpu.VMEM((3, SUBLANES, c), F32)],
               compiler_params=_params(1))(dout, dout, sc_in, sc_in, sc_in, conv_w)


def _merge_fwd(gates, ya, yb, yc, b_gate, name):
    n, d = ya.shape
    tl = _tile(n)

    def body(gt_ref, ya_ref, yb_ref, yc_ref, b_ref, o_ref):
        gt = _sig(gt_ref[...] + b_ref[...])
        o_ref[...] = (gt[:, :d] * ya_ref[...] + gt[:, d:2 * d] * yb_ref[...] + gt[:, 2 * d:] * yc_ref[...]).astype(BF16)

    return _pc(body, name=name, out_shape=_sds((n, d), BF16), grid=(n // tl,),
               in_specs=[_row(tl, 3 * d), _row(tl, d), _row(tl, d), _row(tl, d), _const((1, 3 * d))],
               out_specs=_row(tl, d), compiler_params=_params(1))(gates, ya, yb, yc, b_gate)


def _merge_bwd(dm, gates, ya, yb, yc, b_gate, name):
    n, d = ya.shape
    tl = _tile(n)
    nb = n // tl

    def body(dm_ref, gt_ref, ya_ref, yb_ref, yc_ref, b_ref, dya_ref, dyb_ref, dyc_ref, dgt_ref, db_ref, acc):
        i = pl.program_id(0)

        @pl.when(i == 0)
        def _():
            acc[...] = jnp.zeros_like(acc)

        dm_ = dm_ref[...]
        gt = _sig(gt_ref[...] + b_ref[...])
        for idx, (y_ref, dy_ref) in enumerate(((ya_ref, dya_ref), (yb_ref, dyb_ref), (yc_ref, dyc_ref))):
            gk = gt[:, idx * d:(idx + 1) * d]
            dy_ref[...] = (dm_ * gk).astype(BF16)
            dpre = dm_ * y_ref[...] * gk * (1.0 - gk)
            dgt_ref[:, idx * d:(idx + 1) * d] = dpre.astype(BF16)
            acc[:, idx * d:(idx + 1) * d] += _fold(dpre)

        @pl.when(i == nb - 1)
        def _():
            db_ref[...] = jnp.sum(acc[...], axis=0, keepdims=True)

    bf = _sds((n, d), BF16)
    return _pc(body, name=name, out_shape=(bf, bf, bf, _sds((n, 3 * d), BF16), _sds((1, 3 * d))), grid=(nb,),
               in_specs=[_row(tl, d), _row(tl, 3 * d), _row(tl, d), _row(tl, d), _row(tl, d), _const((1, 3 * d))],
               out_specs=(_row(tl, d), _row(tl, d), _row(tl, d), _row(tl, 3 * d), _const((1, 3 * d))),
               scratch_shapes=[pltpu.VMEM((SUBLANES, 3 * d), F32)], compiler_params=_params(1))(
        dm, gates, ya, yb, yc, b_gate)


FFN_COLS = 1408
FFN_STRIP = 64
FFN_STRIP_BWD = 32


def _down(prev, cur, s, row):
    return jnp.where(row < s, pltpu.roll(prev, s, 0), pltpu.roll(cur, s, 0))


def _up(cur, nxt, s, row):
    return jnp.where(row < SUBLANES - s, pltpu.roll(cur, SUBLANES - s, 0), pltpu.roll(nxt, SUBLANES - s, 0))


def _ffn_mid(up, conv_w, conv_b, name):
    n = up.shape[0]
    tl = _tile(n)
    hb = CONV_HALO
    tc = FFN_COLS
    ncb = D_FF // tc

    def spec(shape_rows, idx_fn, off):
        return pl.BlockSpec((shape_rows, tc), lambda j, i, off=off: (idx_fn(i), j + off))

    r = tl // hb
    cur = lambda i: i
    prv = lambda i: jnp.maximum(i * r - 1, 0)

    def body(g_ref, gp_ref, v_ref, vp_ref, wg_ref, wv_ref, bg_ref, bv_ref, o_ref):
        i = pl.program_id(1)
        row = lax.broadcasted_iota(jnp.int32, (SUBLANES, LANES), 0)
        full = lambda ref, k, ls: jnp.broadcast_to(ref[k:k + 1, ls], (SUBLANES, LANES))
        for lc in range(tc // LANES):
            ls = slice(lc * LANES, (lc + 1) * LANES)
            wg = [full(wg_ref, k, ls) for k in range(3)]
            wv = [full(wv_ref, k, ls) for k in range(3)]
            bg, bv = full(bg_ref, 0, ls), full(bv_ref, 0, ls)

            def conv(prev, x, w, b):
                return b + w[0] * _down(prev, x, 2, row) + w[1] * _down(prev, x, 1, row) + w[2] * x

            def strip(t, carry):
                gs, vs = [carry[0]], [carry[1]]
                for u in range(FFN_STRIP // SUBLANES):
                    rows = pl.ds(pl.multiple_of(t * FFN_STRIP + u * SUBLANES, SUBLANES), SUBLANES)
                    gs.append(g_ref[rows, ls])
                    vs.append(v_ref[rows, ls])
                outs = []
                for u in range(FFN_STRIP // SUBLANES):
                    ug = conv(gs[u], gs[u + 1], wg, bg)
                    outs.append(ug * _sig(ug) * conv(vs[u], vs[u + 1], wv, bv))
                for p in range(FFN_STRIP // 16):
                    rows = pl.ds(pl.multiple_of(t * FFN_STRIP + p * 16, 16), 16)
                    o_ref[rows, ls] = jnp.concatenate(outs[2 * p:2 * p + 2], axis=0).astype(BF16)
                return gs[-1], vs[-1]

            lax.fori_loop(0, tl // FFN_STRIP, strip,
                          (jnp.where(i > 0, gp_ref[:, ls], 0.0), jnp.where(i > 0, vp_ref[:, ls], 0.0)))

    wspec = lambda off: pl.BlockSpec((SUBLANES, tc), lambda j, i, off=off: (0, j + off))
    bspec = lambda off: pl.BlockSpec((1, tc), lambda j, i, off=off: (0, j + off))
    return _pc(body, name=name, out_shape=_sds((n, D_FF), BF16), grid=(ncb, n // tl),
               in_specs=[spec(tl, cur, 0), spec(hb, prv, 0), spec(tl, cur, ncb), spec(hb, prv, ncb),
                         wspec(0), wspec(ncb), bspec(0), bspec(ncb)],
               out_specs=pl.BlockSpec((tl, tc), lambda j, i: (i, j)), compiler_params=_params(2))(
        up, up, up, up, conv_w, conv_w, conv_b, conv_b)


def _ffn_mid_bwd(da, up, conv_w, conv_b, name):
    n = up.shape[0]
    tl = _tile(n)
    nb = n // tl
    nt = tl // FFN_STRIP_BWD
    hb = CONV_HALO
    tc = FFN_COLS
    ncb = D_FF // tc
    r = tl // hb
    last = n // hb - 1
    cur = lambda i: i
    prv = lambda i: jnp.maximum(i * r - 1, 0)
    nxt = lambda i: jnp.minimum((i + 1) * r, last)

    def spec(shape_rows, idx_fn, off):
        return pl.BlockSpec((shape_rows, tc), lambda j, i, off=off: (idx_fn(i), j + off))

    def body(da_ref, dan_ref, g_ref, gp_ref, gn_ref, v_ref, vp_ref, vn_ref, wg_ref, wv_ref, bg_ref, bv_ref,
             dg_ref, dv_ref, dwg_ref, dwv_ref, dbg_ref, dbv_ref, acc_w, acc_b):
        i = pl.program_id(1)

        @pl.when(i == 0)
        def _():
            acc_w[...] = jnp.zeros_like(acc_w)
            acc_b[...] = jnp.zeros_like(acc_b)

        row = lax.broadcasted_iota(jnp.int32, (SUBLANES, LANES), 0)
        full = lambda ref, k, ls: jnp.broadcast_to(ref[k:k + 1, ls], (SUBLANES, LANES))
        zero = jnp.zeros((SUBLANES, LANES), F32)

        def d_conv_out(da_, ug, uv):
            s = _sig(ug)
            return da_ * uv * (s * (1.0 + ug * (1.0 - s))), da_ * (ug * s)

        for lc in range(tc // LANES):
            ls = slice(lc * LANES, (lc + 1) * LANES)
            wg = [full(wg_ref, k, ls) for k in range(3)]
            wv = [full(wv_ref, k, ls) for k in range(3)]
            bg, bv = full(bg_ref, 0, ls), full(bv_ref, 0, ls)

            def unit(prev_g, g, prev_v, v, da_):
                g1, g2 = _down(prev_g, g, 1, row), _down(prev_g, g, 2, row)
                v1, v2 = _down(prev_v, v, 1, row), _down(prev_v, v, 2, row)
                ug = bg + wg[0] * g2 + wg[1] * g1 + wg[2] * g
                uv = bv + wv[0] * v2 + wv[1] * v1 + wv[2] * v
                dug, duv = d_conv_out(da_, ug, uv)
                return dug, duv, (g2, g1, g), (v2, v1, v)

            def d_in(d, d_next, w):
                return w[2] * d + w[1] * _up(d, d_next, 1, row) + w[0] * _up(d, d_next, 2, row)

            tail = pl.ds(tl - SUBLANES, SUBLANES)
            dgn, dvn, _, _ = unit(g_ref[tail, ls], gn_ref[:, ls], v_ref[tail, ls], vn_ref[:, ls], dan_ref[:, ls])
            dgn = jnp.where(i < nb - 1, dgn, 0.0)
            dvn = jnp.where(i < nb - 1, dvn, 0.0)
            gp0 = jnp.where(i > 0, gp_ref[:, ls], 0.0)
            vp0 = jnp.where(i > 0, vp_ref[:, ls], 0.0)

            def strip(tt, carry):
                dgn, dvn = carry[0], carry[1]
                aw, ab = list(carry[2:8]), list(carry[8:10])
                t = nt - 1 - tt
                nu = FFN_STRIP_BWD // SUBLANES
                rm = pl.multiple_of(jnp.maximum(t * FFN_STRIP_BWD - SUBLANES, 0), SUBLANES)
                gs = [jnp.where(t > 0, g_ref[pl.ds(rm, SUBLANES), ls], gp0)]
                vs = [jnp.where(t > 0, v_ref[pl.ds(rm, SUBLANES), ls], vp0)]
                das = []
                for u in range(nu):
                    rows = pl.ds(pl.multiple_of(t * FFN_STRIP_BWD + u * SUBLANES, SUBLANES), SUBLANES)
                    gs.append(g_ref[rows, ls])
                    vs.append(v_ref[rows, ls])
                    das.append(da_ref[rows, ls])
                dgs, dvs = [None] * nu + [dgn], [None] * nu + [dvn]
                for u in reversed(range(nu)):
                    dgs[u], dvs[u], gsh, vsh = unit(gs[u], gs[u + 1], vs[u], vs[u + 1], das[u])
                    for k in range(3):
                        aw[k] = aw[k] + dgs[u] * gsh[k]
                        aw[3 + k] = aw[3 + k] + dvs[u] * vsh[k]
                    ab[0] = ab[0] + dgs[u]
                    ab[1] = ab[1] + dvs[u]
                for p in range(nu // 2):
                    rows = pl.ds(pl.multiple_of(t * FFN_STRIP_BWD + p * 16, 16), 16)
                    dg_ref[rows, ls] = jnp.concatenate([d_in(dgs[2 * p], dgs[2 * p + 1], wg),
                                                        d_in(dgs[2 * p + 1], dgs[2 * p + 2], wg)], axis=0).astype(BF16)
                    dv_ref[rows, ls] = jnp.concatenate([d_in(dvs[2 * p], dvs[2 * p + 1], wv),
                                                        d_in(dvs[2 * p + 1], dvs[2 * p + 2], wv)], axis=0).astype(BF16)
                return (dgs[0], dvs[0], *aw, *ab)

            res = lax.fori_loop(0, nt, strip, (dgn, dvn) + (zero,) * 8)
            for k in range(3):
                acc_w[0, k, :, ls] += res[2 + k]
                acc_w[1, k, :, ls] += res[5 + k]
            acc_b[0, :, ls] += res[8]
            acc_b[1, :, ls] += res[9]

        @pl.when(i == nb - 1)
        def _():
            for t, (dw_ref, db_ref) in enumerate(((dwg_ref, dbg_ref), (dwv_ref, dbv_ref))):
                dw_ref[...] = jnp.zeros_like(dw_ref)
                for k in range(3):
                    dw_ref[k:k + 1, :] = jnp.sum(acc_w[t, k], axis=0, keepdims=True)
                db_ref[...] = jnp.sum(acc_b[t], axis=0, keepdims=True)

    wspec = lambda off: pl.BlockSpec((SUBLANES, tc), lambda j, i, off=off: (0, j + off))
    bspec = lambda off: pl.BlockSpec((1, tc), lambda j, i, off=off: (0, j + off))
    ospec = lambda off: pl.BlockSpec((tl, tc), lambda j, i, off=off: (i, j + off))
    dg, dv, dwg, dwv, dbg, dbv = _pc(
        body, name=name,
        out_shape=(_sds((n, D_FF), BF16), _sds((n, D_FF), BF16), _sds((SUBLANES, D_FF)), _sds((SUBLANES, D_FF)),
                   _sds((1, D_FF)), _sds((1, D_FF))),
        grid=(ncb, nb),
        in_specs=[spec(tl, cur, 0), spec(hb, nxt, 0),
                  spec(tl, cur, 0), spec(hb, prv, 0), spec(hb, nxt, 0),
                  spec(tl, cur, ncb), spec(hb, prv, ncb), spec(hb, nxt, ncb),
                  wspec(0), wspec(ncb), bspec(0), bspec(ncb)],
        out_specs=(ospec(0), ospec(0), wspec(0), wspec(0), bspec(0), bspec(0)),
        scratch_shapes=[pltpu.VMEM((2, 3, SUBLANES, tc), F32), pltpu.VMEM((2, SUBLANES, tc), F32)],
        compiler_params=_params(2))(da, da, up, up, up, up, up, up, conv_w, conv_w, conv_b, conv_b)
    return dg, dv, jnp.concatenate([dwg, dwv], axis=1), jnp.concatenate([dbg, dbv], axis=1)


def _position():
    return lax.axis_index("x"), lax.axis_index("y"), lax.axis_index("c")


def _all_gather(locals_, name):
    n = len(locals_)

    def body(*refs):
        x_refs, out_refs = refs[:n], refs[n:2 * n]
        send_sems, recv_sems, local_sems = refs[2 * n:]
        x, y, cc = _position()
        me, sibling = (x, y, cc), (x, y, 1 - cc)
        chips = [(1 - x, y), (x, 1 - y), (1 - x, 1 - y)]

        def slot(a, px, py, pc):
            return out_refs[a].at[4 * px + 2 * py + pc]

        def copy(k, a, block, to, own=False):
            return pltpu.make_async_remote_copy(
                src_ref=x_refs[a] if own else slot(a, *block), dst_ref=slot(a, *block),
                send_sem=send_sems.at[k, a], recv_sem=recv_sems.at[k, a], device_id=to, device_id_type=MESH)

        mine = [pltpu.make_async_copy(x_refs[a], slot(a, *me), local_sems.at[a]) for a in range(n)]
        first = [copy(1 + j, a, me, (*chip, cc), own=True) for j, chip in enumerate(chips) for a in range(n)]
        first += [copy(0, a, me, sibling, own=True) for a in range(n)]
        for cp in mine + first:
            cp.start()
        passed = []
        for j, chip in enumerate(chips):
            for a in range(n):
                copy(1 + j, a, (*chip, cc), me).wait_recv()
                cp = copy(4 + j, a, (*chip, cc), sibling)
                cp.start()
                passed.append(cp)
        for a in range(n):
            copy(0, a, sibling, me).wait_recv()
        for j, chip in enumerate(chips):
            for a in range(n):
                copy(4 + j, a, (*chip, 1 - cc), me).wait_recv()
        for cp in first + passed:
            cp.wait_send()
        for cp in mine:
            cp.wait()

    hbm = pl.BlockSpec(memory_space=pl.ANY)
    return _pc(body, name=name, out_shape=[_sds((N_DEV,) + a.shape, a.dtype) for a in locals_],
               in_specs=[hbm] * n, out_specs=[hbm] * n,
               scratch_shapes=[pltpu.SemaphoreType.DMA((7, n)), pltpu.SemaphoreType.DMA((7, n)),
                               pltpu.SemaphoreType.DMA((n,))])(*locals_)


N_CHIPS = 4


def _sibling_swap(parts, name):
    n = len(parts)

    def body(*refs):
        g_refs, got_refs = refs[:n], refs[n:2 * n]
        send_sems, recv_sems = refs[2 * n:]
        x, y, cc = _position()
        swaps = []
        for q in range(N_CHIPS):
            for a in range(n):
                swaps.append(pltpu.make_async_remote_copy(
                    src_ref=g_refs[a].at[2 * q + 1 - cc], dst_ref=got_refs[a].at[q], send_sem=send_sems.at[q, a],
                    recv_sem=recv_sems.at[q, a], device_id=(x, y, 1 - cc), device_id_type=MESH))
        for cp in swaps:
            cp.start()
        for cp in swaps:
            cp.wait_recv()
        for cp in swaps:
            cp.wait_send()

    hbm = pl.BlockSpec(memory_space=pl.ANY)
    return _pc(body, name=name, out_shape=[_sds((N_CHIPS,) + a.shape[1:], a.dtype) for a in parts],
               in_specs=[hbm] * n, out_specs=[hbm] * n,
               scratch_shapes=[pltpu.SemaphoreType.DMA((N_CHIPS, n)), pltpu.SemaphoreType.DMA((N_CHIPS, n))])(*parts)


def _pair_add(part, got, core, name):
    q, s, a, b = got.shape
    ta = _block_rows(a, b)

    def body(core_ref, k_ref, g_ref, o_ref):
        o_ref[...] = (k_ref[...].astype(F32) + g_ref[...].astype(F32)).astype(BF16)

    spec = pl.BlockSpec((None, None, ta, b), lambda c, l, i, core_ref: (c, l, i, 0))
    own = pl.BlockSpec((None, None, None, ta, b), lambda c, l, i, core_ref: (c, core_ref[0], l, i, 0))
    grid_spec = pltpu.PrefetchScalarGridSpec(num_scalar_prefetch=1, grid=(q, s, a // ta), in_specs=[own, spec],
                                             out_specs=spec)
    return _pc(body, name=name, out_shape=_sds(got.shape, BF16), grid_spec=grid_spec, compiler_params=_params(3))(
        core, part.reshape((N_CHIPS, 2) + part.shape[1:]), got)


def _chip_exchange(sums, name):
    n = len(sums)
    flips = [(1, 0), (0, 1), (1, 1)]

    def body(*refs):
        g_refs, out_refs = refs[:n], refs[n:2 * n]
        send_sems, recv_sems, local_sems = refs[2 * n:]
        x, y, cc = _position()
        me_q = 2 * x + y
        mine = [pltpu.make_async_copy(g_refs[a].at[me_q], out_refs[a].at[me_q], local_sems.at[a]) for a in range(n)]
        sends, recvs = [], []
        for k, (fx, fy) in enumerate(flips):
            px, py = (1 - x if fx else x), (1 - y if fy else y)
            peer_q = 2 * px + py
            for a in range(n):
                sends.append(pltpu.make_async_remote_copy(
                    src_ref=g_refs[a].at[peer_q], dst_ref=out_refs[a].at[me_q], send_sem=send_sems.at[k, a],
                    recv_sem=recv_sems.at[k, a], device_id=(px, py, cc), device_id_type=MESH))
                recvs.append(pltpu.make_async_remote_copy(
                    src_ref=g_refs[a].at[me_q], dst_ref=out_refs[a].at[peer_q], send_sem=send_sems.at[k, a],
                    recv_sem=recv_sems.at[k, a], device_id=(px, py, cc), device_id_type=MESH))
        for cp in sends + mine:
            cp.start()
        for cp in recvs:
            cp.wait_recv()
        for cp in sends:
            cp.wait_send()
        for cp in mine:
            cp.wait()

    hbm = pl.BlockSpec(memory_space=pl.ANY)
    return _pc(body, name=name, out_shape=[_sds(a.shape, a.dtype) for a in sums],
               in_specs=[hbm] * n, out_specs=[hbm] * n,
               scratch_shapes=[pltpu.SemaphoreType.DMA((3, n)), pltpu.SemaphoreType.DMA((3, n)),
                               pltpu.SemaphoreType.DMA((n,))])(*sums)


def _block_rows(a, b):
    ta = a
    while ta * b > 256 * 1024 and ta % 32 == 0:
        ta //= 2
    return ta


def _reduce_adamw(parts, w, m, v, name):
    n_parts, s, a, b = parts.shape
    ta = _block_rows(a, b)
    c1 = 1.0 - ADAM_B1 ** ADAM_STEP
    c2 = 1.0 - ADAM_B2 ** ADAM_STEP

    def body(p_ref, w_ref, m_ref, v_ref, g_out, d_out, m_out, v_out):
        g = p_ref[0].astype(F32)
        for j in range(1, n_parts):
            g = g + p_ref[j].astype(F32)
        m_new = ADAM_B1 * m_ref[...] + (1.0 - ADAM_B1) * g
        v_new = ADAM_B2 * v_ref[...] + (1.0 - ADAM_B2) * (g * g)
        m_hat = m_new / c1
        v_hat = v_new / c2
        g_out[...] = g
        d_out[...] = -ADAM_LR * (m_hat / (jnp.sqrt(v_hat) + ADAM_EPS) + ADAM_WD * w_ref[...])
        m_out[...] = m_new
        v_out[...] = v_new

    spec = pl.BlockSpec((None, ta, b), lambda l, i: (l, i, 0))
    return _pc(body, name=name, out_shape=(_sds((s, a, b)),) * 4, grid=(s, a // ta),
               in_specs=[pl.BlockSpec((n_parts, None, ta, b), lambda l, i: (0, l, i, 0)), spec, spec, spec],
               out_specs=(spec,) * 4, compiler_params=_params(2))(parts, w, m, v)


MATRICES = (("ada_mix_w", 2), ("w_in", 2), ("w_ssd_out", 1), ("w_conf_out", 2), ("w_sc_out", 2), ("w_o", 1),
            ("ada_ffn_w", 2), ("w_up", 2), ("w_down", 1))
CONV_WEIGHTS = (("ssd_conv_w", 2), ("conf_conv_w", 2), ("sc_conv_w", 2), ("ffn_conv_w", 2))
SHARDED = MATRICES + CONV_WEIGHTS
REPLICATED = ("ada_mix_b", "norm_mix_g", "b_gate", "ssd_conv_b", "ssd_dt_bias", "ssd_a_log", "ssd_d", "ssd_norm_g",
              "conf_conv_b", "conf_ln_g", "conf_ln_b", "ada_ffn_b", "norm_ffn_g", "ffn_conv_b", "final_norm_g")
WEIGHT_NAMES = ("ada_mix_w", "ada_mix_b", "norm_mix_g", "w_in", "b_gate", "ssd_conv_w", "ssd_conv_b", "ssd_dt_bias",
                "ssd_a_log", "ssd_d", "ssd_norm_g", "w_ssd_out", "conf_conv_w", "conf_conv_b", "conf_ln_g",
                "conf_ln_b", "w_conf_out", "sc_conv_w", "w_sc_out", "w_o", "ada_ffn_w", "ada_ffn_b", "norm_ffn_g",
                "w_up", "ffn_conv_w", "ffn_conv_b", "w_down", "final_norm_g")


def _pack_flat(arrays, cols, row_multiple, dtype):
    flat = jnp.concatenate([a.reshape(-1).astype(dtype) for a in arrays])
    rows = -(-flat.shape[0] // cols)
    rows = -(-rows // row_multiple) * row_multiple
    return jnp.pad(flat, (0, rows * cols - flat.shape[0])).reshape(rows, cols)


def _unpack_flat(flat2d, shapes):
    flat = flat2d.reshape(-1)
    out, off = [], 0
    for s in shapes:
        n = 1
        for d in s:
            n *= d
        out.append(flat[off:off + n].reshape(s))
        off += n
    return out


def _cols(g, i, lo, hi):
    b = g.shape[-1]
    pieces = []
    for k in range(N_DEV):
        a, e = max(lo, k * b), min(hi, (k + 1) * b)
        if a < e:
            pieces.append(g[k, i, :, a - k * b:e - k * b])
    return pieces[0] if len(pieces) == 1 else jnp.concatenate(pieces, axis=1)


def _rows(g, i):
    return g[:, i].reshape(N_DEV * g.shape[2], g.shape[3])


def _col_shards(segs, b):
    shards = []
    for k in range(N_DEV):
        lo, hi = k * b, (k + 1) * b
        pieces, off = [], 0
        for seg in segs:
            n = seg.shape[1]
            a, e = max(lo, off), min(hi, off + n)
            if a < e:
                pieces.append(seg[:, a - off:e - off])
            off += n
        shards.append(pieces[0] if len(pieces) == 1 else jnp.concatenate(pieces, axis=1))
    return jnp.stack(shards)


def _row_shards(full):
    return full.reshape(N_DEV, full.shape[0] // N_DEV, full.shape[1])


def _pad_rows(a, rows):
    return jnp.pad(a, ((0, rows - a.shape[0]), (0, 0)))


def _pad_lanes(a):
    return jnp.pad(a, ((0, 0), (0, LANES - a.shape[1])))


def _layer_weights(full, i):
    row = lambda name: full[name][i].reshape(1, -1)
    whole = lambda name: _cols(full[name], i, 0, N_DEV * full[name].shape[-1])
    w_in = full["w_in"]
    wl = {
        "ada_mix_w": whole("ada_mix_w"), "ada_mix_b": row("ada_mix_b"), "norm_mix_g": row("norm_mix_g"),
        "w_z": _cols(w_in, i, 0, OFF_Z), "w_xbc": _cols(w_in, i, OFF_Z, OFF_XBC),
        "w_dt": _pad_lanes(_cols(w_in, i, OFF_XBC, OFF_DT)), "w_conf": _cols(w_in, i, OFF_DT, OFF_CONF),
        "w_sc": _cols(w_in, i, OFF_CONF, OFF_SC), "w_gates": _cols(w_in, i, OFF_SC, N_IN),
        "b_gate": row("b_gate"),
        "ssd_conv_w": _pad_rows(whole("ssd_conv_w"), SUBLANES), "ssd_conv_b": row("ssd_conv_b"),
        "dt_bias": _pad_lanes(row("ssd_dt_bias")), "a_log": _pad_lanes(row("ssd_a_log")),
        "ssd_d": _pad_lanes(row("ssd_d")), "ssd_norm_g": row("ssd_norm_g"), "w_ssd_out": _rows(full["w_ssd_out"], i),
        "conf_conv_w": _pad_rows(whole("conf_conv_w"), CONF_HALO), "conf_conv_b": row("conf_conv_b"),
        "conf_ln_g": row("conf_ln_g"), "conf_ln_b": row("conf_ln_b"), "w_conf_out": whole("w_conf_out"),
        "sc_conv_w": _pad_rows(whole("sc_conv_w"), SUBLANES), "w_sc_out": whole("w_sc_out"),
        "w_o": _rows(full["w_o"], i),
        "ada_ffn_w": whole("ada_ffn_w"), "ada_ffn_b": row("ada_ffn_b"), "norm_ffn_g": row("norm_ffn_g"),
        "w_up": whole("w_up"), "ffn_conv_w": _pad_rows(whole("ffn_conv_w"), SUBLANES),
        "ffn_conv_b": row("ffn_conv_b"), "w_down": _rows(full["w_down"], i),
    }
    return wl


def _adaln(sc8, w, b, name):
    mod = _matmul(sc8, w, "nn", F32, name)[0:1, :] + b
    return mod[:, :D_MODEL], mod[:, D_MODEL:2 * D_MODEL], mod[:, 2 * D_MODEL:]


def _layer_fwd(i, x, prev, sc8, wl):
    t = f"l{i}_"
    s = {}
    shift, scale, gate = _adaln(sc8, wl["ada_mix_w"], wl["ada_mix_b"], t + "ada_mix")
    if prev is None:
        s["x_in"] = x
        s["h"] = _prenorm_first(x, wl["norm_mix_g"], scale, shift, t + "norm_mix")
    else:
        s["x_in"], s["h"] = _prenorm_res(x, prev[0], prev[1], wl["norm_mix_g"], scale, shift, t + "norm_mix")
    s["scale_mix"], s["gate_mix"] = scale, gate
    h = s["h"]
    s["z"] = _matmul(h, wl["w_z"], "nn", F32, t + "in_z")
    s["xbc"] = _matmul(h, wl["w_xbc"], "nn", F32, t + "in_xbc")
    s["dt_raw"] = _matmul(h, wl["w_dt"], "nn", F32, t + "in_dt")
    s["conf"] = _matmul(h, wl["w_conf"], "nn", F32, t + "in_conf")
    s["sc"] = _matmul(h, wl["w_sc"], "nn", F32, t + "in_sc")
    s["gates"] = _matmul(h, wl["w_gates"], "nn", F32, t + "in_gates")
    s["pre"], s["dt"] = _ssd_pre(s["xbc"], s["dt_raw"], wl["ssd_conv_w"], wl["ssd_conv_b"], wl["dt_bias"],
                                 t + "ssd_pre")
    s["y"], s["hprev"] = _ssd_scan(s["pre"], s["dt"], wl["a_log"], wl["ssd_d"], t + "ssd_scan")
    s["ya_in"] = _ssd_post(s["y"], s["z"], wl["ssd_norm_g"], t + "ssd_post")
    s["yb_in"], s["uc"] = _conf_fwd(s["conf"], wl["conf_conv_w"], wl["conf_conv_b"], wl["conf_ln_g"],
                                    wl["conf_ln_b"], t + "conf")
    s["yc_in"] = _sc_fwd(s["sc"], wl["sc_conv_w"], t + "sconv")
    s["ya"] = _matmul(s["ya_in"], wl["w_ssd_out"], "nn", F32, t + "ssd_out")
    s["yb"] = _matmul(s["yb_in"], wl["w_conf_out"], "nn", F32, t + "conf_out")
    s["yc"] = _matmul(s["yc_in"], wl["w_sc_out"], "nn", F32, t + "sc_out")
    s["merged"] = _merge_fwd(s["gates"], s["ya"], s["yb"], s["yc"], wl["b_gate"], t + "merge")
    s["mix"] = _matmul(s["merged"], wl["w_o"], "nn", F32, t + "w_o")
    shift2, scale2, gate2 = _adaln(sc8, wl["ada_ffn_w"], wl["ada_ffn_b"], t + "ada_ffn")
    s["x_mid"], s["h2"] = _prenorm_res(s["x_in"], s["mix"], gate, wl["norm_ffn_g"], scale2, shift2, t + "norm_ffn")
    s["scale_ffn"], s["gate_ffn"] = scale2, gate2
    s["up"] = _matmul(s["h2"], wl["w_up"], "nn", F32, t + "w_up")
    s["a"] = _ffn_mid(s["up"], wl["ffn_conv_w"], wl["ffn_conv_b"], t + "ffn_mid")
    s["out"] = _matmul(s["a"], wl["w_down"], "nn", F32, t + "w_down")
    return s


def _layer_bwd(i, s, wl, sc8, dys_ffn, dx_after, dgate_ffn, prev):
    t = f"l{i}_b_"
    g = {}
    da = _matmul(dys_ffn, wl["w_down"], "nt", F32, t + "d_a")
    g["w_down"] = _matmul(s["a"], dys_ffn, "tn", BF16, t + "dw_down")
    dug, duv, dfw, g["ffn_conv_b"] = _ffn_mid_bwd(da, s["up"], wl["ffn_conv_w"], wl["ffn_conv_b"], t + "ffn_mid")
    g["ffn_conv_w"] = dfw[:3]
    dh2 = _matmul(dug, wl["w_up"][:, :D_FF], "nt", F32, t + "d_h2_g")
    dh2 = _matmul(duv, wl["w_up"][:, D_FF:], "nt", F32, t + "d_h2_v", add=dh2)
    g["w_up"] = [_matmul(s["h2"], dug, "tn", BF16, t + "dw_up_g"), _matmul(s["h2"], duv, "tn", BF16, t + "dw_up_v")]
    dx_mid, dshift2, dscale2, g["norm_ffn_g"], dys_mix, dgate_mix = _norm_bwd(
        dh2, s["x_mid"], dx_after, wl["norm_ffn_g"], s["scale_ffn"], t + "norm_ffn", s["mix"], s["gate_mix"])
    dmod_ffn = jnp.concatenate([dshift2, dscale2, dgate_ffn], axis=1)
    g["ada_ffn_b"] = dmod_ffn
    g["ada_ffn_w"] = [_matmul(sc8, _pad_rows(dmod_ffn, SUBLANES), "tn", BF16, t + "dw_ada_ffn")]
    dmerged = _matmul(dys_mix, wl["w_o"], "nt", F32, t + "d_merged")
    g["w_o"] = _matmul(s["merged"], dys_mix, "tn", BF16, t + "dw_o")
    dya, dyb, dyc, dgates, g["b_gate"] = _merge_bwd(dmerged, s["gates"], s["ya"], s["yb"], s["yc"], wl["b_gate"],
                                                    t + "merge")
    dya_in = _matmul(dya, wl["w_ssd_out"], "nt", F32, t + "d_ya_in")
    g["w_ssd_out"] = _matmul(s["ya_in"], dya, "tn", BF16, t + "dw_ssd_out")
    dyb_in = _matmul(dyb, wl["w_conf_out"], "nt", F32, t + "d_yb_in")
    g["w_conf_out"] = [_matmul(s["yb_in"], dyb, "tn", BF16, t + "dw_conf_out")]
    dyc_in = _matmul(dyc, wl["w_sc_out"], "nt", F32, t + "d_yc_in")
    g["w_sc_out"] = [_matmul(s["yc_in"], dyc, "tn", BF16, t + "dw_sc_out")]
    dy, dz, g["ssd_norm_g"] = _ssd_post_bwd(dya_in, s["y"], s["z"], wl["ssd_norm_g"], t + "ssd_post")
    dpre, ddt, da_log, dd = _ssd_scan_bwd(s["pre"], s["dt"], s["hprev"], dy, wl["a_log"], wl["ssd_d"],
                                          t + "ssd_scan")
    g["ssd_a_log"], g["ssd_d"] = da_log[:, :SSD_HEADS], dd[:, :SSD_HEADS]
    dxbc, ddt_raw, dcw, g["ssd_conv_b"], ddtb = _ssd_pre_bwd(dpre, s["xbc"], ddt, s["dt_raw"], wl["ssd_conv_w"],
                                                             wl["dt_bias"], t + "ssd_pre")
    g["ssd_conv_w"], g["ssd_dt_bias"] = dcw[:4], ddtb[:, :SSD_HEADS]
    dconf, dccw, g["conf_conv_b"], g["conf_ln_g"], g["conf_ln_b"] = _conf_bwd(
        dyb_in, s["uc"], s["conf"], wl["conf_conv_w"], wl["conf_ln_g"], wl["conf_ln_b"], t + "conf")
    g["conf_conv_w"] = dccw[:CONF_KERNEL]
    dsc, dscw = _sc_bwd(dyc_in, s["sc"], wl["sc_conv_w"], t + "sconv")
    g["sc_conv_w"] = dscw[:3]
    segs = (("z", dz, "w_z"), ("xbc", dxbc, "w_xbc"), ("dt", ddt_raw, "w_dt"), ("conf", dconf, "w_conf"),
            ("sc", dsc, "w_sc"), ("gates", dgates, "w_gates"))
    dh, dw_segs = None, []
    for nm, dseg, wname in segs:
        dh = _matmul(dseg, wl[wname], "nt", F32, t + "d_h_" + nm, add=dh)
        dw = _matmul(s["h"], dseg, "tn", BF16, t + "dw_in_" + nm)
        dw_segs.append(dw[:, :SSD_HEADS] if nm == "dt" else dw)
    g["w_in"] = dw_segs
    if prev is None:
        dx_in, dshift, dscale, g["norm_mix_g"] = _norm_bwd(dh, s["x_in"], dx_mid, wl["norm_mix_g"], s["scale_mix"],
                                                          t + "norm_mix")
        back = None
    else:
        dx_in, dshift, dscale, g["norm_mix_g"], dys_prev, dgate_prev = _norm_bwd(
            dh, s["x_in"], dx_mid, wl["norm_mix_g"], s["scale_mix"], t + "norm_mix", prev[0], prev[1])
        back = (dys_prev, dgate_prev)
    dmod_mix = jnp.concatenate([dshift, dscale, dgate_mix], axis=1)
    g["ada_mix_b"] = dmod_mix
    g["ada_mix_w"] = [_matmul(sc8, _pad_rows(dmod_mix, SUBLANES), "tn", BF16, t + "dw_ada_mix")]
    return g, dx_in, back


def _device_step(x, c, target, full):
    sc8 = _pad_rows(c * (1.0 / (1.0 + jnp.exp(-c))), SUBLANES)
    wls = [_layer_weights(full, i) for i in range(DEPTH)]
    saved, prev, xcur = [], None, x
    for i in range(DEPTH):
        s = _layer_fwd(i, xcur, prev, sc8, wls[i])
        saved.append(s)
        xcur, prev = s["x_mid"], (s["out"], s["gate_ffn"])
    gf = full["final_norm_g"].reshape(1, -1)
    last = saved[-1]
    loss, dx, dys, dgate, dgf = _final_loss(last["x_mid"], last["out"], last["gate_ffn"], gf, target, "final_loss")
    grads = [None] * DEPTH
    for i in reversed(range(DEPTH)):
        prev = None if i == 0 else (saved[i - 1]["out"], saved[i - 1]["gate_ffn"])
        grads[i], dx, back = _layer_bwd(i, saved[i], wls[i], sc8, dys, dx, dgate, prev)
        if back is not None:
            dys, dgate = back
    return loss[0, 0], dx, grads, dgf


def _step(x, c, target, weights, moments_m, moments_v):
    mat_names = [n for n, _ in MATRICES]
    sharded_names = [n for n, _ in SHARDED]
    local = [weights[n].astype(BF16) if n in mat_names else weights[n] for n in sharded_names]
    full = {n: weights[n] for n in REPLICATED}
    full.update(zip(sharded_names, _all_gather(local, "gather_weights")))
    loss, grad_x, grads, dgf = _device_step(x[0], c, target[0], full)
    parts = []
    for name, axis in SHARDED:
        per_layer = []
        for i in range(DEPTH):
            gw = grads[i][name]
            if axis == 1:
                per_layer.append(_row_shards(gw))
            else:
                per_layer.append(_col_shards(gw if isinstance(gw, list) else [gw], weights[name].shape[-1]))
        parts.append(jnp.stack(per_layer, axis=1).astype(BF16))
    got = _sibling_swap(parts, "swap_grads")
    core = lax.axis_index("c").astype(jnp.int32).reshape(1)
    sums = [_pair_add(p, g, core, "pair_add_" + n) for n, p, g in zip(sharded_names, parts, got)]
    received = _chip_exchange(sums, "exchange_grads")
    big = {n: _reduce_adamw(r, weights[n], moments_m[n], moments_v[n], "adamw_" + n)
           for n, r in zip(sharded_names, received)}
    rep_grads = [dgf if n == "final_norm_g" else jnp.stack([grads[i][n].reshape(-1) for i in range(DEPTH)])
                 for n in REPLICATED]
    small_parts, = _all_gather([_pack_flat(rep_grads, LANES, SUBLANES, F32)], "gather_small_grads")
    pack_s = lambda d: _pack_flat([d[n] for n in REPLICATED], LANES, SUBLANES, F32)[None]
    small = _reduce_adamw(small_parts[:, None], pack_s(weights), pack_s(moments_m), pack_s(moments_v),
                          "adamw_replicated")
    small = [_unpack_flat(b, [weights[n].shape for n in REPLICATED]) for b in small]
    results = []
    for kind in range(4):
        by_name = {n: big[n][kind] for n in sharded_names}
        by_name.update(zip(REPLICATED, small[kind]))
        results.append([by_name[n] for n in WEIGHT_NAMES])
    loss = lax.psum(loss, ("x", "y", "c"))
    return (loss, grad_x[None], *results[0], *results[1], *results[2], *results[3])


def kernel(x, c, ada_mix_w, ada_mix_b, norm_mix_g, w_in, b_gate, ssd_conv_w, ssd_conv_b, ssd_dt_bias, ssd_a_log, ssd_d, ssd_norm_g, w_ssd_out, conf_conv_w, conf_conv_b, conf_ln_g, conf_ln_b, w_conf_out, sc_conv_w, w_sc_out, w_o, ada_ffn_w, ada_ffn_b, norm_ffn_g, w_up, ffn_conv_w, ffn_conv_b, w_down, final_norm_g, loss_target, m_ada_mix_w, m_ada_mix_b, m_norm_mix_g, m_w_in, m_b_gate, m_ssd_conv_w, m_ssd_conv_b, m_ssd_dt_bias, m_ssd_a_log, m_ssd_d, m_ssd_norm_g, m_w_ssd_out, m_conf_conv_w, m_conf_conv_b, m_conf_ln_g, m_conf_ln_b, m_w_conf_out, m_sc_conv_w, m_w_sc_out, m_w_o, m_ada_ffn_w, m_ada_ffn_b, m_norm_ffn_g, m_w_up, m_ffn_conv_w, m_ffn_conv_b, m_w_down, m_final_norm_g, v_ada_mix_w, v_ada_mix_b, v_norm_mix_g, v_w_in, v_b_gate, v_ssd_conv_w, v_ssd_conv_b, v_ssd_dt_bias, v_ssd_a_log, v_ssd_d, v_ssd_norm_g, v_w_ssd_out, v_conf_conv_w, v_conf_conv_b, v_conf_ln_g, v_conf_ln_b, v_w_conf_out, v_sc_conv_w, v_w_sc_out, v_w_o, v_ada_ffn_w, v_ada_ffn_b, v_norm_ffn_g, v_w_up, v_ffn_conv_w, v_ffn_conv_b, v_w_down, v_final_norm_g):
    given = dict(locals())
    weights = {n: given[n] for n in WEIGHT_NAMES}
    moments_m = {n: given["m_" + n] for n in WEIGHT_NAMES}
    moments_v = {n: given["v_" + n] for n in WEIGHT_NAMES}
    return _step(x, c, loss_target, weights, moments_m, moments_v)
```

```python
import functools

import jax
import jax.numpy as jnp
from jax import lax
from jax.experimental import pallas as pl
from jax.experimental.pallas import tpu as pltpu

F32 = jnp.float32
BF16 = jnp.bfloat16
MESH = pl.DeviceIdType.MESH

N_DEV = 8
DEPTH = 2
D_MODEL = 1024
SSD_HEADS = 16
SSD_HEAD_DIM = 64
SSD_INNER = 1024
SSD_STATE = 64
SSD_CHUNK = 128
SSD_XBC = 1280
CONF_WIDTH = 512
CONF_KERNEL = 31
SC_WIDTH = 512
D_FF = 2816
EPS = 1e-6
OFF_Z, OFF_XBC, OFF_DT, OFF_CONF, OFF_SC, N_IN = 1024, 2304, 2320, 3344, 4880, 7952

ADAM_LR, ADAM_B1, ADAM_B2, ADAM_EPS, ADAM_WD, ADAM_STEP = 0.001, 0.9, 0.999, 1e-08, 0.01, 10

LANES = 128
SUBLANES = 8
VMEM_LIMIT = 56 * 1024 * 1024
ROW_TILE = 256

NN = (((1,), (0,)), ((), ()))
NT = (((1,), (1,)), ((), ()))
TN = (((0,), (0,)), ((), ()))


def _params(n_axes):
    return pltpu.CompilerParams(dimension_semantics=("arbitrary",) * n_axes, vmem_limit_bytes=VMEM_LIMIT)


def _pc(body, **kw):
    return pl.pallas_call(body, **kw)


def _dot(a, b, dn=NN, precision=None):
    return lax.dot_general(a, b, dn, precision=precision, preferred_element_type=F32)


def _split3(x):
    hi = x.astype(BF16)
    r1 = x - hi.astype(F32)
    mid = r1.astype(BF16)
    return hi, mid, (r1 - mid.astype(F32)).astype(BF16)


def _dot_sel(x, sel):
    hi, mid, lo = _split3(x)
    return _dot(hi, sel) + _dot(mid, sel) + _dot(lo, sel)


def _sel_dot(sel, x):
    hi, mid, lo = _split3(x)
    return _dot(sel, hi) + _dot(sel, mid) + _dot(sel, lo)


def _sig(x):
    return 1.0 / (1.0 + jnp.exp(-x))


def _fold(v):
    r, c = v.shape
    return v.reshape(r // SUBLANES, SUBLANES, c).sum(axis=0)


def _tile(n_rows):
    return min(ROW_TILE, n_rows // 2)


def _row(tl, c, col=0):
    return pl.BlockSpec((tl, c), lambda i, col=col: (i, col))


def _prev(tl, hb, c, col=0):
    r = tl // hb
    return pl.BlockSpec((hb, c), lambda i, col=col: (jnp.maximum(i * r - 1, 0), col))


def _next(tl, hb, c, n_rows, col=0):
    r = tl // hb
    last = n_rows // hb - 1
    return pl.BlockSpec((hb, c), lambda i, col=col: (jnp.minimum((i + 1) * r, last), col))


def _const(shape):
    return pl.BlockSpec(shape, lambda i: (0,) * len(shape))


def _sds(shape, dtype=F32):
    return jax.ShapeDtypeStruct(shape, dtype)


MM_TILE = 1536
MM_FULL_K = 3072
MM_K_TILE = 1024


def _pick(dim, target):
    if dim <= target:
        return dim
    best = None
    for t in range(LANES, target + 1, LANES):
        if dim % t == 0:
            best = t
    assert best is not None, (dim, target)
    return best


def _matmul(a, b, mode, out_dtype, name):
    if mode == "nn":
        (m, k), (k2, n) = a.shape, b.shape
    elif mode == "nt":
        (m, k), (n, k2) = a.shape, b.shape
    else:
        (k, m), (k2, n) = a.shape, b.shape
    assert k == k2, (a.shape, b.shape, mode)
    tm, tn = _pick(m, MM_TILE), _pick(n, MM_TILE)
    tk = k if k <= MM_FULL_K else _pick(k, MM_K_TILE)
    nk = k // tk
    dn = {"nn": NN, "nt": NT, "tn": TN}[mode]

    def body_one(a_ref, b_ref, o_ref):
        o_ref[...] = _dot(a_ref[...].astype(BF16), b_ref[...].astype(BF16), dn).astype(out_dtype)

    def body_acc(a_ref, b_ref, o_ref, acc):
        kk = pl.program_id(2)

        @pl.when(kk == 0)
        def _():
            acc[...] = jnp.zeros_like(acc)

        acc[...] += _dot(a_ref[...].astype(BF16), b_ref[...].astype(BF16), dn)

        @pl.when(kk == nk - 1)
        def _():
            o_ref[...] = acc[...].astype(out_dtype)

    a_spec = {"nn": pl.BlockSpec((tm, tk), lambda i, j, kk: (i, kk)),
              "nt": pl.BlockSpec((tm, tk), lambda i, j, kk: (i, kk)),
              "tn": pl.BlockSpec((tk, tm), lambda i, j, kk: (kk, i))}[mode]
    b_spec = {"nn": pl.BlockSpec((tk, tn), lambda i, j, kk: (kk, j)),
              "nt": pl.BlockSpec((tn, tk), lambda i, j, kk: (j, kk)),
              "tn": pl.BlockSpec((tk, tn), lambda i, j, kk: (kk, j))}[mode]
    o_spec = pl.BlockSpec((tm, tn), lambda i, j, kk: (i, j))
    return _pc(body_one if nk == 1 else body_acc, name=name, out_shape=_sds((m, n), out_dtype),
               grid=(m // tm, n // tn, nk), in_specs=[a_spec, b_spec], out_specs=o_spec,
               scratch_shapes=[] if nk == 1 else [pltpu.VMEM((tm, tn), F32)], compiler_params=_params(3))(a, b)


SUM_NT_TILE = 512


def _matmul_sum_nt(a_list, b_list, name):
    m, n = a_list[0].shape[0], b_list[0].shape[0]
    cnt = len(a_list)
    tm, tn = _pick(m, SUM_NT_TILE), _pick(n, SUM_NT_TILE)

    def body(*refs):
        a_refs, b_refs, o_ref = refs[:cnt], refs[cnt:2 * cnt], refs[2 * cnt]
        acc = _dot(a_refs[0][...].astype(BF16), b_refs[0][...].astype(BF16), NT)
        for t in range(1, cnt):
            acc = acc + _dot(a_refs[t][...].astype(BF16), b_refs[t][...].astype(BF16), NT)
        o_ref[...] = acc

    in_specs = [pl.BlockSpec((tm, a.shape[1]), lambda j, i: (i, 0)) for a in a_list]
    in_specs += [pl.BlockSpec((tn, b.shape[1]), lambda j, i: (j, 0)) for b in b_list]
    return _pc(body, name=name, out_shape=_sds((m, n)), grid=(n // tn, m // tm), in_specs=in_specs,
               out_specs=pl.BlockSpec((tm, tn), lambda j, i: (i, j)), compiler_params=_params(2))(*a_list, *b_list)


def _norm_mod(x, g, scale, shift):
    r = lax.rsqrt(jnp.mean(x * x, axis=-1, keepdims=True) + EPS)
    return ((x * r) * g) * (1.0 + scale) + shift


def _prenorm_first(x, g, scale, shift, name):
    n, d = x.shape
    tl = _tile(n)

    def body(x_ref, g_ref, sc_ref, sh_ref, h_ref):
        h_ref[...] = _norm_mod(x_ref[...], g_ref[...], sc_ref[...], sh_ref[...]).astype(BF16)

    return _pc(body, name=name, out_shape=_sds((n, d), BF16), grid=(n // tl,),
               in_specs=[_row(tl, d)] + [_const((1, d))] * 3, out_specs=_row(tl, d),
               compiler_params=_params(1))(x, g, scale, shift)


def _prenorm_res(x, y, gate, g, scale, shift, name):
    n, d = x.shape
    tl = _tile(n)

    def body(x_ref, y_ref, gate_ref, g_ref, sc_ref, sh_ref, xo_ref, h_ref):
        xn = x_ref[...] + gate_ref[...] * y_ref[...]
        xo_ref[...] = xn
        h_ref[...] = _norm_mod(xn, g_ref[...], sc_ref[...], sh_ref[...]).astype(BF16)

    return _pc(body, name=name, out_shape=(_sds((n, d)), _sds((n, d), BF16)), grid=(n // tl,),
               in_specs=[_row(tl, d), _row(tl, d)] + [_const((1, d))] * 4,
               out_specs=(_row(tl, d), _row(tl, d)), compiler_params=_params(1))(x, y, gate, g, scale, shift)


def _final_loss(x, y, gate, gf, target, name):
    n, d = x.shape
    tl = _tile(n)
    nb = n // tl

    def body(x_ref, y_ref, gate_ref, gf_ref, t_ref, loss_ref, dx_ref, dys_ref, dgate_ref, dgf_ref,
             acc_l, acc_gate, acc_gf):
        i = pl.program_id(0)

        @pl.when(i == 0)
        def _():
            acc_l[...] = jnp.zeros_like(acc_l)
            acc_gate[...] = jnp.zeros_like(acc_gate)
            acc_gf[...] = jnp.zeros_like(acc_gf)

        yv = y_ref[...]
        gate = gate_ref[...]
        gf = gf_ref[...]
        x2 = x_ref[...] + gate * yv
        r = lax.rsqrt(jnp.mean(x2 * x2, axis=-1, keepdims=True) + EPS)
        xn = x2 * r
        e = xn * gf - t_ref[...]
        acc_l[...] += _fold(e * e)
        dy = e * (1.0 / d)
        acc_gf[...] += _fold(dy * xn)
        dxn = dy * gf
        dx = r * (dxn - xn * jnp.mean(dxn * xn, axis=-1, keepdims=True))
        dx_ref[...] = dx
        dys_ref[...] = (dx * gate).astype(BF16)
        acc_gate[...] += _fold(dx * yv)

        @pl.when(i == nb - 1)
        def _():
            loss_ref[...] = jnp.full((SUBLANES, LANES), 0.5 / d, F32) * jnp.sum(acc_l[...])
            dgate_ref[...] = jnp.sum(acc_gate[...], axis=0, keepdims=True)
            dgf_ref[...] = jnp.sum(acc_gf[...], axis=0, keepdims=True)

    return _pc(body, name=name,
               out_shape=(_sds((SUBLANES, LANES)), _sds((n, d)), _sds((n, d), BF16), _sds((1, d)), _sds((1, d))),
               grid=(nb,),
               in_specs=[_row(tl, d), _row(tl, d), _const((1, d)), _const((1, d)), _row(tl, d)],
               out_specs=(_const((SUBLANES, LANES)), _row(tl, d), _row(tl, d), _const((1, d)), _const((1, d))),
               scratch_shapes=[pltpu.VMEM((SUBLANES, d), F32)] * 3,
               compiler_params=_params(1))(x, y, gate, gf, target)


def _norm_bwd(dh, x, dxo, g, scale, name, y_prev=None, gate_prev=None):
    n, d = x.shape
    tl = _tile(n)
    nb = n // tl
    has_prev = y_prev is not None

    def body(*refs):
        if has_prev:
            (dh_ref, x_ref, dxo_ref, g_ref, sc_ref, yp_ref, gp_ref,
             dx_ref, dsh_ref, dsc_ref, dg_ref, dys_ref, dgp_ref, acc_sh, acc_s, acc_gp) = refs
        else:
            (dh_ref, x_ref, dxo_ref, g_ref, sc_ref,
             dx_ref, dsh_ref, dsc_ref, dg_ref, acc_sh, acc_s) = refs
        i = pl.program_id(0)

        @pl.when(i == 0)
        def _():
            acc_sh[...] = jnp.zeros_like(acc_sh)
            acc_s[...] = jnp.zeros_like(acc_s)
            if has_prev:
                acc_gp[...] = jnp.zeros_like(acc_gp)

        x_ = x_ref[...]
        dh_ = dh_ref[...]
        g_ = g_ref[...]
        one_sc = 1.0 + sc_ref[...]
        r = lax.rsqrt(jnp.mean(x_ * x_, axis=-1, keepdims=True) + EPS)
        xn = x_ * r
        dxn = dh_ * (g_ * one_sc)
        dx = dxo_ref[...] + r * (dxn - xn * jnp.mean(dxn * xn, axis=-1, keepdims=True))
        dx_ref[...] = dx
        acc_sh[...] += _fold(dh_)
        acc_s[...] += _fold(dh_ * xn)
        if has_prev:
            dys_ref[...] = (dx * gp_ref[...]).astype(BF16)
            acc_gp[...] += _fold(dx * yp_ref[...])

        @pl.when(i == nb - 1)
        def _():
            s = jnp.sum(acc_s[...], axis=0, keepdims=True)
            dsh_ref[...] = jnp.sum(acc_sh[...], axis=0, keepdims=True)
            dsc_ref[...] = s * g_
            dg_ref[...] = s * one_sc
            if has_prev:
                dgp_ref[...] = jnp.sum(acc_gp[...], axis=0, keepdims=True)

    vec = _sds((1, d))
    in_specs = [_row(tl, d)] * 3 + [_const((1, d))] * 2
    out_shape = [_sds((n, d)), vec, vec, vec]
    out_specs = [_row(tl, d)] + [_const((1, d))] * 3
    scratch = [pltpu.VMEM((SUBLANES, d), F32)] * 2
    args = [dh, x, dxo, g, scale]
    if has_prev:
        in_specs += [_row(tl, d), _const((1, d))]
        out_shape += [_sds((n, d), BF16), vec]
        out_specs += [_row(tl, d), _const((1, d))]
        scratch += [pltpu.VMEM((SUBLANES, d), F32)]
        args += [y_prev, gate_prev]
    return _pc(body, name=name, out_shape=tuple(out_shape), grid=(nb,), in_specs=in_specs,
               out_specs=tuple(out_specs), scratch_shapes=scratch, compiler_params=_params(1))(*args)


CONV_HALO = 8
CONF_HALO = 32


def _ssd_pre(xbc, dt_raw, conv_w, conv_b, dt_bias, name):
    n, c = xbc.shape
    tl = _tile(n)
    hb = CONV_HALO
    k_taps = 4

    def body(x_ref, xp_ref, dt_ref, w_ref, b_ref, dtb_ref, pre_ref, dts_ref, buf):
        i = pl.program_id(0)
        buf[pl.ds(0, hb), :] = jnp.where(i > 0, xp_ref[...], 0.0)
        buf[pl.ds(hb, tl), :] = x_ref[...]
        acc = b_ref[...] + w_ref[0:1, :] * buf[pl.ds(hb - 3, tl), :]
        for k in range(1, k_taps):
            acc = acc + w_ref[k:k + 1, :] * buf[pl.ds(hb - 3 + k, tl), :]
        pre_ref[...] = acc
        v = dt_ref[...] + dtb_ref[...]
        dts_ref[...] = jnp.maximum(v, 0.0) + jnp.log1p(jnp.exp(-jnp.abs(v)))

    return _pc(body, name=name, out_shape=(_sds((n, c)), _sds((n, LANES))), grid=(n // tl,),
               in_specs=[_row(tl, c), _prev(tl, hb, c), _row(tl, LANES), _const((SUBLANES, c)), _const((1, c)),
                         _const((1, LANES))],
               out_specs=(_row(tl, c), _row(tl, LANES)),
               scratch_shapes=[pltpu.VMEM((tl + hb, c), F32)], compiler_params=_params(1))(
        xbc, xbc, dt_raw, conv_w, conv_b, dt_bias)


def _ssd_pre_bwd(dpre, xbc, ddt, dt_raw, conv_w, dt_bias, name):
    n, c = xbc.shape
    tl = _tile(n)
    nb = n // tl
    hb = CONV_HALO
    k_taps = 4

    def body(dp_ref, dpn_ref, x_ref, xp_ref, ddt_ref, dt_ref, w_ref, dtb_ref,
             dx_ref, ddr_ref, dw_ref, db_ref, ddtb_ref, dbuf, xbuf, acc_w, acc_b, acc_dtb):
        i = pl.program_id(0)

        @pl.when(i == 0)
        def _():
            acc_w[...] = jnp.zeros_like(acc_w)
            acc_b[...] = jnp.zeros_like(acc_b)
            acc_dtb[...] = jnp.zeros_like(acc_dtb)

        dp = dp_ref[...]
        dbuf[pl.ds(0, tl), :] = dp
        dbuf[pl.ds(tl, hb), :] = jnp.where(i < nb - 1, dpn_ref[...], 0.0)
        xbuf[pl.ds(0, hb), :] = jnp.where(i > 0, xp_ref[...], 0.0)
        xbuf[pl.ds(hb, tl), :] = x_ref[...]
        dx = w_ref[0:1, :] * dbuf[pl.ds(3, tl), :]
        for k in range(1, k_taps):
            dx = dx + w_ref[k:k + 1, :] * dbuf[pl.ds(3 - k, tl), :]
        dx_ref[...] = dx.astype(BF16)
        for k in range(k_taps):
            acc_w[k] += _fold(dp * xbuf[pl.ds(hb - 3 + k, tl), :])
        acc_b[...] += _fold(dp)
        ddr = ddt_ref[...] * _sig(dt_ref[...] + dtb_ref[...])
        ddr_ref[...] = ddr.astype(BF16)
        acc_dtb[...] += _fold(ddr)

        @pl.when(i == nb - 1)
        def _():
            dw_ref[...] = jnp.zeros_like(dw_ref)
            for k in range(k_taps):
                dw_ref[k:k + 1, :] = jnp.sum(acc_w[k], axis=0, keepdims=True)
            db_ref[...] = jnp.sum(acc_b[...], axis=0, keepdims=True)
            ddtb_ref[...] = jnp.sum(acc_dtb[...], axis=0, keepdims=True)

    return _pc(body, name=name,
               out_shape=(_sds((n, c), BF16), _sds((n, LANES), BF16), _sds((SUBLANES, c)), _sds((1, c)),
                          _sds((1, LANES))),
               grid=(nb,),
               in_specs=[_row(tl, c), _next(tl, hb, c, n), _row(tl, c), _prev(tl, hb, c), _row(tl, LANES),
                         _row(tl, LANES), _const((SUBLANES, c)), _const((1, LANES))],
               out_specs=(_row(tl, c), _row(tl, LANES), _const((SUBLANES, c)), _const((1, c)), _const((1, LANES))),
               scratch_shapes=[pltpu.VMEM((tl + hb, c), F32), pltpu.VMEM((tl + hb, c), F32),
                               pltpu.VMEM((k_taps, SUBLANES, c), F32), pltpu.VMEM((SUBLANES, c), F32),
                               pltpu.VMEM((SUBLANES, LANES), F32)],
               compiler_params=_params(1))(dpre, dpre, xbc, xbc, ddt, dt_raw, conv_w, dt_bias)


def _expand_mat():
    r = lax.broadcasted_iota(jnp.int32, (LANES, SSD_INNER), 0)
    c = lax.broadcasted_iota(jnp.int32, (LANES, SSD_INNER), 1)
    return (jnp.right_shift(c, 6) == r).astype(BF16)


def _reduce_mat():
    r = lax.broadcasted_iota(jnp.int32, (SSD_INNER, LANES), 0)
    c = lax.broadcasted_iota(jnp.int32, (SSD_INNER, LANES), 1)
    return (jnp.right_shift(r, 6) == c).astype(BF16)


def _ssd_common(pre, dt, alog):
    q = SSD_CHUNK
    sg = _sig(pre)
    act = pre * sg
    lane = lax.broadcasted_iota(jnp.int32, (1, LANES), 1)
    a_neg = jnp.where(lane < SSD_HEADS, -jnp.exp(alog), 0.0)
    rr = lax.broadcasted_iota(jnp.int32, (q, q), 0)
    cc = lax.broadcasted_iota(jnp.int32, (q, q), 1)
    causal = rr >= cc
    cum = _sel_dot(causal.astype(BF16), dt * a_neg)
    e_mat = _expand_mat()
    dtx = _dot_sel(dt, e_mat)
    cumx = _dot_sel(cum, e_mat)
    return sg, act, a_neg, causal, cum, e_mat, dtx, cumx


def _ssd_scan(pre, dt, alog, dvec, name):
    n = pre.shape[0]
    q = SSD_CHUNK
    nc = n // q

    def body(pre_ref, dt_ref, alog_ref, d_ref, y_ref, hp_ref, state):
        i = pl.program_id(0)

        @pl.when(i == 0)
        def _():
            state[...] = jnp.zeros_like(state)

        dt_ = dt_ref[...]
        _, act, _, causal, cum, e_mat, dtx, cumx = _ssd_common(pre_ref[...], dt_, alog_ref[...])
        xs = act[:, :SSD_INNER]
        bm = act[:, SSD_INNER:SSD_INNER + LANES]
        cm = act[:, SSD_INNER + LANES:]
        cum_t = cum.T
        clx = cumx[q - 1:q, :]
        xc = xs * dtx
        xd = xc * jnp.exp(clx - cumx)
        doutx = jnp.exp(cumx)
        edec = jnp.exp(clx)
        dx_row = _dot_sel(jnp.broadcast_to(d_ref[...], (SUBLANES, LANES)), e_mat)[0:1, :]
        hp_ref[0] = state[...]
        bb = bm.astype(BF16)
        cb = cm.astype(BF16)
        lane = lax.broadcasted_iota(jnp.int32, (1, LANES), 1)
        row = lax.broadcasted_iota(jnp.int32, (LANES, 1), 0)
        cbs = []
        for g in range(2):
            cg = jnp.where(jnp.right_shift(lane, 6) == g, cm, 0.0).astype(BF16)
            cbs.append(_dot(cg, bb, NT))
        for j in range(SSD_HEADS // 2):
            sl = slice(j * LANES, (j + 1) * LANES)
            g = j // 4
            xcj = xc[:, sl].astype(BF16)
            halves = []
            for half in range(2):
                h = 2 * j + half
                seg = cum[:, h:h + 1] - cum_t[h:h + 1, :]
                w = cbs[g] * jnp.exp(jnp.where(causal, seg, -jnp.inf))
                halves.append(_dot(w.astype(BF16), xcj))
            y_diag = jnp.where(lane < SSD_HEAD_DIM, halves[0], halves[1])
            hj = state[:, sl]
            y_off = doutx[:, sl] * _dot(cb, hj.astype(BF16))
            y_ref[:, sl] = y_diag + y_off + xs[:, sl] * dx_row[:, sl]
            st = _dot(bb, xd[:, sl].astype(BF16), TN)
            state[:, sl] = hj * edec[:, sl] + jnp.where(jnp.right_shift(row, 6) == g, st, 0.0)

    return _pc(body, name=name, out_shape=(_sds((n, SSD_INNER)), _sds((nc, LANES, SSD_INNER))), grid=(nc,),
               in_specs=[_row(q, SSD_XBC), _row(q, LANES), _const((1, LANES)), _const((1, LANES))],
               out_specs=(_row(q, SSD_INNER), pl.BlockSpec((1, LANES, SSD_INNER), lambda i: (i, 0, 0))),
               scratch_shapes=[pltpu.VMEM((LANES, SSD_INNER), F32)], compiler_params=_params(1))(pre, dt, alog, dvec)


def _ssd_scan_bwd(pre, dt, hprev, dy, alog, dvec, name):
    n = pre.shape[0]
    q = SSD_CHUNK
    nc = n // q

    def body(pre_ref, dt_ref, hp_ref, dy_ref, alog_ref, d_ref, dpre_ref, ddt_ref, da_ref, dd_ref,
             d_state, dxc_s, dcx_s, dcl_s, acc_a, acc_d):
        i = pl.program_id(0)

        @pl.when(i == 0)
        def _():
            d_state[...] = jnp.zeros_like(d_state)
            acc_a[...] = jnp.zeros_like(acc_a)
            acc_d[...] = jnp.zeros_like(acc_d)

        pre_ = pre_ref[...]
        dt_ = dt_ref[...]
        sg, act, a_neg, causal, cum, e_mat, dtx, cumx = _ssd_common(pre_, dt_, alog_ref[...])
        r_mat = _reduce_mat()
        xs = act[:, :SSD_INNER]
        bm = act[:, SSD_INNER:SSD_INNER + LANES]
        cm = act[:, SSD_INNER + LANES:]
        cum_t = cum.T
        clx = cumx[q - 1:q, :]
        xc = xs * dtx
        dsx = jnp.exp(clx - cumx)
        doutx = jnp.exp(cumx)
        edec = jnp.exp(clx)
        dx_row = _dot_sel(jnp.broadcast_to(d_ref[...], (SUBLANES, LANES)), e_mat)[0:1, :]
        dy_ = dy_ref[...]
        acc_d[...] += _fold(dy_ * xs)
        bb = bm.astype(BF16)
        cb = cm.astype(BF16)
        lane = lax.broadcasted_iota(jnp.int32, (1, LANES), 1)
        row = lax.broadcasted_iota(jnp.int32, (LANES, 1), 0)
        d_c = jnp.zeros((q, LANES), F32)
        d_b = jnp.zeros((q, LANES), F32)

        for j in range(SSD_HEADS // 2):
            sl = slice(j * LANES, (j + 1) * LANES)
            g = j // 4
            hj = hp_ref[0, :, sl]
            hjb = hj.astype(BF16)
            dyj = dy_[:, sl]
            tj = _dot(cb, hjb)
            dtj = (doutx[:, sl] * dyj).astype(BF16)
            dcx = dyj * tj * doutx[:, sl]
            d_c = d_c + _dot(dtj, hjb, NT)
            dhn = d_state[:, sl]
            dhp = dhn * edec[:, sl] + jnp.where(jnp.right_shift(row, 6) == g, _dot(cb, dtj, TN), 0.0)
            dcl = jnp.sum(dhn * hj, axis=0, keepdims=True) * edec[:, sl]
            dsb = dhn.astype(BF16)
            dxd = _dot(bb, dsb)
            xcj = xc[:, sl]
            dsj = dsx[:, sl]
            d_b = d_b + _dot((xcj * dsj).astype(BF16), dsb, NT)
            dds = dxd * xcj * dsj
            d_state[:, sl] = dhp
            dxc_s[:, sl] = dxd * dsj
            dcx_s[:, sl] = dcx - dds
            dcl_s[:, sl] = jnp.broadcast_to(dcl + jnp.sum(dds, axis=0, keepdims=True), (SUBLANES, LANES))

        dcum_c = jnp.zeros((q, LANES), F32)
        dcum_t = jnp.zeros((LANES, q), F32)
        for g in range(2):
            gmask = jnp.right_shift(lane, 6) == g
            cg = jnp.where(gmask, cm, 0.0).astype(BF16)
            cbg = _dot(cg, bb, NT)
            d_cb = jnp.zeros((q, q), F32)
            for hh in range(SSD_HEADS // 2):
                h = g * (SSD_HEADS // 2) + hh
                j, half = h // 2, h % 2
                sl = slice(j * LANES, (j + 1) * LANES)
                hmask = jnp.right_shift(lane, 6) == half
                seg = cum[:, h:h + 1] - cum_t[h:h + 1, :]
                lm = jnp.exp(jnp.where(causal, seg, -jnp.inf))
                w = cbg * lm
                dyj = dy_[:, sl]
                dw = _dot(jnp.where(hmask, dyj, 0.0).astype(BF16), xc[:, sl].astype(BF16), NT)
                dxch = _dot(w.astype(BF16), dyj.astype(BF16), TN)
                dxc_s[:, sl] += jnp.where(hmask, dxch, 0.0)
                d_cb = d_cb + dw * lm
                m = dw * w
                dcum_c = dcum_c + jnp.sum(m, axis=1, keepdims=True) * (lane == h).astype(F32)
                dcum_t = dcum_t + (row == h).astype(F32) * jnp.sum(m, axis=0, keepdims=True)
            d_cbb = d_cb.astype(BF16)
            d_c = d_c + jnp.where(gmask, _dot(d_cbb, bb), 0.0)
            d_b = d_b + jnp.where(gmask, _dot(d_cbb, cb, TN), 0.0)

        dcl_row = _dot_sel(dcl_s[...], r_mat)[0:1, :]
        rowq = lax.broadcasted_iota(jnp.int32, (q, 1), 0)
        dcum = (dcum_c - dcum_t.T + _dot_sel(dcx_s[...], r_mat)
                + jnp.where(rowq == q - 1, dcl_row, 0.0))
        rr = lax.broadcasted_iota(jnp.int32, (q, q), 0)
        cc = lax.broadcasted_iota(jnp.int32, (q, q), 1)
        dadt = _sel_dot((rr <= cc).astype(BF16), dcum)
        dxc = dxc_s[...]
        ddt_ref[...] = dadt * a_neg + _dot_sel(dxc * xs, r_mat)
        acc_a[...] += _fold(dadt * dt_)
        dsilu = sg * (1.0 + pre_ * (1.0 - sg))
        dpre_ref[:, :SSD_INNER] = (dxc * dtx + dy_ * dx_row) * dsilu[:, :SSD_INNER]
        dpre_ref[:, SSD_INNER:SSD_INNER + LANES] = d_b * dsilu[:, SSD_INNER:SSD_INNER + LANES]
        dpre_ref[:, SSD_INNER + LANES:] = d_c * dsilu[:, SSD_INNER + LANES:]

        @pl.when(i == nc - 1)
        def _():
            da_ref[...] = jnp.sum(acc_a[...], axis=0, keepdims=True) * a_neg
            dd_ref[...] = jnp.sum(_dot_sel(acc_d[...], r_mat), axis=0, keepdims=True)

    rev = lambda i: (nc - 1 - i, 0)
    return _pc(body, name=name,
               out_shape=(_sds((n, SSD_XBC)), _sds((n, LANES)), _sds((1, LANES)), _sds((1, LANES))), grid=(nc,),
               in_specs=[pl.BlockSpec((q, SSD_XBC), rev), pl.BlockSpec((q, LANES), rev),
                         pl.BlockSpec((1, LANES, SSD_INNER), lambda i: (nc - 1 - i, 0, 0)),
                         pl.BlockSpec((q, SSD_INNER), rev), _const((1, LANES)), _const((1, LANES))],
               out_specs=(pl.BlockSpec((q, SSD_XBC), rev), pl.BlockSpec((q, LANES), rev), _const((1, LANES)),
                          _const((1, LANES))),
               scratch_shapes=[pltpu.VMEM((LANES, SSD_INNER), F32), pltpu.VMEM((q, SSD_INNER), F32),
                               pltpu.VMEM((q, SSD_INNER), F32), pltpu.VMEM((SUBLANES, SSD_INNER), F32),
                               pltpu.VMEM((SUBLANES, LANES), F32), pltpu.VMEM((SUBLANES, SSD_INNER), F32)],
               compiler_params=_params(1))(pre, dt, hprev, dy, alog, dvec)


def _group_norm_parts(v):
    half = SSD_INNER // 2
    r0 = lax.rsqrt(jnp.mean(v[:, :half] * v[:, :half], axis=-1, keepdims=True) + EPS)
    r1 = lax.rsqrt(jnp.mean(v[:, half:] * v[:, half:], axis=-1, keepdims=True) + EPS)
    lane = lax.broadcasted_iota(jnp.int32, (1, SSD_INNER), 1)
    return jnp.where(lane < half, r0, r1)


def _group_mean(v):
    half = SSD_INNER // 2
    m0 = jnp.mean(v[:, :half], axis=-1, keepdims=True)
    m1 = jnp.mean(v[:, half:], axis=-1, keepdims=True)
    lane = lax.broadcasted_iota(jnp.int32, (1, SSD_INNER), 1)
    return jnp.where(lane < half, m0, m1)


def _ssd_post(y, z, g, name):
    n, d = y.shape
    tl = _tile(n)

    def body(y_ref, z_ref, g_ref, o_ref):
        z_ = z_ref[...]
        v = y_ref[...] * (z_ * _sig(z_))
        o_ref[...] = ((v * _group_norm_parts(v)) * g_ref[...]).astype(BF16)

    return _pc(body, name=name, out_shape=_sds((n, d), BF16), grid=(n // tl,),
               in_specs=[_row(tl, d), _row(tl, d), _const((1, d))], out_specs=_row(tl, d),
               compiler_params=_params(1))(y, z, g)


def _ssd_post_bwd(dout, y, z, g, name):
    n, d = y.shape
    tl = _tile(n)
    nb = n // tl

    def body(do_ref, y_ref, z_ref, g_ref, dy_ref, dz_ref, dg_ref, acc_g):
        i = pl.program_id(0)

        @pl.when(i == 0)
        def _():
            acc_g[...] = jnp.zeros_like(acc_g)

        z_ = z_ref[...]
        y_ = y_ref[...]
        sz = _sig(z_)
        silu_z = z_ * sz
        v = y_ * silu_z
        rs = _group_norm_parts(v)
        nv = v * rs
        do_ = do_ref[...]
        acc_g[...] += _fold(do_ * nv)
        dn = do_ * g_ref[...]
        dv = rs * (dn - nv * _group_mean(dn * nv))
        dy_ref[...] = dv * silu_z
        dz_ref[...] = (dv * y_ * (sz * (1.0 + z_ * (1.0 - sz)))).astype(BF16)

        @pl.when(i == nb - 1)
        def _():
            dg_ref[...] = jnp.sum(acc_g[...], axis=0, keepdims=True)

    return _pc(body, name=name, out_shape=(_sds((n, d)), _sds((n, d), BF16), _sds((1, d))), grid=(nb,),
               in_specs=[_row(tl, d), _row(tl, d), _row(tl, d), _const((1, d))],
               out_specs=(_row(tl, d), _row(tl, d), _const((1, d))),
               scratch_shapes=[pltpu.VMEM((SUBLANES, d), F32)], compiler_params=_params(1))(dout, y, z, g)


def _layer_norm_parts(uc):
    mu = jnp.mean(uc, axis=-1, keepdims=True)
    xc = uc - mu
    rstd = lax.rsqrt(jnp.mean(xc * xc, axis=-1, keepdims=True) + EPS)
    return xc * rstd, rstd


def _conf_fwd(conf_in, conv_w, conv_b, ln_g, ln_b, name):
    n = conf_in.shape[0]
    c = CONF_WIDTH
    tl = _tile(n)
    hb = CONF_HALO
    k_taps = CONF_KERNEL

    def body(x_ref, xp_ref, w_ref, b_ref, g_ref, beta_ref, o_ref, uc_ref, buf):
        i = pl.program_id(0)
        xp = xp_ref[...]
        buf[pl.ds(0, hb), :] = jnp.where(i > 0, xp[:, :c] * _sig(xp[:, c:]), 0.0)
        x_ = x_ref[...]
        buf[pl.ds(hb, tl), :] = x_[:, :c] * _sig(x_[:, c:])
        acc = b_ref[...] + w_ref[0:1, :] * buf[pl.ds(hb - (k_taps - 1), tl), :]
        for k in range(1, k_taps):
            acc = acc + w_ref[k:k + 1, :] * buf[pl.ds(hb - (k_taps - 1) + k, tl), :]
        uc_ref[...] = acc
        nv, _ = _layer_norm_parts(acc)
        v = nv * g_ref[...] + beta_ref[...]
        o_ref[...] = (v * _sig(v)).astype(BF16)

    return _pc(body, name=name, out_shape=(_sds((n, c), BF16), _sds((n, c))), grid=(n // tl,),
               in_specs=[_row(tl, 2 * c), _prev(tl, hb, 2 * c), _const((hb, c)), _const((1, c)), _const((1, c)),
                         _const((1, c))],
               out_specs=(_row(tl, c), _row(tl, c)),
               scratch_shapes=[pltpu.VMEM((tl + hb, c), F32)], compiler_params=_params(1))(
        conf_in, conf_in, conv_w, conv_b, ln_g, ln_b)


def _conf_bwd(dout, uc, conf_in, conv_w, ln_g, ln_b, name):
    n = conf_in.shape[0]
    c = CONF_WIDTH
    tl = _tile(n)
    nb = n // tl
    hb = CONF_HALO
    k_taps = CONF_KERNEL

    def body(do_ref, don_ref, uc_ref, ucn_ref, x_ref, xp_ref, w_ref, g_ref, beta_ref,
             dx_ref, dw_ref, db_ref, dg_ref, dbeta_ref, dbuf, ubuf, acc_w, acc_b, acc_g, acc_beta):
        i = pl.program_id(0)

        @pl.when(i == 0)
        def _():
            acc_w[...] = jnp.zeros_like(acc_w)
            acc_b[...] = jnp.zeros_like(acc_b)
            acc_g[...] = jnp.zeros_like(acc_g)
            acc_beta[...] = jnp.zeros_like(acc_beta)

        g_ = g_ref[...]
        beta_ = beta_ref[...]

        def d_conv_out(do_, uc_):
            nv, rstd = _layer_norm_parts(uc_)
            v = nv * g_ + beta_
            sv = _sig(v)
            dv = do_ * (sv * (1.0 + v * (1.0 - sv)))
            dn = dv * g_
            duc = rstd * (dn - jnp.mean(dn, axis=-1, keepdims=True)
                          - nv * jnp.mean(dn * nv, axis=-1, keepdims=True))
            return duc, dv, nv

        duc, dv, nv = d_conv_out(do_ref[...], uc_ref[...])
        acc_g[...] += _fold(dv * nv)
        acc_beta[...] += _fold(dv)
        acc_b[...] += _fold(duc)
        dbuf[pl.ds(0, tl), :] = duc
        ducn, _, _ = d_conv_out(don_ref[...], ucn_ref[...])
        dbuf[pl.ds(tl, hb), :] = jnp.where(i < nb - 1, ducn, 0.0)
        xp = xp_ref[...]
        ubuf[pl.ds(0, hb), :] = jnp.where(i > 0, xp[:, :c] * _sig(xp[:, c:]), 0.0)
        x_ = x_ref[...]
        val = x_[:, :c]
        sgate = _sig(x_[:, c:])
        ubuf[pl.ds(hb, tl), :] = val * sgate
        du = w_ref[0:1, :] * dbuf[pl.ds(k_taps - 1, tl), :]
        for k in range(1, k_taps):
            du = du + w_ref[k:k + 1, :] * dbuf[pl.ds(k_taps - 1 - k, tl), :]
        for k in range(k_taps):
            acc_w[k] += _fold(duc * ubuf[pl.ds(hb - (k_taps - 1) + k, tl), :])
        dx_ref[:, :c] = (du * sgate).astype(BF16)
        dx_ref[:, c:] = (du * val * sgate * (1.0 - sgate)).astype(BF16)

        @pl.when(i == nb - 1)
        def _():
            dw_ref[...] = jnp.zeros_like(dw_ref)
            for k in range(k_taps):
                dw_ref[k:k + 1, :] = jnp.sum(acc_w[k], axis=0, keepdims=True)
            db_ref[...] = jnp.sum(acc_b[...], axis=0, keepdims=True)
            dg_ref[...] = jnp.sum(acc_g[...], axis=0, keepdims=True)
            dbeta_ref[...] = jnp.sum(acc_beta[...], axis=0, keepdims=True)

    vec = _sds((1, c))
    return _pc(body, name=name, out_shape=(_sds((n, 2 * c), BF16), _sds((hb, c)), vec, vec, vec), grid=(nb,),
               in_specs=[_row(tl, c), _next(tl, hb, c, n), _row(tl, c), _next(tl, hb, c, n), _row(tl, 2 * c),
                         _prev(tl, hb, 2 * c), _const((hb, c)), _const((1, c)), _const((1, c))],
               out_specs=(_row(tl, 2 * c), _const((hb, c)), _const((1, c)), _const((1, c)), _const((1, c))),
               scratch_shapes=[pltpu.VMEM((tl + hb, c), F32), pltpu.VMEM((tl + hb, c), F32),
                               pltpu.VMEM((k_taps, SUBLANES, c), F32), pltpu.VMEM((SUBLANES, c), F32),
                               pltpu.VMEM((SUBLANES, c), F32), pltpu.VMEM((SUBLANES, c), F32)],
               compiler_params=_params(1))(dout, dout, uc, uc, conf_in, conf_in, conv_w, ln_g, ln_b)


def _sc_fwd(sc_in, conv_w, name):
    n = sc_in.shape[0]
    c = SC_WIDTH
    tl = _tile(n)
    hb = CONV_HALO

    def body(x_ref, xp_ref, w_ref, o_ref, buf):
        i = pl.program_id(0)
        xp = xp_ref[...]
        buf[pl.ds(0, hb), :] = jnp.where(i > 0, xp[:, c:2 * c] * xp[:, 2 * c:], 0.0)
        x_ = x_ref[...]
        buf[pl.ds(hb, tl), :] = x_[:, c:2 * c] * x_[:, 2 * c:]
        cv = w_ref[0:1, :] * buf[pl.ds(hb - 2, tl), :]
        for k in range(1, 3):
            cv = cv + w_ref[k:k + 1, :] * buf[pl.ds(hb - 2 + k, tl), :]
        o_ref[...] = (x_[:, :c] * cv).astype(BF16)

    return _pc(body, name=name, out_shape=_sds((n, c), BF16), grid=(n // tl,),
               in_specs=[_row(tl, 3 * c), _prev(tl, hb, 3 * c), _const((SUBLANES, c))], out_specs=_row(tl, c),
               scratch_shapes=[pltpu.VMEM((tl + hb, c), F32)], compiler_params=_params(1))(sc_in, sc_in, conv_w)


def _sc_bwd(dout, sc_in, conv_w, name):
    n = sc_in.shape[0]
    c = SC_WIDTH
    tl = _tile(n)
    nb = n // tl
    hb = CONV_HALO

    def body(do_ref, don_ref, x_ref, xp_ref, xn_ref, w_ref, dx_ref, dw_ref, dbuf, pbuf, acc_w):
        i = pl.program_id(0)

        @pl.when(i == 0)
        def _():
            acc_w[...] = jnp.zeros_like(acc_w)

        x_ = x_ref[...]
        gb, gc, xv = x_[:, :c], x_[:, c:2 * c], x_[:, 2 * c:]
        do_ = do_ref[...]
        dcv = do_ * gb
        dbuf[pl.ds(0, tl), :] = dcv
        dbuf[pl.ds(tl, hb), :] = jnp.where(i < nb - 1, don_ref[...] * xn_ref[...], 0.0)
        xp = xp_ref[...]
        pbuf[pl.ds(0, hb), :] = jnp.where(i > 0, xp[:, c:2 * c] * xp[:, 2 * c:], 0.0)
        pbuf[pl.ds(hb, tl), :] = gc * xv
        cv = w_ref[0:1, :] * pbuf[pl.ds(hb - 2, tl), :]
        dp = w_ref[0:1, :] * dbuf[pl.ds(2, tl), :]
        for k in range(1, 3):
            cv = cv + w_ref[k:k + 1, :] * pbuf[pl.ds(hb - 2 + k, tl), :]
            dp = dp + w_ref[k:k + 1, :] * dbuf[pl.ds(2 - k, tl), :]
        for k in range(3):
            acc_w[k] += _fold(dcv * pbuf[pl.ds(hb - 2 + k, tl), :])
        dx_ref[:, :c] = (do_ * cv).astype(BF16)
        dx_ref[:, c:2 * c] = (dp * xv).astype(BF16)
        dx_ref[:, 2 * c:] = (dp * gc).astype(BF16)

        @pl.when(i == nb - 1)
        def _():
            dw_ref[...] = jnp.zeros_like(dw_ref)
            for k in range(3):
                dw_ref[k:k + 1, :] = jnp.sum(acc_w[k], axis=0, keepdims=True)

    return _pc(body, name=name, out_shape=(_sds((n, 3 * c), BF16), _sds((SUBLANES, c))), grid=(nb,),
               in_specs=[_row(tl, c), _next(tl, hb, c, n), _row(tl, 3 * c), _prev(tl, hb, 3 * c),
                         _next(tl, hb, c, n), _const((SUBLANES, c))],
               out_specs=(_row(tl, 3 * c), _const((SUBLANES, c))),
               scratch_shapes=[pltpu.VMEM((tl + hb, c), F32), pltpu.VMEM((tl + hb, c), F32),
                               pltpu.VMEM((3, SUBLANES, c), F32)],
               compiler_params=_params(1))(dout, dout, sc_in, sc_in, sc_in, conv_w)


def _merge_fwd(gates, ya, yb, yc, b_gate, name):
    n, d = ya.shape
    tl = _tile(n)

    def body(gt_ref, ya_ref, yb_ref, yc_ref, b_ref, o_ref):
        gt = _sig(gt_ref[...] + b_ref[...])
        o_ref[...] = (gt[:, :d] * ya_ref[...] + gt[:, d:2 * d] * yb_ref[...] + gt[:, 2 * d:] * yc_ref[...]).astype(BF16)

    return _pc(body, name=name, out_shape=_sds((n, d), BF16), grid=(n // tl,),
               in_specs=[_row(tl, 3 * d), _row(tl, d), _row(tl, d), _row(tl, d), _const((1, 3 * d))],
               out_specs=_row(tl, d), compiler_params=_params(1))(gates, ya, yb, yc, b_gate)


def _merge_bwd(dm, gates, ya, yb, yc, b_gate, name):
    n, d = ya.shape
    tl = _tile(n)
    nb = n // tl

    def body(dm_ref, gt_ref, ya_ref, yb_ref, yc_ref, b_ref, dya_ref, dyb_ref, dyc_ref, dgt_ref, db_ref, acc):
        i = pl.program_id(0)

        @pl.when(i == 0)
        def _():
            acc[...] = jnp.zeros_like(acc)

        dm_ = dm_ref[...]
        gt = _sig(gt_ref[...] + b_ref[...])
        for idx, (y_ref, dy_ref) in enumerate(((ya_ref, dya_ref), (yb_ref, dyb_ref), (yc_ref, dyc_ref))):
            gk = gt[:, idx * d:(idx + 1) * d]
            dy_ref[...] = (dm_ * gk).astype(BF16)
            dpre = dm_ * y_ref[...] * gk * (1.0 - gk)
            dgt_ref[:, idx * d:(idx + 1) * d] = dpre.astype(BF16)
            acc[:, idx * d:(idx + 1) * d] += _fold(dpre)

        @pl.when(i == nb - 1)
        def _():
            db_ref[...] = jnp.sum(acc[...], axis=0, keepdims=True)

    bf = _sds((n, d), BF16)
    return _pc(body, name=name, out_shape=(bf, bf, bf, _sds((n, 3 * d), BF16), _sds((1, 3 * d))), grid=(nb,),
               in_specs=[_row(tl, d), _row(tl, 3 * d), _row(tl, d), _row(tl, d), _row(tl, d), _const((1, 3 * d))],
               out_specs=(_row(tl, d), _row(tl, d), _row(tl, d), _row(tl, 3 * d), _const((1, 3 * d))),
               scratch_shapes=[pltpu.VMEM((SUBLANES, 3 * d), F32)], compiler_params=_params(1))(
        dm, gates, ya, yb, yc, b_gate)


FFN_COLS = 1408
FFN_STRIP = 64
FFN_STRIP_BWD = 32


def _down(prev, cur, s, row):
    return jnp.where(row < s, pltpu.roll(prev, s, 0), pltpu.roll(cur, s, 0))


def _up(cur, nxt, s, row):
    return jnp.where(row < SUBLANES - s, pltpu.roll(cur, SUBLANES - s, 0), pltpu.roll(nxt, SUBLANES - s, 0))


def _ffn_mid(up, conv_w, conv_b, name):
    n = up.shape[0]
    tl = _tile(n)
    hb = CONV_HALO
    tc = FFN_COLS
    ncb = D_FF // tc

    def spec(shape_rows, idx_fn, off):
        return pl.BlockSpec((shape_rows, tc), lambda j, i, off=off: (idx_fn(i), j + off))

    r = tl // hb
    cur = lambda i: i
    prv = lambda i: jnp.maximum(i * r - 1, 0)

    def body(g_ref, gp_ref, v_ref, vp_ref, wg_ref, wv_ref, bg_ref, bv_ref, o_ref):
        i = pl.program_id(1)
        row = lax.broadcasted_iota(jnp.int32, (SUBLANES, LANES), 0)
        full = lambda ref, k, ls: jnp.broadcast_to(ref[k:k + 1, ls], (SUBLANES, LANES))
        for lc in range(tc // LANES):
            ls = slice(lc * LANES, (lc + 1) * LANES)
            wg = [full(wg_ref, k, ls) for k in range(3)]
            wv = [full(wv_ref, k, ls) for k in range(3)]
            bg, bv = full(bg_ref, 0, ls), full(bv_ref, 0, ls)

            def conv(prev, x, w, b):
                return b + w[0] * _down(prev, x, 2, row) + w[1] * _down(prev, x, 1, row) + w[2] * x

            def strip(t, carry):
                gs, vs = [carry[0]], [carry[1]]
                for u in range(FFN_STRIP // SUBLANES):
                    rows = pl.ds(pl.multiple_of(t * FFN_STRIP + u * SUBLANES, SUBLANES), SUBLANES)
                    gs.append(g_ref[rows, ls])
                    vs.append(v_ref[rows, ls])
                outs = []
                for u in range(FFN_STRIP // SUBLANES):
                    ug = conv(gs[u], gs[u + 1], wg, bg)
                    outs.append(ug * _sig(ug) * conv(vs[u], vs[u + 1], wv, bv))
                for p in range(FFN_STRIP // 16):
                    rows = pl.ds(pl.multiple_of(t * FFN_STRIP + p * 16, 16), 16)
                    o_ref[rows, ls] = jnp.concatenate(outs[2 * p:2 * p + 2], axis=0).astype(BF16)
                return gs[-1], vs[-1]

            lax.fori_loop(0, tl // FFN_STRIP, strip,
                          (jnp.where(i > 0, gp_ref[:, ls], 0.0), jnp.where(i > 0, vp_ref[:, ls], 0.0)))

    wspec = lambda off: pl.BlockSpec((SUBLANES, tc), lambda j, i, off=off: (0, j + off))
    bspec = lambda off: pl.BlockSpec((1, tc), lambda j, i, off=off: (0, j + off))
    return _pc(body, name=name, out_shape=_sds((n, D_FF), BF16), grid=(ncb, n // tl),
               in_specs=[spec(tl, cur, 0), spec(hb, prv, 0), spec(tl, cur, ncb), spec(hb, prv, ncb),
                         wspec(0), wspec(ncb), bspec(0), bspec(ncb)],
               out_specs=pl.BlockSpec((tl, tc), lambda j, i: (i, j)), compiler_params=_params(2))(
        up, up, up, up, conv_w, conv_w, conv_b, conv_b)


def _ffn_mid_bwd(da, up, conv_w, conv_b, name):
    n = up.shape[0]
    tl = _tile(n)
    nb = n // tl
    nt = tl // FFN_STRIP_BWD
    hb = CONV_HALO
    tc = FFN_COLS
    ncb = D_FF // tc
    r = tl // hb
    last = n // hb - 1
    cur = lambda i: i
    prv = lambda i: jnp.maximum(i * r - 1, 0)
    nxt = lambda i: jnp.minimum((i + 1) * r, last)

    def spec(shape_rows, idx_fn, off):
        return pl.BlockSpec((shape_rows, tc), lambda j, i, off=off: (idx_fn(i), j + off))

    def body(da_ref, dan_ref, g_ref, gp_ref, gn_ref, v_ref, vp_ref, vn_ref, wg_ref, wv_ref, bg_ref, bv_ref,
             dg_ref, dv_ref, dwg_ref, dwv_ref, dbg_ref, dbv_ref, acc_w, acc_b):
        i = pl.program_id(1)

        @pl.when(i == 0)
        def _():
            acc_w[...] = jnp.zeros_like(acc_w)
            acc_b[...] = jnp.zeros_like(acc_b)

        row = lax.broadcasted_iota(jnp.int32, (SUBLANES, LANES), 0)
        full = lambda ref, k, ls: jnp.broadcast_to(ref[k:k + 1, ls], (SUBLANES, LANES))
        zero = jnp.zeros((SUBLANES, LANES), F32)

        def d_conv_out(da_, ug, uv):
            s = _sig(ug)
            return da_ * uv * (s * (1.0 + ug * (1.0 - s))), da_ * (ug * s)

        for lc in range(tc // LANES):
            ls = slice(lc * LANES, (lc + 1) * LANES)
            wg = [full(wg_ref, k, ls) for k in range(3)]
            wv = [full(wv_ref, k, ls) for k in range(3)]
            bg, bv = full(bg_ref, 0, ls), full(bv_ref, 0, ls)

            def unit(prev_g, g, prev_v, v, da_):
                g1, g2 = _down(prev_g, g, 1, row), _down(prev_g, g, 2, row)
                v1, v2 = _down(prev_v, v, 1, row), _down(prev_v, v, 2, row)
                ug = bg + wg[0] * g2 + wg[1] * g1 + wg[2] * g
                uv = bv + wv[0] * v2 + wv[1] * v1 + wv[2] * v
                dug, duv = d_conv_out(da_, ug, uv)
                return dug, duv, (g2, g1, g), (v2, v1, v)

            def d_in(d, d_next, w):
                return w[2] * d + w[1] * _up(d, d_next, 1, row) + w[0] * _up(d, d_next, 2, row)

            tail = pl.ds(tl - SUBLANES, SUBLANES)
            dgn, dvn, _, _ = unit(g_ref[tail, ls], gn_ref[:, ls], v_ref[tail, ls], vn_ref[:, ls], dan_ref[:, ls])
            dgn = jnp.where(i < nb - 1, dgn, 0.0)
            dvn = jnp.where(i < nb - 1, dvn, 0.0)
            gp0 = jnp.where(i > 0, gp_ref[:, ls], 0.0)
            vp0 = jnp.where(i > 0, vp_ref[:, ls], 0.0)

            def strip(tt, carry):
                dgn, dvn = carry[0], carry[1]
                aw, ab = list(carry[2:8]), list(carry[8:10])
                t = nt - 1 - tt
                nu = FFN_STRIP_BWD // SUBLANES
                rm = pl.multiple_of(jnp.maximum(t * FFN_STRIP_BWD - SUBLANES, 0), SUBLANES)
                gs = [jnp.where(t > 0, g_ref[pl.ds(rm, SUBLANES), ls], gp0)]
                vs = [jnp.where(t > 0, v_ref[pl.ds(rm, SUBLANES), ls], vp0)]
                das = []
                for u in range(nu):
                    rows = pl.ds(pl.multiple_of(t * FFN_STRIP_BWD + u * SUBLANES, SUBLANES), SUBLANES)
                    gs.append(g_ref[rows, ls])
                    vs.append(v_ref[rows, ls])
                    das.append(da_ref[rows, ls])
                dgs, dvs = [None] * nu + [dgn], [None] * nu + [dvn]
                for u in reversed(range(nu)):
                    dgs[u], dvs[u], gsh, vsh = unit(gs[u], gs[u + 1], vs[u], vs[u + 1], das[u])
                    for k in range(3):
                        aw[k] = aw[k] + dgs[u] * gsh[k]
                        aw[3 + k] = aw[3 + k] + dvs[u] * vsh[k]
                    ab[0] = ab[0] + dgs[u]
                    ab[1] = ab[1] + dvs[u]
                for p in range(nu // 2):
                    rows = pl.ds(pl.multiple_of(t * FFN_STRIP_BWD + p * 16, 16), 16)
                    dg_ref[rows, ls] = jnp.concatenate([d_in(dgs[2 * p], dgs[2 * p + 1], wg),
                                                        d_in(dgs[2 * p + 1], dgs[2 * p + 2], wg)], axis=0).astype(BF16)
                    dv_ref[rows, ls] = jnp.concatenate([d_in(dvs[2 * p], dvs[2 * p + 1], wv),
                                                        d_in(dvs[2 * p + 1], dvs[2 * p + 2], wv)], axis=0).astype(BF16)
                return (dgs[0], dvs[0], *aw, *ab)

            res = lax.fori_loop(0, nt, strip, (dgn, dvn) + (zero,) * 8)
            for k in range(3):
                acc_w[0, k, :, ls] += res[2 + k]
                acc_w[1, k, :, ls] += res[5 + k]
            acc_b[0, :, ls] += res[8]
            acc_b[1, :, ls] += res[9]

        @pl.when(i == nb - 1)
        def _():
            for t, (dw_ref, db_ref) in enumerate(((dwg_ref, dbg_ref), (dwv_ref, dbv_ref))):
                dw_ref[...] = jnp.zeros_like(dw_ref)
                for k in range(3):
                    dw_ref[k:k + 1, :] = jnp.sum(acc_w[t, k], axis=0, keepdims=True)
                db_ref[...] = jnp.sum(acc_b[t], axis=0, keepdims=True)

    wspec = lambda off: pl.BlockSpec((SUBLANES, tc), lambda j, i, off=off: (0, j + off))
    bspec = lambda off: pl.BlockSpec((1, tc), lambda j, i, off=off: (0, j + off))
    ospec = lambda off: pl.BlockSpec((tl, tc), lambda j, i, off=off: (i, j + off))
    dg, dv, dwg, dwv, dbg, dbv = _pc(
        body, name=name,
        out_shape=(_sds((n, D_FF), BF16), _sds((n, D_FF), BF16), _sds((SUBLANES, D_FF)), _sds((SUBLANES, D_FF)),
                   _sds((1, D_FF)), _sds((1, D_FF))),
        grid=(ncb, nb),
        in_specs=[spec(tl, cur, 0), spec(hb, nxt, 0),
                  spec(tl, cur, 0), spec(hb, prv, 0), spec(hb, nxt, 0),
                  spec(tl, cur, ncb), spec(hb, prv, ncb), spec(hb, nxt, ncb),
                  wspec(0), wspec(ncb), bspec(0), bspec(ncb)],
        out_specs=(ospec(0), ospec(0), wspec(0), wspec(0), bspec(0), bspec(0)),
        scratch_shapes=[pltpu.VMEM((2, 3, SUBLANES, tc), F32), pltpu.VMEM((2, SUBLANES, tc), F32)],
        compiler_params=_params(2))(da, da, up, up, up, up, up, up, conv_w, conv_w, conv_b, conv_b)
    return dg, dv, jnp.concatenate([dwg, dwv], axis=1), jnp.concatenate([dbg, dbv], axis=1)


def _position():
    return lax.axis_index("x"), lax.axis_index("y"), lax.axis_index("c")


def _all_gather(locals_, name):
    n = len(locals_)

    def body(*refs):
        x_refs, out_refs = refs[:n], refs[n:2 * n]
        send_sems, recv_sems, local_sems = refs[2 * n:]
        x, y, cc = _position()
        me, sibling = (x, y, cc), (x, y, 1 - cc)
        chips = [(1 - x, y), (x, 1 - y), (1 - x, 1 - y)]

        def slot(a, px, py, pc):
            return out_refs[a].at[4 * px + 2 * py + pc]

        def copy(k, a, block, to, own=False):
            return pltpu.make_async_remote_copy(
                src_ref=x_refs[a] if own else slot(a, *block), dst_ref=slot(a, *block),
                send_sem=send_sems.at[k, a], recv_sem=recv_sems.at[k, a], device_id=to, device_id_type=MESH)

        mine = [pltpu.make_async_copy(x_refs[a], slot(a, *me), local_sems.at[a]) for a in range(n)]
        first = [copy(1 + j, a, me, (*chip, cc), own=True) for j, chip in enumerate(chips) for a in range(n)]
        first += [copy(0, a, me, sibling, own=True) for a in range(n)]
        for cp in mine + first:
            cp.start()
        passed = []
        for j, chip in enumerate(chips):
            for a in range(n):
                copy(1 + j, a, (*chip, cc), me).wait_recv()
                cp = copy(4 + j, a, (*chip, cc), sibling)
                cp.start()
                passed.append(cp)
        for a in range(n):
            copy(0, a, sibling, me).wait_recv()
        for j, chip in enumerate(chips):
            for a in range(n):
                copy(4 + j, a, (*chip, 1 - cc), me).wait_recv()
        for cp in first + passed:
            cp.wait_send()
        for cp in mine:
            cp.wait()

    hbm = pl.BlockSpec(memory_space=pl.ANY)
    return _pc(body, name=name, out_shape=[_sds((N_DEV,) + a.shape, a.dtype) for a in locals_],
               in_specs=[hbm] * n, out_specs=[hbm] * n,
               scratch_shapes=[pltpu.SemaphoreType.DMA((7, n)), pltpu.SemaphoreType.DMA((7, n)),
                               pltpu.SemaphoreType.DMA((n,))])(*locals_)


N_CHIPS = 4


def _sibling_swap(parts, name):
    n = len(parts)

    def body(*refs):
        g_refs, got_refs = refs[:n], refs[n:2 * n]
        send_sems, recv_sems = refs[2 * n:]
        x, y, cc = _position()
        swaps = []
        for q in range(N_CHIPS):
            for a in range(n):
                swaps.append(pltpu.make_async_remote_copy(
                    src_ref=g_refs[a].at[2 * q + 1 - cc], dst_ref=got_refs[a].at[q], send_sem=send_sems.at[q, a],
                    recv_sem=recv_sems.at[q, a], device_id=(x, y, 1 - cc), device_id_type=MESH))
        for cp in swaps:
            cp.start()
        for cp in swaps:
            cp.wait_recv()
        for cp in swaps:
            cp.wait_send()

    hbm = pl.BlockSpec(memory_space=pl.ANY)
    return _pc(body, name=name, out_shape=[_sds((N_CHIPS,) + a.shape[1:], a.dtype) for a in parts],
               in_specs=[hbm] * n, out_specs=[hbm] * n,
               scratch_shapes=[pltpu.SemaphoreType.DMA((N_CHIPS, n)), pltpu.SemaphoreType.DMA((N_CHIPS, n))])(*parts)


def _pair_add(part, got, core, name):
    q, s, a, b = got.shape
    ta = _block_rows(a, b)

    def body(core_ref, k_ref, g_ref, o_ref):
        o_ref[...] = (k_ref[...].astype(F32) + g_ref[...].astype(F32)).astype(BF16)

    spec = pl.BlockSpec((None, None, ta, b), lambda c, l, i, core_ref: (c, l, i, 0))
    own = pl.BlockSpec((None, None, None, ta, b), lambda c, l, i, core_ref: (c, core_ref[0], l, i, 0))
    grid_spec = pltpu.PrefetchScalarGridSpec(num_scalar_prefetch=1, grid=(q, s, a // ta), in_specs=[own, spec],
                                             out_specs=spec)
    return _pc(body, name=name, out_shape=_sds(got.shape, BF16), grid_spec=grid_spec, compiler_params=_params(3))(
        core, part.reshape((N_CHIPS, 2) + part.shape[1:]), got)


def _chip_exchange(sums, name):
    n = len(sums)
    flips = [(1, 0), (0, 1), (1, 1)]

    def body(*refs):
        g_refs, out_refs = refs[:n], refs[n:2 * n]
        send_sems, recv_sems, local_sems = refs[2 * n:]
        x, y, cc = _position()
        me_q = 2 * x + y
        mine = [pltpu.make_async_copy(g_refs[a].at[me_q], out_refs[a].at[me_q], local_sems.at[a]) for a in range(n)]
        sends, recvs = [], []
        for k, (fx, fy) in enumerate(flips):
            px, py = (1 - x if fx else x), (1 - y if fy else y)
            peer_q = 2 * px + py
            for a in range(n):
                sends.append(pltpu.make_async_remote_copy(
                    src_ref=g_refs[a].at[peer_q], dst_ref=out_refs[a].at[me_q], send_sem=send_sems.at[k, a],
                    recv_sem=recv_sems.at[k, a], device_id=(px, py, cc), device_id_type=MESH))
                recvs.append(pltpu.make_async_remote_copy(
                    src_ref=g_refs[a].at[me_q], dst_ref=out_refs[a].at[peer_q], send_sem=send_sems.at[k, a],
                    recv_sem=recv_sems.at[k, a], device_id=(px, py, cc), device_id_type=MESH))
        for cp in sends + mine:
            cp.start()
        for cp in recvs:
            cp.wait_recv()
        for cp in sends:
            cp.wait_send()
        for cp in mine:
            cp.wait()

    hbm = pl.BlockSpec(memory_space=pl.ANY)
    return _pc(body, name=name, out_shape=[_sds(a.shape, a.dtype) for a in sums],
               in_specs=[hbm] * n, out_specs=[hbm] * n,
               scratch_shapes=[pltpu.SemaphoreType.DMA((3, n)), pltpu.SemaphoreType.DMA((3, n)),
                               pltpu.SemaphoreType.DMA((n,))])(*sums)


def _block_rows(a, b):
    ta = a
    while ta * b > 256 * 1024 and ta % 32 == 0:
        ta //= 2
    return ta


def _reduce_adamw(parts, w, m, v, name):
    n_parts, s, a, b = parts.shape
    ta = _block_rows(a, b)
    c1 = 1.0 - ADAM_B1 ** ADAM_STEP
    c2 = 1.0 - ADAM_B2 ** ADAM_STEP

    def body(p_ref, w_ref, m_ref, v_ref, g_out, d_out, m_out, v_out):
        g = p_ref[0].astype(F32)
        for j in range(1, n_parts):
            g = g + p_ref[j].astype(F32)
        m_new = ADAM_B1 * m_ref[...] + (1.0 - ADAM_B1) * g
        v_new = ADAM_B2 * v_ref[...] + (1.0 - ADAM_B2) * (g * g)
        m_hat = m_new / c1
        v_hat = v_new / c2
        g_out[...] = g
        d_out[...] = -ADAM_LR * (m_hat / (jnp.sqrt(v_hat) + ADAM_EPS) + ADAM_WD * w_ref[...])
        m_out[...] = m_new
        v_out[...] = v_new

    spec = pl.BlockSpec((None, ta, b), lambda l, i: (l, i, 0))
    return _pc(body, name=name, out_shape=(_sds((s, a, b)),) * 4, grid=(s, a // ta),
               in_specs=[pl.BlockSpec((n_parts, None, ta, b), lambda l, i: (0, l, i, 0)), spec, spec, spec],
               out_specs=(spec,) * 4, compiler_params=_params(2))(parts, w, m, v)


MATRICES = (("ada_mix_w", 2), ("w_in", 2), ("w_ssd_out", 1), ("w_conf_out", 2), ("w_sc_out", 2), ("w_o", 1),
            ("ada_ffn_w", 2), ("w_up", 2), ("w_down", 1))
CONV_WEIGHTS = (("ssd_conv_w", 2), ("conf_conv_w", 2), ("sc_conv_w", 2), ("ffn_conv_w", 2))
SHARDED = MATRICES + CONV_WEIGHTS
REPLICATED = ("ada_mix_b", "norm_mix_g", "b_gate", "ssd_conv_b", "ssd_dt_bias", "ssd_a_log", "ssd_d", "ssd_norm_g",
              "conf_conv_b", "conf_ln_g", "conf_ln_b", "ada_ffn_b", "norm_ffn_g", "ffn_conv_b", "final_norm_g")
WEIGHT_NAMES = ("ada_mix_w", "ada_mix_b", "norm_mix_g", "w_in", "b_gate", "ssd_conv_w", "ssd_conv_b", "ssd_dt_bias",
                "ssd_a_log", "ssd_d", "ssd_norm_g", "w_ssd_out", "conf_conv_w", "conf_conv_b", "conf_ln_g",
                "conf_ln_b", "w_conf_out", "sc_conv_w", "w_sc_out", "w_o", "ada_ffn_w", "ada_ffn_b", "norm_ffn_g",
                "w_up", "ffn_conv_w", "ffn_conv_b", "w_down", "final_norm_g")


def _pack_flat(arrays, cols, row_multiple, dtype):
    flat = jnp.concatenate([a.reshape(-1).astype(dtype) for a in arrays])
    rows = -(-flat.shape[0] // cols)
    rows = -(-rows // row_multiple) * row_multiple
    return jnp.pad(flat, (0, rows * cols - flat.shape[0])).reshape(rows, cols)


def _unpack_flat(flat2d, shapes):
    flat = flat2d.reshape(-1)
    out, off = [], 0
    for s in shapes:
        n = 1
        for d in s:
            n *= d
        out.append(flat[off:off + n].reshape(s))
        off += n
    return out


def _cols(g, i, lo, hi):
    b = g.shape[-1]
    pieces = []
    for k in range(N_DEV):
        a, e = max(lo, k * b), min(hi, (k + 1) * b)
        if a < e:
            pieces.append(g[k, i, :, a - k * b:e - k * b])
    return pieces[0] if len(pieces) == 1 else jnp.concatenate(pieces, axis=1)


def _rows(g, i):
    return g[:, i].reshape(N_DEV * g.shape[2], g.shape[3])


def _col_shards(segs, b):
    shards = []
    for k in range(N_DEV):
        lo, hi = k * b, (k + 1) * b
        pieces, off = [], 0
        for seg in segs:
            n = seg.shape[1]
            a, e = max(lo, off), min(hi, off + n)
            if a < e:
                pieces.append(seg[:, a - off:e - off])
            off += n
        shards.append(pieces[0] if len(pieces) == 1 else jnp.concatenate(pieces, axis=1))
    return jnp.stack(shards)


def _row_shards(full):
    return full.reshape(N_DEV, full.shape[0] // N_DEV, full.shape[1])


def _pad_rows(a, rows):
    return jnp.pad(a, ((0, rows - a.shape[0]), (0, 0)))


def _pad_lanes(a):
    return jnp.pad(a, ((0, 0), (0, LANES - a.shape[1])))


def _layer_weights(full, i):
    row = lambda name: full[name][i].reshape(1, -1)
    whole = lambda name: _cols(full[name], i, 0, N_DEV * full[name].shape[-1])
    w_in = full["w_in"]
    wl = {
        "ada_mix_w": whole("ada_mix_w"), "ada_mix_b": row("ada_mix_b"), "norm_mix_g": row("norm_mix_g"),
        "w_z": _cols(w_in, i, 0, OFF_Z), "w_xbc": _cols(w_in, i, OFF_Z, OFF_XBC),
        "w_dt": _pad_lanes(_cols(w_in, i, OFF_XBC, OFF_DT)), "w_conf": _cols(w_in, i, OFF_DT, OFF_CONF),
        "w_sc": _cols(w_in, i, OFF_CONF, OFF_SC), "w_gates": _cols(w_in, i, OFF_SC, N_IN),
        "b_gate": row("b_gate"),
        "ssd_conv_w": _pad_rows(whole("ssd_conv_w"), SUBLANES), "ssd_conv_b": row("ssd_conv_b"),
        "dt_bias": _pad_lanes(row("ssd_dt_bias")), "a_log": _pad_lanes(row("ssd_a_log")),
        "ssd_d": _pad_lanes(row("ssd_d")), "ssd_norm_g": row("ssd_norm_g"), "w_ssd_out": _rows(full["w_ssd_out"], i),
        "conf_conv_w": _pad_rows(whole("conf_conv_w"), CONF_HALO), "conf_conv_b": row("conf_conv_b"),
        "conf_ln_g": row("conf_ln_g"), "conf_ln_b": row("conf_ln_b"), "w_conf_out": whole("w_conf_out"),
        "sc_conv_w": _pad_rows(whole("sc_conv_w"), SUBLANES), "w_sc_out": whole("w_sc_out"),
        "w_o": _rows(full["w_o"], i),
        "ada_ffn_w": whole("ada_ffn_w"), "ada_ffn_b": row("ada_ffn_b"), "norm_ffn_g": row("norm_ffn_g"),
        "w_up": whole("w_up"), "ffn_conv_w": _pad_rows(whole("ffn_conv_w"), SUBLANES),
        "ffn_conv_b": row("ffn_conv_b"), "w_down": _rows(full["w_down"], i),
    }
    return wl


def _adaln(sc8, w, b, name):
    mod = _matmul(sc8, w, "nn", F32, name)[0:1, :] + b
    return mod[:, :D_MODEL], mod[:, D_MODEL:2 * D_MODEL], mod[:, 2 * D_MODEL:]


def _layer_fwd(i, x, prev, sc8, wl):
    t = f"l{i}_"
    s = {}
    shift, scale, gate = _adaln(sc8, wl["ada_mix_w"], wl["ada_mix_b"], t + "ada_mix")
    if prev is None:
        s["x_in"] = x
        s["h"] = _prenorm_first(x, wl["norm_mix_g"], scale, shift, t + "norm_mix")
    else:
        s["x_in"], s["h"] = _prenorm_res(x, prev[0], prev[1], wl["norm_mix_g"], scale, shift, t + "norm_mix")
    s["scale_mix"], s["gate_mix"] = scale, gate
    h = s["h"]
    s["z"] = _matmul(h, wl["w_z"], "nn", F32, t + "in_z")
    s["xbc"] = _matmul(h, wl["w_xbc"], "nn", F32, t + "in_xbc")
    s["dt_raw"] = _matmul(h, wl["w_dt"], "nn", F32, t + "in_dt")
    s["conf"] = _matmul(h, wl["w_conf"], "nn", F32, t + "in_conf")
    s["sc"] = _matmul(h, wl["w_sc"], "nn", F32, t + "in_sc")
    s["gates"] = _matmul(h, wl["w_gates"], "nn", F32, t + "in_gates")
    s["pre"], s["dt"] = _ssd_pre(s["xbc"], s["dt_raw"], wl["ssd_conv_w"], wl["ssd_conv_b"], wl["dt_bias"],
                                 t + "ssd_pre")
    s["y"], s["hprev"] = _ssd_scan(s["pre"], s["dt"], wl["a_log"], wl["ssd_d"], t + "ssd_scan")
    s["ya_in"] = _ssd_post(s["y"], s["z"], wl["ssd_norm_g"], t + "ssd_post")
    s["yb_in"], s["uc"] = _conf_fwd(s["conf"], wl["conf_conv_w"], wl["conf_conv_b"], wl["conf_ln_g"],
                                    wl["conf_ln_b"], t + "conf")
    s["yc_in"] = _sc_fwd(s["sc"], wl["sc_conv_w"], t + "sconv")
    s["ya"] = _matmul(s["ya_in"], wl["w_ssd_out"], "nn", F32, t + "ssd_out")
    s["yb"] = _matmul(s["yb_in"], wl["w_conf_out"], "nn", F32, t + "conf_out")
    s["yc"] = _matmul(s["yc_in"], wl["w_sc_out"], "nn", F32, t + "sc_out")
    s["merged"] = _merge_fwd(s["gates"], s["ya"], s["yb"], s["yc"], wl["b_gate"], t + "merge")
    s["mix"] = _matmul(s["merged"], wl["w_o"], "nn", F32, t + "w_o")
    shift2, scale2, gate2 = _adaln(sc8, wl["ada_ffn_w"], wl["ada_ffn_b"], t + "ada_ffn")
    s["x_mid"], s["h2"] = _prenorm_res(s["x_in"], s["mix"], gate, wl["norm_ffn_g"], scale2, shift2, t + "norm_ffn")
    s["scale_ffn"], s["gate_ffn"] = scale2, gate2
    s["up"] = _matmul(s["h2"], wl["w_up"], "nn", F32, t + "w_up")
    s["a"] = _ffn_mid(s["up"], wl["ffn_conv_w"], wl["ffn_conv_b"], t + "ffn_mid")
    s["out"] = _matmul(s["a"], wl["w_down"], "nn", F32, t + "w_down")
    return s


def _layer_bwd(i, s, wl, sc8, dys_ffn, dx_after, dgate_ffn, prev):
    t = f"l{i}_b_"
    g = {}
    da = _matmul(dys_ffn, wl["w_down"], "nt", F32, t + "d_a")
    g["w_down"] = _matmul(s["a"], dys_ffn, "tn", BF16, t + "dw_down")
    dug, duv, dfw, g["ffn_conv_b"] = _ffn_mid_bwd(da, s["up"], wl["ffn_conv_w"], wl["ffn_conv_b"], t + "ffn_mid")
    g["ffn_conv_w"] = dfw[:3]
    dh2 = _matmul_sum_nt([dug, duv], [wl["w_up"][:, :D_FF], wl["w_up"][:, D_FF:]], t + "d_h2")
    g["w_up"] = [_matmul(s["h2"], dug, "tn", BF16, t + "dw_up_g"), _matmul(s["h2"], duv, "tn", BF16, t + "dw_up_v")]
    dx_mid, dshift2, dscale2, g["norm_ffn_g"], dys_mix, dgate_mix = _norm_bwd(
        dh2, s["x_mid"], dx_after, wl["norm_ffn_g"], s["scale_ffn"], t + "norm_ffn", s["mix"], s["gate_mix"])
    dmod_ffn = jnp.concatenate([dshift2, dscale2, dgate_ffn], axis=1)
    g["ada_ffn_b"] = dmod_ffn
    g["ada_ffn_w"] = [_matmul(sc8, _pad_rows(dmod_ffn, SUBLANES), "tn", BF16, t + "dw_ada_ffn")]
    dmerged = _matmul(dys_mix, wl["w_o"], "nt", F32, t + "d_merged")
    g["w_o"] = _matmul(s["merged"], dys_mix, "tn", BF16, t + "dw_o")
    dya, dyb, dyc, dgates, g["b_gate"] = _merge_bwd(dmerged, s["gates"], s["ya"], s["yb"], s["yc"], wl["b_gate"],
                                                    t + "merge")
    dya_in = _matmul(dya, wl["w_ssd_out"], "nt", F32, t + "d_ya_in")
    g["w_ssd_out"] = _matmul(s["ya_in"], dya, "tn", BF16, t + "dw_ssd_out")
    dyb_in = _matmul(dyb, wl["w_conf_out"], "nt", F32, t + "d_yb_in")
    g["w_conf_out"] = [_matmul(s["yb_in"], dyb, "tn", BF16, t + "dw_conf_out")]
    dyc_in = _matmul(dyc, wl["w_sc_out"], "nt", F32, t + "d_yc_in")
    g["w_sc_out"] = [_matmul(s["yc_in"], dyc, "tn", BF16, t + "dw_sc_out")]
    dy, dz, g["ssd_norm_g"] = _ssd_post_bwd(dya_in, s["y"], s["z"], wl["ssd_norm_g"], t + "ssd_post")
    dpre, ddt, da_log, dd = _ssd_scan_bwd(s["pre"], s["dt"], s["hprev"], dy, wl["a_log"], wl["ssd_d"],
                                          t + "ssd_scan")
    g["ssd_a_log"], g["ssd_d"] = da_log[:, :SSD_HEADS], dd[:, :SSD_HEADS]
    dxbc, ddt_raw, dcw, g["ssd_conv_b"], ddtb = _ssd_pre_bwd(dpre, s["xbc"], ddt, s["dt_raw"], wl["ssd_conv_w"],
                                                             wl["dt_bias"], t + "ssd_pre")
    g["ssd_conv_w"], g["ssd_dt_bias"] = dcw[:4], ddtb[:, :SSD_HEADS]
    dconf, dccw, g["conf_conv_b"], g["conf_ln_g"], g["conf_ln_b"] = _conf_bwd(
        dyb_in, s["uc"], s["conf"], wl["conf_conv_w"], wl["conf_ln_g"], wl["conf_ln_b"], t + "conf")
    g["conf_conv_w"] = dccw[:CONF_KERNEL]
    dsc, dscw = _sc_bwd(dyc_in, s["sc"], wl["sc_conv_w"], t + "sconv")
    g["sc_conv_w"] = dscw[:3]
    segs = (("z", dz, "w_z"), ("xbc", dxbc, "w_xbc"), ("dt", ddt_raw, "w_dt"), ("conf", dconf, "w_conf"),
            ("sc", dsc, "w_sc"), ("gates", dgates, "w_gates"))
    dh = _matmul_sum_nt([dseg for _, dseg, _ in segs], [wl[wname] for _, _, wname in segs], t + "d_h")
    dw_segs = []
    for nm, dseg, wname in segs:
        dw = _matmul(s["h"], dseg, "tn", BF16, t + "dw_in_" + nm)
        dw_segs.append(dw[:, :SSD_HEADS] if nm == "dt" else dw)
    g["w_in"] = dw_segs
    if prev is None:
        dx_in, dshift, dscale, g["norm_mix_g"] = _norm_bwd(dh, s["x_in"], dx_mid, wl["norm_mix_g"], s["scale_mix"],
                                                          t + "norm_mix")
        back = None
    else:
        dx_in, dshift, dscale, g["norm_mix_g"], dys_prev, dgate_prev = _norm_bwd(
            dh, s["x_in"], dx_mid, wl["norm_mix_g"], s["scale_mix"], t + "norm_mix", prev[0], prev[1])
        back = (dys_prev, dgate_prev)
    dmod_mix = jnp.concatenate([dshift, dscale, dgate_mix], axis=1)
    g["ada_mix_b"] = dmod_mix
    g["ada_mix_w"] = [_matmul(sc8, _pad_rows(dmod_mix, SUBLANES), "tn", BF16, t + "dw_ada_mix")]
    return g, dx_in, back


def _device_step(x, c, target, full):
    sc8 = _pad_rows(c * (1.0 / (1.0 + jnp.exp(-c))), SUBLANES)
    wls = [_layer_weights(full, i) for i in range(DEPTH)]
    saved, prev, xcur = [], None, x
    for i in range(DEPTH):
        s = _layer_fwd(i, xcur, prev, sc8, wls[i])
        saved.append(s)
        xcur, prev = s["x_mid"], (s["out"], s["gate_ffn"])
    gf = full["final_norm_g"].reshape(1, -1)
    last = saved[-1]
    loss, dx, dys, dgate, dgf = _final_loss(last["x_mid"], last["out"], last["gate_ffn"], gf, target, "final_loss")
    grads = [None] * DEPTH
    for i in reversed(range(DEPTH)):
        prev = None if i == 0 else (saved[i - 1]["out"], saved[i - 1]["gate_ffn"])
        grads[i], dx, back = _layer_bwd(i, saved[i], wls[i], sc8, dys, dx, dgate, prev)
        if back is not None:
            dys, dgate = back
    return loss[0, 0], dx, grads, dgf


def _step(x, c, target, weights, moments_m, moments_v):
    mat_names = [n for n, _ in MATRICES]
    sharded_names = [n for n, _ in SHARDED]
    local = [weights[n].astype(BF16) if n in mat_names else weights[n] for n in sharded_names]
    full = {n: weights[n] for n in REPLICATED}
    full.update(zip(sharded_names, _all_gather(local, "gather_weights")))
    loss, grad_x, grads, dgf = _device_step(x[0], c, target[0], full)
    parts = []
    for name, axis in SHARDED:
        per_layer = []
        for i in range(DEPTH):
            gw = grads[i][name]
            if axis == 1:
                per_layer.append(_row_shards(gw))
            else:
                per_layer.append(_col_shards(gw if isinstance(gw, list) else [gw], weights[name].shape[-1]))
        parts.append(jnp.stack(per_layer, axis=1).astype(BF16))
    got = _sibling_swap(parts, "swap_grads")
    core = lax.axis_index("c").astype(jnp.int32).reshape(1)
    sums = [_pair_add(p, g, core, "pair_add_" + n) for n, p, g in zip(sharded_names, parts, got)]
    received = _chip_exchange(sums, "exchange_grads")
    big = {n: _reduce_adamw(r, weights[n], moments_m[n], moments_v[n], "adamw_" + n)
           for n, r in zip(sharded_names, received)}
    rep_grads = [dgf if n == "final_norm_g" else jnp.stack([grads[i][n].reshape(-1) for i in range(DEPTH)])
                 for n in REPLICATED]
    small_parts, = _all_gather([_pack_flat(rep_grads, LANES, SUBLANES, F32)], "gather_small_grads")
    pack_s = lambda d: _pack_flat([d[n] for n in REPLICATED], LANES, SUBLANES, F32)[None]
    small = _reduce_adamw(small_parts[:, None], pack_s(weights), pack_s(moments_m), pack_s(moments_v),
                          "adamw_replicated")
    small = [_unpack_flat(b, [weights[n].shape for n in REPLICATED]) for b in small]
    results = []
    for kind in range(4):
        by_name = {n: big[n][kind] for n in sharded_names}
        by_name.update(zip(REPLICATED, small[kind]))
        results.append([by_name[n] for n in WEIGHT_NAMES])
    loss = lax.psum(loss, ("x", "y", "c"))
    return (loss, grad_x[None], *results[0], *results[1], *results[2], *results[3])


def kernel(x, c, ada_mix_w, ada_mix_b, norm_mix_g, w_in, b_gate, ssd_conv_w, ssd_conv_b, ssd_dt_bias, ssd_a_log, ssd_d, ssd_norm_g, w_ssd_out, conf_conv_w, conf_conv_b, conf_ln_g, conf_ln_b, w_conf_out, sc_conv_w, w_sc_out, w_o, ada_ffn_w, ada_ffn_b, norm_ffn_g, w_up, ffn_conv_w, ffn_conv_b, w_down, final_norm_g, loss_target, m_ada_mix_w, m_ada_mix_b, m_norm_mix_g, m_w_in, m_b_gate, m_ssd_conv_w, m_ssd_conv_b, m_ssd_dt_bias, m_ssd_a_log, m_ssd_d, m_ssd_norm_g, m_w_ssd_out, m_conf_conv_w, m_conf_conv_b, m_conf_ln_g, m_conf_ln_b, m_w_conf_out, m_sc_conv_w, m_w_sc_out, m_w_o, m_ada_ffn_w, m_ada_ffn_b, m_norm_ffn_g, m_w_up, m_ffn_conv_w, m_ffn_conv_b, m_w_down, m_final_norm_g, v_ada_mix_w, v_ada_mix_b, v_norm_mix_g, v_w_in, v_b_gate, v_ssd_conv_w, v_ssd_conv_b, v_ssd_dt_bias, v_ssd_a_log, v_ssd_d, v_ssd_norm_g, v_w_ssd_out, v_conf_conv_w, v_conf_conv_b, v_conf_ln_g, v_conf_ln_b, v_w_conf_out, v_sc_conv_w, v_w_sc_out, v_w_o, v_ada_ffn_w, v_ada_ffn_b, v_norm_ffn_g, v_w_up, v_ffn_conv_w, v_ffn_conv_b, v_w_down, v_final_norm_g):
    given = dict(locals())
    weights = {n: given[n] for n in WEIGHT_NAMES}
    moments_m = {n: given["m_" + n] for n in WEIGHT_NAMES}
    moments_v = {n: given["v_" + n] for n in WEIGHT_NAMES}
    return _step(x, c, loss_target, weights, moments_m, moments_v)
```

```python
import functools

import jax
import jax.numpy as jnp
from jax import lax
from jax.experimental import pallas as pl
from jax.experimental.pallas import tpu as pltpu

F32 = jnp.float32
BF16 = jnp.bfloat16
MESH = pl.DeviceIdType.MESH

N_DEV = 8
DEPTH = 2
D_MODEL = 1024
SSD_HEADS = 16
SSD_HEAD_DIM = 64
SSD_INNER = 1024
SSD_STATE = 64
SSD_CHUNK = 128
SSD_XBC = 1280
CONF_WIDTH = 512
CONF_KERNEL = 31
SC_WIDTH = 512
D_FF = 2816
EPS = 1e-6
OFF_Z, OFF_XBC, OFF_DT, OFF_CONF, OFF_SC, N_IN = 1024, 2304, 2320, 3344, 4880, 7952

ADAM_LR, ADAM_B1, ADAM_B2, ADAM_EPS, ADAM_WD, ADAM_STEP = 0.001, 0.9, 0.999, 1e-08, 0.01, 10

LANES = 128
SUBLANES = 8
VMEM_LIMIT = 56 * 1024 * 1024
ROW_TILE = 256

NN = (((1,), (0,)), ((), ()))
NT = (((1,), (1,)), ((), ()))
TN = (((0,), (0,)), ((), ()))


def _params(n_axes):
    return pltpu.CompilerParams(dimension_semantics=("arbitrary",) * n_axes, vmem_limit_bytes=VMEM_LIMIT)


def _pc(body, **kw):
    return pl.pallas_call(body, **kw)


def _dot(a, b, dn=NN, precision=None):
    return lax.dot_general(a, b, dn, precision=precision, preferred_element_type=F32)


def _split3(x):
    hi = x.astype(BF16)
    r1 = x - hi.astype(F32)
    mid = r1.astype(BF16)
    return hi, mid, (r1 - mid.astype(F32)).astype(BF16)


def _dot_sel(x, sel):
    hi, mid, lo = _split3(x)
    return _dot(hi, sel) + _dot(mid, sel) + _dot(lo, sel)


def _sel_dot(sel, x):
    hi, mid, lo = _split3(x)
    return _dot(sel, hi) + _dot(sel, mid) + _dot(sel, lo)


def _sig(x):
    return 1.0 / (1.0 + jnp.exp(-x))


def _fold(v):
    r, c = v.shape
    return v.reshape(r // SUBLANES, SUBLANES, c).sum(axis=0)


def _tile(n_rows):
    return min(ROW_TILE, n_rows // 2)


def _row(tl, c, col=0):
    return pl.BlockSpec((tl, c), lambda i, col=col: (i, col))


def _prev(tl, hb, c, col=0):
    r = tl // hb
    return pl.BlockSpec((hb, c), lambda i, col=col: (jnp.maximum(i * r - 1, 0), col))


def _next(tl, hb, c, n_rows, col=0):
    r = tl // hb
    last = n_rows // hb - 1
    return pl.BlockSpec((hb, c), lambda i, col=col: (jnp.minimum((i + 1) * r, last), col))


def _const(shape):
    return pl.BlockSpec(shape, lambda i: (0,) * len(shape))


def _sds(shape, dtype=F32):
    return jax.ShapeDtypeStruct(shape, dtype)


MM_TILE = 1536
MM_FULL_K = 3072
MM_K_TILE = 1024


def _pick(dim, target):
    if dim <= target:
        return dim
    best = None
    for t in range(LANES, target + 1, LANES):
        if dim % t == 0:
            best = t
    assert best is not None, (dim, target)
    return best


def _matmul(a, b, mode, out_dtype, name):
    if mode == "nn":
        (m, k), (k2, n) = a.shape, b.shape
    elif mode == "nt":
        (m, k), (n, k2) = a.shape, b.shape
    else:
        (k, m), (k2, n) = a.shape, b.shape
    assert k == k2, (a.shape, b.shape, mode)
    tm, tn = _pick(m, MM_TILE), _pick(n, MM_TILE)
    tk = k if k <= MM_FULL_K else _pick(k, MM_K_TILE)
    nk = k // tk
    dn = {"nn": NN, "nt": NT, "tn": TN}[mode]

    def body_one(a_ref, b_ref, o_ref):
        o_ref[...] = _dot(a_ref[...].astype(BF16), b_ref[...].astype(BF16), dn).astype(out_dtype)

    def body_acc(a_ref, b_ref, o_ref, acc):
        kk = pl.program_id(2)

        @pl.when(kk == 0)
        def _():
            acc[...] = jnp.zeros_like(acc)

        acc[...] += _dot(a_ref[...].astype(BF16), b_ref[...].astype(BF16), dn)

        @pl.when(kk == nk - 1)
        def _():
            o_ref[...] = acc[...].astype(out_dtype)

    a_spec = {"nn": pl.BlockSpec((tm, tk), lambda i, j, kk: (i, kk)),
              "nt": pl.BlockSpec((tm, tk), lambda i, j, kk: (i, kk)),
              "tn": pl.BlockSpec((tk, tm), lambda i, j, kk: (kk, i))}[mode]
    b_spec = {"nn": pl.BlockSpec((tk, tn), lambda i, j, kk: (kk, j)),
              "nt": pl.BlockSpec((tn, tk), lambda i, j, kk: (j, kk)),
              "tn": pl.BlockSpec((tk, tn), lambda i, j, kk: (kk, j))}[mode]
    o_spec = pl.BlockSpec((tm, tn), lambda i, j, kk: (i, j))
    return _pc(body_one if nk == 1 else body_acc, name=name, out_shape=_sds((m, n), out_dtype),
               grid=(m // tm, n // tn, nk), in_specs=[a_spec, b_spec], out_specs=o_spec,
               scratch_shapes=[] if nk == 1 else [pltpu.VMEM((tm, tn), F32)], compiler_params=_params(3))(a, b)


SUM_NT_TILE = 512


def _matmul_sum_nt(a_list, b_list, name):
    m, n = a_list[0].shape[0], b_list[0].shape[0]
    cnt = len(a_list)
    tm, tn = _pick(m, SUM_NT_TILE), _pick(n, SUM_NT_TILE)

    def body(*refs):
        a_refs, b_refs, o_ref = refs[:cnt], refs[cnt:2 * cnt], refs[2 * cnt]
        acc = _dot(a_refs[0][...].astype(BF16), b_refs[0][...].astype(BF16), NT)
        for t in range(1, cnt):
            acc = acc + _dot(a_refs[t][...].astype(BF16), b_refs[t][...].astype(BF16), NT)
        o_ref[...] = acc

    in_specs = [pl.BlockSpec((tm, a.shape[1]), lambda j, i: (i, 0)) for a in a_list]
    in_specs += [pl.BlockSpec((tn, b.shape[1]), lambda j, i: (j, 0)) for b in b_list]
    return _pc(body, name=name, out_shape=_sds((m, n)), grid=(n // tn, m // tm), in_specs=in_specs,
               out_specs=pl.BlockSpec((tm, tn), lambda j, i: (i, j)), compiler_params=_params(2))(*a_list, *b_list)


def _norm_mod(x, g, scale, shift):
    r = lax.rsqrt(jnp.mean(x * x, axis=-1, keepdims=True) + EPS)
    return ((x * r) * g) * (1.0 + scale) + shift


def _prenorm_first(x, g, scale, shift, name):
    n, d = x.shape
    tl = _tile(n)

    def body(x_ref, g_ref, sc_ref, sh_ref, h_ref):
        h_ref[...] = _norm_mod(x_ref[...], g_ref[...], sc_ref[...], sh_ref[...]).astype(BF16)

    return _pc(body, name=name, out_shape=_sds((n, d), BF16), grid=(n // tl,),
               in_specs=[_row(tl, d)] + [_const((1, d))] * 3, out_specs=_row(tl, d),
               compiler_params=_params(1))(x, g, scale, shift)


def _prenorm_res(x, y, gate, g, scale, shift, name):
    n, d = x.shape
    tl = _tile(n)

    def body(x_ref, y_ref, gate_ref, g_ref, sc_ref, sh_ref, xo_ref, h_ref):
        xn = x_ref[...] + gate_ref[...] * y_ref[...]
        xo_ref[...] = xn
        h_ref[...] = _norm_mod(xn, g_ref[...], sc_ref[...], sh_ref[...]).astype(BF16)

    return _pc(body, name=name, out_shape=(_sds((n, d)), _sds((n, d), BF16)), grid=(n // tl,),
               in_specs=[_row(tl, d), _row(tl, d)] + [_const((1, d))] * 4,
               out_specs=(_row(tl, d), _row(tl, d)), compiler_params=_params(1))(x, y, gate, g, scale, shift)


def _final_loss(x, y, gate, gf, target, name):
    n, d = x.shape
    tl = _tile(n)
    nb = n // tl

    def body(x_ref, y_ref, gate_ref, gf_ref, t_ref, loss_ref, dx_ref, dys_ref, dgate_ref, dgf_ref,
             acc_l, acc_gate, acc_gf):
        i = pl.program_id(0)

        @pl.when(i == 0)
        def _():
            acc_l[...] = jnp.zeros_like(acc_l)
            acc_gate[...] = jnp.zeros_like(acc_gate)
            acc_gf[...] = jnp.zeros_like(acc_gf)

        yv = y_ref[...]
        gate = gate_ref[...]
        gf = gf_ref[...]
        x2 = x_ref[...] + gate * yv
        r = lax.rsqrt(jnp.mean(x2 * x2, axis=-1, keepdims=True) + EPS)
        xn = x2 * r
        e = xn * gf - t_ref[...]
        acc_l[...] += _fold(e * e)
        dy = e * (1.0 / d)
        acc_gf[...] += _fold(dy * xn)
        dxn = dy * gf
        dx = r * (dxn - xn * jnp.mean(dxn * xn, axis=-1, keepdims=True))
        dx_ref[...] = dx
        dys_ref[...] = (dx * gate).astype(BF16)
        acc_gate[...] += _fold(dx * yv)

        @pl.when(i == nb - 1)
        def _():
            loss_ref[...] = jnp.full((SUBLANES, LANES), 0.5 / d, F32) * jnp.sum(acc_l[...])
            dgate_ref[...] = jnp.sum(acc_gate[...], axis=0, keepdims=True)
            dgf_ref[...] = jnp.sum(acc_gf[...], axis=0, keepdims=True)

    return _pc(body, name=name,
               out_shape=(_sds((SUBLANES, LANES)), _sds((n, d)), _sds((n, d), BF16), _sds((1, d)), _sds((1, d))),
               grid=(nb,),
               in_specs=[_row(tl, d), _row(tl, d), _const((1, d)), _const((1, d)), _row(tl, d)],
               out_specs=(_const((SUBLANES, LANES)), _row(tl, d), _row(tl, d), _const((1, d)), _const((1, d))),
               scratch_shapes=[pltpu.VMEM((SUBLANES, d), F32)] * 3,
               compiler_params=_params(1))(x, y, gate, gf, target)


def _norm_bwd(dh, x, dxo, g, scale, name, y_prev=None, gate_prev=None):
    n, d = x.shape
    tl = _tile(n)
    nb = n // tl
    has_prev = y_prev is not None

    def body(*refs):
        if has_prev:
            (dh_ref, x_ref, dxo_ref, g_ref, sc_ref, yp_ref, gp_ref,
             dx_ref, dsh_ref, dsc_ref, dg_ref, dys_ref, dgp_ref, acc_sh, acc_s, acc_gp) = refs
        else:
            (dh_ref, x_ref, dxo_ref, g_ref, sc_ref,
             dx_ref, dsh_ref, dsc_ref, dg_ref, acc_sh, acc_s) = refs
        i = pl.program_id(0)

        @pl.when(i == 0)
        def _():
            acc_sh[...] = jnp.zeros_like(acc_sh)
            acc_s[...] = jnp.zeros_like(acc_s)
            if has_prev:
                acc_gp[...] = jnp.zeros_like(acc_gp)

        x_ = x_ref[...]
        dh_ = dh_ref[...]
        g_ = g_ref[...]
        one_sc = 1.0 + sc_ref[...]
        r = lax.rsqrt(jnp.mean(x_ * x_, axis=-1, keepdims=True) + EPS)
        xn = x_ * r
        dxn = dh_ * (g_ * one_sc)
        dx = dxo_ref[...] + r * (dxn - xn * jnp.mean(dxn * xn, axis=-1, keepdims=True))
        dx_ref[...] = dx
        acc_sh[...] += _fold(dh_)
        acc_s[...] += _fold(dh_ * xn)
        if has_prev:
            dys_ref[...] = (dx * gp_ref[...]).astype(BF16)
            acc_gp[...] += _fold(dx * yp_ref[...])

        @pl.when(i == nb - 1)
        def _():
            s = jnp.sum(acc_s[...], axis=0, keepdims=True)
            dsh_ref[...] = jnp.sum(acc_sh[...], axis=0, keepdims=True)
            dsc_ref[...] = s * g_
            dg_ref[...] = s * one_sc
            if has_prev:
                dgp_ref[...] = jnp.sum(acc_gp[...], axis=0, keepdims=True)

    vec = _sds((1, d))
    in_specs = [_row(tl, d)] * 3 + [_const((1, d))] * 2
    out_shape = [_sds((n, d)), vec, vec, vec]
    out_specs = [_row(tl, d)] + [_const((1, d))] * 3
    scratch = [pltpu.VMEM((SUBLANES, d), F32)] * 2
    args = [dh, x, dxo, g, scale]
    if has_prev:
        in_specs += [_row(tl, d), _const((1, d))]
        out_shape += [_sds((n, d), BF16), vec]
        out_specs += [_row(tl, d), _const((1, d))]
        scratch += [pltpu.VMEM((SUBLANES, d), F32)]
        args += [y_prev, gate_prev]
    return _pc(body, name=name, out_shape=tuple(out_shape), grid=(nb,), in_specs=in_specs,
               out_specs=tuple(out_specs), scratch_shapes=scratch, compiler_params=_params(1))(*args)


CONV_HALO = 8
CONF_HALO = 32


def _ssd_pre(xbc, dt_raw, conv_w, conv_b, dt_bias, name):
    n, c = xbc.shape
    tl = _tile(n)
    hb = CONV_HALO
    k_taps = 4

    def body(x_ref, xp_ref, dt_ref, w_ref, b_ref, dtb_ref, pre_ref, dts_ref, buf):
        i = pl.program_id(0)
        buf[pl.ds(0, hb), :] = jnp.where(i > 0, xp_ref[...], 0.0)
        buf[pl.ds(hb, tl), :] = x_ref[...]
        acc = b_ref[...] + w_ref[0:1, :] * buf[pl.ds(hb - 3, tl), :]
        for k in range(1, k_taps):
            acc = acc + w_ref[k:k + 1, :] * buf[pl.ds(hb - 3 + k, tl), :]
        pre_ref[...] = acc
        v = dt_ref[...] + dtb_ref[...]
        dts_ref[...] = jnp.maximum(v, 0.0) + jnp.log1p(jnp.exp(-jnp.abs(v)))

    return _pc(body, name=name, out_shape=(_sds((n, c)), _sds((n, LANES))), grid=(n // tl,),
               in_specs=[_row(tl, c), _prev(tl, hb, c), _row(tl, LANES), _const((SUBLANES, c)), _const((1, c)),
                         _const((1, LANES))],
               out_specs=(_row(tl, c), _row(tl, LANES)),
               scratch_shapes=[pltpu.VMEM((tl + hb, c), F32)], compiler_params=_params(1))(
        xbc, xbc, dt_raw, conv_w, conv_b, dt_bias)


def _ssd_pre_bwd(dpre, xbc, ddt, dt_raw, conv_w, dt_bias, name):
    n, c = xbc.shape
    tl = _tile(n)
    nb = n // tl
    hb = CONV_HALO
    k_taps = 4

    def body(dp_ref, dpn_ref, x_ref, xp_ref, ddt_ref, dt_ref, w_ref, dtb_ref,
             dx_ref, ddr_ref, dw_ref, db_ref, ddtb_ref, dbuf, xbuf, acc_w, acc_b, acc_dtb):
        i = pl.program_id(0)

        @pl.when(i == 0)
        def _():
            acc_w[...] = jnp.zeros_like(acc_w)
            acc_b[...] = jnp.zeros_like(acc_b)
            acc_dtb[...] = jnp.zeros_like(acc_dtb)

        dp = dp_ref[...]
        dbuf[pl.ds(0, tl), :] = dp
        dbuf[pl.ds(tl, hb), :] = jnp.where(i < nb - 1, dpn_ref[...], 0.0)
        xbuf[pl.ds(0, hb), :] = jnp.where(i > 0, xp_ref[...], 0.0)
        xbuf[pl.ds(hb, tl), :] = x_ref[...]
        dx = w_ref[0:1, :] * dbuf[pl.ds(3, tl), :]
        for k in range(1, k_taps):
            dx = dx + w_ref[k:k + 1, :] * dbuf[pl.ds(3 - k, tl), :]
        dx_ref[...] = dx.astype(BF16)
        for k in range(k_taps):
            acc_w[k] += _fold(dp * xbuf[pl.ds(hb - 3 + k, tl), :])
        acc_b[...] += _fold(dp)
        ddr = ddt_ref[...] * _sig(dt_ref[...] + dtb_ref[...])
        ddr_ref[...] = ddr.astype(BF16)
        acc_dtb[...] += _fold(ddr)

        @pl.when(i == nb - 1)
        def _():
            dw_ref[...] = jnp.zeros_like(dw_ref)
            for k in range(k_taps):
                dw_ref[k:k + 1, :] = jnp.sum(acc_w[k], axis=0, keepdims=True)
            db_ref[...] = jnp.sum(acc_b[...], axis=0, keepdims=True)
            ddtb_ref[...] = jnp.sum(acc_dtb[...], axis=0, keepdims=True)

    return _pc(body, name=name,
               out_shape=(_sds((n, c), BF16), _sds((n, LANES), BF16), _sds((SUBLANES, c)), _sds((1, c)),
                          _sds((1, LANES))),
               grid=(nb,),
               in_specs=[_row(tl, c), _next(tl, hb, c, n), _row(tl, c), _prev(tl, hb, c), _row(tl, LANES),
                         _row(tl, LANES), _const((SUBLANES, c)), _const((1, LANES))],
               out_specs=(_row(tl, c), _row(tl, LANES), _const((SUBLANES, c)), _const((1, c)), _const((1, LANES))),
               scratch_shapes=[pltpu.VMEM((tl + hb, c), F32), pltpu.VMEM((tl + hb, c), F32),
                               pltpu.VMEM((k_taps, SUBLANES, c), F32), pltpu.VMEM((SUBLANES, c), F32),
                               pltpu.VMEM((SUBLANES, LANES), F32)],
               compiler_params=_params(1))(dpre, dpre, xbc, xbc, ddt, dt_raw, conv_w, dt_bias)


def _expand_mat():
    r = lax.broadcasted_iota(jnp.int32, (LANES, SSD_INNER), 0)
    c = lax.broadcasted_iota(jnp.int32, (LANES, SSD_INNER), 1)
    return (jnp.right_shift(c, 6) == r).astype(BF16)


def _reduce_mat():
    r = lax.broadcasted_iota(jnp.int32, (SSD_INNER, LANES), 0)
    c = lax.broadcasted_iota(jnp.int32, (SSD_INNER, LANES), 1)
    return (jnp.right_shift(r, 6) == c).astype(BF16)


def _ssd_common(pre, dt, alog):
    q = SSD_CHUNK
    sg = _sig(pre)
    act = pre * sg
    lane = lax.broadcasted_iota(jnp.int32, (1, LANES), 1)
    a_neg = jnp.where(lane < SSD_HEADS, -jnp.exp(alog), 0.0)
    rr = lax.broadcasted_iota(jnp.int32, (q, q), 0)
    cc = lax.broadcasted_iota(jnp.int32, (q, q), 1)
    causal = rr >= cc
    cum = _sel_dot(causal.astype(BF16), dt * a_neg)
    e_mat = _expand_mat()
    dtx = _dot_sel(dt, e_mat)
    cumx = _dot_sel(cum, e_mat)
    return sg, act, a_neg, causal, cum, e_mat, dtx, cumx


def _ssd_scan(pre, dt, alog, dvec, name):
    n = pre.shape[0]
    q = SSD_CHUNK
    nc = n // q

    def body(pre_ref, dt_ref, alog_ref, d_ref, y_ref, hp_ref, state):
        i = pl.program_id(0)

        @pl.when(i == 0)
        def _():
            state[...] = jnp.zeros_like(state)

        dt_ = dt_ref[...]
        _, act, _, causal, cum, e_mat, dtx, cumx = _ssd_common(pre_ref[...], dt_, alog_ref[...])
        xs = act[:, :SSD_INNER]
        bm = act[:, SSD_INNER:SSD_INNER + LANES]
        cm = act[:, SSD_INNER + LANES:]
        cum_t = cum.T
        clx = cumx[q - 1:q, :]
        xc = xs * dtx
        xd = xc * jnp.exp(clx - cumx)
        doutx = jnp.exp(cumx)
        edec = jnp.exp(clx)
        dx_row = _dot_sel(jnp.broadcast_to(d_ref[...], (SUBLANES, LANES)), e_mat)[0:1, :]
        hp_ref[0] = state[...]
        bb = bm.astype(BF16)
        cb = cm.astype(BF16)
        lane = lax.broadcasted_iota(jnp.int32, (1, LANES), 1)
        row = lax.broadcasted_iota(jnp.int32, (LANES, 1), 0)
        cbs = []
        for g in range(2):
            cg = jnp.where(jnp.right_shift(lane, 6) == g, cm, 0.0).astype(BF16)
            cbs.append(_dot(cg, bb, NT))
        for j in range(SSD_HEADS // 2):
            sl = slice(j * LANES, (j + 1) * LANES)
            g = j // 4
            xcj = xc[:, sl].astype(BF16)
            halves = []
            for half in range(2):
                h = 2 * j + half
                seg = cum[:, h:h + 1] - cum_t[h:h + 1, :]
                w = cbs[g] * jnp.exp(jnp.where(causal, seg, -jnp.inf))
                halves.append(_dot(w.astype(BF16), xcj))
            y_diag = jnp.where(lane < SSD_HEAD_DIM, halves[0], halves[1])
            hj = state[:, sl]
            y_off = doutx[:, sl] * _dot(cb, hj.astype(BF16))
            y_ref[:, sl] = y_diag + y_off + xs[:, sl] * dx_row[:, sl]
            st = _dot(bb, xd[:, sl].astype(BF16), TN)
            state[:, sl] = hj * edec[:, sl] + jnp.where(jnp.right_shift(row, 6) == g, st, 0.0)

    return _pc(body, name=name, out_shape=(_sds((n, SSD_INNER)), _sds((nc, LANES, SSD_INNER))), grid=(nc,),
               in_specs=[_row(q, SSD_XBC), _row(q, LANES), _const((1, LANES)), _const((1, LANES))],
               out_specs=(_row(q, SSD_INNER), pl.BlockSpec((1, LANES, SSD_INNER), lambda i: (i, 0, 0))),
               scratch_shapes=[pltpu.VMEM((LANES, SSD_INNER), F32)], compiler_params=_params(1))(pre, dt, alog, dvec)


def _ssd_scan_bwd(pre, dt, hprev, dy, alog, dvec, name):
    n = pre.shape[0]
    q = SSD_CHUNK
    nc = n // q

    def body(pre_ref, dt_ref, hp_ref, dy_ref, alog_ref, d_ref, dpre_ref, ddt_ref, da_ref, dd_ref,
             d_state, dxc_s, dcx_s, dcl_s, acc_a, acc_d):
        i = pl.program_id(0)

        @pl.when(i == 0)
        def _():
            d_state[...] = jnp.zeros_like(d_state)
            acc_a[...] = jnp.zeros_like(acc_a)
            acc_d[...] = jnp.zeros_like(acc_d)

        pre_ = pre_ref[...]
        dt_ = dt_ref[...]
        sg, act, a_neg, causal, cum, e_mat, dtx, cumx = _ssd_common(pre_, dt_, alog_ref[...])
        r_mat = _reduce_mat()
        xs = act[:, :SSD_INNER]
        bm = act[:, SSD_INNER:SSD_INNER + LANES]
        cm = act[:, SSD_INNER + LANES:]
        cum_t = cum.T
        clx = cumx[q - 1:q, :]
        xc = xs * dtx
        dsx = jnp.exp(clx - cumx)
        doutx = jnp.exp(cumx)
        edec = jnp.exp(clx)
        dx_row = _dot_sel(jnp.broadcast_to(d_ref[...], (SUBLANES, LANES)), e_mat)[0:1, :]
        dy_ = dy_ref[...]
        acc_d[...] += _fold(dy_ * xs)
        bb = bm.astype(BF16)
        cb = cm.astype(BF16)
        lane = lax.broadcasted_iota(jnp.int32, (1, LANES), 1)
        row = lax.broadcasted_iota(jnp.int32, (LANES, 1), 0)
        d_c = jnp.zeros((q, LANES), F32)
        d_b = jnp.zeros((q, LANES), F32)

        for j in range(SSD_HEADS // 2):
            sl = slice(j * LANES, (j + 1) * LANES)
            g = j // 4
            hj = hp_ref[0, :, sl]
            hjb = hj.astype(BF16)
            dyj = dy_[:, sl]
            tj = _dot(cb, hjb)
            dtj = (doutx[:, sl] * dyj).astype(BF16)
            dcx = dyj * tj * doutx[:, sl]
            d_c = d_c + _dot(dtj, hjb, NT)
            dhn = d_state[:, sl]
            dhp = dhn * edec[:, sl] + jnp.where(jnp.right_shift(row, 6) == g, _dot(cb, dtj, TN), 0.0)
            dcl = jnp.sum(dhn * hj, axis=0, keepdims=True) * edec[:, sl]
            dsb = dhn.astype(BF16)
            dxd = _dot(bb, dsb)
            xcj = xc[:, sl]
            dsj = dsx[:, sl]
            d_b = d_b + _dot((xcj * dsj).astype(BF16), dsb, NT)
            dds = dxd * xcj * dsj
            d_state[:, sl] = dhp
            dxc_s[:, sl] = dxd * dsj
            dcx_s[:, sl] = dcx - dds
            dcl_s[:, sl] = jnp.broadcast_to(dcl + jnp.sum(dds, axis=0, keepdims=True), (SUBLANES, LANES))

        dcum_c = jnp.zeros((q, LANES), F32)
        dcum_t = jnp.zeros((LANES, q), F32)
        for g in range(2):
            gmask = jnp.right_shift(lane, 6) == g
            cg = jnp.where(gmask, cm, 0.0).astype(BF16)
            cbg = _dot(cg, bb, NT)
            d_cb = jnp.zeros((q, q), F32)
            for hh in range(SSD_HEADS // 2):
                h = g * (SSD_HEADS // 2) + hh
                j, half = h // 2, h % 2
                sl = slice(j * LANES, (j + 1) * LANES)
                hmask = jnp.right_shift(lane, 6) == half
                seg = cum[:, h:h + 1] - cum_t[h:h + 1, :]
                lm = jnp.exp(jnp.where(causal, seg, -jnp.inf))
                w = cbg * lm
                dyj = dy_[:, sl]
                dw = _dot(jnp.where(hmask, dyj, 0.0).astype(BF16), xc[:, sl].astype(BF16), NT)
                dxch = _dot(w.astype(BF16), dyj.astype(BF16), TN)
                dxc_s[:, sl] += jnp.where(hmask, dxch, 0.0)
                d_cb = d_cb + dw * lm
                m = dw * w
                dcum_c = dcum_c + jnp.sum(m, axis=1, keepdims=True) * (lane == h).astype(F32)
                dcum_t = dcum_t + (row == h).astype(F32) * jnp.sum(m, axis=0, keepdims=True)
            d_cbb = d_cb.astype(BF16)
            d_c = d_c + jnp.where(gmask, _dot(d_cbb, bb), 0.0)
            d_b = d_b + jnp.where(gmask, _dot(d_cbb, cb, TN), 0.0)

        dcl_row = _dot_sel(dcl_s[...], r_mat)[0:1, :]
        rowq = lax.broadcasted_iota(jnp.int32, (q, 1), 0)
        dcum = (dcum_c - dcum_t.T + _dot_sel(dcx_s[...], r_mat)
                + jnp.where(rowq == q - 1, dcl_row, 0.0))
        rr = lax.broadcasted_iota(jnp.int32, (q, q), 0)
        cc = lax.broadcasted_iota(jnp.int32, (q, q), 1)
        dadt = _sel_dot((rr <= cc).astype(BF16), dcum)
        dxc = dxc_s[...]
        ddt_ref[...] = dadt * a_neg + _dot_sel(dxc * xs, r_mat)
        acc_a[...] += _fold(dadt * dt_)
        dsilu = sg * (1.0 + pre_ * (1.0 - sg))
        dpre_ref[:, :SSD_INNER] = (dxc * dtx + dy_ * dx_row) * dsilu[:, :SSD_INNER]
        dpre_ref[:, SSD_INNER:SSD_INNER + LANES] = d_b * dsilu[:, SSD_INNER:SSD_INNER + LANES]
        dpre_ref[:, SSD_INNER + LANES:] = d_c * dsilu[:, SSD_INNER + LANES:]

        @pl.when(i == nc - 1)
        def _():
            da_ref[...] = jnp.sum(acc_a[...], axis=0, keepdims=True) * a_neg
            dd_ref[...] = jnp.sum(_dot_sel(acc_d[...], r_mat), axis=0, keepdims=True)

    rev = lambda i: (nc - 1 - i, 0)
    return _pc(body, name=name,
               out_shape=(_sds((n, SSD_XBC)), _sds((n, LANES)), _sds((1, LANES)), _sds((1, LANES))), grid=(nc,),
               in_specs=[pl.BlockSpec((q, SSD_XBC), rev), pl.BlockSpec((q, LANES), rev),
                         pl.BlockSpec((1, LANES, SSD_INNER), lambda i: (nc - 1 - i, 0, 0)),
                         pl.BlockSpec((q, SSD_INNER), rev), _const((1, LANES)), _const((1, LANES))],
               out_specs=(pl.BlockSpec((q, SSD_XBC), rev), pl.BlockSpec((q, LANES), rev), _const((1, LANES)),
                          _const((1, LANES))),
               scratch_shapes=[pltpu.VMEM((LANES, SSD_INNER), F32), pltpu.VMEM((q, SSD_INNER), F32),
                               pltpu.VMEM((q, SSD_INNER), F32), pltpu.VMEM((SUBLANES, SSD_INNER), F32),
                               pltpu.VMEM((SUBLANES, LANES), F32), pltpu.VMEM((SUBLANES, SSD_INNER), F32)],
               compiler_params=_params(1))(pre, dt, hprev, dy, alog, dvec)


def _group_norm_parts(v):
    half = SSD_INNER // 2
    r0 = lax.rsqrt(jnp.mean(v[:, :half] * v[:, :half], axis=-1, keepdims=True) + EPS)
    r1 = lax.rsqrt(jnp.mean(v[:, half:] * v[:, half:], axis=-1, keepdims=True) + EPS)
    lane = lax.broadcasted_iota(jnp.int32, (1, SSD_INNER), 1)
    return jnp.where(lane < half, r0, r1)


def _group_mean(v):
    half = SSD_INNER // 2
    m0 = jnp.mean(v[:, :half], axis=-1, keepdims=True)
    m1 = jnp.mean(v[:, half:], axis=-1, keepdims=True)
    lane = lax.broadcasted_iota(jnp.int32, (1, SSD_INNER), 1)
    return jnp.where(lane < half, m0, m1)


def _ssd_post(y, z, g, name):
    n, d = y.shape
    tl = _tile(n)

    def body(y_ref, z_ref, g_ref, o_ref):
        z_ = z_ref[...]
        v = y_ref[...] * (z_ * _sig(z_))
        o_ref[...] = ((v * _group_norm_parts(v)) * g_ref[...]).astype(BF16)

    return _pc(body, name=name, out_shape=_sds((n, d), BF16), grid=(n // tl,),
               in_specs=[_row(tl, d), _row(tl, d), _const((1, d))], out_specs=_row(tl, d),
               compiler_params=_params(1))(y, z, g)


def _ssd_post_bwd(dout, y, z, g, name):
    n, d = y.shape
    tl = _tile(n)
    nb = n // tl

    def body(do_ref, y_ref, z_ref, g_ref, dy_ref, dz_ref, dg_ref, acc_g):
        i = pl.program_id(0)

        @pl.when(i == 0)
        def _():
            acc_g[...] = jnp.zeros_like(acc_g)

        z_ = z_ref[...]
        y_ = y_ref[...]
        sz = _sig(z_)
        silu_z = z_ * sz
        v = y_ * silu_z
        rs = _group_norm_parts(v)
        nv = v * rs
        do_ = do_ref[...]
        acc_g[...] += _fold(do_ * nv)
        dn = do_ * g_ref[...]
        dv = rs * (dn - nv * _group_mean(dn * nv))
        dy_ref[...] = dv * silu_z
        dz_ref[...] = (dv * y_ * (sz * (1.0 + z_ * (1.0 - sz)))).astype(BF16)

        @pl.when(i == nb - 1)
        def _():
            dg_ref[...] = jnp.sum(acc_g[...], axis=0, keepdims=True)

    return _pc(body, name=name, out_shape=(_sds((n, d)), _sds((n, d), BF16), _sds((1, d))), grid=(nb,),
               in_specs=[_row(tl, d), _row(tl, d), _row(tl, d), _const((1, d))],
               out_specs=(_row(tl, d), _row(tl, d), _const((1, d))),
               scratch_shapes=[pltpu.VMEM((SUBLANES, d), F32)], compiler_params=_params(1))(dout, y, z, g)


def _layer_norm_parts(uc):
    mu = jnp.mean(uc, axis=-1, keepdims=True)
    xc = uc - mu
    rstd = lax.rsqrt(jnp.mean(xc * xc, axis=-1, keepdims=True) + EPS)
    return xc * rstd, rstd


def _conf_fwd(conf_in, conv_w, conv_b, ln_g, ln_b, name):
    n = conf_in.shape[0]
    c = CONF_WIDTH
    tl = _tile(n)
    hb = CONF_HALO
    k_taps = CONF_KERNEL

    def body(x_ref, xp_ref, w_ref, b_ref, g_ref, beta_ref, o_ref, uc_ref, buf):
        i = pl.program_id(0)
        xp = xp_ref[...]
        buf[pl.ds(0, hb), :] = jnp.where(i > 0, xp[:, :c] * _sig(xp[:, c:]), 0.0)
        x_ = x_ref[...]
        buf[pl.ds(hb, tl), :] = x_[:, :c] * _sig(x_[:, c:])
        acc = b_ref[...] + w_ref[0:1, :] * buf[pl.ds(hb - (k_taps - 1), tl), :]
        for k in range(1, k_taps):
            acc = acc + w_ref[k:k + 1, :] * buf[pl.ds(hb - (k_taps - 1) + k, tl), :]
        uc_ref[...] = acc
        nv, _ = _layer_norm_parts(acc)
        v = nv * g_ref[...] + beta_ref[...]
        o_ref[...] = (v * _sig(v)).astype(BF16)

    return _pc(body, name=name, out_shape=(_sds((n, c), BF16), _sds((n, c))), grid=(n // tl,),
               in_specs=[_row(tl, 2 * c), _prev(tl, hb, 2 * c), _const((hb, c)), _const((1, c)), _const((1, c)),
                         _const((1, c))],
               out_specs=(_row(tl, c), _row(tl, c)),
               scratch_shapes=[pltpu.VMEM((tl + hb, c), F32)], compiler_params=_params(1))(
        conf_in, conf_in, conv_w, conv_b, ln_g, ln_b)


def _conf_bwd(dout, uc, conf_in, conv_w, ln_g, ln_b, name):
    n = conf_in.shape[0]
    c = CONF_WIDTH
    tl = _tile(n)
    nb = n // tl
    hb = CONF_HALO
    k_taps = CONF_KERNEL

    def body(do_ref, don_ref, uc_ref, ucn_ref, x_ref, xp_ref, w_ref, g_ref, beta_ref,
             dx_ref, dw_ref, db_ref, dg_ref, dbeta_ref, dbuf, ubuf, acc_w, acc_b, acc_g, acc_beta):
        i = pl.program_id(0)

        @pl.when(i == 0)
        def _():
            acc_w[...] = jnp.zeros_like(acc_w)
            acc_b[...] = jnp.zeros_like(acc_b)
            acc_g[...] = jnp.zeros_like(acc_g)
            acc_beta[...] = jnp.zeros_like(acc_beta)

        g_ = g_ref[...]
        beta_ = beta_ref[...]

        def d_conv_out(do_, uc_):
            nv, rstd = _layer_norm_parts(uc_)
            v = nv * g_ + beta_
            sv = _sig(v)
            dv = do_ * (sv * (1.0 + v * (1.0 - sv)))
            dn = dv * g_
            duc = rstd * (dn - jnp.mean(dn, axis=-1, keepdims=True)
                          - nv * jnp.mean(dn * nv, axis=-1, keepdims=True))
            return duc, dv, nv

        duc, dv, nv = d_conv_out(do_ref[...], uc_ref[...])
        acc_g[...] += _fold(dv * nv)
        acc_beta[...] += _fold(dv)
        acc_b[...] += _fold(duc)
        dbuf[pl.ds(0, tl), :] = duc
        ducn, _, _ = d_conv_out(don_ref[...], ucn_ref[...])
        dbuf[pl.ds(tl, hb), :] = jnp.where(i < nb - 1, ducn, 0.0)
        xp = xp_ref[...]
        ubuf[pl.ds(0, hb), :] = jnp.where(i > 0, xp[:, :c] * _sig(xp[:, c:]), 0.0)
        x_ = x_ref[...]
        val = x_[:, :c]
        sgate = _sig(x_[:, c:])
        ubuf[pl.ds(hb, tl), :] = val * sgate
        du = w_ref[0:1, :] * dbuf[pl.ds(k_taps - 1, tl), :]
        for k in range(1, k_taps):
            du = du + w_ref[k:k + 1, :] * dbuf[pl.ds(k_taps - 1 - k, tl), :]
        for k in range(k_taps):
            acc_w[k] += _fold(duc * ubuf[pl.ds(hb - (k_taps - 1) + k, tl), :])
        dx_ref[:, :c] = (du * sgate).astype(BF16)
        dx_ref[:, c:] = (du * val * sgate * (1.0 - sgate)).astype(BF16)

        @pl.when(i == nb - 1)
        def _():
            dw_ref[...] = jnp.zeros_like(dw_ref)
            for k in range(k_taps):
                dw_ref[k:k + 1, :] = jnp.sum(acc_w[k], axis=0, keepdims=True)
            db_ref[...] = jnp.sum(acc_b[...], axis=0, keepdims=True)
            dg_ref[...] = jnp.sum(acc_g[...], axis=0, keepdims=True)
            dbeta_ref[...] = jnp.sum(acc_beta[...], axis=0, keepdims=True)

    vec = _sds((1, c))
    return _pc(body, name=name, out_shape=(_sds((n, 2 * c), BF16), _sds((hb, c)), vec, vec, vec), grid=(nb,),
               in_specs=[_row(tl, c), _next(tl, hb, c, n), _row(tl, c), _next(tl, hb, c, n), _row(tl, 2 * c),
                         _prev(tl, hb, 2 * c), _const((hb, c)), _const((1, c)), _const((1, c))],
               out_specs=(_row(tl, 2 * c), _const((hb, c)), _const((1, c)), _const((1, c)), _const((1, c))),
               scratch_shapes=[pltpu.VMEM((tl + hb, c), F32), pltpu.VMEM((tl + hb, c), F32),
                               pltpu.VMEM((k_taps, SUBLANES, c), F32), pltpu.VMEM((SUBLANES, c), F32),
                               pltpu.VMEM((SUBLANES, c), F32), pltpu.VMEM((SUBLANES, c), F32)],
               compiler_params=_params(1))(dout, dout, uc, uc, conf_in, conf_in, conv_w, ln_g, ln_b)


def _sc_fwd(sc_in, conv_w, name):
    n = sc_in.shape[0]
    c = SC_WIDTH
    tl = _tile(n)
    hb = CONV_HALO

    def body(x_ref, xp_ref, w_ref, o_ref, buf):
        i = pl.program_id(0)
        xp = xp_ref[...]
        buf[pl.ds(0, hb), :] = jnp.where(i > 0, xp[:, c:2 * c] * xp[:, 2 * c:], 0.0)
        x_ = x_ref[...]
        buf[pl.ds(hb, tl), :] = x_[:, c:2 * c] * x_[:, 2 * c:]
        cv = w_ref[0:1, :] * buf[pl.ds(hb - 2, tl), :]
        for k in range(1, 3):
            cv = cv + w_ref[k:k + 1, :] * buf[pl.ds(hb - 2 + k, tl), :]
        o_ref[...] = (x_[:, :c] * cv).astype(BF16)

    return _pc(body, name=name, out_shape=_sds((n, c), BF16), grid=(n // tl,),
               in_specs=[_row(tl, 3 * c), _prev(tl, hb, 3 * c), _const((SUBLANES, c))], out_specs=_row(tl, c),
               scratch_shapes=[pltpu.VMEM((tl + hb, c), F32)], compiler_params=_params(1))(sc_in, sc_in, conv_w)


def _sc_bwd(dout, sc_in, conv_w, name):
    n = sc_in.shape[0]
    c = SC_WIDTH
    tl = _tile(n)
    nb = n // tl
    hb = CONV_HALO

    def body(do_ref, don_ref, x_ref, xp_ref, xn_ref, w_ref, dx_ref, dw_ref, dbuf, pbuf, acc_w):
        i = pl.program_id(0)

        @pl.when(i == 0)
        def _():
            acc_w[...] = jnp.zeros_like(acc_w)

        x_ = x_ref[...]
        gb, gc, xv = x_[:, :c], x_[:, c:2 * c], x_[:, 2 * c:]
        do_ = do_ref[...]
        dcv = do_ * gb
        dbuf[pl.ds(0, tl), :] = dcv
        dbuf[pl.ds(tl, hb), :] = jnp.where(i < nb - 1, don_ref[...] * xn_ref[...], 0.0)
        xp = xp_ref[...]
        pbuf[pl.ds(0, hb), :] = jnp.where(i > 0, xp[:, c:2 * c] * xp[:, 2 * c:], 0.0)
        pbuf[pl.ds(hb, tl), :] = gc * xv
        cv = w_ref[0:1, :] * pbuf[pl.ds(hb - 2, tl), :]
        dp = w_ref[0:1, :] * dbuf[pl.ds(2, tl), :]
        for k in range(1, 3):
            cv = cv + w_ref[k:k + 1, :] * pbuf[pl.ds(hb - 2 + k, tl), :]
            dp = dp + w_ref[k:k + 1, :] * dbuf[pl.ds(2 - k, tl), :]
        for k in range(3):
            acc_w[k] += _fold(dcv * pbuf[pl.ds(hb - 2 + k, tl), :])
        dx_ref[:, :c] = (do_ * cv).astype(BF16)
        dx_ref[:, c:2 * c] = (dp * xv).astype(BF16)
        dx_ref[:, 2 * c:] = (dp * gc).astype(BF16)

        @pl.when(i == nb - 1)
        def _():
            dw_ref[...] = jnp.zeros_like(dw_ref)
            for k in range(3):
                dw_ref[k:k + 1, :] = jnp.sum(acc_w[k], axis=0, keepdims=True)

    return _pc(body, name=name, out_shape=(_sds((n, 3 * c), BF16), _sds((SUBLANES, c))), grid=(nb,),
               in_specs=[_row(tl, c), _next(tl, hb, c, n), _row(tl, 3 * c), _prev(tl, hb, 3 * c),
                         _next(tl, hb, c, n), _const((SUBLANES, c))],
               out_specs=(_row(tl, 3 * c), _const((SUBLANES, c))),
               scratch_shapes=[pltpu.VMEM((tl + hb, c), F32), pltpu.VMEM((tl + hb, c), F32),
                               pltpu.VMEM((3, SUBLANES, c), F32)],
               compiler_params=_params(1))(dout, dout, sc_in, sc_in, sc_in, conv_w)


def _merge_fwd(gates, ya, yb, yc, b_gate, name):
    n, d = ya.shape
    tl = _tile(n)

    def body(gt_ref, ya_ref, yb_ref, yc_ref, b_ref, o_ref):
        gt = _sig(gt_ref[...] + b_ref[...])
        o_ref[...] = (gt[:, :d] * ya_ref[...] + gt[:, d:2 * d] * yb_ref[...] + gt[:, 2 * d:] * yc_ref[...]).astype(BF16)

    return _pc(body, name=name, out_shape=_sds((n, d), BF16), grid=(n // tl,),
               in_specs=[_row(tl, 3 * d), _row(tl, d), _row(tl, d), _row(tl, d), _const((1, 3 * d))],
               out_specs=_row(tl, d), compiler_params=_params(1))(gates, ya, yb, yc, b_gate)


def _merge_bwd(dm, gates, ya, yb, yc, b_gate, name):
    n, d = ya.shape
    tl = _tile(n)
    nb = n // tl

    def body(dm_ref, gt_ref, ya_ref, yb_ref, yc_ref, b_ref, dya_ref, dyb_ref, dyc_ref, dgt_ref, db_ref, acc):
        i = pl.program_id(0)

        @pl.when(i == 0)
        def _():
            acc[...] = jnp.zeros_like(acc)

        dm_ = dm_ref[...]
        gt = _sig(gt_ref[...] + b_ref[...])
        for idx, (y_ref, dy_ref) in enumerate(((ya_ref, dya_ref), (yb_ref, dyb_ref), (yc_ref, dyc_ref))):
            gk = gt[:, idx * d:(idx + 1) * d]
            dy_ref[...] = (dm_ * gk).astype(BF16)
            dpre = dm_ * y_ref[...] * gk * (1.0 - gk)
            dgt_ref[:, idx * d:(idx + 1) * d] = dpre.astype(BF16)
            acc[:, idx * d:(idx + 1) * d] += _fold(dpre)

        @pl.when(i == nb - 1)
        def _():
            db_ref[...] = jnp.sum(acc[...], axis=0, keepdims=True)

    bf = _sds((n, d), BF16)
    return _pc(body, name=name, out_shape=(bf, bf, bf, _sds((n, 3 * d), BF16), _sds((1, 3 * d))), grid=(nb,),
               in_specs=[_row(tl, d), _row(tl, 3 * d), _row(tl, d), _row(tl, d), _row(tl, d), _const((1, 3 * d))],
               out_specs=(_row(tl, d), _row(tl, d), _row(tl, d), _row(tl, 3 * d), _const((1, 3 * d))),
               scratch_shapes=[pltpu.VMEM((SUBLANES, 3 * d), F32)], compiler_params=_params(1))(
        dm, gates, ya, yb, yc, b_gate)


FFN_COLS = 1408
FFN_STRIP = 64
FFN_STRIP_BWD = 32


def _down(prev, cur, s, row):
    return jnp.where(row < s, pltpu.roll(prev, s, 0), pltpu.roll(cur, s, 0))


def _up(cur, nxt, s, row):
    return jnp.where(row < SUBLANES - s, pltpu.roll(cur, SUBLANES - s, 0), pltpu.roll(nxt, SUBLANES - s, 0))


def _ffn_mid(up, conv_w, conv_b, name):
    n = up.shape[0]
    tl = _tile(n)
    hb = CONV_HALO
    tc = FFN_COLS
    ncb = D_FF // tc

    def spec(shape_rows, idx_fn, off):
        return pl.BlockSpec((shape_rows, tc), lambda j, i, off=off: (idx_fn(i), j + off))

    r = tl // hb
    cur = lambda i: i
    prv = lambda i: jnp.maximum(i * r - 1, 0)

    def body(g_ref, gp_ref, v_ref, vp_ref, wg_ref, wv_ref, bg_ref, bv_ref, o_ref):
        i = pl.program_id(1)
        row = lax.broadcasted_iota(jnp.int32, (SUBLANES, LANES), 0)
        full = lambda ref, k, ls: jnp.broadcast_to(ref[k:k + 1, ls], (SUBLANES, LANES))
        for lc in range(tc // LANES):
            ls = slice(lc * LANES, (lc + 1) * LANES)
            wg = [full(wg_ref, k, ls) for k in range(3)]
            wv = [full(wv_ref, k, ls) for k in range(3)]
            bg, bv = full(bg_ref, 0, ls), full(bv_ref, 0, ls)

            def conv(prev, x, w, b):
                return b + w[0] * _down(prev, x, 2, row) + w[1] * _down(prev, x, 1, row) + w[2] * x

            def strip(t, carry):
                gs, vs = [carry[0]], [carry[1]]
                for u in range(FFN_STRIP // SUBLANES):
                    rows = pl.ds(pl.multiple_of(t * FFN_STRIP + u * SUBLANES, SUBLANES), SUBLANES)
                    gs.append(g_ref[rows, ls])
                    vs.append(v_ref[rows, ls])
                outs = []
                for u in range(FFN_STRIP // SUBLANES):
                    ug = conv(gs[u], gs[u + 1], wg, bg)
                    outs.append(ug * _sig(ug) * conv(vs[u], vs[u + 1], wv, bv))
                for p in range(FFN_STRIP // 16):
                    rows = pl.ds(pl.multiple_of(t * FFN_STRIP + p * 16, 16), 16)
                    o_ref[rows, ls] = jnp.concatenate(outs[2 * p:2 * p + 2], axis=0).astype(BF16)
                return gs[-1], vs[-1]

            lax.fori_loop(0, tl // FFN_STRIP, strip,
                          (jnp.where(i > 0, gp_ref[:, ls], 0.0), jnp.where(i > 0, vp_ref[:, ls], 0.0)))

    wspec = lambda off: pl.BlockSpec((SUBLANES, tc), lambda j, i, off=off: (0, j + off))
    bspec = lambda off: pl.BlockSpec((1, tc), lambda j, i, off=off: (0, j + off))
    return _pc(body, name=name, out_shape=_sds((n, D_FF), BF16), grid=(ncb, n // tl),
               in_specs=[spec(tl, cur, 0), spec(hb, prv, 0), spec(tl, cur, ncb), spec(hb, prv, ncb),
                         wspec(0), wspec(ncb), bspec(0), bspec(ncb)],
               out_specs=pl.BlockSpec((tl, tc), lambda j, i: (i, j)), compiler_params=_params(2))(
        up, up, up, up, conv_w, conv_w, conv_b, conv_b)


def _ffn_mid_bwd(da, up, conv_w, conv_b, name):
    n = up.shape[0]
    tl = _tile(n)
    nb = n // tl
    nt = tl // FFN_STRIP_BWD
    hb = CONV_HALO
    tc = FFN_COLS
    ncb = D_FF // tc
    r = tl // hb
    last = n // hb - 1
    cur = lambda i: i
    prv = lambda i: jnp.maximum(i * r - 1, 0)
    nxt = lambda i: jnp.minimum((i + 1) * r, last)

    def spec(shape_rows, idx_fn, off):
        return pl.BlockSpec((shape_rows, tc), lambda j, i, off=off: (idx_fn(i), j + off))

    def body(da_ref, dan_ref, g_ref, gp_ref, gn_ref, v_ref, vp_ref, vn_ref, wg_ref, wv_ref, bg_ref, bv_ref,
             dg_ref, dv_ref, dwg_ref, dwv_ref, dbg_ref, dbv_ref, acc_w, acc_b):
        i = pl.program_id(1)

        @pl.when(i == 0)
        def _():
            acc_w[...] = jnp.zeros_like(acc_w)
            acc_b[...] = jnp.zeros_like(acc_b)

        row = lax.broadcasted_iota(jnp.int32, (SUBLANES, LANES), 0)
        full = lambda ref, k, ls: jnp.broadcast_to(ref[k:k + 1, ls], (SUBLANES, LANES))
        zero = jnp.zeros((SUBLANES, LANES), F32)

        def d_conv_out(da_, ug, uv):
            s = _sig(ug)
            return da_ * uv * (s * (1.0 + ug * (1.0 - s))), da_ * (ug * s)

        for lc in range(tc // LANES):
            ls = slice(lc * LANES, (lc + 1) * LANES)
            wg = [full(wg_ref, k, ls) for k in range(3)]
            wv = [full(wv_ref, k, ls) for k in range(3)]
            bg, bv = full(bg_ref, 0, ls), full(bv_ref, 0, ls)

            def unit(prev_g, g, prev_v, v, da_):
                g1, g2 = _down(prev_g, g, 1, row), _down(prev_g, g, 2, row)
                v1, v2 = _down(prev_v, v, 1, row), _down(prev_v, v, 2, row)
                ug = bg + wg[0] * g2 + wg[1] * g1 + wg[2] * g
                uv = bv + wv[0] * v2 + wv[1] * v1 + wv[2] * v
                dug, duv = d_conv_out(da_, ug, uv)
                return dug, duv, (g2, g1, g), (v2, v1, v)

            def d_in(d, d_next, w):
                return w[2] * d + w[1] * _up(d, d_next, 1, row) + w[0] * _up(d, d_next, 2, row)

            tail = pl.ds(tl - SUBLANES, SUBLANES)
            dgn, dvn, _, _ = unit(g_ref[tail, ls], gn_ref[:, ls], v_ref[tail, ls], vn_ref[:, ls], dan_ref[:, ls])
            dgn = jnp.where(i < nb - 1, dgn, 0.0)
            dvn = jnp.where(i < nb - 1, dvn, 0.0)
            gp0 = jnp.where(i > 0, gp_ref[:, ls], 0.0)
            vp0 = jnp.where(i > 0, vp_ref[:, ls], 0.0)

            def strip(tt, carry):
                dgn, dvn = carry[0], carry[1]
                aw, ab = list(carry[2:8]), list(carry[8:10])
                t = nt - 1 - tt
                nu = FFN_STRIP_BWD // SUBLANES
                rm = pl.multiple_of(jnp.maximum(t * FFN_STRIP_BWD - SUBLANES, 0), SUBLANES)
                gs = [jnp.where(t > 0, g_ref[pl.ds(rm, SUBLANES), ls], gp0)]
                vs = [jnp.where(t > 0, v_ref[pl.ds(rm, SUBLANES), ls], vp0)]
                das = []
                for u in range(nu):
                    rows = pl.ds(pl.multiple_of(t * FFN_STRIP_BWD + u * SUBLANES, SUBLANES), SUBLANES)
                    gs.append(g_ref[rows, ls])
                    vs.append(v_ref[rows, ls])
                    das.append(da_ref[rows, ls])
                dgs, dvs = [None] * nu + [dgn], [None] * nu + [dvn]
                for u in reversed(range(nu)):
                    dgs[u], dvs[u], gsh, vsh = unit(gs[u], gs[u + 1], vs[u], vs[u + 1], das[u])
                    for k in range(3):
                        aw[k] = aw[k] + dgs[u] * gsh[k]
                        aw[3 + k] = aw[3 + k] + dvs[u] * vsh[k]
                    ab[0] = ab[0] + dgs[u]
                    ab[1] = ab[1] + dvs[u]
                for p in range(nu // 2):
                    rows = pl.ds(pl.multiple_of(t * FFN_STRIP_BWD + p * 16, 16), 16)
                    dg_ref[rows, ls] = jnp.concatenate([d_in(dgs[2 * p], dgs[2 * p + 1], wg),
                                                        d_in(dgs[2 * p + 1], dgs[2 * p + 2], wg)], axis=0).astype(BF16)
                    dv_ref[rows, ls] = jnp.concatenate([d_in(dvs[2 * p], dvs[2 * p + 1], wv),
                                                        d_in(dvs[2 * p + 1], dvs[2 * p + 2], wv)], axis=0).astype(BF16)
                return (dgs[0], dvs[0], *aw, *ab)

            res = lax.fori_loop(0, nt, strip, (dgn, dvn) + (zero,) * 8)
            for k in range(3):
                acc_w[0, k, :, ls] += res[2 + k]
                acc_w[1, k, :, ls] += res[5 + k]
            acc_b[0, :, ls] += res[8]
            acc_b[1, :, ls] += res[9]

        @pl.when(i == nb - 1)
        def _():
            for t, (dw_ref, db_ref) in enumerate(((dwg_ref, dbg_ref), (dwv_ref, dbv_ref))):
                dw_ref[...] = jnp.zeros_like(dw_ref)
                for k in range(3):
                    dw_ref[k:k + 1, :] = jnp.sum(acc_w[t, k], axis=0, keepdims=True)
                db_ref[...] = jnp.sum(acc_b[t], axis=0, keepdims=True)

    wspec = lambda off: pl.BlockSpec((SUBLANES, tc), lambda j, i, off=off: (0, j + off))
    bspec = lambda off: pl.BlockSpec((1, tc), lambda j, i, off=off: (0, j + off))
    ospec = lambda off: pl.BlockSpec((tl, tc), lambda j, i, off=off: (i, j + off))
    dg, dv, dwg, dwv, dbg, dbv = _pc(
        body, name=name,
        out_shape=(_sds((n, D_FF), BF16), _sds((n, D_FF), BF16), _sds((SUBLANES, D_FF)), _sds((SUBLANES, D_FF)),
                   _sds((1, D_FF)), _sds((1, D_FF))),
        grid=(ncb, nb),
        in_specs=[spec(tl, cur, 0), spec(hb, nxt, 0),
                  spec(tl, cur, 0), spec(hb, prv, 0), spec(hb, nxt, 0),
                  spec(tl, cur, ncb), spec(hb, prv, ncb), spec(hb, nxt, ncb),
                  wspec(0), wspec(ncb), bspec(0), bspec(ncb)],
        out_specs=(ospec(0), ospec(0), wspec(0), wspec(0), bspec(0), bspec(0)),
        scratch_shapes=[pltpu.VMEM((2, 3, SUBLANES, tc), F32), pltpu.VMEM((2, SUBLANES, tc), F32)],
        compiler_params=_params(2))(da, da, up, up, up, up, up, up, conv_w, conv_w, conv_b, conv_b)
    return dg, dv, jnp.concatenate([dwg, dwv], axis=1), jnp.concatenate([dbg, dbv], axis=1)


def _position():
    return lax.axis_index("x"), lax.axis_index("y"), lax.axis_index("c")


def _all_gather(locals_, name):
    n = len(locals_)

    def body(*refs):
        x_refs, out_refs = refs[:n], refs[n:2 * n]
        send_sems, recv_sems, local_sems = refs[2 * n:]
        x, y, cc = _position()
        me, sibling = (x, y, cc), (x, y, 1 - cc)
        chips = [(1 - x, y), (x, 1 - y), (1 - x, 1 - y)]

        def slot(a, px, py, pc):
            return out_refs[a].at[4 * px + 2 * py + pc]

        def copy(k, a, block, to, own=False):
            return pltpu.make_async_remote_copy(
                src_ref=x_refs[a] if own else slot(a, *block), dst_ref=slot(a, *block),
                send_sem=send_sems.at[k, a], recv_sem=recv_sems.at[k, a], device_id=to, device_id_type=MESH)

        mine = [pltpu.make_async_copy(x_refs[a], slot(a, *me), local_sems.at[a]) for a in range(n)]
        first = [copy(1 + j, a, me, (*chip, cc), own=True) for j, chip in enumerate(chips) for a in range(n)]
        first += [copy(0, a, me, sibling, own=True) for a in range(n)]
        for cp in mine + first:
            cp.start()
        passed = []
        for j, chip in enumerate(chips):
            for a in range(n):
                copy(1 + j, a, (*chip, cc), me).wait_recv()
                cp = copy(4 + j, a, (*chip, cc), sibling)
                cp.start()
                passed.append(cp)
        for a in range(n):
            copy(0, a, sibling, me).wait_recv()
        for j, chip in enumerate(chips):
            for a in range(n):
                copy(4 + j, a, (*chip, 1 - cc), me).wait_recv()
        for cp in first + passed:
            cp.wait_send()
        for cp in mine:
            cp.wait()

    hbm = pl.BlockSpec(memory_space=pl.ANY)
    return _pc(body, name=name, out_shape=[_sds((N_DEV,) + a.shape, a.dtype) for a in locals_],
               in_specs=[hbm] * n, out_specs=[hbm] * n,
               scratch_shapes=[pltpu.SemaphoreType.DMA((7, n)), pltpu.SemaphoreType.DMA((7, n)),
                               pltpu.SemaphoreType.DMA((n,))])(*locals_)


def _peers():
    x, y, cc = _position()
    others = []
    for fx, fy, fc in ((0, 0, 1), (1, 0, 0), (0, 1, 0), (1, 1, 0), (1, 0, 1), (0, 1, 1), (1, 1, 1)):
        p = (1 - x if fx else x, 1 - y if fy else y, 1 - cc if fc else cc)
        others.append((p, 4 * p[0] + 2 * p[1] + p[2]))
    return 4 * x + 2 * y + cc, others


def _gather_start(locals_, after, name):
    n = len(locals_)
    me, _ = _peers()
    lands = [lax.dynamic_update_slice(lax.empty((N_DEV,) + a.shape, a.dtype), a[None], (me,) + (0,) * a.ndim)
             for a in locals_]

    def body(*refs):
        x_refs, land_refs = refs[:n], refs[n:2 * n]
        send_sems, recv_sems, token = refs[2 * n + 1], refs[2 * n + 2], refs[-1]
        me_idx, others = _peers()
        for k, (peer, _) in enumerate(others):
            for a in range(n):
                pltpu.make_async_remote_copy(
                    src_ref=x_refs[a], dst_ref=land_refs[a].at[me_idx], send_sem=send_sems.at[k * n + a],
                    recv_sem=recv_sems.at[k * n + a], device_id=peer, device_id_type=MESH).start()
        token[...] = jnp.zeros_like(token)

    hbm = pl.BlockSpec(memory_space=pltpu.HBM)
    sem = pl.BlockSpec(memory_space=pltpu.SEMAPHORE)
    out = _pc(body, name=name,
              out_shape=(pltpu.SemaphoreType.DMA((7 * n,)), pltpu.SemaphoreType.DMA((7 * n,)),
                         *[pltpu.HBM(a.shape, a.dtype) for a in locals_], *[pltpu.HBM(l.shape, l.dtype) for l in lands],
                         _sds((SUBLANES, LANES))),
              in_specs=[hbm] * (2 * n) + [pl.BlockSpec(memory_space=pl.ANY)],
              out_specs=(sem, sem, *([hbm] * (2 * n)), pl.BlockSpec(memory_space=pltpu.VMEM)),
              input_output_aliases={i: 2 + i for i in range(2 * n)},
              compiler_params=pltpu.CompilerParams(has_side_effects=pltpu.SideEffectType.DATAFLOW_SIDE_EFFECTING))(
        *[pltpu.with_memory_space_constraint(a, pltpu.HBM) for a in locals_],
        *[pltpu.with_memory_space_constraint(l, pltpu.HBM) for l in lands], after)
    return (out[0], out[1], list(out[2:2 + n]), list(out[2 + n:2 + 2 * n])), out[-1]


def _gather_wait(state, after, name):
    send_sems, recv_sems, x_thru, land_thru = state
    n = len(x_thru)

    def body(*refs):
        x_refs, land_refs = refs[:n], refs[n:2 * n]
        send_sems, recv_sems = refs[2 * n], refs[2 * n + 1]
        _, others = _peers()
        for k, (peer, peer_idx) in enumerate(others):
            for a in range(n):
                cp = pltpu.make_async_remote_copy(
                    src_ref=x_refs[a], dst_ref=land_refs[a].at[peer_idx], send_sem=send_sems.at[k * n + a],
                    recv_sem=recv_sems.at[k * n + a], device_id=peer, device_id_type=MESH)
                cp.wait_send()
                cp.wait_recv()

    hbm = pl.BlockSpec(memory_space=pltpu.HBM)
    sem = pl.BlockSpec(memory_space=pltpu.SEMAPHORE)
    out = _pc(body, name=name, out_shape=tuple(pltpu.HBM(a.shape, a.dtype) for a in x_thru + land_thru),
              in_specs=[hbm] * (2 * n) + [sem, sem, pl.BlockSpec(memory_space=pl.ANY)], out_specs=tuple([hbm] * (2 * n)),
              input_output_aliases={i: i for i in range(2 * n)},
              compiler_params=pltpu.CompilerParams(has_side_effects=pltpu.SideEffectType.DATAFLOW_SIDE_EFFECTING))(
        *x_thru, *land_thru, send_sems, recv_sems, after)
    return list(out[n:])


N_CHIPS = 4


def _sibling_swap(parts, name):
    n = len(parts)

    def body(*refs):
        g_refs, got_refs = refs[:n], refs[n:2 * n]
        send_sems, recv_sems = refs[2 * n:]
        x, y, cc = _position()
        swaps = []
        for q in range(N_CHIPS):
            for a in range(n):
                swaps.append(pltpu.make_async_remote_copy(
                    src_ref=g_refs[a].at[2 * q + 1 - cc], dst_ref=got_refs[a].at[q], send_sem=send_sems.at[q, a],
                    recv_sem=recv_sems.at[q, a], device_id=(x, y, 1 - cc), device_id_type=MESH))
        for cp in swaps:
            cp.start()
        for cp in swaps:
            cp.wait_recv()
        for cp in swaps:
            cp.wait_send()

    hbm = pl.BlockSpec(memory_space=pl.ANY)
    return _pc(body, name=name, out_shape=[_sds((N_CHIPS,) + a.shape[1:], a.dtype) for a in parts],
               in_specs=[hbm] * n, out_specs=[hbm] * n,
               scratch_shapes=[pltpu.SemaphoreType.DMA((N_CHIPS, n)), pltpu.SemaphoreType.DMA((N_CHIPS, n))])(*parts)


def _pair_add(part, got, core, name):
    q, s, a, b = got.shape
    ta = _block_rows(a, b)

    def body(core_ref, k_ref, g_ref, o_ref):
        o_ref[...] = (k_ref[...].astype(F32) + g_ref[...].astype(F32)).astype(BF16)

    spec = pl.BlockSpec((None, None, ta, b), lambda c, l, i, core_ref: (c, l, i, 0))
    own = pl.BlockSpec((None, None, None, ta, b), lambda c, l, i, core_ref: (c, core_ref[0], l, i, 0))
    grid_spec = pltpu.PrefetchScalarGridSpec(num_scalar_prefetch=1, grid=(q, s, a // ta), in_specs=[own, spec],
                                             out_specs=spec)
    return _pc(body, name=name, out_shape=_sds(got.shape, BF16), grid_spec=grid_spec, compiler_params=_params(3))(
        core, part.reshape((N_CHIPS, 2) + part.shape[1:]), got)


def _chip_exchange(sums, name):
    n = len(sums)
    flips = [(1, 0), (0, 1), (1, 1)]

    def body(*refs):
        g_refs, out_refs = refs[:n], refs[n:2 * n]
        send_sems, recv_sems, local_sems = refs[2 * n:]
        x, y, cc = _position()
        me_q = 2 * x + y
        mine = [pltpu.make_async_copy(g_refs[a].at[me_q], out_refs[a].at[me_q], local_sems.at[a]) for a in range(n)]
        sends, recvs = [], []
        for k, (fx, fy) in enumerate(flips):
            px, py = (1 - x if fx else x), (1 - y if fy else y)
            peer_q = 2 * px + py
            for a in range(n):
                sends.append(pltpu.make_async_remote_copy(
                    src_ref=g_refs[a].at[peer_q], dst_ref=out_refs[a].at[me_q], send_sem=send_sems.at[k, a],
                    recv_sem=recv_sems.at[k, a], device_id=(px, py, cc), device_id_type=MESH))
                recvs.append(pltpu.make_async_remote_copy(
                    src_ref=g_refs[a].at[me_q], dst_ref=out_refs[a].at[peer_q], send_sem=send_sems.at[k, a],
                    recv_sem=recv_sems.at[k, a], device_id=(px, py, cc), device_id_type=MESH))
        for cp in sends + mine:
            cp.start()
        for cp in recvs:
            cp.wait_recv()
        for cp in sends:
            cp.wait_send()
        for cp in mine:
            cp.wait()

    hbm = pl.BlockSpec(memory_space=pl.ANY)
    return _pc(body, name=name, out_shape=[_sds(a.shape, a.dtype) for a in sums],
               in_specs=[hbm] * n, out_specs=[hbm] * n,
               scratch_shapes=[pltpu.SemaphoreType.DMA((3, n)), pltpu.SemaphoreType.DMA((3, n)),
                               pltpu.SemaphoreType.DMA((n,))])(*sums)


def _block_rows(a, b):
    ta = a
    while ta * b > 256 * 1024 and ta % 32 == 0:
        ta //= 2
    return ta


def _reduce_adamw(parts, w, m, v, name):
    n_parts, s, a, b = parts.shape
    ta = _block_rows(a, b)
    c1 = 1.0 - ADAM_B1 ** ADAM_STEP
    c2 = 1.0 - ADAM_B2 ** ADAM_STEP

    def body(p_ref, w_ref, m_ref, v_ref, g_out, d_out, m_out, v_out):
        g = p_ref[0].astype(F32)
        for j in range(1, n_parts):
            g = g + p_ref[j].astype(F32)
        m_new = ADAM_B1 * m_ref[...] + (1.0 - ADAM_B1) * g
        v_new = ADAM_B2 * v_ref[...] + (1.0 - ADAM_B2) * (g * g)
        m_hat = m_new / c1
        v_hat = v_new / c2
        g_out[...] = g
        d_out[...] = -ADAM_LR * (m_hat / (jnp.sqrt(v_hat) + ADAM_EPS) + ADAM_WD * w_ref[...])
        m_out[...] = m_new
        v_out[...] = v_new

    spec = pl.BlockSpec((None, ta, b), lambda l, i: (l, i, 0))
    return _pc(body, name=name, out_shape=(_sds((s, a, b)),) * 4, grid=(s, a // ta),
               in_specs=[pl.BlockSpec((n_parts, None, ta, b), lambda l, i: (0, l, i, 0)), spec, spec, spec],
               out_specs=(spec,) * 4, compiler_params=_params(2))(parts, w, m, v)


MATRICES = (("ada_mix_w", 2), ("w_in", 2), ("w_ssd_out", 1), ("w_conf_out", 2), ("w_sc_out", 2), ("w_o", 1),
            ("ada_ffn_w", 2), ("w_up", 2), ("w_down", 1))
MIXER_MATRICES = ("ada_mix_w", "w_in", "w_ssd_out", "w_conf_out", "w_sc_out", "w_o")
FFN_MATRICES = ("ada_ffn_w", "w_up", "w_down")
CONV_WEIGHTS = (("ssd_conv_w", 2), ("conf_conv_w", 2), ("sc_conv_w", 2), ("ffn_conv_w", 2))
SHARDED = MATRICES + CONV_WEIGHTS
REPLICATED = ("ada_mix_b", "norm_mix_g", "b_gate", "ssd_conv_b", "ssd_dt_bias", "ssd_a_log", "ssd_d", "ssd_norm_g",
              "conf_conv_b", "conf_ln_g", "conf_ln_b", "ada_ffn_b", "norm_ffn_g", "ffn_conv_b", "final_norm_g")
WEIGHT_NAMES = ("ada_mix_w", "ada_mix_b", "norm_mix_g", "w_in", "b_gate", "ssd_conv_w", "ssd_conv_b", "ssd_dt_bias",
                "ssd_a_log", "ssd_d", "ssd_norm_g", "w_ssd_out", "conf_conv_w", "conf_conv_b", "conf_ln_g",
                "conf_ln_b", "w_conf_out", "sc_conv_w", "w_sc_out", "w_o", "ada_ffn_w", "ada_ffn_b", "norm_ffn_g",
                "w_up", "ffn_conv_w", "ffn_conv_b", "w_down", "final_norm_g")


def _pack_flat(arrays, cols, row_multiple, dtype):
    flat = jnp.concatenate([a.reshape(-1).astype(dtype) for a in arrays])
    rows = -(-flat.shape[0] // cols)
    rows = -(-rows // row_multiple) * row_multiple
    return jnp.pad(flat, (0, rows * cols - flat.shape[0])).reshape(rows, cols)


def _unpack_flat(flat2d, shapes):
    flat = flat2d.reshape(-1)
    out, off = [], 0
    for s in shapes:
        n = 1
        for d in s:
            n *= d
        out.append(flat[off:off + n].reshape(s))
        off += n
    return out


def _cols(g, lo, hi):
    b = g.shape[-1]
    pieces = []
    for k in range(N_DEV):
        a, e = max(lo, k * b), min(hi, (k + 1) * b)
        if a < e:
            pieces.append(g[k, :, a - k * b:e - k * b])
    return pieces[0] if len(pieces) == 1 else jnp.concatenate(pieces, axis=1)


def _rows(g):
    return g.reshape(N_DEV * g.shape[1], g.shape[2])


def _col_shards(segs, b):
    shards = []
    for k in range(N_DEV):
        lo, hi = k * b, (k + 1) * b
        pieces, off = [], 0
        for seg in segs:
            n = seg.shape[1]
            a, e = max(lo, off), min(hi, off + n)
            if a < e:
                pieces.append(seg[:, a - off:e - off])
            off += n
        shards.append(pieces[0] if len(pieces) == 1 else jnp.concatenate(pieces, axis=1))
    return jnp.stack(shards)


def _row_shards(full):
    return full.reshape(N_DEV, full.shape[0] // N_DEV, full.shape[1])


def _pad_rows(a, rows):
    return jnp.pad(a, ((0, rows - a.shape[0]), (0, 0)))


def _pad_lanes(a):
    return jnp.pad(a, ((0, 0), (0, LANES - a.shape[1])))


def _whole(g):
    return _cols(g, 0, N_DEV * g.shape[-1])


def _mixer_weights(full, i):
    row = lambda name: full[name][i].reshape(1, -1)
    conv = lambda name: _whole(full[name][:, i])
    w_in = full["w_in", i]
    return {
        "ada_mix_w": _whole(full["ada_mix_w", i]), "ada_mix_b": row("ada_mix_b"), "norm_mix_g": row("norm_mix_g"),
        "w_z": _cols(w_in, 0, OFF_Z), "w_xbc": _cols(w_in, OFF_Z, OFF_XBC),
        "w_dt": _pad_lanes(_cols(w_in, OFF_XBC, OFF_DT)), "w_conf": _cols(w_in, OFF_DT, OFF_CONF),
        "w_sc": _cols(w_in, OFF_CONF, OFF_SC), "w_gates": _cols(w_in, OFF_SC, N_IN),
        "b_gate": row("b_gate"),
        "ssd_conv_w": _pad_rows(conv("ssd_conv_w"), SUBLANES), "ssd_conv_b": row("ssd_conv_b"),
        "dt_bias": _pad_lanes(row("ssd_dt_bias")), "a_log": _pad_lanes(row("ssd_a_log")),
        "ssd_d": _pad_lanes(row("ssd_d")), "ssd_norm_g": row("ssd_norm_g"), "w_ssd_out": _rows(full["w_ssd_out", i]),
        "conf_conv_w": _pad_rows(conv("conf_conv_w"), CONF_HALO), "conf_conv_b": row("conf_conv_b"),
        "conf_ln_g": row("conf_ln_g"), "conf_ln_b": row("conf_ln_b"), "w_conf_out": _whole(full["w_conf_out", i]),
        "sc_conv_w": _pad_rows(conv("sc_conv_w"), SUBLANES), "w_sc_out": _whole(full["w_sc_out", i]),
        "w_o": _rows(full["w_o", i]),
    }


def _ffn_weights(full, i):
    row = lambda name: full[name][i].reshape(1, -1)
    return {
        "ada_ffn_w": _whole(full["ada_ffn_w", i]), "ada_ffn_b": row("ada_ffn_b"), "norm_ffn_g": row("norm_ffn_g"),
        "w_up": _whole(full["w_up", i]), "ffn_conv_w": _pad_rows(_whole(full["ffn_conv_w"][:, i]), SUBLANES),
        "ffn_conv_b": row("ffn_conv_b"), "w_down": _rows(full["w_down", i]),
    }


def _adaln(sc8, w, b, name):
    mod = _matmul(sc8, w, "nn", F32, name)[0:1, :] + b
    return mod[:, :D_MODEL], mod[:, D_MODEL:2 * D_MODEL], mod[:, 2 * D_MODEL:]


def _mixer_fwd(i, x, prev, sc8, wl):
    t = f"l{i}_"
    s = {}
    shift, scale, gate = _adaln(sc8, wl["ada_mix_w"], wl["ada_mix_b"], t + "ada_mix")
    if prev is None:
        s["x_in"] = x
        s["h"] = _prenorm_first(x, wl["norm_mix_g"], scale, shift, t + "norm_mix")
    else:
        s["x_in"], s["h"] = _prenorm_res(x, prev[0], prev[1], wl["norm_mix_g"], scale, shift, t + "norm_mix")
    s["scale_mix"], s["gate_mix"] = scale, gate
    h = s["h"]
    s["z"] = _matmul(h, wl["w_z"], "nn", F32, t + "in_z")
    s["xbc"] = _matmul(h, wl["w_xbc"], "nn", F32, t + "in_xbc")
    s["dt_raw"] = _matmul(h, wl["w_dt"], "nn", F32, t + "in_dt")
    s["conf"] = _matmul(h, wl["w_conf"], "nn", F32, t + "in_conf")
    s["sc"] = _matmul(h, wl["w_sc"], "nn", F32, t + "in_sc")
    s["gates"] = _matmul(h, wl["w_gates"], "nn", F32, t + "in_gates")
    s["pre"], s["dt"] = _ssd_pre(s["xbc"], s["dt_raw"], wl["ssd_conv_w"], wl["ssd_conv_b"], wl["dt_bias"],
                                 t + "ssd_pre")
    s["y"], s["hprev"] = _ssd_scan(s["pre"], s["dt"], wl["a_log"], wl["ssd_d"], t + "ssd_scan")
    s["ya_in"] = _ssd_post(s["y"], s["z"], wl["ssd_norm_g"], t + "ssd_post")
    s["yb_in"], s["uc"] = _conf_fwd(s["conf"], wl["conf_conv_w"], wl["conf_conv_b"], wl["conf_ln_g"],
                                    wl["conf_ln_b"], t + "conf")
    s["yc_in"] = _sc_fwd(s["sc"], wl["sc_conv_w"], t + "sconv")
    s["ya"] = _matmul(s["ya_in"], wl["w_ssd_out"], "nn", F32, t + "ssd_out")
    s["yb"] = _matmul(s["yb_in"], wl["w_conf_out"], "nn", F32, t + "conf_out")
    s["yc"] = _matmul(s["yc_in"], wl["w_sc_out"], "nn", F32, t + "sc_out")
    s["merged"] = _merge_fwd(s["gates"], s["ya"], s["yb"], s["yc"], wl["b_gate"], t + "merge")
    s["mix"] = _matmul(s["merged"], wl["w_o"], "nn", F32, t + "w_o")
    return s


def _ffn_fwd(i, s, sc8, wl):
    t = f"l{i}_"
    shift2, scale2, gate2 = _adaln(sc8, wl["ada_ffn_w"], wl["ada_ffn_b"], t + "ada_ffn")
    s["x_mid"], s["h2"] = _prenorm_res(s["x_in"], s["mix"], s["gate_mix"], wl["norm_ffn_g"], scale2, shift2,
                                       t + "norm_ffn")
    s["scale_ffn"], s["gate_ffn"] = scale2, gate2
    s["up"] = _matmul(s["h2"], wl["w_up"], "nn", F32, t + "w_up")
    s["a"] = _ffn_mid(s["up"], wl["ffn_conv_w"], wl["ffn_conv_b"], t + "ffn_mid")
    s["out"] = _matmul(s["a"], wl["w_down"], "nn", F32, t + "w_down")
    return s


def _layer_bwd(i, s, wl, sc8, dys_ffn, dx_after, dgate_ffn, prev):
    t = f"l{i}_b_"
    g = {}
    da = _matmul(dys_ffn, wl["w_down"], "nt", F32, t + "d_a")
    g["w_down"] = _matmul(s["a"], dys_ffn, "tn", BF16, t + "dw_down")
    dug, duv, dfw, g["ffn_conv_b"] = _ffn_mid_bwd(da, s["up"], wl["ffn_conv_w"], wl["ffn_conv_b"], t + "ffn_mid")
    g["ffn_conv_w"] = dfw[:3]
    dh2 = _matmul_sum_nt([dug, duv], [wl["w_up"][:, :D_FF], wl["w_up"][:, D_FF:]], t + "d_h2")
    g["w_up"] = [_matmul(s["h2"], dug, "tn", BF16, t + "dw_up_g"), _matmul(s["h2"], duv, "tn", BF16, t + "dw_up_v")]
    dx_mid, dshift2, dscale2, g["norm_ffn_g"], dys_mix, dgate_mix = _norm_bwd(
        dh2, s["x_mid"], dx_after, wl["norm_ffn_g"], s["scale_ffn"], t + "norm_ffn", s["mix"], s["gate_mix"])
    dmod_ffn = jnp.concatenate([dshift2, dscale2, dgate_ffn], axis=1)
    g["ada_ffn_b"] = dmod_ffn
    g["ada_ffn_w"] = [_matmul(sc8, _pad_rows(dmod_ffn, SUBLANES), "tn", BF16, t + "dw_ada_ffn")]
    dmerged = _matmul(dys_mix, wl["w_o"], "nt", F32, t + "d_merged")
    g["w_o"] = _matmul(s["merged"], dys_mix, "tn", BF16, t + "dw_o")
    dya, dyb, dyc, dgates, g["b_gate"] = _merge_bwd(dmerged, s["gates"], s["ya"], s["yb"], s["yc"], wl["b_gate"],
                                                    t + "merge")
    dya_in = _matmul(dya, wl["w_ssd_out"], "nt", F32, t + "d_ya_in")
    g["w_ssd_out"] = _matmul(s["ya_in"], dya, "tn", BF16, t + "dw_ssd_out")
    dyb_in = _matmul(dyb, wl["w_conf_out"], "nt", F32, t + "d_yb_in")
    g["w_conf_out"] = [_matmul(s["yb_in"], dyb, "tn", BF16, t + "dw_conf_out")]
    dyc_in = _matmul(dyc, wl["w_sc_out"], "nt", F32, t + "d_yc_in")
    g["w_sc_out"] = [_matmul(s["yc_in"], dyc, "tn", BF16, t + "dw_sc_out")]
    dy, dz, g["ssd_norm_g"] = _ssd_post_bwd(dya_in, s["y"], s["z"], wl["ssd_norm_g"], t + "ssd_post")
    dpre, ddt, da_log, dd = _ssd_scan_bwd(s["pre"], s["dt"], s["hprev"], dy, wl["a_log"], wl["ssd_d"],
                                          t + "ssd_scan")
    g["ssd_a_log"], g["ssd_d"] = da_log[:, :SSD_HEADS], dd[:, :SSD_HEADS]
    dxbc, ddt_raw, dcw, g["ssd_conv_b"], ddtb = _ssd_pre_bwd(dpre, s["xbc"], ddt, s["dt_raw"], wl["ssd_conv_w"],
                                                             wl["dt_bias"], t + "ssd_pre")
    g["ssd_conv_w"], g["ssd_dt_bias"] = dcw[:4], ddtb[:, :SSD_HEADS]
    dconf, dccw, g["conf_conv_b"], g["conf_ln_g"], g["conf_ln_b"] = _conf_bwd(
        dyb_in, s["uc"], s["conf"], wl["conf_conv_w"], wl["conf_ln_g"], wl["conf_ln_b"], t + "conf")
    g["conf_conv_w"] = dccw[:CONF_KERNEL]
    dsc, dscw = _sc_bwd(dyc_in, s["sc"], wl["sc_conv_w"], t + "sconv")
    g["sc_conv_w"] = dscw[:3]
    segs = (("z", dz, "w_z"), ("xbc", dxbc, "w_xbc"), ("dt", ddt_raw, "w_dt"), ("conf", dconf, "w_conf"),
            ("sc", dsc, "w_sc"), ("gates", dgates, "w_gates"))
    dh = _matmul_sum_nt([dseg for _, dseg, _ in segs], [wl[wname] for _, _, wname in segs], t + "d_h")
    dw_segs = []
    for nm, dseg, wname in segs:
        dw = _matmul(s["h"], dseg, "tn", BF16, t + "dw_in_" + nm)
        dw_segs.append(dw[:, :SSD_HEADS] if nm == "dt" else dw)
    g["w_in"] = dw_segs
    if prev is None:
        dx_in, dshift, dscale, g["norm_mix_g"] = _norm_bwd(dh, s["x_in"], dx_mid, wl["norm_mix_g"], s["scale_mix"],
                                                          t + "norm_mix")
        back = None
    else:
        dx_in, dshift, dscale, g["norm_mix_g"], dys_prev, dgate_prev = _norm_bwd(
            dh, s["x_in"], dx_mid, wl["norm_mix_g"], s["scale_mix"], t + "norm_mix", prev[0], prev[1])
        back = (dys_prev, dgate_prev)
    dmod_mix = jnp.concatenate([dshift, dscale, dgate_mix], axis=1)
    g["ada_mix_b"] = dmod_mix
    g["ada_mix_w"] = [_matmul(sc8, _pad_rows(dmod_mix, SUBLANES), "tn", BF16, t + "dw_ada_mix")]
    return g, dx_in, back


def _device_step(x, c, target, full, fetch=None):
    fetch = fetch or {}
    full = dict(full)
    sc8 = _pad_rows(c * (1.0 / (1.0 + jnp.exp(-c))), SUBLANES)
    wls, saved, prev, xcur = [], [], None, x
    for i in range(DEPTH):
        if (i, "mixer") in fetch:
            full.update(fetch[i, "mixer"](prev[0]))
        wl = _mixer_weights(full, i)
        s = _mixer_fwd(i, xcur, prev, sc8, wl)
        if (i, "ffn") in fetch:
            full.update(fetch[i, "ffn"](s["mix"]))
        wf = _ffn_weights(full, i)
        _ffn_fwd(i, s, sc8, wf)
        wls.append({**wl, **wf})
        saved.append(s)
        xcur, prev = s["x_mid"], (s["out"], s["gate_ffn"])
    gf = full["final_norm_g"].reshape(1, -1)
    last = saved[-1]
    loss, dx, dys, dgate, dgf = _final_loss(last["x_mid"], last["out"], last["gate_ffn"], gf, target, "final_loss")
    grads = [None] * DEPTH
    for i in reversed(range(DEPTH)):
        prev = None if i == 0 else (saved[i - 1]["out"], saved[i - 1]["gate_ffn"])
        grads[i], dx, back = _layer_bwd(i, saved[i], wls[i], sc8, dys, dx, dgate, prev)
        if back is not None:
            dys, dgate = back
    return loss[0, 0], dx, grads, dgf


def _step(x, c, target, weights, moments_m, moments_v):
    sharded_names = [n for n, _ in SHARDED]
    conv_names = [n for n, _ in CONV_WEIGHTS]
    shard = lambda key: weights[key[0]][key[1]].astype(BF16)
    first = [(n, 0) for n in MIXER_MATRICES]
    later = {(0, "ffn"): [(n, 0) for n in FFN_MATRICES], (1, "mixer"): [(n, 1) for n in MIXER_MATRICES + FFN_MATRICES]}
    gathered = _all_gather([shard(k) for k in first] + [weights[n] for n in conv_names], "gather_first")
    full = {n: weights[n] for n in REPLICATED}
    full.update(zip(first + conv_names, gathered))
    fetch, after = {}, gathered[0]
    for stage, keys in later.items():
        state, after = _gather_start([shard(k) for k in keys], after, f"gather_l{stage[0]}_{stage[1]}_start")
        fetch[stage] = functools.partial(
            lambda act, state, keys, nm: dict(zip(keys, _gather_wait(state, act, nm))),
            state=state, keys=keys, nm=f"gather_l{stage[0]}_{stage[1]}_wait")
    loss, grad_x, grads, dgf = _device_step(x[0], c + after[0:1, 0:1], target[0], full, fetch)
    parts = []
    for name, axis in SHARDED:
        per_layer = []
        for i in range(DEPTH):
            gw = grads[i][name]
            if axis == 1:
                per_layer.append(_row_shards(gw))
            else:
                per_layer.append(_col_shards(gw if isinstance(gw, list) else [gw], weights[name].shape[-1]))
        parts.append(jnp.stack(per_layer, axis=1).astype(BF16))
    got = _sibling_swap(parts, "swap_grads")
    core = lax.axis_index("c").astype(jnp.int32).reshape(1)
    sums = [_pair_add(p, g, core, "pair_add_" + n) for n, p, g in zip(sharded_names, parts, got)]
    received = _chip_exchange(sums, "exchange_grads")
    big = {n: _reduce_adamw(r, weights[n], moments_m[n], moments_v[n], "adamw_" + n)
           for n, r in zip(sharded_names, received)}
    rep_grads = [dgf if n == "final_norm_g" else jnp.stack([grads[i][n].reshape(-1) for i in range(DEPTH)])
                 for n in REPLICATED]
    small_parts, = _all_gather([_pack_flat(rep_grads, LANES, SUBLANES, F32)], "gather_small_grads")
    pack_s = lambda d: _pack_flat([d[n] for n in REPLICATED], LANES, SUBLANES, F32)[None]
    small = _reduce_adamw(small_parts[:, None], pack_s(weights), pack_s(moments_m), pack_s(moments_v),
                          "adamw_replicated")
    small = [_unpack_flat(b, [weights[n].shape for n in REPLICATED]) for b in small]
    results = []
    for kind in range(4):
        by_name = {n: big[n][kind] for n in sharded_names}
        by_name.update(zip(REPLICATED, small[kind]))
        results.append([by_name[n] for n in WEIGHT_NAMES])
    loss = lax.psum(loss, ("x", "y", "c"))
    return (loss, grad_x[None], *results[0], *results[1], *results[2], *results[3])


def kernel(x, c, ada_mix_w, ada_mix_b, norm_mix_g, w_in, b_gate, ssd_conv_w, ssd_conv_b, ssd_dt_bias, ssd_a_log, ssd_d, ssd_norm_g, w_ssd_out, conf_conv_w, conf_conv_b, conf_ln_g, conf_ln_b, w_conf_out, sc_conv_w, w_sc_out, w_o, ada_ffn_w, ada_ffn_b, norm_ffn_g, w_up, ffn_conv_w, ffn_conv_b, w_down, final_norm_g, loss_target, m_ada_mix_w, m_ada_mix_b, m_norm_mix_g, m_w_in, m_b_gate, m_ssd_conv_w, m_ssd_conv_b, m_ssd_dt_bias, m_ssd_a_log, m_ssd_d, m_ssd_norm_g, m_w_ssd_out, m_conf_conv_w, m_conf_conv_b, m_conf_ln_g, m_conf_ln_b, m_w_conf_out, m_sc_conv_w, m_w_sc_out, m_w_o, m_ada_ffn_w, m_ada_ffn_b, m_norm_ffn_g, m_w_up, m_ffn_conv_w, m_ffn_conv_b, m_w_down, m_final_norm_g, v_ada_mix_w, v_ada_mix_b, v_norm_mix_g, v_w_in, v_b_gate, v_ssd_conv_w, v_ssd_conv_b, v_ssd_dt_bias, v_ssd_a_log, v_ssd_d, v_ssd_norm_g, v_w_ssd_out, v_conf_conv_w, v_conf_conv_b, v_conf_ln_g, v_conf_ln_b, v_w_conf_out, v_sc_conv_w, v_w_sc_out, v_w_o, v_ada_ffn_w, v_ada_ffn_b, v_norm_ffn_g, v_w_up, v_ffn_conv_w, v_ffn_conv_b, v_w_down, v_final_norm_g):
    given = dict(locals())
    weights = {n: given[n] for n in WEIGHT_NAMES}
    moments_m = {n: given["m_" + n] for n in WEIGHT_NAMES}
    moments_v = {n: given["v_" + n] for n in WEIGHT_NAMES}
    return _step(x, c, loss_target, weights, moments_m, moments_v)
```

```python
import functools

import jax
import jax.numpy as jnp
from jax import lax
from jax.experimental import pallas as pl
from jax.experimental.pallas import tpu as pltpu

F32 = jnp.float32
BF16 = jnp.bfloat16
MESH = pl.DeviceIdType.MESH

N_DEV = 8
DEPTH = 2
D_MODEL = 1024
SSD_HEADS = 16
SSD_HEAD_DIM = 64
SSD_INNER = 1024
SSD_STATE = 64
SSD_CHUNK = 128
SSD_XBC = 1280
CONF_WIDTH = 512
CONF_KERNEL = 31
SC_WIDTH = 512
D_FF = 2816
EPS = 1e-6
OFF_Z, OFF_XBC, OFF_DT, OFF_CONF, OFF_SC, N_IN = 1024, 2304, 2320, 3344, 4880, 7952

ADAM_LR, ADAM_B1, ADAM_B2, ADAM_EPS, ADAM_WD, ADAM_STEP = 0.001, 0.9, 0.999, 1e-08, 0.01, 10

LANES = 128
SUBLANES = 8
VMEM_LIMIT = 56 * 1024 * 1024
ROW_TILE = 256

NN = (((1,), (0,)), ((), ()))
NT = (((1,), (1,)), ((), ()))
TN = (((0,), (0,)), ((), ()))


def _params(n_axes):
    return pltpu.CompilerParams(dimension_semantics=("arbitrary",) * n_axes, vmem_limit_bytes=VMEM_LIMIT)


def _pc(body, **kw):
    return pl.pallas_call(body, **kw)


def _dot(a, b, dn=NN, precision=None):
    return lax.dot_general(a, b, dn, precision=precision, preferred_element_type=F32)


def _split3(x):
    hi = x.astype(BF16)
    r1 = x - hi.astype(F32)
    mid = r1.astype(BF16)
    return hi, mid, (r1 - mid.astype(F32)).astype(BF16)


def _dot_sel(x, sel):
    hi, mid, lo = _split3(x)
    return _dot(hi, sel) + _dot(mid, sel) + _dot(lo, sel)


def _sel_dot(sel, x):
    hi, mid, lo = _split3(x)
    return _dot(sel, hi) + _dot(sel, mid) + _dot(sel, lo)


def _sig(x):
    return 1.0 / (1.0 + jnp.exp(-x))


def _fold(v):
    r, c = v.shape
    return v.reshape(r // SUBLANES, SUBLANES, c).sum(axis=0)


def _tile(n_rows):
    return min(ROW_TILE, n_rows // 2)


def _row(tl, c, col=0):
    return pl.BlockSpec((tl, c), lambda i, col=col: (i, col))


def _prev(tl, hb, c, col=0):
    r = tl // hb
    return pl.BlockSpec((hb, c), lambda i, col=col: (jnp.maximum(i * r - 1, 0), col))


def _next(tl, hb, c, n_rows, col=0):
    r = tl // hb
    last = n_rows // hb - 1
    return pl.BlockSpec((hb, c), lambda i, col=col: (jnp.minimum((i + 1) * r, last), col))


def _const(shape):
    return pl.BlockSpec(shape, lambda i: (0,) * len(shape))


def _sds(shape, dtype=F32):
    return jax.ShapeDtypeStruct(shape, dtype)


MM_TILE = 1536
MM_FULL_K = 3072
MM_K_TILE = 1024


def _pick(dim, target):
    if dim <= target:
        return dim
    best = None
    for t in range(LANES, target + 1, LANES):
        if dim % t == 0:
            best = t
    assert best is not None, (dim, target)
    return best


def _matmul(a, b, mode, out_dtype, name):
    if mode == "nn":
        (m, k), (k2, n) = a.shape, b.shape
    elif mode == "nt":
        (m, k), (n, k2) = a.shape, b.shape
    else:
        (k, m), (k2, n) = a.shape, b.shape
    assert k == k2, (a.shape, b.shape, mode)
    tm, tn = _pick(m, MM_TILE), _pick(n, MM_TILE)
    tk = k if k <= MM_FULL_K else _pick(k, MM_K_TILE)
    nk = k // tk
    dn = {"nn": NN, "nt": NT, "tn": TN}[mode]

    def body_one(a_ref, b_ref, o_ref):
        o_ref[...] = _dot(a_ref[...].astype(BF16), b_ref[...].astype(BF16), dn).astype(out_dtype)

    def body_acc(a_ref, b_ref, o_ref, acc):
        kk = pl.program_id(2)

        @pl.when(kk == 0)
        def _():
            acc[...] = jnp.zeros_like(acc)

        acc[...] += _dot(a_ref[...].astype(BF16), b_ref[...].astype(BF16), dn)

        @pl.when(kk == nk - 1)
        def _():
            o_ref[...] = acc[...].astype(out_dtype)

    a_spec = {"nn": pl.BlockSpec((tm, tk), lambda i, j, kk: (i, kk)),
              "nt": pl.BlockSpec((tm, tk), lambda i, j, kk: (i, kk)),
              "tn": pl.BlockSpec((tk, tm), lambda i, j, kk: (kk, i))}[mode]
    b_spec = {"nn": pl.BlockSpec((tk, tn), lambda i, j, kk: (kk, j)),
              "nt": pl.BlockSpec((tn, tk), lambda i, j, kk: (j, kk)),
              "tn": pl.BlockSpec((tk, tn), lambda i, j, kk: (kk, j))}[mode]
    o_spec = pl.BlockSpec((tm, tn), lambda i, j, kk: (i, j))
    return _pc(body_one if nk == 1 else body_acc, name=name, out_shape=_sds((m, n), out_dtype),
               grid=(m // tm, n // tn, nk), in_specs=[a_spec, b_spec], out_specs=o_spec,
               scratch_shapes=[] if nk == 1 else [pltpu.VMEM((tm, tn), F32)], compiler_params=_params(3))(a, b)


SUM_NT_TILE = 512


def _matmul_sum_nt(a_list, b_list, name):
    m, n = a_list[0].shape[0], b_list[0].shape[0]
    cnt = len(a_list)
    tm, tn = _pick(m, SUM_NT_TILE), _pick(n, SUM_NT_TILE)

    def body(*refs):
        a_refs, b_refs, o_ref = refs[:cnt], refs[cnt:2 * cnt], refs[2 * cnt]
        acc = _dot(a_refs[0][...].astype(BF16), b_refs[0][...].astype(BF16), NT)
        for t in range(1, cnt):
            acc = acc + _dot(a_refs[t][...].astype(BF16), b_refs[t][...].astype(BF16), NT)
        o_ref[...] = acc

    in_specs = [pl.BlockSpec((tm, a.shape[1]), lambda j, i: (i, 0)) for a in a_list]
    in_specs += [pl.BlockSpec((tn, b.shape[1]), lambda j, i: (j, 0)) for b in b_list]
    return _pc(body, name=name, out_shape=_sds((m, n)), grid=(n // tn, m // tm), in_specs=in_specs,
               out_specs=pl.BlockSpec((tm, tn), lambda j, i: (i, j)), compiler_params=_params(2))(*a_list, *b_list)


def _norm_mod(x, g, scale, shift):
    r = lax.rsqrt(jnp.mean(x * x, axis=-1, keepdims=True) + EPS)
    return ((x * r) * g) * (1.0 + scale) + shift


def _prenorm_first(x, g, scale, shift, name):
    n, d = x.shape
    tl = _tile(n)

    def body(x_ref, g_ref, sc_ref, sh_ref, h_ref):
        h_ref[...] = _norm_mod(x_ref[...], g_ref[...], sc_ref[...], sh_ref[...]).astype(BF16)

    return _pc(body, name=name, out_shape=_sds((n, d), BF16), grid=(n // tl,),
               in_specs=[_row(tl, d)] + [_const((1, d))] * 3, out_specs=_row(tl, d),
               compiler_params=_params(1))(x, g, scale, shift)


def _prenorm_res(x, y, gate, g, scale, shift, name):
    n, d = x.shape
    tl = _tile(n)

    def body(x_ref, y_ref, gate_ref, g_ref, sc_ref, sh_ref, xo_ref, h_ref):
        xn = x_ref[...] + gate_ref[...] * y_ref[...]
        xo_ref[...] = xn
        h_ref[...] = _norm_mod(xn, g_ref[...], sc_ref[...], sh_ref[...]).astype(BF16)

    return _pc(body, name=name, out_shape=(_sds((n, d)), _sds((n, d), BF16)), grid=(n // tl,),
               in_specs=[_row(tl, d), _row(tl, d)] + [_const((1, d))] * 4,
               out_specs=(_row(tl, d), _row(tl, d)), compiler_params=_params(1))(x, y, gate, g, scale, shift)


def _final_loss(x, y, gate, gf, target, name):
    n, d = x.shape
    tl = _tile(n)
    nb = n // tl

    def body(x_ref, y_ref, gate_ref, gf_ref, t_ref, loss_ref, dx_ref, dys_ref, dgate_ref, dgf_ref,
             acc_l, acc_gate, acc_gf):
        i = pl.program_id(0)

        @pl.when(i == 0)
        def _():
            acc_l[...] = jnp.zeros_like(acc_l)
            acc_gate[...] = jnp.zeros_like(acc_gate)
            acc_gf[...] = jnp.zeros_like(acc_gf)

        yv = y_ref[...]
        gate = gate_ref[...]
        gf = gf_ref[...]
        x2 = x_ref[...] + gate * yv
        r = lax.rsqrt(jnp.mean(x2 * x2, axis=-1, keepdims=True) + EPS)
        xn = x2 * r
        e = xn * gf - t_ref[...]
        acc_l[...] += _fold(e * e)
        dy = e * (1.0 / d)
        acc_gf[...] += _fold(dy * xn)
        dxn = dy * gf
        dx = r * (dxn - xn * jnp.mean(dxn * xn, axis=-1, keepdims=True))
        dx_ref[...] = dx
        dys_ref[...] = (dx * gate).astype(BF16)
        acc_gate[...] += _fold(dx * yv)

        @pl.when(i == nb - 1)
        def _():
            loss_ref[...] = jnp.full((SUBLANES, LANES), 0.5 / d, F32) * jnp.sum(acc_l[...])
            dgate_ref[...] = jnp.sum(acc_gate[...], axis=0, keepdims=True)
            dgf_ref[...] = jnp.sum(acc_gf[...], axis=0, keepdims=True)

    return _pc(body, name=name,
               out_shape=(_sds((SUBLANES, LANES)), _sds((n, d)), _sds((n, d), BF16), _sds((1, d)), _sds((1, d))),
               grid=(nb,),
               in_specs=[_row(tl, d), _row(tl, d), _const((1, d)), _const((1, d)), _row(tl, d)],
               out_specs=(_const((SUBLANES, LANES)), _row(tl, d), _row(tl, d), _const((1, d)), _const((1, d))),
               scratch_shapes=[pltpu.VMEM((SUBLANES, d), F32)] * 3,
               compiler_params=_params(1))(x, y, gate, gf, target)


def _norm_bwd(dh, x, dxo, g, scale, name, y_prev=None, gate_prev=None):
    n, d = x.shape
    tl = _tile(n)
    nb = n // tl
    has_prev = y_prev is not None

    def body(*refs):
        if has_prev:
            (dh_ref, x_ref, dxo_ref, g_ref, sc_ref, yp_ref, gp_ref,
             dx_ref, dsh_ref, dsc_ref, dg_ref, dys_ref, dgp_ref, acc_sh, acc_s, acc_gp) = refs
        else:
            (dh_ref, x_ref, dxo_ref, g_ref, sc_ref,
             dx_ref, dsh_ref, dsc_ref, dg_ref, acc_sh, acc_s) = refs
        i = pl.program_id(0)

        @pl.when(i == 0)
        def _():
            acc_sh[...] = jnp.zeros_like(acc_sh)
            acc_s[...] = jnp.zeros_like(acc_s)
            if has_prev:
                acc_gp[...] = jnp.zeros_like(acc_gp)

        x_ = x_ref[...]
        dh_ = dh_ref[...]
        g_ = g_ref[...]
        one_sc = 1.0 + sc_ref[...]
        r = lax.rsqrt(jnp.mean(x_ * x_, axis=-1, keepdims=True) + EPS)
        xn = x_ * r
        dxn = dh_ * (g_ * one_sc)
        dx = dxo_ref[...] + r * (dxn - xn * jnp.mean(dxn * xn, axis=-1, keepdims=True))
        dx_ref[...] = dx
        acc_sh[...] += _fold(dh_)
        acc_s[...] += _fold(dh_ * xn)
        if has_prev:
            dys_ref[...] = (dx * gp_ref[...]).astype(BF16)
            acc_gp[...] += _fold(dx * yp_ref[...])

        @pl.when(i == nb - 1)
        def _():
            s = jnp.sum(acc_s[...], axis=0, keepdims=True)
            dsh_ref[...] = jnp.sum(acc_sh[...], axis=0, keepdims=True)
            dsc_ref[...] = s * g_
            dg_ref[...] = s * one_sc
            if has_prev:
                dgp_ref[...] = jnp.sum(acc_gp[...], axis=0, keepdims=True)

    vec = _sds((1, d))
    in_specs = [_row(tl, d)] * 3 + [_const((1, d))] * 2
    out_shape = [_sds((n, d)), vec, vec, vec]
    out_specs = [_row(tl, d)] + [_const((1, d))] * 3
    scratch = [pltpu.VMEM((SUBLANES, d), F32)] * 2
    args = [dh, x, dxo, g, scale]
    if has_prev:
        in_specs += [_row(tl, d), _const((1, d))]
        out_shape += [_sds((n, d), BF16), vec]
        out_specs += [_row(tl, d), _const((1, d))]
        scratch += [pltpu.VMEM((SUBLANES, d), F32)]
        args += [y_prev, gate_prev]
    return _pc(body, name=name, out_shape=tuple(out_shape), grid=(nb,), in_specs=in_specs,
               out_specs=tuple(out_specs), scratch_shapes=scratch, compiler_params=_params(1))(*args)


CONV_HALO = 8
CONF_HALO = 32


def _ssd_pre(xbc, dt_raw, conv_w, conv_b, dt_bias, name):
    n, c = xbc.shape
    tl = _tile(n)
    hb = CONV_HALO
    k_taps = 4

    def body(x_ref, xp_ref, dt_ref, w_ref, b_ref, dtb_ref, pre_ref, dts_ref, buf):
        i = pl.program_id(0)
        buf[pl.ds(0, hb), :] = jnp.where(i > 0, xp_ref[...], 0.0)
        buf[pl.ds(hb, tl), :] = x_ref[...]
        acc = b_ref[...] + w_ref[0:1, :] * buf[pl.ds(hb - 3, tl), :]
        for k in range(1, k_taps):
            acc = acc + w_ref[k:k + 1, :] * buf[pl.ds(hb - 3 + k, tl), :]
        pre_ref[...] = acc
        v = dt_ref[...] + dtb_ref[...]
        dts_ref[...] = jnp.maximum(v, 0.0) + jnp.log1p(jnp.exp(-jnp.abs(v)))

    return _pc(body, name=name, out_shape=(_sds((n, c)), _sds((n, LANES))), grid=(n // tl,),
               in_specs=[_row(tl, c), _prev(tl, hb, c), _row(tl, LANES), _const((SUBLANES, c)), _const((1, c)),
                         _const((1, LANES))],
               out_specs=(_row(tl, c), _row(tl, LANES)),
               scratch_shapes=[pltpu.VMEM((tl + hb, c), F32)], compiler_params=_params(1))(
        xbc, xbc, dt_raw, conv_w, conv_b, dt_bias)


def _ssd_pre_bwd(dpre, xbc, ddt, dt_raw, conv_w, dt_bias, name):
    n, c = xbc.shape
    tl = _tile(n)
    nb = n // tl
    hb = CONV_HALO
    k_taps = 4

    def body(dp_ref, dpn_ref, x_ref, xp_ref, ddt_ref, dt_ref, w_ref, dtb_ref,
             dx_ref, ddr_ref, dw_ref, db_ref, ddtb_ref, dbuf, xbuf, acc_w, acc_b, acc_dtb):
        i = pl.program_id(0)

        @pl.when(i == 0)
        def _():
            acc_w[...] = jnp.zeros_like(acc_w)
            acc_b[...] = jnp.zeros_like(acc_b)
            acc_dtb[...] = jnp.zeros_like(acc_dtb)

        dp = dp_ref[...]
        dbuf[pl.ds(0, tl), :] = dp
        dbuf[pl.ds(tl, hb), :] = jnp.where(i < nb - 1, dpn_ref[...], 0.0)
        xbuf[pl.ds(0, hb), :] = jnp.where(i > 0, xp_ref[...], 0.0)
        xbuf[pl.ds(hb, tl), :] = x_ref[...]
        dx = w_ref[0:1, :] * dbuf[pl.ds(3, tl), :]
        for k in range(1, k_taps):
            dx = dx + w_ref[k:k + 1, :] * dbuf[pl.ds(3 - k, tl), :]
        dx_ref[...] = dx.astype(BF16)
        for k in range(k_taps):
            acc_w[k] += _fold(dp * xbuf[pl.ds(hb - 3 + k, tl), :])
        acc_b[...] += _fold(dp)
        ddr = ddt_ref[...] * _sig(dt_ref[...] + dtb_ref[...])
        ddr_ref[...] = ddr.astype(BF16)
        acc_dtb[...] += _fold(ddr)

        @pl.when(i == nb - 1)
        def _():
            dw_ref[...] = jnp.zeros_like(dw_ref)
            for k in range(k_taps):
                dw_ref[k:k + 1, :] = jnp.sum(acc_w[k], axis=0, keepdims=True)
            db_ref[...] = jnp.sum(acc_b[...], axis=0, keepdims=True)
            ddtb_ref[...] = jnp.sum(acc_dtb[...], axis=0, keepdims=True)

    return _pc(body, name=name,
               out_shape=(_sds((n, c), BF16), _sds((n, LANES), BF16), _sds((SUBLANES, c)), _sds((1, c)),
                          _sds((1, LANES))),
               grid=(nb,),
               in_specs=[_row(tl, c), _next(tl, hb, c, n), _row(tl, c), _prev(tl, hb, c), _row(tl, LANES),
                         _row(tl, LANES), _const((SUBLANES, c)), _const((1, LANES))],
               out_specs=(_row(tl, c), _row(tl, LANES), _const((SUBLANES, c)), _const((1, c)), _const((1, LANES))),
               scratch_shapes=[pltpu.VMEM((tl + hb, c), F32), pltpu.VMEM((tl + hb, c), F32),
                               pltpu.VMEM((k_taps, SUBLANES, c), F32), pltpu.VMEM((SUBLANES, c), F32),
                               pltpu.VMEM((SUBLANES, LANES), F32)],
               compiler_params=_params(1))(dpre, dpre, xbc, xbc, ddt, dt_raw, conv_w, dt_bias)


def _expand_mat():
    r = lax.broadcasted_iota(jnp.int32, (LANES, SSD_INNER), 0)
    c = lax.broadcasted_iota(jnp.int32, (LANES, SSD_INNER), 1)
    return (jnp.right_shift(c, 6) == r).astype(BF16)


def _reduce_mat():
    r = lax.broadcasted_iota(jnp.int32, (SSD_INNER, LANES), 0)
    c = lax.broadcasted_iota(jnp.int32, (SSD_INNER, LANES), 1)
    return (jnp.right_shift(r, 6) == c).astype(BF16)


def _ssd_common(pre, dt, alog):
    q = SSD_CHUNK
    sg = _sig(pre)
    act = pre * sg
    lane = lax.broadcasted_iota(jnp.int32, (1, LANES), 1)
    a_neg = jnp.where(lane < SSD_HEADS, -jnp.exp(alog), 0.0)
    rr = lax.broadcasted_iota(jnp.int32, (q, q), 0)
    cc = lax.broadcasted_iota(jnp.int32, (q, q), 1)
    causal = rr >= cc
    cum = _sel_dot(causal.astype(BF16), dt * a_neg)
    e_mat = _expand_mat()
    dtx = _dot_sel(dt, e_mat)
    cumx = _dot_sel(cum, e_mat)
    return sg, act, a_neg, causal, cum, e_mat, dtx, cumx


def _ssd_scan(pre, dt, alog, dvec, name):
    n = pre.shape[0]
    q = SSD_CHUNK
    nc = n // q

    def body(pre_ref, dt_ref, alog_ref, d_ref, y_ref, hp_ref, state):
        i = pl.program_id(0)

        @pl.when(i == 0)
        def _():
            state[...] = jnp.zeros_like(state)

        dt_ = dt_ref[...]
        _, act, _, causal, cum, e_mat, dtx, cumx = _ssd_common(pre_ref[...], dt_, alog_ref[...])
        xs = act[:, :SSD_INNER]
        bm = act[:, SSD_INNER:SSD_INNER + LANES]
        cm = act[:, SSD_INNER + LANES:]
        cum_t = cum.T
        clx = cumx[q - 1:q, :]
        xc = xs * dtx
        xd = xc * jnp.exp(clx - cumx)
        doutx = jnp.exp(cumx)
        edec = jnp.exp(clx)
        dx_row = _dot_sel(jnp.broadcast_to(d_ref[...], (SUBLANES, LANES)), e_mat)[0:1, :]
        hp_ref[0] = state[...]
        bb = bm.astype(BF16)
        cb = cm.astype(BF16)
        lane = lax.broadcasted_iota(jnp.int32, (1, LANES), 1)
        row = lax.broadcasted_iota(jnp.int32, (LANES, 1), 0)
        cbs = []
        for g in range(2):
            cg = jnp.where(jnp.right_shift(lane, 6) == g, cm, 0.0).astype(BF16)
            cbs.append(_dot(cg, bb, NT))
        for j in range(SSD_HEADS // 2):
            sl = slice(j * LANES, (j + 1) * LANES)
            g = j // 4
            xcj = xc[:, sl].astype(BF16)
            halves = []
            for half in range(2):
                h = 2 * j + half
                seg = cum[:, h:h + 1] - cum_t[h:h + 1, :]
                w = cbs[g] * jnp.exp(jnp.where(causal, seg, -jnp.inf))
                halves.append(_dot(w.astype(BF16), xcj))
            y_diag = jnp.where(lane < SSD_HEAD_DIM, halves[0], halves[1])
            hj = state[:, sl]
            y_off = doutx[:, sl] * _dot(cb, hj.astype(BF16))
            y_ref[:, sl] = y_diag + y_off + xs[:, sl] * dx_row[:, sl]
            st = _dot(bb, xd[:, sl].astype(BF16), TN)
            state[:, sl] = hj * edec[:, sl] + jnp.where(jnp.right_shift(row, 6) == g, st, 0.0)

    return _pc(body, name=name, out_shape=(_sds((n, SSD_INNER)), _sds((nc, LANES, SSD_INNER))), grid=(nc,),
               in_specs=[_row(q, SSD_XBC), _row(q, LANES), _const((1, LANES)), _const((1, LANES))],
               out_specs=(_row(q, SSD_INNER), pl.BlockSpec((1, LANES, SSD_INNER), lambda i: (i, 0, 0))),
               scratch_shapes=[pltpu.VMEM((LANES, SSD_INNER), F32)], compiler_params=_params(1))(pre, dt, alog, dvec)


def _ssd_scan_bwd(pre, dt, hprev, dy, alog, dvec, name):
    n = pre.shape[0]
    q = SSD_CHUNK
    nc = n // q

    def body(pre_ref, dt_ref, hp_ref, dy_ref, alog_ref, d_ref, dpre_ref, ddt_ref, da_ref, dd_ref,
             d_state, dxc_s, dcx_s, dcl_s, acc_a, acc_d):
        i = pl.program_id(0)

        @pl.when(i == 0)
        def _():
            d_state[...] = jnp.zeros_like(d_state)
            acc_a[...] = jnp.zeros_like(acc_a)
            acc_d[...] = jnp.zeros_like(acc_d)

        pre_ = pre_ref[...]
        dt_ = dt_ref[...]
        sg, act, a_neg, causal, cum, e_mat, dtx, cumx = _ssd_common(pre_, dt_, alog_ref[...])
        r_mat = _reduce_mat()
        xs = act[:, :SSD_INNER]
        bm = act[:, SSD_INNER:SSD_INNER + LANES]
        cm = act[:, SSD_INNER + LANES:]
        cum_t = cum.T
        clx = cumx[q - 1:q, :]
        xc = xs * dtx
        dsx = jnp.exp(clx - cumx)
        doutx = jnp.exp(cumx)
        edec = jnp.exp(clx)
        dx_row = _dot_sel(jnp.broadcast_to(d_ref[...], (SUBLANES, LANES)), e_mat)[0:1, :]
        dy_ = dy_ref[...]
        acc_d[...] += _fold(dy_ * xs)
        bb = bm.astype(BF16)
        cb = cm.astype(BF16)
        lane = lax.broadcasted_iota(jnp.int32, (1, LANES), 1)
        row = lax.broadcasted_iota(jnp.int32, (LANES, 1), 0)
        d_c = jnp.zeros((q, LANES), F32)
        d_b = jnp.zeros((q, LANES), F32)

        for j in range(SSD_HEADS // 2):
            sl = slice(j * LANES, (j + 1) * LANES)
            g = j // 4
            hj = hp_ref[0, :, sl]
            hjb = hj.astype(BF16)
            dyj = dy_[:, sl]
            tj = _dot(cb, hjb)
            dtj = (doutx[:, sl] * dyj).astype(BF16)
            dcx = dyj * tj * doutx[:, sl]
            d_c = d_c + _dot(dtj, hjb, NT)
            dhn = d_state[:, sl]
            dhp = dhn * edec[:, sl] + jnp.where(jnp.right_shift(row, 6) == g, _dot(cb, dtj, TN), 0.0)
            dcl = jnp.sum(dhn * hj, axis=0, keepdims=True) * edec[:, sl]
            dsb = dhn.astype(BF16)
            dxd = _dot(bb, dsb)
            xcj = xc[:, sl]
            dsj = dsx[:, sl]
            d_b = d_b + _dot((xcj * dsj).astype(BF16), dsb, NT)
            dds = dxd * xcj * dsj
            d_state[:, sl] = dhp
            dxc_s[:, sl] = dxd * dsj
            dcx_s[:, sl] = dcx - dds
            dcl_s[:, sl] = jnp.broadcast_to(dcl + jnp.sum(dds, axis=0, keepdims=True), (SUBLANES, LANES))

        dcum_c = jnp.zeros((q, LANES), F32)
        dcum_t = jnp.zeros((LANES, q), F32)
        for g in range(2):
            gmask = jnp.right_shift(lane, 6) == g
            cg = jnp.where(gmask, cm, 0.0).astype(BF16)
            cbg = _dot(cg, bb, NT)
            d_cb = jnp.zeros((q, q), F32)
            for hh in range(SSD_HEADS // 2):
                h = g * (SSD_HEADS // 2) + hh
                j, half = h // 2, h % 2
                sl = slice(j * LANES, (j + 1) * LANES)
                hmask = jnp.right_shift(lane, 6) == half
                seg = cum[:, h:h + 1] - cum_t[h:h + 1, :]
                lm = jnp.exp(jnp.where(causal, seg, -jnp.inf))
                w = cbg * lm
                dyj = dy_[:, sl]
                dw = _dot(jnp.where(hmask, dyj, 0.0).astype(BF16), xc[:, sl].astype(BF16), NT)
                dxch = _dot(w.astype(BF16), dyj.astype(BF16), TN)
                dxc_s[:, sl] += jnp.where(hmask, dxch, 0.0)
                d_cb = d_cb + dw * lm
                m = dw * w
                dcum_c = dcum_c + jnp.sum(m, axis=1, keepdims=True) * (lane == h).astype(F32)
                dcum_t = dcum_t + (row == h).astype(F32) * jnp.sum(m, axis=0, keepdims=True)
            d_cbb = d_cb.astype(BF16)
            d_c = d_c + jnp.where(gmask, _dot(d_cbb, bb), 0.0)
            d_b = d_b + jnp.where(gmask, _dot(d_cbb, cb, TN), 0.0)

        dcl_row = _dot_sel(dcl_s[...], r_mat)[0:1, :]
        rowq = lax.broadcasted_iota(jnp.int32, (q, 1), 0)
        dcum = (dcum_c - dcum_t.T + _dot_sel(dcx_s[...], r_mat)
                + jnp.where(rowq == q - 1, dcl_row, 0.0))
        rr = lax.broadcasted_iota(jnp.int32, (q, q), 0)
        cc = lax.broadcasted_iota(jnp.int32, (q, q), 1)
        dadt = _sel_dot((rr <= cc).astype(BF16), dcum)
        dxc = dxc_s[...]
        ddt_ref[...] = dadt * a_neg + _dot_sel(dxc * xs, r_mat)
        acc_a[...] += _fold(dadt * dt_)
        dsilu = sg * (1.0 + pre_ * (1.0 - sg))
        dpre_ref[:, :SSD_INNER] = (dxc * dtx + dy_ * dx_row) * dsilu[:, :SSD_INNER]
        dpre_ref[:, SSD_INNER:SSD_INNER + LANES] = d_b * dsilu[:, SSD_INNER:SSD_INNER + LANES]
        dpre_ref[:, SSD_INNER + LANES:] = d_c * dsilu[:, SSD_INNER + LANES:]

        @pl.when(i == nc - 1)
        def _():
            da_ref[...] = jnp.sum(acc_a[...], axis=0, keepdims=True) * a_neg
            dd_ref[...] = jnp.sum(_dot_sel(acc_d[...], r_mat), axis=0, keepdims=True)

    rev = lambda i: (nc - 1 - i, 0)
    return _pc(body, name=name,
               out_shape=(_sds((n, SSD_XBC)), _sds((n, LANES)), _sds((1, LANES)), _sds((1, LANES))), grid=(nc,),
               in_specs=[pl.BlockSpec((q, SSD_XBC), rev), pl.BlockSpec((q, LANES), rev),
                         pl.BlockSpec((1, LANES, SSD_INNER), lambda i: (nc - 1 - i, 0, 0)),
                         pl.BlockSpec((q, SSD_INNER), rev), _const((1, LANES)), _const((1, LANES))],
               out_specs=(pl.BlockSpec((q, SSD_XBC), rev), pl.BlockSpec((q, LANES), rev), _const((1, LANES)),
                          _const((1, LANES))),
               scratch_shapes=[pltpu.VMEM((LANES, SSD_INNER), F32), pltpu.VMEM((q, SSD_INNER), F32),
                               pltpu.VMEM((q, SSD_INNER), F32), pltpu.VMEM((SUBLANES, SSD_INNER), F32),
                               pltpu.VMEM((SUBLANES, LANES), F32), pltpu.VMEM((SUBLANES, SSD_INNER), F32)],
               compiler_params=_params(1))(pre, dt, hprev, dy, alog, dvec)


def _group_norm_parts(v):
    half = SSD_INNER // 2
    r0 = lax.rsqrt(jnp.mean(v[:, :half] * v[:, :half], axis=-1, keepdims=True) + EPS)
    r1 = lax.rsqrt(jnp.mean(v[:, half:] * v[:, half:], axis=-1, keepdims=True) + EPS)
    lane = lax.broadcasted_iota(jnp.int32, (1, SSD_INNER), 1)
    return jnp.where(lane < half, r0, r1)


def _group_mean(v):
    half = SSD_INNER // 2
    m0 = jnp.mean(v[:, :half], axis=-1, keepdims=True)
    m1 = jnp.mean(v[:, half:], axis=-1, keepdims=True)
    lane = lax.broadcasted_iota(jnp.int32, (1, SSD_INNER), 1)
    return jnp.where(lane < half, m0, m1)


def _ssd_post(y, z, g, name):
    n, d = y.shape
    tl = _tile(n)

    def body(y_ref, z_ref, g_ref, o_ref):
        z_ = z_ref[...]
        v = y_ref[...] * (z_ * _sig(z_))
        o_ref[...] = ((v * _group_norm_parts(v)) * g_ref[...]).astype(BF16)

    return _pc(body, name=name, out_shape=_sds((n, d), BF16), grid=(n // tl,),
               in_specs=[_row(tl, d), _row(tl, d), _const((1, d))], out_specs=_row(tl, d),
               compiler_params=_params(1))(y, z, g)


def _ssd_post_bwd(dout, y, z, g, name):
    n, d = y.shape
    tl = _tile(n)
    nb = n // tl

    def body(do_ref, y_ref, z_ref, g_ref, dy_ref, dz_ref, dg_ref, acc_g):
        i = pl.program_id(0)

        @pl.when(i == 0)
        def _():
            acc_g[...] = jnp.zeros_like(acc_g)

        z_ = z_ref[...]
        y_ = y_ref[...]
        sz = _sig(z_)
        silu_z = z_ * sz
        v = y_ * silu_z
        rs = _group_norm_parts(v)
        nv = v * rs
        do_ = do_ref[...]
        acc_g[...] += _fold(do_ * nv)
        dn = do_ * g_ref[...]
        dv = rs * (dn - nv * _group_mean(dn * nv))
        dy_ref[...] = dv * silu_z
        dz_ref[...] = (dv * y_ * (sz * (1.0 + z_ * (1.0 - sz)))).astype(BF16)

        @pl.when(i == nb - 1)
        def _():
            dg_ref[...] = jnp.sum(acc_g[...], axis=0, keepdims=True)

    return _pc(body, name=name, out_shape=(_sds((n, d)), _sds((n, d), BF16), _sds((1, d))), grid=(nb,),
               in_specs=[_row(tl, d), _row(tl, d), _row(tl, d), _const((1, d))],
               out_specs=(_row(tl, d), _row(tl, d), _const((1, d))),
               scratch_shapes=[pltpu.VMEM((SUBLANES, d), F32)], compiler_params=_params(1))(dout, y, z, g)


def _layer_norm_parts(uc):
    mu = jnp.mean(uc, axis=-1, keepdims=True)
    xc = uc - mu
    rstd = lax.rsqrt(jnp.mean(xc * xc, axis=-1, keepdims=True) + EPS)
    return xc * rstd, rstd


def _conf_fwd(conf_in, conv_w, conv_b, ln_g, ln_b, name):
    n = conf_in.shape[0]
    c = CONF_WIDTH
    tl = _tile(n)
    hb = CONF_HALO
    k_taps = CONF_KERNEL

    def body(x_ref, xp_ref, w_ref, b_ref, g_ref, beta_ref, o_ref, uc_ref, buf):
        i = pl.program_id(0)
        xp = xp_ref[...]
        buf[pl.ds(0, hb), :] = jnp.where(i > 0, xp[:, :c] * _sig(xp[:, c:]), 0.0)
        x_ = x_ref[...]
        buf[pl.ds(hb, tl), :] = x_[:, :c] * _sig(x_[:, c:])
        acc = b_ref[...] + w_ref[0:1, :] * buf[pl.ds(hb - (k_taps - 1), tl), :]
        for k in range(1, k_taps):
            acc = acc + w_ref[k:k + 1, :] * buf[pl.ds(hb - (k_taps - 1) + k, tl), :]
        uc_ref[...] = acc
        nv, _ = _layer_norm_parts(acc)
        v = nv * g_ref[...] + beta_ref[...]
        o_ref[...] = (v * _sig(v)).astype(BF16)

    return _pc(body, name=name, out_shape=(_sds((n, c), BF16), _sds((n, c))), grid=(n // tl,),
               in_specs=[_row(tl, 2 * c), _prev(tl, hb, 2 * c), _const((hb, c)), _const((1, c)), _const((1, c)),
                         _const((1, c))],
               out_specs=(_row(tl, c), _row(tl, c)),
               scratch_shapes=[pltpu.VMEM((tl + hb, c), F32)], compiler_params=_params(1))(
        conf_in, conf_in, conv_w, conv_b, ln_g, ln_b)


def _conf_bwd(dout, uc, conf_in, conv_w, ln_g, ln_b, name):
    n = conf_in.shape[0]
    c = CONF_WIDTH
    tl = _tile(n)
    nb = n // tl
    hb = CONF_HALO
    k_taps = CONF_KERNEL

    def body(do_ref, don_ref, uc_ref, ucn_ref, x_ref, xp_ref, w_ref, g_ref, beta_ref,
             dx_ref, dw_ref, db_ref, dg_ref, dbeta_ref, dbuf, ubuf, acc_w, acc_b, acc_g, acc_beta):
        i = pl.program_id(0)

        @pl.when(i == 0)
        def _():
            acc_w[...] = jnp.zeros_like(acc_w)
            acc_b[...] = jnp.zeros_like(acc_b)
            acc_g[...] = jnp.zeros_like(acc_g)
            acc_beta[...] = jnp.zeros_like(acc_beta)

        g_ = g_ref[...]
        beta_ = beta_ref[...]

        def d_conv_out(do_, uc_):
            nv, rstd = _layer_norm_parts(uc_)
            v = nv * g_ + beta_
            sv = _sig(v)
            dv = do_ * (sv * (1.0 + v * (1.0 - sv)))
            dn = dv * g_
            duc = rstd * (dn - jnp.mean(dn, axis=-1, keepdims=True)
                          - nv * jnp.mean(dn * nv, axis=-1, keepdims=True))
            return duc, dv, nv

        duc, dv, nv = d_conv_out(do_ref[...], uc_ref[...])
        acc_g[...] += _fold(dv * nv)
        acc_beta[...] += _fold(dv)
        acc_b[...] += _fold(duc)
        dbuf[pl.ds(0, tl), :] = duc
        ducn, _, _ = d_conv_out(don_ref[...], ucn_ref[...])
        dbuf[pl.ds(tl, hb), :] = jnp.where(i < nb - 1, ducn, 0.0)
        xp = xp_ref[...]
        ubuf[pl.ds(0, hb), :] = jnp.where(i > 0, xp[:, :c] * _sig(xp[:, c:]), 0.0)
        x_ = x_ref[...]
        val = x_[:, :c]
        sgate = _sig(x_[:, c:])
        ubuf[pl.ds(hb, tl), :] = val * sgate
        du = w_ref[0:1, :] * dbuf[pl.ds(k_taps - 1, tl), :]
        for k in range(1, k_taps):
            du = du + w_ref[k:k + 1, :] * dbuf[pl.ds(k_taps - 1 - k, tl), :]
        for k in range(k_taps):
            acc_w[k] += _fold(duc * ubuf[pl.ds(hb - (k_taps - 1) + k, tl), :])
        dx_ref[:, :c] = (du * sgate).astype(BF16)
        dx_ref[:, c:] = (du * val * sgate * (1.0 - sgate)).astype(BF16)

        @pl.when(i == nb - 1)
        def _():
            dw_ref[...] = jnp.zeros_like(dw_ref)
            for k in range(k_taps):
                dw_ref[k:k + 1, :] = jnp.sum(acc_w[k], axis=0, keepdims=True)
            db_ref[...] = jnp.sum(acc_b[...], axis=0, keepdims=True)
            dg_ref[...] = jnp.sum(acc_g[...], axis=0, keepdims=True)
            dbeta_ref[...] = jnp.sum(acc_beta[...], axis=0, keepdims=True)

    vec = _sds((1, c))
    return _pc(body, name=name, out_shape=(_sds((n, 2 * c), BF16), _sds((hb, c)), vec, vec, vec), grid=(nb,),
               in_specs=[_row(tl, c), _next(tl, hb, c, n), _row(tl, c), _next(tl, hb, c, n), _row(tl, 2 * c),
                         _prev(tl, hb, 2 * c), _const((hb, c)), _const((1, c)), _const((1, c))],
               out_specs=(_row(tl, 2 * c), _const((hb, c)), _const((1, c)), _const((1, c)), _const((1, c))),
               scratch_shapes=[pltpu.VMEM((tl + hb, c), F32), pltpu.VMEM((tl + hb, c), F32),
                               pltpu.VMEM((k_taps, SUBLANES, c), F32), pltpu.VMEM((SUBLANES, c), F32),
                               pltpu.VMEM((SUBLANES, c), F32), pltpu.VMEM((SUBLANES, c), F32)],
               compiler_params=_params(1))(dout, dout, uc, uc, conf_in, conf_in, conv_w, ln_g, ln_b)


def _sc_fwd(sc_in, conv_w, name):
    n = sc_in.shape[0]
    c = SC_WIDTH
    tl = _tile(n)
    hb = CONV_HALO

    def body(x_ref, xp_ref, w_ref, o_ref, buf):
        i = pl.program_id(0)
        xp = xp_ref[...]
        buf[pl.ds(0, hb), :] = jnp.where(i > 0, xp[:, c:2 * c] * xp[:, 2 * c:], 0.0)
        x_ = x_ref[...]
        buf[pl.ds(hb, tl), :] = x_[:, c:2 * c] * x_[:, 2 * c:]
        cv = w_ref[0:1, :] * buf[pl.ds(hb - 2, tl), :]
        for k in range(1, 3):
            cv = cv + w_ref[k:k + 1, :] * buf[pl.ds(hb - 2 + k, tl), :]
        o_ref[...] = (x_[:, :c] * cv).astype(BF16)

    return _pc(body, name=name, out_shape=_sds((n, c), BF16), grid=(n // tl,),
               in_specs=[_row(tl, 3 * c), _prev(tl, hb, 3 * c), _const((SUBLANES, c))], out_specs=_row(tl, c),
               scratch_shapes=[pltpu.VMEM((tl + hb, c), F32)], compiler_params=_params(1))(sc_in, sc_in, conv_w)


def _sc_bwd(dout, sc_in, conv_w, name):
    n = sc_in.shape[0]
    c = SC_WIDTH
    tl = _tile(n)
    nb = n // tl
    hb = CONV_HALO

    def body(do_ref, don_ref, x_ref, xp_ref, xn_ref, w_ref, dx_ref, dw_ref, dbuf, pbuf, acc_w):
        i = pl.program_id(0)

        @pl.when(i == 0)
        def _():
            acc_w[...] = jnp.zeros_like(acc_w)

        x_ = x_ref[...]
        gb, gc, xv = x_[:, :c], x_[:, c:2 * c], x_[:, 2 * c:]
        do_ = do_ref[...]
        dcv = do_ * gb
        dbuf[pl.ds(0, tl), :] = dcv
        dbuf[pl.ds(tl, hb), :] = jnp.where(i < nb - 1, don_ref[...] * xn_ref[...], 0.0)
        xp = xp_ref[...]
        pbuf[pl.ds(0, hb), :] = jnp.where(i > 0, xp[:, c:2 * c] * xp[:, 2 * c:], 0.0)
        pbuf[pl.ds(hb, tl), :] = gc * xv
        cv = w_ref[0:1, :] * pbuf[pl.ds(hb - 2, tl), :]
        dp = w_ref[0:1, :] * dbuf[pl.ds(2, tl), :]
        for k in range(1, 3):
            cv = cv + w_ref[k:k + 1, :] * pbuf[pl.ds(hb - 2 + k, tl), :]
            dp = dp + w_ref[k:k + 1, :] * dbuf[pl.ds(2 - k, tl), :]
        for k in range(3):
            acc_w[k] += _fold(dcv * pbuf[pl.ds(hb - 2 + k, tl), :])
        dx_ref[:, :c] = (do_ * cv).astype(BF16)
        dx_ref[:, c:2 * c] = (dp * xv).astype(BF16)
        dx_ref[:, 2 * c:] = (dp * gc).astype(BF16)

        @pl.when(i == nb - 1)
        def _():
            dw_ref[...] = jnp.zeros_like(dw_ref)
            for k in range(3):
                dw_ref[k:k + 1, :] = jnp.sum(acc_w[k], axis=0, keepdims=True)

    return _pc(body, name=name, out_shape=(_sds((n, 3 * c), BF16), _sds((SUBLANES, c))), grid=(nb,),
               in_specs=[_row(tl, c), _next(tl, hb, c, n), _row(tl, 3 * c), _prev(tl, hb, 3 * c),
                         _next(tl, hb, c, n), _const((SUBLANES, c))],
               out_specs=(_row(tl, 3 * c), _const((SUBLANES, c))),
               scratch_shapes=[pltpu.VMEM((tl + hb, c), F32), pltpu.VMEM((tl + hb, c), F32),
                               pltpu.VMEM((3, SUBLANES, c), F32)],
               compiler_params=_params(1))(dout, dout, sc_in, sc_in, sc_in, conv_w)


def _merge_fwd(gates, ya, yb, yc, b_gate, name):
    n, d = ya.shape
    tl = _tile(n)

    def body(gt_ref, ya_ref, yb_ref, yc_ref, b_ref, o_ref):
        gt = _sig(gt_ref[...] + b_ref[...])
        o_ref[...] = (gt[:, :d] * ya_ref[...] + gt[:, d:2 * d] * yb_ref[...] + gt[:, 2 * d:] * yc_ref[...]).astype(BF16)

    return _pc(body, name=name, out_shape=_sds((n, d), BF16), grid=(n // tl,),
               in_specs=[_row(tl, 3 * d), _row(tl, d), _row(tl, d), _row(tl, d), _const((1, 3 * d))],
               out_specs=_row(tl, d), compiler_params=_params(1))(gates, ya, yb, yc, b_gate)


def _merge_bwd(dm, gates, ya, yb, yc, b_gate, name):
    n, d = ya.shape
    tl = _tile(n)
    nb = n // tl

    def body(dm_ref, gt_ref, ya_ref, yb_ref, yc_ref, b_ref, dya_ref, dyb_ref, dyc_ref, dgt_ref, db_ref, acc):
        i = pl.program_id(0)

        @pl.when(i == 0)
        def _():
            acc[...] = jnp.zeros_like(acc)

        dm_ = dm_ref[...]
        gt = _sig(gt_ref[...] + b_ref[...])
        for idx, (y_ref, dy_ref) in enumerate(((ya_ref, dya_ref), (yb_ref, dyb_ref), (yc_ref, dyc_ref))):
            gk = gt[:, idx * d:(idx + 1) * d]
            dy_ref[...] = (dm_ * gk).astype(BF16)
            dpre = dm_ * y_ref[...] * gk * (1.0 - gk)
            dgt_ref[:, idx * d:(idx + 1) * d] = dpre.astype(BF16)
            acc[:, idx * d:(idx + 1) * d] += _fold(dpre)

        @pl.when(i == nb - 1)
        def _():
            db_ref[...] = jnp.sum(acc[...], axis=0, keepdims=True)

    bf = _sds((n, d), BF16)
    return _pc(body, name=name, out_shape=(bf, bf, bf, _sds((n, 3 * d), BF16), _sds((1, 3 * d))), grid=(nb,),
               in_specs=[_row(tl, d), _row(tl, 3 * d), _row(tl, d), _row(tl, d), _row(tl, d), _const((1, 3 * d))],
               out_specs=(_row(tl, d), _row(tl, d), _row(tl, d), _row(tl, 3 * d), _const((1, 3 * d))),
               scratch_shapes=[pltpu.VMEM((SUBLANES, 3 * d), F32)], compiler_params=_params(1))(
        dm, gates, ya, yb, yc, b_gate)


FFN_COLS = 1408
FFN_STRIP = 64
FFN_STRIP_BWD = 32


def _down(prev, cur, s, row):
    return jnp.where(row < s, pltpu.roll(prev, s, 0), pltpu.roll(cur, s, 0))


def _up(cur, nxt, s, row):
    return jnp.where(row < SUBLANES - s, pltpu.roll(cur, SUBLANES - s, 0), pltpu.roll(nxt, SUBLANES - s, 0))


def _ffn_mid(up, conv_w, conv_b, name):
    n = up.shape[0]
    tl = _tile(n)
    hb = CONV_HALO
    tc = FFN_COLS
    ncb = D_FF // tc

    def spec(shape_rows, idx_fn, off):
        return pl.BlockSpec((shape_rows, tc), lambda j, i, off=off: (idx_fn(i), j + off))

    r = tl // hb
    cur = lambda i: i
    prv = lambda i: jnp.maximum(i * r - 1, 0)

    def body(g_ref, gp_ref, v_ref, vp_ref, wg_ref, wv_ref, bg_ref, bv_ref, o_ref):
        i = pl.program_id(1)
        row = lax.broadcasted_iota(jnp.int32, (SUBLANES, LANES), 0)
        full = lambda ref, k, ls: jnp.broadcast_to(ref[k:k + 1, ls], (SUBLANES, LANES))
        for lc in range(tc // LANES):
            ls = slice(lc * LANES, (lc + 1) * LANES)
            wg = [full(wg_ref, k, ls) for k in range(3)]
            wv = [full(wv_ref, k, ls) for k in range(3)]
            bg, bv = full(bg_ref, 0, ls), full(bv_ref, 0, ls)

            def conv(prev, x, w, b):
                return b + w[0] * _down(prev, x, 2, row) + w[1] * _down(prev, x, 1, row) + w[2] * x

            def strip(t, carry):
                gs, vs = [carry[0]], [carry[1]]
                for u in range(FFN_STRIP // SUBLANES):
                    rows = pl.ds(pl.multiple_of(t * FFN_STRIP + u * SUBLANES, SUBLANES), SUBLANES)
                    gs.append(g_ref[rows, ls])
                    vs.append(v_ref[rows, ls])
                outs = []
                for u in range(FFN_STRIP // SUBLANES):
                    ug = conv(gs[u], gs[u + 1], wg, bg)
                    outs.append(ug * _sig(ug) * conv(vs[u], vs[u + 1], wv, bv))
                for p in range(FFN_STRIP // 16):
                    rows = pl.ds(pl.multiple_of(t * FFN_STRIP + p * 16, 16), 16)
                    o_ref[rows, ls] = jnp.concatenate(outs[2 * p:2 * p + 2], axis=0).astype(BF16)
                return gs[-1], vs[-1]

            lax.fori_loop(0, tl // FFN_STRIP, strip,
                          (jnp.where(i > 0, gp_ref[:, ls], 0.0), jnp.where(i > 0, vp_ref[:, ls], 0.0)))

    wspec = lambda off: pl.BlockSpec((SUBLANES, tc), lambda j, i, off=off: (0, j + off))
    bspec = lambda off: pl.BlockSpec((1, tc), lambda j, i, off=off: (0, j + off))
    return _pc(body, name=name, out_shape=_sds((n, D_FF), BF16), grid=(ncb, n // tl),
               in_specs=[spec(tl, cur, 0), spec(hb, prv, 0), spec(tl, cur, ncb), spec(hb, prv, ncb),
                         wspec(0), wspec(ncb), bspec(0), bspec(ncb)],
               out_specs=pl.BlockSpec((tl, tc), lambda j, i: (i, j)), compiler_params=_params(2))(
        up, up, up, up, conv_w, conv_w, conv_b, conv_b)


def _ffn_mid_bwd(da, up, conv_w, conv_b, name):
    n = up.shape[0]
    tl = _tile(n)
    nb = n // tl
    nt = tl // FFN_STRIP_BWD
    hb = CONV_HALO
    tc = FFN_COLS
    ncb = D_FF // tc
    r = tl // hb
    last = n // hb - 1
    cur = lambda i: i
    prv = lambda i: jnp.maximum(i * r - 1, 0)
    nxt = lambda i: jnp.minimum((i + 1) * r, last)

    def spec(shape_rows, idx_fn, off):
        return pl.BlockSpec((shape_rows, tc), lambda j, i, off=off: (idx_fn(i), j + off))

    def body(da_ref, dan_ref, g_ref, gp_ref, gn_ref, v_ref, vp_ref, vn_ref, wg_ref, wv_ref, bg_ref, bv_ref,
             dg_ref, dv_ref, dwg_ref, dwv_ref, dbg_ref, dbv_ref, acc_w, acc_b):
        i = pl.program_id(1)

        @pl.when(i == 0)
        def _():
            acc_w[...] = jnp.zeros_like(acc_w)
            acc_b[...] = jnp.zeros_like(acc_b)

        row = lax.broadcasted_iota(jnp.int32, (SUBLANES, LANES), 0)
        full = lambda ref, k, ls: jnp.broadcast_to(ref[k:k + 1, ls], (SUBLANES, LANES))
        zero = jnp.zeros((SUBLANES, LANES), F32)

        def d_conv_out(da_, ug, uv):
            s = _sig(ug)
            return da_ * uv * (s * (1.0 + ug * (1.0 - s))), da_ * (ug * s)

        for lc in range(tc // LANES):
            ls = slice(lc * LANES, (lc + 1) * LANES)
            wg = [full(wg_ref, k, ls) for k in range(3)]
            wv = [full(wv_ref, k, ls) for k in range(3)]
            bg, bv = full(bg_ref, 0, ls), full(bv_ref, 0, ls)

            def unit(prev_g, g, prev_v, v, da_):
                g1, g2 = _down(prev_g, g, 1, row), _down(prev_g, g, 2, row)
                v1, v2 = _down(prev_v, v, 1, row), _down(prev_v, v, 2, row)
                ug = bg + wg[0] * g2 + wg[1] * g1 + wg[2] * g
                uv = bv + wv[0] * v2 + wv[1] * v1 + wv[2] * v
                dug, duv = d_conv_out(da_, ug, uv)
                return dug, duv, (g2, g1, g), (v2, v1, v)

            def d_in(d, d_next, w):
                return w[2] * d + w[1] * _up(d, d_next, 1, row) + w[0] * _up(d, d_next, 2, row)

            tail = pl.ds(tl - SUBLANES, SUBLANES)
            dgn, dvn, _, _ = unit(g_ref[tail, ls], gn_ref[:, ls], v_ref[tail, ls], vn_ref[:, ls], dan_ref[:, ls])
            dgn = jnp.where(i < nb - 1, dgn, 0.0)
            dvn = jnp.where(i < nb - 1, dvn, 0.0)
            gp0 = jnp.where(i > 0, gp_ref[:, ls], 0.0)
            vp0 = jnp.where(i > 0, vp_ref[:, ls], 0.0)

            def strip(tt, carry):
                dgn, dvn = carry[0], carry[1]
                aw, ab = list(carry[2:8]), list(carry[8:10])
                t = nt - 1 - tt
                nu = FFN_STRIP_BWD // SUBLANES
                rm = pl.multiple_of(jnp.maximum(t * FFN_STRIP_BWD - SUBLANES, 0), SUBLANES)
                gs = [jnp.where(t > 0, g_ref[pl.ds(rm, SUBLANES), ls], gp0)]
                vs = [jnp.where(t > 0, v_ref[pl.ds(rm, SUBLANES), ls], vp0)]
                das = []
                for u in range(nu):
                    rows = pl.ds(pl.multiple_of(t * FFN_STRIP_BWD + u * SUBLANES, SUBLANES), SUBLANES)
                    gs.append(g_ref[rows, ls])
                    vs.append(v_ref[rows, ls])
                    das.append(da_ref[rows, ls])
                dgs, dvs = [None] * nu + [dgn], [None] * nu + [dvn]
                for u in reversed(range(nu)):
                    dgs[u], dvs[u], gsh, vsh = unit(gs[u], gs[u + 1], vs[u], vs[u + 1], das[u])
                    for k in range(3):
                        aw[k] = aw[k] + dgs[u] * gsh[k]
                        aw[3 + k] = aw[3 + k] + dvs[u] * vsh[k]
                    ab[0] = ab[0] + dgs[u]
                    ab[1] = ab[1] + dvs[u]
                for p in range(nu // 2):
                    rows = pl.ds(pl.multiple_of(t * FFN_STRIP_BWD + p * 16, 16), 16)
                    dg_ref[rows, ls] = jnp.concatenate([d_in(dgs[2 * p], dgs[2 * p + 1], wg),
                                                        d_in(dgs[2 * p + 1], dgs[2 * p + 2], wg)], axis=0).astype(BF16)
                    dv_ref[rows, ls] = jnp.concatenate([d_in(dvs[2 * p], dvs[2 * p + 1], wv),
                                                        d_in(dvs[2 * p + 1], dvs[2 * p + 2], wv)], axis=0).astype(BF16)
                return (dgs[0], dvs[0], *aw, *ab)

            res = lax.fori_loop(0, nt, strip, (dgn, dvn) + (zero,) * 8)
            for k in range(3):
                acc_w[0, k, :, ls] += res[2 + k]
                acc_w[1, k, :, ls] += res[5 + k]
            acc_b[0, :, ls] += res[8]
            acc_b[1, :, ls] += res[9]

        @pl.when(i == nb - 1)
        def _():
            for t, (dw_ref, db_ref) in enumerate(((dwg_ref, dbg_ref), (dwv_ref, dbv_ref))):
                dw_ref[...] = jnp.zeros_like(dw_ref)
                for k in range(3):
                    dw_ref[k:k + 1, :] = jnp.sum(acc_w[t, k], axis=0, keepdims=True)
                db_ref[...] = jnp.sum(acc_b[t], axis=0, keepdims=True)

    wspec = lambda off: pl.BlockSpec((SUBLANES, tc), lambda j, i, off=off: (0, j + off))
    bspec = lambda off: pl.BlockSpec((1, tc), lambda j, i, off=off: (0, j + off))
    ospec = lambda off: pl.BlockSpec((tl, tc), lambda j, i, off=off: (i, j + off))
    dg, dv, dwg, dwv, dbg, dbv = _pc(
        body, name=name,
        out_shape=(_sds((n, D_FF), BF16), _sds((n, D_FF), BF16), _sds((SUBLANES, D_FF)), _sds((SUBLANES, D_FF)),
                   _sds((1, D_FF)), _sds((1, D_FF))),
        grid=(ncb, nb),
        in_specs=[spec(tl, cur, 0), spec(hb, nxt, 0),
                  spec(tl, cur, 0), spec(hb, prv, 0), spec(hb, nxt, 0),
                  spec(tl, cur, ncb), spec(hb, prv, ncb), spec(hb, nxt, ncb),
                  wspec(0), wspec(ncb), bspec(0), bspec(ncb)],
        out_specs=(ospec(0), ospec(0), wspec(0), wspec(0), bspec(0), bspec(0)),
        scratch_shapes=[pltpu.VMEM((2, 3, SUBLANES, tc), F32), pltpu.VMEM((2, SUBLANES, tc), F32)],
        compiler_params=_params(2))(da, da, up, up, up, up, up, up, conv_w, conv_w, conv_b, conv_b)
    return dg, dv, jnp.concatenate([dwg, dwv], axis=1), jnp.concatenate([dbg, dbv], axis=1)


def _position():
    return lax.axis_index("x"), lax.axis_index("y"), lax.axis_index("c")


def _all_gather(locals_, name):
    n = len(locals_)

    def body(*refs):
        x_refs, out_refs = refs[:n], refs[n:2 * n]
        send_sems, recv_sems, local_sems = refs[2 * n:]
        x, y, cc = _position()
        me, sibling = (x, y, cc), (x, y, 1 - cc)
        chips = [(1 - x, y), (x, 1 - y), (1 - x, 1 - y)]

        def slot(a, px, py, pc):
            return out_refs[a].at[4 * px + 2 * py + pc]

        def copy(k, a, block, to, own=False):
            return pltpu.make_async_remote_copy(
                src_ref=x_refs[a] if own else slot(a, *block), dst_ref=slot(a, *block),
                send_sem=send_sems.at[k, a], recv_sem=recv_sems.at[k, a], device_id=to, device_id_type=MESH)

        mine = [pltpu.make_async_copy(x_refs[a], slot(a, *me), local_sems.at[a]) for a in range(n)]
        first = [copy(1 + j, a, me, (*chip, cc), own=True) for j, chip in enumerate(chips) for a in range(n)]
        first += [copy(0, a, me, sibling, own=True) for a in range(n)]
        for cp in mine + first:
            cp.start()
        passed = []
        for j, chip in enumerate(chips):
            for a in range(n):
                copy(1 + j, a, (*chip, cc), me).wait_recv()
                cp = copy(4 + j, a, (*chip, cc), sibling)
                cp.start()
                passed.append(cp)
        for a in range(n):
            copy(0, a, sibling, me).wait_recv()
        for j, chip in enumerate(chips):
            for a in range(n):
                copy(4 + j, a, (*chip, 1 - cc), me).wait_recv()
        for cp in first + passed:
            cp.wait_send()
        for cp in mine:
            cp.wait()

    hbm = pl.BlockSpec(memory_space=pl.ANY)
    return _pc(body, name=name, out_shape=[_sds((N_DEV,) + a.shape, a.dtype) for a in locals_],
               in_specs=[hbm] * n, out_specs=[hbm] * n,
               scratch_shapes=[pltpu.SemaphoreType.DMA((7, n)), pltpu.SemaphoreType.DMA((7, n)),
                               pltpu.SemaphoreType.DMA((n,))])(*locals_)


def _peers():
    x, y, cc = _position()
    others = []
    for fx, fy, fc in ((0, 0, 1), (1, 0, 0), (0, 1, 0), (1, 1, 0), (1, 0, 1), (0, 1, 1), (1, 1, 1)):
        p = (1 - x if fx else x, 1 - y if fy else y, 1 - cc if fc else cc)
        others.append((p, 4 * p[0] + 2 * p[1] + p[2]))
    return 4 * x + 2 * y + cc, others


def _gather_start(locals_, after, name):
    n = len(locals_)
    me, _ = _peers()
    lands = [lax.dynamic_update_slice(lax.empty((N_DEV,) + a.shape, a.dtype), a[None], (me,) + (0,) * a.ndim)
             for a in locals_]

    def body(*refs):
        x_refs, land_refs = refs[:n], refs[n:2 * n]
        send_sems, recv_sems, token = refs[2 * n + 1], refs[2 * n + 2], refs[-1]
        me_idx, others = _peers()
        for k, (peer, _) in enumerate(others):
            for a in range(n):
                pltpu.make_async_remote_copy(
                    src_ref=x_refs[a], dst_ref=land_refs[a].at[me_idx], send_sem=send_sems.at[k * n + a],
                    recv_sem=recv_sems.at[k * n + a], device_id=peer, device_id_type=MESH).start()
        token[...] = jnp.zeros_like(token)

    hbm = pl.BlockSpec(memory_space=pltpu.HBM)
    sem = pl.BlockSpec(memory_space=pltpu.SEMAPHORE)
    out = _pc(body, name=name,
              out_shape=(pltpu.SemaphoreType.DMA((7 * n,)), pltpu.SemaphoreType.DMA((7 * n,)),
                         *[pltpu.HBM(a.shape, a.dtype) for a in locals_], *[pltpu.HBM(l.shape, l.dtype) for l in lands],
                         _sds((SUBLANES, LANES))),
              in_specs=[hbm] * (2 * n) + [pl.BlockSpec(memory_space=pl.ANY)],
              out_specs=(sem, sem, *([hbm] * (2 * n)), pl.BlockSpec(memory_space=pltpu.VMEM)),
              input_output_aliases={i: 2 + i for i in range(2 * n)},
              compiler_params=pltpu.CompilerParams(has_side_effects=pltpu.SideEffectType.DATAFLOW_SIDE_EFFECTING))(
        *[pltpu.with_memory_space_constraint(a, pltpu.HBM) for a in locals_],
        *[pltpu.with_memory_space_constraint(l, pltpu.HBM) for l in lands], after)
    return (out[0], out[1], list(out[2:2 + n]), list(out[2 + n:2 + 2 * n])), out[-1]


def _gather_wait(state, after, name):
    send_sems, recv_sems, x_thru, land_thru = state
    n = len(x_thru)

    def body(*refs):
        x_refs, land_refs = refs[:n], refs[n:2 * n]
        send_sems, recv_sems = refs[2 * n], refs[2 * n + 1]
        _, others = _peers()
        for k, (peer, peer_idx) in enumerate(others):
            for a in range(n):
                cp = pltpu.make_async_remote_copy(
                    src_ref=x_refs[a], dst_ref=land_refs[a].at[peer_idx], send_sem=send_sems.at[k * n + a],
                    recv_sem=recv_sems.at[k * n + a], device_id=peer, device_id_type=MESH)
                cp.wait_send()
                cp.wait_recv()

    hbm = pl.BlockSpec(memory_space=pltpu.HBM)
    sem = pl.BlockSpec(memory_space=pltpu.SEMAPHORE)
    out = _pc(body, name=name, out_shape=tuple(pltpu.HBM(a.shape, a.dtype) for a in x_thru + land_thru),
              in_specs=[hbm] * (2 * n) + [sem, sem, pl.BlockSpec(memory_space=pl.ANY)], out_specs=tuple([hbm] * (2 * n)),
              input_output_aliases={i: i for i in range(2 * n)},
              compiler_params=pltpu.CompilerParams(has_side_effects=pltpu.SideEffectType.DATAFLOW_SIDE_EFFECTING))(
        *x_thru, *land_thru, send_sems, recv_sems, after)
    return list(out[n:])


N_CHIPS = 4


def _sibling_swap(parts, name):
    n = len(parts)

    def body(*refs):
        g_refs, got_refs = refs[:n], refs[n:2 * n]
        send_sems, recv_sems = refs[2 * n:]
        x, y, cc = _position()
        swaps = []
        for q in range(N_CHIPS):
            for a in range(n):
                swaps.append(pltpu.make_async_remote_copy(
                    src_ref=g_refs[a].at[2 * q + 1 - cc], dst_ref=got_refs[a].at[q], send_sem=send_sems.at[q, a],
                    recv_sem=recv_sems.at[q, a], device_id=(x, y, 1 - cc), device_id_type=MESH))
        for cp in swaps:
            cp.start()
        for cp in swaps:
            cp.wait_recv()
        for cp in swaps:
            cp.wait_send()

    hbm = pl.BlockSpec(memory_space=pl.ANY)
    return _pc(body, name=name, out_shape=[_sds((N_CHIPS,) + a.shape[1:], a.dtype) for a in parts],
               in_specs=[hbm] * n, out_specs=[hbm] * n,
               scratch_shapes=[pltpu.SemaphoreType.DMA((N_CHIPS, n)), pltpu.SemaphoreType.DMA((N_CHIPS, n))])(*parts)


def _pair_add(part, got, core, name):
    q, a, b = got.shape
    ta = _block_rows(a, b)

    def body(core_ref, k_ref, g_ref, o_ref):
        o_ref[...] = (k_ref[...].astype(F32) + g_ref[...].astype(F32)).astype(BF16)

    spec = pl.BlockSpec((None, ta, b), lambda c, i, core_ref: (c, i, 0))
    own = pl.BlockSpec((None, None, ta, b), lambda c, i, core_ref: (c, core_ref[0], i, 0))
    grid_spec = pltpu.PrefetchScalarGridSpec(num_scalar_prefetch=1, grid=(q, a // ta), in_specs=[own, spec],
                                             out_specs=spec)
    return _pc(body, name=name, out_shape=_sds(got.shape, BF16), grid_spec=grid_spec, compiler_params=_params(2))(
        core, part.reshape((N_CHIPS, 2) + part.shape[1:]), got)


def _chip_exchange(sums, name):
    n = len(sums)

    def body(*refs):
        g_refs, out_refs = refs[:n], refs[n:2 * n]
        send_sems, recv_sems = refs[2 * n:]
        me_q, others = _chip_peers()
        sends, recvs = [], []
        for k, (peer, peer_q) in enumerate(others):
            for a in range(n):
                sends.append(pltpu.make_async_remote_copy(
                    src_ref=g_refs[a].at[peer_q], dst_ref=out_refs[a].at[me_q], send_sem=send_sems.at[k, a],
                    recv_sem=recv_sems.at[k, a], device_id=peer, device_id_type=MESH))
                recvs.append(pltpu.make_async_remote_copy(
                    src_ref=g_refs[a].at[me_q], dst_ref=out_refs[a].at[peer_q], send_sem=send_sems.at[k, a],
                    recv_sem=recv_sems.at[k, a], device_id=peer, device_id_type=MESH))
        for cp in sends:
            cp.start()
        for cp in recvs:
            cp.wait_recv()
        for cp in sends:
            cp.wait_send()

    hbm = pl.BlockSpec(memory_space=pl.ANY)
    return _pc(body, name=name, out_shape=[_sds(a.shape, a.dtype) for a in sums],
               in_specs=[hbm] * n, out_specs=[hbm] * n,
               scratch_shapes=[pltpu.SemaphoreType.DMA((3, n)), pltpu.SemaphoreType.DMA((3, n))])(*sums)


def _chip_peers():
    x, y, cc = _position()
    others = []
    for fx, fy in ((1, 0), (0, 1), (1, 1)):
        px, py = (1 - x if fx else x), (1 - y if fy else y)
        others.append(((px, py, cc), 2 * px + py))
    return 2 * x + y, others


def _chip_exchange_start(sums, name):
    n = len(sums)
    lands = [lax.empty(a.shape, a.dtype) for a in sums]

    def body(*refs):
        g_refs, land_refs = refs[:n], refs[n:2 * n]
        send_sems, recv_sems = refs[2 * n], refs[2 * n + 1]
        me_q, others = _chip_peers()
        for k, (peer, peer_q) in enumerate(others):
            for a in range(n):
                pltpu.make_async_remote_copy(
                    src_ref=g_refs[a].at[peer_q], dst_ref=land_refs[a].at[me_q], send_sem=send_sems.at[k * n + a],
                    recv_sem=recv_sems.at[k * n + a], device_id=peer, device_id_type=MESH).start()

    hbm = pl.BlockSpec(memory_space=pltpu.HBM)
    sem = pl.BlockSpec(memory_space=pltpu.SEMAPHORE)
    out = _pc(body, name=name,
              out_shape=(pltpu.SemaphoreType.DMA((3 * n,)), pltpu.SemaphoreType.DMA((3 * n,)),
                         *[pltpu.HBM(a.shape, a.dtype) for a in sums], *[pltpu.HBM(a.shape, a.dtype) for a in sums]),
              in_specs=[hbm] * (2 * n), out_specs=(sem, sem, *([hbm] * (2 * n))),
              input_output_aliases={i: 2 + i for i in range(2 * n)},
              compiler_params=pltpu.CompilerParams(has_side_effects=pltpu.SideEffectType.DATAFLOW_SIDE_EFFECTING))(
        *[pltpu.with_memory_space_constraint(a, pltpu.HBM) for a in sums],
        *[pltpu.with_memory_space_constraint(l, pltpu.HBM) for l in lands])
    return out[0], out[1], list(out[2:2 + n]), list(out[2 + n:])


def _chip_exchange_wait(state, after, name):
    send_sems, recv_sems, g_thru, land_thru = state
    n = len(g_thru)

    def body(*refs):
        g_refs, land_refs = refs[:n], refs[n:2 * n]
        send_sems, recv_sems = refs[2 * n], refs[2 * n + 1]
        me_q, others = _chip_peers()
        for k, (peer, peer_q) in enumerate(others):
            for a in range(n):
                cp = pltpu.make_async_remote_copy(
                    src_ref=g_refs[a].at[me_q], dst_ref=land_refs[a].at[peer_q], send_sem=send_sems.at[k * n + a],
                    recv_sem=recv_sems.at[k * n + a], device_id=peer, device_id_type=MESH)
                cp.wait_send()
                cp.wait_recv()

    hbm = pl.BlockSpec(memory_space=pltpu.HBM)
    sem = pl.BlockSpec(memory_space=pltpu.SEMAPHORE)
    out = _pc(body, name=name, out_shape=tuple(pltpu.HBM(a.shape, a.dtype) for a in g_thru + land_thru),
              in_specs=[hbm] * (2 * n) + [sem, sem, pl.BlockSpec(memory_space=pl.ANY)], out_specs=tuple([hbm] * (2 * n)),
              input_output_aliases={i: i for i in range(2 * n)},
              compiler_params=pltpu.CompilerParams(has_side_effects=pltpu.SideEffectType.DATAFLOW_SIDE_EFFECTING))(
        *g_thru, *land_thru, send_sems, recv_sems, after)
    return list(out[:n]), list(out[n:])


def _block_rows(a, b):
    ta = a
    while ta * b > 256 * 1024 and ta % 32 == 0:
        ta //= 2
    return ta


def _reduce_adamw(parts, w, m, v, name):
    n_parts, s, a, b = parts.shape
    ta = _block_rows(a, b)

    def body(p_ref, w_ref, m_ref, v_ref, g_out, d_out, m_out, v_out):
        g = p_ref[0].astype(F32)
        for j in range(1, n_parts):
            g = g + p_ref[j].astype(F32)
        delta, m_new, v_new = _adamw(g, w_ref[...], m_ref[...], v_ref[...])
        g_out[...] = g
        d_out[...] = delta
        m_out[...] = m_new
        v_out[...] = v_new

    spec = pl.BlockSpec((None, ta, b), lambda l, i: (l, i, 0))
    return _pc(body, name=name, out_shape=(_sds((s, a, b)),) * 4, grid=(s, a // ta),
               in_specs=[pl.BlockSpec((n_parts, None, ta, b), lambda l, i: (0, l, i, 0)), spec, spec, spec],
               out_specs=(spec,) * 4, compiler_params=_params(2))(parts, w, m, v)


def _adamw(g, w, m, v):
    c1 = 1.0 - ADAM_B1 ** ADAM_STEP
    c2 = 1.0 - ADAM_B2 ** ADAM_STEP
    m_new = ADAM_B1 * m + (1.0 - ADAM_B1) * g
    v_new = ADAM_B2 * v + (1.0 - ADAM_B2) * (g * g)
    delta = -ADAM_LR * ((m_new / c1) / (jnp.sqrt(v_new / c2) + ADAM_EPS) + ADAM_WD * w)
    return delta, m_new, v_new


def _chip_reduce_adamw(own, recv, w, m, v, chip, name):
    s, a, b = w.shape
    ta = _block_rows(a, b)

    def body(chip_ref, *refs):
        p_refs, (w_ref, m_ref, v_ref), (g_out, d_out, m_out, v_out) = refs[:4 * s], refs[4 * s:4 * s + 3], refs[4 * s + 3:]
        layer = pl.program_id(0)
        g = None
        for l in range(s):
            gl = p_refs[4 * l][...].astype(F32)
            for j in range(1, N_CHIPS):
                gl = gl + p_refs[4 * l + j][...].astype(F32)
            g = gl if g is None else jnp.where(layer == l, gl, g)
        delta, m_new, v_new = _adamw(g, w_ref[...], m_ref[...], v_ref[...])
        g_out[...] = g
        d_out[...] = delta
        m_out[...] = m_new
        v_out[...] = v_new

    def part_spec(l, j):
        return pl.BlockSpec((None, ta, b), lambda layer, i, chip_ref, l=l, j=j: (
            (chip_ref[0] + j) % N_CHIPS, jnp.where(layer == l, i, 0), 0))

    spec = pl.BlockSpec((None, ta, b), lambda layer, i, chip_ref: (layer, i, 0))
    in_specs, args = [], []
    for l in range(s):
        for j in range(N_CHIPS):
            in_specs.append(part_spec(l, j))
            args.append(own[l] if j == 0 else recv[l])
    grid_spec = pltpu.PrefetchScalarGridSpec(num_scalar_prefetch=1, grid=(s, a // ta), in_specs=in_specs + [spec] * 3,
                                             out_specs=(spec,) * 4)
    return _pc(body, name=name, out_shape=(_sds((s, a, b)),) * 4, grid_spec=grid_spec, compiler_params=_params(2))(
        chip, *args, w, m, v)


MATRICES = (("ada_mix_w", 2), ("w_in", 2), ("w_ssd_out", 1), ("w_conf_out", 2), ("w_sc_out", 2), ("w_o", 1),
            ("ada_ffn_w", 2), ("w_up", 2), ("w_down", 1))
MIXER_MATRICES = ("ada_mix_w", "w_in", "w_ssd_out", "w_conf_out", "w_sc_out", "w_o")
FFN_MATRICES = ("ada_ffn_w", "w_up", "w_down")
CONV_WEIGHTS = (("ssd_conv_w", 2), ("conf_conv_w", 2), ("sc_conv_w", 2), ("ffn_conv_w", 2))
SHARDED = MATRICES + CONV_WEIGHTS
REPLICATED = ("ada_mix_b", "norm_mix_g", "b_gate", "ssd_conv_b", "ssd_dt_bias", "ssd_a_log", "ssd_d", "ssd_norm_g",
              "conf_conv_b", "conf_ln_g", "conf_ln_b", "ada_ffn_b", "norm_ffn_g", "ffn_conv_b", "final_norm_g")
WEIGHT_NAMES = ("ada_mix_w", "ada_mix_b", "norm_mix_g", "w_in", "b_gate", "ssd_conv_w", "ssd_conv_b", "ssd_dt_bias",
                "ssd_a_log", "ssd_d", "ssd_norm_g", "w_ssd_out", "conf_conv_w", "conf_conv_b", "conf_ln_g",
                "conf_ln_b", "w_conf_out", "sc_conv_w", "w_sc_out", "w_o", "ada_ffn_w", "ada_ffn_b", "norm_ffn_g",
                "w_up", "ffn_conv_w", "ffn_conv_b", "w_down", "final_norm_g")


def _pack_flat(arrays, cols, row_multiple, dtype):
    flat = jnp.concatenate([a.reshape(-1).astype(dtype) for a in arrays])
    rows = -(-flat.shape[0] // cols)
    rows = -(-rows // row_multiple) * row_multiple
    return jnp.pad(flat, (0, rows * cols - flat.shape[0])).reshape(rows, cols)


def _unpack_flat(flat2d, shapes):
    flat = flat2d.reshape(-1)
    out, off = [], 0
    for s in shapes:
        n = 1
        for d in s:
            n *= d
        out.append(flat[off:off + n].reshape(s))
        off += n
    return out


def _cols(g, lo, hi):
    b = g.shape[-1]
    pieces = []
    for k in range(N_DEV):
        a, e = max(lo, k * b), min(hi, (k + 1) * b)
        if a < e:
            pieces.append(g[k, :, a - k * b:e - k * b])
    return pieces[0] if len(pieces) == 1 else jnp.concatenate(pieces, axis=1)


def _rows(g):
    return g.reshape(N_DEV * g.shape[1], g.shape[2])


def _col_shards(segs, b):
    shards = []
    for k in range(N_DEV):
        lo, hi = k * b, (k + 1) * b
        pieces, off = [], 0
        for seg in segs:
            n = seg.shape[1]
            a, e = max(lo, off), min(hi, off + n)
            if a < e:
                pieces.append(seg[:, a - off:e - off])
            off += n
        shards.append(pieces[0] if len(pieces) == 1 else jnp.concatenate(pieces, axis=1))
    return jnp.stack(shards)


def _row_shards(full):
    return full.reshape(N_DEV, full.shape[0] // N_DEV, full.shape[1])


def _pad_rows(a, rows):
    return jnp.pad(a, ((0, rows - a.shape[0]), (0, 0)))


def _pad_lanes(a):
    return jnp.pad(a, ((0, 0), (0, LANES - a.shape[1])))


def _whole(g):
    return _cols(g, 0, N_DEV * g.shape[-1])


def _mixer_weights(full, i):
    row = lambda name: full[name][i].reshape(1, -1)
    conv = lambda name: _whole(full[name][:, i])
    w_in = full["w_in", i]
    return {
        "ada_mix_w": _whole(full["ada_mix_w", i]), "ada_mix_b": row("ada_mix_b"), "norm_mix_g": row("norm_mix_g"),
        "w_z": _cols(w_in, 0, OFF_Z), "w_xbc": _cols(w_in, OFF_Z, OFF_XBC),
        "w_dt": _pad_lanes(_cols(w_in, OFF_XBC, OFF_DT)), "w_conf": _cols(w_in, OFF_DT, OFF_CONF),
        "w_sc": _cols(w_in, OFF_CONF, OFF_SC), "w_gates": _cols(w_in, OFF_SC, N_IN),
        "b_gate": row("b_gate"),
        "ssd_conv_w": _pad_rows(conv("ssd_conv_w"), SUBLANES), "ssd_conv_b": row("ssd_conv_b"),
        "dt_bias": _pad_lanes(row("ssd_dt_bias")), "a_log": _pad_lanes(row("ssd_a_log")),
        "ssd_d": _pad_lanes(row("ssd_d")), "ssd_norm_g": row("ssd_norm_g"), "w_ssd_out": _rows(full["w_ssd_out", i]),
        "conf_conv_w": _pad_rows(conv("conf_conv_w"), CONF_HALO), "conf_conv_b": row("conf_conv_b"),
        "conf_ln_g": row("conf_ln_g"), "conf_ln_b": row("conf_ln_b"), "w_conf_out": _whole(full["w_conf_out", i]),
        "sc_conv_w": _pad_rows(conv("sc_conv_w"), SUBLANES), "w_sc_out": _whole(full["w_sc_out", i]),
        "w_o": _rows(full["w_o", i]),
    }


def _ffn_weights(full, i):
    row = lambda name: full[name][i].reshape(1, -1)
    return {
        "ada_ffn_w": _whole(full["ada_ffn_w", i]), "ada_ffn_b": row("ada_ffn_b"), "norm_ffn_g": row("norm_ffn_g"),
        "w_up": _whole(full["w_up", i]), "ffn_conv_w": _pad_rows(_whole(full["ffn_conv_w"][:, i]), SUBLANES),
        "ffn_conv_b": row("ffn_conv_b"), "w_down": _rows(full["w_down", i]),
    }


def _adaln(sc8, w, b, name):
    mod = _matmul(sc8, w, "nn", F32, name)[0:1, :] + b
    return mod[:, :D_MODEL], mod[:, D_MODEL:2 * D_MODEL], mod[:, 2 * D_MODEL:]


def _mixer_fwd(i, x, prev, sc8, wl):
    t = f"l{i}_"
    s = {}
    shift, scale, gate = _adaln(sc8, wl["ada_mix_w"], wl["ada_mix_b"], t + "ada_mix")
    if prev is None:
        s["x_in"] = x
        s["h"] = _prenorm_first(x, wl["norm_mix_g"], scale, shift, t + "norm_mix")
    else:
        s["x_in"], s["h"] = _prenorm_res(x, prev[0], prev[1], wl["norm_mix_g"], scale, shift, t + "norm_mix")
    s["scale_mix"], s["gate_mix"] = scale, gate
    h = s["h"]
    s["z"] = _matmul(h, wl["w_z"], "nn", F32, t + "in_z")
    s["xbc"] = _matmul(h, wl["w_xbc"], "nn", F32, t + "in_xbc")
    s["dt_raw"] = _matmul(h, wl["w_dt"], "nn", F32, t + "in_dt")
    s["conf"] = _matmul(h, wl["w_conf"], "nn", F32, t + "in_conf")
    s["sc"] = _matmul(h, wl["w_sc"], "nn", F32, t + "in_sc")
    s["gates"] = _matmul(h, wl["w_gates"], "nn", F32, t + "in_gates")
    s["pre"], s["dt"] = _ssd_pre(s["xbc"], s["dt_raw"], wl["ssd_conv_w"], wl["ssd_conv_b"], wl["dt_bias"],
                                 t + "ssd_pre")
    s["y"], s["hprev"] = _ssd_scan(s["pre"], s["dt"], wl["a_log"], wl["ssd_d"], t + "ssd_scan")
    s["ya_in"] = _ssd_post(s["y"], s["z"], wl["ssd_norm_g"], t + "ssd_post")
    s["yb_in"], s["uc"] = _conf_fwd(s["conf"], wl["conf_conv_w"], wl["conf_conv_b"], wl["conf_ln_g"],
                                    wl["conf_ln_b"], t + "conf")
    s["yc_in"] = _sc_fwd(s["sc"], wl["sc_conv_w"], t + "sconv")
    s["ya"] = _matmul(s["ya_in"], wl["w_ssd_out"], "nn", F32, t + "ssd_out")
    s["yb"] = _matmul(s["yb_in"], wl["w_conf_out"], "nn", F32, t + "conf_out")
    s["yc"] = _matmul(s["yc_in"], wl["w_sc_out"], "nn", F32, t + "sc_out")
    s["merged"] = _merge_fwd(s["gates"], s["ya"], s["yb"], s["yc"], wl["b_gate"], t + "merge")
    s["mix"] = _matmul(s["merged"], wl["w_o"], "nn", F32, t + "w_o")
    return s


def _ffn_fwd(i, s, sc8, wl):
    t = f"l{i}_"
    shift2, scale2, gate2 = _adaln(sc8, wl["ada_ffn_w"], wl["ada_ffn_b"], t + "ada_ffn")
    s["x_mid"], s["h2"] = _prenorm_res(s["x_in"], s["mix"], s["gate_mix"], wl["norm_ffn_g"], scale2, shift2,
                                       t + "norm_ffn")
    s["scale_ffn"], s["gate_ffn"] = scale2, gate2
    s["up"] = _matmul(s["h2"], wl["w_up"], "nn", F32, t + "w_up")
    s["a"] = _ffn_mid(s["up"], wl["ffn_conv_w"], wl["ffn_conv_b"], t + "ffn_mid")
    s["out"] = _matmul(s["a"], wl["w_down"], "nn", F32, t + "w_down")
    return s


def _layer_bwd(i, s, wl, sc8, dys_ffn, dx_after, dgate_ffn, prev, emit=None):
    t = f"l{i}_b_"
    g = {}
    da = _matmul(dys_ffn, wl["w_down"], "nt", F32, t + "d_a")
    g["w_down"] = _matmul(s["a"], dys_ffn, "tn", BF16, t + "dw_down")
    dug, duv, dfw, g["ffn_conv_b"] = _ffn_mid_bwd(da, s["up"], wl["ffn_conv_w"], wl["ffn_conv_b"], t + "ffn_mid")
    g["ffn_conv_w"] = dfw[:3]
    dh2 = _matmul_sum_nt([dug, duv], [wl["w_up"][:, :D_FF], wl["w_up"][:, D_FF:]], t + "d_h2")
    g["w_up"] = [_matmul(s["h2"], dug, "tn", BF16, t + "dw_up_g"), _matmul(s["h2"], duv, "tn", BF16, t + "dw_up_v")]
    dx_mid, dshift2, dscale2, g["norm_ffn_g"], dys_mix, dgate_mix = _norm_bwd(
        dh2, s["x_mid"], dx_after, wl["norm_ffn_g"], s["scale_ffn"], t + "norm_ffn", s["mix"], s["gate_mix"])
    dmod_ffn = jnp.concatenate([dshift2, dscale2, dgate_ffn], axis=1)
    g["ada_ffn_b"] = dmod_ffn
    g["ada_ffn_w"] = [_matmul(sc8, _pad_rows(dmod_ffn, SUBLANES), "tn", BF16, t + "dw_ada_ffn")]
    if emit is not None:
        emit(i, "ffn", g)
    dmerged = _matmul(dys_mix, wl["w_o"], "nt", F32, t + "d_merged")
    g["w_o"] = _matmul(s["merged"], dys_mix, "tn", BF16, t + "dw_o")
    dya, dyb, dyc, dgates, g["b_gate"] = _merge_bwd(dmerged, s["gates"], s["ya"], s["yb"], s["yc"], wl["b_gate"],
                                                    t + "merge")
    dya_in = _matmul(dya, wl["w_ssd_out"], "nt", F32, t + "d_ya_in")
    g["w_ssd_out"] = _matmul(s["ya_in"], dya, "tn", BF16, t + "dw_ssd_out")
    dyb_in = _matmul(dyb, wl["w_conf_out"], "nt", F32, t + "d_yb_in")
    g["w_conf_out"] = [_matmul(s["yb_in"], dyb, "tn", BF16, t + "dw_conf_out")]
    dyc_in = _matmul(dyc, wl["w_sc_out"], "nt", F32, t + "d_yc_in")
    g["w_sc_out"] = [_matmul(s["yc_in"], dyc, "tn", BF16, t + "dw_sc_out")]
    dy, dz, g["ssd_norm_g"] = _ssd_post_bwd(dya_in, s["y"], s["z"], wl["ssd_norm_g"], t + "ssd_post")
    dpre, ddt, da_log, dd = _ssd_scan_bwd(s["pre"], s["dt"], s["hprev"], dy, wl["a_log"], wl["ssd_d"],
                                          t + "ssd_scan")
    g["ssd_a_log"], g["ssd_d"] = da_log[:, :SSD_HEADS], dd[:, :SSD_HEADS]
    dxbc, ddt_raw, dcw, g["ssd_conv_b"], ddtb = _ssd_pre_bwd(dpre, s["xbc"], ddt, s["dt_raw"], wl["ssd_conv_w"],
                                                             wl["dt_bias"], t + "ssd_pre")
    g["ssd_conv_w"], g["ssd_dt_bias"] = dcw[:4], ddtb[:, :SSD_HEADS]
    dconf, dccw, g["conf_conv_b"], g["conf_ln_g"], g["conf_ln_b"] = _conf_bwd(
        dyb_in, s["uc"], s["conf"], wl["conf_conv_w"], wl["conf_ln_g"], wl["conf_ln_b"], t + "conf")
    g["conf_conv_w"] = dccw[:CONF_KERNEL]
    dsc, dscw = _sc_bwd(dyc_in, s["sc"], wl["sc_conv_w"], t + "sconv")
    g["sc_conv_w"] = dscw[:3]
    segs = (("z", dz, "w_z"), ("xbc", dxbc, "w_xbc"), ("dt", ddt_raw, "w_dt"), ("conf", dconf, "w_conf"),
            ("sc", dsc, "w_sc"), ("gates", dgates, "w_gates"))
    dh = _matmul_sum_nt([dseg for _, dseg, _ in segs], [wl[wname] for _, _, wname in segs], t + "d_h")
    dw_segs = []
    for nm, dseg, wname in segs:
        dw = _matmul(s["h"], dseg, "tn", BF16, t + "dw_in_" + nm)
        dw_segs.append(dw[:, :SSD_HEADS] if nm == "dt" else dw)
    g["w_in"] = dw_segs
    if prev is None:
        dx_in, dshift, dscale, g["norm_mix_g"] = _norm_bwd(dh, s["x_in"], dx_mid, wl["norm_mix_g"], s["scale_mix"],
                                                          t + "norm_mix")
        back = None
    else:
        dx_in, dshift, dscale, g["norm_mix_g"], dys_prev, dgate_prev = _norm_bwd(
            dh, s["x_in"], dx_mid, wl["norm_mix_g"], s["scale_mix"], t + "norm_mix", prev[0], prev[1])
        back = (dys_prev, dgate_prev)
    dmod_mix = jnp.concatenate([dshift, dscale, dgate_mix], axis=1)
    g["ada_mix_b"] = dmod_mix
    g["ada_mix_w"] = [_matmul(sc8, _pad_rows(dmod_mix, SUBLANES), "tn", BF16, t + "dw_ada_mix")]
    return g, dx_in, back


def _device_step(x, c, target, full, fetch=None, emit=None):
    fetch = fetch or {}
    full = dict(full)
    sc8 = _pad_rows(c * (1.0 / (1.0 + jnp.exp(-c))), SUBLANES)
    wls, saved, prev, xcur = [], [], None, x
    for i in range(DEPTH):
        if (i, "mixer") in fetch:
            full.update(fetch[i, "mixer"](prev[0]))
        wl = _mixer_weights(full, i)
        s = _mixer_fwd(i, xcur, prev, sc8, wl)
        if (i, "ffn") in fetch:
            full.update(fetch[i, "ffn"](s["mix"]))
        wf = _ffn_weights(full, i)
        _ffn_fwd(i, s, sc8, wf)
        wls.append({**wl, **wf})
        saved.append(s)
        xcur, prev = s["x_mid"], (s["out"], s["gate_ffn"])
    gf = full["final_norm_g"].reshape(1, -1)
    last = saved[-1]
    loss, dx, dys, dgate, dgf = _final_loss(last["x_mid"], last["out"], last["gate_ffn"], gf, target, "final_loss")
    grads = [None] * DEPTH
    for i in reversed(range(DEPTH)):
        prev = None if i == 0 else (saved[i - 1]["out"], saved[i - 1]["gate_ffn"])
        grads[i], dx, back = _layer_bwd(i, saved[i], wls[i], sc8, dys, dx, dgate, prev, emit)
        if emit is not None:
            emit(i, "mixer", grads[i])
        if back is not None:
            dys, dgate = back
    return loss[0, 0], dx, grads, dgf


def _step(x, c, target, weights, moments_m, moments_v):
    sharded_names = [n for n, _ in SHARDED]
    conv_names = [n for n, _ in CONV_WEIGHTS]
    shard = lambda key: weights[key[0]][key[1]].astype(BF16)
    first = [(n, 0) for n in MIXER_MATRICES]
    later = {(0, "ffn"): [(n, 0) for n in FFN_MATRICES], (1, "mixer"): [(n, 1) for n in MIXER_MATRICES + FFN_MATRICES]}
    gathered = _all_gather([shard(k) for k in first] + [weights[n] for n in conv_names], "gather_first")
    full = {n: weights[n] for n in REPLICATED}
    full.update(zip(first + conv_names, gathered))
    fetch, after = {}, gathered[0]
    for stage, keys in later.items():
        state, after = _gather_start([shard(k) for k in keys], after, f"gather_l{stage[0]}_{stage[1]}_start")
        fetch[stage] = functools.partial(
            lambda act, state, keys, nm: dict(zip(keys, _gather_wait(state, act, nm))),
            state=state, keys=keys, nm=f"gather_l{stage[0]}_{stage[1]}_wait")
    axis_of = dict(SHARDED)
    core = lax.axis_index("c").astype(jnp.int32).reshape(1)
    chip = (2 * lax.axis_index("x") + lax.axis_index("y")).astype(jnp.int32).reshape(1)
    ffn_names = list(FFN_MATRICES) + ["ffn_conv_w"]
    mixer_names = [n for n in sharded_names if n not in ffn_names]
    sums, received, pending = {}, {}, []

    def send(i, names, grads_i, last):
        keys = [(n, i) for n in names]
        parts = []
        for n in names:
            gw = grads_i[n]
            part = _row_shards(gw) if axis_of[n] == 1 else _col_shards(gw if isinstance(gw, list) else [gw],
                                                                       weights[n].shape[-1])
            parts.append(part.astype(BF16))
        tag = f"l{i}_{'ffn' if names is ffn_names else 'mixer'}"
        got = _sibling_swap(parts, "swap_grads_" + tag)
        pair = [_pair_add(p, g, core, f"pair_add_{n}_{i}") for n, p, g in zip(names, parts, got)]
        if last:
            sums.update(zip(keys, pair))
            received.update(zip(keys, _chip_exchange(pair, "exchange_grads_" + tag)))
        else:
            pending.append((keys, _chip_exchange_start(pair, "exchange_grads_" + tag + "_start"), tag))

    def emit(i, kind, grads_i):
        if kind == "ffn" and i == 0:
            send(0, ffn_names, grads_i, False)
        elif kind == "mixer":
            send(i, mixer_names if i == 0 else sharded_names, grads_i, i == 0)

    loss, grad_x, grads, dgf = _device_step(x[0], c + after[0:1, 0:1], target[0], full, fetch, emit)
    for keys, state, tag in pending:
        own, got = _chip_exchange_wait(state, grad_x, "exchange_grads_" + tag + "_wait")
        sums.update(zip(keys, own))
        received.update(zip(keys, got))
    big = {n: _chip_reduce_adamw([sums[n, i] for i in range(DEPTH)], [received[n, i] for i in range(DEPTH)],
                                 weights[n], moments_m[n], moments_v[n], chip, "adamw_" + n)
           for n in sharded_names}
    rep_grads = [dgf if n == "final_norm_g" else jnp.stack([grads[i][n].reshape(-1) for i in range(DEPTH)])
                 for n in REPLICATED]
    small_parts, = _all_gather([_pack_flat(rep_grads, LANES, SUBLANES, F32)], "gather_small_grads")
    pack_s = lambda d: _pack_flat([d[n] for n in REPLICATED], LANES, SUBLANES, F32)[None]
    small = _reduce_adamw(small_parts[:, None], pack_s(weights), pack_s(moments_m), pack_s(moments_v),
                          "adamw_replicated")
    small = [_unpack_flat(b, [weights[n].shape for n in REPLICATED]) for b in small]
    results = []
    for kind in range(4):
        by_name = {n: big[n][kind] for n in sharded_names}
        by_name.update(zip(REPLICATED, small[kind]))
        results.append([by_name[n] for n in WEIGHT_NAMES])
    loss = lax.psum(loss, ("x", "y", "c"))
    return (loss, grad_x[None], *results[0], *results[1], *results[2], *results[3])


def kernel(x, c, ada_mix_w, ada_mix_b, norm_mix_g, w_in, b_gate, ssd_conv_w, ssd_conv_b, ssd_dt_bias, ssd_a_log, ssd_d, ssd_norm_g, w_ssd_out, conf_conv_w, conf_conv_b, conf_ln_g, conf_ln_b, w_conf_out, sc_conv_w, w_sc_out, w_o, ada_ffn_w, ada_ffn_b, norm_ffn_g, w_up, ffn_conv_w, ffn_conv_b, w_down, final_norm_g, loss_target, m_ada_mix_w, m_ada_mix_b, m_norm_mix_g, m_w_in, m_b_gate, m_ssd_conv_w, m_ssd_conv_b, m_ssd_dt_bias, m_ssd_a_log, m_ssd_d, m_ssd_norm_g, m_w_ssd_out, m_conf_conv_w, m_conf_conv_b, m_conf_ln_g, m_conf_ln_b, m_w_conf_out, m_sc_conv_w, m_w_sc_out, m_w_o, m_ada_ffn_w, m_ada_ffn_b, m_norm_ffn_g, m_w_up, m_ffn_conv_w, m_ffn_conv_b, m_w_down, m_final_norm_g, v_ada_mix_w, v_ada_mix_b, v_norm_mix_g, v_w_in, v_b_gate, v_ssd_conv_w, v_ssd_conv_b, v_ssd_dt_bias, v_ssd_a_log, v_ssd_d, v_ssd_norm_g, v_w_ssd_out, v_conf_conv_w, v_conf_conv_b, v_conf_ln_g, v_conf_ln_b, v_w_conf_out, v_sc_conv_w, v_w_sc_out, v_w_o, v_ada_ffn_w, v_ada_ffn_b, v_norm_ffn_g, v_w_up, v_ffn_conv_w, v_ffn_conv_b, v_w_down, v_final_norm_g):
    given = dict(locals())
    weights = {n: given[n] for n in WEIGHT_NAMES}
    moments_m = {n: given["m_" + n] for n in WEIGHT_NAMES}
    moments_v = {n: given["v_" + n] for n in WEIGHT_NAMES}
    return _step(x, c, loss_target, weights, moments_m, moments_v)
```

```python
import functools

import jax
import jax.numpy as jnp
from jax import lax
from jax.experimental import pallas as pl
from jax.experimental.pallas import tpu as pltpu

F32 = jnp.float32
BF16 = jnp.bfloat16
MESH = pl.DeviceIdType.MESH

N_DEV = 8
DEPTH = 2
D_MODEL = 1024
SSD_HEADS = 16
SSD_HEAD_DIM = 64
SSD_INNER = 1024
SSD_STATE = 64
SSD_CHUNK = 128
SSD_XBC = 1280
CONF_WIDTH = 512
CONF_KERNEL = 31
SC_WIDTH = 512
D_FF = 2816
EPS = 1e-6
OFF_Z, OFF_XBC, OFF_DT, OFF_CONF, OFF_SC, N_IN = 1024, 2304, 2320, 3344, 4880, 7952

ADAM_LR, ADAM_B1, ADAM_B2, ADAM_EPS, ADAM_WD, ADAM_STEP = 0.001, 0.9, 0.999, 1e-08, 0.01, 10

LANES = 128
SUBLANES = 8
VMEM_LIMIT = 56 * 1024 * 1024
ROW_TILE = 256

NN = (((1,), (0,)), ((), ()))
NT = (((1,), (1,)), ((), ()))
TN = (((0,), (0,)), ((), ()))


def _params(n_axes):
    return pltpu.CompilerParams(dimension_semantics=("arbitrary",) * n_axes, vmem_limit_bytes=VMEM_LIMIT)


def _pc(body, **kw):
    return pl.pallas_call(body, **kw)


def _dot(a, b, dn=NN, precision=None):
    return lax.dot_general(a, b, dn, precision=precision, preferred_element_type=F32)


def _split3(x):
    hi = x.astype(BF16)
    r1 = x - hi.astype(F32)
    mid = r1.astype(BF16)
    return hi, mid, (r1 - mid.astype(F32)).astype(BF16)


def _dot_sel(x, sel):
    hi, mid, lo = _split3(x)
    return _dot(hi, sel) + _dot(mid, sel) + _dot(lo, sel)


def _sel_dot(sel, x):
    hi, mid, lo = _split3(x)
    return _dot(sel, hi) + _dot(sel, mid) + _dot(sel, lo)


def _sig(x):
    return 1.0 / (1.0 + jnp.exp(-x))


def _fold(v):
    r, c = v.shape
    return v.reshape(r // SUBLANES, SUBLANES, c).sum(axis=0)


def _tile(n_rows):
    return min(ROW_TILE, n_rows // 2)


def _row(tl, c, col=0):
    return pl.BlockSpec((tl, c), lambda i, col=col: (i, col))


def _prev(tl, hb, c, col=0):
    r = tl // hb
    return pl.BlockSpec((hb, c), lambda i, col=col: (jnp.maximum(i * r - 1, 0), col))


def _next(tl, hb, c, n_rows, col=0):
    r = tl // hb
    last = n_rows // hb - 1
    return pl.BlockSpec((hb, c), lambda i, col=col: (jnp.minimum((i + 1) * r, last), col))


def _const(shape):
    return pl.BlockSpec(shape, lambda i: (0,) * len(shape))


def _sds(shape, dtype=F32):
    return jax.ShapeDtypeStruct(shape, dtype)


MM_TILE = 1536
MM_FULL_K = 3072
MM_K_TILE = 1024


def _pick(dim, target):
    if dim <= target:
        return dim
    best = None
    for t in range(LANES, target + 1, LANES):
        if dim % t == 0:
            best = t
    assert best is not None, (dim, target)
    return best


def _matmul(a, b, mode, out_dtype, name):
    if mode == "nn":
        (m, k), (k2, n) = a.shape, b.shape
    elif mode == "nt":
        (m, k), (n, k2) = a.shape, b.shape
    else:
        (k, m), (k2, n) = a.shape, b.shape
    assert k == k2, (a.shape, b.shape, mode)
    tm, tn = _pick(m, MM_TILE), _pick(n, MM_TILE)
    tk = k if k <= MM_FULL_K else _pick(k, MM_K_TILE)
    nk = k // tk
    dn = {"nn": NN, "nt": NT, "tn": TN}[mode]

    def body_one(a_ref, b_ref, o_ref):
        o_ref[...] = _dot(a_ref[...].astype(BF16), b_ref[...].astype(BF16), dn).astype(out_dtype)

    def body_acc(a_ref, b_ref, o_ref, acc):
        kk = pl.program_id(2)

        @pl.when(kk == 0)
        def _():
            acc[...] = jnp.zeros_like(acc)

        acc[...] += _dot(a_ref[...].astype(BF16), b_ref[...].astype(BF16), dn)

        @pl.when(kk == nk - 1)
        def _():
            o_ref[...] = acc[...].astype(out_dtype)

    a_spec = {"nn": pl.BlockSpec((tm, tk), lambda i, j, kk: (i, kk)),
              "nt": pl.BlockSpec((tm, tk), lambda i, j, kk: (i, kk)),
              "tn": pl.BlockSpec((tk, tm), lambda i, j, kk: (kk, i))}[mode]
    b_spec = {"nn": pl.BlockSpec((tk, tn), lambda i, j, kk: (kk, j)),
              "nt": pl.BlockSpec((tn, tk), lambda i, j, kk: (j, kk)),
              "tn": pl.BlockSpec((tk, tn), lambda i, j, kk: (kk, j))}[mode]
    o_spec = pl.BlockSpec((tm, tn), lambda i, j, kk: (i, j))
    return _pc(body_one if nk == 1 else body_acc, name=name, out_shape=_sds((m, n), out_dtype),
               grid=(m // tm, n // tn, nk), in_specs=[a_spec, b_spec], out_specs=o_spec,
               scratch_shapes=[] if nk == 1 else [pltpu.VMEM((tm, tn), F32)], compiler_params=_params(3))(a, b)


SUM_NT_TILE = 512


def _matmul_sum_nt(a_list, b_list, name):
    m, n = a_list[0].shape[0], b_list[0].shape[0]
    cnt = len(a_list)
    tm, tn = _pick(m, SUM_NT_TILE), _pick(n, SUM_NT_TILE)

    def body(*refs):
        a_refs, b_refs, o_ref = refs[:cnt], refs[cnt:2 * cnt], refs[2 * cnt]
        acc = _dot(a_refs[0][...].astype(BF16), b_refs[0][...].astype(BF16), NT)
        for t in range(1, cnt):
            acc = acc + _dot(a_refs[t][...].astype(BF16), b_refs[t][...].astype(BF16), NT)
        o_ref[...] = acc

    in_specs = [pl.BlockSpec((tm, a.shape[1]), lambda j, i: (i, 0)) for a in a_list]
    in_specs += [pl.BlockSpec((tn, b.shape[1]), lambda j, i: (j, 0)) for b in b_list]
    return _pc(body, name=name, out_shape=_sds((m, n)), grid=(n // tn, m // tm), in_specs=in_specs,
               out_specs=pl.BlockSpec((tm, tn), lambda j, i: (i, j)), compiler_params=_params(2))(*a_list, *b_list)


def _norm_mod(x, g, scale, shift):
    r = lax.rsqrt(jnp.mean(x * x, axis=-1, keepdims=True) + EPS)
    return ((x * r) * g) * (1.0 + scale) + shift


def _prenorm_first(x, g, scale, shift, name):
    n, d = x.shape
    tl = _tile(n)

    def body(x_ref, g_ref, sc_ref, sh_ref, h_ref):
        h_ref[...] = _norm_mod(x_ref[...], g_ref[...], sc_ref[...], sh_ref[...]).astype(BF16)

    return _pc(body, name=name, out_shape=_sds((n, d), BF16), grid=(n // tl,),
               in_specs=[_row(tl, d)] + [_const((1, d))] * 3, out_specs=_row(tl, d),
               compiler_params=_params(1))(x, g, scale, shift)


def _prenorm_res(x, y, gate, g, scale, shift, name):
    n, d = x.shape
    tl = _tile(n)

    def body(x_ref, y_ref, gate_ref, g_ref, sc_ref, sh_ref, xo_ref, h_ref):
        xn = x_ref[...] + gate_ref[...] * y_ref[...]
        xo_ref[...] = xn
        h_ref[...] = _norm_mod(xn, g_ref[...], sc_ref[...], sh_ref[...]).astype(BF16)

    return _pc(body, name=name, out_shape=(_sds((n, d)), _sds((n, d), BF16)), grid=(n // tl,),
               in_specs=[_row(tl, d), _row(tl, d)] + [_const((1, d))] * 4,
               out_specs=(_row(tl, d), _row(tl, d)), compiler_params=_params(1))(x, y, gate, g, scale, shift)


def _final_loss(x, y, gate, gf, target, name):
    n, d = x.shape
    tl = _tile(n)
    nb = n // tl

    def body(x_ref, y_ref, gate_ref, gf_ref, t_ref, loss_ref, dx_ref, dys_ref, dgate_ref, dgf_ref,
             acc_l, acc_gate, acc_gf):
        i = pl.program_id(0)

        @pl.when(i == 0)
        def _():
            acc_l[...] = jnp.zeros_like(acc_l)
            acc_gate[...] = jnp.zeros_like(acc_gate)
            acc_gf[...] = jnp.zeros_like(acc_gf)

        yv = y_ref[...]
        gate = gate_ref[...]
        gf = gf_ref[...]
        x2 = x_ref[...] + gate * yv
        r = lax.rsqrt(jnp.mean(x2 * x2, axis=-1, keepdims=True) + EPS)
        xn = x2 * r
        e = xn * gf - t_ref[...]
        acc_l[...] += _fold(e * e)
        dy = e * (1.0 / d)
        acc_gf[...] += _fold(dy * xn)
        dxn = dy * gf
        dx = r * (dxn - xn * jnp.mean(dxn * xn, axis=-1, keepdims=True))
        dx_ref[...] = dx
        dys_ref[...] = (dx * gate).astype(BF16)
        acc_gate[...] += _fold(dx * yv)

        @pl.when(i == nb - 1)
        def _():
            loss_ref[...] = jnp.full((SUBLANES, LANES), 0.5 / d, F32) * jnp.sum(acc_l[...])
            dgate_ref[...] = jnp.sum(acc_gate[...], axis=0, keepdims=True)
            dgf_ref[...] = jnp.sum(acc_gf[...], axis=0, keepdims=True)

    return _pc(body, name=name,
               out_shape=(_sds((SUBLANES, LANES)), _sds((n, d)), _sds((n, d), BF16), _sds((1, d)), _sds((1, d))),
               grid=(nb,),
               in_specs=[_row(tl, d), _row(tl, d), _const((1, d)), _const((1, d)), _row(tl, d)],
               out_specs=(_const((SUBLANES, LANES)), _row(tl, d), _row(tl, d), _const((1, d)), _const((1, d))),
               scratch_shapes=[pltpu.VMEM((SUBLANES, d), F32)] * 3,
               compiler_params=_params(1))(x, y, gate, gf, target)


def _norm_bwd(dh, x, dxo, g, scale, name, y_prev=None, gate_prev=None):
    n, d = x.shape
    tl = _tile(n)
    nb = n // tl
    has_prev = y_prev is not None

    def body(*refs):
        if has_prev:
            (dh_ref, x_ref, dxo_ref, g_ref, sc_ref, yp_ref, gp_ref,
             dx_ref, dsh_ref, dsc_ref, dg_ref, dys_ref, dgp_ref, acc_sh, acc_s, acc_gp) = refs
        else:
            (dh_ref, x_ref, dxo_ref, g_ref, sc_ref,
             dx_ref, dsh_ref, dsc_ref, dg_ref, acc_sh, acc_s) = refs
        i = pl.program_id(0)

        @pl.when(i == 0)
        def _():
            acc_sh[...] = jnp.zeros_like(acc_sh)
            acc_s[...] = jnp.zeros_like(acc_s)
            if has_prev:
                acc_gp[...] = jnp.zeros_like(acc_gp)

        x_ = x_ref[...]
        dh_ = dh_ref[...]
        g_ = g_ref[...]
        one_sc = 1.0 + sc_ref[...]
        r = lax.rsqrt(jnp.mean(x_ * x_, axis=-1, keepdims=True) + EPS)
        xn = x_ * r
        dxn = dh_ * (g_ * one_sc)
        dx = dxo_ref[...] + r * (dxn - xn * jnp.mean(dxn * xn, axis=-1, keepdims=True))
        dx_ref[...] = dx
        acc_sh[...] += _fold(dh_)
        acc_s[...] += _fold(dh_ * xn)
        if has_prev:
            dys_ref[...] = (dx * gp_ref[...]).astype(BF16)
            acc_gp[...] += _fold(dx * yp_ref[...])

        @pl.when(i == nb - 1)
        def _():
            s = jnp.sum(acc_s[...], axis=0, keepdims=True)
            dsh_ref[...] = jnp.sum(acc_sh[...], axis=0, keepdims=True)
            dsc_ref[...] = s * g_
            dg_ref[...] = s * one_sc
            if has_prev:
                dgp_ref[...] = jnp.sum(acc_gp[...], axis=0, keepdims=True)

    vec = _sds((1, d))
    in_specs = [_row(tl, d)] * 3 + [_const((1, d))] * 2
    out_shape = [_sds((n, d)), vec, vec, vec]
    out_specs = [_row(tl, d)] + [_const((1, d))] * 3
    scratch = [pltpu.VMEM((SUBLANES, d), F32)] * 2
    args = [dh, x, dxo, g, scale]
    if has_prev:
        in_specs += [_row(tl, d), _const((1, d))]
        out_shape += [_sds((n, d), BF16), vec]
        out_specs += [_row(tl, d), _const((1, d))]
        scratch += [pltpu.VMEM((SUBLANES, d), F32)]
        args += [y_prev, gate_prev]
    return _pc(body, name=name, out_shape=tuple(out_shape), grid=(nb,), in_specs=in_specs,
               out_specs=tuple(out_specs), scratch_shapes=scratch, compiler_params=_params(1))(*args)


CONV_HALO = 8
CONF_HALO = 32


def _ssd_pre(xbc, dt_raw, conv_w, conv_b, dt_bias, name):
    n, c = xbc.shape
    tl = _tile(n)
    hb = CONV_HALO
    k_taps = 4

    def body(x_ref, xp_ref, dt_ref, w_ref, b_ref, dtb_ref, pre_ref, dts_ref, buf):
        i = pl.program_id(0)
        buf[pl.ds(0, hb), :] = jnp.where(i > 0, xp_ref[...], 0.0)
        buf[pl.ds(hb, tl), :] = x_ref[...]
        acc = b_ref[...] + w_ref[0:1, :] * buf[pl.ds(hb - 3, tl), :]
        for k in range(1, k_taps):
            acc = acc + w_ref[k:k + 1, :] * buf[pl.ds(hb - 3 + k, tl), :]
        pre_ref[...] = acc
        v = dt_ref[...] + dtb_ref[...]
        dts_ref[...] = jnp.maximum(v, 0.0) + jnp.log1p(jnp.exp(-jnp.abs(v)))

    return _pc(body, name=name, out_shape=(_sds((n, c)), _sds((n, LANES))), grid=(n // tl,),
               in_specs=[_row(tl, c), _prev(tl, hb, c), _row(tl, LANES), _const((SUBLANES, c)), _const((1, c)),
                         _const((1, LANES))],
               out_specs=(_row(tl, c), _row(tl, LANES)),
               scratch_shapes=[pltpu.VMEM((tl + hb, c), F32)], compiler_params=_params(1))(
        xbc, xbc, dt_raw, conv_w, conv_b, dt_bias)


def _ssd_pre_bwd(dpre, xbc, ddt, dt_raw, conv_w, dt_bias, name):
    n, c = xbc.shape
    tl = _tile(n)
    nb = n // tl
    hb = CONV_HALO
    k_taps = 4

    def body(dp_ref, dpn_ref, x_ref, xp_ref, ddt_ref, dt_ref, w_ref, dtb_ref,
             dx_ref, ddr_ref, dw_ref, db_ref, ddtb_ref, dbuf, xbuf, acc_w, acc_b, acc_dtb):
        i = pl.program_id(0)

        @pl.when(i == 0)
        def _():
            acc_w[...] = jnp.zeros_like(acc_w)
            acc_b[...] = jnp.zeros_like(acc_b)
            acc_dtb[...] = jnp.zeros_like(acc_dtb)

        dp = dp_ref[...]
        dbuf[pl.ds(0, tl), :] = dp
        dbuf[pl.ds(tl, hb), :] = jnp.where(i < nb - 1, dpn_ref[...], 0.0)
        xbuf[pl.ds(0, hb), :] = jnp.where(i > 0, xp_ref[...], 0.0)
        xbuf[pl.ds(hb, tl), :] = x_ref[...]
        dx = w_ref[0:1, :] * dbuf[pl.ds(3, tl), :]
        for k in range(1, k_taps):
            dx = dx + w_ref[k:k + 1, :] * dbuf[pl.ds(3 - k, tl), :]
        dx_ref[...] = dx.astype(BF16)
        for k in range(k_taps):
            acc_w[k] += _fold(dp * xbuf[pl.ds(hb - 3 + k, tl), :])
        acc_b[...] += _fold(dp)
        ddr = ddt_ref[...] * _sig(dt_ref[...] + dtb_ref[...])
        ddr_ref[...] = ddr.astype(BF16)
        acc_dtb[...] += _fold(ddr)

        @pl.when(i == nb - 1)
        def _():
            dw_ref[...] = jnp.zeros_like(dw_ref)
            for k in range(k_taps):
                dw_ref[k:k + 1, :] = jnp.sum(acc_w[k], axis=0, keepdims=True)
            db_ref[...] = jnp.sum(acc_b[...], axis=0, keepdims=True)
            ddtb_ref[...] = jnp.sum(acc_dtb[...], axis=0, keepdims=True)

    return _pc(body, name=name,
               out_shape=(_sds((n, c), BF16), _sds((n, LANES), BF16), _sds((SUBLANES, c)), _sds((1, c)),
                          _sds((1, LANES))),
               grid=(nb,),
               in_specs=[_row(tl, c), _next(tl, hb, c, n), _row(tl, c), _prev(tl, hb, c), _row(tl, LANES),
                         _row(tl, LANES), _const((SUBLANES, c)), _const((1, LANES))],
               out_specs=(_row(tl, c), _row(tl, LANES), _const((SUBLANES, c)), _const((1, c)), _const((1, LANES))),
               scratch_shapes=[pltpu.VMEM((tl + hb, c), F32), pltpu.VMEM((tl + hb, c), F32),
                               pltpu.VMEM((k_taps, SUBLANES, c), F32), pltpu.VMEM((SUBLANES, c), F32),
                               pltpu.VMEM((SUBLANES, LANES), F32)],
               compiler_params=_params(1))(dpre, dpre, xbc, xbc, ddt, dt_raw, conv_w, dt_bias)


def _expand_mat():
    r = lax.broadcasted_iota(jnp.int32, (LANES, SSD_INNER), 0)
    c = lax.broadcasted_iota(jnp.int32, (LANES, SSD_INNER), 1)
    return (jnp.right_shift(c, 6) == r).astype(BF16)


def _reduce_mat():
    r = lax.broadcasted_iota(jnp.int32, (SSD_INNER, LANES), 0)
    c = lax.broadcasted_iota(jnp.int32, (SSD_INNER, LANES), 1)
    return (jnp.right_shift(r, 6) == c).astype(BF16)


def _ssd_common(pre, dt, alog):
    q = SSD_CHUNK
    sg = _sig(pre)
    act = pre * sg
    lane = lax.broadcasted_iota(jnp.int32, (1, LANES), 1)
    a_neg = jnp.where(lane < SSD_HEADS, -jnp.exp(alog), 0.0)
    rr = lax.broadcasted_iota(jnp.int32, (q, q), 0)
    cc = lax.broadcasted_iota(jnp.int32, (q, q), 1)
    causal = rr >= cc
    cum = _sel_dot(causal.astype(BF16), dt * a_neg)
    e_mat = _expand_mat()
    dtx = _dot_sel(dt, e_mat)
    cumx = _dot_sel(cum, e_mat)
    return sg, act, a_neg, causal, cum, e_mat, dtx, cumx


def _ssd_scan(pre, dt, alog, dvec, name):
    n = pre.shape[0]
    q = SSD_CHUNK
    nc = n // q

    def body(pre_ref, dt_ref, alog_ref, d_ref, y_ref, hp_ref, state):
        i = pl.program_id(0)

        @pl.when(i == 0)
        def _():
            state[...] = jnp.zeros_like(state)

        dt_ = dt_ref[...]
        _, act, _, causal, cum, e_mat, dtx, cumx = _ssd_common(pre_ref[...], dt_, alog_ref[...])
        xs = act[:, :SSD_INNER]
        bm = act[:, SSD_INNER:SSD_INNER + LANES]
        cm = act[:, SSD_INNER + LANES:]
        cum_t = cum.T
        clx = cumx[q - 1:q, :]
        xc = xs * dtx
        xd = xc * jnp.exp(clx - cumx)
        doutx = jnp.exp(cumx)
        edec = jnp.exp(clx)
        dx_row = _dot_sel(jnp.broadcast_to(d_ref[...], (SUBLANES, LANES)), e_mat)[0:1, :]
        hp_ref[0] = state[...]
        bb = bm.astype(BF16)
        cb = cm.astype(BF16)
        lane = lax.broadcasted_iota(jnp.int32, (1, LANES), 1)
        row = lax.broadcasted_iota(jnp.int32, (LANES, 1), 0)
        cbs = []
        for g in range(2):
            cg = jnp.where(jnp.right_shift(lane, 6) == g, cm, 0.0).astype(BF16)
            cbs.append(_dot(cg, bb, NT))
        for j in range(SSD_HEADS // 2):
            sl = slice(j * LANES, (j + 1) * LANES)
            g = j // 4
            xcj = xc[:, sl].astype(BF16)
            halves = []
            for half in range(2):
                h = 2 * j + half
                seg = cum[:, h:h + 1] - cum_t[h:h + 1, :]
                w = cbs[g] * jnp.exp(jnp.where(causal, seg, -jnp.inf))
                halves.append(_dot(w.astype(BF16), xcj))
            y_diag = jnp.where(lane < SSD_HEAD_DIM, halves[0], halves[1])
            hj = state[:, sl]
            y_off = doutx[:, sl] * _dot(cb, hj.astype(BF16))
            y_ref[:, sl] = y_diag + y_off + xs[:, sl] * dx_row[:, sl]
            st = _dot(bb, xd[:, sl].astype(BF16), TN)
            state[:, sl] = hj * edec[:, sl] + jnp.where(jnp.right_shift(row, 6) == g, st, 0.0)

    return _pc(body, name=name, out_shape=(_sds((n, SSD_INNER)), _sds((nc, LANES, SSD_INNER))), grid=(nc,),
               in_specs=[_row(q, SSD_XBC), _row(q, LANES), _const((1, LANES)), _const((1, LANES))],
               out_specs=(_row(q, SSD_INNER), pl.BlockSpec((1, LANES, SSD_INNER), lambda i: (i, 0, 0))),
               scratch_shapes=[pltpu.VMEM((LANES, SSD_INNER), F32)], compiler_params=_params(1))(pre, dt, alog, dvec)


def _ssd_scan_bwd(pre, dt, hprev, dy, alog, dvec, name):
    n = pre.shape[0]
    q = SSD_CHUNK
    nc = n // q

    def body(pre_ref, dt_ref, hp_ref, dy_ref, alog_ref, d_ref, dpre_ref, ddt_ref, da_ref, dd_ref,
             d_state, dxc_s, dcx_s, dcl_s, acc_a, acc_d):
        i = pl.program_id(0)

        @pl.when(i == 0)
        def _():
            d_state[...] = jnp.zeros_like(d_state)
            acc_a[...] = jnp.zeros_like(acc_a)
            acc_d[...] = jnp.zeros_like(acc_d)

        pre_ = pre_ref[...]
        dt_ = dt_ref[...]
        sg, act, a_neg, causal, cum, e_mat, dtx, cumx = _ssd_common(pre_, dt_, alog_ref[...])
        r_mat = _reduce_mat()
        xs = act[:, :SSD_INNER]
        bm = act[:, SSD_INNER:SSD_INNER + LANES]
        cm = act[:, SSD_INNER + LANES:]
        cum_t = cum.T
        clx = cumx[q - 1:q, :]
        xc = xs * dtx
        dsx = jnp.exp(clx - cumx)
        doutx = jnp.exp(cumx)
        edec = jnp.exp(clx)
        dx_row = _dot_sel(jnp.broadcast_to(d_ref[...], (SUBLANES, LANES)), e_mat)[0:1, :]
        dy_ = dy_ref[...]
        acc_d[...] += _fold(dy_ * xs)
        bb = bm.astype(BF16)
        cb = cm.astype(BF16)
        lane = lax.broadcasted_iota(jnp.int32, (1, LANES), 1)
        row = lax.broadcasted_iota(jnp.int32, (LANES, 1), 0)
        d_c = jnp.zeros((q, LANES), F32)
        d_b = jnp.zeros((q, LANES), F32)

        for j in range(SSD_HEADS // 2):
            sl = slice(j * LANES, (j + 1) * LANES)
            g = j // 4
            hj = hp_ref[0, :, sl]
            hjb = hj.astype(BF16)
            dyj = dy_[:, sl]
            tj = _dot(cb, hjb)
            dtj = (doutx[:, sl] * dyj).astype(BF16)
            dcx = dyj * tj * doutx[:, sl]
            d_c = d_c + _dot(dtj, hjb, NT)
            dhn = d_state[:, sl]
            dhp = dhn * edec[:, sl] + jnp.where(jnp.right_shift(row, 6) == g, _dot(cb, dtj, TN), 0.0)
            dcl = jnp.sum(dhn * hj, axis=0, keepdims=True) * edec[:, sl]
            dsb = dhn.astype(BF16)
            dxd = _dot(bb, dsb)
            xcj = xc[:, sl]
            dsj = dsx[:, sl]
            d_b = d_b + _dot((xcj * dsj).astype(BF16), dsb, NT)
            dds = dxd * xcj * dsj
            d_state[:, sl] = dhp
            dxc_s[:, sl] = dxd * dsj
            dcx_s[:, sl] = dcx - dds
            dcl_s[:, sl] = jnp.broadcast_to(dcl + jnp.sum(dds, axis=0, keepdims=True), (SUBLANES, LANES))

        dcum_c = jnp.zeros((q, LANES), F32)
        dcum_t = jnp.zeros((LANES, q), F32)
        for g in range(2):
            gmask = jnp.right_shift(lane, 6) == g
            cg = jnp.where(gmask, cm, 0.0).astype(BF16)
            cbg = _dot(cg, bb, NT)
            d_cb = jnp.zeros((q, q), F32)
            for hh in range(SSD_HEADS // 2):
                h = g * (SSD_HEADS // 2) + hh
                j, half = h // 2, h % 2
                sl = slice(j * LANES, (j + 1) * LANES)
                hmask = jnp.right_shift(lane, 6) == half
                seg = cum[:, h:h + 1] - cum_t[h:h + 1, :]
                lm = jnp.exp(jnp.where(causal, seg, -jnp.inf))
                w = cbg * lm
                dyj = dy_[:, sl]
                dw = _dot(jnp.where(hmask, dyj, 0.0).astype(BF16), xc[:, sl].astype(BF16), NT)
                dxch = _dot(w.astype(BF16), dyj.astype(BF16), TN)
                dxc_s[:, sl] += jnp.where(hmask, dxch, 0.0)
                d_cb = d_cb + dw * lm
                m = dw * w
                dcum_c = dcum_c + jnp.sum(m, axis=1, keepdims=True) * (lane == h).astype(F32)
                dcum_t = dcum_t + (row == h).astype(F32) * jnp.sum(m, axis=0, keepdims=True)
            d_cbb = d_cb.astype(BF16)
            d_c = d_c + jnp.where(gmask, _dot(d_cbb, bb), 0.0)
            d_b = d_b + jnp.where(gmask, _dot(d_cbb, cb, TN), 0.0)

        dcl_row = _dot_sel(dcl_s[...], r_mat)[0:1, :]
        rowq = lax.broadcasted_iota(jnp.int32, (q, 1), 0)
        dcum = (dcum_c - dcum_t.T + _dot_sel(dcx_s[...], r_mat)
                + jnp.where(rowq == q - 1, dcl_row, 0.0))
        rr = lax.broadcasted_iota(jnp.int32, (q, q), 0)
        cc = lax.broadcasted_iota(jnp.int32, (q, q), 1)
        dadt = _sel_dot((rr <= cc).astype(BF16), dcum)
        dxc = dxc_s[...]
        ddt_ref[...] = dadt * a_neg + _dot_sel(dxc * xs, r_mat)
        acc_a[...] += _fold(dadt * dt_)
        dsilu = sg * (1.0 + pre_ * (1.0 - sg))
        dpre_ref[:, :SSD_INNER] = (dxc * dtx + dy_ * dx_row) * dsilu[:, :SSD_INNER]
        dpre_ref[:, SSD_INNER:SSD_INNER + LANES] = d_b * dsilu[:, SSD_INNER:SSD_INNER + LANES]
        dpre_ref[:, SSD_INNER + LANES:] = d_c * dsilu[:, SSD_INNER + LANES:]

        @pl.when(i == nc - 1)
        def _():
            da_ref[...] = jnp.sum(acc_a[...], axis=0, keepdims=True) * a_neg
            dd_ref[...] = jnp.sum(_dot_sel(acc_d[...], r_mat), axis=0, keepdims=True)

    rev = lambda i: (nc - 1 - i, 0)
    return _pc(body, name=name,
               out_shape=(_sds((n, SSD_XBC)), _sds((n, LANES)), _sds((1, LANES)), _sds((1, LANES))), grid=(nc,),
               in_specs=[pl.BlockSpec((q, SSD_XBC), rev), pl.BlockSpec((q, LANES), rev),
                         pl.BlockSpec((1, LANES, SSD_INNER), lambda i: (nc - 1 - i, 0, 0)),
                         pl.BlockSpec((q, SSD_INNER), rev), _const((1, LANES)), _const((1, LANES))],
               out_specs=(pl.BlockSpec((q, SSD_XBC), rev), pl.BlockSpec((q, LANES), rev), _const((1, LANES)),
                          _const((1, LANES))),
               scratch_shapes=[pltpu.VMEM((LANES, SSD_INNER), F32), pltpu.VMEM((q, SSD_INNER), F32),
                               pltpu.VMEM((q, SSD_INNER), F32), pltpu.VMEM((SUBLANES, SSD_INNER), F32),
                               pltpu.VMEM((SUBLANES, LANES), F32), pltpu.VMEM((SUBLANES, SSD_INNER), F32)],
               compiler_params=_params(1))(pre, dt, hprev, dy, alog, dvec)


def _group_norm_parts(v):
    half = SSD_INNER // 2
    r0 = lax.rsqrt(jnp.mean(v[:, :half] * v[:, :half], axis=-1, keepdims=True) + EPS)
    r1 = lax.rsqrt(jnp.mean(v[:, half:] * v[:, half:], axis=-1, keepdims=True) + EPS)
    lane = lax.broadcasted_iota(jnp.int32, (1, SSD_INNER), 1)
    return jnp.where(lane < half, r0, r1)


def _group_mean(v):
    half = SSD_INNER // 2
    m0 = jnp.mean(v[:, :half], axis=-1, keepdims=True)
    m1 = jnp.mean(v[:, half:], axis=-1, keepdims=True)
    lane = lax.broadcasted_iota(jnp.int32, (1, SSD_INNER), 1)
    return jnp.where(lane < half, m0, m1)


def _ssd_post(y, z, g, name):
    n, d = y.shape
    tl = _tile(n)

    def body(y_ref, z_ref, g_ref, o_ref):
        z_ = z_ref[...]
        v = y_ref[...] * (z_ * _sig(z_))
        o_ref[...] = ((v * _group_norm_parts(v)) * g_ref[...]).astype(BF16)

    return _pc(body, name=name, out_shape=_sds((n, d), BF16), grid=(n // tl,),
               in_specs=[_row(tl, d), _row(tl, d), _const((1, d))], out_specs=_row(tl, d),
               compiler_params=_params(1))(y, z, g)


def _ssd_post_bwd(dout, y, z, g, name):
    n, d = y.shape
    tl = _tile(n)
    nb = n // tl

    def body(do_ref, y_ref, z_ref, g_ref, dy_ref, dz_ref, dg_ref, acc_g):
        i = pl.program_id(0)

        @pl.when(i == 0)
        def _():
            acc_g[...] = jnp.zeros_like(acc_g)

        z_ = z_ref[...]
        y_ = y_ref[...]
        sz = _sig(z_)
        silu_z = z_ * sz
        v = y_ * silu_z
        rs = _group_norm_parts(v)
        nv = v * rs
        do_ = do_ref[...]
        acc_g[...] += _fold(do_ * nv)
        dn = do_ * g_ref[...]
        dv = rs * (dn - nv * _group_mean(dn * nv))
        dy_ref[...] = dv * silu_z
        dz_ref[...] = (dv * y_ * (sz * (1.0 + z_ * (1.0 - sz)))).astype(BF16)

        @pl.when(i == nb - 1)
        def _():
            dg_ref[...] = jnp.sum(acc_g[...], axis=0, keepdims=True)

    return _pc(body, name=name, out_shape=(_sds((n, d)), _sds((n, d), BF16), _sds((1, d))), grid=(nb,),
               in_specs=[_row(tl, d), _row(tl, d), _row(tl, d), _const((1, d))],
               out_specs=(_row(tl, d), _row(tl, d), _const((1, d))),
               scratch_shapes=[pltpu.VMEM((SUBLANES, d), F32)], compiler_params=_params(1))(dout, y, z, g)


def _layer_norm_parts(uc):
    mu = jnp.mean(uc, axis=-1, keepdims=True)
    xc = uc - mu
    rstd = lax.rsqrt(jnp.mean(xc * xc, axis=-1, keepdims=True) + EPS)
    return xc * rstd, rstd


def _conf_fwd(conf_in, conv_w, conv_b, ln_g, ln_b, name):
    n = conf_in.shape[0]
    c = CONF_WIDTH
    tl = _tile(n)
    hb = CONF_HALO
    k_taps = CONF_KERNEL

    def body(x_ref, xp_ref, w_ref, b_ref, g_ref, beta_ref, o_ref, uc_ref, buf):
        i = pl.program_id(0)
        xp = xp_ref[...]
        buf[pl.ds(0, hb), :] = jnp.where(i > 0, xp[:, :c] * _sig(xp[:, c:]), 0.0)
        x_ = x_ref[...]
        buf[pl.ds(hb, tl), :] = x_[:, :c] * _sig(x_[:, c:])
        acc = b_ref[...] + w_ref[0:1, :] * buf[pl.ds(hb - (k_taps - 1), tl), :]
        for k in range(1, k_taps):
            acc = acc + w_ref[k:k + 1, :] * buf[pl.ds(hb - (k_taps - 1) + k, tl), :]
        uc_ref[...] = acc
        nv, _ = _layer_norm_parts(acc)
        v = nv * g_ref[...] + beta_ref[...]
        o_ref[...] = (v * _sig(v)).astype(BF16)

    return _pc(body, name=name, out_shape=(_sds((n, c), BF16), _sds((n, c))), grid=(n // tl,),
               in_specs=[_row(tl, 2 * c), _prev(tl, hb, 2 * c), _const((hb, c)), _const((1, c)), _const((1, c)),
                         _const((1, c))],
               out_specs=(_row(tl, c), _row(tl, c)),
               scratch_shapes=[pltpu.VMEM((tl + hb, c), F32)], compiler_params=_params(1))(
        conf_in, conf_in, conv_w, conv_b, ln_g, ln_b)


def _conf_bwd(dout, uc, conf_in, conv_w, ln_g, ln_b, name):
    n = conf_in.shape[0]
    c = CONF_WIDTH
    tl = _tile(n)
    nb = n // tl
    hb = CONF_HALO
    k_taps = CONF_KERNEL

    def body(do_ref, don_ref, uc_ref, ucn_ref, x_ref, xp_ref, w_ref, g_ref, beta_ref,
             dx_ref, dw_ref, db_ref, dg_ref, dbeta_ref, dbuf, ubuf, acc_w, acc_b, acc_g, acc_beta):
        i = pl.program_id(0)

        @pl.when(i == 0)
        def _():
            acc_w[...] = jnp.zeros_like(acc_w)
            acc_b[...] = jnp.zeros_like(acc_b)
            acc_g[...] = jnp.zeros_like(acc_g)
            acc_beta[...] = jnp.zeros_like(acc_beta)

        g_ = g_ref[...]
        beta_ = beta_ref[...]

        def d_conv_out(do_, uc_):
            nv, rstd = _layer_norm_parts(uc_)
            v = nv * g_ + beta_
            sv = _sig(v)
            dv = do_ * (sv * (1.0 + v * (1.0 - sv)))
            dn = dv * g_
            duc = rstd * (dn - jnp.mean(dn, axis=-1, keepdims=True)
                          - nv * jnp.mean(dn * nv, axis=-1, keepdims=True))
            return duc, dv, nv

        duc, dv, nv = d_conv_out(do_ref[...], uc_ref[...])
        acc_g[...] += _fold(dv * nv)
        acc_beta[...] += _fold(dv)
        acc_b[...] += _fold(duc)
        dbuf[pl.ds(0, tl), :] = duc
        ducn, _, _ = d_conv_out(don_ref[...], ucn_ref[...])
        dbuf[pl.ds(tl, hb), :] = jnp.where(i < nb - 1, ducn, 0.0)
        xp = xp_ref[...]
        ubuf[pl.ds(0, hb), :] = jnp.where(i > 0, xp[:, :c] * _sig(xp[:, c:]), 0.0)
        x_ = x_ref[...]
        val = x_[:, :c]
        sgate = _sig(x_[:, c:])
        ubuf[pl.ds(hb, tl), :] = val * sgate
        du = w_ref[0:1, :] * dbuf[pl.ds(k_taps - 1, tl), :]
        for k in range(1, k_taps):
            du = du + w_ref[k:k + 1, :] * dbuf[pl.ds(k_taps - 1 - k, tl), :]
        for k in range(k_taps):
            acc_w[k] += _fold(duc * ubuf[pl.ds(hb - (k_taps - 1) + k, tl), :])
        dx_ref[:, :c] = (du * sgate).astype(BF16)
        dx_ref[:, c:] = (du * val * sgate * (1.0 - sgate)).astype(BF16)

        @pl.when(i == nb - 1)
        def _():
            dw_ref[...] = jnp.zeros_like(dw_ref)
            for k in range(k_taps):
                dw_ref[k:k + 1, :] = jnp.sum(acc_w[k], axis=0, keepdims=True)
            db_ref[...] = jnp.sum(acc_b[...], axis=0, keepdims=True)
            dg_ref[...] = jnp.sum(acc_g[...], axis=0, keepdims=True)
            dbeta_ref[...] = jnp.sum(acc_beta[...], axis=0, keepdims=True)

    vec = _sds((1, c))
    return _pc(body, name=name, out_shape=(_sds((n, 2 * c), BF16), _sds((hb, c)), vec, vec, vec), grid=(nb,),
               in_specs=[_row(tl, c), _next(tl, hb, c, n), _row(tl, c), _next(tl, hb, c, n), _row(tl, 2 * c),
                         _prev(tl, hb, 2 * c), _const((hb, c)), _const((1, c)), _const((1, c))],
               out_specs=(_row(tl, 2 * c), _const((hb, c)), _const((1, c)), _const((1, c)), _const((1, c))),
               scratch_shapes=[pltpu.VMEM((tl + hb, c), F32), pltpu.VMEM((tl + hb, c), F32),
                               pltpu.VMEM((k_taps, SUBLANES, c), F32), pltpu.VMEM((SUBLANES, c), F32),
                               pltpu.VMEM((SUBLANES, c), F32), pltpu.VMEM((SUBLANES, c), F32)],
               compiler_params=_params(1))(dout, dout, uc, uc, conf_in, conf_in, conv_w, ln_g, ln_b)


def _sc_fwd(sc_in, conv_w, name):
    n = sc_in.shape[0]
    c = SC_WIDTH
    tl = _tile(n)
    hb = CONV_HALO

    def body(x_ref, xp_ref, w_ref, o_ref, buf):
        i = pl.program_id(0)
        xp = xp_ref[...]
        buf[pl.ds(0, hb), :] = jnp.where(i > 0, xp[:, c:2 * c] * xp[:, 2 * c:], 0.0)
        x_ = x_ref[...]
        buf[pl.ds(hb, tl), :] = x_[:, c:2 * c] * x_[:, 2 * c:]
        cv = w_ref[0:1, :] * buf[pl.ds(hb - 2, tl), :]
        for k in range(1, 3):
            cv = cv + w_ref[k:k + 1, :] * buf[pl.ds(hb - 2 + k, tl), :]
        o_ref[...] = (x_[:, :c] * cv).astype(BF16)

    return _pc(body, name=name, out_shape=_sds((n, c), BF16), grid=(n // tl,),
               in_specs=[_row(tl, 3 * c), _prev(tl, hb, 3 * c), _const((SUBLANES, c))], out_specs=_row(tl, c),
               scratch_shapes=[pltpu.VMEM((tl + hb, c), F32)], compiler_params=_params(1))(sc_in, sc_in, conv_w)


def _sc_bwd(dout, sc_in, conv_w, name):
    n = sc_in.shape[0]
    c = SC_WIDTH
    tl = _tile(n)
    nb = n // tl
    hb = CONV_HALO

    def body(do_ref, don_ref, x_ref, xp_ref, xn_ref, w_ref, dx_ref, dw_ref, dbuf, pbuf, acc_w):
        i = pl.program_id(0)

        @pl.when(i == 0)
        def _():
            acc_w[...] = jnp.zeros_like(acc_w)

        x_ = x_ref[...]
        gb, gc, xv = x_[:, :c], x_[:, c:2 * c], x_[:, 2 * c:]
        do_ = do_ref[...]
        dcv = do_ * gb
        dbuf[pl.ds(0, tl), :] = dcv
        dbuf[pl.ds(tl, hb), :] = jnp.where(i < nb - 1, don_ref[...] * xn_ref[...], 0.0)
        xp = xp_ref[...]
        pbuf[pl.ds(0, hb), :] = jnp.where(i > 0, xp[:, c:2 * c] * xp[:, 2 * c:], 0.0)
        pbuf[pl.ds(hb, tl), :] = gc * xv
        cv = w_ref[0:1, :] * pbuf[pl.ds(hb - 2, tl), :]
        dp = w_ref[0:1, :] * dbuf[pl.ds(2, tl), :]
        for k in range(1, 3):
            cv = cv + w_ref[k:k + 1, :] * pbuf[pl.ds(hb - 2 + k, tl), :]
            dp = dp + w_ref[k:k + 1, :] * dbuf[pl.ds(2 - k, tl), :]
        for k in range(3):
            acc_w[k] += _fold(dcv * pbuf[pl.ds(hb - 2 + k, tl), :])
        dx_ref[:, :c] = (do_ * cv).astype(BF16)
        dx_ref[:, c:2 * c] = (dp * xv).astype(BF16)
        dx_ref[:, 2 * c:] = (dp * gc).astype(BF16)

        @pl.when(i == nb - 1)
        def _():
            dw_ref[...] = jnp.zeros_like(dw_ref)
            for k in range(3):
                dw_ref[k:k + 1, :] = jnp.sum(acc_w[k], axis=0, keepdims=True)

    return _pc(body, name=name, out_shape=(_sds((n, 3 * c), BF16), _sds((SUBLANES, c))), grid=(nb,),
               in_specs=[_row(tl, c), _next(tl, hb, c, n), _row(tl, 3 * c), _prev(tl, hb, 3 * c),
                         _next(tl, hb, c, n), _const((SUBLANES, c))],
               out_specs=(_row(tl, 3 * c), _const((SUBLANES, c))),
               scratch_shapes=[pltpu.VMEM((tl + hb, c), F32), pltpu.VMEM((tl + hb, c), F32),
                               pltpu.VMEM((3, SUBLANES, c), F32)],
               compiler_params=_params(1))(dout, dout, sc_in, sc_in, sc_in, conv_w)


def _merge_fwd(gates, ya, yb, yc, b_gate, name):
    n, d = ya.shape
    tl = _tile(n)

    def body(gt_ref, ya_ref, yb_ref, yc_ref, b_ref, o_ref):
        gt = _sig(gt_ref[...] + b_ref[...])
        o_ref[...] = (gt[:, :d] * ya_ref[...] + gt[:, d:2 * d] * yb_ref[...] + gt[:, 2 * d:] * yc_ref[...]).astype(BF16)

    return _pc(body, name=name, out_shape=_sds((n, d), BF16), grid=(n // tl,),
               in_specs=[_row(tl, 3 * d), _row(tl, d), _row(tl, d), _row(tl, d), _const((1, 3 * d))],
               out_specs=_row(tl, d), compiler_params=_params(1))(gates, ya, yb, yc, b_gate)


def _merge_bwd(dm, gates, ya, yb, yc, b_gate, name):
    n, d = ya.shape
    tl = _tile(n)
    nb = n // tl

    def body(dm_ref, gt_ref, ya_ref, yb_ref, yc_ref, b_ref, dya_ref, dyb_ref, dyc_ref, dgt_ref, db_ref, acc):
        i = pl.program_id(0)

        @pl.when(i == 0)
        def _():
            acc[...] = jnp.zeros_like(acc)

        dm_ = dm_ref[...]
        gt = _sig(gt_ref[...] + b_ref[...])
        for idx, (y_ref, dy_ref) in enumerate(((ya_ref, dya_ref), (yb_ref, dyb_ref), (yc_ref, dyc_ref))):
            gk = gt[:, idx * d:(idx + 1) * d]
            dy_ref[...] = (dm_ * gk).astype(BF16)
            dpre = dm_ * y_ref[...] * gk * (1.0 - gk)
            dgt_ref[:, idx * d:(idx + 1) * d] = dpre.astype(BF16)
            acc[:, idx * d:(idx + 1) * d] += _fold(dpre)

        @pl.when(i == nb - 1)
        def _():
            db_ref[...] = jnp.sum(acc[...], axis=0, keepdims=True)

    bf = _sds((n, d), BF16)
    return _pc(body, name=name, out_shape=(bf, bf, bf, _sds((n, 3 * d), BF16), _sds((1, 3 * d))), grid=(nb,),
               in_specs=[_row(tl, d), _row(tl, 3 * d), _row(tl, d), _row(tl, d), _row(tl, d), _const((1, 3 * d))],
               out_specs=(_row(tl, d), _row(tl, d), _row(tl, d), _row(tl, 3 * d), _const((1, 3 * d))),
               scratch_shapes=[pltpu.VMEM((SUBLANES, 3 * d), F32)], compiler_params=_params(1))(
        dm, gates, ya, yb, yc, b_gate)


FFN_COLS = 1408
FFN_STRIP = 64
FFN_STRIP_BWD = 32


def _down(prev, cur, s, row):
    return jnp.where(row < s, pltpu.roll(prev, s, 0), pltpu.roll(cur, s, 0))


def _up(cur, nxt, s, row):
    return jnp.where(row < SUBLANES - s, pltpu.roll(cur, SUBLANES - s, 0), pltpu.roll(nxt, SUBLANES - s, 0))


def _ffn_mid(up, conv_w, conv_b, name):
    n = up.shape[0]
    tl = _tile(n)
    hb = CONV_HALO
    tc = FFN_COLS
    ncb = D_FF // tc

    def spec(shape_rows, idx_fn, off):
        return pl.BlockSpec((shape_rows, tc), lambda j, i, off=off: (idx_fn(i), j + off))

    r = tl // hb
    cur = lambda i: i
    prv = lambda i: jnp.maximum(i * r - 1, 0)

    def body(g_ref, gp_ref, v_ref, vp_ref, wg_ref, wv_ref, bg_ref, bv_ref, o_ref):
        i = pl.program_id(1)
        row = lax.broadcasted_iota(jnp.int32, (SUBLANES, LANES), 0)
        full = lambda ref, k, ls: jnp.broadcast_to(ref[k:k + 1, ls], (SUBLANES, LANES))
        for lc in range(tc // LANES):
            ls = slice(lc * LANES, (lc + 1) * LANES)
            wg = [full(wg_ref, k, ls) for k in range(3)]
            wv = [full(wv_ref, k, ls) for k in range(3)]
            bg, bv = full(bg_ref, 0, ls), full(bv_ref, 0, ls)

            def conv(prev, x, w, b):
                return b + w[0] * _down(prev, x, 2, row) + w[1] * _down(prev, x, 1, row) + w[2] * x

            def strip(t, carry):
                gs, vs = [carry[0]], [carry[1]]
                for u in range(FFN_STRIP // SUBLANES):
                    rows = pl.ds(pl.multiple_of(t * FFN_STRIP + u * SUBLANES, SUBLANES), SUBLANES)
                    gs.append(g_ref[rows, ls])
                    vs.append(v_ref[rows, ls])
                outs = []
                for u in range(FFN_STRIP // SUBLANES):
                    ug = conv(gs[u], gs[u + 1], wg, bg)
                    outs.append(ug * _sig(ug) * conv(vs[u], vs[u + 1], wv, bv))
                for p in range(FFN_STRIP // 16):
                    rows = pl.ds(pl.multiple_of(t * FFN_STRIP + p * 16, 16), 16)
                    o_ref[rows, ls] = jnp.concatenate(outs[2 * p:2 * p + 2], axis=0).astype(BF16)
                return gs[-1], vs[-1]

            lax.fori_loop(0, tl // FFN_STRIP, strip,
                          (jnp.where(i > 0, gp_ref[:, ls], 0.0), jnp.where(i > 0, vp_ref[:, ls], 0.0)))

    wspec = lambda off: pl.BlockSpec((SUBLANES, tc), lambda j, i, off=off: (0, j + off))
    bspec = lambda off: pl.BlockSpec((1, tc), lambda j, i, off=off: (0, j + off))
    return _pc(body, name=name, out_shape=_sds((n, D_FF), BF16), grid=(ncb, n // tl),
               in_specs=[spec(tl, cur, 0), spec(hb, prv, 0), spec(tl, cur, ncb), spec(hb, prv, ncb),
                         wspec(0), wspec(ncb), bspec(0), bspec(ncb)],
               out_specs=pl.BlockSpec((tl, tc), lambda j, i: (i, j)), compiler_params=_params(2))(
        up, up, up, up, conv_w, conv_w, conv_b, conv_b)


def _ffn_mid_bwd(da, up, conv_w, conv_b, name):
    n = up.shape[0]
    tl = _tile(n)
    nb = n // tl
    nt = tl // FFN_STRIP_BWD
    hb = CONV_HALO
    tc = FFN_COLS
    ncb = D_FF // tc
    r = tl // hb
    last = n // hb - 1
    cur = lambda i: i
    prv = lambda i: jnp.maximum(i * r - 1, 0)
    nxt = lambda i: jnp.minimum((i + 1) * r, last)

    def spec(shape_rows, idx_fn, off):
        return pl.BlockSpec((shape_rows, tc), lambda j, i, off=off: (idx_fn(i), j + off))

    def body(da_ref, dan_ref, g_ref, gp_ref, gn_ref, v_ref, vp_ref, vn_ref, wg_ref, wv_ref, bg_ref, bv_ref,
             dg_ref, dv_ref, dwg_ref, dwv_ref, dbg_ref, dbv_ref, acc_w, acc_b):
        i = pl.program_id(1)

        @pl.when(i == 0)
        def _():
            acc_w[...] = jnp.zeros_like(acc_w)
            acc_b[...] = jnp.zeros_like(acc_b)

        row = lax.broadcasted_iota(jnp.int32, (SUBLANES, LANES), 0)
        full = lambda ref, k, ls: jnp.broadcast_to(ref[k:k + 1, ls], (SUBLANES, LANES))
        zero = jnp.zeros((SUBLANES, LANES), F32)

        def d_conv_out(da_, ug, uv):
            s = _sig(ug)
            return da_ * uv * (s * (1.0 + ug * (1.0 - s))), da_ * (ug * s)

        for lc in range(tc // LANES):
            ls = slice(lc * LANES, (lc + 1) * LANES)
            wg = [full(wg_ref, k, ls) for k in range(3)]
            wv = [full(wv_ref, k, ls) for k in range(3)]
            bg, bv = full(bg_ref, 0, ls), full(bv_ref, 0, ls)

            def unit(prev_g, g, prev_v, v, da_):
                g1, g2 = _down(prev_g, g, 1, row), _down(prev_g, g, 2, row)
                v1, v2 = _down(prev_v, v, 1, row), _down(prev_v, v, 2, row)
                ug = bg + wg[0] * g2 + wg[1] * g1 + wg[2] * g
                uv = bv + wv[0] * v2 + wv[1] * v1 + wv[2] * v
                dug, duv = d_conv_out(da_, ug, uv)
                return dug, duv, (g2, g1, g), (v2, v1, v)

            def d_in(d, d_next, w):
                return w[2] * d + w[1] * _up(d, d_next, 1, row) + w[0] * _up(d, d_next, 2, row)

            tail = pl.ds(tl - SUBLANES, SUBLANES)
            dgn, dvn, _, _ = unit(g_ref[tail, ls], gn_ref[:, ls], v_ref[tail, ls], vn_ref[:, ls], dan_ref[:, ls])
            dgn = jnp.where(i < nb - 1, dgn, 0.0)
            dvn = jnp.where(i < nb - 1, dvn, 0.0)
            gp0 = jnp.where(i > 0, gp_ref[:, ls], 0.0)
            vp0 = jnp.where(i > 0, vp_ref[:, ls], 0.0)

            def strip(tt, carry):
                dgn, dvn = carry[0], carry[1]
                aw, ab = list(carry[2:8]), list(carry[8:10])
                t = nt - 1 - tt
                nu = FFN_STRIP_BWD // SUBLANES
                rm = pl.multiple_of(jnp.maximum(t * FFN_STRIP_BWD - SUBLANES, 0), SUBLANES)
                gs = [jnp.where(t > 0, g_ref[pl.ds(rm, SUBLANES), ls], gp0)]
                vs = [jnp.where(t > 0, v_ref[pl.ds(rm, SUBLANES), ls], vp0)]
                das = []
                for u in range(nu):
                    rows = pl.ds(pl.multiple_of(t * FFN_STRIP_BWD + u * SUBLANES, SUBLANES), SUBLANES)
                    gs.append(g_ref[rows, ls])
                    vs.append(v_ref[rows, ls])
                    das.append(da_ref[rows, ls])
                dgs, dvs = [None] * nu + [dgn], [None] * nu + [dvn]
                for u in reversed(range(nu)):
                    dgs[u], dvs[u], gsh, vsh = unit(gs[u], gs[u + 1], vs[u], vs[u + 1], das[u])
                    for k in range(3):
                        aw[k] = aw[k] + dgs[u] * gsh[k]
                        aw[3 + k] = aw[3 + k] + dvs[u] * vsh[k]
                    ab[0] = ab[0] + dgs[u]
                    ab[1] = ab[1] + dvs[u]
                for p in range(nu // 2):
                    rows = pl.ds(pl.multiple_of(t * FFN_STRIP_BWD + p * 16, 16), 16)
                    dg_ref[rows, ls] = jnp.concatenate([d_in(dgs[2 * p], dgs[2 * p + 1], wg),
                                                        d_in(dgs[2 * p + 1], dgs[2 * p + 2], wg)], axis=0).astype(BF16)
                    dv_ref[rows, ls] = jnp.concatenate([d_in(dvs[2 * p], dvs[2 * p + 1], wv),
                                                        d_in(dvs[2 * p + 1], dvs[2 * p + 2], wv)], axis=0).astype(BF16)
                return (dgs[0], dvs[0], *aw, *ab)

            res = lax.fori_loop(0, nt, strip, (dgn, dvn) + (zero,) * 8)
            for k in range(3):
                acc_w[0, k, :, ls] += res[2 + k]
                acc_w[1, k, :, ls] += res[5 + k]
            acc_b[0, :, ls] += res[8]
            acc_b[1, :, ls] += res[9]

        @pl.when(i == nb - 1)
        def _():
            for t, (dw_ref, db_ref) in enumerate(((dwg_ref, dbg_ref), (dwv_ref, dbv_ref))):
                dw_ref[...] = jnp.zeros_like(dw_ref)
                for k in range(3):
                    dw_ref[k:k + 1, :] = jnp.sum(acc_w[t, k], axis=0, keepdims=True)
                db_ref[...] = jnp.sum(acc_b[t], axis=0, keepdims=True)

    wspec = lambda off: pl.BlockSpec((SUBLANES, tc), lambda j, i, off=off: (0, j + off))
    bspec = lambda off: pl.BlockSpec((1, tc), lambda j, i, off=off: (0, j + off))
    ospec = lambda off: pl.BlockSpec((tl, tc), lambda j, i, off=off: (i, j + off))
    dg, dv, dwg, dwv, dbg, dbv = _pc(
        body, name=name,
        out_shape=(_sds((n, D_FF), BF16), _sds((n, D_FF), BF16), _sds((SUBLANES, D_FF)), _sds((SUBLANES, D_FF)),
                   _sds((1, D_FF)), _sds((1, D_FF))),
        grid=(ncb, nb),
        in_specs=[spec(tl, cur, 0), spec(hb, nxt, 0),
                  spec(tl, cur, 0), spec(hb, prv, 0), spec(hb, nxt, 0),
                  spec(tl, cur, ncb), spec(hb, prv, ncb), spec(hb, nxt, ncb),
                  wspec(0), wspec(ncb), bspec(0), bspec(ncb)],
        out_specs=(ospec(0), ospec(0), wspec(0), wspec(0), bspec(0), bspec(0)),
        scratch_shapes=[pltpu.VMEM((2, 3, SUBLANES, tc), F32), pltpu.VMEM((2, SUBLANES, tc), F32)],
        compiler_params=_params(2))(da, da, up, up, up, up, up, up, conv_w, conv_w, conv_b, conv_b)
    return dg, dv, jnp.concatenate([dwg, dwv], axis=1), jnp.concatenate([dbg, dbv], axis=1)


def _position():
    return lax.axis_index("x"), lax.axis_index("y"), lax.axis_index("c")


def _all_gather(locals_, name):
    n = len(locals_)

    def body(*refs):
        x_refs, out_refs = refs[:n], refs[n:2 * n]
        send_sems, recv_sems, local_sems = refs[2 * n:]
        x, y, cc = _position()
        me, sibling = (x, y, cc), (x, y, 1 - cc)
        chips = [(1 - x, y), (x, 1 - y), (1 - x, 1 - y)]

        def slot(a, px, py, pc):
            return out_refs[a].at[4 * px + 2 * py + pc]

        def copy(k, a, block, to, own=False):
            return pltpu.make_async_remote_copy(
                src_ref=x_refs[a] if own else slot(a, *block), dst_ref=slot(a, *block),
                send_sem=send_sems.at[k, a], recv_sem=recv_sems.at[k, a], device_id=to, device_id_type=MESH)

        mine = [pltpu.make_async_copy(x_refs[a], slot(a, *me), local_sems.at[a]) for a in range(n)]
        first = [copy(1 + j, a, me, (*chip, cc), own=True) for j, chip in enumerate(chips) for a in range(n)]
        first += [copy(0, a, me, sibling, own=True) for a in range(n)]
        for cp in mine + first:
            cp.start()
        passed = []
        for j, chip in enumerate(chips):
            for a in range(n):
                copy(1 + j, a, (*chip, cc), me).wait_recv()
                cp = copy(4 + j, a, (*chip, cc), sibling)
                cp.start()
                passed.append(cp)
        for a in range(n):
            copy(0, a, sibling, me).wait_recv()
        for j, chip in enumerate(chips):
            for a in range(n):
                copy(4 + j, a, (*chip, 1 - cc), me).wait_recv()
        for cp in first + passed:
            cp.wait_send()
        for cp in mine:
            cp.wait()

    hbm = pl.BlockSpec(memory_space=pl.ANY)
    return _pc(body, name=name, out_shape=[_sds((N_DEV,) + a.shape, a.dtype) for a in locals_],
               in_specs=[hbm] * n, out_specs=[hbm] * n,
               scratch_shapes=[pltpu.SemaphoreType.DMA((7, n)), pltpu.SemaphoreType.DMA((7, n)),
                               pltpu.SemaphoreType.DMA((n,))])(*locals_)


def _peers():
    x, y, cc = _position()
    others = []
    for fx, fy, fc in ((0, 0, 1), (1, 0, 0), (0, 1, 0), (1, 1, 0), (1, 0, 1), (0, 1, 1), (1, 1, 1)):
        p = (1 - x if fx else x, 1 - y if fy else y, 1 - cc if fc else cc)
        others.append((p, 4 * p[0] + 2 * p[1] + p[2]))
    return 4 * x + 2 * y + cc, others


def _gather_start(locals_, after, name):
    n = len(locals_)
    me, _ = _peers()
    lands = [lax.dynamic_update_slice(lax.empty((N_DEV,) + a.shape, a.dtype), a[None], (me,) + (0,) * a.ndim)
             for a in locals_]

    def body(*refs):
        x_refs, land_refs = refs[:n], refs[n:2 * n]
        send_sems, recv_sems, token = refs[2 * n + 1], refs[2 * n + 2], refs[-1]
        me_idx, others = _peers()
        for k, (peer, _) in enumerate(others):
            for a in range(n):
                pltpu.make_async_remote_copy(
                    src_ref=x_refs[a], dst_ref=land_refs[a].at[me_idx], send_sem=send_sems.at[k * n + a],
                    recv_sem=recv_sems.at[k * n + a], device_id=peer, device_id_type=MESH).start()
        token[...] = jnp.zeros_like(token)

    hbm = pl.BlockSpec(memory_space=pltpu.HBM)
    sem = pl.BlockSpec(memory_space=pltpu.SEMAPHORE)
    out = _pc(body, name=name,
              out_shape=(pltpu.SemaphoreType.DMA((7 * n,)), pltpu.SemaphoreType.DMA((7 * n,)),
                         *[pltpu.HBM(a.shape, a.dtype) for a in locals_], *[pltpu.HBM(l.shape, l.dtype) for l in lands],
                         _sds((SUBLANES, LANES))),
              in_specs=[hbm] * (2 * n) + [pl.BlockSpec(memory_space=pl.ANY)],
              out_specs=(sem, sem, *([hbm] * (2 * n)), pl.BlockSpec(memory_space=pltpu.VMEM)),
              input_output_aliases={i: 2 + i for i in range(2 * n)},
              compiler_params=pltpu.CompilerParams(has_side_effects=pltpu.SideEffectType.DATAFLOW_SIDE_EFFECTING))(
        *[pltpu.with_memory_space_constraint(a, pltpu.HBM) for a in locals_],
        *[pltpu.with_memory_space_constraint(l, pltpu.HBM) for l in lands], after)
    return (out[0], out[1], list(out[2:2 + n]), list(out[2 + n:2 + 2 * n])), out[-1]


def _gather_wait(state, after, name):
    send_sems, recv_sems, x_thru, land_thru = state
    n = len(x_thru)

    def body(*refs):
        x_refs, land_refs = refs[:n], refs[n:2 * n]
        send_sems, recv_sems = refs[2 * n], refs[2 * n + 1]
        _, others = _peers()
        for k, (peer, peer_idx) in enumerate(others):
            for a in range(n):
                cp = pltpu.make_async_remote_copy(
                    src_ref=x_refs[a], dst_ref=land_refs[a].at[peer_idx], send_sem=send_sems.at[k * n + a],
                    recv_sem=recv_sems.at[k * n + a], device_id=peer, device_id_type=MESH)
                cp.wait_send()
                cp.wait_recv()

    hbm = pl.BlockSpec(memory_space=pltpu.HBM)
    sem = pl.BlockSpec(memory_space=pltpu.SEMAPHORE)
    out = _pc(body, name=name, out_shape=tuple(pltpu.HBM(a.shape, a.dtype) for a in x_thru + land_thru),
              in_specs=[hbm] * (2 * n) + [sem, sem, pl.BlockSpec(memory_space=pl.ANY)], out_specs=tuple([hbm] * (2 * n)),
              input_output_aliases={i: i for i in range(2 * n)},
              compiler_params=pltpu.CompilerParams(has_side_effects=pltpu.SideEffectType.DATAFLOW_SIDE_EFFECTING))(
        *x_thru, *land_thru, send_sems, recv_sems, after)
    return list(out[n:])


N_CHIPS = 4


def _sibling_swap(parts, name):
    n = len(parts)

    def body(*refs):
        g_refs, got_refs = refs[:n], refs[n:2 * n]
        send_sems, recv_sems = refs[2 * n:]
        x, y, cc = _position()
        swaps = []
        for q in range(N_CHIPS):
            for a in range(n):
                swaps.append(pltpu.make_async_remote_copy(
                    src_ref=g_refs[a].at[2 * q + 1 - cc], dst_ref=got_refs[a].at[q], send_sem=send_sems.at[q, a],
                    recv_sem=recv_sems.at[q, a], device_id=(x, y, 1 - cc), device_id_type=MESH))
        for cp in swaps:
            cp.start()
        for cp in swaps:
            cp.wait_recv()
        for cp in swaps:
            cp.wait_send()

    hbm = pl.BlockSpec(memory_space=pl.ANY)
    return _pc(body, name=name, out_shape=[_sds((N_CHIPS,) + a.shape[1:], a.dtype) for a in parts],
               in_specs=[hbm] * n, out_specs=[hbm] * n,
               scratch_shapes=[pltpu.SemaphoreType.DMA((N_CHIPS, n)), pltpu.SemaphoreType.DMA((N_CHIPS, n))])(*parts)


def _pair_add(part, got, core, name):
    q, a, b = got.shape
    ta = _block_rows(a, b)

    def body(core_ref, k_ref, g_ref, o_ref):
        o_ref[...] = (k_ref[...].astype(F32) + g_ref[...].astype(F32)).astype(BF16)

    spec = pl.BlockSpec((None, ta, b), lambda c, i, core_ref: (c, i, 0))
    own = pl.BlockSpec((None, None, ta, b), lambda c, i, core_ref: (c, core_ref[0], i, 0))
    grid_spec = pltpu.PrefetchScalarGridSpec(num_scalar_prefetch=1, grid=(q, a // ta), in_specs=[own, spec],
                                             out_specs=spec)
    return _pc(body, name=name, out_shape=_sds(got.shape, BF16), grid_spec=grid_spec, compiler_params=_params(2))(
        core, part.reshape((N_CHIPS, 2) + part.shape[1:]), got)


def _chip_exchange(sums, name):
    n = len(sums)

    def body(*refs):
        g_refs, out_refs = refs[:n], refs[n:2 * n]
        send_sems, recv_sems = refs[2 * n:]
        me_q, others = _chip_peers()
        sends, recvs = [], []
        for k, (peer, peer_q) in enumerate(others):
            for a in range(n):
                sends.append(pltpu.make_async_remote_copy(
                    src_ref=g_refs[a].at[peer_q], dst_ref=out_refs[a].at[me_q], send_sem=send_sems.at[k, a],
                    recv_sem=recv_sems.at[k, a], device_id=peer, device_id_type=MESH))
                recvs.append(pltpu.make_async_remote_copy(
                    src_ref=g_refs[a].at[me_q], dst_ref=out_refs[a].at[peer_q], send_sem=send_sems.at[k, a],
                    recv_sem=recv_sems.at[k, a], device_id=peer, device_id_type=MESH))
        for cp in sends:
            cp.start()
        for cp in recvs:
            cp.wait_recv()
        for cp in sends:
            cp.wait_send()

    hbm = pl.BlockSpec(memory_space=pl.ANY)
    return _pc(body, name=name, out_shape=[_sds(a.shape, a.dtype) for a in sums],
               in_specs=[hbm] * n, out_specs=[hbm] * n,
               scratch_shapes=[pltpu.SemaphoreType.DMA((3, n)), pltpu.SemaphoreType.DMA((3, n))])(*sums)


def _chip_peers():
    x, y, cc = _position()
    others = []
    for fx, fy in ((1, 0), (0, 1), (1, 1)):
        px, py = (1 - x if fx else x), (1 - y if fy else y)
        others.append(((px, py, cc), 2 * px + py))
    return 2 * x + y, others


def _chip_exchange_start(sums, name):
    n = len(sums)
    lands = [lax.empty(a.shape, a.dtype) for a in sums]

    def body(*refs):
        g_refs, land_refs = refs[:n], refs[n:2 * n]
        send_sems, recv_sems, token = refs[2 * n], refs[2 * n + 1], refs[-1]
        me_q, others = _chip_peers()
        for k, (peer, peer_q) in enumerate(others):
            for a in range(n):
                pltpu.make_async_remote_copy(
                    src_ref=g_refs[a].at[peer_q], dst_ref=land_refs[a].at[me_q], send_sem=send_sems.at[k * n + a],
                    recv_sem=recv_sems.at[k * n + a], device_id=peer, device_id_type=MESH).start()
        token[...] = jnp.zeros_like(token)

    hbm = pl.BlockSpec(memory_space=pltpu.HBM)
    sem = pl.BlockSpec(memory_space=pltpu.SEMAPHORE)
    out = _pc(body, name=name,
              out_shape=(pltpu.SemaphoreType.DMA((3 * n,)), pltpu.SemaphoreType.DMA((3 * n,)),
                         *[pltpu.HBM(a.shape, a.dtype) for a in sums], *[pltpu.HBM(a.shape, a.dtype) for a in sums],
                         _sds((SUBLANES, LANES))),
              in_specs=[hbm] * (2 * n), out_specs=(sem, sem, *([hbm] * (2 * n)), pl.BlockSpec(memory_space=pltpu.VMEM)),
              input_output_aliases={i: 2 + i for i in range(2 * n)},
              compiler_params=pltpu.CompilerParams(has_side_effects=pltpu.SideEffectType.DATAFLOW_SIDE_EFFECTING))(
        *[pltpu.with_memory_space_constraint(a, pltpu.HBM) for a in sums],
        *[pltpu.with_memory_space_constraint(l, pltpu.HBM) for l in lands])
    return (out[0], out[1], list(out[2:2 + n]), list(out[2 + n:2 + 2 * n])), out[-1]


def _chip_exchange_wait(state, after, name):
    send_sems, recv_sems, g_thru, land_thru = state
    n = len(g_thru)

    def body(*refs):
        g_refs, land_refs = refs[:n], refs[n:2 * n]
        send_sems, recv_sems = refs[2 * n], refs[2 * n + 1]
        me_q, others = _chip_peers()
        for k, (peer, peer_q) in enumerate(others):
            for a in range(n):
                cp = pltpu.make_async_remote_copy(
                    src_ref=g_refs[a].at[me_q], dst_ref=land_refs[a].at[peer_q], send_sem=send_sems.at[k * n + a],
                    recv_sem=recv_sems.at[k * n + a], device_id=peer, device_id_type=MESH)
                cp.wait_send()
                cp.wait_recv()

    hbm = pl.BlockSpec(memory_space=pltpu.HBM)
    sem = pl.BlockSpec(memory_space=pltpu.SEMAPHORE)
    out = _pc(body, name=name, out_shape=tuple(pltpu.HBM(a.shape, a.dtype) for a in g_thru + land_thru),
              in_specs=[hbm] * (2 * n) + [sem, sem, pl.BlockSpec(memory_space=pl.ANY)], out_specs=tuple([hbm] * (2 * n)),
              input_output_aliases={i: i for i in range(2 * n)},
              compiler_params=pltpu.CompilerParams(has_side_effects=pltpu.SideEffectType.DATAFLOW_SIDE_EFFECTING))(
        *g_thru, *land_thru, send_sems, recv_sems, after)
    return list(out[:n]), list(out[n:])


def _block_rows(a, b):
    ta = a
    while ta * b > 256 * 1024 and ta % 32 == 0:
        ta //= 2
    return ta


def _reduce_adamw(parts, w, m, v, name):
    n_parts, s, a, b = parts.shape
    ta = _block_rows(a, b)

    def body(p_ref, w_ref, m_ref, v_ref, g_out, d_out, m_out, v_out):
        g = p_ref[0].astype(F32)
        for j in range(1, n_parts):
            g = g + p_ref[j].astype(F32)
        delta, m_new, v_new = _adamw(g, w_ref[...], m_ref[...], v_ref[...])
        g_out[...] = g
        d_out[...] = delta
        m_out[...] = m_new
        v_out[...] = v_new

    spec = pl.BlockSpec((None, ta, b), lambda l, i: (l, i, 0))
    return _pc(body, name=name, out_shape=(_sds((s, a, b)),) * 4, grid=(s, a // ta),
               in_specs=[pl.BlockSpec((n_parts, None, ta, b), lambda l, i: (0, l, i, 0)), spec, spec, spec],
               out_specs=(spec,) * 4, compiler_params=_params(2))(parts, w, m, v)


def _adamw(g, w, m, v):
    c1 = 1.0 - ADAM_B1 ** ADAM_STEP
    c2 = 1.0 - ADAM_B2 ** ADAM_STEP
    m_new = ADAM_B1 * m + (1.0 - ADAM_B1) * g
    v_new = ADAM_B2 * v + (1.0 - ADAM_B2) * (g * g)
    delta = -ADAM_LR * ((m_new / c1) / (jnp.sqrt(v_new / c2) + ADAM_EPS) + ADAM_WD * w)
    return delta, m_new, v_new


def _chip_reduce_adamw(own, recv, w, m, v, chip, name):
    s, a, b = w.shape
    ta = _block_rows(a, b)

    def body(chip_ref, *refs):
        p_refs, (w_ref, m_ref, v_ref), (g_out, d_out, m_out, v_out) = refs[:4 * s], refs[4 * s:4 * s + 3], refs[4 * s + 3:]
        layer = pl.program_id(0)
        g = None
        for l in range(s):
            gl = p_refs[4 * l][...].astype(F32)
            for j in range(1, N_CHIPS):
                gl = gl + p_refs[4 * l + j][...].astype(F32)
            g = gl if g is None else jnp.where(layer == l, gl, g)
        delta, m_new, v_new = _adamw(g, w_ref[...], m_ref[...], v_ref[...])
        g_out[...] = g
        d_out[...] = delta
        m_out[...] = m_new
        v_out[...] = v_new

    def part_spec(l, j):
        return pl.BlockSpec((None, ta, b), lambda layer, i, chip_ref, l=l, j=j: (
            (chip_ref[0] + j) % N_CHIPS, jnp.where(layer == l, i, 0), 0))

    spec = pl.BlockSpec((None, ta, b), lambda layer, i, chip_ref: (layer, i, 0))
    in_specs, args = [], []
    for l in range(s):
        for j in range(N_CHIPS):
            in_specs.append(part_spec(l, j))
            args.append(own[l] if j == 0 else recv[l])
    grid_spec = pltpu.PrefetchScalarGridSpec(num_scalar_prefetch=1, grid=(s, a // ta), in_specs=in_specs + [spec] * 3,
                                             out_specs=(spec,) * 4)
    return _pc(body, name=name, out_shape=(_sds((s, a, b)),) * 4, grid_spec=grid_spec, compiler_params=_params(2))(
        chip, *args, w, m, v)


MATRICES = (("ada_mix_w", 2), ("w_in", 2), ("w_ssd_out", 1), ("w_conf_out", 2), ("w_sc_out", 2), ("w_o", 1),
            ("ada_ffn_w", 2), ("w_up", 2), ("w_down", 1))
MIXER_MATRICES = ("ada_mix_w", "w_in", "w_ssd_out", "w_conf_out", "w_sc_out", "w_o")
FFN_MATRICES = ("ada_ffn_w", "w_up", "w_down")
CONV_WEIGHTS = (("ssd_conv_w", 2), ("conf_conv_w", 2), ("sc_conv_w", 2), ("ffn_conv_w", 2))
SHARDED = MATRICES + CONV_WEIGHTS
REPLICATED = ("ada_mix_b", "norm_mix_g", "b_gate", "ssd_conv_b", "ssd_dt_bias", "ssd_a_log", "ssd_d", "ssd_norm_g",
              "conf_conv_b", "conf_ln_g", "conf_ln_b", "ada_ffn_b", "norm_ffn_g", "ffn_conv_b", "final_norm_g")
WEIGHT_NAMES = ("ada_mix_w", "ada_mix_b", "norm_mix_g", "w_in", "b_gate", "ssd_conv_w", "ssd_conv_b", "ssd_dt_bias",
                "ssd_a_log", "ssd_d", "ssd_norm_g", "w_ssd_out", "conf_conv_w", "conf_conv_b", "conf_ln_g",
                "conf_ln_b", "w_conf_out", "sc_conv_w", "w_sc_out", "w_o", "ada_ffn_w", "ada_ffn_b", "norm_ffn_g",
                "w_up", "ffn_conv_w", "ffn_conv_b", "w_down", "final_norm_g")


def _pack_flat(arrays, cols, row_multiple, dtype):
    flat = jnp.concatenate([a.reshape(-1).astype(dtype) for a in arrays])
    rows = -(-flat.shape[0] // cols)
    rows = -(-rows // row_multiple) * row_multiple
    return jnp.pad(flat, (0, rows * cols - flat.shape[0])).reshape(rows, cols)


def _unpack_flat(flat2d, shapes):
    flat = flat2d.reshape(-1)
    out, off = [], 0
    for s in shapes:
        n = 1
        for d in s:
            n *= d
        out.append(flat[off:off + n].reshape(s))
        off += n
    return out


def _cols(g, lo, hi):
    b = g.shape[-1]
    pieces = []
    for k in range(N_DEV):
        a, e = max(lo, k * b), min(hi, (k + 1) * b)
        if a < e:
            pieces.append(g[k, :, a - k * b:e - k * b])
    return pieces[0] if len(pieces) == 1 else jnp.concatenate(pieces, axis=1)


def _rows(g):
    return g.reshape(N_DEV * g.shape[1], g.shape[2])


def _col_shards(segs, b):
    shards = []
    for k in range(N_DEV):
        lo, hi = k * b, (k + 1) * b
        pieces, off = [], 0
        for seg in segs:
            n = seg.shape[1]
            a, e = max(lo, off), min(hi, off + n)
            if a < e:
                pieces.append(seg[:, a - off:e - off])
            off += n
        shards.append(pieces[0] if len(pieces) == 1 else jnp.concatenate(pieces, axis=1))
    return jnp.stack(shards)


def _row_shards(full):
    return full.reshape(N_DEV, full.shape[0] // N_DEV, full.shape[1])


def _pad_rows(a, rows):
    return jnp.pad(a, ((0, rows - a.shape[0]), (0, 0)))


def _pad_lanes(a):
    return jnp.pad(a, ((0, 0), (0, LANES - a.shape[1])))


def _whole(g):
    return _cols(g, 0, N_DEV * g.shape[-1])


def _mixer_weights(full, i):
    row = lambda name: full[name][i].reshape(1, -1)
    conv = lambda name: _whole(full[name][:, i])
    w_in = full["w_in", i]
    return {
        "ada_mix_w": _whole(full["ada_mix_w", i]), "ada_mix_b": row("ada_mix_b"), "norm_mix_g": row("norm_mix_g"),
        "w_z": _cols(w_in, 0, OFF_Z), "w_xbc": _cols(w_in, OFF_Z, OFF_XBC),
        "w_dt": _pad_lanes(_cols(w_in, OFF_XBC, OFF_DT)), "w_conf": _cols(w_in, OFF_DT, OFF_CONF),
        "w_sc": _cols(w_in, OFF_CONF, OFF_SC), "w_gates": _cols(w_in, OFF_SC, N_IN),
        "b_gate": row("b_gate"),
        "ssd_conv_w": _pad_rows(conv("ssd_conv_w"), SUBLANES), "ssd_conv_b": row("ssd_conv_b"),
        "dt_bias": _pad_lanes(row("ssd_dt_bias")), "a_log": _pad_lanes(row("ssd_a_log")),
        "ssd_d": _pad_lanes(row("ssd_d")), "ssd_norm_g": row("ssd_norm_g"), "w_ssd_out": _rows(full["w_ssd_out", i]),
        "conf_conv_w": _pad_rows(conv("conf_conv_w"), CONF_HALO), "conf_conv_b": row("conf_conv_b"),
        "conf_ln_g": row("conf_ln_g"), "conf_ln_b": row("conf_ln_b"), "w_conf_out": _whole(full["w_conf_out", i]),
        "sc_conv_w": _pad_rows(conv("sc_conv_w"), SUBLANES), "w_sc_out": _whole(full["w_sc_out", i]),
        "w_o": _rows(full["w_o", i]),
    }


def _ffn_weights(full, i):
    row = lambda name: full[name][i].reshape(1, -1)
    return {
        "ada_ffn_w": _whole(full["ada_ffn_w", i]), "ada_ffn_b": row("ada_ffn_b"), "norm_ffn_g": row("norm_ffn_g"),
        "w_up": _whole(full["w_up", i]), "ffn_conv_w": _pad_rows(_whole(full["ffn_conv_w"][:, i]), SUBLANES),
        "ffn_conv_b": row("ffn_conv_b"), "w_down": _rows(full["w_down", i]),
    }


def _adaln(sc8, w, b, name):
    mod = _matmul(sc8, w, "nn", F32, name)[0:1, :] + b
    return mod[:, :D_MODEL], mod[:, D_MODEL:2 * D_MODEL], mod[:, 2 * D_MODEL:]


def _mixer_fwd(i, x, prev, sc8, wl):
    t = f"l{i}_"
    s = {}
    shift, scale, gate = _adaln(sc8, wl["ada_mix_w"], wl["ada_mix_b"], t + "ada_mix")
    if prev is None:
        s["x_in"] = x
        s["h"] = _prenorm_first(x, wl["norm_mix_g"], scale, shift, t + "norm_mix")
    else:
        s["x_in"], s["h"] = _prenorm_res(x, prev[0], prev[1], wl["norm_mix_g"], scale, shift, t + "norm_mix")
    s["scale_mix"], s["gate_mix"] = scale, gate
    h = s["h"]
    s["z"] = _matmul(h, wl["w_z"], "nn", F32, t + "in_z")
    s["xbc"] = _matmul(h, wl["w_xbc"], "nn", F32, t + "in_xbc")
    s["dt_raw"] = _matmul(h, wl["w_dt"], "nn", F32, t + "in_dt")
    s["conf"] = _matmul(h, wl["w_conf"], "nn", F32, t + "in_conf")
    s["sc"] = _matmul(h, wl["w_sc"], "nn", F32, t + "in_sc")
    s["gates"] = _matmul(h, wl["w_gates"], "nn", F32, t + "in_gates")
    s["pre"], s["dt"] = _ssd_pre(s["xbc"], s["dt_raw"], wl["ssd_conv_w"], wl["ssd_conv_b"], wl["dt_bias"],
                                 t + "ssd_pre")
    s["y"], s["hprev"] = _ssd_scan(s["pre"], s["dt"], wl["a_log"], wl["ssd_d"], t + "ssd_scan")
    s["ya_in"] = _ssd_post(s["y"], s["z"], wl["ssd_norm_g"], t + "ssd_post")
    s["yb_in"], s["uc"] = _conf_fwd(s["conf"], wl["conf_conv_w"], wl["conf_conv_b"], wl["conf_ln_g"],
                                    wl["conf_ln_b"], t + "conf")
    s["yc_in"] = _sc_fwd(s["sc"], wl["sc_conv_w"], t + "sconv")
    s["ya"] = _matmul(s["ya_in"], wl["w_ssd_out"], "nn", F32, t + "ssd_out")
    s["yb"] = _matmul(s["yb_in"], wl["w_conf_out"], "nn", F32, t + "conf_out")
    s["yc"] = _matmul(s["yc_in"], wl["w_sc_out"], "nn", F32, t + "sc_out")
    s["merged"] = _merge_fwd(s["gates"], s["ya"], s["yb"], s["yc"], wl["b_gate"], t + "merge")
    s["mix"] = _matmul(s["merged"], wl["w_o"], "nn", F32, t + "w_o")
    return s


def _ffn_fwd(i, s, sc8, wl):
    t = f"l{i}_"
    shift2, scale2, gate2 = _adaln(sc8, wl["ada_ffn_w"], wl["ada_ffn_b"], t + "ada_ffn")
    s["x_mid"], s["h2"] = _prenorm_res(s["x_in"], s["mix"], s["gate_mix"], wl["norm_ffn_g"], scale2, shift2,
                                       t + "norm_ffn")
    s["scale_ffn"], s["gate_ffn"] = scale2, gate2
    s["up"] = _matmul(s["h2"], wl["w_up"], "nn", F32, t + "w_up")
    s["a"] = _ffn_mid(s["up"], wl["ffn_conv_w"], wl["ffn_conv_b"], t + "ffn_mid")
    s["out"] = _matmul(s["a"], wl["w_down"], "nn", F32, t + "w_down")
    return s


def _layer_bwd(i, s, wl, sc8, dys_ffn, dx_after, dgate_ffn, prev, emit=None):
    t = f"l{i}_b_"
    g = {}
    da = _matmul(dys_ffn, wl["w_down"], "nt", F32, t + "d_a")
    g["w_down"] = _matmul(s["a"], dys_ffn, "tn", BF16, t + "dw_down")
    dug, duv, dfw, g["ffn_conv_b"] = _ffn_mid_bwd(da, s["up"], wl["ffn_conv_w"], wl["ffn_conv_b"], t + "ffn_mid")
    g["ffn_conv_w"] = dfw[:3]
    dh2 = _matmul_sum_nt([dug, duv], [wl["w_up"][:, :D_FF], wl["w_up"][:, D_FF:]], t + "d_h2")
    g["w_up"] = [_matmul(s["h2"], dug, "tn", BF16, t + "dw_up_g"), _matmul(s["h2"], duv, "tn", BF16, t + "dw_up_v")]
    dx_mid, dshift2, dscale2, g["norm_ffn_g"], dys_mix, dgate_mix = _norm_bwd(
        dh2, s["x_mid"], dx_after, wl["norm_ffn_g"], s["scale_ffn"], t + "norm_ffn", s["mix"], s["gate_mix"])
    dmod_ffn = jnp.concatenate([dshift2, dscale2, dgate_ffn], axis=1)
    g["ada_ffn_b"] = dmod_ffn
    g["ada_ffn_w"] = [_matmul(sc8, _pad_rows(dmod_ffn, SUBLANES), "tn", BF16, t + "dw_ada_ffn")]
    token = emit(i, "ffn", g) if emit is not None else None
    if token is not None:
        wl = {**wl, "b_gate": wl["b_gate"] + token}
    dmerged = _matmul(dys_mix, wl["w_o"], "nt", F32, t + "d_merged")
    g["w_o"] = _matmul(s["merged"], dys_mix, "tn", BF16, t + "dw_o")
    dya, dyb, dyc, dgates, g["b_gate"] = _merge_bwd(dmerged, s["gates"], s["ya"], s["yb"], s["yc"], wl["b_gate"],
                                                    t + "merge")
    dya_in = _matmul(dya, wl["w_ssd_out"], "nt", F32, t + "d_ya_in")
    g["w_ssd_out"] = _matmul(s["ya_in"], dya, "tn", BF16, t + "dw_ssd_out")
    dyb_in = _matmul(dyb, wl["w_conf_out"], "nt", F32, t + "d_yb_in")
    g["w_conf_out"] = [_matmul(s["yb_in"], dyb, "tn", BF16, t + "dw_conf_out")]
    dyc_in = _matmul(dyc, wl["w_sc_out"], "nt", F32, t + "d_yc_in")
    g["w_sc_out"] = [_matmul(s["yc_in"], dyc, "tn", BF16, t + "dw_sc_out")]
    dy, dz, g["ssd_norm_g"] = _ssd_post_bwd(dya_in, s["y"], s["z"], wl["ssd_norm_g"], t + "ssd_post")
    dpre, ddt, da_log, dd = _ssd_scan_bwd(s["pre"], s["dt"], s["hprev"], dy, wl["a_log"], wl["ssd_d"],
                                          t + "ssd_scan")
    g["ssd_a_log"], g["ssd_d"] = da_log[:, :SSD_HEADS], dd[:, :SSD_HEADS]
    dxbc, ddt_raw, dcw, g["ssd_conv_b"], ddtb = _ssd_pre_bwd(dpre, s["xbc"], ddt, s["dt_raw"], wl["ssd_conv_w"],
                                                             wl["dt_bias"], t + "ssd_pre")
    g["ssd_conv_w"], g["ssd_dt_bias"] = dcw[:4], ddtb[:, :SSD_HEADS]
    dconf, dccw, g["conf_conv_b"], g["conf_ln_g"], g["conf_ln_b"] = _conf_bwd(
        dyb_in, s["uc"], s["conf"], wl["conf_conv_w"], wl["conf_ln_g"], wl["conf_ln_b"], t + "conf")
    g["conf_conv_w"] = dccw[:CONF_KERNEL]
    dsc, dscw = _sc_bwd(dyc_in, s["sc"], wl["sc_conv_w"], t + "sconv")
    g["sc_conv_w"] = dscw[:3]
    segs = (("z", dz, "w_z"), ("xbc", dxbc, "w_xbc"), ("dt", ddt_raw, "w_dt"), ("conf", dconf, "w_conf"),
            ("sc", dsc, "w_sc"), ("gates", dgates, "w_gates"))
    dh = _matmul_sum_nt([dseg for _, dseg, _ in segs], [wl[wname] for _, _, wname in segs], t + "d_h")
    dw_segs = []
    for nm, dseg, wname in segs:
        dw = _matmul(s["h"], dseg, "tn", BF16, t + "dw_in_" + nm)
        dw_segs.append(dw[:, :SSD_HEADS] if nm == "dt" else dw)
    g["w_in"] = dw_segs
    if prev is None:
        dx_in, dshift, dscale, g["norm_mix_g"] = _norm_bwd(dh, s["x_in"], dx_mid, wl["norm_mix_g"], s["scale_mix"],
                                                          t + "norm_mix")
        back = None
    else:
        dx_in, dshift, dscale, g["norm_mix_g"], dys_prev, dgate_prev = _norm_bwd(
            dh, s["x_in"], dx_mid, wl["norm_mix_g"], s["scale_mix"], t + "norm_mix", prev[0], prev[1])
        back = (dys_prev, dgate_prev)
    dmod_mix = jnp.concatenate([dshift, dscale, dgate_mix], axis=1)
    g["ada_mix_b"] = dmod_mix
    g["ada_mix_w"] = [_matmul(sc8, _pad_rows(dmod_mix, SUBLANES), "tn", BF16, t + "dw_ada_mix")]
    return g, dx_in, back


def _device_step(x, c, target, full, fetch=None, emit=None):
    fetch = fetch or {}
    full = dict(full)
    sc8 = _pad_rows(c * (1.0 / (1.0 + jnp.exp(-c))), SUBLANES)
    wls, saved, prev, xcur = [], [], None, x
    for i in range(DEPTH):
        if (i, "mixer") in fetch:
            full.update(fetch[i, "mixer"](prev[0]))
        wl = _mixer_weights(full, i)
        s = _mixer_fwd(i, xcur, prev, sc8, wl)
        if (i, "ffn") in fetch:
            full.update(fetch[i, "ffn"](s["mix"]))
        wf = _ffn_weights(full, i)
        _ffn_fwd(i, s, sc8, wf)
        wls.append({**wl, **wf})
        saved.append(s)
        xcur, prev = s["x_mid"], (s["out"], s["gate_ffn"])
    gf = full["final_norm_g"].reshape(1, -1)
    last = saved[-1]
    loss, dx, dys, dgate, dgf = _final_loss(last["x_mid"], last["out"], last["gate_ffn"], gf, target, "final_loss")
    grads = [None] * DEPTH
    for i in reversed(range(DEPTH)):
        prev = None if i == 0 else (saved[i - 1]["out"], saved[i - 1]["gate_ffn"])
        grads[i], dx, back = _layer_bwd(i, saved[i], wls[i], sc8, dys, dx, dgate, prev, emit)
        token = emit(i, "mixer", grads[i]) if emit is not None else None
        if token is not None:
            wls[i - 1] = {**wls[i - 1], "ffn_conv_b": wls[i - 1]["ffn_conv_b"] + token}
        if back is not None:
            dys, dgate = back
    return loss[0, 0], dx, grads, dgf


def _step(x, c, target, weights, moments_m, moments_v):
    sharded_names = [n for n, _ in SHARDED]
    conv_names = [n for n, _ in CONV_WEIGHTS]
    shard = lambda key: weights[key[0]][key[1]].astype(BF16)
    first = [(n, 0) for n in MIXER_MATRICES]
    later = {(0, "ffn"): [(n, 0) for n in FFN_MATRICES], (1, "mixer"): [(n, 1) for n in MIXER_MATRICES + FFN_MATRICES]}
    gathered = _all_gather([shard(k) for k in first] + [weights[n] for n in conv_names], "gather_first")
    full = {n: weights[n] for n in REPLICATED}
    full.update(zip(first + conv_names, gathered))
    fetch, after = {}, gathered[0]
    for stage, keys in later.items():
        state, after = _gather_start([shard(k) for k in keys], after, f"gather_l{stage[0]}_{stage[1]}_start")
        fetch[stage] = functools.partial(
            lambda act, state, keys, nm: dict(zip(keys, _gather_wait(state, act, nm))),
            state=state, keys=keys, nm=f"gather_l{stage[0]}_{stage[1]}_wait")
    axis_of = dict(SHARDED)
    core = lax.axis_index("c").astype(jnp.int32).reshape(1)
    chip = (2 * lax.axis_index("x") + lax.axis_index("y")).astype(jnp.int32).reshape(1)
    ffn_names = list(FFN_MATRICES) + ["ffn_conv_w"]
    mixer_names = [n for n in sharded_names if n not in ffn_names]
    sums, received, pending = {}, {}, []

    def send(i, names, grads_i, last):
        keys = [(n, i) for n in names]
        parts = []
        for n in names:
            gw = grads_i[n]
            part = _row_shards(gw) if axis_of[n] == 1 else _col_shards(gw if isinstance(gw, list) else [gw],
                                                                       weights[n].shape[-1])
            parts.append(part.astype(BF16))
        tag = f"l{i}_{'ffn' if names is ffn_names else 'mixer'}"
        got = _sibling_swap(parts, "swap_grads_" + tag)
        pair = [_pair_add(p, g, core, f"pair_add_{n}_{i}") for n, p, g in zip(names, parts, got)]
        if last:
            sums.update(zip(keys, pair))
            received.update(zip(keys, _chip_exchange(pair, "exchange_grads_" + tag)))
            return None
        state, token = _chip_exchange_start(pair, "exchange_grads_" + tag + "_start")
        pending.append((keys, state, tag))
        return token[0:1, 0:1]

    def emit(i, kind, grads_i):
        if kind == "ffn" and i == 0:
            return send(0, ffn_names, grads_i, False)
        if kind == "mixer":
            return send(i, mixer_names if i == 0 else sharded_names, grads_i, i == 0)
        return None

    loss, grad_x, grads, dgf = _device_step(x[0], c + after[0:1, 0:1], target[0], full, fetch, emit)
    for keys, state, tag in pending:
        own, got = _chip_exchange_wait(state, grad_x, "exchange_grads_" + tag + "_wait")
        sums.update(zip(keys, own))
        received.update(zip(keys, got))
    big = {n: _chip_reduce_adamw([sums[n, i] for i in range(DEPTH)], [received[n, i] for i in range(DEPTH)],
                                 weights[n], moments_m[n], moments_v[n], chip, "adamw_" + n)
           for n in sharded_names}
    rep_grads = [dgf if n == "final_norm_g" else jnp.stack([grads[i][n].reshape(-1) for i in range(DEPTH)])
                 for n in REPLICATED]
    small_parts, = _all_gather([_pack_flat(rep_grads, LANES, SUBLANES, F32)], "gather_small_grads")
    pack_s = lambda d: _pack_flat([d[n] for n in REPLICATED], LANES, SUBLANES, F32)[None]
    small = _reduce_adamw(small_parts[:, None], pack_s(weights), pack_s(moments_m), pack_s(moments_v),
                          "adamw_replicated")
    small = [_unpack_flat(b, [weights[n].shape for n in REPLICATED]) for b in small]
    results = []
    for kind in range(4):
        by_name = {n: big[n][kind] for n in sharded_names}
        by_name.update(zip(REPLICATED, small[kind]))
        results.append([by_name[n] for n in WEIGHT_NAMES])
    loss = lax.psum(loss, ("x", "y", "c"))
    return (loss, grad_x[None], *results[0], *results[1], *results[2], *results[3])


def kernel(x, c, ada_mix_w, ada_mix_b, norm_mix_g, w_in, b_gate, ssd_conv_w, ssd_conv_b, ssd_dt_bias, ssd_a_log, ssd_d, ssd_norm_g, w_ssd_out, conf_conv_w, conf_conv_b, conf_ln_g, conf_ln_b, w_conf_out, sc_conv_w, w_sc_out, w_o, ada_ffn_w, ada_ffn_b, norm_ffn_g, w_up, ffn_conv_w, ffn_conv_b, w_down, final_norm_g, loss_target, m_ada_mix_w, m_ada_mix_b, m_norm_mix_g, m_w_in, m_b_gate, m_ssd_conv_w, m_ssd_conv_b, m_ssd_dt_bias, m_ssd_a_log, m_ssd_d, m_ssd_norm_g, m_w_ssd_out, m_conf_conv_w, m_conf_conv_b, m_conf_ln_g, m_conf_ln_b, m_w_conf_out, m_sc_conv_w, m_w_sc_out, m_w_o, m_ada_ffn_w, m_ada_ffn_b, m_norm_ffn_g, m_w_up, m_ffn_conv_w, m_ffn_conv_b, m_w_down, m_final_norm_g, v_ada_mix_w, v_ada_mix_b, v_norm_mix_g, v_w_in, v_b_gate, v_ssd_conv_w, v_ssd_conv_b, v_ssd_dt_bias, v_ssd_a_log, v_ssd_d, v_ssd_norm_g, v_w_ssd_out, v_conf_conv_w, v_conf_conv_b, v_conf_ln_g, v_conf_ln_b, v_w_conf_out, v_sc_conv_w, v_w_sc_out, v_w_o, v_ada_ffn_w, v_ada_ffn_b, v_norm_ffn_g, v_w_up, v_ffn_conv_w, v_ffn_conv_b, v_w_down, v_final_norm_g):
    given = dict(locals())
    weights = {n: given[n] for n in WEIGHT_NAMES}
    moments_m = {n: given["m_" + n] for n in WEIGHT_NAMES}
    moments_v = {n: given["v_" + n] for n in WEIGHT_NAMES}
    return _step(x, c, loss_target, weights, moments_m, moments_v)
```

```python
import functools

import jax
import jax.numpy as jnp
from jax import lax
from jax.experimental import pallas as pl
from jax.experimental.pallas import tpu as pltpu

F32 = jnp.float32
BF16 = jnp.bfloat16
MESH = pl.DeviceIdType.MESH

N_DEV = 8
DEPTH = 2
D_MODEL = 1024
SSD_HEADS = 16
SSD_HEAD_DIM = 64
SSD_INNER = 1024
SSD_STATE = 64
SSD_CHUNK = 128
SSD_XBC = 1280
CONF_WIDTH = 512
CONF_KERNEL = 31
SC_WIDTH = 512
D_FF = 2816
EPS = 1e-6
OFF_Z, OFF_XBC, OFF_DT, OFF_CONF, OFF_SC, N_IN = 1024, 2304, 2320, 3344, 4880, 7952

ADAM_LR, ADAM_B1, ADAM_B2, ADAM_EPS, ADAM_WD, ADAM_STEP = 0.001, 0.9, 0.999, 1e-08, 0.01, 10

LANES = 128
SUBLANES = 8
VMEM_LIMIT = 56 * 1024 * 1024
ROW_TILE = 256

NN = (((1,), (0,)), ((), ()))
NT = (((1,), (1,)), ((), ()))
TN = (((0,), (0,)), ((), ()))


def _params(n_axes):
    return pltpu.CompilerParams(dimension_semantics=("arbitrary",) * n_axes, vmem_limit_bytes=VMEM_LIMIT)


def _pc(body, **kw):
    return pl.pallas_call(body, **kw)


def _dot(a, b, dn=NN, precision=None):
    return lax.dot_general(a, b, dn, precision=precision, preferred_element_type=F32)


def _split3(x):
    hi = x.astype(BF16)
    r1 = x - hi.astype(F32)
    mid = r1.astype(BF16)
    return hi, mid, (r1 - mid.astype(F32)).astype(BF16)


def _dot_sel(x, sel):
    hi, mid, lo = _split3(x)
    return _dot(hi, sel) + _dot(mid, sel) + _dot(lo, sel)


def _sel_dot(sel, x):
    hi, mid, lo = _split3(x)
    return _dot(sel, hi) + _dot(sel, mid) + _dot(sel, lo)


def _sig(x):
    return 1.0 / (1.0 + jnp.exp(-x))


def _fold(v):
    r, c = v.shape
    return v.reshape(r // SUBLANES, SUBLANES, c).sum(axis=0)


def _tile(n_rows):
    return min(ROW_TILE, n_rows // 2)


STRIP_UNITS = 8


def _strip_units_per_trip(tl):
    return min(STRIP_UNITS, tl // SUBLANES)


def _bcast_row(ref, k, ls):
    return jnp.broadcast_to(ref[k:k + 1, ls], (SUBLANES, LANES))


def _unit_rows(t, u, nu):
    return pl.ds(pl.multiple_of((t * nu + u) * SUBLANES, SUBLANES), SUBLANES)


def _pair_rows(t, p, nu):
    return pl.ds(pl.multiple_of((t * nu + 2 * p) * SUBLANES, 2 * SUBLANES), 2 * SUBLANES)


def _strip_units(ref, ls, t, nu, nt, halo_prev=None, halo_next=None):
    units = [ref[_unit_rows(t, u, nu), ls] for u in range(nu)]
    if halo_prev is not None:
        before = pl.ds(pl.multiple_of(jnp.maximum(t * nu - 1, 0) * SUBLANES, SUBLANES), SUBLANES)
        units.insert(0, jnp.where(t > 0, ref[before, ls], halo_prev))
    if halo_next is not None:
        after = pl.ds(pl.multiple_of(jnp.minimum((t + 1) * nu, nt * nu - 1) * SUBLANES, SUBLANES), SUBLANES)
        units.append(jnp.where(t < nt - 1, ref[after, ls], halo_next))
    return units


def _row(tl, c, col=0):
    return pl.BlockSpec((tl, c), lambda i, col=col: (i, col))


def _prev(tl, hb, c, col=0):
    r = tl // hb
    return pl.BlockSpec((hb, c), lambda i, col=col: (jnp.maximum(i * r - 1, 0), col))


def _next(tl, hb, c, n_rows, col=0):
    r = tl // hb
    last = n_rows // hb - 1
    return pl.BlockSpec((hb, c), lambda i, col=col: (jnp.minimum((i + 1) * r, last), col))


def _const(shape):
    return pl.BlockSpec(shape, lambda i: (0,) * len(shape))


def _sds(shape, dtype=F32):
    return jax.ShapeDtypeStruct(shape, dtype)


MM_TILE = 1536
MM_FULL_K = 3072
MM_K_TILE = 1024


def _pick(dim, target):
    if dim <= target:
        return dim
    best = None
    for t in range(LANES, target + 1, LANES):
        if dim % t == 0:
            best = t
    assert best is not None, (dim, target)
    return best


def _matmul(a, b, mode, out_dtype, name):
    if mode == "nn":
        (m, k), (k2, n) = a.shape, b.shape
    elif mode == "nt":
        (m, k), (n, k2) = a.shape, b.shape
    else:
        (k, m), (k2, n) = a.shape, b.shape
    assert k == k2, (a.shape, b.shape, mode)
    tm, tn = _pick(m, MM_TILE), _pick(n, MM_TILE)
    tk = k if k <= MM_FULL_K else _pick(k, MM_K_TILE)
    nk = k // tk
    dn = {"nn": NN, "nt": NT, "tn": TN}[mode]

    def body_one(a_ref, b_ref, o_ref):
        o_ref[...] = _dot(a_ref[...].astype(BF16), b_ref[...].astype(BF16), dn).astype(out_dtype)

    def body_acc(a_ref, b_ref, o_ref, acc):
        kk = pl.program_id(2)

        @pl.when(kk == 0)
        def _():
            acc[...] = jnp.zeros_like(acc)

        acc[...] += _dot(a_ref[...].astype(BF16), b_ref[...].astype(BF16), dn)

        @pl.when(kk == nk - 1)
        def _():
            o_ref[...] = acc[...].astype(out_dtype)

    a_spec = {"nn": pl.BlockSpec((tm, tk), lambda i, j, kk: (i, kk)),
              "nt": pl.BlockSpec((tm, tk), lambda i, j, kk: (i, kk)),
              "tn": pl.BlockSpec((tk, tm), lambda i, j, kk: (kk, i))}[mode]
    b_spec = {"nn": pl.BlockSpec((tk, tn), lambda i, j, kk: (kk, j)),
              "nt": pl.BlockSpec((tn, tk), lambda i, j, kk: (j, kk)),
              "tn": pl.BlockSpec((tk, tn), lambda i, j, kk: (kk, j))}[mode]
    o_spec = pl.BlockSpec((tm, tn), lambda i, j, kk: (i, j))
    return _pc(body_one if nk == 1 else body_acc, name=name, out_shape=_sds((m, n), out_dtype),
               grid=(m // tm, n // tn, nk), in_specs=[a_spec, b_spec], out_specs=o_spec,
               scratch_shapes=[] if nk == 1 else [pltpu.VMEM((tm, tn), F32)], compiler_params=_params(3))(a, b)


SUM_NT_TILE = 512


def _matmul_sum_nt(a_list, b_list, name):
    m, n = a_list[0].shape[0], b_list[0].shape[0]
    cnt = len(a_list)
    tm, tn = _pick(m, SUM_NT_TILE), _pick(n, SUM_NT_TILE)

    def body(*refs):
        a_refs, b_refs, o_ref = refs[:cnt], refs[cnt:2 * cnt], refs[2 * cnt]
        acc = _dot(a_refs[0][...].astype(BF16), b_refs[0][...].astype(BF16), NT)
        for t in range(1, cnt):
            acc = acc + _dot(a_refs[t][...].astype(BF16), b_refs[t][...].astype(BF16), NT)
        o_ref[...] = acc

    in_specs = [pl.BlockSpec((tm, a.shape[1]), lambda j, i: (i, 0)) for a in a_list]
    in_specs += [pl.BlockSpec((tn, b.shape[1]), lambda j, i: (j, 0)) for b in b_list]
    return _pc(body, name=name, out_shape=_sds((m, n)), grid=(n // tn, m // tm), in_specs=in_specs,
               out_specs=pl.BlockSpec((tm, tn), lambda j, i: (i, j)), compiler_params=_params(2))(*a_list, *b_list)


def _norm_mod(x, g, scale, shift):
    r = lax.rsqrt(jnp.mean(x * x, axis=-1, keepdims=True) + EPS)
    return ((x * r) * g) * (1.0 + scale) + shift


def _prenorm_first(x, g, scale, shift, name):
    n, d = x.shape
    tl = _tile(n)

    def body(x_ref, g_ref, sc_ref, sh_ref, h_ref):
        h_ref[...] = _norm_mod(x_ref[...], g_ref[...], sc_ref[...], sh_ref[...]).astype(BF16)

    return _pc(body, name=name, out_shape=_sds((n, d), BF16), grid=(n // tl,),
               in_specs=[_row(tl, d)] + [_const((1, d))] * 3, out_specs=_row(tl, d),
               compiler_params=_params(1))(x, g, scale, shift)


def _prenorm_res(x, y, gate, g, scale, shift, name):
    n, d = x.shape
    tl = _tile(n)

    def body(x_ref, y_ref, gate_ref, g_ref, sc_ref, sh_ref, xo_ref, h_ref):
        xn = x_ref[...] + gate_ref[...] * y_ref[...]
        xo_ref[...] = xn
        h_ref[...] = _norm_mod(xn, g_ref[...], sc_ref[...], sh_ref[...]).astype(BF16)

    return _pc(body, name=name, out_shape=(_sds((n, d)), _sds((n, d), BF16)), grid=(n // tl,),
               in_specs=[_row(tl, d), _row(tl, d)] + [_const((1, d))] * 4,
               out_specs=(_row(tl, d), _row(tl, d)), compiler_params=_params(1))(x, y, gate, g, scale, shift)


def _final_loss(x, y, gate, gf, target, name):
    n, d = x.shape
    tl = _tile(n)
    nb = n // tl

    def body(x_ref, y_ref, gate_ref, gf_ref, t_ref, loss_ref, dx_ref, dys_ref, dgate_ref, dgf_ref,
             acc_l, acc_gate, acc_gf):
        i = pl.program_id(0)

        @pl.when(i == 0)
        def _():
            acc_l[...] = jnp.zeros_like(acc_l)
            acc_gate[...] = jnp.zeros_like(acc_gate)
            acc_gf[...] = jnp.zeros_like(acc_gf)

        yv = y_ref[...]
        gate = gate_ref[...]
        gf = gf_ref[...]
        x2 = x_ref[...] + gate * yv
        r = lax.rsqrt(jnp.mean(x2 * x2, axis=-1, keepdims=True) + EPS)
        xn = x2 * r
        e = xn * gf - t_ref[...]
        acc_l[...] += _fold(e * e)
        dy = e * (1.0 / d)
        acc_gf[...] += _fold(dy * xn)
        dxn = dy * gf
        dx = r * (dxn - xn * jnp.mean(dxn * xn, axis=-1, keepdims=True))
        dx_ref[...] = dx
        dys_ref[...] = (dx * gate).astype(BF16)
        acc_gate[...] += _fold(dx * yv)

        @pl.when(i == nb - 1)
        def _():
            loss_ref[...] = jnp.full((SUBLANES, LANES), 0.5 / d, F32) * jnp.sum(acc_l[...])
            dgate_ref[...] = jnp.sum(acc_gate[...], axis=0, keepdims=True)
            dgf_ref[...] = jnp.sum(acc_gf[...], axis=0, keepdims=True)

    return _pc(body, name=name,
               out_shape=(_sds((SUBLANES, LANES)), _sds((n, d)), _sds((n, d), BF16), _sds((1, d)), _sds((1, d))),
               grid=(nb,),
               in_specs=[_row(tl, d), _row(tl, d), _const((1, d)), _const((1, d)), _row(tl, d)],
               out_specs=(_const((SUBLANES, LANES)), _row(tl, d), _row(tl, d), _const((1, d)), _const((1, d))),
               scratch_shapes=[pltpu.VMEM((SUBLANES, d), F32)] * 3,
               compiler_params=_params(1))(x, y, gate, gf, target)


def _norm_bwd(dh, x, dxo, g, scale, name, y_prev=None, gate_prev=None):
    n, d = x.shape
    tl = _tile(n)
    nb = n // tl
    has_prev = y_prev is not None

    def body(*refs):
        if has_prev:
            (dh_ref, x_ref, dxo_ref, g_ref, sc_ref, yp_ref, gp_ref,
             dx_ref, dsh_ref, dsc_ref, dg_ref, dys_ref, dgp_ref, acc_sh, acc_s, acc_gp) = refs
        else:
            (dh_ref, x_ref, dxo_ref, g_ref, sc_ref,
             dx_ref, dsh_ref, dsc_ref, dg_ref, acc_sh, acc_s) = refs
        i = pl.program_id(0)

        @pl.when(i == 0)
        def _():
            acc_sh[...] = jnp.zeros_like(acc_sh)
            acc_s[...] = jnp.zeros_like(acc_s)
            if has_prev:
                acc_gp[...] = jnp.zeros_like(acc_gp)

        x_ = x_ref[...]
        dh_ = dh_ref[...]
        g_ = g_ref[...]
        one_sc = 1.0 + sc_ref[...]
        r = lax.rsqrt(jnp.mean(x_ * x_, axis=-1, keepdims=True) + EPS)
        xn = x_ * r
        dxn = dh_ * (g_ * one_sc)
        dx = dxo_ref[...] + r * (dxn - xn * jnp.mean(dxn * xn, axis=-1, keepdims=True))
        dx_ref[...] = dx
        acc_sh[...] += _fold(dh_)
        acc_s[...] += _fold(dh_ * xn)
        if has_prev:
            dys_ref[...] = (dx * gp_ref[...]).astype(BF16)
            acc_gp[...] += _fold(dx * yp_ref[...])

        @pl.when(i == nb - 1)
        def _():
            s = jnp.sum(acc_s[...], axis=0, keepdims=True)
            dsh_ref[...] = jnp.sum(acc_sh[...], axis=0, keepdims=True)
            dsc_ref[...] = s * g_
            dg_ref[...] = s * one_sc
            if has_prev:
                dgp_ref[...] = jnp.sum(acc_gp[...], axis=0, keepdims=True)

    vec = _sds((1, d))
    in_specs = [_row(tl, d)] * 3 + [_const((1, d))] * 2
    out_shape = [_sds((n, d)), vec, vec, vec]
    out_specs = [_row(tl, d)] + [_const((1, d))] * 3
    scratch = [pltpu.VMEM((SUBLANES, d), F32)] * 2
    args = [dh, x, dxo, g, scale]
    if has_prev:
        in_specs += [_row(tl, d), _const((1, d))]
        out_shape += [_sds((n, d), BF16), vec]
        out_specs += [_row(tl, d), _const((1, d))]
        scratch += [pltpu.VMEM((SUBLANES, d), F32)]
        args += [y_prev, gate_prev]
    return _pc(body, name=name, out_shape=tuple(out_shape), grid=(nb,), in_specs=in_specs,
               out_specs=tuple(out_specs), scratch_shapes=scratch, compiler_params=_params(1))(*args)


CONV_HALO = 8
CONF_HALO = 32


def _ssd_pre(xbc, dt_raw, conv_w, conv_b, dt_bias, name):
    n, c = xbc.shape
    tl = _tile(n)
    hb = CONV_HALO
    k_taps = 4

    nu = _strip_units_per_trip(tl)
    nt = tl // (nu * SUBLANES)

    def body(x_ref, xp_ref, dt_ref, w_ref, b_ref, dtb_ref, pre_ref, dts_ref):
        i = pl.program_id(0)
        row = lax.broadcasted_iota(jnp.int32, (SUBLANES, LANES), 0)
        for lc in range(c // LANES):
            ls = slice(lc * LANES, (lc + 1) * LANES)
            w = [_bcast_row(w_ref, k, ls) for k in range(k_taps)]
            b = _bcast_row(b_ref, 0, ls)
            xp0 = jnp.where(i > 0, xp_ref[:, ls], 0.0)

            def strip(t, carry):
                xs = _strip_units(x_ref, ls, t, nu, nt, halo_prev=xp0)
                for u in range(nu):
                    prev, cur = xs[u], xs[u + 1]
                    pre_ref[_unit_rows(t, u, nu), ls] = (
                        b + w[0] * _down(prev, cur, 3, row) + w[1] * _down(prev, cur, 2, row)
                        + w[2] * _down(prev, cur, 1, row) + w[3] * cur)
                return carry

            lax.fori_loop(0, nt, strip, 0)
        v = dt_ref[...] + dtb_ref[...]
        dts_ref[...] = jnp.maximum(v, 0.0) + jnp.log1p(jnp.exp(-jnp.abs(v)))

    return _pc(body, name=name, out_shape=(_sds((n, c)), _sds((n, LANES))), grid=(n // tl,),
               in_specs=[_row(tl, c), _prev(tl, hb, c), _row(tl, LANES), _const((SUBLANES, c)), _const((1, c)),
                         _const((1, LANES))],
               out_specs=(_row(tl, c), _row(tl, LANES)), compiler_params=_params(1))(
        xbc, xbc, dt_raw, conv_w, conv_b, dt_bias)


def _ssd_pre_bwd(dpre, xbc, ddt, dt_raw, conv_w, dt_bias, name):
    n, c = xbc.shape
    tl = _tile(n)
    nb = n // tl
    hb = CONV_HALO
    k_taps = 4

    nu = _strip_units_per_trip(tl)
    nt = tl // (nu * SUBLANES)

    def body(dp_ref, dpn_ref, x_ref, xp_ref, ddt_ref, dt_ref, w_ref, dtb_ref,
             dx_ref, ddr_ref, dw_ref, db_ref, ddtb_ref, acc_w, acc_b, acc_dtb):
        i = pl.program_id(0)

        @pl.when(i == 0)
        def _():
            acc_w[...] = jnp.zeros_like(acc_w)
            acc_b[...] = jnp.zeros_like(acc_b)
            acc_dtb[...] = jnp.zeros_like(acc_dtb)

        row = lax.broadcasted_iota(jnp.int32, (SUBLANES, LANES), 0)
        zero = jnp.zeros((SUBLANES, LANES), F32)
        for lc in range(c // LANES):
            ls = slice(lc * LANES, (lc + 1) * LANES)
            w = [_bcast_row(w_ref, k, ls) for k in range(k_taps)]
            xp0 = jnp.where(i > 0, xp_ref[:, ls], 0.0)
            dpn0 = jnp.where(i < nb - 1, dpn_ref[:, ls], 0.0)

            def strip(t, carry):
                acc = list(carry)
                dps = _strip_units(dp_ref, ls, t, nu, nt, halo_next=dpn0)
                xs = _strip_units(x_ref, ls, t, nu, nt, halo_prev=xp0)
                dxs = []
                for u in range(nu):
                    d, dn = dps[u], dps[u + 1]
                    dxs.append(w[3] * d + w[2] * _up(d, dn, 1, row) + w[1] * _up(d, dn, 2, row)
                               + w[0] * _up(d, dn, 3, row))
                    prev, cur = xs[u], xs[u + 1]
                    acc[3] = acc[3] + d * cur
                    for k in range(3):
                        acc[k] = acc[k] + d * _down(prev, cur, 3 - k, row)
                    acc[4] = acc[4] + d
                for p in range(nu // 2):
                    dx_ref[_pair_rows(t, p, nu), ls] = jnp.concatenate(dxs[2 * p:2 * p + 2], axis=0).astype(BF16)
                return tuple(acc)

            res = lax.fori_loop(0, nt, strip, (zero,) * 5)
            for k in range(k_taps):
                acc_w[k, :, ls] += res[k]
            acc_b[:, ls] += res[4]
        ddr = ddt_ref[...] * _sig(dt_ref[...] + dtb_ref[...])
        ddr_ref[...] = ddr.astype(BF16)
        acc_dtb[...] += _fold(ddr)

        @pl.when(i == nb - 1)
        def _():
            dw_ref[...] = jnp.zeros_like(dw_ref)
            for k in range(k_taps):
                dw_ref[k:k + 1, :] = jnp.sum(acc_w[k], axis=0, keepdims=True)
            db_ref[...] = jnp.sum(acc_b[...], axis=0, keepdims=True)
            ddtb_ref[...] = jnp.sum(acc_dtb[...], axis=0, keepdims=True)

    return _pc(body, name=name,
               out_shape=(_sds((n, c), BF16), _sds((n, LANES), BF16), _sds((SUBLANES, c)), _sds((1, c)),
                          _sds((1, LANES))),
               grid=(nb,),
               in_specs=[_row(tl, c), _next(tl, hb, c, n), _row(tl, c), _prev(tl, hb, c), _row(tl, LANES),
                         _row(tl, LANES), _const((SUBLANES, c)), _const((1, LANES))],
               out_specs=(_row(tl, c), _row(tl, LANES), _const((SUBLANES, c)), _const((1, c)), _const((1, LANES))),
               scratch_shapes=[pltpu.VMEM((k_taps, SUBLANES, c), F32), pltpu.VMEM((SUBLANES, c), F32),
                               pltpu.VMEM((SUBLANES, LANES), F32)],
               compiler_params=_params(1))(dpre, dpre, xbc, xbc, ddt, dt_raw, conv_w, dt_bias)


def _expand_mat():
    r = lax.broadcasted_iota(jnp.int32, (LANES, SSD_INNER), 0)
    c = lax.broadcasted_iota(jnp.int32, (LANES, SSD_INNER), 1)
    return (jnp.right_shift(c, 6) == r).astype(BF16)


def _reduce_mat():
    r = lax.broadcasted_iota(jnp.int32, (SSD_INNER, LANES), 0)
    c = lax.broadcasted_iota(jnp.int32, (SSD_INNER, LANES), 1)
    return (jnp.right_shift(r, 6) == c).astype(BF16)


def _ssd_common(pre, dt, alog):
    q = SSD_CHUNK
    sg = _sig(pre)
    act = pre * sg
    lane = lax.broadcasted_iota(jnp.int32, (1, LANES), 1)
    a_neg = jnp.where(lane < SSD_HEADS, -jnp.exp(alog), 0.0)
    rr = lax.broadcasted_iota(jnp.int32, (q, q), 0)
    cc = lax.broadcasted_iota(jnp.int32, (q, q), 1)
    causal = rr >= cc
    cum = _sel_dot(causal.astype(BF16), dt * a_neg)
    e_mat = _expand_mat()
    dtx = _dot_sel(dt, e_mat)
    cumx = _dot_sel(cum, e_mat)
    return sg, act, a_neg, causal, cum, e_mat, dtx, cumx


def _ssd_scan(pre, dt, alog, dvec, name):
    n = pre.shape[0]
    q = SSD_CHUNK
    nc = n // q

    def body(pre_ref, dt_ref, alog_ref, d_ref, y_ref, hp_ref, state):
        i = pl.program_id(0)

        @pl.when(i == 0)
        def _():
            state[...] = jnp.zeros_like(state)

        dt_ = dt_ref[...]
        _, act, _, causal, cum, e_mat, dtx, cumx = _ssd_common(pre_ref[...], dt_, alog_ref[...])
        xs = act[:, :SSD_INNER]
        bm = act[:, SSD_INNER:SSD_INNER + LANES]
        cm = act[:, SSD_INNER + LANES:]
        cum_t = cum.T
        clx = cumx[q - 1:q, :]
        xc = xs * dtx
        xd = xc * jnp.exp(clx - cumx)
        doutx = jnp.exp(cumx)
        edec = jnp.exp(clx)
        dx_row = _dot_sel(jnp.broadcast_to(d_ref[...], (SUBLANES, LANES)), e_mat)[0:1, :]
        hp_ref[0] = state[...]
        bb = bm.astype(BF16)
        cb = cm.astype(BF16)
        lane = lax.broadcasted_iota(jnp.int32, (1, LANES), 1)
        row = lax.broadcasted_iota(jnp.int32, (LANES, 1), 0)
        cbs = []
        for g in range(2):
            cg = jnp.where(jnp.right_shift(lane, 6) == g, cm, 0.0).astype(BF16)
            cbs.append(_dot(cg, bb, NT))
        for j in range(SSD_HEADS // 2):
            sl = slice(j * LANES, (j + 1) * LANES)
            g = j // 4
            xcj = xc[:, sl].astype(BF16)
            halves = []
            for half in range(2):
                h = 2 * j + half
                seg = cum[:, h:h + 1] - cum_t[h:h + 1, :]
                w = cbs[g] * jnp.exp(jnp.where(causal, seg, -jnp.inf))
                halves.append(_dot(w.astype(BF16), xcj))
            y_diag = jnp.where(lane < SSD_HEAD_DIM, halves[0], halves[1])
            hj = state[:, sl]
            y_off = doutx[:, sl] * _dot(cb, hj.astype(BF16))
            y_ref[:, sl] = y_diag + y_off + xs[:, sl] * dx_row[:, sl]
            st = _dot(bb, xd[:, sl].astype(BF16), TN)
            state[:, sl] = hj * edec[:, sl] + jnp.where(jnp.right_shift(row, 6) == g, st, 0.0)

    return _pc(body, name=name, out_shape=(_sds((n, SSD_INNER)), _sds((nc, LANES, SSD_INNER))), grid=(nc,),
               in_specs=[_row(q, SSD_XBC), _row(q, LANES), _const((1, LANES)), _const((1, LANES))],
               out_specs=(_row(q, SSD_INNER), pl.BlockSpec((1, LANES, SSD_INNER), lambda i: (i, 0, 0))),
               scratch_shapes=[pltpu.VMEM((LANES, SSD_INNER), F32)], compiler_params=_params(1))(pre, dt, alog, dvec)


def _ssd_scan_bwd(pre, dt, hprev, dy, alog, dvec, name):
    n = pre.shape[0]
    q = SSD_CHUNK
    nc = n // q

    def body(pre_ref, dt_ref, hp_ref, dy_ref, alog_ref, d_ref, dpre_ref, ddt_ref, da_ref, dd_ref,
             d_state, dxc_s, dcx_s, dcl_s, acc_a, acc_d):
        i = pl.program_id(0)

        @pl.when(i == 0)
        def _():
            d_state[...] = jnp.zeros_like(d_state)
            acc_a[...] = jnp.zeros_like(acc_a)
            acc_d[...] = jnp.zeros_like(acc_d)

        pre_ = pre_ref[...]
        dt_ = dt_ref[...]
        sg, act, a_neg, causal, cum, e_mat, dtx, cumx = _ssd_common(pre_, dt_, alog_ref[...])
        r_mat = _reduce_mat()
        xs = act[:, :SSD_INNER]
        bm = act[:, SSD_INNER:SSD_INNER + LANES]
        cm = act[:, SSD_INNER + LANES:]
        cum_t = cum.T
        clx = cumx[q - 1:q, :]
        xc = xs * dtx
        dsx = jnp.exp(clx - cumx)
        doutx = jnp.exp(cumx)
        edec = jnp.exp(clx)
        dx_row = _dot_sel(jnp.broadcast_to(d_ref[...], (SUBLANES, LANES)), e_mat)[0:1, :]
        dy_ = dy_ref[...]
        acc_d[...] += _fold(dy_ * xs)
        bb = bm.astype(BF16)
        cb = cm.astype(BF16)
        lane = lax.broadcasted_iota(jnp.int32, (1, LANES), 1)
        row = lax.broadcasted_iota(jnp.int32, (LANES, 1), 0)
        d_c = jnp.zeros((q, LANES), F32)
        d_b = jnp.zeros((q, LANES), F32)

        for j in range(SSD_HEADS // 2):
            sl = slice(j * LANES, (j + 1) * LANES)
            g = j // 4
            hj = hp_ref[0, :, sl]
            hjb = hj.astype(BF16)
            dyj = dy_[:, sl]
            tj = _dot(cb, hjb)
            dtj = (doutx[:, sl] * dyj).astype(BF16)
            dcx = dyj * tj * doutx[:, sl]
            d_c = d_c + _dot(dtj, hjb, NT)
            dhn = d_state[:, sl]
            dhp = dhn * edec[:, sl] + jnp.where(jnp.right_shift(row, 6) == g, _dot(cb, dtj, TN), 0.0)
            dcl = jnp.sum(dhn * hj, axis=0, keepdims=True) * edec[:, sl]
            dsb = dhn.astype(BF16)
            dxd = _dot(bb, dsb)
            xcj = xc[:, sl]
            dsj = dsx[:, sl]
            d_b = d_b + _dot((xcj * dsj).astype(BF16), dsb, NT)
            dds = dxd * xcj * dsj
            d_state[:, sl] = dhp
            dxc_s[:, sl] = dxd * dsj
            dcx_s[:, sl] = dcx - dds
            dcl_s[:, sl] = jnp.broadcast_to(dcl + jnp.sum(dds, axis=0, keepdims=True), (SUBLANES, LANES))

        dcum_c = jnp.zeros((q, LANES), F32)
        dcum_t = jnp.zeros((LANES, q), F32)
        for g in range(2):
            gmask = jnp.right_shift(lane, 6) == g
            cg = jnp.where(gmask, cm, 0.0).astype(BF16)
            cbg = _dot(cg, bb, NT)
            d_cb = jnp.zeros((q, q), F32)
            for hh in range(SSD_HEADS // 2):
                h = g * (SSD_HEADS // 2) + hh
                j, half = h // 2, h % 2
                sl = slice(j * LANES, (j + 1) * LANES)
                hmask = jnp.right_shift(lane, 6) == half
                seg = cum[:, h:h + 1] - cum_t[h:h + 1, :]
                lm = jnp.exp(jnp.where(causal, seg, -jnp.inf))
                w = cbg * lm
                dyj = dy_[:, sl]
                dw = _dot(jnp.where(hmask, dyj, 0.0).astype(BF16), xc[:, sl].astype(BF16), NT)
                dxch = _dot(w.astype(BF16), dyj.astype(BF16), TN)
                dxc_s[:, sl] += jnp.where(hmask, dxch, 0.0)
                d_cb = d_cb + dw * lm
                m = dw * w
                dcum_c = dcum_c + jnp.sum(m, axis=1, keepdims=True) * (lane == h).astype(F32)
                dcum_t = dcum_t + (row == h).astype(F32) * jnp.sum(m, axis=0, keepdims=True)
            d_cbb = d_cb.astype(BF16)
            d_c = d_c + jnp.where(gmask, _dot(d_cbb, bb), 0.0)
            d_b = d_b + jnp.where(gmask, _dot(d_cbb, cb, TN), 0.0)

        dcl_row = _dot_sel(dcl_s[...], r_mat)[0:1, :]
        rowq = lax.broadcasted_iota(jnp.int32, (q, 1), 0)
        dcum = (dcum_c - dcum_t.T + _dot_sel(dcx_s[...], r_mat)
                + jnp.where(rowq == q - 1, dcl_row, 0.0))
        rr = lax.broadcasted_iota(jnp.int32, (q, q), 0)
        cc = lax.broadcasted_iota(jnp.int32, (q, q), 1)
        dadt = _sel_dot((rr <= cc).astype(BF16), dcum)
        dxc = dxc_s[...]
        ddt_ref[...] = dadt * a_neg + _dot_sel(dxc * xs, r_mat)
        acc_a[...] += _fold(dadt * dt_)
        dsilu = sg * (1.0 + pre_ * (1.0 - sg))
        dpre_ref[:, :SSD_INNER] = (dxc * dtx + dy_ * dx_row) * dsilu[:, :SSD_INNER]
        dpre_ref[:, SSD_INNER:SSD_INNER + LANES] = d_b * dsilu[:, SSD_INNER:SSD_INNER + LANES]
        dpre_ref[:, SSD_INNER + LANES:] = d_c * dsilu[:, SSD_INNER + LANES:]

        @pl.when(i == nc - 1)
        def _():
            da_ref[...] = jnp.sum(acc_a[...], axis=0, keepdims=True) * a_neg
            dd_ref[...] = jnp.sum(_dot_sel(acc_d[...], r_mat), axis=0, keepdims=True)

    rev = lambda i: (nc - 1 - i, 0)
    return _pc(body, name=name,
               out_shape=(_sds((n, SSD_XBC)), _sds((n, LANES)), _sds((1, LANES)), _sds((1, LANES))), grid=(nc,),
               in_specs=[pl.BlockSpec((q, SSD_XBC), rev), pl.BlockSpec((q, LANES), rev),
                         pl.BlockSpec((1, LANES, SSD_INNER), lambda i: (nc - 1 - i, 0, 0)),
                         pl.BlockSpec((q, SSD_INNER), rev), _const((1, LANES)), _const((1, LANES))],
               out_specs=(pl.BlockSpec((q, SSD_XBC), rev), pl.BlockSpec((q, LANES), rev), _const((1, LANES)),
                          _const((1, LANES))),
               scratch_shapes=[pltpu.VMEM((LANES, SSD_INNER), F32), pltpu.VMEM((q, SSD_INNER), F32),
                               pltpu.VMEM((q, SSD_INNER), F32), pltpu.VMEM((SUBLANES, SSD_INNER), F32),
                               pltpu.VMEM((SUBLANES, LANES), F32), pltpu.VMEM((SUBLANES, SSD_INNER), F32)],
               compiler_params=_params(1))(pre, dt, hprev, dy, alog, dvec)


def _group_norm_parts(v):
    half = SSD_INNER // 2
    r0 = lax.rsqrt(jnp.mean(v[:, :half] * v[:, :half], axis=-1, keepdims=True) + EPS)
    r1 = lax.rsqrt(jnp.mean(v[:, half:] * v[:, half:], axis=-1, keepdims=True) + EPS)
    lane = lax.broadcasted_iota(jnp.int32, (1, SSD_INNER), 1)
    return jnp.where(lane < half, r0, r1)


def _group_mean(v):
    half = SSD_INNER // 2
    m0 = jnp.mean(v[:, :half], axis=-1, keepdims=True)
    m1 = jnp.mean(v[:, half:], axis=-1, keepdims=True)
    lane = lax.broadcasted_iota(jnp.int32, (1, SSD_INNER), 1)
    return jnp.where(lane < half, m0, m1)


def _ssd_post(y, z, g, name):
    n, d = y.shape
    tl = _tile(n)

    def body(y_ref, z_ref, g_ref, o_ref):
        z_ = z_ref[...]
        v = y_ref[...] * (z_ * _sig(z_))
        o_ref[...] = ((v * _group_norm_parts(v)) * g_ref[...]).astype(BF16)

    return _pc(body, name=name, out_shape=_sds((n, d), BF16), grid=(n // tl,),
               in_specs=[_row(tl, d), _row(tl, d), _const((1, d))], out_specs=_row(tl, d),
               compiler_params=_params(1))(y, z, g)


def _ssd_post_bwd(dout, y, z, g, name):
    n, d = y.shape
    tl = _tile(n)
    nb = n // tl

    def body(do_ref, y_ref, z_ref, g_ref, dy_ref, dz_ref, dg_ref, acc_g):
        i = pl.program_id(0)

        @pl.when(i == 0)
        def _():
            acc_g[...] = jnp.zeros_like(acc_g)

        z_ = z_ref[...]
        y_ = y_ref[...]
        sz = _sig(z_)
        silu_z = z_ * sz
        v = y_ * silu_z
        rs = _group_norm_parts(v)
        nv = v * rs
        do_ = do_ref[...]
        acc_g[...] += _fold(do_ * nv)
        dn = do_ * g_ref[...]
        dv = rs * (dn - nv * _group_mean(dn * nv))
        dy_ref[...] = dv * silu_z
        dz_ref[...] = (dv * y_ * (sz * (1.0 + z_ * (1.0 - sz)))).astype(BF16)

        @pl.when(i == nb - 1)
        def _():
            dg_ref[...] = jnp.sum(acc_g[...], axis=0, keepdims=True)

    return _pc(body, name=name, out_shape=(_sds((n, d)), _sds((n, d), BF16), _sds((1, d))), grid=(nb,),
               in_specs=[_row(tl, d), _row(tl, d), _row(tl, d), _const((1, d))],
               out_specs=(_row(tl, d), _row(tl, d), _const((1, d))),
               scratch_shapes=[pltpu.VMEM((SUBLANES, d), F32)], compiler_params=_params(1))(dout, y, z, g)


def _layer_norm_parts(uc):
    mu = jnp.mean(uc, axis=-1, keepdims=True)
    xc = uc - mu
    rstd = lax.rsqrt(jnp.mean(xc * xc, axis=-1, keepdims=True) + EPS)
    return xc * rstd, rstd


def _conf_fwd(conf_in, conv_w, conv_b, ln_g, ln_b, name):
    n = conf_in.shape[0]
    c = CONF_WIDTH
    tl = _tile(n)
    hb = CONF_HALO
    k_taps = CONF_KERNEL

    def body(x_ref, xp_ref, w_ref, b_ref, g_ref, beta_ref, o_ref, uc_ref, buf):
        i = pl.program_id(0)
        xp = xp_ref[...]
        buf[pl.ds(0, hb), :] = jnp.where(i > 0, xp[:, :c] * _sig(xp[:, c:]), 0.0)
        x_ = x_ref[...]
        buf[pl.ds(hb, tl), :] = x_[:, :c] * _sig(x_[:, c:])
        acc = b_ref[...] + w_ref[0:1, :] * buf[pl.ds(hb - (k_taps - 1), tl), :]
        for k in range(1, k_taps):
            acc = acc + w_ref[k:k + 1, :] * buf[pl.ds(hb - (k_taps - 1) + k, tl), :]
        uc_ref[...] = acc
        nv, _ = _layer_norm_parts(acc)
        v = nv * g_ref[...] + beta_ref[...]
        o_ref[...] = (v * _sig(v)).astype(BF16)

    return _pc(body, name=name, out_shape=(_sds((n, c), BF16), _sds((n, c))), grid=(n // tl,),
               in_specs=[_row(tl, 2 * c), _prev(tl, hb, 2 * c), _const((hb, c)), _const((1, c)), _const((1, c)),
                         _const((1, c))],
               out_specs=(_row(tl, c), _row(tl, c)),
               scratch_shapes=[pltpu.VMEM((tl + hb, c), F32)], compiler_params=_params(1))(
        conf_in, conf_in, conv_w, conv_b, ln_g, ln_b)


def _conf_bwd(dout, uc, conf_in, conv_w, ln_g, ln_b, name):
    n = conf_in.shape[0]
    c = CONF_WIDTH
    tl = _tile(n)
    nb = n // tl
    hb = CONF_HALO
    k_taps = CONF_KERNEL

    def body(do_ref, don_ref, uc_ref, ucn_ref, x_ref, xp_ref, w_ref, g_ref, beta_ref,
             dx_ref, dw_ref, db_ref, dg_ref, dbeta_ref, dbuf, ubuf, acc_w, acc_b, acc_g, acc_beta):
        i = pl.program_id(0)

        @pl.when(i == 0)
        def _():
            acc_w[...] = jnp.zeros_like(acc_w)
            acc_b[...] = jnp.zeros_like(acc_b)
            acc_g[...] = jnp.zeros_like(acc_g)
            acc_beta[...] = jnp.zeros_like(acc_beta)

        g_ = g_ref[...]
        beta_ = beta_ref[...]

        def d_conv_out(do_, uc_):
            nv, rstd = _layer_norm_parts(uc_)
            v = nv * g_ + beta_
            sv = _sig(v)
            dv = do_ * (sv * (1.0 + v * (1.0 - sv)))
            dn = dv * g_
            duc = rstd * (dn - jnp.mean(dn, axis=-1, keepdims=True)
                          - nv * jnp.mean(dn * nv, axis=-1, keepdims=True))
            return duc, dv, nv

        duc, dv, nv = d_conv_out(do_ref[...], uc_ref[...])
        acc_g[...] += _fold(dv * nv)
        acc_beta[...] += _fold(dv)
        acc_b[...] += _fold(duc)
        dbuf[pl.ds(0, tl), :] = duc
        ducn, _, _ = d_conv_out(don_ref[...], ucn_ref[...])
        dbuf[pl.ds(tl, hb), :] = jnp.where(i < nb - 1, ducn, 0.0)
        xp = xp_ref[...]
        ubuf[pl.ds(0, hb), :] = jnp.where(i > 0, xp[:, :c] * _sig(xp[:, c:]), 0.0)
        x_ = x_ref[...]
        val = x_[:, :c]
        sgate = _sig(x_[:, c:])
        ubuf[pl.ds(hb, tl), :] = val * sgate
        du = w_ref[0:1, :] * dbuf[pl.ds(k_taps - 1, tl), :]
        for k in range(1, k_taps):
            du = du + w_ref[k:k + 1, :] * dbuf[pl.ds(k_taps - 1 - k, tl), :]
        for k in range(k_taps):
            acc_w[k] += _fold(duc * ubuf[pl.ds(hb - (k_taps - 1) + k, tl), :])
        dx_ref[:, :c] = (du * sgate).astype(BF16)
        dx_ref[:, c:] = (du * val * sgate * (1.0 - sgate)).astype(BF16)

        @pl.when(i == nb - 1)
        def _():
            dw_ref[...] = jnp.zeros_like(dw_ref)
            for k in range(k_taps):
                dw_ref[k:k + 1, :] = jnp.sum(acc_w[k], axis=0, keepdims=True)
            db_ref[...] = jnp.sum(acc_b[...], axis=0, keepdims=True)
            dg_ref[...] = jnp.sum(acc_g[...], axis=0, keepdims=True)
            dbeta_ref[...] = jnp.sum(acc_beta[...], axis=0, keepdims=True)

    vec = _sds((1, c))
    return _pc(body, name=name, out_shape=(_sds((n, 2 * c), BF16), _sds((hb, c)), vec, vec, vec), grid=(nb,),
               in_specs=[_row(tl, c), _next(tl, hb, c, n), _row(tl, c), _next(tl, hb, c, n), _row(tl, 2 * c),
                         _prev(tl, hb, 2 * c), _const((hb, c)), _const((1, c)), _const((1, c))],
               out_specs=(_row(tl, 2 * c), _const((hb, c)), _const((1, c)), _const((1, c)), _const((1, c))),
               scratch_shapes=[pltpu.VMEM((tl + hb, c), F32), pltpu.VMEM((tl + hb, c), F32),
                               pltpu.VMEM((k_taps, SUBLANES, c), F32), pltpu.VMEM((SUBLANES, c), F32),
                               pltpu.VMEM((SUBLANES, c), F32), pltpu.VMEM((SUBLANES, c), F32)],
               compiler_params=_params(1))(dout, dout, uc, uc, conf_in, conf_in, conv_w, ln_g, ln_b)


def _sc_fwd(sc_in, conv_w, name):
    n = sc_in.shape[0]
    c = SC_WIDTH
    tl = _tile(n)
    hb = CONV_HALO

    nu = _strip_units_per_trip(tl)
    nt = tl // (nu * SUBLANES)

    def body(x_ref, xp_ref, w_ref, o_ref):
        i = pl.program_id(0)
        row = lax.broadcasted_iota(jnp.int32, (SUBLANES, LANES), 0)
        for lc in range(c // LANES):
            ls = slice(lc * LANES, (lc + 1) * LANES)
            gc_ls = slice(c + lc * LANES, c + (lc + 1) * LANES)
            xv_ls = slice(2 * c + lc * LANES, 2 * c + (lc + 1) * LANES)
            w = [_bcast_row(w_ref, k, ls) for k in range(3)]
            gc0 = jnp.where(i > 0, xp_ref[:, gc_ls], 0.0)
            xv0 = xp_ref[:, xv_ls]

            def strip(t, carry):
                gcs = _strip_units(x_ref, gc_ls, t, nu, nt, halo_prev=gc0)
                xvs = _strip_units(x_ref, xv_ls, t, nu, nt, halo_prev=xv0)
                ps = [a * b for a, b in zip(gcs, xvs)]
                outs = []
                for u in range(nu):
                    prev, cur = ps[u], ps[u + 1]
                    cv = w[0] * _down(prev, cur, 2, row) + w[1] * _down(prev, cur, 1, row) + w[2] * cur
                    outs.append(x_ref[_unit_rows(t, u, nu), ls] * cv)
                for p in range(nu // 2):
                    o_ref[_pair_rows(t, p, nu), ls] = jnp.concatenate(outs[2 * p:2 * p + 2], axis=0).astype(BF16)
                return carry

            lax.fori_loop(0, nt, strip, 0)

    return _pc(body, name=name, out_shape=_sds((n, c), BF16), grid=(n // tl,),
               in_specs=[_row(tl, 3 * c), _prev(tl, hb, 3 * c), _const((SUBLANES, c))], out_specs=_row(tl, c),
               compiler_params=_params(1))(sc_in, sc_in, conv_w)


def _sc_bwd(dout, sc_in, conv_w, name):
    n = sc_in.shape[0]
    c = SC_WIDTH
    tl = _tile(n)
    nb = n // tl
    hb = CONV_HALO

    nu = _strip_units_per_trip(tl)
    nt = tl // (nu * SUBLANES)

    def body(do_ref, don_ref, x_ref, xp_ref, xn_ref, w_ref, dx_ref, dw_ref, acc_w):
        i = pl.program_id(0)

        @pl.when(i == 0)
        def _():
            acc_w[...] = jnp.zeros_like(acc_w)

        row = lax.broadcasted_iota(jnp.int32, (SUBLANES, LANES), 0)
        zero = jnp.zeros((SUBLANES, LANES), F32)
        for lc in range(c // LANES):
            ls = slice(lc * LANES, (lc + 1) * LANES)
            gc_ls = slice(c + lc * LANES, c + (lc + 1) * LANES)
            xv_ls = slice(2 * c + lc * LANES, 2 * c + (lc + 1) * LANES)
            w = [_bcast_row(w_ref, k, ls) for k in range(3)]
            gc0 = jnp.where(i > 0, xp_ref[:, gc_ls], 0.0)
            xv0 = xp_ref[:, xv_ls]
            don0 = jnp.where(i < nb - 1, don_ref[:, ls], 0.0)
            gbn0 = xn_ref[:, ls]

            def strip(t, carry):
                acc = list(carry)
                dos = _strip_units(do_ref, ls, t, nu, nt, halo_next=don0)
                gbs = _strip_units(x_ref, ls, t, nu, nt, halo_next=gbn0)
                gcs = _strip_units(x_ref, gc_ls, t, nu, nt, halo_prev=gc0)
                xvs = _strip_units(x_ref, xv_ls, t, nu, nt, halo_prev=xv0)
                dcv = [a * b for a, b in zip(dos, gbs)]
                ps = [a * b for a, b in zip(gcs, xvs)]
                d_gb, d_gc, d_xv = [], [], []
                for u in range(nu):
                    prev, cur = ps[u], ps[u + 1]
                    p1, p2 = _down(prev, cur, 1, row), _down(prev, cur, 2, row)
                    d, dn = dcv[u], dcv[u + 1]
                    dp = w[2] * d + w[1] * _up(d, dn, 1, row) + w[0] * _up(d, dn, 2, row)
                    d_gb.append(dos[u] * (w[0] * p2 + w[1] * p1 + w[2] * cur))
                    d_gc.append(dp * xvs[u + 1])
                    d_xv.append(dp * gcs[u + 1])
                    acc[0] = acc[0] + d * p2
                    acc[1] = acc[1] + d * p1
                    acc[2] = acc[2] + d * cur
                for p in range(nu // 2):
                    rows = _pair_rows(t, p, nu)
                    for vals, lanes in ((d_gb, ls), (d_gc, gc_ls), (d_xv, xv_ls)):
                        dx_ref[rows, lanes] = jnp.concatenate(vals[2 * p:2 * p + 2], axis=0).astype(BF16)
                return tuple(acc)

            res = lax.fori_loop(0, nt, strip, (zero,) * 3)
            for k in range(3):
                acc_w[k, :, ls] += res[k]

        @pl.when(i == nb - 1)
        def _():
            dw_ref[...] = jnp.zeros_like(dw_ref)
            for k in range(3):
                dw_ref[k:k + 1, :] = jnp.sum(acc_w[k], axis=0, keepdims=True)

    return _pc(body, name=name, out_shape=(_sds((n, 3 * c), BF16), _sds((SUBLANES, c))), grid=(nb,),
               in_specs=[_row(tl, c), _next(tl, hb, c, n), _row(tl, 3 * c), _prev(tl, hb, 3 * c),
                         _next(tl, hb, c, n), _const((SUBLANES, c))],
               out_specs=(_row(tl, 3 * c), _const((SUBLANES, c))),
               scratch_shapes=[pltpu.VMEM((3, SUBLANES, c), F32)],
               compiler_params=_params(1))(dout, dout, sc_in, sc_in, sc_in, conv_w)


def _merge_fwd(gates, ya, yb, yc, b_gate, name):
    n, d = ya.shape
    tl = _tile(n)

    def body(gt_ref, ya_ref, yb_ref, yc_ref, b_ref, o_ref):
        gt = _sig(gt_ref[...] + b_ref[...])
        o_ref[...] = (gt[:, :d] * ya_ref[...] + gt[:, d:2 * d] * yb_ref[...] + gt[:, 2 * d:] * yc_ref[...]).astype(BF16)

    return _pc(body, name=name, out_shape=_sds((n, d), BF16), grid=(n // tl,),
               in_specs=[_row(tl, 3 * d), _row(tl, d), _row(tl, d), _row(tl, d), _const((1, 3 * d))],
               out_specs=_row(tl, d), compiler_params=_params(1))(gates, ya, yb, yc, b_gate)


def _merge_bwd(dm, gates, ya, yb, yc, b_gate, name):
    n, d = ya.shape
    tl = _tile(n)
    nb = n // tl

    def body(dm_ref, gt_ref, ya_ref, yb_ref, yc_ref, b_ref, dya_ref, dyb_ref, dyc_ref, dgt_ref, db_ref, acc):
        i = pl.program_id(0)

        @pl.when(i == 0)
        def _():
            acc[...] = jnp.zeros_like(acc)

        dm_ = dm_ref[...]
        gt = _sig(gt_ref[...] + b_ref[...])
        for idx, (y_ref, dy_ref) in enumerate(((ya_ref, dya_ref), (yb_ref, dyb_ref), (yc_ref, dyc_ref))):
            gk = gt[:, idx * d:(idx + 1) * d]
            dy_ref[...] = (dm_ * gk).astype(BF16)
            dpre = dm_ * y_ref[...] * gk * (1.0 - gk)
            dgt_ref[:, idx * d:(idx + 1) * d] = dpre.astype(BF16)
            acc[:, idx * d:(idx + 1) * d] += _fold(dpre)

        @pl.when(i == nb - 1)
        def _():
            db_ref[...] = jnp.sum(acc[...], axis=0, keepdims=True)

    bf = _sds((n, d), BF16)
    return _pc(body, name=name, out_shape=(bf, bf, bf, _sds((n, 3 * d), BF16), _sds((1, 3 * d))), grid=(nb,),
               in_specs=[_row(tl, d), _row(tl, 3 * d), _row(tl, d), _row(tl, d), _row(tl, d), _const((1, 3 * d))],
               out_specs=(_row(tl, d), _row(tl, d), _row(tl, d), _row(tl, 3 * d), _const((1, 3 * d))),
               scratch_shapes=[pltpu.VMEM((SUBLANES, 3 * d), F32)], compiler_params=_params(1))(
        dm, gates, ya, yb, yc, b_gate)


FFN_COLS = 1408
FFN_STRIP = 64
FFN_STRIP_BWD = 32


def _down(prev, cur, s, row):
    return jnp.where(row < s, pltpu.roll(prev, s, 0), pltpu.roll(cur, s, 0))


def _up(cur, nxt, s, row):
    return jnp.where(row < SUBLANES - s, pltpu.roll(cur, SUBLANES - s, 0), pltpu.roll(nxt, SUBLANES - s, 0))


def _ffn_mid(up, conv_w, conv_b, name):
    n = up.shape[0]
    tl = _tile(n)
    hb = CONV_HALO
    tc = FFN_COLS
    ncb = D_FF // tc

    def spec(shape_rows, idx_fn, off):
        return pl.BlockSpec((shape_rows, tc), lambda j, i, off=off: (idx_fn(i), j + off))

    r = tl // hb
    cur = lambda i: i
    prv = lambda i: jnp.maximum(i * r - 1, 0)

    def body(g_ref, gp_ref, v_ref, vp_ref, wg_ref, wv_ref, bg_ref, bv_ref, o_ref):
        i = pl.program_id(1)
        row = lax.broadcasted_iota(jnp.int32, (SUBLANES, LANES), 0)
        full = lambda ref, k, ls: jnp.broadcast_to(ref[k:k + 1, ls], (SUBLANES, LANES))
        for lc in range(tc // LANES):
            ls = slice(lc * LANES, (lc + 1) * LANES)
            wg = [full(wg_ref, k, ls) for k in range(3)]
            wv = [full(wv_ref, k, ls) for k in range(3)]
            bg, bv = full(bg_ref, 0, ls), full(bv_ref, 0, ls)

            def conv(prev, x, w, b):
                return b + w[0] * _down(prev, x, 2, row) + w[1] * _down(prev, x, 1, row) + w[2] * x

            def strip(t, carry):
                gs, vs = [carry[0]], [carry[1]]
                for u in range(FFN_STRIP // SUBLANES):
                    rows = pl.ds(pl.multiple_of(t * FFN_STRIP + u * SUBLANES, SUBLANES), SUBLANES)
                    gs.append(g_ref[rows, ls])
                    vs.append(v_ref[rows, ls])
                outs = []
                for u in range(FFN_STRIP // SUBLANES):
                    ug = conv(gs[u], gs[u + 1], wg, bg)
                    outs.append(ug * _sig(ug) * conv(vs[u], vs[u + 1], wv, bv))
                for p in range(FFN_STRIP // 16):
                    rows = pl.ds(pl.multiple_of(t * FFN_STRIP + p * 16, 16), 16)
                    o_ref[rows, ls] = jnp.concatenate(outs[2 * p:2 * p + 2], axis=0).astype(BF16)
                return gs[-1], vs[-1]

            lax.fori_loop(0, tl // FFN_STRIP, strip,
                          (jnp.where(i > 0, gp_ref[:, ls], 0.0), jnp.where(i > 0, vp_ref[:, ls], 0.0)))

    wspec = lambda off: pl.BlockSpec((SUBLANES, tc), lambda j, i, off=off: (0, j + off))
    bspec = lambda off: pl.BlockSpec((1, tc), lambda j, i, off=off: (0, j + off))
    return _pc(body, name=name, out_shape=_sds((n, D_FF), BF16), grid=(ncb, n // tl),
               in_specs=[spec(tl, cur, 0), spec(hb, prv, 0), spec(tl, cur, ncb), spec(hb, prv, ncb),
                         wspec(0), wspec(ncb), bspec(0), bspec(ncb)],
               out_specs=pl.BlockSpec((tl, tc), lambda j, i: (i, j)), compiler_params=_params(2))(
        up, up, up, up, conv_w, conv_w, conv_b, conv_b)


def _ffn_mid_bwd(da, up, conv_w, conv_b, name):
    n = up.shape[0]
    tl = _tile(n)
    nb = n // tl
    nt = tl // FFN_STRIP_BWD
    hb = CONV_HALO
    tc = FFN_COLS
    ncb = D_FF // tc
    r = tl // hb
    last = n // hb - 1
    cur = lambda i: i
    prv = lambda i: jnp.maximum(i * r - 1, 0)
    nxt = lambda i: jnp.minimum((i + 1) * r, last)

    def spec(shape_rows, idx_fn, off):
        return pl.BlockSpec((shape_rows, tc), lambda j, i, off=off: (idx_fn(i), j + off))

    def body(da_ref, dan_ref, g_ref, gp_ref, gn_ref, v_ref, vp_ref, vn_ref, wg_ref, wv_ref, bg_ref, bv_ref,
             dg_ref, dv_ref, dwg_ref, dwv_ref, dbg_ref, dbv_ref, acc_w, acc_b):
        i = pl.program_id(1)

        @pl.when(i == 0)
        def _():
            acc_w[...] = jnp.zeros_like(acc_w)
            acc_b[...] = jnp.zeros_like(acc_b)

        row = lax.broadcasted_iota(jnp.int32, (SUBLANES, LANES), 0)
        full = lambda ref, k, ls: jnp.broadcast_to(ref[k:k + 1, ls], (SUBLANES, LANES))
        zero = jnp.zeros((SUBLANES, LANES), F32)

        def d_conv_out(da_, ug, uv):
            s = _sig(ug)
            return da_ * uv * (s * (1.0 + ug * (1.0 - s))), da_ * (ug * s)

        for lc in range(tc // LANES):
            ls = slice(lc * LANES, (lc + 1) * LANES)
            wg = [full(wg_ref, k, ls) for k in range(3)]
            wv = [full(wv_ref, k, ls) for k in range(3)]
            bg, bv = full(bg_ref, 0, ls), full(bv_ref, 0, ls)

            def unit(prev_g, g, prev_v, v, da_):
                g1, g2 = _down(prev_g, g, 1, row), _down(prev_g, g, 2, row)
                v1, v2 = _down(prev_v, v, 1, row), _down(prev_v, v, 2, row)
                ug = bg + wg[0] * g2 + wg[1] * g1 + wg[2] * g
                uv = bv + wv[0] * v2 + wv[1] * v1 + wv[2] * v
                dug, duv = d_conv_out(da_, ug, uv)
                return dug, duv, (g2, g1, g), (v2, v1, v)

            def d_in(d, d_next, w):
                return w[2] * d + w[1] * _up(d, d_next, 1, row) + w[0] * _up(d, d_next, 2, row)

            tail = pl.ds(tl - SUBLANES, SUBLANES)
            dgn, dvn, _, _ = unit(g_ref[tail, ls], gn_ref[:, ls], v_ref[tail, ls], vn_ref[:, ls], dan_ref[:, ls])
            dgn = jnp.where(i < nb - 1, dgn, 0.0)
            dvn = jnp.where(i < nb - 1, dvn, 0.0)
            gp0 = jnp.where(i > 0, gp_ref[:, ls], 0.0)
            vp0 = jnp.where(i > 0, vp_ref[:, ls], 0.0)

            def strip(tt, carry):
                dgn, dvn = carry[0], carry[1]
                aw, ab = list(carry[2:8]), list(carry[8:10])
                t = nt - 1 - tt
                nu = FFN_STRIP_BWD // SUBLANES
                rm = pl.multiple_of(jnp.maximum(t * FFN_STRIP_BWD - SUBLANES, 0), SUBLANES)
                gs = [jnp.where(t > 0, g_ref[pl.ds(rm, SUBLANES), ls], gp0)]
                vs = [jnp.where(t > 0, v_ref[pl.ds(rm, SUBLANES), ls], vp0)]
                das = []
                for u in range(nu):
                    rows = pl.ds(pl.multiple_of(t * FFN_STRIP_BWD + u * SUBLANES, SUBLANES), SUBLANES)
                    gs.append(g_ref[rows, ls])
                    vs.append(v_ref[rows, ls])
                    das.append(da_ref[rows, ls])
                dgs, dvs = [None] * nu + [dgn], [None] * nu + [dvn]
                for u in reversed(range(nu)):
                    dgs[u], dvs[u], gsh, vsh = unit(gs[u], gs[u + 1], vs[u], vs[u + 1], das[u])
                    for k in range(3):
                        aw[k] = aw[k] + dgs[u] * gsh[k]
                        aw[3 + k] = aw[3 + k] + dvs[u] * vsh[k]
                    ab[0] = ab[0] + dgs[u]
                    ab[1] = ab[1] + dvs[u]
                for p in range(nu // 2):
                    rows = pl.ds(pl.multiple_of(t * FFN_STRIP_BWD + p * 16, 16), 16)
                    dg_ref[rows, ls] = jnp.concatenate([d_in(dgs[2 * p], dgs[2 * p + 1], wg),
                                                        d_in(dgs[2 * p + 1], dgs[2 * p + 2], wg)], axis=0).astype(BF16)
                    dv_ref[rows, ls] = jnp.concatenate([d_in(dvs[2 * p], dvs[2 * p + 1], wv),
                                                        d_in(dvs[2 * p + 1], dvs[2 * p + 2], wv)], axis=0).astype(BF16)
                return (dgs[0], dvs[0], *aw, *ab)

            res = lax.fori_loop(0, nt, strip, (dgn, dvn) + (zero,) * 8)
            for k in range(3):
                acc_w[0, k, :, ls] += res[2 + k]
                acc_w[1, k, :, ls] += res[5 + k]
            acc_b[0, :, ls] += res[8]
            acc_b[1, :, ls] += res[9]

        @pl.when(i == nb - 1)
        def _():
            for t, (dw_ref, db_ref) in enumerate(((dwg_ref, dbg_ref), (dwv_ref, dbv_ref))):
                dw_ref[...] = jnp.zeros_like(dw_ref)
                for k in range(3):
                    dw_ref[k:k + 1, :] = jnp.sum(acc_w[t, k], axis=0, keepdims=True)
                db_ref[...] = jnp.sum(acc_b[t], axis=0, keepdims=True)

    wspec = lambda off: pl.BlockSpec((SUBLANES, tc), lambda j, i, off=off: (0, j + off))
    bspec = lambda off: pl.BlockSpec((1, tc), lambda j, i, off=off: (0, j + off))
    ospec = lambda off: pl.BlockSpec((tl, tc), lambda j, i, off=off: (i, j + off))
    dg, dv, dwg, dwv, dbg, dbv = _pc(
        body, name=name,
        out_shape=(_sds((n, D_FF), BF16), _sds((n, D_FF), BF16), _sds((SUBLANES, D_FF)), _sds((SUBLANES, D_FF)),
                   _sds((1, D_FF)), _sds((1, D_FF))),
        grid=(ncb, nb),
        in_specs=[spec(tl, cur, 0), spec(hb, nxt, 0),
                  spec(tl, cur, 0), spec(hb, prv, 0), spec(hb, nxt, 0),
                  spec(tl, cur, ncb), spec(hb, prv, ncb), spec(hb, nxt, ncb),
                  wspec(0), wspec(ncb), bspec(0), bspec(ncb)],
        out_specs=(ospec(0), ospec(0), wspec(0), wspec(0), bspec(0), bspec(0)),
        scratch_shapes=[pltpu.VMEM((2, 3, SUBLANES, tc), F32), pltpu.VMEM((2, SUBLANES, tc), F32)],
        compiler_params=_params(2))(da, da, up, up, up, up, up, up, conv_w, conv_w, conv_b, conv_b)
    return dg, dv, jnp.concatenate([dwg, dwv], axis=1), jnp.concatenate([dbg, dbv], axis=1)


def _position():
    return lax.axis_index("x"), lax.axis_index("y"), lax.axis_index("c")


def _all_gather(locals_, name):
    n = len(locals_)

    def body(*refs):
        x_refs, out_refs = refs[:n], refs[n:2 * n]
        send_sems, recv_sems, local_sems = refs[2 * n:]
        x, y, cc = _position()
        me, sibling = (x, y, cc), (x, y, 1 - cc)
        chips = [(1 - x, y), (x, 1 - y), (1 - x, 1 - y)]

        def slot(a, px, py, pc):
            return out_refs[a].at[4 * px + 2 * py + pc]

        def copy(k, a, block, to, own=False):
            return pltpu.make_async_remote_copy(
                src_ref=x_refs[a] if own else slot(a, *block), dst_ref=slot(a, *block),
                send_sem=send_sems.at[k, a], recv_sem=recv_sems.at[k, a], device_id=to, device_id_type=MESH)

        mine = [pltpu.make_async_copy(x_refs[a], slot(a, *me), local_sems.at[a]) for a in range(n)]
        first = [copy(1 + j, a, me, (*chip, cc), own=True) for j, chip in enumerate(chips) for a in range(n)]
        first += [copy(0, a, me, sibling, own=True) for a in range(n)]
        for cp in mine + first:
            cp.start()
        passed = []
        for j, chip in enumerate(chips):
            for a in range(n):
                copy(1 + j, a, (*chip, cc), me).wait_recv()
                cp = copy(4 + j, a, (*chip, cc), sibling)
                cp.start()
                passed.append(cp)
        for a in range(n):
            copy(0, a, sibling, me).wait_recv()
        for j, chip in enumerate(chips):
            for a in range(n):
                copy(4 + j, a, (*chip, 1 - cc), me).wait_recv()
        for cp in first + passed:
            cp.wait_send()
        for cp in mine:
            cp.wait()

    hbm = pl.BlockSpec(memory_space=pl.ANY)
    return _pc(body, name=name, out_shape=[_sds((N_DEV,) + a.shape, a.dtype) for a in locals_],
               in_specs=[hbm] * n, out_specs=[hbm] * n,
               scratch_shapes=[pltpu.SemaphoreType.DMA((7, n)), pltpu.SemaphoreType.DMA((7, n)),
                               pltpu.SemaphoreType.DMA((n,))])(*locals_)


def _peers():
    x, y, cc = _position()
    others = []
    for fx, fy, fc in ((0, 0, 1), (1, 0, 0), (0, 1, 0), (1, 1, 0), (1, 0, 1), (0, 1, 1), (1, 1, 1)):
        p = (1 - x if fx else x, 1 - y if fy else y, 1 - cc if fc else cc)
        others.append((p, 4 * p[0] + 2 * p[1] + p[2]))
    return 4 * x + 2 * y + cc, others


def _gather_start(locals_, after, name):
    n = len(locals_)
    me, _ = _peers()
    lands = [lax.dynamic_update_slice(lax.empty((N_DEV,) + a.shape, a.dtype), a[None], (me,) + (0,) * a.ndim)
             for a in locals_]

    def body(*refs):
        x_refs, land_refs = refs[:n], refs[n:2 * n]
        send_sems, recv_sems, token = refs[2 * n + 1], refs[2 * n + 2], refs[-1]
        me_idx, others = _peers()
        for k, (peer, _) in enumerate(others):
            for a in range(n):
                pltpu.make_async_remote_copy(
                    src_ref=x_refs[a], dst_ref=land_refs[a].at[me_idx], send_sem=send_sems.at[k * n + a],
                    recv_sem=recv_sems.at[k * n + a], device_id=peer, device_id_type=MESH).start()
        token[...] = jnp.zeros_like(token)

    hbm = pl.BlockSpec(memory_space=pltpu.HBM)
    sem = pl.BlockSpec(memory_space=pltpu.SEMAPHORE)
    out = _pc(body, name=name,
              out_shape=(pltpu.SemaphoreType.DMA((7 * n,)), pltpu.SemaphoreType.DMA((7 * n,)),
                         *[pltpu.HBM(a.shape, a.dtype) for a in locals_], *[pltpu.HBM(l.shape, l.dtype) for l in lands],
                         _sds((SUBLANES, LANES))),
              in_specs=[hbm] * (2 * n) + [pl.BlockSpec(memory_space=pl.ANY)],
              out_specs=(sem, sem, *([hbm] * (2 * n)), pl.BlockSpec(memory_space=pltpu.VMEM)),
              input_output_aliases={i: 2 + i for i in range(2 * n)},
              compiler_params=pltpu.CompilerParams(has_side_effects=pltpu.SideEffectType.DATAFLOW_SIDE_EFFECTING))(
        *[pltpu.with_memory_space_constraint(a, pltpu.HBM) for a in locals_],
        *[pltpu.with_memory_space_constraint(l, pltpu.HBM) for l in lands], after)
    return (out[0], out[1], list(out[2:2 + n]), list(out[2 + n:2 + 2 * n])), out[-1]


def _gather_wait(state, after, name):
    send_sems, recv_sems, x_thru, land_thru = state
    n = len(x_thru)

    def body(*refs):
        x_refs, land_refs = refs[:n], refs[n:2 * n]
        send_sems, recv_sems = refs[2 * n], refs[2 * n + 1]
        _, others = _peers()
        for k, (peer, peer_idx) in enumerate(others):
            for a in range(n):
                cp = pltpu.make_async_remote_copy(
                    src_ref=x_refs[a], dst_ref=land_refs[a].at[peer_idx], send_sem=send_sems.at[k * n + a],
                    recv_sem=recv_sems.at[k * n + a], device_id=peer, device_id_type=MESH)
                cp.wait_send()
                cp.wait_recv()

    hbm = pl.BlockSpec(memory_space=pltpu.HBM)
    sem = pl.BlockSpec(memory_space=pltpu.SEMAPHORE)
    out = _pc(body, name=name, out_shape=tuple(pltpu.HBM(a.shape, a.dtype) for a in x_thru + land_thru),
              in_specs=[hbm] * (2 * n) + [sem, sem, pl.BlockSpec(memory_space=pl.ANY)], out_specs=tuple([hbm] * (2 * n)),
              input_output_aliases={i: i for i in range(2 * n)},
              compiler_params=pltpu.CompilerParams(has_side_effects=pltpu.SideEffectType.DATAFLOW_SIDE_EFFECTING))(
        *x_thru, *land_thru, send_sems, recv_sems, after)
    return list(out[n:])


N_CHIPS = 4


def _sibling_swap(parts, name):
    n = len(parts)

    def body(*refs):
        g_refs, got_refs = refs[:n], refs[n:2 * n]
        send_sems, recv_sems = refs[2 * n:]
        x, y, cc = _position()
        swaps = []
        for q in range(N_CHIPS):
            for a in range(n):
                swaps.append(pltpu.make_async_remote_copy(
                    src_ref=g_refs[a].at[2 * q + 1 - cc], dst_ref=got_refs[a].at[q], send_sem=send_sems.at[q, a],
                    recv_sem=recv_sems.at[q, a], device_id=(x, y, 1 - cc), device_id_type=MESH))
        for cp in swaps:
            cp.start()
        for cp in swaps:
            cp.wait_recv()
        for cp in swaps:
            cp.wait_send()

    hbm = pl.BlockSpec(memory_space=pl.ANY)
    return _pc(body, name=name, out_shape=[_sds((N_CHIPS,) + a.shape[1:], a.dtype) for a in parts],
               in_specs=[hbm] * n, out_specs=[hbm] * n,
               scratch_shapes=[pltpu.SemaphoreType.DMA((N_CHIPS, n)), pltpu.SemaphoreType.DMA((N_CHIPS, n))])(*parts)


def _pair_add(part, got, core, name):
    q, a, b = got.shape
    ta = _block_rows(a, b)

    def body(core_ref, k_ref, g_ref, o_ref):
        o_ref[...] = (k_ref[...].astype(F32) + g_ref[...].astype(F32)).astype(BF16)

    spec = pl.BlockSpec((None, ta, b), lambda c, i, core_ref: (c, i, 0))
    own = pl.BlockSpec((None, None, ta, b), lambda c, i, core_ref: (c, core_ref[0], i, 0))
    grid_spec = pltpu.PrefetchScalarGridSpec(num_scalar_prefetch=1, grid=(q, a // ta), in_specs=[own, spec],
                                             out_specs=spec)
    return _pc(body, name=name, out_shape=_sds(got.shape, BF16), grid_spec=grid_spec, compiler_params=_params(2))(
        core, part.reshape((N_CHIPS, 2) + part.shape[1:]), got)


def _chip_exchange(sums, name):
    n = len(sums)

    def body(*refs):
        g_refs, out_refs = refs[:n], refs[n:2 * n]
        send_sems, recv_sems = refs[2 * n:]
        me_q, others = _chip_peers()
        sends, recvs = [], []
        for k, (peer, peer_q) in enumerate(others):
            for a in range(n):
                sends.append(pltpu.make_async_remote_copy(
                    src_ref=g_refs[a].at[peer_q], dst_ref=out_refs[a].at[me_q], send_sem=send_sems.at[k, a],
                    recv_sem=recv_sems.at[k, a], device_id=peer, device_id_type=MESH))
                recvs.append(pltpu.make_async_remote_copy(
                    src_ref=g_refs[a].at[me_q], dst_ref=out_refs[a].at[peer_q], send_sem=send_sems.at[k, a],
                    recv_sem=recv_sems.at[k, a], device_id=peer, device_id_type=MESH))
        for cp in sends:
            cp.start()
        for cp in recvs:
            cp.wait_recv()
        for cp in sends:
            cp.wait_send()

    hbm = pl.BlockSpec(memory_space=pl.ANY)
    return _pc(body, name=name, out_shape=[_sds(a.shape, a.dtype) for a in sums],
               in_specs=[hbm] * n, out_specs=[hbm] * n,
               scratch_shapes=[pltpu.SemaphoreType.DMA((3, n)), pltpu.SemaphoreType.DMA((3, n))])(*sums)


def _chip_peers():
    x, y, cc = _position()
    others = []
    for fx, fy in ((1, 0), (0, 1), (1, 1)):
        px, py = (1 - x if fx else x), (1 - y if fy else y)
        others.append(((px, py, cc), 2 * px + py))
    return 2 * x + y, others


def _chip_exchange_start(sums, name):
    n = len(sums)
    lands = [lax.empty(a.shape, a.dtype) for a in sums]

    def body(*refs):
        g_refs, land_refs = refs[:n], refs[n:2 * n]
        send_sems, recv_sems, token = refs[2 * n], refs[2 * n + 1], refs[-1]
        me_q, others = _chip_peers()
        for k, (peer, peer_q) in enumerate(others):
            for a in range(n):
                pltpu.make_async_remote_copy(
                    src_ref=g_refs[a].at[peer_q], dst_ref=land_refs[a].at[me_q], send_sem=send_sems.at[k * n + a],
                    recv_sem=recv_sems.at[k * n + a], device_id=peer, device_id_type=MESH).start()
        token[...] = jnp.zeros_like(token)

    hbm = pl.BlockSpec(memory_space=pltpu.HBM)
    sem = pl.BlockSpec(memory_space=pltpu.SEMAPHORE)
    out = _pc(body, name=name,
              out_shape=(pltpu.SemaphoreType.DMA((3 * n,)), pltpu.SemaphoreType.DMA((3 * n,)),
                         *[pltpu.HBM(a.shape, a.dtype) for a in sums], *[pltpu.HBM(a.shape, a.dtype) for a in sums],
                         _sds((SUBLANES, LANES))),
              in_specs=[hbm] * (2 * n), out_specs=(sem, sem, *([hbm] * (2 * n)), pl.BlockSpec(memory_space=pltpu.VMEM)),
              input_output_aliases={i: 2 + i for i in range(2 * n)},
              compiler_params=pltpu.CompilerParams(has_side_effects=pltpu.SideEffectType.DATAFLOW_SIDE_EFFECTING))(
        *[pltpu.with_memory_space_constraint(a, pltpu.HBM) for a in sums],
        *[pltpu.with_memory_space_constraint(l, pltpu.HBM) for l in lands])
    return (out[0], out[1], list(out[2:2 + n]), list(out[2 + n:2 + 2 * n])), out[-1]


def _chip_exchange_wait(state, after, name):
    send_sems, recv_sems, g_thru, land_thru = state
    n = len(g_thru)

    def body(*refs):
        g_refs, land_refs = refs[:n], refs[n:2 * n]
        send_sems, recv_sems = refs[2 * n], refs[2 * n + 1]
        me_q, others = _chip_peers()
        for k, (peer, peer_q) in enumerate(others):
            for a in range(n):
                cp = pltpu.make_async_remote_copy(
                    src_ref=g_refs[a].at[me_q], dst_ref=land_refs[a].at[peer_q], send_sem=send_sems.at[k * n + a],
                    recv_sem=recv_sems.at[k * n + a], device_id=peer, device_id_type=MESH)
                cp.wait_send()
                cp.wait_recv()

    hbm = pl.BlockSpec(memory_space=pltpu.HBM)
    sem = pl.BlockSpec(memory_space=pltpu.SEMAPHORE)
    out = _pc(body, name=name, out_shape=tuple(pltpu.HBM(a.shape, a.dtype) for a in g_thru + land_thru),
              in_specs=[hbm] * (2 * n) + [sem, sem, pl.BlockSpec(memory_space=pl.ANY)], out_specs=tuple([hbm] * (2 * n)),
              input_output_aliases={i: i for i in range(2 * n)},
              compiler_params=pltpu.CompilerParams(has_side_effects=pltpu.SideEffectType.DATAFLOW_SIDE_EFFECTING))(
        *g_thru, *land_thru, send_sems, recv_sems, after)
    return list(out[:n]), list(out[n:])


def _block_rows(a, b):
    ta = a
    while ta * b > 256 * 1024 and ta % 32 == 0:
        ta //= 2
    return ta


def _reduce_adamw(parts, w, m, v, name):
    n_parts, s, a, b = parts.shape
    ta = _block_rows(a, b)

    def body(p_ref, w_ref, m_ref, v_ref, g_out, d_out, m_out, v_out):
        g = p_ref[0].astype(F32)
        for j in range(1, n_parts):
            g = g + p_ref[j].astype(F32)
        delta, m_new, v_new = _adamw(g, w_ref[...], m_ref[...], v_ref[...])
        g_out[...] = g
        d_out[...] = delta
        m_out[...] = m_new
        v_out[...] = v_new

    spec = pl.BlockSpec((None, ta, b), lambda l, i: (l, i, 0))
    return _pc(body, name=name, out_shape=(_sds((s, a, b)),) * 4, grid=(s, a // ta),
               in_specs=[pl.BlockSpec((n_parts, None, ta, b), lambda l, i: (0, l, i, 0)), spec, spec, spec],
               out_specs=(spec,) * 4, compiler_params=_params(2))(parts, w, m, v)


def _adamw(g, w, m, v):
    c1 = 1.0 - ADAM_B1 ** ADAM_STEP
    c2 = 1.0 - ADAM_B2 ** ADAM_STEP
    m_new = ADAM_B1 * m + (1.0 - ADAM_B1) * g
    v_new = ADAM_B2 * v + (1.0 - ADAM_B2) * (g * g)
    delta = -ADAM_LR * ((m_new / c1) / (jnp.sqrt(v_new / c2) + ADAM_EPS) + ADAM_WD * w)
    return delta, m_new, v_new


def _chip_reduce_adamw(own, recv, w, m, v, chip, name):
    s, a, b = w.shape
    ta = _block_rows(a, b)

    def body(chip_ref, *refs):
        p_refs, (w_ref, m_ref, v_ref), (g_out, d_out, m_out, v_out) = refs[:4 * s], refs[4 * s:4 * s + 3], refs[4 * s + 3:]
        layer = pl.program_id(0)
        g = None
        for l in range(s):
            gl = p_refs[4 * l][...].astype(F32)
            for j in range(1, N_CHIPS):
                gl = gl + p_refs[4 * l + j][...].astype(F32)
            g = gl if g is None else jnp.where(layer == l, gl, g)
        delta, m_new, v_new = _adamw(g, w_ref[...], m_ref[...], v_ref[...])
        g_out[...] = g
        d_out[...] = delta
        m_out[...] = m_new
        v_out[...] = v_new

    def part_spec(l, j):
        return pl.BlockSpec((None, ta, b), lambda layer, i, chip_ref, l=l, j=j: (
            (chip_ref[0] + j) % N_CHIPS, jnp.where(layer == l, i, 0), 0))

    spec = pl.BlockSpec((None, ta, b), lambda layer, i, chip_ref: (layer, i, 0))
    in_specs, args = [], []
    for l in range(s):
        for j in range(N_CHIPS):
            in_specs.append(part_spec(l, j))
            args.append(own[l] if j == 0 else recv[l])
    grid_spec = pltpu.PrefetchScalarGridSpec(num_scalar_prefetch=1, grid=(s, a // ta), in_specs=in_specs + [spec] * 3,
                                             out_specs=(spec,) * 4)
    return _pc(body, name=name, out_shape=(_sds((s, a, b)),) * 4, grid_spec=grid_spec, compiler_params=_params(2))(
        chip, *args, w, m, v)


MATRICES = (("ada_mix_w", 2), ("w_in", 2), ("w_ssd_out", 1), ("w_conf_out", 2), ("w_sc_out", 2), ("w_o", 1),
            ("ada_ffn_w", 2), ("w_up", 2), ("w_down", 1))
MIXER_MATRICES = ("ada_mix_w", "w_in", "w_ssd_out", "w_conf_out", "w_sc_out", "w_o")
FFN_MATRICES = ("ada_ffn_w", "w_up", "w_down")
CONV_WEIGHTS = (("ssd_conv_w", 2), ("conf_conv_w", 2), ("sc_conv_w", 2), ("ffn_conv_w", 2))
SHARDED = MATRICES + CONV_WEIGHTS
REPLICATED = ("ada_mix_b", "norm_mix_g", "b_gate", "ssd_conv_b", "ssd_dt_bias", "ssd_a_log", "ssd_d", "ssd_norm_g",
              "conf_conv_b", "conf_ln_g", "conf_ln_b", "ada_ffn_b", "norm_ffn_g", "ffn_conv_b", "final_norm_g")
WEIGHT_NAMES = ("ada_mix_w", "ada_mix_b", "norm_mix_g", "w_in", "b_gate", "ssd_conv_w", "ssd_conv_b", "ssd_dt_bias",
                "ssd_a_log", "ssd_d", "ssd_norm_g", "w_ssd_out", "conf_conv_w", "conf_conv_b", "conf_ln_g",
                "conf_ln_b", "w_conf_out", "sc_conv_w", "w_sc_out", "w_o", "ada_ffn_w", "ada_ffn_b", "norm_ffn_g",
                "w_up", "ffn_conv_w", "ffn_conv_b", "w_down", "final_norm_g")


def _pack_flat(arrays, cols, row_multiple, dtype):
    flat = jnp.concatenate([a.reshape(-1).astype(dtype) for a in arrays])
    rows = -(-flat.shape[0] // cols)
    rows = -(-rows // row_multiple) * row_multiple
    return jnp.pad(flat, (0, rows * cols - flat.shape[0])).reshape(rows, cols)


def _unpack_flat(flat2d, shapes):
    flat = flat2d.reshape(-1)
    out, off = [], 0
    for s in shapes:
        n = 1
        for d in s:
            n *= d
        out.append(flat[off:off + n].reshape(s))
        off += n
    return out


def _cols(g, lo, hi):
    b = g.shape[-1]
    pieces = []
    for k in range(N_DEV):
        a, e = max(lo, k * b), min(hi, (k + 1) * b)
        if a < e:
            pieces.append(g[k, :, a - k * b:e - k * b])
    return pieces[0] if len(pieces) == 1 else jnp.concatenate(pieces, axis=1)


def _rows(g):
    return g.reshape(N_DEV * g.shape[1], g.shape[2])


def _col_shards(segs, b):
    shards = []
    for k in range(N_DEV):
        lo, hi = k * b, (k + 1) * b
        pieces, off = [], 0
        for seg in segs:
            n = seg.shape[1]
            a, e = max(lo, off), min(hi, off + n)
            if a < e:
                pieces.append(seg[:, a - off:e - off])
            off += n
        shards.append(pieces[0] if len(pieces) == 1 else jnp.concatenate(pieces, axis=1))
    return jnp.stack(shards)


def _row_shards(full):
    return full.reshape(N_DEV, full.shape[0] // N_DEV, full.shape[1])


def _pad_rows(a, rows):
    return jnp.pad(a, ((0, rows - a.shape[0]), (0, 0)))


def _pad_lanes(a):
    return jnp.pad(a, ((0, 0), (0, LANES - a.shape[1])))


def _whole(g):
    return _cols(g, 0, N_DEV * g.shape[-1])


def _mixer_weights(full, i):
    row = lambda name: full[name][i].reshape(1, -1)
    conv = lambda name: _whole(full[name][:, i])
    w_in = full["w_in", i]
    return {
        "ada_mix_w": _whole(full["ada_mix_w", i]), "ada_mix_b": row("ada_mix_b"), "norm_mix_g": row("norm_mix_g"),
        "w_z": _cols(w_in, 0, OFF_Z), "w_xbc": _cols(w_in, OFF_Z, OFF_XBC),
        "w_dt": _pad_lanes(_cols(w_in, OFF_XBC, OFF_DT)), "w_conf": _cols(w_in, OFF_DT, OFF_CONF),
        "w_sc": _cols(w_in, OFF_CONF, OFF_SC), "w_gates": _cols(w_in, OFF_SC, N_IN),
        "b_gate": row("b_gate"),
        "ssd_conv_w": _pad_rows(conv("ssd_conv_w"), SUBLANES), "ssd_conv_b": row("ssd_conv_b"),
        "dt_bias": _pad_lanes(row("ssd_dt_bias")), "a_log": _pad_lanes(row("ssd_a_log")),
        "ssd_d": _pad_lanes(row("ssd_d")), "ssd_norm_g": row("ssd_norm_g"), "w_ssd_out": _rows(full["w_ssd_out", i]),
        "conf_conv_w": _pad_rows(conv("conf_conv_w"), CONF_HALO), "conf_conv_b": row("conf_conv_b"),
        "conf_ln_g": row("conf_ln_g"), "conf_ln_b": row("conf_ln_b"), "w_conf_out": _whole(full["w_conf_out", i]),
        "sc_conv_w": _pad_rows(conv("sc_conv_w"), SUBLANES), "w_sc_out": _whole(full["w_sc_out", i]),
        "w_o": _rows(full["w_o", i]),
    }


def _ffn_weights(full, i):
    row = lambda name: full[name][i].reshape(1, -1)
    return {
        "ada_ffn_w": _whole(full["ada_ffn_w", i]), "ada_ffn_b": row("ada_ffn_b"), "norm_ffn_g": row("norm_ffn_g"),
        "w_up": _whole(full["w_up", i]), "ffn_conv_w": _pad_rows(_whole(full["ffn_conv_w"][:, i]), SUBLANES),
        "ffn_conv_b": row("ffn_conv_b"), "w_down": _rows(full["w_down", i]),
    }


def _adaln(sc8, w, b, name):
    mod = _matmul(sc8, w, "nn", F32, name)[0:1, :] + b
    return mod[:, :D_MODEL], mod[:, D_MODEL:2 * D_MODEL], mod[:, 2 * D_MODEL:]


def _mixer_fwd(i, x, prev, sc8, wl):
    t = f"l{i}_"
    s = {}
    shift, scale, gate = _adaln(sc8, wl["ada_mix_w"], wl["ada_mix_b"], t + "ada_mix")
    if prev is None:
        s["x_in"] = x
        s["h"] = _prenorm_first(x, wl["norm_mix_g"], scale, shift, t + "norm_mix")
    else:
        s["x_in"], s["h"] = _prenorm_res(x, prev[0], prev[1], wl["norm_mix_g"], scale, shift, t + "norm_mix")
    s["scale_mix"], s["gate_mix"] = scale, gate
    h = s["h"]
    s["z"] = _matmul(h, wl["w_z"], "nn", F32, t + "in_z")
    s["xbc"] = _matmul(h, wl["w_xbc"], "nn", F32, t + "in_xbc")
    s["dt_raw"] = _matmul(h, wl["w_dt"], "nn", F32, t + "in_dt")
    s["conf"] = _matmul(h, wl["w_conf"], "nn", F32, t + "in_conf")
    s["sc"] = _matmul(h, wl["w_sc"], "nn", F32, t + "in_sc")
    s["gates"] = _matmul(h, wl["w_gates"], "nn", F32, t + "in_gates")
    s["pre"], s["dt"] = _ssd_pre(s["xbc"], s["dt_raw"], wl["ssd_conv_w"], wl["ssd_conv_b"], wl["dt_bias"],
                                 t + "ssd_pre")
    s["y"], s["hprev"] = _ssd_scan(s["pre"], s["dt"], wl["a_log"], wl["ssd_d"], t + "ssd_scan")
    s["ya_in"] = _ssd_post(s["y"], s["z"], wl["ssd_norm_g"], t + "ssd_post")
    s["yb_in"], s["uc"] = _conf_fwd(s["conf"], wl["conf_conv_w"], wl["conf_conv_b"], wl["conf_ln_g"],
                                    wl["conf_ln_b"], t + "conf")
    s["yc_in"] = _sc_fwd(s["sc"], wl["sc_conv_w"], t + "sconv")
    s["ya"] = _matmul(s["ya_in"], wl["w_ssd_out"], "nn", F32, t + "ssd_out")
    s["yb"] = _matmul(s["yb_in"], wl["w_conf_out"], "nn", F32, t + "conf_out")
    s["yc"] = _matmul(s["yc_in"], wl["w_sc_out"], "nn", F32, t + "sc_out")
    s["merged"] = _merge_fwd(s["gates"], s["ya"], s["yb"], s["yc"], wl["b_gate"], t + "merge")
    s["mix"] = _matmul(s["merged"], wl["w_o"], "nn", F32, t + "w_o")
    return s


def _ffn_fwd(i, s, sc8, wl):
    t = f"l{i}_"
    shift2, scale2, gate2 = _adaln(sc8, wl["ada_ffn_w"], wl["ada_ffn_b"], t + "ada_ffn")
    s["x_mid"], s["h2"] = _prenorm_res(s["x_in"], s["mix"], s["gate_mix"], wl["norm_ffn_g"], scale2, shift2,
                                       t + "norm_ffn")
    s["scale_ffn"], s["gate_ffn"] = scale2, gate2
    s["up"] = _matmul(s["h2"], wl["w_up"], "nn", F32, t + "w_up")
    s["a"] = _ffn_mid(s["up"], wl["ffn_conv_w"], wl["ffn_conv_b"], t + "ffn_mid")
    s["out"] = _matmul(s["a"], wl["w_down"], "nn", F32, t + "w_down")
    return s


def _layer_bwd(i, s, wl, sc8, dys_ffn, dx_after, dgate_ffn, prev, emit=None):
    t = f"l{i}_b_"
    g = {}
    da = _matmul(dys_ffn, wl["w_down"], "nt", F32, t + "d_a")
    g["w_down"] = _matmul(s["a"], dys_ffn, "tn", BF16, t + "dw_down")
    dug, duv, dfw, g["ffn_conv_b"] = _ffn_mid_bwd(da, s["up"], wl["ffn_conv_w"], wl["ffn_conv_b"], t + "ffn_mid")
    g["ffn_conv_w"] = dfw[:3]
    dh2 = _matmul_sum_nt([dug, duv], [wl["w_up"][:, :D_FF], wl["w_up"][:, D_FF:]], t + "d_h2")
    g["w_up"] = [_matmul(s["h2"], dug, "tn", BF16, t + "dw_up_g"), _matmul(s["h2"], duv, "tn", BF16, t + "dw_up_v")]
    dx_mid, dshift2, dscale2, g["norm_ffn_g"], dys_mix, dgate_mix = _norm_bwd(
        dh2, s["x_mid"], dx_after, wl["norm_ffn_g"], s["scale_ffn"], t + "norm_ffn", s["mix"], s["gate_mix"])
    dmod_ffn = jnp.concatenate([dshift2, dscale2, dgate_ffn], axis=1)
    g["ada_ffn_b"] = dmod_ffn
    g["ada_ffn_w"] = [_matmul(sc8, _pad_rows(dmod_ffn, SUBLANES), "tn", BF16, t + "dw_ada_ffn")]
    token = emit(i, "ffn", g) if emit is not None else None
    if token is not None:
        wl = {**wl, "b_gate": wl["b_gate"] + token}
    dmerged = _matmul(dys_mix, wl["w_o"], "nt", F32, t + "d_merged")
    g["w_o"] = _matmul(s["merged"], dys_mix, "tn", BF16, t + "dw_o")
    dya, dyb, dyc, dgates, g["b_gate"] = _merge_bwd(dmerged, s["gates"], s["ya"], s["yb"], s["yc"], wl["b_gate"],
                                                    t + "merge")
    dya_in = _matmul(dya, wl["w_ssd_out"], "nt", F32, t + "d_ya_in")
    g["w_ssd_out"] = _matmul(s["ya_in"], dya, "tn", BF16, t + "dw_ssd_out")
    dyb_in = _matmul(dyb, wl["w_conf_out"], "nt", F32, t + "d_yb_in")
    g["w_conf_out"] = [_matmul(s["yb_in"], dyb, "tn", BF16, t + "dw_conf_out")]
    dyc_in = _matmul(dyc, wl["w_sc_out"], "nt", F32, t + "d_yc_in")
    g["w_sc_out"] = [_matmul(s["yc_in"], dyc, "tn", BF16, t + "dw_sc_out")]
    dy, dz, g["ssd_norm_g"] = _ssd_post_bwd(dya_in, s["y"], s["z"], wl["ssd_norm_g"], t + "ssd_post")
    dpre, ddt, da_log, dd = _ssd_scan_bwd(s["pre"], s["dt"], s["hprev"], dy, wl["a_log"], wl["ssd_d"],
                                          t + "ssd_scan")
    g["ssd_a_log"], g["ssd_d"] = da_log[:, :SSD_HEADS], dd[:, :SSD_HEADS]
    dxbc, ddt_raw, dcw, g["ssd_conv_b"], ddtb = _ssd_pre_bwd(dpre, s["xbc"], ddt, s["dt_raw"], wl["ssd_conv_w"],
                                                             wl["dt_bias"], t + "ssd_pre")
    g["ssd_conv_w"], g["ssd_dt_bias"] = dcw[:4], ddtb[:, :SSD_HEADS]
    dconf, dccw, g["conf_conv_b"], g["conf_ln_g"], g["conf_ln_b"] = _conf_bwd(
        dyb_in, s["uc"], s["conf"], wl["conf_conv_w"], wl["conf_ln_g"], wl["conf_ln_b"], t + "conf")
    g["conf_conv_w"] = dccw[:CONF_KERNEL]
    dsc, dscw = _sc_bwd(dyc_in, s["sc"], wl["sc_conv_w"], t + "sconv")
    g["sc_conv_w"] = dscw[:3]
    segs = (("z", dz, "w_z"), ("xbc", dxbc, "w_xbc"), ("dt", ddt_raw, "w_dt"), ("conf", dconf, "w_conf"),
            ("sc", dsc, "w_sc"), ("gates", dgates, "w_gates"))
    dh = _matmul_sum_nt([dseg for _, dseg, _ in segs], [wl[wname] for _, _, wname in segs], t + "d_h")
    dw_segs = []
    for nm, dseg, wname in segs:
        dw = _matmul(s["h"], dseg, "tn", BF16, t + "dw_in_" + nm)
        dw_segs.append(dw[:, :SSD_HEADS] if nm == "dt" else dw)
    g["w_in"] = dw_segs
    if prev is None:
        dx_in, dshift, dscale, g["norm_mix_g"] = _norm_bwd(dh, s["x_in"], dx_mid, wl["norm_mix_g"], s["scale_mix"],
                                                          t + "norm_mix")
        back = None
    else:
        dx_in, dshift, dscale, g["norm_mix_g"], dys_prev, dgate_prev = _norm_bwd(
            dh, s["x_in"], dx_mid, wl["norm_mix_g"], s["scale_mix"], t + "norm_mix", prev[0], prev[1])
        back = (dys_prev, dgate_prev)
    dmod_mix = jnp.concatenate([dshift, dscale, dgate_mix], axis=1)
    g["ada_mix_b"] = dmod_mix
    g["ada_mix_w"] = [_matmul(sc8, _pad_rows(dmod_mix, SUBLANES), "tn", BF16, t + "dw_ada_mix")]
    return g, dx_in, back


def _device_step(x, c, target, full, fetch=None, emit=None):
    fetch = fetch or {}
    full = dict(full)
    sc8 = _pad_rows(c * (1.0 / (1.0 + jnp.exp(-c))), SUBLANES)
    wls, saved, prev, xcur = [], [], None, x
    for i in range(DEPTH):
        if (i, "mixer") in fetch:
            full.update(fetch[i, "mixer"](prev[0]))
        wl = _mixer_weights(full, i)
        s = _mixer_fwd(i, xcur, prev, sc8, wl)
        if (i, "ffn") in fetch:
            full.update(fetch[i, "ffn"](s["mix"]))
        wf = _ffn_weights(full, i)
        _ffn_fwd(i, s, sc8, wf)
        wls.append({**wl, **wf})
        saved.append(s)
        xcur, prev = s["x_mid"], (s["out"], s["gate_ffn"])
    gf = full["final_norm_g"].reshape(1, -1)
    last = saved[-1]
    loss, dx, dys, dgate, dgf = _final_loss(last["x_mid"], last["out"], last["gate_ffn"], gf, target, "final_loss")
    grads = [None] * DEPTH
    for i in reversed(range(DEPTH)):
        prev = None if i == 0 else (saved[i - 1]["out"], saved[i - 1]["gate_ffn"])
        grads[i], dx, back = _layer_bwd(i, saved[i], wls[i], sc8, dys, dx, dgate, prev, emit)
        token = emit(i, "mixer", grads[i]) if emit is not None else None
        if token is not None:
            wls[i - 1] = {**wls[i - 1], "ffn_conv_b": wls[i - 1]["ffn_conv_b"] + token}
        if back is not None:
            dys, dgate = back
    return loss[0, 0], dx, grads, dgf


def _step(x, c, target, weights, moments_m, moments_v):
    sharded_names = [n for n, _ in SHARDED]
    conv_names = [n for n, _ in CONV_WEIGHTS]
    shard = lambda key: weights[key[0]][key[1]].astype(BF16)
    first = [(n, 0) for n in MIXER_MATRICES]
    later = {(0, "ffn"): [(n, 0) for n in FFN_MATRICES], (1, "mixer"): [(n, 1) for n in MIXER_MATRICES + FFN_MATRICES]}
    gathered = _all_gather([shard(k) for k in first] + [weights[n] for n in conv_names], "gather_first")
    full = {n: weights[n] for n in REPLICATED}
    full.update(zip(first + conv_names, gathered))
    fetch, after = {}, gathered[0]
    for stage, keys in later.items():
        state, after = _gather_start([shard(k) for k in keys], after, f"gather_l{stage[0]}_{stage[1]}_start")
        fetch[stage] = functools.partial(
            lambda act, state, keys, nm: dict(zip(keys, _gather_wait(state, act, nm))),
            state=state, keys=keys, nm=f"gather_l{stage[0]}_{stage[1]}_wait")
    axis_of = dict(SHARDED)
    core = lax.axis_index("c").astype(jnp.int32).reshape(1)
    chip = (2 * lax.axis_index("x") + lax.axis_index("y")).astype(jnp.int32).reshape(1)
    ffn_names = list(FFN_MATRICES) + ["ffn_conv_w"]
    mixer_names = [n for n in sharded_names if n not in ffn_names]
    sums, received, pending = {}, {}, []

    def send(i, names, grads_i, last):
        keys = [(n, i) for n in names]
        parts = []
        for n in names:
            gw = grads_i[n]
            part = _row_shards(gw) if axis_of[n] == 1 else _col_shards(gw if isinstance(gw, list) else [gw],
                                                                       weights[n].shape[-1])
            parts.append(part.astype(BF16))
        tag = f"l{i}_{'ffn' if names is ffn_names else 'mixer'}"
        got = _sibling_swap(parts, "swap_grads_" + tag)
        pair = [_pair_add(p, g, core, f"pair_add_{n}_{i}") for n, p, g in zip(names, parts, got)]
        if last:
            sums.update(zip(keys, pair))
            received.update(zip(keys, _chip_exchange(pair, "exchange_grads_" + tag)))
            return None
        state, token = _chip_exchange_start(pair, "exchange_grads_" + tag + "_start")
        pending.append((keys, state, tag))
        return token[0:1, 0:1]

    def emit(i, kind, grads_i):
        if kind == "ffn" and i == 0:
            return send(0, ffn_names, grads_i, False)
        if kind == "mixer":
            return send(i, mixer_names if i == 0 else sharded_names, grads_i, i == 0)
        return None

    loss, grad_x, grads, dgf = _device_step(x[0], c + after[0:1, 0:1], target[0], full, fetch, emit)
    for keys, state, tag in pending:
        own, got = _chip_exchange_wait(state, grad_x, "exchange_grads_" + tag + "_wait")
        sums.update(zip(keys, own))
        received.update(zip(keys, got))
    big = {n: _chip_reduce_adamw([sums[n, i] for i in range(DEPTH)], [received[n, i] for i in range(DEPTH)],
                                 weights[n], moments_m[n], moments_v[n], chip, "adamw_" + n)
           for n in sharded_names}
    rep_grads = [dgf if n == "final_norm_g" else jnp.stack([grads[i][n].reshape(-1) for i in range(DEPTH)])
                 for n in REPLICATED]
    small_parts, = _all_gather([_pack_flat(rep_grads, LANES, SUBLANES, F32)], "gather_small_grads")
    pack_s = lambda d: _pack_flat([d[n] for n in REPLICATED], LANES, SUBLANES, F32)[None]
    small = _reduce_adamw(small_parts[:, None], pack_s(weights), pack_s(moments_m), pack_s(moments_v),
                          "adamw_replicated")
    small = [_unpack_flat(b, [weights[n].shape for n in REPLICATED]) for b in small]
    results = []
    for kind in range(4):
        by_name = {n: big[n][kind] for n in sharded_names}
        by_name.update(zip(REPLICATED, small[kind]))
        results.append([by_name[n] for n in WEIGHT_NAMES])
    loss = lax.psum(loss, ("x", "y", "c"))
    return (loss, grad_x[None], *results[0], *results[1], *results[2], *results[3])


def kernel(x, c, ada_mix_w, ada_mix_b, norm_mix_g, w_in, b_gate, ssd_conv_w, ssd_conv_b, ssd_dt_bias, ssd_a_log, ssd_d, ssd_norm_g, w_ssd_out, conf_conv_w, conf_conv_b, conf_ln_g, conf_ln_b, w_conf_out, sc_conv_w, w_sc_out, w_o, ada_ffn_w, ada_ffn_b, norm_ffn_g, w_up, ffn_conv_w, ffn_conv_b, w_down, final_norm_g, loss_target, m_ada_mix_w, m_ada_mix_b, m_norm_mix_g, m_w_in, m_b_gate, m_ssd_conv_w, m_ssd_conv_b, m_ssd_dt_bias, m_ssd_a_log, m_ssd_d, m_ssd_norm_g, m_w_ssd_out, m_conf_conv_w, m_conf_conv_b, m_conf_ln_g, m_conf_ln_b, m_w_conf_out, m_sc_conv_w, m_w_sc_out, m_w_o, m_ada_ffn_w, m_ada_ffn_b, m_norm_ffn_g, m_w_up, m_ffn_conv_w, m_ffn_conv_b, m_w_down, m_final_norm_g, v_ada_mix_w, v_ada_mix_b, v_norm_mix_g, v_w_in, v_b_gate, v_ssd_conv_w, v_ssd_conv_b, v_ssd_dt_bias, v_ssd_a_log, v_ssd_d, v_ssd_norm_g, v_w_ssd_out, v_conf_conv_w, v_conf_conv_b, v_conf_ln_g, v_conf_ln_b, v_w_conf_out, v_sc_conv_w, v_w_sc_out, v_w_o, v_ada_ffn_w, v_ada_ffn_b, v_norm_ffn_g, v_w_up, v_ffn_conv_w, v_ffn_conv_b, v_w_down, v_final_norm_g):
    given = dict(locals())
    weights = {n: given[n] for n in WEIGHT_NAMES}
    moments_m = {n: given["m_" + n] for n in WEIGHT_NAMES}
    moments_v = {n: given["v_" + n] for n in WEIGHT_NAMES}
    return _step(x, c, loss_target, weights, moments_m, moments_v)
```

```python
import functools

import jax
import jax.numpy as jnp
from jax import lax
from jax.experimental import pallas as pl
from jax.experimental.pallas import tpu as pltpu

F32 = jnp.float32
BF16 = jnp.bfloat16
MESH = pl.DeviceIdType.MESH

N_DEV = 8
DEPTH = 2
D_MODEL = 1024
SSD_HEADS = 16
SSD_HEAD_DIM = 64
SSD_INNER = 1024
SSD_STATE = 64
SSD_CHUNK = 128
SSD_XBC = 1280
CONF_WIDTH = 512
CONF_KERNEL = 31
SC_WIDTH = 512
D_FF = 2816
EPS = 1e-6
OFF_Z, OFF_XBC, OFF_DT, OFF_CONF, OFF_SC, N_IN = 1024, 2304, 2320, 3344, 4880, 7952

ADAM_LR, ADAM_B1, ADAM_B2, ADAM_EPS, ADAM_WD, ADAM_STEP = 0.001, 0.9, 0.999, 1e-08, 0.01, 10

LANES = 128
SUBLANES = 8
VMEM_LIMIT = 56 * 1024 * 1024
ROW_TILE = 256

NN = (((1,), (0,)), ((), ()))
NT = (((1,), (1,)), ((), ()))
TN = (((0,), (0,)), ((), ()))


def _params(n_axes):
    return pltpu.CompilerParams(dimension_semantics=("arbitrary",) * n_axes, vmem_limit_bytes=VMEM_LIMIT)


def _pc(body, **kw):
    return pl.pallas_call(body, **kw)


def _dot(a, b, dn=NN, precision=None):
    return lax.dot_general(a, b, dn, precision=precision, preferred_element_type=F32)


def _split3(x):
    hi = x.astype(BF16)
    r1 = x - hi.astype(F32)
    mid = r1.astype(BF16)
    return hi, mid, (r1 - mid.astype(F32)).astype(BF16)


def _dot_sel(x, sel):
    hi, mid, lo = _split3(x)
    return _dot(hi, sel) + _dot(mid, sel) + _dot(lo, sel)


def _sel_dot(sel, x):
    hi, mid, lo = _split3(x)
    return _dot(sel, hi) + _dot(sel, mid) + _dot(sel, lo)


def _sig(x):
    return 1.0 / (1.0 + jnp.exp(-x))


def _fold(v):
    r, c = v.shape
    return v.reshape(r // SUBLANES, SUBLANES, c).sum(axis=0)


def _tile(n_rows, target=ROW_TILE):
    return min(target, n_rows // 2)


STRIP_UNITS = 8
STRIP_ROW_TILE = 512


def _strip_units_per_trip(tl):
    return min(STRIP_UNITS, tl // SUBLANES)


def _bcast_row(ref, k, ls):
    return jnp.broadcast_to(ref[k:k + 1, ls], (SUBLANES, LANES))


def _unit_rows(t, u, nu):
    return pl.ds(pl.multiple_of((t * nu + u) * SUBLANES, SUBLANES), SUBLANES)


def _pair_rows(t, p, nu):
    return pl.ds(pl.multiple_of((t * nu + 2 * p) * SUBLANES, 2 * SUBLANES), 2 * SUBLANES)


def _strip_units(ref, ls, t, nu, nt, halo_prev=None, halo_next=None):
    units = [ref[_unit_rows(t, u, nu), ls] for u in range(nu)]
    if halo_prev is not None:
        before = pl.ds(pl.multiple_of(jnp.maximum(t * nu - 1, 0) * SUBLANES, SUBLANES), SUBLANES)
        units.insert(0, jnp.where(t > 0, ref[before, ls], halo_prev))
    if halo_next is not None:
        after = pl.ds(pl.multiple_of(jnp.minimum((t + 1) * nu, nt * nu - 1) * SUBLANES, SUBLANES), SUBLANES)
        units.append(jnp.where(t < nt - 1, ref[after, ls], halo_next))
    return units


def _row(tl, c, col=0):
    return pl.BlockSpec((tl, c), lambda i, col=col: (i, col))


def _prev(tl, hb, c, col=0):
    r = tl // hb
    return pl.BlockSpec((hb, c), lambda i, col=col: (jnp.maximum(i * r - 1, 0), col))


def _next(tl, hb, c, n_rows, col=0):
    r = tl // hb
    last = n_rows // hb - 1
    return pl.BlockSpec((hb, c), lambda i, col=col: (jnp.minimum((i + 1) * r, last), col))


def _const(shape):
    return pl.BlockSpec(shape, lambda i: (0,) * len(shape))


def _sds(shape, dtype=F32):
    return jax.ShapeDtypeStruct(shape, dtype)


MM_TILE = 1536
MM_FULL_K = 3072
MM_K_TILE = 1024


def _pick(dim, target):
    if dim <= target:
        return dim
    best = None
    for t in range(LANES, target + 1, LANES):
        if dim % t == 0:
            best = t
    assert best is not None, (dim, target)
    return best


def _matmul(a, b, mode, out_dtype, name):
    if mode == "nn":
        (m, k), (k2, n) = a.shape, b.shape
    elif mode == "nt":
        (m, k), (n, k2) = a.shape, b.shape
    else:
        (k, m), (k2, n) = a.shape, b.shape
    assert k == k2, (a.shape, b.shape, mode)
    tm, tn = _pick(m, MM_TILE), _pick(n, MM_TILE)
    tk = k if k <= MM_FULL_K else _pick(k, MM_K_TILE)
    nk = k // tk
    dn = {"nn": NN, "nt": NT, "tn": TN}[mode]

    def body_one(a_ref, b_ref, o_ref):
        o_ref[...] = _dot(a_ref[...].astype(BF16), b_ref[...].astype(BF16), dn).astype(out_dtype)

    def body_acc(a_ref, b_ref, o_ref, acc):
        kk = pl.program_id(2)

        @pl.when(kk == 0)
        def _():
            acc[...] = jnp.zeros_like(acc)

        acc[...] += _dot(a_ref[...].astype(BF16), b_ref[...].astype(BF16), dn)

        @pl.when(kk == nk - 1)
        def _():
            o_ref[...] = acc[...].astype(out_dtype)

    a_spec = {"nn": pl.BlockSpec((tm, tk), lambda i, j, kk: (i, kk)),
              "nt": pl.BlockSpec((tm, tk), lambda i, j, kk: (i, kk)),
              "tn": pl.BlockSpec((tk, tm), lambda i, j, kk: (kk, i))}[mode]
    b_spec = {"nn": pl.BlockSpec((tk, tn), lambda i, j, kk: (kk, j)),
              "nt": pl.BlockSpec((tn, tk), lambda i, j, kk: (j, kk)),
              "tn": pl.BlockSpec((tk, tn), lambda i, j, kk: (kk, j))}[mode]
    o_spec = pl.BlockSpec((tm, tn), lambda i, j, kk: (i, j))
    return _pc(body_one if nk == 1 else body_acc, name=name, out_shape=_sds((m, n), out_dtype),
               grid=(m // tm, n // tn, nk), in_specs=[a_spec, b_spec], out_specs=o_spec,
               scratch_shapes=[] if nk == 1 else [pltpu.VMEM((tm, tn), F32)], compiler_params=_params(3))(a, b)


SUM_NT_TILE = 512


def _matmul_sum_nt(a_list, b_list, name):
    m, n = a_list[0].shape[0], b_list[0].shape[0]
    cnt = len(a_list)
    tm, tn = _pick(m, SUM_NT_TILE), _pick(n, SUM_NT_TILE)

    def body(*refs):
        a_refs, b_refs, o_ref = refs[:cnt], refs[cnt:2 * cnt], refs[2 * cnt]
        acc = _dot(a_refs[0][...].astype(BF16), b_refs[0][...].astype(BF16), NT)
        for t in range(1, cnt):
            acc = acc + _dot(a_refs[t][...].astype(BF16), b_refs[t][...].astype(BF16), NT)
        o_ref[...] = acc

    in_specs = [pl.BlockSpec((tm, a.shape[1]), lambda j, i: (i, 0)) for a in a_list]
    in_specs += [pl.BlockSpec((tn, b.shape[1]), lambda j, i: (j, 0)) for b in b_list]
    return _pc(body, name=name, out_shape=_sds((m, n)), grid=(n // tn, m // tm), in_specs=in_specs,
               out_specs=pl.BlockSpec((tm, tn), lambda j, i: (i, j)), compiler_params=_params(2))(*a_list, *b_list)


def _norm_mod(x, g, scale, shift):
    r = lax.rsqrt(jnp.mean(x * x, axis=-1, keepdims=True) + EPS)
    return ((x * r) * g) * (1.0 + scale) + shift


def _prenorm_first(x, g, scale, shift, name):
    n, d = x.shape
    tl = _tile(n)

    def body(x_ref, g_ref, sc_ref, sh_ref, h_ref):
        h_ref[...] = _norm_mod(x_ref[...], g_ref[...], sc_ref[...], sh_ref[...]).astype(BF16)

    return _pc(body, name=name, out_shape=_sds((n, d), BF16), grid=(n // tl,),
               in_specs=[_row(tl, d)] + [_const((1, d))] * 3, out_specs=_row(tl, d),
               compiler_params=_params(1))(x, g, scale, shift)


def _prenorm_res(x, y, gate, g, scale, shift, name):
    n, d = x.shape
    tl = _tile(n)

    def body(x_ref, y_ref, gate_ref, g_ref, sc_ref, sh_ref, xo_ref, h_ref):
        xn = x_ref[...] + gate_ref[...] * y_ref[...]
        xo_ref[...] = xn
        h_ref[...] = _norm_mod(xn, g_ref[...], sc_ref[...], sh_ref[...]).astype(BF16)

    return _pc(body, name=name, out_shape=(_sds((n, d)), _sds((n, d), BF16)), grid=(n // tl,),
               in_specs=[_row(tl, d), _row(tl, d)] + [_const((1, d))] * 4,
               out_specs=(_row(tl, d), _row(tl, d)), compiler_params=_params(1))(x, y, gate, g, scale, shift)


def _final_loss(x, y, gate, gf, target, name):
    n, d = x.shape
    tl = _tile(n)
    nb = n // tl

    def body(x_ref, y_ref, gate_ref, gf_ref, t_ref, loss_ref, dx_ref, dys_ref, dgate_ref, dgf_ref,
             acc_l, acc_gate, acc_gf):
        i = pl.program_id(0)

        @pl.when(i == 0)
        def _():
            acc_l[...] = jnp.zeros_like(acc_l)
            acc_gate[...] = jnp.zeros_like(acc_gate)
            acc_gf[...] = jnp.zeros_like(acc_gf)

        yv = y_ref[...]
        gate = gate_ref[...]
        gf = gf_ref[...]
        x2 = x_ref[...] + gate * yv
        r = lax.rsqrt(jnp.mean(x2 * x2, axis=-1, keepdims=True) + EPS)
        xn = x2 * r
        e = xn * gf - t_ref[...]
        acc_l[...] += _fold(e * e)
        dy = e * (1.0 / d)
        acc_gf[...] += _fold(dy * xn)
        dxn = dy * gf
        dx = r * (dxn - xn * jnp.mean(dxn * xn, axis=-1, keepdims=True))
        dx_ref[...] = dx
        dys_ref[...] = (dx * gate).astype(BF16)
        acc_gate[...] += _fold(dx * yv)

        @pl.when(i == nb - 1)
        def _():
            loss_ref[...] = jnp.full((SUBLANES, LANES), 0.5 / d, F32) * jnp.sum(acc_l[...])
            dgate_ref[...] = jnp.sum(acc_gate[...], axis=0, keepdims=True)
            dgf_ref[...] = jnp.sum(acc_gf[...], axis=0, keepdims=True)

    return _pc(body, name=name,
               out_shape=(_sds((SUBLANES, LANES)), _sds((n, d)), _sds((n, d), BF16), _sds((1, d)), _sds((1, d))),
               grid=(nb,),
               in_specs=[_row(tl, d), _row(tl, d), _const((1, d)), _const((1, d)), _row(tl, d)],
               out_specs=(_const((SUBLANES, LANES)), _row(tl, d), _row(tl, d), _const((1, d)), _const((1, d))),
               scratch_shapes=[pltpu.VMEM((SUBLANES, d), F32)] * 3,
               compiler_params=_params(1))(x, y, gate, gf, target)


def _norm_bwd(dh, x, dxo, g, scale, name, y_prev=None, gate_prev=None):
    n, d = x.shape
    tl = _tile(n)
    nb = n // tl
    has_prev = y_prev is not None

    def body(*refs):
        if has_prev:
            (dh_ref, x_ref, dxo_ref, g_ref, sc_ref, yp_ref, gp_ref,
             dx_ref, dsh_ref, dsc_ref, dg_ref, dys_ref, dgp_ref, acc_sh, acc_s, acc_gp) = refs
        else:
            (dh_ref, x_ref, dxo_ref, g_ref, sc_ref,
             dx_ref, dsh_ref, dsc_ref, dg_ref, acc_sh, acc_s) = refs
        i = pl.program_id(0)

        @pl.when(i == 0)
        def _():
            acc_sh[...] = jnp.zeros_like(acc_sh)
            acc_s[...] = jnp.zeros_like(acc_s)
            if has_prev:
                acc_gp[...] = jnp.zeros_like(acc_gp)

        x_ = x_ref[...]
        dh_ = dh_ref[...]
        g_ = g_ref[...]
        one_sc = 1.0 + sc_ref[...]
        r = lax.rsqrt(jnp.mean(x_ * x_, axis=-1, keepdims=True) + EPS)
        xn = x_ * r
        dxn = dh_ * (g_ * one_sc)
        dx = dxo_ref[...] + r * (dxn - xn * jnp.mean(dxn * xn, axis=-1, keepdims=True))
        dx_ref[...] = dx
        acc_sh[...] += _fold(dh_)
        acc_s[...] += _fold(dh_ * xn)
        if has_prev:
            dys_ref[...] = (dx * gp_ref[...]).astype(BF16)
            acc_gp[...] += _fold(dx * yp_ref[...])

        @pl.when(i == nb - 1)
        def _():
            s = jnp.sum(acc_s[...], axis=0, keepdims=True)
            dsh_ref[...] = jnp.sum(acc_sh[...], axis=0, keepdims=True)
            dsc_ref[...] = s * g_
            dg_ref[...] = s * one_sc
            if has_prev:
                dgp_ref[...] = jnp.sum(acc_gp[...], axis=0, keepdims=True)

    vec = _sds((1, d))
    in_specs = [_row(tl, d)] * 3 + [_const((1, d))] * 2
    out_shape = [_sds((n, d)), vec, vec, vec]
    out_specs = [_row(tl, d)] + [_const((1, d))] * 3
    scratch = [pltpu.VMEM((SUBLANES, d), F32)] * 2
    args = [dh, x, dxo, g, scale]
    if has_prev:
        in_specs += [_row(tl, d), _const((1, d))]
        out_shape += [_sds((n, d), BF16), vec]
        out_specs += [_row(tl, d), _const((1, d))]
        scratch += [pltpu.VMEM((SUBLANES, d), F32)]
        args += [y_prev, gate_prev]
    return _pc(body, name=name, out_shape=tuple(out_shape), grid=(nb,), in_specs=in_specs,
               out_specs=tuple(out_specs), scratch_shapes=scratch, compiler_params=_params(1))(*args)


CONV_HALO = 8
CONF_HALO = 32


def _ssd_pre(xbc, dt_raw, conv_w, conv_b, dt_bias, name):
    n, c = xbc.shape
    tl = _tile(n, STRIP_ROW_TILE)
    hb = CONV_HALO
    k_taps = 4

    nu = _strip_units_per_trip(tl)
    nt = tl // (nu * SUBLANES)

    def body(x_ref, xp_ref, dt_ref, w_ref, b_ref, dtb_ref, pre_ref, dts_ref):
        i = pl.program_id(0)
        row = lax.broadcasted_iota(jnp.int32, (SUBLANES, LANES), 0)
        for lc in range(c // LANES):
            ls = slice(lc * LANES, (lc + 1) * LANES)
            w = [_bcast_row(w_ref, k, ls) for k in range(k_taps)]
            b = _bcast_row(b_ref, 0, ls)
            xp0 = jnp.where(i > 0, xp_ref[:, ls], 0.0)

            def strip(t, carry):
                xs = _strip_units(x_ref, ls, t, nu, nt, halo_prev=xp0)
                for u in range(nu):
                    prev, cur = xs[u], xs[u + 1]
                    pre_ref[_unit_rows(t, u, nu), ls] = (
                        b + w[0] * _down(prev, cur, 3, row) + w[1] * _down(prev, cur, 2, row)
                        + w[2] * _down(prev, cur, 1, row) + w[3] * cur)
                return carry

            lax.fori_loop(0, nt, strip, 0)
        v = dt_ref[...] + dtb_ref[...]
        dts_ref[...] = jnp.maximum(v, 0.0) + jnp.log1p(jnp.exp(-jnp.abs(v)))

    return _pc(body, name=name, out_shape=(_sds((n, c)), _sds((n, LANES))), grid=(n // tl,),
               in_specs=[_row(tl, c), _prev(tl, hb, c), _row(tl, LANES), _const((SUBLANES, c)), _const((1, c)),
                         _const((1, LANES))],
               out_specs=(_row(tl, c), _row(tl, LANES)), compiler_params=_params(1))(
        xbc, xbc, dt_raw, conv_w, conv_b, dt_bias)


def _ssd_pre_bwd(dpre, xbc, ddt, dt_raw, conv_w, dt_bias, name):
    n, c = xbc.shape
    tl = _tile(n, STRIP_ROW_TILE)
    nb = n // tl
    hb = CONV_HALO
    k_taps = 4

    nu = _strip_units_per_trip(tl)
    nt = tl // (nu * SUBLANES)

    def body(dp_ref, dpn_ref, x_ref, xp_ref, ddt_ref, dt_ref, w_ref, dtb_ref,
             dx_ref, ddr_ref, dw_ref, db_ref, ddtb_ref, acc_w, acc_b, acc_dtb):
        i = pl.program_id(0)

        @pl.when(i == 0)
        def _():
            acc_w[...] = jnp.zeros_like(acc_w)
            acc_b[...] = jnp.zeros_like(acc_b)
            acc_dtb[...] = jnp.zeros_like(acc_dtb)

        row = lax.broadcasted_iota(jnp.int32, (SUBLANES, LANES), 0)
        zero = jnp.zeros((SUBLANES, LANES), F32)
        for lc in range(c // LANES):
            ls = slice(lc * LANES, (lc + 1) * LANES)
            w = [_bcast_row(w_ref, k, ls) for k in range(k_taps)]
            xp0 = jnp.where(i > 0, xp_ref[:, ls], 0.0)
            dpn0 = jnp.where(i < nb - 1, dpn_ref[:, ls], 0.0)

            def strip(t, carry):
                acc = list(carry)
                dps = _strip_units(dp_ref, ls, t, nu, nt, halo_next=dpn0)
                xs = _strip_units(x_ref, ls, t, nu, nt, halo_prev=xp0)
                dxs = []
                for u in range(nu):
                    d, dn = dps[u], dps[u + 1]
                    dxs.append(w[3] * d + w[2] * _up(d, dn, 1, row) + w[1] * _up(d, dn, 2, row)
                               + w[0] * _up(d, dn, 3, row))
                    prev, cur = xs[u], xs[u + 1]
                    acc[3] = acc[3] + d * cur
                    for k in range(3):
                        acc[k] = acc[k] + d * _down(prev, cur, 3 - k, row)
                    acc[4] = acc[4] + d
                for p in range(nu // 2):
                    dx_ref[_pair_rows(t, p, nu), ls] = jnp.concatenate(dxs[2 * p:2 * p + 2], axis=0).astype(BF16)
                return tuple(acc)

            res = lax.fori_loop(0, nt, strip, (zero,) * 5)
            for k in range(k_taps):
                acc_w[k, :, ls] += res[k]
            acc_b[:, ls] += res[4]
        ddr = ddt_ref[...] * _sig(dt_ref[...] + dtb_ref[...])
        ddr_ref[...] = ddr.astype(BF16)
        acc_dtb[...] += _fold(ddr)

        @pl.when(i == nb - 1)
        def _():
            dw_ref[...] = jnp.zeros_like(dw_ref)
            for k in range(k_taps):
                dw_ref[k:k + 1, :] = jnp.sum(acc_w[k], axis=0, keepdims=True)
            db_ref[...] = jnp.sum(acc_b[...], axis=0, keepdims=True)
            ddtb_ref[...] = jnp.sum(acc_dtb[...], axis=0, keepdims=True)

    return _pc(body, name=name,
               out_shape=(_sds((n, c), BF16), _sds((n, LANES), BF16), _sds((SUBLANES, c)), _sds((1, c)),
                          _sds((1, LANES))),
               grid=(nb,),
               in_specs=[_row(tl, c), _next(tl, hb, c, n), _row(tl, c), _prev(tl, hb, c), _row(tl, LANES),
                         _row(tl, LANES), _const((SUBLANES, c)), _const((1, LANES))],
               out_specs=(_row(tl, c), _row(tl, LANES), _const((SUBLANES, c)), _const((1, c)), _const((1, LANES))),
               scratch_shapes=[pltpu.VMEM((k_taps, SUBLANES, c), F32), pltpu.VMEM((SUBLANES, c), F32),
                               pltpu.VMEM((SUBLANES, LANES), F32)],
               compiler_params=_params(1))(dpre, dpre, xbc, xbc, ddt, dt_raw, conv_w, dt_bias)


def _expand_mat():
    r = lax.broadcasted_iota(jnp.int32, (LANES, SSD_INNER), 0)
    c = lax.broadcasted_iota(jnp.int32, (LANES, SSD_INNER), 1)
    return (jnp.right_shift(c, 6) == r).astype(BF16)


def _reduce_mat():
    r = lax.broadcasted_iota(jnp.int32, (SSD_INNER, LANES), 0)
    c = lax.broadcasted_iota(jnp.int32, (SSD_INNER, LANES), 1)
    return (jnp.right_shift(r, 6) == c).astype(BF16)


def _ssd_common(pre, dt, alog):
    q = SSD_CHUNK
    sg = _sig(pre)
    act = pre * sg
    lane = lax.broadcasted_iota(jnp.int32, (1, LANES), 1)
    a_neg = jnp.where(lane < SSD_HEADS, -jnp.exp(alog), 0.0)
    rr = lax.broadcasted_iota(jnp.int32, (q, q), 0)
    cc = lax.broadcasted_iota(jnp.int32, (q, q), 1)
    causal = rr >= cc
    cum = _sel_dot(causal.astype(BF16), dt * a_neg)
    e_mat = _expand_mat()
    dtx = _dot_sel(dt, e_mat)
    cumx = _dot_sel(cum, e_mat)
    return sg, act, a_neg, causal, cum, e_mat, dtx, cumx


def _ssd_scan(pre, dt, alog, dvec, name):
    n = pre.shape[0]
    q = SSD_CHUNK
    nc = n // q

    def body(pre_ref, dt_ref, alog_ref, d_ref, y_ref, hp_ref, state):
        i = pl.program_id(0)

        @pl.when(i == 0)
        def _():
            state[...] = jnp.zeros_like(state)

        dt_ = dt_ref[...]
        _, act, _, causal, cum, e_mat, dtx, cumx = _ssd_common(pre_ref[...], dt_, alog_ref[...])
        xs = act[:, :SSD_INNER]
        bm = act[:, SSD_INNER:SSD_INNER + LANES]
        cm = act[:, SSD_INNER + LANES:]
        cum_t = cum.T
        clx = cumx[q - 1:q, :]
        xc = xs * dtx
        xd = xc * jnp.exp(clx - cumx)
        doutx = jnp.exp(cumx)
        edec = jnp.exp(clx)
        dx_row = _dot_sel(jnp.broadcast_to(d_ref[...], (SUBLANES, LANES)), e_mat)[0:1, :]
        hp_ref[0] = state[...]
        bb = bm.astype(BF16)
        cb = cm.astype(BF16)
        lane = lax.broadcasted_iota(jnp.int32, (1, LANES), 1)
        row = lax.broadcasted_iota(jnp.int32, (LANES, 1), 0)
        cbs = []
        for g in range(2):
            cg = jnp.where(jnp.right_shift(lane, 6) == g, cm, 0.0).astype(BF16)
            cbs.append(_dot(cg, bb, NT))
        for j in range(SSD_HEADS // 2):
            sl = slice(j * LANES, (j + 1) * LANES)
            g = j // 4
            xcj = xc[:, sl].astype(BF16)
            halves = []
            for half in range(2):
                h = 2 * j + half
                seg = cum[:, h:h + 1] - cum_t[h:h + 1, :]
                w = cbs[g] * jnp.exp(jnp.where(causal, seg, -jnp.inf))
                halves.append(_dot(w.astype(BF16), xcj))
            y_diag = jnp.where(lane < SSD_HEAD_DIM, halves[0], halves[1])
            hj = state[:, sl]
            y_off = doutx[:, sl] * _dot(cb, hj.astype(BF16))
            y_ref[:, sl] = y_diag + y_off + xs[:, sl] * dx_row[:, sl]
            st = _dot(bb, xd[:, sl].astype(BF16), TN)
            state[:, sl] = hj * edec[:, sl] + jnp.where(jnp.right_shift(row, 6) == g, st, 0.0)

    return _pc(body, name=name, out_shape=(_sds((n, SSD_INNER)), _sds((nc, LANES, SSD_INNER))), grid=(nc,),
               in_specs=[_row(q, SSD_XBC), _row(q, LANES), _const((1, LANES)), _const((1, LANES))],
               out_specs=(_row(q, SSD_INNER), pl.BlockSpec((1, LANES, SSD_INNER), lambda i: (i, 0, 0))),
               scratch_shapes=[pltpu.VMEM((LANES, SSD_INNER), F32)], compiler_params=_params(1))(pre, dt, alog, dvec)


def _ssd_scan_bwd(pre, dt, hprev, dy, alog, dvec, name):
    n = pre.shape[0]
    q = SSD_CHUNK
    nc = n // q

    def body(pre_ref, dt_ref, hp_ref, dy_ref, alog_ref, d_ref, dpre_ref, ddt_ref, da_ref, dd_ref,
             d_state, dxc_s, dcx_s, dcl_s, acc_a, acc_d):
        i = pl.program_id(0)

        @pl.when(i == 0)
        def _():
            d_state[...] = jnp.zeros_like(d_state)
            acc_a[...] = jnp.zeros_like(acc_a)
            acc_d[...] = jnp.zeros_like(acc_d)

        pre_ = pre_ref[...]
        dt_ = dt_ref[...]
        sg, act, a_neg, causal, cum, e_mat, dtx, cumx = _ssd_common(pre_, dt_, alog_ref[...])
        r_mat = _reduce_mat()
        xs = act[:, :SSD_INNER]
        bm = act[:, SSD_INNER:SSD_INNER + LANES]
        cm = act[:, SSD_INNER + LANES:]
        cum_t = cum.T
        clx = cumx[q - 1:q, :]
        xc = xs * dtx
        dsx = jnp.exp(clx - cumx)
        doutx = jnp.exp(cumx)
        edec = jnp.exp(clx)
        dx_row = _dot_sel(jnp.broadcast_to(d_ref[...], (SUBLANES, LANES)), e_mat)[0:1, :]
        dy_ = dy_ref[...]
        acc_d[...] += _fold(dy_ * xs)
        bb = bm.astype(BF16)
        cb = cm.astype(BF16)
        lane = lax.broadcasted_iota(jnp.int32, (1, LANES), 1)
        row = lax.broadcasted_iota(jnp.int32, (LANES, 1), 0)
        d_c = jnp.zeros((q, LANES), F32)
        d_b = jnp.zeros((q, LANES), F32)

        for j in range(SSD_HEADS // 2):
            sl = slice(j * LANES, (j + 1) * LANES)
            g = j // 4
            hj = hp_ref[0, :, sl]
            hjb = hj.astype(BF16)
            dyj = dy_[:, sl]
            tj = _dot(cb, hjb)
            dtj = (doutx[:, sl] * dyj).astype(BF16)
            dcx = dyj * tj * doutx[:, sl]
            d_c = d_c + _dot(dtj, hjb, NT)
            dhn = d_state[:, sl]
            dhp = dhn * edec[:, sl] + jnp.where(jnp.right_shift(row, 6) == g, _dot(cb, dtj, TN), 0.0)
            dcl = jnp.sum(dhn * hj, axis=0, keepdims=True) * edec[:, sl]
            dsb = dhn.astype(BF16)
            dxd = _dot(bb, dsb)
            xcj = xc[:, sl]
            dsj = dsx[:, sl]
            d_b = d_b + _dot((xcj * dsj).astype(BF16), dsb, NT)
            dds = dxd * xcj * dsj
            d_state[:, sl] = dhp
            dxc_s[:, sl] = dxd * dsj
            dcx_s[:, sl] = dcx - dds
            dcl_s[:, sl] = jnp.broadcast_to(dcl + jnp.sum(dds, axis=0, keepdims=True), (SUBLANES, LANES))

        dcum_c = jnp.zeros((q, LANES), F32)
        dcum_t = jnp.zeros((LANES, q), F32)
        for g in range(2):
            gmask = jnp.right_shift(lane, 6) == g
            cg = jnp.where(gmask, cm, 0.0).astype(BF16)
            cbg = _dot(cg, bb, NT)
            d_cb = jnp.zeros((q, q), F32)
            for hh in range(SSD_HEADS // 2):
                h = g * (SSD_HEADS // 2) + hh
                j, half = h // 2, h % 2
                sl = slice(j * LANES, (j + 1) * LANES)
                hmask = jnp.right_shift(lane, 6) == half
                seg = cum[:, h:h + 1] - cum_t[h:h + 1, :]
                lm = jnp.exp(jnp.where(causal, seg, -jnp.inf))
                w = cbg * lm
                dyj = dy_[:, sl]
                dw = _dot(jnp.where(hmask, dyj, 0.0).astype(BF16), xc[:, sl].astype(BF16), NT)
                dxch = _dot(w.astype(BF16), dyj.astype(BF16), TN)
                dxc_s[:, sl] += jnp.where(hmask, dxch, 0.0)
                d_cb = d_cb + dw * lm
                m = dw * w
                dcum_c = dcum_c + jnp.sum(m, axis=1, keepdims=True) * (lane == h).astype(F32)
                dcum_t = dcum_t + (row == h).astype(F32) * jnp.sum(m, axis=0, keepdims=True)
            d_cbb = d_cb.astype(BF16)
            d_c = d_c + jnp.where(gmask, _dot(d_cbb, bb), 0.0)
            d_b = d_b + jnp.where(gmask, _dot(d_cbb, cb, TN), 0.0)

        dcl_row = _dot_sel(dcl_s[...], r_mat)[0:1, :]
        rowq = lax.broadcasted_iota(jnp.int32, (q, 1), 0)
        dcum = (dcum_c - dcum_t.T + _dot_sel(dcx_s[...], r_mat)
                + jnp.where(rowq == q - 1, dcl_row, 0.0))
        rr = lax.broadcasted_iota(jnp.int32, (q, q), 0)
        cc = lax.broadcasted_iota(jnp.int32, (q, q), 1)
        dadt = _sel_dot((rr <= cc).astype(BF16), dcum)
        dxc = dxc_s[...]
        ddt_ref[...] = dadt * a_neg + _dot_sel(dxc * xs, r_mat)
        acc_a[...] += _fold(dadt * dt_)
        dsilu = sg * (1.0 + pre_ * (1.0 - sg))
        dpre_ref[:, :SSD_INNER] = (dxc * dtx + dy_ * dx_row) * dsilu[:, :SSD_INNER]
        dpre_ref[:, SSD_INNER:SSD_INNER + LANES] = d_b * dsilu[:, SSD_INNER:SSD_INNER + LANES]
        dpre_ref[:, SSD_INNER + LANES:] = d_c * dsilu[:, SSD_INNER + LANES:]

        @pl.when(i == nc - 1)
        def _():
            da_ref[...] = jnp.sum(acc_a[...], axis=0, keepdims=True) * a_neg
            dd_ref[...] = jnp.sum(_dot_sel(acc_d[...], r_mat), axis=0, keepdims=True)

    rev = lambda i: (nc - 1 - i, 0)
    return _pc(body, name=name,
               out_shape=(_sds((n, SSD_XBC)), _sds((n, LANES)), _sds((1, LANES)), _sds((1, LANES))), grid=(nc,),
               in_specs=[pl.BlockSpec((q, SSD_XBC), rev), pl.BlockSpec((q, LANES), rev),
                         pl.BlockSpec((1, LANES, SSD_INNER), lambda i: (nc - 1 - i, 0, 0)),
                         pl.BlockSpec((q, SSD_INNER), rev), _const((1, LANES)), _const((1, LANES))],
               out_specs=(pl.BlockSpec((q, SSD_XBC), rev), pl.BlockSpec((q, LANES), rev), _const((1, LANES)),
                          _const((1, LANES))),
               scratch_shapes=[pltpu.VMEM((LANES, SSD_INNER), F32), pltpu.VMEM((q, SSD_INNER), F32),
                               pltpu.VMEM((q, SSD_INNER), F32), pltpu.VMEM((SUBLANES, SSD_INNER), F32),
                               pltpu.VMEM((SUBLANES, LANES), F32), pltpu.VMEM((SUBLANES, SSD_INNER), F32)],
               compiler_params=_params(1))(pre, dt, hprev, dy, alog, dvec)


def _group_norm_parts(v):
    half = SSD_INNER // 2
    r0 = lax.rsqrt(jnp.mean(v[:, :half] * v[:, :half], axis=-1, keepdims=True) + EPS)
    r1 = lax.rsqrt(jnp.mean(v[:, half:] * v[:, half:], axis=-1, keepdims=True) + EPS)
    lane = lax.broadcasted_iota(jnp.int32, (1, SSD_INNER), 1)
    return jnp.where(lane < half, r0, r1)


def _group_mean(v):
    half = SSD_INNER // 2
    m0 = jnp.mean(v[:, :half], axis=-1, keepdims=True)
    m1 = jnp.mean(v[:, half:], axis=-1, keepdims=True)
    lane = lax.broadcasted_iota(jnp.int32, (1, SSD_INNER), 1)
    return jnp.where(lane < half, m0, m1)


def _ssd_post(y, z, g, name):
    n, d = y.shape
    tl = _tile(n)

    def body(y_ref, z_ref, g_ref, o_ref):
        z_ = z_ref[...]
        v = y_ref[...] * (z_ * _sig(z_))
        o_ref[...] = ((v * _group_norm_parts(v)) * g_ref[...]).astype(BF16)

    return _pc(body, name=name, out_shape=_sds((n, d), BF16), grid=(n // tl,),
               in_specs=[_row(tl, d), _row(tl, d), _const((1, d))], out_specs=_row(tl, d),
               compiler_params=_params(1))(y, z, g)


def _ssd_post_bwd(dout, y, z, g, name):
    n, d = y.shape
    tl = _tile(n)
    nb = n // tl

    def body(do_ref, y_ref, z_ref, g_ref, dy_ref, dz_ref, dg_ref, acc_g):
        i = pl.program_id(0)

        @pl.when(i == 0)
        def _():
            acc_g[...] = jnp.zeros_like(acc_g)

        z_ = z_ref[...]
        y_ = y_ref[...]
        sz = _sig(z_)
        silu_z = z_ * sz
        v = y_ * silu_z
        rs = _group_norm_parts(v)
        nv = v * rs
        do_ = do_ref[...]
        acc_g[...] += _fold(do_ * nv)
        dn = do_ * g_ref[...]
        dv = rs * (dn - nv * _group_mean(dn * nv))
        dy_ref[...] = dv * silu_z
        dz_ref[...] = (dv * y_ * (sz * (1.0 + z_ * (1.0 - sz)))).astype(BF16)

        @pl.when(i == nb - 1)
        def _():
            dg_ref[...] = jnp.sum(acc_g[...], axis=0, keepdims=True)

    return _pc(body, name=name, out_shape=(_sds((n, d)), _sds((n, d), BF16), _sds((1, d))), grid=(nb,),
               in_specs=[_row(tl, d), _row(tl, d), _row(tl, d), _const((1, d))],
               out_specs=(_row(tl, d), _row(tl, d), _const((1, d))),
               scratch_shapes=[pltpu.VMEM((SUBLANES, d), F32)], compiler_params=_params(1))(dout, y, z, g)


def _layer_norm_parts(uc):
    mu = jnp.mean(uc, axis=-1, keepdims=True)
    xc = uc - mu
    rstd = lax.rsqrt(jnp.mean(xc * xc, axis=-1, keepdims=True) + EPS)
    return xc * rstd, rstd


def _rows_x8(w):
    return jnp.broadcast_to(w[:, None, :], (w.shape[0], SUBLANES, w.shape[1]))


def _glu_units(x_ref, ls, gl, t, nu, halo):
    nh = len(halo)
    units = []
    for h in range(nh):
        rows = pl.ds(pl.multiple_of(jnp.maximum(t * nu - nh + h, 0) * SUBLANES, SUBLANES), SUBLANES)
        units.append(jnp.where(t > 0, x_ref[rows, ls] * _sig(x_ref[rows, gl]), halo[h]))
    for u in range(nu):
        rows = _unit_rows(t, u, nu)
        units.append(x_ref[rows, ls] * _sig(x_ref[rows, gl]))
    return units


def _memo_rolls(units):
    memo = {}

    def rolls(key):
        if key not in memo:
            memo[key] = pltpu.roll(units[key[0]], key[1], 0)
        return memo[key]

    return rolls


def _window(units, e, s, rolls, row, up):
    a, b = divmod(s, SUBLANES)
    if b == 0:
        return units[e + a] if up else units[e - a]
    if up:
        sh = SUBLANES - b
        return jnp.where(row < sh, rolls((e + a, sh)), rolls((e + a + 1, sh)))
    return jnp.where(row < b, rolls((e - a - 1, b)), rolls((e - a, b)))


def _conf_fwd(conf_in, conv_w, conv_b, ln_g, ln_b, name):
    n = conf_in.shape[0]
    c = CONF_WIDTH
    tl = _tile(n)
    hb = CONF_HALO
    k_taps = CONF_KERNEL

    nu = _strip_units_per_trip(tl)
    nt = tl // (nu * SUBLANES)
    nh = hb // SUBLANES

    def body(x_ref, xp_ref, w_ref, b_ref, g_ref, beta_ref, o_ref, uc_ref):
        i = pl.program_id(0)
        row = lax.broadcasted_iota(jnp.int32, (SUBLANES, LANES), 0)
        for lc in range(c // LANES):
            ls = slice(lc * LANES, (lc + 1) * LANES)
            gl = slice(c + lc * LANES, c + (lc + 1) * LANES)
            bias = _bcast_row(b_ref, 0, ls)
            halo = [jnp.where(i > 0, xp_ref[SUBLANES * h:SUBLANES * (h + 1), ls]
                              * _sig(xp_ref[SUBLANES * h:SUBLANES * (h + 1), gl]), 0.0) for h in range(nh)]

            def strip(t, carry):
                units = _glu_units(x_ref, ls, gl, t, nu, halo)
                rolls = _memo_rolls(units)
                for u in range(nu):
                    acc = bias
                    for k in range(k_taps):
                        acc = acc + w_ref[k, :, ls] * _window(units, u + nh, k_taps - 1 - k, rolls, row, up=False)
                    uc_ref[_unit_rows(t, u, nu), ls] = acc
                return carry

            lax.fori_loop(0, nt, strip, 0)
        nv, _ = _layer_norm_parts(uc_ref[...])
        v = nv * g_ref[...] + beta_ref[...]
        o_ref[...] = (v * _sig(v)).astype(BF16)

    return _pc(body, name=name, out_shape=(_sds((n, c), BF16), _sds((n, c))), grid=(n // tl,),
               in_specs=[_row(tl, 2 * c), _prev(tl, hb, 2 * c), _const((hb, SUBLANES, c)), _const((1, c)),
                         _const((1, c)), _const((1, c))],
               out_specs=(_row(tl, c), _row(tl, c)), compiler_params=_params(1))(
        conf_in, conf_in, _rows_x8(conv_w), conv_b, ln_g, ln_b)


def _conf_bwd(dout, uc, conf_in, conv_w, ln_g, ln_b, name):
    n = conf_in.shape[0]
    c = CONF_WIDTH
    tl = _tile(n)
    nb = n // tl
    hb = CONF_HALO
    k_taps = CONF_KERNEL

    nu = _strip_units_per_trip(tl)
    nt = tl // (nu * SUBLANES)
    nh = hb // SUBLANES

    def body(do_ref, don_ref, uc_ref, ucn_ref, x_ref, xp_ref, w_ref, g_ref, beta_ref,
             dx_ref, dw_ref, db_ref, dg_ref, dbeta_ref, dbuf, acc_w, acc_b, acc_g, acc_beta):
        i = pl.program_id(0)

        @pl.when(i == 0)
        def _():
            acc_w[...] = jnp.zeros_like(acc_w)
            acc_b[...] = jnp.zeros_like(acc_b)
            acc_g[...] = jnp.zeros_like(acc_g)
            acc_beta[...] = jnp.zeros_like(acc_beta)

        g_ = g_ref[...]
        beta_ = beta_ref[...]

        def d_conv_out(do_, uc_):
            nv, rstd = _layer_norm_parts(uc_)
            v = nv * g_ + beta_
            sv = _sig(v)
            dv = do_ * (sv * (1.0 + v * (1.0 - sv)))
            dn = dv * g_
            duc = rstd * (dn - jnp.mean(dn, axis=-1, keepdims=True)
                          - nv * jnp.mean(dn * nv, axis=-1, keepdims=True))
            return duc, dv, nv

        duc, dv, nv = d_conv_out(do_ref[...], uc_ref[...])
        acc_g[...] += _fold(dv * nv)
        acc_beta[...] += _fold(dv)
        acc_b[...] += _fold(duc)
        dbuf[pl.ds(0, tl), :] = duc
        ducn, _, _ = d_conv_out(don_ref[...], ucn_ref[...])
        dbuf[pl.ds(tl, hb), :] = jnp.where(i < nb - 1, ducn, 0.0)

        row = lax.broadcasted_iota(jnp.int32, (SUBLANES, LANES), 0)
        for lc in range(c // LANES):
            ls = slice(lc * LANES, (lc + 1) * LANES)
            gl = slice(c + lc * LANES, c + (lc + 1) * LANES)
            halo = [jnp.where(i > 0, xp_ref[SUBLANES * h:SUBLANES * (h + 1), ls]
                              * _sig(xp_ref[SUBLANES * h:SUBLANES * (h + 1), gl]), 0.0) for h in range(nh)]

            def strip(t, carry):
                us = _glu_units(x_ref, ls, gl, t, nu, halo)
                ds = [dbuf[pl.ds(pl.multiple_of((t * nu + u) * SUBLANES, SUBLANES), SUBLANES), ls]
                      for u in range(nu + nh)]
                u_rolls, d_rolls = _memo_rolls(us), _memo_rolls(ds)
                for k in range(k_taps):
                    part = ds[0] * _window(us, nh, k_taps - 1 - k, u_rolls, row, up=False)
                    for u in range(1, nu):
                        part = part + ds[u] * _window(us, u + nh, k_taps - 1 - k, u_rolls, row, up=False)
                    acc_w[k, :, ls] += part
                dus = []
                for u in range(nu):
                    du = w_ref[0, :, ls] * _window(ds, u, k_taps - 1, d_rolls, row, up=True)
                    for k in range(1, k_taps):
                        du = du + w_ref[k, :, ls] * _window(ds, u, k_taps - 1 - k, d_rolls, row, up=True)
                    dus.append(du)
                for p in range(nu // 2):
                    rows = _pair_rows(t, p, nu)
                    du2 = jnp.concatenate(dus[2 * p:2 * p + 2], axis=0)
                    val, sgate = x_ref[rows, ls], _sig(x_ref[rows, gl])
                    dx_ref[rows, ls] = (du2 * sgate).astype(BF16)
                    dx_ref[rows, gl] = (du2 * val * sgate * (1.0 - sgate)).astype(BF16)
                return carry

            lax.fori_loop(0, nt, strip, 0)

        @pl.when(i == nb - 1)
        def _():
            dw_ref[...] = jnp.zeros_like(dw_ref)
            for k in range(k_taps):
                dw_ref[k:k + 1, :] = jnp.sum(acc_w[k], axis=0, keepdims=True)
            db_ref[...] = jnp.sum(acc_b[...], axis=0, keepdims=True)
            dg_ref[...] = jnp.sum(acc_g[...], axis=0, keepdims=True)
            dbeta_ref[...] = jnp.sum(acc_beta[...], axis=0, keepdims=True)

    vec = _sds((1, c))
    return _pc(body, name=name, out_shape=(_sds((n, 2 * c), BF16), _sds((hb, c)), vec, vec, vec), grid=(nb,),
               in_specs=[_row(tl, c), _next(tl, hb, c, n), _row(tl, c), _next(tl, hb, c, n), _row(tl, 2 * c),
                         _prev(tl, hb, 2 * c), _const((hb, SUBLANES, c)), _const((1, c)), _const((1, c))],
               out_specs=(_row(tl, 2 * c), _const((hb, c)), _const((1, c)), _const((1, c)), _const((1, c))),
               scratch_shapes=[pltpu.VMEM((tl + hb, c), F32),
                               pltpu.VMEM((k_taps, SUBLANES, c), F32), pltpu.VMEM((SUBLANES, c), F32),
                               pltpu.VMEM((SUBLANES, c), F32), pltpu.VMEM((SUBLANES, c), F32)],
               compiler_params=_params(1))(dout, dout, uc, uc, conf_in, conf_in, _rows_x8(conv_w), ln_g, ln_b)


def _sc_fwd(sc_in, conv_w, name):
    n = sc_in.shape[0]
    c = SC_WIDTH
    tl = _tile(n, STRIP_ROW_TILE)
    hb = CONV_HALO

    nu = _strip_units_per_trip(tl)
    nt = tl // (nu * SUBLANES)

    def body(x_ref, xp_ref, w_ref, o_ref):
        i = pl.program_id(0)
        row = lax.broadcasted_iota(jnp.int32, (SUBLANES, LANES), 0)
        for lc in range(c // LANES):
            ls = slice(lc * LANES, (lc + 1) * LANES)
            gc_ls = slice(c + lc * LANES, c + (lc + 1) * LANES)
            xv_ls = slice(2 * c + lc * LANES, 2 * c + (lc + 1) * LANES)
            w = [_bcast_row(w_ref, k, ls) for k in range(3)]
            gc0 = jnp.where(i > 0, xp_ref[:, gc_ls], 0.0)
            xv0 = xp_ref[:, xv_ls]

            def strip(t, carry):
                gcs = _strip_units(x_ref, gc_ls, t, nu, nt, halo_prev=gc0)
                xvs = _strip_units(x_ref, xv_ls, t, nu, nt, halo_prev=xv0)
                ps = [a * b for a, b in zip(gcs, xvs)]
                outs = []
                for u in range(nu):
                    prev, cur = ps[u], ps[u + 1]
                    cv = w[0] * _down(prev, cur, 2, row) + w[1] * _down(prev, cur, 1, row) + w[2] * cur
                    outs.append(x_ref[_unit_rows(t, u, nu), ls] * cv)
                for p in range(nu // 2):
                    o_ref[_pair_rows(t, p, nu), ls] = jnp.concatenate(outs[2 * p:2 * p + 2], axis=0).astype(BF16)
                return carry

            lax.fori_loop(0, nt, strip, 0)

    return _pc(body, name=name, out_shape=_sds((n, c), BF16), grid=(n // tl,),
               in_specs=[_row(tl, 3 * c), _prev(tl, hb, 3 * c), _const((SUBLANES, c))], out_specs=_row(tl, c),
               compiler_params=_params(1))(sc_in, sc_in, conv_w)


def _sc_bwd(dout, sc_in, conv_w, name):
    n = sc_in.shape[0]
    c = SC_WIDTH
    tl = _tile(n, STRIP_ROW_TILE)
    nb = n // tl
    hb = CONV_HALO

    nu = _strip_units_per_trip(tl)
    nt = tl // (nu * SUBLANES)

    def body(do_ref, don_ref, x_ref, xp_ref, xn_ref, w_ref, dx_ref, dw_ref, acc_w):
        i = pl.program_id(0)

        @pl.when(i == 0)
        def _():
            acc_w[...] = jnp.zeros_like(acc_w)

        row = lax.broadcasted_iota(jnp.int32, (SUBLANES, LANES), 0)
        zero = jnp.zeros((SUBLANES, LANES), F32)
        for lc in range(c // LANES):
            ls = slice(lc * LANES, (lc + 1) * LANES)
            gc_ls = slice(c + lc * LANES, c + (lc + 1) * LANES)
            xv_ls = slice(2 * c + lc * LANES, 2 * c + (lc + 1) * LANES)
            w = [_bcast_row(w_ref, k, ls) for k in range(3)]
            gc0 = jnp.where(i > 0, xp_ref[:, gc_ls], 0.0)
            xv0 = xp_ref[:, xv_ls]
            don0 = jnp.where(i < nb - 1, don_ref[:, ls], 0.0)
            gbn0 = xn_ref[:, ls]

            def strip(t, carry):
                acc = list(carry)
                dos = _strip_units(do_ref, ls, t, nu, nt, halo_next=don0)
                gbs = _strip_units(x_ref, ls, t, nu, nt, halo_next=gbn0)
                gcs = _strip_units(x_ref, gc_ls, t, nu, nt, halo_prev=gc0)
                xvs = _strip_units(x_ref, xv_ls, t, nu, nt, halo_prev=xv0)
                dcv = [a * b for a, b in zip(dos, gbs)]
                ps = [a * b for a, b in zip(gcs, xvs)]
                d_gb, d_gc, d_xv = [], [], []
                for u in range(nu):
                    prev, cur = ps[u], ps[u + 1]
                    p1, p2 = _down(prev, cur, 1, row), _down(prev, cur, 2, row)
                    d, dn = dcv[u], dcv[u + 1]
                    dp = w[2] * d + w[1] * _up(d, dn, 1, row) + w[0] * _up(d, dn, 2, row)
                    d_gb.append(dos[u] * (w[0] * p2 + w[1] * p1 + w[2] * cur))
                    d_gc.append(dp * xvs[u + 1])
                    d_xv.append(dp * gcs[u + 1])
                    acc[0] = acc[0] + d * p2
                    acc[1] = acc[1] + d * p1
                    acc[2] = acc[2] + d * cur
                for p in range(nu // 2):
                    rows = _pair_rows(t, p, nu)
                    for vals, lanes in ((d_gb, ls), (d_gc, gc_ls), (d_xv, xv_ls)):
                        dx_ref[rows, lanes] = jnp.concatenate(vals[2 * p:2 * p + 2], axis=0).astype(BF16)
                return tuple(acc)

            res = lax.fori_loop(0, nt, strip, (zero,) * 3)
            for k in range(3):
                acc_w[k, :, ls] += res[k]

        @pl.when(i == nb - 1)
        def _():
            dw_ref[...] = jnp.zeros_like(dw_ref)
            for k in range(3):
                dw_ref[k:k + 1, :] = jnp.sum(acc_w[k], axis=0, keepdims=True)

    return _pc(body, name=name, out_shape=(_sds((n, 3 * c), BF16), _sds((SUBLANES, c))), grid=(nb,),
               in_specs=[_row(tl, c), _next(tl, hb, c, n), _row(tl, 3 * c), _prev(tl, hb, 3 * c),
                         _next(tl, hb, c, n), _const((SUBLANES, c))],
               out_specs=(_row(tl, 3 * c), _const((SUBLANES, c))),
               scratch_shapes=[pltpu.VMEM((3, SUBLANES, c), F32)],
               compiler_params=_params(1))(dout, dout, sc_in, sc_in, sc_in, conv_w)


def _merge_fwd(gates, ya, yb, yc, b_gate, name):
    n, d = ya.shape
    tl = _tile(n)

    def body(gt_ref, ya_ref, yb_ref, yc_ref, b_ref, o_ref):
        gt = _sig(gt_ref[...] + b_ref[...])
        o_ref[...] = (gt[:, :d] * ya_ref[...] + gt[:, d:2 * d] * yb_ref[...] + gt[:, 2 * d:] * yc_ref[...]).astype(BF16)

    return _pc(body, name=name, out_shape=_sds((n, d), BF16), grid=(n // tl,),
               in_specs=[_row(tl, 3 * d), _row(tl, d), _row(tl, d), _row(tl, d), _const((1, 3 * d))],
               out_specs=_row(tl, d), compiler_params=_params(1))(gates, ya, yb, yc, b_gate)


def _merge_bwd(dm, gates, ya, yb, yc, b_gate, name):
    n, d = ya.shape
    tl = _tile(n)
    nb = n // tl

    def body(dm_ref, gt_ref, ya_ref, yb_ref, yc_ref, b_ref, dya_ref, dyb_ref, dyc_ref, dgt_ref, db_ref, acc):
        i = pl.program_id(0)

        @pl.when(i == 0)
        def _():
            acc[...] = jnp.zeros_like(acc)

        dm_ = dm_ref[...]
        gt = _sig(gt_ref[...] + b_ref[...])
        for idx, (y_ref, dy_ref) in enumerate(((ya_ref, dya_ref), (yb_ref, dyb_ref), (yc_ref, dyc_ref))):
            gk = gt[:, idx * d:(idx + 1) * d]
            dy_ref[...] = (dm_ * gk).astype(BF16)
            dpre = dm_ * y_ref[...] * gk * (1.0 - gk)
            dgt_ref[:, idx * d:(idx + 1) * d] = dpre.astype(BF16)
            acc[:, idx * d:(idx + 1) * d] += _fold(dpre)

        @pl.when(i == nb - 1)
        def _():
            db_ref[...] = jnp.sum(acc[...], axis=0, keepdims=True)

    bf = _sds((n, d), BF16)
    return _pc(body, name=name, out_shape=(bf, bf, bf, _sds((n, 3 * d), BF16), _sds((1, 3 * d))), grid=(nb,),
               in_specs=[_row(tl, d), _row(tl, 3 * d), _row(tl, d), _row(tl, d), _row(tl, d), _const((1, 3 * d))],
               out_specs=(_row(tl, d), _row(tl, d), _row(tl, d), _row(tl, 3 * d), _const((1, 3 * d))),
               scratch_shapes=[pltpu.VMEM((SUBLANES, 3 * d), F32)], compiler_params=_params(1))(
        dm, gates, ya, yb, yc, b_gate)


FFN_COLS = 1408
FFN_STRIP = 64
FFN_STRIP_BWD = 32


def _down(prev, cur, s, row):
    return jnp.where(row < s, pltpu.roll(prev, s, 0), pltpu.roll(cur, s, 0))


def _up(cur, nxt, s, row):
    return jnp.where(row < SUBLANES - s, pltpu.roll(cur, SUBLANES - s, 0), pltpu.roll(nxt, SUBLANES - s, 0))


def _ffn_mid(up, conv_w, conv_b, name):
    n = up.shape[0]
    tl = _tile(n, STRIP_ROW_TILE)
    hb = CONV_HALO
    tc = FFN_COLS
    ncb = D_FF // tc

    def spec(shape_rows, idx_fn, off):
        return pl.BlockSpec((shape_rows, tc), lambda j, i, off=off: (idx_fn(i), j + off))

    r = tl // hb
    cur = lambda i: i
    prv = lambda i: jnp.maximum(i * r - 1, 0)

    def body(g_ref, gp_ref, v_ref, vp_ref, wg_ref, wv_ref, bg_ref, bv_ref, o_ref):
        i = pl.program_id(1)
        row = lax.broadcasted_iota(jnp.int32, (SUBLANES, LANES), 0)
        full = lambda ref, k, ls: jnp.broadcast_to(ref[k:k + 1, ls], (SUBLANES, LANES))
        for lc in range(tc // LANES):
            ls = slice(lc * LANES, (lc + 1) * LANES)
            wg = [full(wg_ref, k, ls) for k in range(3)]
            wv = [full(wv_ref, k, ls) for k in range(3)]
            bg, bv = full(bg_ref, 0, ls), full(bv_ref, 0, ls)

            def conv(prev, x, w, b):
                return b + w[0] * _down(prev, x, 2, row) + w[1] * _down(prev, x, 1, row) + w[2] * x

            def strip(t, carry):
                gs, vs = [carry[0]], [carry[1]]
                for u in range(FFN_STRIP // SUBLANES):
                    rows = pl.ds(pl.multiple_of(t * FFN_STRIP + u * SUBLANES, SUBLANES), SUBLANES)
                    gs.append(g_ref[rows, ls])
                    vs.append(v_ref[rows, ls])
                outs = []
                for u in range(FFN_STRIP // SUBLANES):
                    ug = conv(gs[u], gs[u + 1], wg, bg)
                    outs.append(ug * _sig(ug) * conv(vs[u], vs[u + 1], wv, bv))
                for p in range(FFN_STRIP // 16):
                    rows = pl.ds(pl.multiple_of(t * FFN_STRIP + p * 16, 16), 16)
                    o_ref[rows, ls] = jnp.concatenate(outs[2 * p:2 * p + 2], axis=0).astype(BF16)
                return gs[-1], vs[-1]

            lax.fori_loop(0, tl // FFN_STRIP, strip,
                          (jnp.where(i > 0, gp_ref[:, ls], 0.0), jnp.where(i > 0, vp_ref[:, ls], 0.0)))

    wspec = lambda off: pl.BlockSpec((SUBLANES, tc), lambda j, i, off=off: (0, j + off))
    bspec = lambda off: pl.BlockSpec((1, tc), lambda j, i, off=off: (0, j + off))
    return _pc(body, name=name, out_shape=_sds((n, D_FF), BF16), grid=(ncb, n // tl),
               in_specs=[spec(tl, cur, 0), spec(hb, prv, 0), spec(tl, cur, ncb), spec(hb, prv, ncb),
                         wspec(0), wspec(ncb), bspec(0), bspec(ncb)],
               out_specs=pl.BlockSpec((tl, tc), lambda j, i: (i, j)), compiler_params=_params(2))(
        up, up, up, up, conv_w, conv_w, conv_b, conv_b)


def _ffn_mid_bwd(da, up, conv_w, conv_b, name):
    n = up.shape[0]
    tl = _tile(n, STRIP_ROW_TILE)
    nb = n // tl
    nt = tl // FFN_STRIP_BWD
    hb = CONV_HALO
    tc = FFN_COLS
    ncb = D_FF // tc
    r = tl // hb
    last = n // hb - 1
    cur = lambda i: i
    prv = lambda i: jnp.maximum(i * r - 1, 0)
    nxt = lambda i: jnp.minimum((i + 1) * r, last)

    def spec(shape_rows, idx_fn, off):
        return pl.BlockSpec((shape_rows, tc), lambda j, i, off=off: (idx_fn(i), j + off))

    def body(da_ref, dan_ref, g_ref, gp_ref, gn_ref, v_ref, vp_ref, vn_ref, wg_ref, wv_ref, bg_ref, bv_ref,
             dg_ref, dv_ref, dwg_ref, dwv_ref, dbg_ref, dbv_ref, acc_w, acc_b):
        i = pl.program_id(1)

        @pl.when(i == 0)
        def _():
            acc_w[...] = jnp.zeros_like(acc_w)
            acc_b[...] = jnp.zeros_like(acc_b)

        row = lax.broadcasted_iota(jnp.int32, (SUBLANES, LANES), 0)
        full = lambda ref, k, ls: jnp.broadcast_to(ref[k:k + 1, ls], (SUBLANES, LANES))
        zero = jnp.zeros((SUBLANES, LANES), F32)

        def d_conv_out(da_, ug, uv):
            s = _sig(ug)
            return da_ * uv * (s * (1.0 + ug * (1.0 - s))), da_ * (ug * s)

        for lc in range(tc // LANES):
            ls = slice(lc * LANES, (lc + 1) * LANES)
            wg = [full(wg_ref, k, ls) for k in range(3)]
            wv = [full(wv_ref, k, ls) for k in range(3)]
            bg, bv = full(bg_ref, 0, ls), full(bv_ref, 0, ls)

            def unit(prev_g, g, prev_v, v, da_):
                g1, g2 = _down(prev_g, g, 1, row), _down(prev_g, g, 2, row)
                v1, v2 = _down(prev_v, v, 1, row), _down(prev_v, v, 2, row)
                ug = bg + wg[0] * g2 + wg[1] * g1 + wg[2] * g
                uv = bv + wv[0] * v2 + wv[1] * v1 + wv[2] * v
                dug, duv = d_conv_out(da_, ug, uv)
                return dug, duv, (g2, g1, g), (v2, v1, v)

            def d_in(d, d_next, w):
                return w[2] * d + w[1] * _up(d, d_next, 1, row) + w[0] * _up(d, d_next, 2, row)

            tail = pl.ds(tl - SUBLANES, SUBLANES)
            dgn, dvn, _, _ = unit(g_ref[tail, ls], gn_ref[:, ls], v_ref[tail, ls], vn_ref[:, ls], dan_ref[:, ls])
            dgn = jnp.where(i < nb - 1, dgn, 0.0)
            dvn = jnp.where(i < nb - 1, dvn, 0.0)
            gp0 = jnp.where(i > 0, gp_ref[:, ls], 0.0)
            vp0 = jnp.where(i > 0, vp_ref[:, ls], 0.0)

            def strip(tt, carry):
                dgn, dvn = carry[0], carry[1]
                aw, ab = list(carry[2:8]), list(carry[8:10])
                t = nt - 1 - tt
                nu = FFN_STRIP_BWD // SUBLANES
                rm = pl.multiple_of(jnp.maximum(t * FFN_STRIP_BWD - SUBLANES, 0), SUBLANES)
                gs = [jnp.where(t > 0, g_ref[pl.ds(rm, SUBLANES), ls], gp0)]
                vs = [jnp.where(t > 0, v_ref[pl.ds(rm, SUBLANES), ls], vp0)]
                das = []
                for u in range(nu):
                    rows = pl.ds(pl.multiple_of(t * FFN_STRIP_BWD + u * SUBLANES, SUBLANES), SUBLANES)
                    gs.append(g_ref[rows, ls])
                    vs.append(v_ref[rows, ls])
                    das.append(da_ref[rows, ls])
                dgs, dvs = [None] * nu + [dgn], [None] * nu + [dvn]
                for u in reversed(range(nu)):
                    dgs[u], dvs[u], gsh, vsh = unit(gs[u], gs[u + 1], vs[u], vs[u + 1], das[u])
                    for k in range(3):
                        aw[k] = aw[k] + dgs[u] * gsh[k]
                        aw[3 + k] = aw[3 + k] + dvs[u] * vsh[k]
                    ab[0] = ab[0] + dgs[u]
                    ab[1] = ab[1] + dvs[u]
                for p in range(nu // 2):
                    rows = pl.ds(pl.multiple_of(t * FFN_STRIP_BWD + p * 16, 16), 16)
                    dg_ref[rows, ls] = jnp.concatenate([d_in(dgs[2 * p], dgs[2 * p + 1], wg),
                                                        d_in(dgs[2 * p + 1], dgs[2 * p + 2], wg)], axis=0).astype(BF16)
                    dv_ref[rows, ls] = jnp.concatenate([d_in(dvs[2 * p], dvs[2 * p + 1], wv),
                                                        d_in(dvs[2 * p + 1], dvs[2 * p + 2], wv)], axis=0).astype(BF16)
                return (dgs[0], dvs[0], *aw, *ab)

            res = lax.fori_loop(0, nt, strip, (dgn, dvn) + (zero,) * 8)
            for k in range(3):
                acc_w[0, k, :, ls] += res[2 + k]
                acc_w[1, k, :, ls] += res[5 + k]
            acc_b[0, :, ls] += res[8]
            acc_b[1, :, ls] += res[9]

        @pl.when(i == nb - 1)
        def _():
            for t, (dw_ref, db_ref) in enumerate(((dwg_ref, dbg_ref), (dwv_ref, dbv_ref))):
                dw_ref[...] = jnp.zeros_like(dw_ref)
                for k in range(3):
                    dw_ref[k:k + 1, :] = jnp.sum(acc_w[t, k], axis=0, keepdims=True)
                db_ref[...] = jnp.sum(acc_b[t], axis=0, keepdims=True)

    wspec = lambda off: pl.BlockSpec((SUBLANES, tc), lambda j, i, off=off: (0, j + off))
    bspec = lambda off: pl.BlockSpec((1, tc), lambda j, i, off=off: (0, j + off))
    ospec = lambda off: pl.BlockSpec((tl, tc), lambda j, i, off=off: (i, j + off))
    dg, dv, dwg, dwv, dbg, dbv = _pc(
        body, name=name,
        out_shape=(_sds((n, D_FF), BF16), _sds((n, D_FF), BF16), _sds((SUBLANES, D_FF)), _sds((SUBLANES, D_FF)),
                   _sds((1, D_FF)), _sds((1, D_FF))),
        grid=(ncb, nb),
        in_specs=[spec(tl, cur, 0), spec(hb, nxt, 0),
                  spec(tl, cur, 0), spec(hb, prv, 0), spec(hb, nxt, 0),
                  spec(tl, cur, ncb), spec(hb, prv, ncb), spec(hb, nxt, ncb),
                  wspec(0), wspec(ncb), bspec(0), bspec(ncb)],
        out_specs=(ospec(0), ospec(0), wspec(0), wspec(0), bspec(0), bspec(0)),
        scratch_shapes=[pltpu.VMEM((2, 3, SUBLANES, tc), F32), pltpu.VMEM((2, SUBLANES, tc), F32)],
        compiler_params=_params(2))(da, da, up, up, up, up, up, up, conv_w, conv_w, conv_b, conv_b)
    return dg, dv, jnp.concatenate([dwg, dwv], axis=1), jnp.concatenate([dbg, dbv], axis=1)


def _position():
    return lax.axis_index("x"), lax.axis_index("y"), lax.axis_index("c")


def _all_gather(locals_, name):
    n = len(locals_)

    def body(*refs):
        x_refs, out_refs = refs[:n], refs[n:2 * n]
        send_sems, recv_sems, local_sems = refs[2 * n:]
        x, y, cc = _position()
        me, sibling = (x, y, cc), (x, y, 1 - cc)
        chips = [(1 - x, y), (x, 1 - y), (1 - x, 1 - y)]

        def slot(a, px, py, pc):
            return out_refs[a].at[4 * px + 2 * py + pc]

        def copy(k, a, block, to, own=False):
            return pltpu.make_async_remote_copy(
                src_ref=x_refs[a] if own else slot(a, *block), dst_ref=slot(a, *block),
                send_sem=send_sems.at[k, a], recv_sem=recv_sems.at[k, a], device_id=to, device_id_type=MESH)

        mine = [pltpu.make_async_copy(x_refs[a], slot(a, *me), local_sems.at[a]) for a in range(n)]
        first = [copy(1 + j, a, me, (*chip, cc), own=True) for j, chip in enumerate(chips) for a in range(n)]
        first += [copy(0, a, me, sibling, own=True) for a in range(n)]
        for cp in mine + first:
            cp.start()
        passed = []
        for j, chip in enumerate(chips):
            for a in range(n):
                copy(1 + j, a, (*chip, cc), me).wait_recv()
                cp = copy(4 + j, a, (*chip, cc), sibling)
                cp.start()
                passed.append(cp)
        for a in range(n):
            copy(0, a, sibling, me).wait_recv()
        for j, chip in enumerate(chips):
            for a in range(n):
                copy(4 + j, a, (*chip, 1 - cc), me).wait_recv()
        for cp in first + passed:
            cp.wait_send()
        for cp in mine:
            cp.wait()

    hbm = pl.BlockSpec(memory_space=pl.ANY)
    return _pc(body, name=name, out_shape=[_sds((N_DEV,) + a.shape, a.dtype) for a in locals_],
               in_specs=[hbm] * n, out_specs=[hbm] * n,
               scratch_shapes=[pltpu.SemaphoreType.DMA((7, n)), pltpu.SemaphoreType.DMA((7, n)),
                               pltpu.SemaphoreType.DMA((n,))])(*locals_)


def _peers():
    x, y, cc = _position()
    others = []
    for fx, fy, fc in ((0, 0, 1), (1, 0, 0), (0, 1, 0), (1, 1, 0), (1, 0, 1), (0, 1, 1), (1, 1, 1)):
        p = (1 - x if fx else x, 1 - y if fy else y, 1 - cc if fc else cc)
        others.append((p, 4 * p[0] + 2 * p[1] + p[2]))
    return 4 * x + 2 * y + cc, others


def _gather_start(locals_, after, name):
    n = len(locals_)
    me, _ = _peers()
    lands = [lax.dynamic_update_slice(lax.empty((N_DEV,) + a.shape, a.dtype), a[None], (me,) + (0,) * a.ndim)
             for a in locals_]

    def body(*refs):
        x_refs, land_refs = refs[:n], refs[n:2 * n]
        send_sems, recv_sems, token = refs[2 * n + 1], refs[2 * n + 2], refs[-1]
        me_idx, others = _peers()
        for k, (peer, _) in enumerate(others):
            for a in range(n):
                pltpu.make_async_remote_copy(
                    src_ref=x_refs[a], dst_ref=land_refs[a].at[me_idx], send_sem=send_sems.at[k * n + a],
                    recv_sem=recv_sems.at[k * n + a], device_id=peer, device_id_type=MESH).start()
        token[...] = jnp.zeros_like(token)

    hbm = pl.BlockSpec(memory_space=pltpu.HBM)
    sem = pl.BlockSpec(memory_space=pltpu.SEMAPHORE)
    out = _pc(body, name=name,
              out_shape=(pltpu.SemaphoreType.DMA((7 * n,)), pltpu.SemaphoreType.DMA((7 * n,)),
                         *[pltpu.HBM(a.shape, a.dtype) for a in locals_], *[pltpu.HBM(l.shape, l.dtype) for l in lands],
                         _sds((SUBLANES, LANES))),
              in_specs=[hbm] * (2 * n) + [pl.BlockSpec(memory_space=pl.ANY)],
              out_specs=(sem, sem, *([hbm] * (2 * n)), pl.BlockSpec(memory_space=pltpu.VMEM)),
              input_output_aliases={i: 2 + i for i in range(2 * n)},
              compiler_params=pltpu.CompilerParams(has_side_effects=pltpu.SideEffectType.DATAFLOW_SIDE_EFFECTING))(
        *[pltpu.with_memory_space_constraint(a, pltpu.HBM) for a in locals_],
        *[pltpu.with_memory_space_constraint(l, pltpu.HBM) for l in lands], after)
    return (out[0], out[1], list(out[2:2 + n]), list(out[2 + n:2 + 2 * n])), out[-1]


def _gather_wait(state, after, name):
    send_sems, recv_sems, x_thru, land_thru = state
    n = len(x_thru)

    def body(*refs):
        x_refs, land_refs = refs[:n], refs[n:2 * n]
        send_sems, recv_sems = refs[2 * n], refs[2 * n + 1]
        _, others = _peers()
        for k, (peer, peer_idx) in enumerate(others):
            for a in range(n):
                cp = pltpu.make_async_remote_copy(
                    src_ref=x_refs[a], dst_ref=land_refs[a].at[peer_idx], send_sem=send_sems.at[k * n + a],
                    recv_sem=recv_sems.at[k * n + a], device_id=peer, device_id_type=MESH)
                cp.wait_send()
                cp.wait_recv()

    hbm = pl.BlockSpec(memory_space=pltpu.HBM)
    sem = pl.BlockSpec(memory_space=pltpu.SEMAPHORE)
    out = _pc(body, name=name, out_shape=tuple(pltpu.HBM(a.shape, a.dtype) for a in x_thru + land_thru),
              in_specs=[hbm] * (2 * n) + [sem, sem, pl.BlockSpec(memory_space=pl.ANY)], out_specs=tuple([hbm] * (2 * n)),
              input_output_aliases={i: i for i in range(2 * n)},
              compiler_params=pltpu.CompilerParams(has_side_effects=pltpu.SideEffectType.DATAFLOW_SIDE_EFFECTING))(
        *x_thru, *land_thru, send_sems, recv_sems, after)
    return list(out[n:])


N_CHIPS = 4


def _sibling_swap(parts, name):
    n = len(parts)

    def body(*refs):
        g_refs, got_refs = refs[:n], refs[n:2 * n]
        send_sems, recv_sems = refs[2 * n:]
        x, y, cc = _position()
        swaps = []
        for q in range(N_CHIPS):
            for a in range(n):
                swaps.append(pltpu.make_async_remote_copy(
                    src_ref=g_refs[a].at[2 * q + 1 - cc], dst_ref=got_refs[a].at[q], send_sem=send_sems.at[q, a],
                    recv_sem=recv_sems.at[q, a], device_id=(x, y, 1 - cc), device_id_type=MESH))
        for cp in swaps:
            cp.start()
        for cp in swaps:
            cp.wait_recv()
        for cp in swaps:
            cp.wait_send()

    hbm = pl.BlockSpec(memory_space=pl.ANY)
    return _pc(body, name=name, out_shape=[_sds((N_CHIPS,) + a.shape[1:], a.dtype) for a in parts],
               in_specs=[hbm] * n, out_specs=[hbm] * n,
               scratch_shapes=[pltpu.SemaphoreType.DMA((N_CHIPS, n)), pltpu.SemaphoreType.DMA((N_CHIPS, n))])(*parts)


def _pair_add(part, got, core, name):
    q, a, b = got.shape
    ta = _block_rows(a, b)

    def body(core_ref, k_ref, g_ref, o_ref):
        o_ref[...] = (k_ref[...].astype(F32) + g_ref[...].astype(F32)).astype(BF16)

    spec = pl.BlockSpec((None, ta, b), lambda c, i, core_ref: (c, i, 0))
    own = pl.BlockSpec((None, None, ta, b), lambda c, i, core_ref: (c, core_ref[0], i, 0))
    grid_spec = pltpu.PrefetchScalarGridSpec(num_scalar_prefetch=1, grid=(q, a // ta), in_specs=[own, spec],
                                             out_specs=spec)
    return _pc(body, name=name, out_shape=_sds(got.shape, BF16), grid_spec=grid_spec, compiler_params=_params(2))(
        core, part.reshape((N_CHIPS, 2) + part.shape[1:]), got)


def _chip_exchange(sums, name):
    n = len(sums)

    def body(*refs):
        g_refs, out_refs = refs[:n], refs[n:2 * n]
        send_sems, recv_sems = refs[2 * n:]
        me_q, others = _chip_peers()
        sends, recvs = [], []
        for k, (peer, peer_q) in enumerate(others):
            for a in range(n):
                sends.append(pltpu.make_async_remote_copy(
                    src_ref=g_refs[a].at[peer_q], dst_ref=out_refs[a].at[me_q], send_sem=send_sems.at[k, a],
                    recv_sem=recv_sems.at[k, a], device_id=peer, device_id_type=MESH))
                recvs.append(pltpu.make_async_remote_copy(
                    src_ref=g_refs[a].at[me_q], dst_ref=out_refs[a].at[peer_q], send_sem=send_sems.at[k, a],
                    recv_sem=recv_sems.at[k, a], device_id=peer, device_id_type=MESH))
        for cp in sends:
            cp.start()
        for cp in recvs:
            cp.wait_recv()
        for cp in sends:
            cp.wait_send()

    hbm = pl.BlockSpec(memory_space=pl.ANY)
    return _pc(body, name=name, out_shape=[_sds(a.shape, a.dtype) for a in sums],
               in_specs=[hbm] * n, out_specs=[hbm] * n,
               scratch_shapes=[pltpu.SemaphoreType.DMA((3, n)), pltpu.SemaphoreType.DMA((3, n))])(*sums)


def _chip_peers():
    x, y, cc = _position()
    others = []
    for fx, fy in ((1, 0), (0, 1), (1, 1)):
        px, py = (1 - x if fx else x), (1 - y if fy else y)
        others.append(((px, py, cc), 2 * px + py))
    return 2 * x + y, others


def _chip_exchange_start(sums, name):
    n = len(sums)
    lands = [lax.empty(a.shape, a.dtype) for a in sums]

    def body(*refs):
        g_refs, land_refs = refs[:n], refs[n:2 * n]
        send_sems, recv_sems, token = refs[2 * n], refs[2 * n + 1], refs[-1]
        me_q, others = _chip_peers()
        for k, (peer, peer_q) in enumerate(others):
            for a in range(n):
                pltpu.make_async_remote_copy(
                    src_ref=g_refs[a].at[peer_q], dst_ref=land_refs[a].at[me_q], send_sem=send_sems.at[k * n + a],
                    recv_sem=recv_sems.at[k * n + a], device_id=peer, device_id_type=MESH).start()
        token[...] = jnp.zeros_like(token)

    hbm = pl.BlockSpec(memory_space=pltpu.HBM)
    sem = pl.BlockSpec(memory_space=pltpu.SEMAPHORE)
    out = _pc(body, name=name,
              out_shape=(pltpu.SemaphoreType.DMA((3 * n,)), pltpu.SemaphoreType.DMA((3 * n,)),
                         *[pltpu.HBM(a.shape, a.dtype) for a in sums], *[pltpu.HBM(a.shape, a.dtype) for a in sums],
                         _sds((SUBLANES, LANES))),
              in_specs=[hbm] * (2 * n), out_specs=(sem, sem, *([hbm] * (2 * n)), pl.BlockSpec(memory_space=pltpu.VMEM)),
              input_output_aliases={i: 2 + i for i in range(2 * n)},
              compiler_params=pltpu.CompilerParams(has_side_effects=pltpu.SideEffectType.DATAFLOW_SIDE_EFFECTING))(
        *[pltpu.with_memory_space_constraint(a, pltpu.HBM) for a in sums],
        *[pltpu.with_memory_space_constraint(l, pltpu.HBM) for l in lands])
    return (out[0], out[1], list(out[2:2 + n]), list(out[2 + n:2 + 2 * n])), out[-1]


def _chip_exchange_wait(state, after, name):
    send_sems, recv_sems, g_thru, land_thru = state
    n = len(g_thru)

    def body(*refs):
        g_refs, land_refs = refs[:n], refs[n:2 * n]
        send_sems, recv_sems = refs[2 * n], refs[2 * n + 1]
        me_q, others = _chip_peers()
        for k, (peer, peer_q) in enumerate(others):
            for a in range(n):
                cp = pltpu.make_async_remote_copy(
                    src_ref=g_refs[a].at[me_q], dst_ref=land_refs[a].at[peer_q], send_sem=send_sems.at[k * n + a],
                    recv_sem=recv_sems.at[k * n + a], device_id=peer, device_id_type=MESH)
                cp.wait_send()
                cp.wait_recv()

    hbm = pl.BlockSpec(memory_space=pltpu.HBM)
    sem = pl.BlockSpec(memory_space=pltpu.SEMAPHORE)
    out = _pc(body, name=name, out_shape=tuple(pltpu.HBM(a.shape, a.dtype) for a in g_thru + land_thru),
              in_specs=[hbm] * (2 * n) + [sem, sem, pl.BlockSpec(memory_space=pl.ANY)], out_specs=tuple([hbm] * (2 * n)),
              input_output_aliases={i: i for i in range(2 * n)},
              compiler_params=pltpu.CompilerParams(has_side_effects=pltpu.SideEffectType.DATAFLOW_SIDE_EFFECTING))(
        *g_thru, *land_thru, send_sems, recv_sems, after)
    return list(out[:n]), list(out[n:])


def _block_rows(a, b):
    ta = a
    while ta * b > 256 * 1024 and ta % 32 == 0:
        ta //= 2
    return ta


def _reduce_adamw(parts, w, m, v, name):
    n_parts, s, a, b = parts.shape
    ta = _block_rows(a, b)

    def body(p_ref, w_ref, m_ref, v_ref, g_out, d_out, m_out, v_out):
        g = p_ref[0].astype(F32)
        for j in range(1, n_parts):
            g = g + p_ref[j].astype(F32)
        delta, m_new, v_new = _adamw(g, w_ref[...], m_ref[...], v_ref[...])
        g_out[...] = g
        d_out[...] = delta
        m_out[...] = m_new
        v_out[...] = v_new

    spec = pl.BlockSpec((None, ta, b), lambda l, i: (l, i, 0))
    return _pc(body, name=name, out_shape=(_sds((s, a, b)),) * 4, grid=(s, a // ta),
               in_specs=[pl.BlockSpec((n_parts, None, ta, b), lambda l, i: (0, l, i, 0)), spec, spec, spec],
               out_specs=(spec,) * 4, compiler_params=_params(2))(parts, w, m, v)


def _adamw(g, w, m, v):
    c1 = 1.0 - ADAM_B1 ** ADAM_STEP
    c2 = 1.0 - ADAM_B2 ** ADAM_STEP
    m_new = ADAM_B1 * m + (1.0 - ADAM_B1) * g
    v_new = ADAM_B2 * v + (1.0 - ADAM_B2) * (g * g)
    delta = -ADAM_LR * ((m_new / c1) / (jnp.sqrt(v_new / c2) + ADAM_EPS) + ADAM_WD * w)
    return delta, m_new, v_new


def _chip_reduce_adamw(own, recv, w, m, v, chip, name):
    s, a, b = w.shape
    ta = _block_rows(a, b)

    def body(chip_ref, *refs):
        p_refs, (w_ref, m_ref, v_ref), (g_out, d_out, m_out, v_out) = refs[:4 * s], refs[4 * s:4 * s + 3], refs[4 * s + 3:]
        layer = pl.program_id(0)
        g = None
        for l in range(s):
            gl = p_refs[4 * l][...].astype(F32)
            for j in range(1, N_CHIPS):
                gl = gl + p_refs[4 * l + j][...].astype(F32)
            g = gl if g is None else jnp.where(layer == l, gl, g)
        delta, m_new, v_new = _adamw(g, w_ref[...], m_ref[...], v_ref[...])
        g_out[...] = g
        d_out[...] = delta
        m_out[...] = m_new
        v_out[...] = v_new

    def part_spec(l, j):
        return pl.BlockSpec((None, ta, b), lambda layer, i, chip_ref, l=l, j=j: (
            (chip_ref[0] + j) % N_CHIPS, jnp.where(layer == l, i, 0), 0))

    spec = pl.BlockSpec((None, ta, b), lambda layer, i, chip_ref: (layer, i, 0))
    in_specs, args = [], []
    for l in range(s):
        for j in range(N_CHIPS):
            in_specs.append(part_spec(l, j))
            args.append(own[l] if j == 0 else recv[l])
    grid_spec = pltpu.PrefetchScalarGridSpec(num_scalar_prefetch=1, grid=(s, a // ta), in_specs=in_specs + [spec] * 3,
                                             out_specs=(spec,) * 4)
    return _pc(body, name=name, out_shape=(_sds((s, a, b)),) * 4, grid_spec=grid_spec, compiler_params=_params(2))(
        chip, *args, w, m, v)


MATRICES = (("ada_mix_w", 2), ("w_in", 2), ("w_ssd_out", 1), ("w_conf_out", 2), ("w_sc_out", 2), ("w_o", 1),
            ("ada_ffn_w", 2), ("w_up", 2), ("w_down", 1))
MIXER_MATRICES = ("ada_mix_w", "w_in", "w_ssd_out", "w_conf_out", "w_sc_out", "w_o")
FFN_MATRICES = ("ada_ffn_w", "w_up", "w_down")
CONV_WEIGHTS = (("ssd_conv_w", 2), ("conf_conv_w", 2), ("sc_conv_w", 2), ("ffn_conv_w", 2))
SHARDED = MATRICES + CONV_WEIGHTS
REPLICATED = ("ada_mix_b", "norm_mix_g", "b_gate", "ssd_conv_b", "ssd_dt_bias", "ssd_a_log", "ssd_d", "ssd_norm_g",
              "conf_conv_b", "conf_ln_g", "conf_ln_b", "ada_ffn_b", "norm_ffn_g", "ffn_conv_b", "final_norm_g")
WEIGHT_NAMES = ("ada_mix_w", "ada_mix_b", "norm_mix_g", "w_in", "b_gate", "ssd_conv_w", "ssd_conv_b", "ssd_dt_bias",
                "ssd_a_log", "ssd_d", "ssd_norm_g", "w_ssd_out", "conf_conv_w", "conf_conv_b", "conf_ln_g",
                "conf_ln_b", "w_conf_out", "sc_conv_w", "w_sc_out", "w_o", "ada_ffn_w", "ada_ffn_b", "norm_ffn_g",
                "w_up", "ffn_conv_w", "ffn_conv_b", "w_down", "final_norm_g")


def _pack_flat(arrays, cols, row_multiple, dtype):
    flat = jnp.concatenate([a.reshape(-1).astype(dtype) for a in arrays])
    rows = -(-flat.shape[0] // cols)
    rows = -(-rows // row_multiple) * row_multiple
    return jnp.pad(flat, (0, rows * cols - flat.shape[0])).reshape(rows, cols)


def _unpack_flat(flat2d, shapes):
    flat = flat2d.reshape(-1)
    out, off = [], 0
    for s in shapes:
        n = 1
        for d in s:
            n *= d
        out.append(flat[off:off + n].reshape(s))
        off += n
    return out


def _cols(g, lo, hi):
    b = g.shape[-1]
    pieces = []
    for k in range(N_DEV):
        a, e = max(lo, k * b), min(hi, (k + 1) * b)
        if a < e:
            pieces.append(g[k, :, a - k * b:e - k * b])
    return pieces[0] if len(pieces) == 1 else jnp.concatenate(pieces, axis=1)


def _rows(g):
    return g.reshape(N_DEV * g.shape[1], g.shape[2])


def _col_shards(segs, b):
    shards = []
    for k in range(N_DEV):
        lo, hi = k * b, (k + 1) * b
        pieces, off = [], 0
        for seg in segs:
            n = seg.shape[1]
            a, e = max(lo, off), min(hi, off + n)
            if a < e:
                pieces.append(seg[:, a - off:e - off])
            off += n
        shards.append(pieces[0] if len(pieces) == 1 else jnp.concatenate(pieces, axis=1))
    return jnp.stack(shards)


def _row_shards(full):
    return full.reshape(N_DEV, full.shape[0] // N_DEV, full.shape[1])


def _pad_rows(a, rows):
    return jnp.pad(a, ((0, rows - a.shape[0]), (0, 0)))


def _pad_lanes(a):
    return jnp.pad(a, ((0, 0), (0, LANES - a.shape[1])))


def _whole(g):
    return _cols(g, 0, N_DEV * g.shape[-1])


def _mixer_weights(full, i):
    row = lambda name: full[name][i].reshape(1, -1)
    conv = lambda name: _whole(full[name][:, i])
    w_in = full["w_in", i]
    return {
        "ada_mix_w": _whole(full["ada_mix_w", i]), "ada_mix_b": row("ada_mix_b"), "norm_mix_g": row("norm_mix_g"),
        "w_z": _cols(w_in, 0, OFF_Z), "w_xbc": _cols(w_in, OFF_Z, OFF_XBC),
        "w_dt": _pad_lanes(_cols(w_in, OFF_XBC, OFF_DT)), "w_conf": _cols(w_in, OFF_DT, OFF_CONF),
        "w_sc": _cols(w_in, OFF_CONF, OFF_SC), "w_gates": _cols(w_in, OFF_SC, N_IN),
        "b_gate": row("b_gate"),
        "ssd_conv_w": _pad_rows(conv("ssd_conv_w"), SUBLANES), "ssd_conv_b": row("ssd_conv_b"),
        "dt_bias": _pad_lanes(row("ssd_dt_bias")), "a_log": _pad_lanes(row("ssd_a_log")),
        "ssd_d": _pad_lanes(row("ssd_d")), "ssd_norm_g": row("ssd_norm_g"), "w_ssd_out": _rows(full["w_ssd_out", i]),
        "conf_conv_w": _pad_rows(conv("conf_conv_w"), CONF_HALO), "conf_conv_b": row("conf_conv_b"),
        "conf_ln_g": row("conf_ln_g"), "conf_ln_b": row("conf_ln_b"), "w_conf_out": _whole(full["w_conf_out", i]),
        "sc_conv_w": _pad_rows(conv("sc_conv_w"), SUBLANES), "w_sc_out": _whole(full["w_sc_out", i]),
        "w_o": _rows(full["w_o", i]),
    }


def _ffn_weights(full, i):
    row = lambda name: full[name][i].reshape(1, -1)
    return {
        "ada_ffn_w": _whole(full["ada_ffn_w", i]), "ada_ffn_b": row("ada_ffn_b"), "norm_ffn_g": row("norm_ffn_g"),
        "w_up": _whole(full["w_up", i]), "ffn_conv_w": _pad_rows(_whole(full["ffn_conv_w"][:, i]), SUBLANES),
        "ffn_conv_b": row("ffn_conv_b"), "w_down": _rows(full["w_down", i]),
    }


def _adaln(sc8, w, b, name):
    mod = _matmul(sc8, w, "nn", F32, name)[0:1, :] + b
    return mod[:, :D_MODEL], mod[:, D_MODEL:2 * D_MODEL], mod[:, 2 * D_MODEL:]


def _mixer_fwd(i, x, prev, sc8, wl):
    t = f"l{i}_"
    s = {}
    shift, scale, gate = _adaln(sc8, wl["ada_mix_w"], wl["ada_mix_b"], t + "ada_mix")
    if prev is None:
        s["x_in"] = x
        s["h"] = _prenorm_first(x, wl["norm_mix_g"], scale, shift, t + "norm_mix")
    else:
        s["x_in"], s["h"] = _prenorm_res(x, prev[0], prev[1], wl["norm_mix_g"], scale, shift, t + "norm_mix")
    s["scale_mix"], s["gate_mix"] = scale, gate
    h = s["h"]
    s["z"] = _matmul(h, wl["w_z"], "nn", F32, t + "in_z")
    s["xbc"] = _matmul(h, wl["w_xbc"], "nn", F32, t + "in_xbc")
    s["dt_raw"] = _matmul(h, wl["w_dt"], "nn", F32, t + "in_dt")
    s["conf"] = _matmul(h, wl["w_conf"], "nn", F32, t + "in_conf")
    s["sc"] = _matmul(h, wl["w_sc"], "nn", F32, t + "in_sc")
    s["gates"] = _matmul(h, wl["w_gates"], "nn", F32, t + "in_gates")
    s["pre"], s["dt"] = _ssd_pre(s["xbc"], s["dt_raw"], wl["ssd_conv_w"], wl["ssd_conv_b"], wl["dt_bias"],
                                 t + "ssd_pre")
    s["y"], s["hprev"] = _ssd_scan(s["pre"], s["dt"], wl["a_log"], wl["ssd_d"], t + "ssd_scan")
    s["ya_in"] = _ssd_post(s["y"], s["z"], wl["ssd_norm_g"], t + "ssd_post")
    s["yb_in"], s["uc"] = _conf_fwd(s["conf"], wl["conf_conv_w"], wl["conf_conv_b"], wl["conf_ln_g"],
                                    wl["conf_ln_b"], t + "conf")
    s["yc_in"] = _sc_fwd(s["sc"], wl["sc_conv_w"], t + "sconv")
    s["ya"] = _matmul(s["ya_in"], wl["w_ssd_out"], "nn", F32, t + "ssd_out")
    s["yb"] = _matmul(s["yb_in"], wl["w_conf_out"], "nn", F32, t + "conf_out")
    s["yc"] = _matmul(s["yc_in"], wl["w_sc_out"], "nn", F32, t + "sc_out")
    s["merged"] = _merge_fwd(s["gates"], s["ya"], s["yb"], s["yc"], wl["b_gate"], t + "merge")
    s["mix"] = _matmul(s["merged"], wl["w_o"], "nn", F32, t + "w_o")
    return s


def _ffn_fwd(i, s, sc8, wl):
    t = f"l{i}_"
    shift2, scale2, gate2 = _adaln(sc8, wl["ada_ffn_w"], wl["ada_ffn_b"], t + "ada_ffn")
    s["x_mid"], s["h2"] = _prenorm_res(s["x_in"], s["mix"], s["gate_mix"], wl["norm_ffn_g"], scale2, shift2,
                                       t + "norm_ffn")
    s["scale_ffn"], s["gate_ffn"] = scale2, gate2
    s["up"] = _matmul(s["h2"], wl["w_up"], "nn", F32, t + "w_up")
    s["a"] = _ffn_mid(s["up"], wl["ffn_conv_w"], wl["ffn_conv_b"], t + "ffn_mid")
    s["out"] = _matmul(s["a"], wl["w_down"], "nn", F32, t + "w_down")
    return s


def _layer_bwd(i, s, wl, sc8, dys_ffn, dx_after, dgate_ffn, prev, emit=None):
    t = f"l{i}_b_"
    g = {}
    da = _matmul(dys_ffn, wl["w_down"], "nt", F32, t + "d_a")
    g["w_down"] = _matmul(s["a"], dys_ffn, "tn", BF16, t + "dw_down")
    dug, duv, dfw, g["ffn_conv_b"] = _ffn_mid_bwd(da, s["up"], wl["ffn_conv_w"], wl["ffn_conv_b"], t + "ffn_mid")
    g["ffn_conv_w"] = dfw[:3]
    dh2 = _matmul_sum_nt([dug, duv], [wl["w_up"][:, :D_FF], wl["w_up"][:, D_FF:]], t + "d_h2")
    g["w_up"] = [_matmul(s["h2"], dug, "tn", BF16, t + "dw_up_g"), _matmul(s["h2"], duv, "tn", BF16, t + "dw_up_v")]
    dx_mid, dshift2, dscale2, g["norm_ffn_g"], dys_mix, dgate_mix = _norm_bwd(
        dh2, s["x_mid"], dx_after, wl["norm_ffn_g"], s["scale_ffn"], t + "norm_ffn", s["mix"], s["gate_mix"])
    dmod_ffn = jnp.concatenate([dshift2, dscale2, dgate_ffn], axis=1)
    g["ada_ffn_b"] = dmod_ffn
    g["ada_ffn_w"] = [_matmul(sc8, _pad_rows(dmod_ffn, SUBLANES), "tn", BF16, t + "dw_ada_ffn")]
    token = emit(i, "ffn", g) if emit is not None else None
    if token is not None:
        wl = {**wl, "b_gate": wl["b_gate"] + token}
    dmerged = _matmul(dys_mix, wl["w_o"], "nt", F32, t + "d_merged")
    g["w_o"] = _matmul(s["merged"], dys_mix, "tn", BF16, t + "dw_o")
    dya, dyb, dyc, dgates, g["b_gate"] = _merge_bwd(dmerged, s["gates"], s["ya"], s["yb"], s["yc"], wl["b_gate"],
                                                    t + "merge")
    dya_in = _matmul(dya, wl["w_ssd_out"], "nt", F32, t + "d_ya_in")
    g["w_ssd_out"] = _matmul(s["ya_in"], dya, "tn", BF16, t + "dw_ssd_out")
    dyb_in = _matmul(dyb, wl["w_conf_out"], "nt", F32, t + "d_yb_in")
    g["w_conf_out"] = [_matmul(s["yb_in"], dyb, "tn", BF16, t + "dw_conf_out")]
    dyc_in = _matmul(dyc, wl["w_sc_out"], "nt", F32, t + "d_yc_in")
    g["w_sc_out"] = [_matmul(s["yc_in"], dyc, "tn", BF16, t + "dw_sc_out")]
    dy, dz, g["ssd_norm_g"] = _ssd_post_bwd(dya_in, s["y"], s["z"], wl["ssd_norm_g"], t + "ssd_post")
    dpre, ddt, da_log, dd = _ssd_scan_bwd(s["pre"], s["dt"], s["hprev"], dy, wl["a_log"], wl["ssd_d"],
                                          t + "ssd_scan")
    g["ssd_a_log"], g["ssd_d"] = da_log[:, :SSD_HEADS], dd[:, :SSD_HEADS]
    dxbc, ddt_raw, dcw, g["ssd_conv_b"], ddtb = _ssd_pre_bwd(dpre, s["xbc"], ddt, s["dt_raw"], wl["ssd_conv_w"],
                                                             wl["dt_bias"], t + "ssd_pre")
    g["ssd_conv_w"], g["ssd_dt_bias"] = dcw[:4], ddtb[:, :SSD_HEADS]
    dconf, dccw, g["conf_conv_b"], g["conf_ln_g"], g["conf_ln_b"] = _conf_bwd(
        dyb_in, s["uc"], s["conf"], wl["conf_conv_w"], wl["conf_ln_g"], wl["conf_ln_b"], t + "conf")
    g["conf_conv_w"] = dccw[:CONF_KERNEL]
    dsc, dscw = _sc_bwd(dyc_in, s["sc"], wl["sc_conv_w"], t + "sconv")
    g["sc_conv_w"] = dscw[:3]
    segs = (("z", dz, "w_z"), ("xbc", dxbc, "w_xbc"), ("dt", ddt_raw, "w_dt"), ("conf", dconf, "w_conf"),
            ("sc", dsc, "w_sc"), ("gates", dgates, "w_gates"))
    dh = _matmul_sum_nt([dseg for _, dseg, _ in segs], [wl[wname] for _, _, wname in segs], t + "d_h")
    dw_segs = []
    for nm, dseg, wname in segs:
        dw = _matmul(s["h"], dseg, "tn", BF16, t + "dw_in_" + nm)
        dw_segs.append(dw[:, :SSD_HEADS] if nm == "dt" else dw)
    g["w_in"] = dw_segs
    if prev is None:
        dx_in, dshift, dscale, g["norm_mix_g"] = _norm_bwd(dh, s["x_in"], dx_mid, wl["norm_mix_g"], s["scale_mix"],
                                                          t + "norm_mix")
        back = None
    else:
        dx_in, dshift, dscale, g["norm_mix_g"], dys_prev, dgate_prev = _norm_bwd(
            dh, s["x_in"], dx_mid, wl["norm_mix_g"], s["scale_mix"], t + "norm_mix", prev[0], prev[1])
        back = (dys_prev, dgate_prev)
    dmod_mix = jnp.concatenate([dshift, dscale, dgate_mix], axis=1)
    g["ada_mix_b"] = dmod_mix
    g["ada_mix_w"] = [_matmul(sc8, _pad_rows(dmod_mix, SUBLANES), "tn", BF16, t + "dw_ada_mix")]
    return g, dx_in, back


def _device_step(x, c, target, full, fetch=None, emit=None):
    fetch = fetch or {}
    full = dict(full)
    sc8 = _pad_rows(c * (1.0 / (1.0 + jnp.exp(-c))), SUBLANES)
    wls, saved, prev, xcur = [], [], None, x
    for i in range(DEPTH):
        if (i, "mixer") in fetch:
            full.update(fetch[i, "mixer"](prev[0]))
        wl = _mixer_weights(full, i)
        s = _mixer_fwd(i, xcur, prev, sc8, wl)
        if (i, "ffn") in fetch:
            full.update(fetch[i, "ffn"](s["mix"]))
        wf = _ffn_weights(full, i)
        _ffn_fwd(i, s, sc8, wf)
        wls.append({**wl, **wf})
        saved.append(s)
        xcur, prev = s["x_mid"], (s["out"], s["gate_ffn"])
    gf = full["final_norm_g"].reshape(1, -1)
    last = saved[-1]
    loss, dx, dys, dgate, dgf = _final_loss(last["x_mid"], last["out"], last["gate_ffn"], gf, target, "final_loss")
    grads = [None] * DEPTH
    for i in reversed(range(DEPTH)):
        prev = None if i == 0 else (saved[i - 1]["out"], saved[i - 1]["gate_ffn"])
        grads[i], dx, back = _layer_bwd(i, saved[i], wls[i], sc8, dys, dx, dgate, prev, emit)
        token = emit(i, "mixer", grads[i]) if emit is not None else None
        if token is not None:
            wls[i - 1] = {**wls[i - 1], "ffn_conv_b": wls[i - 1]["ffn_conv_b"] + token}
        if back is not None:
            dys, dgate = back
    return loss[0, 0], dx, grads, dgf


def _step(x, c, target, weights, moments_m, moments_v):
    sharded_names = [n for n, _ in SHARDED]
    conv_names = [n for n, _ in CONV_WEIGHTS]
    shard = lambda key: weights[key[0]][key[1]].astype(BF16)
    first = [(n, 0) for n in MIXER_MATRICES]
    later = {(0, "ffn"): [(n, 0) for n in FFN_MATRICES], (1, "mixer"): [(n, 1) for n in MIXER_MATRICES + FFN_MATRICES]}
    gathered = _all_gather([shard(k) for k in first] + [weights[n] for n in conv_names], "gather_first")
    full = {n: weights[n] for n in REPLICATED}
    full.update(zip(first + conv_names, gathered))
    fetch, after = {}, gathered[0]
    for stage, keys in later.items():
        state, after = _gather_start([shard(k) for k in keys], after, f"gather_l{stage[0]}_{stage[1]}_start")
        fetch[stage] = functools.partial(
            lambda act, state, keys, nm: dict(zip(keys, _gather_wait(state, act, nm))),
            state=state, keys=keys, nm=f"gather_l{stage[0]}_{stage[1]}_wait")
    axis_of = dict(SHARDED)
    core = lax.axis_index("c").astype(jnp.int32).reshape(1)
    chip = (2 * lax.axis_index("x") + lax.axis_index("y")).astype(jnp.int32).reshape(1)
    ffn_names = list(FFN_MATRICES) + ["ffn_conv_w"]
    mixer_names = [n for n in sharded_names if n not in ffn_names]
    sums, received, pending = {}, {}, []

    def send(i, names, grads_i, last):
        keys = [(n, i) for n in names]
        parts = []
        for n in names:
            gw = grads_i[n]
            part = _row_shards(gw) if axis_of[n] == 1 else _col_shards(gw if isinstance(gw, list) else [gw],
                                                                       weights[n].shape[-1])
            parts.append(part.astype(BF16))
        tag = f"l{i}_{'ffn' if names is ffn_names else 'mixer'}"
        got = _sibling_swap(parts, "swap_grads_" + tag)
        pair = [_pair_add(p, g, core, f"pair_add_{n}_{i}") for n, p, g in zip(names, parts, got)]
        if last:
            sums.update(zip(keys, pair))
            received.update(zip(keys, _chip_exchange(pair, "exchange_grads_" + tag)))
            return None
        state, token = _chip_exchange_start(pair, "exchange_grads_" + tag + "_start")
        pending.append((keys, state, tag))
        return token[0:1, 0:1]

    def emit(i, kind, grads_i):
        if kind == "ffn" and i == 0:
            return send(0, ffn_names, grads_i, False)
        if kind == "mixer":
            return send(i, mixer_names if i == 0 else sharded_names, grads_i, i == 0)
        return None

    loss, grad_x, grads, dgf = _device_step(x[0], c + after[0:1, 0:1], target[0], full, fetch, emit)
    for keys, state, tag in pending:
        own, got = _chip_exchange_wait(state, grad_x, "exchange_grads_" + tag + "_wait")
        sums.update(zip(keys, own))
        received.update(zip(keys, got))
    big = {n: _chip_reduce_adamw([sums[n, i] for i in range(DEPTH)], [received[n, i] for i in range(DEPTH)],
                                 weights[n], moments_m[n], moments_v[n], chip, "adamw_" + n)
           for n in sharded_names}
    rep_grads = [dgf if n == "final_norm_g" else jnp.stack([grads[i][n].reshape(-1) for i in range(DEPTH)])
                 for n in REPLICATED]
    small_parts, = _all_gather([_pack_flat(rep_grads, LANES, SUBLANES, F32)], "gather_small_grads")
    pack_s = lambda d: _pack_flat([d[n] for n in REPLICATED], LANES, SUBLANES, F32)[None]
    small = _reduce_adamw(small_parts[:, None], pack_s(weights), pack_s(moments_m), pack_s(moments_v),
                          "adamw_replicated")
    small = [_unpack_flat(b, [weights[n].shape for n in REPLICATED]) for b in small]
    results = []
    for kind in range(4):
        by_name = {n: big[n][kind] for n in sharded_names}
        by_name.update(zip(REPLICATED, small[kind]))
        results.append([by_name[n] for n in WEIGHT_NAMES])
    loss = lax.psum(loss, ("x", "y", "c"))
    return (loss, grad_x[None], *results[0], *results[1], *results[2], *results[3])


def kernel(x, c, ada_mix_w, ada_mix_b, norm_mix_g, w_in, b_gate, ssd_conv_w, ssd_conv_b, ssd_dt_bias, ssd_a_log, ssd_d, ssd_norm_g, w_ssd_out, conf_conv_w, conf_conv_b, conf_ln_g, conf_ln_b, w_conf_out, sc_conv_w, w_sc_out, w_o, ada_ffn_w, ada_ffn_b, norm_ffn_g, w_up, ffn_conv_w, ffn_conv_b, w_down, final_norm_g, loss_target, m_ada_mix_w, m_ada_mix_b, m_norm_mix_g, m_w_in, m_b_gate, m_ssd_conv_w, m_ssd_conv_b, m_ssd_dt_bias, m_ssd_a_log, m_ssd_d, m_ssd_norm_g, m_w_ssd_out, m_conf_conv_w, m_conf_conv_b, m_conf_ln_g, m_conf_ln_b, m_w_conf_out, m_sc_conv_w, m_w_sc_out, m_w_o, m_ada_ffn_w, m_ada_ffn_b, m_norm_ffn_g, m_w_up, m_ffn_conv_w, m_ffn_conv_b, m_w_down, m_final_norm_g, v_ada_mix_w, v_ada_mix_b, v_norm_mix_g, v_w_in, v_b_gate, v_ssd_conv_w, v_ssd_conv_b, v_ssd_dt_bias, v_ssd_a_log, v_ssd_d, v_ssd_norm_g, v_w_ssd_out, v_conf_conv_w, v_conf_conv_b, v_conf_ln_g, v_conf_ln_b, v_w_conf_out, v_sc_conv_w, v_w_sc_out, v_w_o, v_ada_ffn_w, v_ada_ffn_b, v_norm_ffn_g, v_w_up, v_ffn_conv_w, v_ffn_conv_b, v_w_down, v_final_norm_g):
    given = dict(locals())
    weights = {n: given[n] for n in WEIGHT_NAMES}
    moments_m = {n: given["m_" + n] for n in WEIGHT_NAMES}
    moments_v = {n: given["v_" + n] for n in WEIGHT_NAMES}
    return _step(x, c, loss_target, weights, moments_m, moments_v)
```

```python
import functools

import jax
import jax.numpy as jnp
from jax import lax
from jax.experimental import pallas as pl
from jax.experimental.pallas import tpu as pltpu

F32 = jnp.float32
BF16 = jnp.bfloat16
MESH = pl.DeviceIdType.MESH

N_DEV = 8
DEPTH = 2
D_MODEL = 1024
SSD_HEADS = 16
SSD_HEAD_DIM = 64
SSD_INNER = 1024
SSD_STATE = 64
SSD_CHUNK = 128
SSD_XBC = 1280
CONF_WIDTH = 512
CONF_KERNEL = 31
SC_WIDTH = 512
D_FF = 2816
EPS = 1e-6
OFF_Z, OFF_XBC, OFF_DT, OFF_CONF, OFF_SC, N_IN = 1024, 2304, 2320, 3344, 4880, 7952

ADAM_LR, ADAM_B1, ADAM_B2, ADAM_EPS, ADAM_WD, ADAM_STEP = 0.001, 0.9, 0.999, 1e-08, 0.01, 10

LANES = 128
SUBLANES = 8
VMEM_LIMIT = 56 * 1024 * 1024
ROW_TILE = 256

NN = (((1,), (0,)), ((), ()))
NT = (((1,), (1,)), ((), ()))
TN = (((0,), (0,)), ((), ()))


def _params(n_axes):
    return pltpu.CompilerParams(dimension_semantics=("arbitrary",) * n_axes, vmem_limit_bytes=VMEM_LIMIT)


def _pc(body, **kw):
    return pl.pallas_call(body, **kw)


def _dot(a, b, dn=NN, precision=None):
    return lax.dot_general(a, b, dn, precision=precision, preferred_element_type=F32)


def _split3(x):
    hi = x.astype(BF16)
    r1 = x - hi.astype(F32)
    mid = r1.astype(BF16)
    return hi, mid, (r1 - mid.astype(F32)).astype(BF16)


def _dot_sel(x, sel):
    hi, mid, lo = _split3(x)
    return _dot(hi, sel) + _dot(mid, sel) + _dot(lo, sel)


def _sel_dot(sel, x):
    hi, mid, lo = _split3(x)
    return _dot(sel, hi) + _dot(sel, mid) + _dot(sel, lo)


def _sig(x):
    return 1.0 / (1.0 + jnp.exp(-x))


def _fold(v):
    r, c = v.shape
    return v.reshape(r // SUBLANES, SUBLANES, c).sum(axis=0)


def _tile(n_rows, target=ROW_TILE):
    return min(target, n_rows // 2)


STRIP_UNITS = 8
STRIP_ROW_TILE = 512


def _strip_units_per_trip(tl):
    return min(STRIP_UNITS, tl // SUBLANES)


def _bcast_row(ref, k, ls):
    return jnp.broadcast_to(ref[k:k + 1, ls], (SUBLANES, LANES))


def _unit_rows(t, u, nu):
    return pl.ds(pl.multiple_of((t * nu + u) * SUBLANES, SUBLANES), SUBLANES)


def _pair_rows(t, p, nu):
    return pl.ds(pl.multiple_of((t * nu + 2 * p) * SUBLANES, 2 * SUBLANES), 2 * SUBLANES)


def _strip_units(ref, ls, t, nu, nt, halo_prev=None, halo_next=None):
    units = [ref[_unit_rows(t, u, nu), ls] for u in range(nu)]
    if halo_prev is not None:
        before = pl.ds(pl.multiple_of(jnp.maximum(t * nu - 1, 0) * SUBLANES, SUBLANES), SUBLANES)
        units.insert(0, jnp.where(t > 0, ref[before, ls], halo_prev))
    if halo_next is not None:
        after = pl.ds(pl.multiple_of(jnp.minimum((t + 1) * nu, nt * nu - 1) * SUBLANES, SUBLANES), SUBLANES)
        units.append(jnp.where(t < nt - 1, ref[after, ls], halo_next))
    return units


def _row(tl, c, col=0):
    return pl.BlockSpec((tl, c), lambda i, col=col: (i, col))


def _prev(tl, hb, c, col=0):
    r = tl // hb
    return pl.BlockSpec((hb, c), lambda i, col=col: (jnp.maximum(i * r - 1, 0), col))


def _next(tl, hb, c, n_rows, col=0):
    r = tl // hb
    last = n_rows // hb - 1
    return pl.BlockSpec((hb, c), lambda i, col=col: (jnp.minimum((i + 1) * r, last), col))


def _const(shape):
    return pl.BlockSpec(shape, lambda i: (0,) * len(shape))


def _sds(shape, dtype=F32):
    return jax.ShapeDtypeStruct(shape, dtype)


MM_TILE = 1536
MM_TILE_ROWS = 2048
MM_FULL_K = 3072
MM_K_TILE = 1024


def _pick(dim, target):
    if dim <= target:
        return dim
    best = None
    for t in range(LANES, target + 1, LANES):
        if dim % t == 0:
            best = t
    assert best is not None, (dim, target)
    return best


def _matmul(a, b, mode, out_dtype, name):
    if mode == "nn":
        (m, k), (k2, n) = a.shape, b.shape
    elif mode == "nt":
        (m, k), (n, k2) = a.shape, b.shape
    else:
        (k, m), (k2, n) = a.shape, b.shape
    assert k == k2, (a.shape, b.shape, mode)
    tn = _pick(n, MM_TILE)
    tk = k if k <= MM_FULL_K else _pick(k, MM_K_TILE)
    tm = _pick(m, MM_TILE_ROWS if tn <= 1024 and tk <= 1024 else 1024)
    nk = k // tk
    dn = {"nn": NN, "nt": NT, "tn": TN}[mode]

    def body_one(a_ref, b_ref, o_ref):
        o_ref[...] = _dot(a_ref[...].astype(BF16), b_ref[...].astype(BF16), dn).astype(out_dtype)

    def body_acc(a_ref, b_ref, o_ref, acc):
        kk = pl.program_id(2)

        @pl.when(kk == 0)
        def _():
            acc[...] = jnp.zeros_like(acc)

        acc[...] += _dot(a_ref[...].astype(BF16), b_ref[...].astype(BF16), dn)

        @pl.when(kk == nk - 1)
        def _():
            o_ref[...] = acc[...].astype(out_dtype)

    a_spec = {"nn": pl.BlockSpec((tm, tk), lambda i, j, kk: (i, kk)),
              "nt": pl.BlockSpec((tm, tk), lambda i, j, kk: (i, kk)),
              "tn": pl.BlockSpec((tk, tm), lambda i, j, kk: (kk, i))}[mode]
    b_spec = {"nn": pl.BlockSpec((tk, tn), lambda i, j, kk: (kk, j)),
              "nt": pl.BlockSpec((tn, tk), lambda i, j, kk: (j, kk)),
              "tn": pl.BlockSpec((tk, tn), lambda i, j, kk: (kk, j))}[mode]
    o_spec = pl.BlockSpec((tm, tn), lambda i, j, kk: (i, j))
    return _pc(body_one if nk == 1 else body_acc, name=name, out_shape=_sds((m, n), out_dtype),
               grid=(m // tm, n // tn, nk), in_specs=[a_spec, b_spec], out_specs=o_spec,
               scratch_shapes=[] if nk == 1 else [pltpu.VMEM((tm, tn), F32)], compiler_params=_params(3))(a, b)


SUM_NT_TILE = 512


def _matmul_sum_nt(a_list, b_list, name):
    m, n = a_list[0].shape[0], b_list[0].shape[0]
    cnt = len(a_list)
    tm, tn = _pick(m, SUM_NT_TILE), _pick(n, SUM_NT_TILE)

    def body(*refs):
        a_refs, b_refs, o_ref = refs[:cnt], refs[cnt:2 * cnt], refs[2 * cnt]
        acc = _dot(a_refs[0][...].astype(BF16), b_refs[0][...].astype(BF16), NT)
        for t in range(1, cnt):
            acc = acc + _dot(a_refs[t][...].astype(BF16), b_refs[t][...].astype(BF16), NT)
        o_ref[...] = acc

    in_specs = [pl.BlockSpec((tm, a.shape[1]), lambda j, i: (i, 0)) for a in a_list]
    in_specs += [pl.BlockSpec((tn, b.shape[1]), lambda j, i: (j, 0)) for b in b_list]
    return _pc(body, name=name, out_shape=_sds((m, n)), grid=(n // tn, m // tm), in_specs=in_specs,
               out_specs=pl.BlockSpec((tm, tn), lambda j, i: (i, j)), compiler_params=_params(2))(*a_list, *b_list)


def _norm_mod(x, g, scale, shift):
    r = lax.rsqrt(jnp.mean(x * x, axis=-1, keepdims=True) + EPS)
    return ((x * r) * g) * (1.0 + scale) + shift


def _prenorm_first(x, g, scale, shift, name):
    n, d = x.shape
    tl = _tile(n)

    def body(x_ref, g_ref, sc_ref, sh_ref, h_ref):
        h_ref[...] = _norm_mod(x_ref[...], g_ref[...], sc_ref[...], sh_ref[...]).astype(BF16)

    return _pc(body, name=name, out_shape=_sds((n, d), BF16), grid=(n // tl,),
               in_specs=[_row(tl, d)] + [_const((1, d))] * 3, out_specs=_row(tl, d),
               compiler_params=_params(1))(x, g, scale, shift)


def _prenorm_res(x, y, gate, g, scale, shift, name):
    n, d = x.shape
    tl = _tile(n)

    def body(x_ref, y_ref, gate_ref, g_ref, sc_ref, sh_ref, xo_ref, h_ref):
        xn = x_ref[...] + gate_ref[...] * y_ref[...]
        xo_ref[...] = xn
        h_ref[...] = _norm_mod(xn, g_ref[...], sc_ref[...], sh_ref[...]).astype(BF16)

    return _pc(body, name=name, out_shape=(_sds((n, d)), _sds((n, d), BF16)), grid=(n // tl,),
               in_specs=[_row(tl, d), _row(tl, d)] + [_const((1, d))] * 4,
               out_specs=(_row(tl, d), _row(tl, d)), compiler_params=_params(1))(x, y, gate, g, scale, shift)


def _final_loss(x, y, gate, gf, target, name):
    n, d = x.shape
    tl = _tile(n)
    nb = n // tl

    def body(x_ref, y_ref, gate_ref, gf_ref, t_ref, loss_ref, dx_ref, dys_ref, dgate_ref, dgf_ref,
             acc_l, acc_gate, acc_gf):
        i = pl.program_id(0)

        @pl.when(i == 0)
        def _():
            acc_l[...] = jnp.zeros_like(acc_l)
            acc_gate[...] = jnp.zeros_like(acc_gate)
            acc_gf[...] = jnp.zeros_like(acc_gf)

        yv = y_ref[...]
        gate = gate_ref[...]
        gf = gf_ref[...]
        x2 = x_ref[...] + gate * yv
        r = lax.rsqrt(jnp.mean(x2 * x2, axis=-1, keepdims=True) + EPS)
        xn = x2 * r
        e = xn * gf - t_ref[...]
        acc_l[...] += _fold(e * e)
        dy = e * (1.0 / d)
        acc_gf[...] += _fold(dy * xn)
        dxn = dy * gf
        dx = r * (dxn - xn * jnp.mean(dxn * xn, axis=-1, keepdims=True))
        dx_ref[...] = dx
        dys_ref[...] = (dx * gate).astype(BF16)
        acc_gate[...] += _fold(dx * yv)

        @pl.when(i == nb - 1)
        def _():
            loss_ref[...] = jnp.full((SUBLANES, LANES), 0.5 / d, F32) * jnp.sum(acc_l[...])
            dgate_ref[...] = jnp.sum(acc_gate[...], axis=0, keepdims=True)
            dgf_ref[...] = jnp.sum(acc_gf[...], axis=0, keepdims=True)

    return _pc(body, name=name,
               out_shape=(_sds((SUBLANES, LANES)), _sds((n, d)), _sds((n, d), BF16), _sds((1, d)), _sds((1, d))),
               grid=(nb,),
               in_specs=[_row(tl, d), _row(tl, d), _const((1, d)), _const((1, d)), _row(tl, d)],
               out_specs=(_const((SUBLANES, LANES)), _row(tl, d), _row(tl, d), _const((1, d)), _const((1, d))),
               scratch_shapes=[pltpu.VMEM((SUBLANES, d), F32)] * 3,
               compiler_params=_params(1))(x, y, gate, gf, target)


def _norm_bwd(dh, x, dxo, g, scale, name, y_prev=None, gate_prev=None):
    n, d = x.shape
    tl = _tile(n)
    nb = n // tl
    has_prev = y_prev is not None

    def body(*refs):
        if has_prev:
            (dh_ref, x_ref, dxo_ref, g_ref, sc_ref, yp_ref, gp_ref,
             dx_ref, dsh_ref, dsc_ref, dg_ref, dys_ref, dgp_ref, acc_sh, acc_s, acc_gp) = refs
        else:
            (dh_ref, x_ref, dxo_ref, g_ref, sc_ref,
             dx_ref, dsh_ref, dsc_ref, dg_ref, acc_sh, acc_s) = refs
        i = pl.program_id(0)

        @pl.when(i == 0)
        def _():
            acc_sh[...] = jnp.zeros_like(acc_sh)
            acc_s[...] = jnp.zeros_like(acc_s)
            if has_prev:
                acc_gp[...] = jnp.zeros_like(acc_gp)

        x_ = x_ref[...]
        dh_ = dh_ref[...]
        g_ = g_ref[...]
        one_sc = 1.0 + sc_ref[...]
        r = lax.rsqrt(jnp.mean(x_ * x_, axis=-1, keepdims=True) + EPS)
        xn = x_ * r
        dxn = dh_ * (g_ * one_sc)
        dx = dxo_ref[...] + r * (dxn - xn * jnp.mean(dxn * xn, axis=-1, keepdims=True))
        dx_ref[...] = dx
        acc_sh[...] += _fold(dh_)
        acc_s[...] += _fold(dh_ * xn)
        if has_prev:
            dys_ref[...] = (dx * gp_ref[...]).astype(BF16)
            acc_gp[...] += _fold(dx * yp_ref[...])

        @pl.when(i == nb - 1)
        def _():
            s = jnp.sum(acc_s[...], axis=0, keepdims=True)
            dsh_ref[...] = jnp.sum(acc_sh[...], axis=0, keepdims=True)
            dsc_ref[...] = s * g_
            dg_ref[...] = s * one_sc
            if has_prev:
                dgp_ref[...] = jnp.sum(acc_gp[...], axis=0, keepdims=True)

    vec = _sds((1, d))
    in_specs = [_row(tl, d)] * 3 + [_const((1, d))] * 2
    out_shape = [_sds((n, d)), vec, vec, vec]
    out_specs = [_row(tl, d)] + [_const((1, d))] * 3
    scratch = [pltpu.VMEM((SUBLANES, d), F32)] * 2
    args = [dh, x, dxo, g, scale]
    if has_prev:
        in_specs += [_row(tl, d), _const((1, d))]
        out_shape += [_sds((n, d), BF16), vec]
        out_specs += [_row(tl, d), _const((1, d))]
        scratch += [pltpu.VMEM((SUBLANES, d), F32)]
        args += [y_prev, gate_prev]
    return _pc(body, name=name, out_shape=tuple(out_shape), grid=(nb,), in_specs=in_specs,
               out_specs=tuple(out_specs), scratch_shapes=scratch, compiler_params=_params(1))(*args)


CONV_HALO = 8
CONF_HALO = 32


def _ssd_pre(xbc, dt_raw, conv_w, conv_b, dt_bias, name):
    n, c = xbc.shape
    tl = _tile(n, STRIP_ROW_TILE)
    hb = CONV_HALO
    k_taps = 4

    nu = _strip_units_per_trip(tl)
    nt = tl // (nu * SUBLANES)

    def body(x_ref, xp_ref, dt_ref, w_ref, b_ref, dtb_ref, pre_ref, dts_ref):
        i = pl.program_id(0)
        row = lax.broadcasted_iota(jnp.int32, (SUBLANES, LANES), 0)
        for lc in range(c // LANES):
            ls = slice(lc * LANES, (lc + 1) * LANES)
            w = [_bcast_row(w_ref, k, ls) for k in range(k_taps)]
            b = _bcast_row(b_ref, 0, ls)
            xp0 = jnp.where(i > 0, xp_ref[:, ls], 0.0)

            def strip(t, carry):
                xs = _strip_units(x_ref, ls, t, nu, nt, halo_prev=xp0)
                for u in range(nu):
                    prev, cur = xs[u], xs[u + 1]
                    pre_ref[_unit_rows(t, u, nu), ls] = (
                        b + w[0] * _down(prev, cur, 3, row) + w[1] * _down(prev, cur, 2, row)
                        + w[2] * _down(prev, cur, 1, row) + w[3] * cur)
                return carry

            lax.fori_loop(0, nt, strip, 0)
        v = dt_ref[...] + dtb_ref[...]
        dts_ref[...] = jnp.maximum(v, 0.0) + jnp.log1p(jnp.exp(-jnp.abs(v)))

    return _pc(body, name=name, out_shape=(_sds((n, c)), _sds((n, LANES))), grid=(n // tl,),
               in_specs=[_row(tl, c), _prev(tl, hb, c), _row(tl, LANES), _const((SUBLANES, c)), _const((1, c)),
                         _const((1, LANES))],
               out_specs=(_row(tl, c), _row(tl, LANES)), compiler_params=_params(1))(
        xbc, xbc, dt_raw, conv_w, conv_b, dt_bias)


def _ssd_pre_bwd(dpre, xbc, ddt, dt_raw, conv_w, dt_bias, name):
    n, c = xbc.shape
    tl = _tile(n, STRIP_ROW_TILE)
    nb = n // tl
    hb = CONV_HALO
    k_taps = 4

    nu = _strip_units_per_trip(tl)
    nt = tl // (nu * SUBLANES)

    def body(dp_ref, dpn_ref, x_ref, xp_ref, ddt_ref, dt_ref, w_ref, dtb_ref,
             dx_ref, ddr_ref, dw_ref, db_ref, ddtb_ref, acc_w, acc_b, acc_dtb):
        i = pl.program_id(0)

        @pl.when(i == 0)
        def _():
            acc_w[...] = jnp.zeros_like(acc_w)
            acc_b[...] = jnp.zeros_like(acc_b)
            acc_dtb[...] = jnp.zeros_like(acc_dtb)

        row = lax.broadcasted_iota(jnp.int32, (SUBLANES, LANES), 0)
        zero = jnp.zeros((SUBLANES, LANES), F32)
        for lc in range(c // LANES):
            ls = slice(lc * LANES, (lc + 1) * LANES)
            w = [_bcast_row(w_ref, k, ls) for k in range(k_taps)]
            xp0 = jnp.where(i > 0, xp_ref[:, ls], 0.0)
            dpn0 = jnp.where(i < nb - 1, dpn_ref[:, ls], 0.0)

            def strip(t, carry):
                acc = list(carry)
                dps = _strip_units(dp_ref, ls, t, nu, nt, halo_next=dpn0)
                xs = _strip_units(x_ref, ls, t, nu, nt, halo_prev=xp0)
                dxs = []
                for u in range(nu):
                    d, dn = dps[u], dps[u + 1]
                    dxs.append(w[3] * d + w[2] * _up(d, dn, 1, row) + w[1] * _up(d, dn, 2, row)
                               + w[0] * _up(d, dn, 3, row))
                    prev, cur = xs[u], xs[u + 1]
                    acc[3] = acc[3] + d * cur
                    for k in range(3):
                        acc[k] = acc[k] + d * _down(prev, cur, 3 - k, row)
                    acc[4] = acc[4] + d
                for p in range(nu // 2):
                    dx_ref[_pair_rows(t, p, nu), ls] = jnp.concatenate(dxs[2 * p:2 * p + 2], axis=0).astype(BF16)
                return tuple(acc)

            res = lax.fori_loop(0, nt, strip, (zero,) * 5)
            for k in range(k_taps):
                acc_w[k, :, ls] += res[k]
            acc_b[:, ls] += res[4]
        ddr = ddt_ref[...] * _sig(dt_ref[...] + dtb_ref[...])
        ddr_ref[...] = ddr.astype(BF16)
        acc_dtb[...] += _fold(ddr)

        @pl.when(i == nb - 1)
        def _():
            dw_ref[...] = jnp.zeros_like(dw_ref)
            for k in range(k_taps):
                dw_ref[k:k + 1, :] = jnp.sum(acc_w[k], axis=0, keepdims=True)
            db_ref[...] = jnp.sum(acc_b[...], axis=0, keepdims=True)
            ddtb_ref[...] = jnp.sum(acc_dtb[...], axis=0, keepdims=True)

    return _pc(body, name=name,
               out_shape=(_sds((n, c), BF16), _sds((n, LANES), BF16), _sds((SUBLANES, c)), _sds((1, c)),
                          _sds((1, LANES))),
               grid=(nb,),
               in_specs=[_row(tl, c), _next(tl, hb, c, n), _row(tl, c), _prev(tl, hb, c), _row(tl, LANES),
                         _row(tl, LANES), _const((SUBLANES, c)), _const((1, LANES))],
               out_specs=(_row(tl, c), _row(tl, LANES), _const((SUBLANES, c)), _const((1, c)), _const((1, LANES))),
               scratch_shapes=[pltpu.VMEM((k_taps, SUBLANES, c), F32), pltpu.VMEM((SUBLANES, c), F32),
                               pltpu.VMEM((SUBLANES, LANES), F32)],
               compiler_params=_params(1))(dpre, dpre, xbc, xbc, ddt, dt_raw, conv_w, dt_bias)


def _expand_mat():
    r = lax.broadcasted_iota(jnp.int32, (LANES, SSD_INNER), 0)
    c = lax.broadcasted_iota(jnp.int32, (LANES, SSD_INNER), 1)
    return (jnp.right_shift(c, 6) == r).astype(BF16)


def _reduce_mat():
    r = lax.broadcasted_iota(jnp.int32, (SSD_INNER, LANES), 0)
    c = lax.broadcasted_iota(jnp.int32, (SSD_INNER, LANES), 1)
    return (jnp.right_shift(r, 6) == c).astype(BF16)


def _ssd_common(pre, dt, alog):
    q = SSD_CHUNK
    sg = _sig(pre)
    act = pre * sg
    lane = lax.broadcasted_iota(jnp.int32, (1, LANES), 1)
    a_neg = jnp.where(lane < SSD_HEADS, -jnp.exp(alog), 0.0)
    rr = lax.broadcasted_iota(jnp.int32, (q, q), 0)
    cc = lax.broadcasted_iota(jnp.int32, (q, q), 1)
    causal = rr >= cc
    cum = _sel_dot(causal.astype(BF16), dt * a_neg)
    e_mat = _expand_mat()
    dtx = _dot_sel(dt, e_mat)
    cumx = _dot_sel(cum, e_mat)
    return sg, act, a_neg, causal, cum, e_mat, dtx, cumx


def _ssd_scan(pre, dt, alog, dvec, name):
    n = pre.shape[0]
    q = SSD_CHUNK
    nc = n // q

    def body(pre_ref, dt_ref, alog_ref, d_ref, y_ref, hp_ref, state):
        i = pl.program_id(0)

        @pl.when(i == 0)
        def _():
            state[...] = jnp.zeros_like(state)

        dt_ = dt_ref[...]
        _, act, _, causal, cum, e_mat, dtx, cumx = _ssd_common(pre_ref[...], dt_, alog_ref[...])
        xs = act[:, :SSD_INNER]
        bm = act[:, SSD_INNER:SSD_INNER + LANES]
        cm = act[:, SSD_INNER + LANES:]
        cum_t = cum.T
        clx = cumx[q - 1:q, :]
        xc = xs * dtx
        xd = xc * jnp.exp(clx - cumx)
        doutx = jnp.exp(cumx)
        edec = jnp.exp(clx)
        dx_row = _dot_sel(jnp.broadcast_to(d_ref[...], (SUBLANES, LANES)), e_mat)[0:1, :]
        hp_ref[0] = state[...]
        bb = bm.astype(BF16)
        cb = cm.astype(BF16)
        lane = lax.broadcasted_iota(jnp.int32, (1, LANES), 1)
        row = lax.broadcasted_iota(jnp.int32, (LANES, 1), 0)
        cbs = []
        for g in range(2):
            cg = jnp.where(jnp.right_shift(lane, 6) == g, cm, 0.0).astype(BF16)
            cbs.append(_dot(cg, bb, NT))
        for j in range(SSD_HEADS // 2):
            sl = slice(j * LANES, (j + 1) * LANES)
            g = j // 4
            xcj = xc[:, sl].astype(BF16)
            halves = []
            for half in range(2):
                h = 2 * j + half
                seg = cum[:, h:h + 1] - cum_t[h:h + 1, :]
                w = cbs[g] * jnp.exp(jnp.where(causal, seg, -jnp.inf))
                halves.append(_dot(w.astype(BF16), xcj))
            y_diag = jnp.where(lane < SSD_HEAD_DIM, halves[0], halves[1])
            hj = state[:, sl]
            y_off = doutx[:, sl] * _dot(cb, hj.astype(BF16))
            y_ref[:, sl] = y_diag + y_off + xs[:, sl] * dx_row[:, sl]
            st = _dot(bb, xd[:, sl].astype(BF16), TN)
            state[:, sl] = hj * edec[:, sl] + jnp.where(jnp.right_shift(row, 6) == g, st, 0.0)

    return _pc(body, name=name, out_shape=(_sds((n, SSD_INNER)), _sds((nc, LANES, SSD_INNER))), grid=(nc,),
               in_specs=[_row(q, SSD_XBC), _row(q, LANES), _const((1, LANES)), _const((1, LANES))],
               out_specs=(_row(q, SSD_INNER), pl.BlockSpec((1, LANES, SSD_INNER), lambda i: (i, 0, 0))),
               scratch_shapes=[pltpu.VMEM((LANES, SSD_INNER), F32)], compiler_params=_params(1))(pre, dt, alog, dvec)


def _ssd_scan_bwd(pre, dt, hprev, dy, alog, dvec, name):
    n = pre.shape[0]
    q = SSD_CHUNK
    nc = n // q

    def body(pre_ref, dt_ref, hp_ref, dy_ref, alog_ref, d_ref, dpre_ref, ddt_ref, da_ref, dd_ref,
             d_state, dxc_s, dcx_s, dcl_s, acc_a, acc_d):
        i = pl.program_id(0)

        @pl.when(i == 0)
        def _():
            d_state[...] = jnp.zeros_like(d_state)
            acc_a[...] = jnp.zeros_like(acc_a)
            acc_d[...] = jnp.zeros_like(acc_d)

        pre_ = pre_ref[...]
        dt_ = dt_ref[...]
        sg, act, a_neg, causal, cum, e_mat, dtx, cumx = _ssd_common(pre_, dt_, alog_ref[...])
        r_mat = _reduce_mat()
        xs = act[:, :SSD_INNER]
        bm = act[:, SSD_INNER:SSD_INNER + LANES]
        cm = act[:, SSD_INNER + LANES:]
        cum_t = cum.T
        clx = cumx[q - 1:q, :]
        xc = xs * dtx
        dsx = jnp.exp(clx - cumx)
        doutx = jnp.exp(cumx)
        edec = jnp.exp(clx)
        dx_row = _dot_sel(jnp.broadcast_to(d_ref[...], (SUBLANES, LANES)), e_mat)[0:1, :]
        dy_ = dy_ref[...]
        acc_d[...] += _fold(dy_ * xs)
        bb = bm.astype(BF16)
        cb = cm.astype(BF16)
        lane = lax.broadcasted_iota(jnp.int32, (1, LANES), 1)
        row = lax.broadcasted_iota(jnp.int32, (LANES, 1), 0)
        d_c = jnp.zeros((q, LANES), F32)
        d_b = jnp.zeros((q, LANES), F32)

        for j in range(SSD_HEADS // 2):
            sl = slice(j * LANES, (j + 1) * LANES)
            g = j // 4
            hj = hp_ref[0, :, sl]
            hjb = hj.astype(BF16)
            dyj = dy_[:, sl]
            tj = _dot(cb, hjb)
            dtj = (doutx[:, sl] * dyj).astype(BF16)
            dcx = dyj * tj * doutx[:, sl]
            d_c = d_c + _dot(dtj, hjb, NT)
            dhn = d_state[:, sl]
            dhp = dhn * edec[:, sl] + jnp.where(jnp.right_shift(row, 6) == g, _dot(cb, dtj, TN), 0.0)
            dcl = jnp.sum(dhn * hj, axis=0, keepdims=True) * edec[:, sl]
            dsb = dhn.astype(BF16)
            dxd = _dot(bb, dsb)
            xcj = xc[:, sl]
            dsj = dsx[:, sl]
            d_b = d_b + _dot((xcj * dsj).astype(BF16), dsb, NT)
            dds = dxd * xcj * dsj
            d_state[:, sl] = dhp
            dxc_s[:, sl] = dxd * dsj
            dcx_s[:, sl] = dcx - dds
            dcl_s[:, sl] = jnp.broadcast_to(dcl + jnp.sum(dds, axis=0, keepdims=True), (SUBLANES, LANES))

        dcum_c = jnp.zeros((q, LANES), F32)
        dcum_t = jnp.zeros((LANES, q), F32)
        for g in range(2):
            gmask = jnp.right_shift(lane, 6) == g
            cg = jnp.where(gmask, cm, 0.0).astype(BF16)
            cbg = _dot(cg, bb, NT)
            d_cb = jnp.zeros((q, q), F32)
            for hh in range(SSD_HEADS // 2):
                h = g * (SSD_HEADS // 2) + hh
                j, half = h // 2, h % 2
                sl = slice(j * LANES, (j + 1) * LANES)
                hmask = jnp.right_shift(lane, 6) == half
                seg = cum[:, h:h + 1] - cum_t[h:h + 1, :]
                lm = jnp.exp(jnp.where(causal, seg, -jnp.inf))
                w = cbg * lm
                dyj = dy_[:, sl]
                dw = _dot(jnp.where(hmask, dyj, 0.0).astype(BF16), xc[:, sl].astype(BF16), NT)
                dxch = _dot(w.astype(BF16), dyj.astype(BF16), TN)
                dxc_s[:, sl] += jnp.where(hmask, dxch, 0.0)
                d_cb = d_cb + dw * lm
                m = dw * w
                dcum_c = dcum_c + jnp.sum(m, axis=1, keepdims=True) * (lane == h).astype(F32)
                dcum_t = dcum_t + (row == h).astype(F32) * jnp.sum(m, axis=0, keepdims=True)
            d_cbb = d_cb.astype(BF16)
            d_c = d_c + jnp.where(gmask, _dot(d_cbb, bb), 0.0)
            d_b = d_b + jnp.where(gmask, _dot(d_cbb, cb, TN), 0.0)

        dcl_row = _dot_sel(dcl_s[...], r_mat)[0:1, :]
        rowq = lax.broadcasted_iota(jnp.int32, (q, 1), 0)
        dcum = (dcum_c - dcum_t.T + _dot_sel(dcx_s[...], r_mat)
                + jnp.where(rowq == q - 1, dcl_row, 0.0))
        rr = lax.broadcasted_iota(jnp.int32, (q, q), 0)
        cc = lax.broadcasted_iota(jnp.int32, (q, q), 1)
        dadt = _sel_dot((rr <= cc).astype(BF16), dcum)
        dxc = dxc_s[...]
        ddt_ref[...] = dadt * a_neg + _dot_sel(dxc * xs, r_mat)
        acc_a[...] += _fold(dadt * dt_)
        dsilu = sg * (1.0 + pre_ * (1.0 - sg))
        dpre_ref[:, :SSD_INNER] = (dxc * dtx + dy_ * dx_row) * dsilu[:, :SSD_INNER]
        dpre_ref[:, SSD_INNER:SSD_INNER + LANES] = d_b * dsilu[:, SSD_INNER:SSD_INNER + LANES]
        dpre_ref[:, SSD_INNER + LANES:] = d_c * dsilu[:, SSD_INNER + LANES:]

        @pl.when(i == nc - 1)
        def _():
            da_ref[...] = jnp.sum(acc_a[...], axis=0, keepdims=True) * a_neg
            dd_ref[...] = jnp.sum(_dot_sel(acc_d[...], r_mat), axis=0, keepdims=True)

    rev = lambda i: (nc - 1 - i, 0)
    return _pc(body, name=name,
               out_shape=(_sds((n, SSD_XBC)), _sds((n, LANES)), _sds((1, LANES)), _sds((1, LANES))), grid=(nc,),
               in_specs=[pl.BlockSpec((q, SSD_XBC), rev), pl.BlockSpec((q, LANES), rev),
                         pl.BlockSpec((1, LANES, SSD_INNER), lambda i: (nc - 1 - i, 0, 0)),
                         pl.BlockSpec((q, SSD_INNER), rev), _const((1, LANES)), _const((1, LANES))],
               out_specs=(pl.BlockSpec((q, SSD_XBC), rev), pl.BlockSpec((q, LANES), rev), _const((1, LANES)),
                          _const((1, LANES))),
               scratch_shapes=[pltpu.VMEM((LANES, SSD_INNER), F32), pltpu.VMEM((q, SSD_INNER), F32),
                               pltpu.VMEM((q, SSD_INNER), F32), pltpu.VMEM((SUBLANES, SSD_INNER), F32),
                               pltpu.VMEM((SUBLANES, LANES), F32), pltpu.VMEM((SUBLANES, SSD_INNER), F32)],
               compiler_params=_params(1))(pre, dt, hprev, dy, alog, dvec)


def _group_norm_parts(v):
    half = SSD_INNER // 2
    r0 = lax.rsqrt(jnp.mean(v[:, :half] * v[:, :half], axis=-1, keepdims=True) + EPS)
    r1 = lax.rsqrt(jnp.mean(v[:, half:] * v[:, half:], axis=-1, keepdims=True) + EPS)
    lane = lax.broadcasted_iota(jnp.int32, (1, SSD_INNER), 1)
    return jnp.where(lane < half, r0, r1)


def _group_mean(v):
    half = SSD_INNER // 2
    m0 = jnp.mean(v[:, :half], axis=-1, keepdims=True)
    m1 = jnp.mean(v[:, half:], axis=-1, keepdims=True)
    lane = lax.broadcasted_iota(jnp.int32, (1, SSD_INNER), 1)
    return jnp.where(lane < half, m0, m1)


def _ssd_post(y, z, g, name):
    n, d = y.shape
    tl = _tile(n)

    def body(y_ref, z_ref, g_ref, o_ref):
        z_ = z_ref[...]
        v = y_ref[...] * (z_ * _sig(z_))
        o_ref[...] = ((v * _group_norm_parts(v)) * g_ref[...]).astype(BF16)

    return _pc(body, name=name, out_shape=_sds((n, d), BF16), grid=(n // tl,),
               in_specs=[_row(tl, d), _row(tl, d), _const((1, d))], out_specs=_row(tl, d),
               compiler_params=_params(1))(y, z, g)


def _ssd_post_bwd(dout, y, z, g, name):
    n, d = y.shape
    tl = _tile(n)
    nb = n // tl

    def body(do_ref, y_ref, z_ref, g_ref, dy_ref, dz_ref, dg_ref, acc_g):
        i = pl.program_id(0)

        @pl.when(i == 0)
        def _():
            acc_g[...] = jnp.zeros_like(acc_g)

        z_ = z_ref[...]
        y_ = y_ref[...]
        sz = _sig(z_)
        silu_z = z_ * sz
        v = y_ * silu_z
        rs = _group_norm_parts(v)
        nv = v * rs
        do_ = do_ref[...]
        acc_g[...] += _fold(do_ * nv)
        dn = do_ * g_ref[...]
        dv = rs * (dn - nv * _group_mean(dn * nv))
        dy_ref[...] = dv * silu_z
        dz_ref[...] = (dv * y_ * (sz * (1.0 + z_ * (1.0 - sz)))).astype(BF16)

        @pl.when(i == nb - 1)
        def _():
            dg_ref[...] = jnp.sum(acc_g[...], axis=0, keepdims=True)

    return _pc(body, name=name, out_shape=(_sds((n, d)), _sds((n, d), BF16), _sds((1, d))), grid=(nb,),
               in_specs=[_row(tl, d), _row(tl, d), _row(tl, d), _const((1, d))],
               out_specs=(_row(tl, d), _row(tl, d), _const((1, d))),
               scratch_shapes=[pltpu.VMEM((SUBLANES, d), F32)], compiler_params=_params(1))(dout, y, z, g)


def _layer_norm_parts(uc):
    mu = jnp.mean(uc, axis=-1, keepdims=True)
    xc = uc - mu
    rstd = lax.rsqrt(jnp.mean(xc * xc, axis=-1, keepdims=True) + EPS)
    return xc * rstd, rstd


def _rows_x8(w):
    return jnp.broadcast_to(w[:, None, :], (w.shape[0], SUBLANES, w.shape[1]))


def _glu_units(x_ref, ls, gl, t, nu, halo):
    nh = len(halo)
    units = []
    for h in range(nh):
        rows = pl.ds(pl.multiple_of(jnp.maximum(t * nu - nh + h, 0) * SUBLANES, SUBLANES), SUBLANES)
        units.append(jnp.where(t > 0, x_ref[rows, ls] * _sig(x_ref[rows, gl]), halo[h]))
    for u in range(nu):
        rows = _unit_rows(t, u, nu)
        units.append(x_ref[rows, ls] * _sig(x_ref[rows, gl]))
    return units


def _memo_rolls(units):
    memo = {}

    def rolls(key):
        if key not in memo:
            memo[key] = pltpu.roll(units[key[0]], key[1], 0)
        return memo[key]

    return rolls


def _window(units, e, s, rolls, row, up):
    a, b = divmod(s, SUBLANES)
    if b == 0:
        return units[e + a] if up else units[e - a]
    if up:
        sh = SUBLANES - b
        return jnp.where(row < sh, rolls((e + a, sh)), rolls((e + a + 1, sh)))
    return jnp.where(row < b, rolls((e - a - 1, b)), rolls((e - a, b)))


def _conf_fwd(conf_in, conv_w, conv_b, ln_g, ln_b, name):
    n = conf_in.shape[0]
    c = CONF_WIDTH
    tl = _tile(n, STRIP_ROW_TILE)
    hb = CONF_HALO
    k_taps = CONF_KERNEL

    nu = _strip_units_per_trip(tl)
    nt = tl // (nu * SUBLANES)
    nh = hb // SUBLANES

    def body(x_ref, xp_ref, w_ref, b_ref, g_ref, beta_ref, o_ref, uc_ref):
        i = pl.program_id(0)
        row = lax.broadcasted_iota(jnp.int32, (SUBLANES, LANES), 0)
        for lc in range(c // LANES):
            ls = slice(lc * LANES, (lc + 1) * LANES)
            gl = slice(c + lc * LANES, c + (lc + 1) * LANES)
            bias = _bcast_row(b_ref, 0, ls)
            halo = [jnp.where(i > 0, xp_ref[SUBLANES * h:SUBLANES * (h + 1), ls]
                              * _sig(xp_ref[SUBLANES * h:SUBLANES * (h + 1), gl]), 0.0) for h in range(nh)]

            def strip(t, carry):
                units = _glu_units(x_ref, ls, gl, t, nu, halo)
                rolls = _memo_rolls(units)
                for u in range(nu):
                    acc = bias
                    for k in range(k_taps):
                        acc = acc + w_ref[k, :, ls] * _window(units, u + nh, k_taps - 1 - k, rolls, row, up=False)
                    uc_ref[_unit_rows(t, u, nu), ls] = acc
                return carry

            lax.fori_loop(0, nt, strip, 0)
        nv, _ = _layer_norm_parts(uc_ref[...])
        v = nv * g_ref[...] + beta_ref[...]
        o_ref[...] = (v * _sig(v)).astype(BF16)

    return _pc(body, name=name, out_shape=(_sds((n, c), BF16), _sds((n, c))), grid=(n // tl,),
               in_specs=[_row(tl, 2 * c), _prev(tl, hb, 2 * c), _const((hb, SUBLANES, c)), _const((1, c)),
                         _const((1, c)), _const((1, c))],
               out_specs=(_row(tl, c), _row(tl, c)), compiler_params=_params(1))(
        conf_in, conf_in, _rows_x8(conv_w), conv_b, ln_g, ln_b)


def _conf_bwd(dout, uc, conf_in, conv_w, ln_g, ln_b, name):
    n = conf_in.shape[0]
    c = CONF_WIDTH
    tl = _tile(n, STRIP_ROW_TILE)
    nb = n // tl
    hb = CONF_HALO
    k_taps = CONF_KERNEL

    nu = _strip_units_per_trip(tl)
    nt = tl // (nu * SUBLANES)
    nh = hb // SUBLANES

    def body(do_ref, don_ref, uc_ref, ucn_ref, x_ref, xp_ref, w_ref, g_ref, beta_ref,
             dx_ref, dw_ref, db_ref, dg_ref, dbeta_ref, dbuf, acc_w, acc_b, acc_g, acc_beta):
        i = pl.program_id(0)

        @pl.when(i == 0)
        def _():
            acc_w[...] = jnp.zeros_like(acc_w)
            acc_b[...] = jnp.zeros_like(acc_b)
            acc_g[...] = jnp.zeros_like(acc_g)
            acc_beta[...] = jnp.zeros_like(acc_beta)

        g_ = g_ref[...]
        beta_ = beta_ref[...]

        def d_conv_out(do_, uc_):
            nv, rstd = _layer_norm_parts(uc_)
            v = nv * g_ + beta_
            sv = _sig(v)
            dv = do_ * (sv * (1.0 + v * (1.0 - sv)))
            dn = dv * g_
            duc = rstd * (dn - jnp.mean(dn, axis=-1, keepdims=True)
                          - nv * jnp.mean(dn * nv, axis=-1, keepdims=True))
            return duc, dv, nv

        duc, dv, nv = d_conv_out(do_ref[...], uc_ref[...])
        acc_g[...] += _fold(dv * nv)
        acc_beta[...] += _fold(dv)
        acc_b[...] += _fold(duc)
        dbuf[pl.ds(0, tl), :] = duc
        ducn, _, _ = d_conv_out(don_ref[...], ucn_ref[...])
        dbuf[pl.ds(tl, hb), :] = jnp.where(i < nb - 1, ducn, 0.0)

        row = lax.broadcasted_iota(jnp.int32, (SUBLANES, LANES), 0)
        for lc in range(c // LANES):
            ls = slice(lc * LANES, (lc + 1) * LANES)
            gl = slice(c + lc * LANES, c + (lc + 1) * LANES)
            halo = [jnp.where(i > 0, xp_ref[SUBLANES * h:SUBLANES * (h + 1), ls]
                              * _sig(xp_ref[SUBLANES * h:SUBLANES * (h + 1), gl]), 0.0) for h in range(nh)]

            def strip(t, carry):
                us = _glu_units(x_ref, ls, gl, t, nu, halo)
                ds = [dbuf[pl.ds(pl.multiple_of((t * nu + u) * SUBLANES, SUBLANES), SUBLANES), ls]
                      for u in range(nu + nh)]
                u_rolls, d_rolls = _memo_rolls(us), _memo_rolls(ds)
                for k in range(k_taps):
                    part = ds[0] * _window(us, nh, k_taps - 1 - k, u_rolls, row, up=False)
                    for u in range(1, nu):
                        part = part + ds[u] * _window(us, u + nh, k_taps - 1 - k, u_rolls, row, up=False)
                    acc_w[k, :, ls] += part
                dus = []
                for u in range(nu):
                    du = w_ref[0, :, ls] * _window(ds, u, k_taps - 1, d_rolls, row, up=True)
                    for k in range(1, k_taps):
                        du = du + w_ref[k, :, ls] * _window(ds, u, k_taps - 1 - k, d_rolls, row, up=True)
                    dus.append(du)
                for p in range(nu // 2):
                    rows = _pair_rows(t, p, nu)
                    du2 = jnp.concatenate(dus[2 * p:2 * p + 2], axis=0)
                    val, sgate = x_ref[rows, ls], _sig(x_ref[rows, gl])
                    dx_ref[rows, ls] = (du2 * sgate).astype(BF16)
                    dx_ref[rows, gl] = (du2 * val * sgate * (1.0 - sgate)).astype(BF16)
                return carry

            lax.fori_loop(0, nt, strip, 0)

        @pl.when(i == nb - 1)
        def _():
            dw_ref[...] = jnp.zeros_like(dw_ref)
            for k in range(k_taps):
                dw_ref[k:k + 1, :] = jnp.sum(acc_w[k], axis=0, keepdims=True)
            db_ref[...] = jnp.sum(acc_b[...], axis=0, keepdims=True)
            dg_ref[...] = jnp.sum(acc_g[...], axis=0, keepdims=True)
            dbeta_ref[...] = jnp.sum(acc_beta[...], axis=0, keepdims=True)

    vec = _sds((1, c))
    return _pc(body, name=name, out_shape=(_sds((n, 2 * c), BF16), _sds((hb, c)), vec, vec, vec), grid=(nb,),
               in_specs=[_row(tl, c), _next(tl, hb, c, n), _row(tl, c), _next(tl, hb, c, n), _row(tl, 2 * c),
                         _prev(tl, hb, 2 * c), _const((hb, SUBLANES, c)), _const((1, c)), _const((1, c))],
               out_specs=(_row(tl, 2 * c), _const((hb, c)), _const((1, c)), _const((1, c)), _const((1, c))),
               scratch_shapes=[pltpu.VMEM((tl + hb, c), F32),
                               pltpu.VMEM((k_taps, SUBLANES, c), F32), pltpu.VMEM((SUBLANES, c), F32),
                               pltpu.VMEM((SUBLANES, c), F32), pltpu.VMEM((SUBLANES, c), F32)],
               compiler_params=_params(1))(dout, dout, uc, uc, conf_in, conf_in, _rows_x8(conv_w), ln_g, ln_b)


def _sc_fwd(sc_in, conv_w, name):
    n = sc_in.shape[0]
    c = SC_WIDTH
    tl = _tile(n, STRIP_ROW_TILE)
    hb = CONV_HALO

    nu = _strip_units_per_trip(tl)
    nt = tl // (nu * SUBLANES)

    def body(x_ref, xp_ref, w_ref, o_ref):
        i = pl.program_id(0)
        row = lax.broadcasted_iota(jnp.int32, (SUBLANES, LANES), 0)
        for lc in range(c // LANES):
            ls = slice(lc * LANES, (lc + 1) * LANES)
            gc_ls = slice(c + lc * LANES, c + (lc + 1) * LANES)
            xv_ls = slice(2 * c + lc * LANES, 2 * c + (lc + 1) * LANES)
            w = [_bcast_row(w_ref, k, ls) for k in range(3)]
            gc0 = jnp.where(i > 0, xp_ref[:, gc_ls], 0.0)
            xv0 = xp_ref[:, xv_ls]

            def strip(t, carry):
                gcs = _strip_units(x_ref, gc_ls, t, nu, nt, halo_prev=gc0)
                xvs = _strip_units(x_ref, xv_ls, t, nu, nt, halo_prev=xv0)
                ps = [a * b for a, b in zip(gcs, xvs)]
                outs = []
                for u in range(nu):
                    prev, cur = ps[u], ps[u + 1]
                    cv = w[0] * _down(prev, cur, 2, row) + w[1] * _down(prev, cur, 1, row) + w[2] * cur
                    outs.append(x_ref[_unit_rows(t, u, nu), ls] * cv)
                for p in range(nu // 2):
                    o_ref[_pair_rows(t, p, nu), ls] = jnp.concatenate(outs[2 * p:2 * p + 2], axis=0).astype(BF16)
                return carry

            lax.fori_loop(0, nt, strip, 0)

    return _pc(body, name=name, out_shape=_sds((n, c), BF16), grid=(n // tl,),
               in_specs=[_row(tl, 3 * c), _prev(tl, hb, 3 * c), _const((SUBLANES, c))], out_specs=_row(tl, c),
               compiler_params=_params(1))(sc_in, sc_in, conv_w)


def _sc_bwd(dout, sc_in, conv_w, name):
    n = sc_in.shape[0]
    c = SC_WIDTH
    tl = _tile(n, STRIP_ROW_TILE)
    nb = n // tl
    hb = CONV_HALO

    nu = _strip_units_per_trip(tl)
    nt = tl // (nu * SUBLANES)

    def body(do_ref, don_ref, x_ref, xp_ref, xn_ref, w_ref, dx_ref, dw_ref, acc_w):
        i = pl.program_id(0)

        @pl.when(i == 0)
        def _():
            acc_w[...] = jnp.zeros_like(acc_w)

        row = lax.broadcasted_iota(jnp.int32, (SUBLANES, LANES), 0)
        zero = jnp.zeros((SUBLANES, LANES), F32)
        for lc in range(c // LANES):
            ls = slice(lc * LANES, (lc + 1) * LANES)
            gc_ls = slice(c + lc * LANES, c + (lc + 1) * LANES)
            xv_ls = slice(2 * c + lc * LANES, 2 * c + (lc + 1) * LANES)
            w = [_bcast_row(w_ref, k, ls) for k in range(3)]
            gc0 = jnp.where(i > 0, xp_ref[:, gc_ls], 0.0)
            xv0 = xp_ref[:, xv_ls]
            don0 = jnp.where(i < nb - 1, don_ref[:, ls], 0.0)
            gbn0 = xn_ref[:, ls]

            def strip(t, carry):
                acc = list(carry)
                dos = _strip_units(do_ref, ls, t, nu, nt, halo_next=don0)
                gbs = _strip_units(x_ref, ls, t, nu, nt, halo_next=gbn0)
                gcs = _strip_units(x_ref, gc_ls, t, nu, nt, halo_prev=gc0)
                xvs = _strip_units(x_ref, xv_ls, t, nu, nt, halo_prev=xv0)
                dcv = [a * b for a, b in zip(dos, gbs)]
                ps = [a * b for a, b in zip(gcs, xvs)]
                d_gb, d_gc, d_xv = [], [], []
                for u in range(nu):
                    prev, cur = ps[u], ps[u + 1]
                    p1, p2 = _down(prev, cur, 1, row), _down(prev, cur, 2, row)
                    d, dn = dcv[u], dcv[u + 1]
                    dp = w[2] * d + w[1] * _up(d, dn, 1, row) + w[0] * _up(d, dn, 2, row)
                    d_gb.append(dos[u] * (w[0] * p2 + w[1] * p1 + w[2] * cur))
                    d_gc.append(dp * xvs[u + 1])
                    d_xv.append(dp * gcs[u + 1])
                    acc[0] = acc[0] + d * p2
                    acc[1] = acc[1] + d * p1
                    acc[2] = acc[2] + d * cur
                for p in range(nu // 2):
                    rows = _pair_rows(t, p, nu)
                    for vals, lanes in ((d_gb, ls), (d_gc, gc_ls), (d_xv, xv_ls)):
                        dx_ref[rows, lanes] = jnp.concatenate(vals[2 * p:2 * p + 2], axis=0).astype(BF16)
                return tuple(acc)

            res = lax.fori_loop(0, nt, strip, (zero,) * 3)
            for k in range(3):
                acc_w[k, :, ls] += res[k]

        @pl.when(i == nb - 1)
        def _():
            dw_ref[...] = jnp.zeros_like(dw_ref)
            for k in range(3):
                dw_ref[k:k + 1, :] = jnp.sum(acc_w[k], axis=0, keepdims=True)

    return _pc(body, name=name, out_shape=(_sds((n, 3 * c), BF16), _sds((SUBLANES, c))), grid=(nb,),
               in_specs=[_row(tl, c), _next(tl, hb, c, n), _row(tl, 3 * c), _prev(tl, hb, 3 * c),
                         _next(tl, hb, c, n), _const((SUBLANES, c))],
               out_specs=(_row(tl, 3 * c), _const((SUBLANES, c))),
               scratch_shapes=[pltpu.VMEM((3, SUBLANES, c), F32)],
               compiler_params=_params(1))(dout, dout, sc_in, sc_in, sc_in, conv_w)


def _merge_fwd(gates, ya, yb, yc, b_gate, name):
    n, d = ya.shape
    tl = _tile(n)

    def body(gt_ref, ya_ref, yb_ref, yc_ref, b_ref, o_ref):
        gt = _sig(gt_ref[...] + b_ref[...])
        o_ref[...] = (gt[:, :d] * ya_ref[...] + gt[:, d:2 * d] * yb_ref[...] + gt[:, 2 * d:] * yc_ref[...]).astype(BF16)

    return _pc(body, name=name, out_shape=_sds((n, d), BF16), grid=(n // tl,),
               in_specs=[_row(tl, 3 * d), _row(tl, d), _row(tl, d), _row(tl, d), _const((1, 3 * d))],
               out_specs=_row(tl, d), compiler_params=_params(1))(gates, ya, yb, yc, b_gate)


def _merge_bwd(dm, gates, ya, yb, yc, b_gate, name):
    n, d = ya.shape
    tl = _tile(n)
    nb = n // tl

    def body(dm_ref, gt_ref, ya_ref, yb_ref, yc_ref, b_ref, dya_ref, dyb_ref, dyc_ref, dgt_ref, db_ref, acc):
        i = pl.program_id(0)

        @pl.when(i == 0)
        def _():
            acc[...] = jnp.zeros_like(acc)

        dm_ = dm_ref[...]
        gt = _sig(gt_ref[...] + b_ref[...])
        for idx, (y_ref, dy_ref) in enumerate(((ya_ref, dya_ref), (yb_ref, dyb_ref), (yc_ref, dyc_ref))):
            gk = gt[:, idx * d:(idx + 1) * d]
            dy_ref[...] = (dm_ * gk).astype(BF16)
            dpre = dm_ * y_ref[...] * gk * (1.0 - gk)
            dgt_ref[:, idx * d:(idx + 1) * d] = dpre.astype(BF16)
            acc[:, idx * d:(idx + 1) * d] += _fold(dpre)

        @pl.when(i == nb - 1)
        def _():
            db_ref[...] = jnp.sum(acc[...], axis=0, keepdims=True)

    bf = _sds((n, d), BF16)
    return _pc(body, name=name, out_shape=(bf, bf, bf, _sds((n, 3 * d), BF16), _sds((1, 3 * d))), grid=(nb,),
               in_specs=[_row(tl, d), _row(tl, 3 * d), _row(tl, d), _row(tl, d), _row(tl, d), _const((1, 3 * d))],
               out_specs=(_row(tl, d), _row(tl, d), _row(tl, d), _row(tl, 3 * d), _const((1, 3 * d))),
               scratch_shapes=[pltpu.VMEM((SUBLANES, 3 * d), F32)], compiler_params=_params(1))(
        dm, gates, ya, yb, yc, b_gate)


FFN_COLS = 1408
FFN_STRIP = 64
FFN_STRIP_BWD = 32


def _down(prev, cur, s, row):
    return jnp.where(row < s, pltpu.roll(prev, s, 0), pltpu.roll(cur, s, 0))


def _up(cur, nxt, s, row):
    return jnp.where(row < SUBLANES - s, pltpu.roll(cur, SUBLANES - s, 0), pltpu.roll(nxt, SUBLANES - s, 0))


def _ffn_mid(up, conv_w, conv_b, name):
    n = up.shape[0]
    tl = _tile(n, STRIP_ROW_TILE)
    hb = CONV_HALO
    tc = FFN_COLS
    ncb = D_FF // tc

    def spec(shape_rows, idx_fn, off):
        return pl.BlockSpec((shape_rows, tc), lambda j, i, off=off: (idx_fn(i), j + off))

    r = tl // hb
    cur = lambda i: i
    prv = lambda i: jnp.maximum(i * r - 1, 0)

    def body(g_ref, gp_ref, v_ref, vp_ref, wg_ref, wv_ref, bg_ref, bv_ref, o_ref):
        i = pl.program_id(1)
        row = lax.broadcasted_iota(jnp.int32, (SUBLANES, LANES), 0)
        full = lambda ref, k, ls: jnp.broadcast_to(ref[k:k + 1, ls], (SUBLANES, LANES))
        for lc in range(tc // LANES):
            ls = slice(lc * LANES, (lc + 1) * LANES)
            wg = [full(wg_ref, k, ls) for k in range(3)]
            wv = [full(wv_ref, k, ls) for k in range(3)]
            bg, bv = full(bg_ref, 0, ls), full(bv_ref, 0, ls)

            def conv(prev, x, w, b):
                return b + w[0] * _down(prev, x, 2, row) + w[1] * _down(prev, x, 1, row) + w[2] * x

            def strip(t, carry):
                gs, vs = [carry[0]], [carry[1]]
                for u in range(FFN_STRIP // SUBLANES):
                    rows = pl.ds(pl.multiple_of(t * FFN_STRIP + u * SUBLANES, SUBLANES), SUBLANES)
                    gs.append(g_ref[rows, ls])
                    vs.append(v_ref[rows, ls])
                outs = []
                for u in range(FFN_STRIP // SUBLANES):
                    ug = conv(gs[u], gs[u + 1], wg, bg)
                    outs.append(ug * _sig(ug) * conv(vs[u], vs[u + 1], wv, bv))
                for p in range(FFN_STRIP // 16):
                    rows = pl.ds(pl.multiple_of(t * FFN_STRIP + p * 16, 16), 16)
                    o_ref[rows, ls] = jnp.concatenate(outs[2 * p:2 * p + 2], axis=0).astype(BF16)
                return gs[-1], vs[-1]

            lax.fori_loop(0, tl // FFN_STRIP, strip,
                          (jnp.where(i > 0, gp_ref[:, ls], 0.0), jnp.where(i > 0, vp_ref[:, ls], 0.0)))

    wspec = lambda off: pl.BlockSpec((SUBLANES, tc), lambda j, i, off=off: (0, j + off))
    bspec = lambda off: pl.BlockSpec((1, tc), lambda j, i, off=off: (0, j + off))
    return _pc(body, name=name, out_shape=_sds((n, D_FF), BF16), grid=(ncb, n // tl),
               in_specs=[spec(tl, cur, 0), spec(hb, prv, 0), spec(tl, cur, ncb), spec(hb, prv, ncb),
                         wspec(0), wspec(ncb), bspec(0), bspec(ncb)],
               out_specs=pl.BlockSpec((tl, tc), lambda j, i: (i, j)), compiler_params=_params(2))(
        up, up, up, up, conv_w, conv_w, conv_b, conv_b)


def _ffn_mid_bwd(da, up, conv_w, conv_b, name):
    n = up.shape[0]
    tl = _tile(n, STRIP_ROW_TILE)
    nb = n // tl
    nt = tl // FFN_STRIP_BWD
    hb = CONV_HALO
    tc = FFN_COLS
    ncb = D_FF // tc
    r = tl // hb
    last = n // hb - 1
    cur = lambda i: i
    prv = lambda i: jnp.maximum(i * r - 1, 0)
    nxt = lambda i: jnp.minimum((i + 1) * r, last)

    def spec(shape_rows, idx_fn, off):
        return pl.BlockSpec((shape_rows, tc), lambda j, i, off=off: (idx_fn(i), j + off))

    def body(da_ref, dan_ref, g_ref, gp_ref, gn_ref, v_ref, vp_ref, vn_ref, wg_ref, wv_ref, bg_ref, bv_ref,
             dg_ref, dv_ref, dwg_ref, dwv_ref, dbg_ref, dbv_ref, acc_w, acc_b):
        i = pl.program_id(1)

        @pl.when(i == 0)
        def _():
            acc_w[...] = jnp.zeros_like(acc_w)
            acc_b[...] = jnp.zeros_like(acc_b)

        row = lax.broadcasted_iota(jnp.int32, (SUBLANES, LANES), 0)
        full = lambda ref, k, ls: jnp.broadcast_to(ref[k:k + 1, ls], (SUBLANES, LANES))
        zero = jnp.zeros((SUBLANES, LANES), F32)

        def d_conv_out(da_, ug, uv):
            s = _sig(ug)
            return da_ * uv * (s * (1.0 + ug * (1.0 - s))), da_ * (ug * s)

        for lc in range(tc // LANES):
            ls = slice(lc * LANES, (lc + 1) * LANES)
            wg = [full(wg_ref, k, ls) for k in range(3)]
            wv = [full(wv_ref, k, ls) for k in range(3)]
            bg, bv = full(bg_ref, 0, ls), full(bv_ref, 0, ls)

            def unit(prev_g, g, prev_v, v, da_):
                g1, g2 = _down(prev_g, g, 1, row), _down(prev_g, g, 2, row)
                v1, v2 = _down(prev_v, v, 1, row), _down(prev_v, v, 2, row)
                ug = bg + wg[0] * g2 + wg[1] * g1 + wg[2] * g
                uv = bv + wv[0] * v2 + wv[1] * v1 + wv[2] * v
                dug, duv = d_conv_out(da_, ug, uv)
                return dug, duv, (g2, g1, g), (v2, v1, v)

            def d_in(d, d_next, w):
                return w[2] * d + w[1] * _up(d, d_next, 1, row) + w[0] * _up(d, d_next, 2, row)

            tail = pl.ds(tl - SUBLANES, SUBLANES)
            dgn, dvn, _, _ = unit(g_ref[tail, ls], gn_ref[:, ls], v_ref[tail, ls], vn_ref[:, ls], dan_ref[:, ls])
            dgn = jnp.where(i < nb - 1, dgn, 0.0)
            dvn = jnp.where(i < nb - 1, dvn, 0.0)
            gp0 = jnp.where(i > 0, gp_ref[:, ls], 0.0)
            vp0 = jnp.where(i > 0, vp_ref[:, ls], 0.0)

            def strip(tt, carry):
                dgn, dvn = carry[0], carry[1]
                aw, ab = list(carry[2:8]), list(carry[8:10])
                t = nt - 1 - tt
                nu = FFN_STRIP_BWD // SUBLANES
                rm = pl.multiple_of(jnp.maximum(t * FFN_STRIP_BWD - SUBLANES, 0), SUBLANES)
                gs = [jnp.where(t > 0, g_ref[pl.ds(rm, SUBLANES), ls], gp0)]
                vs = [jnp.where(t > 0, v_ref[pl.ds(rm, SUBLANES), ls], vp0)]
                das = []
                for u in range(nu):
                    rows = pl.ds(pl.multiple_of(t * FFN_STRIP_BWD + u * SUBLANES, SUBLANES), SUBLANES)
                    gs.append(g_ref[rows, ls])
                    vs.append(v_ref[rows, ls])
                    das.append(da_ref[rows, ls])
                dgs, dvs = [None] * nu + [dgn], [None] * nu + [dvn]
                for u in reversed(range(nu)):
                    dgs[u], dvs[u], gsh, vsh = unit(gs[u], gs[u + 1], vs[u], vs[u + 1], das[u])
                    for k in range(3):
                        aw[k] = aw[k] + dgs[u] * gsh[k]
                        aw[3 + k] = aw[3 + k] + dvs[u] * vsh[k]
                    ab[0] = ab[0] + dgs[u]
                    ab[1] = ab[1] + dvs[u]
                for p in range(nu // 2):
                    rows = pl.ds(pl.multiple_of(t * FFN_STRIP_BWD + p * 16, 16), 16)
                    dg_ref[rows, ls] = jnp.concatenate([d_in(dgs[2 * p], dgs[2 * p + 1], wg),
                                                        d_in(dgs[2 * p + 1], dgs[2 * p + 2], wg)], axis=0).astype(BF16)
                    dv_ref[rows, ls] = jnp.concatenate([d_in(dvs[2 * p], dvs[2 * p + 1], wv),
                                                        d_in(dvs[2 * p + 1], dvs[2 * p + 2], wv)], axis=0).astype(BF16)
                return (dgs[0], dvs[0], *aw, *ab)

            res = lax.fori_loop(0, nt, strip, (dgn, dvn) + (zero,) * 8)
            for k in range(3):
                acc_w[0, k, :, ls] += res[2 + k]
                acc_w[1, k, :, ls] += res[5 + k]
            acc_b[0, :, ls] += res[8]
            acc_b[1, :, ls] += res[9]

        @pl.when(i == nb - 1)
        def _():
            for t, (dw_ref, db_ref) in enumerate(((dwg_ref, dbg_ref), (dwv_ref, dbv_ref))):
                dw_ref[...] = jnp.zeros_like(dw_ref)
                for k in range(3):
                    dw_ref[k:k + 1, :] = jnp.sum(acc_w[t, k], axis=0, keepdims=True)
                db_ref[...] = jnp.sum(acc_b[t], axis=0, keepdims=True)

    wspec = lambda off: pl.BlockSpec((SUBLANES, tc), lambda j, i, off=off: (0, j + off))
    bspec = lambda off: pl.BlockSpec((1, tc), lambda j, i, off=off: (0, j + off))
    ospec = lambda off: pl.BlockSpec((tl, tc), lambda j, i, off=off: (i, j + off))
    dg, dv, dwg, dwv, dbg, dbv = _pc(
        body, name=name,
        out_shape=(_sds((n, D_FF), BF16), _sds((n, D_FF), BF16), _sds((SUBLANES, D_FF)), _sds((SUBLANES, D_FF)),
                   _sds((1, D_FF)), _sds((1, D_FF))),
        grid=(ncb, nb),
        in_specs=[spec(tl, cur, 0), spec(hb, nxt, 0),
                  spec(tl, cur, 0), spec(hb, prv, 0), spec(hb, nxt, 0),
                  spec(tl, cur, ncb), spec(hb, prv, ncb), spec(hb, nxt, ncb),
                  wspec(0), wspec(ncb), bspec(0), bspec(ncb)],
        out_specs=(ospec(0), ospec(0), wspec(0), wspec(0), bspec(0), bspec(0)),
        scratch_shapes=[pltpu.VMEM((2, 3, SUBLANES, tc), F32), pltpu.VMEM((2, SUBLANES, tc), F32)],
        compiler_params=_params(2))(da, da, up, up, up, up, up, up, conv_w, conv_w, conv_b, conv_b)
    return dg, dv, jnp.concatenate([dwg, dwv], axis=1), jnp.concatenate([dbg, dbv], axis=1)


def _position():
    return lax.axis_index("x"), lax.axis_index("y"), lax.axis_index("c")


def _all_gather(locals_, name):
    n = len(locals_)

    def body(*refs):
        x_refs, out_refs = refs[:n], refs[n:2 * n]
        send_sems, recv_sems, local_sems = refs[2 * n:]
        x, y, cc = _position()
        me, sibling = (x, y, cc), (x, y, 1 - cc)
        chips = [(1 - x, y), (x, 1 - y), (1 - x, 1 - y)]

        def slot(a, px, py, pc):
            return out_refs[a].at[4 * px + 2 * py + pc]

        def copy(k, a, block, to, own=False):
            return pltpu.make_async_remote_copy(
                src_ref=x_refs[a] if own else slot(a, *block), dst_ref=slot(a, *block),
                send_sem=send_sems.at[k, a], recv_sem=recv_sems.at[k, a], device_id=to, device_id_type=MESH)

        mine = [pltpu.make_async_copy(x_refs[a], slot(a, *me), local_sems.at[a]) for a in range(n)]
        first = [copy(1 + j, a, me, (*chip, cc), own=True) for j, chip in enumerate(chips) for a in range(n)]
        first += [copy(0, a, me, sibling, own=True) for a in range(n)]
        for cp in mine + first:
            cp.start()
        passed = []
        for j, chip in enumerate(chips):
            for a in range(n):
                copy(1 + j, a, (*chip, cc), me).wait_recv()
                cp = copy(4 + j, a, (*chip, cc), sibling)
                cp.start()
                passed.append(cp)
        for a in range(n):
            copy(0, a, sibling, me).wait_recv()
        for j, chip in enumerate(chips):
            for a in range(n):
                copy(4 + j, a, (*chip, 1 - cc), me).wait_recv()
        for cp in first + passed:
            cp.wait_send()
        for cp in mine:
            cp.wait()

    hbm = pl.BlockSpec(memory_space=pl.ANY)
    return _pc(body, name=name, out_shape=[_sds((N_DEV,) + a.shape, a.dtype) for a in locals_],
               in_specs=[hbm] * n, out_specs=[hbm] * n,
               scratch_shapes=[pltpu.SemaphoreType.DMA((7, n)), pltpu.SemaphoreType.DMA((7, n)),
                               pltpu.SemaphoreType.DMA((n,))])(*locals_)


def _peers():
    x, y, cc = _position()
    others = []
    for fx, fy, fc in ((0, 0, 1), (1, 0, 0), (0, 1, 0), (1, 1, 0), (1, 0, 1), (0, 1, 1), (1, 1, 1)):
        p = (1 - x if fx else x, 1 - y if fy else y, 1 - cc if fc else cc)
        others.append((p, 4 * p[0] + 2 * p[1] + p[2]))
    return 4 * x + 2 * y + cc, others


def _gather_start(locals_, after, name):
    n = len(locals_)
    me, _ = _peers()
    lands = [lax.dynamic_update_slice(lax.empty((N_DEV,) + a.shape, a.dtype), a[None], (me,) + (0,) * a.ndim)
             for a in locals_]

    def body(*refs):
        x_refs, land_refs = refs[:n], refs[n:2 * n]
        send_sems, recv_sems, token = refs[2 * n + 1], refs[2 * n + 2], refs[-1]
        me_idx, others = _peers()
        for k, (peer, _) in enumerate(others):
            for a in range(n):
                pltpu.make_async_remote_copy(
                    src_ref=x_refs[a], dst_ref=land_refs[a].at[me_idx], send_sem=send_sems.at[k * n + a],
                    recv_sem=recv_sems.at[k * n + a], device_id=peer, device_id_type=MESH).start()
        token[...] = jnp.zeros_like(token)

    hbm = pl.BlockSpec(memory_space=pltpu.HBM)
    sem = pl.BlockSpec(memory_space=pltpu.SEMAPHORE)
    out = _pc(body, name=name,
              out_shape=(pltpu.SemaphoreType.DMA((7 * n,)), pltpu.SemaphoreType.DMA((7 * n,)),
                         *[pltpu.HBM(a.shape, a.dtype) for a in locals_], *[pltpu.HBM(l.shape, l.dtype) for l in lands],
                         _sds((SUBLANES, LANES))),
              in_specs=[hbm] * (2 * n) + [pl.BlockSpec(memory_space=pl.ANY)],
              out_specs=(sem, sem, *([hbm] * (2 * n)), pl.BlockSpec(memory_space=pltpu.VMEM)),
              input_output_aliases={i: 2 + i for i in range(2 * n)},
              compiler_params=pltpu.CompilerParams(has_side_effects=pltpu.SideEffectType.DATAFLOW_SIDE_EFFECTING))(
        *[pltpu.with_memory_space_constraint(a, pltpu.HBM) for a in locals_],
        *[pltpu.with_memory_space_constraint(l, pltpu.HBM) for l in lands], after)
    return (out[0], out[1], list(out[2:2 + n]), list(out[2 + n:2 + 2 * n])), out[-1]


def _gather_wait(state, after, name):
    send_sems, recv_sems, x_thru, land_thru = state
    n = len(x_thru)

    def body(*refs):
        x_refs, land_refs = refs[:n], refs[n:2 * n]
        send_sems, recv_sems = refs[2 * n], refs[2 * n + 1]
        _, others = _peers()
        for k, (peer, peer_idx) in enumerate(others):
            for a in range(n):
                cp = pltpu.make_async_remote_copy(
                    src_ref=x_refs[a], dst_ref=land_refs[a].at[peer_idx], send_sem=send_sems.at[k * n + a],
                    recv_sem=recv_sems.at[k * n + a], device_id=peer, device_id_type=MESH)
                cp.wait_send()
                cp.wait_recv()

    hbm = pl.BlockSpec(memory_space=pltpu.HBM)
    sem = pl.BlockSpec(memory_space=pltpu.SEMAPHORE)
    out = _pc(body, name=name, out_shape=tuple(pltpu.HBM(a.shape, a.dtype) for a in x_thru + land_thru),
              in_specs=[hbm] * (2 * n) + [sem, sem, pl.BlockSpec(memory_space=pl.ANY)], out_specs=tuple([hbm] * (2 * n)),
              input_output_aliases={i: i for i in range(2 * n)},
              compiler_params=pltpu.CompilerParams(has_side_effects=pltpu.SideEffectType.DATAFLOW_SIDE_EFFECTING))(
        *x_thru, *land_thru, send_sems, recv_sems, after)
    return list(out[n:])


N_CHIPS = 4


def _sibling_swap(parts, name):
    n = len(parts)

    def body(*refs):
        g_refs, got_refs = refs[:n], refs[n:2 * n]
        send_sems, recv_sems = refs[2 * n:]
        x, y, cc = _position()
        swaps = []
        for q in range(N_CHIPS):
            for a in range(n):
                swaps.append(pltpu.make_async_remote_copy(
                    src_ref=g_refs[a].at[2 * q + 1 - cc], dst_ref=got_refs[a].at[q], send_sem=send_sems.at[q, a],
                    recv_sem=recv_sems.at[q, a], device_id=(x, y, 1 - cc), device_id_type=MESH))
        for cp in swaps:
            cp.start()
        for cp in swaps:
            cp.wait_recv()
        for cp in swaps:
            cp.wait_send()

    hbm = pl.BlockSpec(memory_space=pl.ANY)
    return _pc(body, name=name, out_shape=[_sds((N_CHIPS,) + a.shape[1:], a.dtype) for a in parts],
               in_specs=[hbm] * n, out_specs=[hbm] * n,
               scratch_shapes=[pltpu.SemaphoreType.DMA((N_CHIPS, n)), pltpu.SemaphoreType.DMA((N_CHIPS, n))])(*parts)


def _pair_add(part, got, core, name):
    q, a, b = got.shape
    ta = _block_rows(a, b)

    def body(core_ref, k_ref, g_ref, o_ref):
        o_ref[...] = (k_ref[...].astype(F32) + g_ref[...].astype(F32)).astype(BF16)

    spec = pl.BlockSpec((None, ta, b), lambda c, i, core_ref: (c, i, 0))
    own = pl.BlockSpec((None, None, ta, b), lambda c, i, core_ref: (c, core_ref[0], i, 0))
    grid_spec = pltpu.PrefetchScalarGridSpec(num_scalar_prefetch=1, grid=(q, a // ta), in_specs=[own, spec],
                                             out_specs=spec)
    return _pc(body, name=name, out_shape=_sds(got.shape, BF16), grid_spec=grid_spec, compiler_params=_params(2))(
        core, part.reshape((N_CHIPS, 2) + part.shape[1:]), got)


def _chip_exchange(sums, name):
    n = len(sums)

    def body(*refs):
        g_refs, out_refs = refs[:n], refs[n:2 * n]
        send_sems, recv_sems = refs[2 * n:]
        me_q, others = _chip_peers()
        sends, recvs = [], []
        for k, (peer, peer_q) in enumerate(others):
            for a in range(n):
                sends.append(pltpu.make_async_remote_copy(
                    src_ref=g_refs[a].at[peer_q], dst_ref=out_refs[a].at[me_q], send_sem=send_sems.at[k, a],
                    recv_sem=recv_sems.at[k, a], device_id=peer, device_id_type=MESH))
                recvs.append(pltpu.make_async_remote_copy(
                    src_ref=g_refs[a].at[me_q], dst_ref=out_refs[a].at[peer_q], send_sem=send_sems.at[k, a],
                    recv_sem=recv_sems.at[k, a], device_id=peer, device_id_type=MESH))
        for cp in sends:
            cp.start()
        for cp in recvs:
            cp.wait_recv()
        for cp in sends:
            cp.wait_send()

    hbm = pl.BlockSpec(memory_space=pl.ANY)
    return _pc(body, name=name, out_shape=[_sds(a.shape, a.dtype) for a in sums],
               in_specs=[hbm] * n, out_specs=[hbm] * n,
               scratch_shapes=[pltpu.SemaphoreType.DMA((3, n)), pltpu.SemaphoreType.DMA((3, n))])(*sums)


def _chip_peers():
    x, y, cc = _position()
    others = []
    for fx, fy in ((1, 0), (0, 1), (1, 1)):
        px, py = (1 - x if fx else x), (1 - y if fy else y)
        others.append(((px, py, cc), 2 * px + py))
    return 2 * x + y, others


def _chip_exchange_start(sums, name):
    n = len(sums)
    lands = [lax.empty(a.shape, a.dtype) for a in sums]

    def body(*refs):
        g_refs, land_refs = refs[:n], refs[n:2 * n]
        send_sems, recv_sems, token = refs[2 * n], refs[2 * n + 1], refs[-1]
        me_q, others = _chip_peers()
        for k, (peer, peer_q) in enumerate(others):
            for a in range(n):
                pltpu.make_async_remote_copy(
                    src_ref=g_refs[a].at[peer_q], dst_ref=land_refs[a].at[me_q], send_sem=send_sems.at[k * n + a],
                    recv_sem=recv_sems.at[k * n + a], device_id=peer, device_id_type=MESH).start()
        token[...] = jnp.zeros_like(token)

    hbm = pl.BlockSpec(memory_space=pltpu.HBM)
    sem = pl.BlockSpec(memory_space=pltpu.SEMAPHORE)
    out = _pc(body, name=name,
              out_shape=(pltpu.SemaphoreType.DMA((3 * n,)), pltpu.SemaphoreType.DMA((3 * n,)),
                         *[pltpu.HBM(a.shape, a.dtype) for a in sums], *[pltpu.HBM(a.shape, a.dtype) for a in sums],
                         _sds((SUBLANES, LANES))),
              in_specs=[hbm] * (2 * n), out_specs=(sem, sem, *([hbm] * (2 * n)), pl.BlockSpec(memory_space=pltpu.VMEM)),
              input_output_aliases={i: 2 + i for i in range(2 * n)},
              compiler_params=pltpu.CompilerParams(has_side_effects=pltpu.SideEffectType.DATAFLOW_SIDE_EFFECTING))(
        *[pltpu.with_memory_space_constraint(a, pltpu.HBM) for a in sums],
        *[pltpu.with_memory_space_constraint(l, pltpu.HBM) for l in lands])
    return (out[0], out[1], list(out[2:2 + n]), list(out[2 + n:2 + 2 * n])), out[-1]


def _chip_exchange_wait(state, after, name):
    send_sems, recv_sems, g_thru, land_thru = state
    n = len(g_thru)

    def body(*refs):
        g_refs, land_refs = refs[:n], refs[n:2 * n]
        send_sems, recv_sems = refs[2 * n], refs[2 * n + 1]
        me_q, others = _chip_peers()
        for k, (peer, peer_q) in enumerate(others):
            for a in range(n):
                cp = pltpu.make_async_remote_copy(
                    src_ref=g_refs[a].at[me_q], dst_ref=land_refs[a].at[peer_q], send_sem=send_sems.at[k * n + a],
                    recv_sem=recv_sems.at[k * n + a], device_id=peer, device_id_type=MESH)
                cp.wait_send()
                cp.wait_recv()

    hbm = pl.BlockSpec(memory_space=pltpu.HBM)
    sem = pl.BlockSpec(memory_space=pltpu.SEMAPHORE)
    out = _pc(body, name=name, out_shape=tuple(pltpu.HBM(a.shape, a.dtype) for a in g_thru + land_thru),
              in_specs=[hbm] * (2 * n) + [sem, sem, pl.BlockSpec(memory_space=pl.ANY)], out_specs=tuple([hbm] * (2 * n)),
              input_output_aliases={i: i for i in range(2 * n)},
              compiler_params=pltpu.CompilerParams(has_side_effects=pltpu.SideEffectType.DATAFLOW_SIDE_EFFECTING))(
        *g_thru, *land_thru, send_sems, recv_sems, after)
    return list(out[:n]), list(out[n:])


def _block_rows(a, b):
    ta = a
    while ta * b > 256 * 1024 and ta % 32 == 0:
        ta //= 2
    return ta


def _reduce_adamw(parts, w, m, v, name):
    n_parts, s, a, b = parts.shape
    ta = _block_rows(a, b)

    def body(p_ref, w_ref, m_ref, v_ref, g_out, d_out, m_out, v_out):
        g = p_ref[0].astype(F32)
        for j in range(1, n_parts):
            g = g + p_ref[j].astype(F32)
        delta, m_new, v_new = _adamw(g, w_ref[...], m_ref[...], v_ref[...])
        g_out[...] = g
        d_out[...] = delta
        m_out[...] = m_new
        v_out[...] = v_new

    spec = pl.BlockSpec((None, ta, b), lambda l, i: (l, i, 0))
    return _pc(body, name=name, out_shape=(_sds((s, a, b)),) * 4, grid=(s, a // ta),
               in_specs=[pl.BlockSpec((n_parts, None, ta, b), lambda l, i: (0, l, i, 0)), spec, spec, spec],
               out_specs=(spec,) * 4, compiler_params=_params(2))(parts, w, m, v)


def _adamw(g, w, m, v):
    c1 = 1.0 - ADAM_B1 ** ADAM_STEP
    c2 = 1.0 - ADAM_B2 ** ADAM_STEP
    m_new = ADAM_B1 * m + (1.0 - ADAM_B1) * g
    v_new = ADAM_B2 * v + (1.0 - ADAM_B2) * (g * g)
    delta = -ADAM_LR * ((m_new / c1) / (jnp.sqrt(v_new / c2) + ADAM_EPS) + ADAM_WD * w)
    return delta, m_new, v_new


def _chip_reduce_adamw(own, recv, w, m, v, chip, name):
    s, a, b = w.shape
    ta = _block_rows(a, b)

    def body(chip_ref, *refs):
        p_refs, (w_ref, m_ref, v_ref), (g_out, d_out, m_out, v_out) = refs[:4 * s], refs[4 * s:4 * s + 3], refs[4 * s + 3:]
        layer = pl.program_id(0)
        g = None
        for l in range(s):
            gl = p_refs[4 * l][...].astype(F32)
            for j in range(1, N_CHIPS):
                gl = gl + p_refs[4 * l + j][...].astype(F32)
            g = gl if g is None else jnp.where(layer == l, gl, g)
        delta, m_new, v_new = _adamw(g, w_ref[...], m_ref[...], v_ref[...])
        g_out[...] = g
        d_out[...] = delta
        m_out[...] = m_new
        v_out[...] = v_new

    def part_spec(l, j):
        return pl.BlockSpec((None, ta, b), lambda layer, i, chip_ref, l=l, j=j: (
            (chip_ref[0] + j) % N_CHIPS, jnp.where(layer == l, i, 0), 0))

    spec = pl.BlockSpec((None, ta, b), lambda layer, i, chip_ref: (layer, i, 0))
    in_specs, args = [], []
    for l in range(s):
        for j in range(N_CHIPS):
            in_specs.append(part_spec(l, j))
            args.append(own[l] if j == 0 else recv[l])
    grid_spec = pltpu.PrefetchScalarGridSpec(num_scalar_prefetch=1, grid=(s, a // ta), in_specs=in_specs + [spec] * 3,
                                             out_specs=(spec,) * 4)
    return _pc(body, name=name, out_shape=(_sds((s, a, b)),) * 4, grid_spec=grid_spec, compiler_params=_params(2))(
        chip, *args, w, m, v)


MATRICES = (("ada_mix_w", 2), ("w_in", 2), ("w_ssd_out", 1), ("w_conf_out", 2), ("w_sc_out", 2), ("w_o", 1),
            ("ada_ffn_w", 2), ("w_up", 2), ("w_down", 1))
MIXER_MATRICES = ("ada_mix_w", "w_in", "w_ssd_out", "w_conf_out", "w_sc_out", "w_o")
FFN_MATRICES = ("ada_ffn_w", "w_up", "w_down")
CONV_WEIGHTS = (("ssd_conv_w", 2), ("conf_conv_w", 2), ("sc_conv_w", 2), ("ffn_conv_w", 2))
SHARDED = MATRICES + CONV_WEIGHTS
REPLICATED = ("ada_mix_b", "norm_mix_g", "b_gate", "ssd_conv_b", "ssd_dt_bias", "ssd_a_log", "ssd_d", "ssd_norm_g",
              "conf_conv_b", "conf_ln_g", "conf_ln_b", "ada_ffn_b", "norm_ffn_g", "ffn_conv_b", "final_norm_g")
WEIGHT_NAMES = ("ada_mix_w", "ada_mix_b", "norm_mix_g", "w_in", "b_gate", "ssd_conv_w", "ssd_conv_b", "ssd_dt_bias",
                "ssd_a_log", "ssd_d", "ssd_norm_g", "w_ssd_out", "conf_conv_w", "conf_conv_b", "conf_ln_g",
                "conf_ln_b", "w_conf_out", "sc_conv_w", "w_sc_out", "w_o", "ada_ffn_w", "ada_ffn_b", "norm_ffn_g",
                "w_up", "ffn_conv_w", "ffn_conv_b", "w_down", "final_norm_g")


def _pack_flat(arrays, cols, row_multiple, dtype):
    flat = jnp.concatenate([a.reshape(-1).astype(dtype) for a in arrays])
    rows = -(-flat.shape[0] // cols)
    rows = -(-rows // row_multiple) * row_multiple
    return jnp.pad(flat, (0, rows * cols - flat.shape[0])).reshape(rows, cols)


def _unpack_flat(flat2d, shapes):
    flat = flat2d.reshape(-1)
    out, off = [], 0
    for s in shapes:
        n = 1
        for d in s:
            n *= d
        out.append(flat[off:off + n].reshape(s))
        off += n
    return out


def _cols(g, lo, hi):
    b = g.shape[-1]
    pieces = []
    for k in range(N_DEV):
        a, e = max(lo, k * b), min(hi, (k + 1) * b)
        if a < e:
            pieces.append(g[k, :, a - k * b:e - k * b])
    return pieces[0] if len(pieces) == 1 else jnp.concatenate(pieces, axis=1)


def _rows(g):
    return g.reshape(N_DEV * g.shape[1], g.shape[2])


def _col_shards(segs, b):
    shards = []
    for k in range(N_DEV):
        lo, hi = k * b, (k + 1) * b
        pieces, off = [], 0
        for seg in segs:
            n = seg.shape[1]
            a, e = max(lo, off), min(hi, off + n)
            if a < e:
                pieces.append(seg[:, a - off:e - off])
            off += n
        shards.append(pieces[0] if len(pieces) == 1 else jnp.concatenate(pieces, axis=1))
    return jnp.stack(shards)


def _row_shards(full):
    return full.reshape(N_DEV, full.shape[0] // N_DEV, full.shape[1])


def _pad_rows(a, rows):
    return jnp.pad(a, ((0, rows - a.shape[0]), (0, 0)))


def _pad_lanes(a):
    return jnp.pad(a, ((0, 0), (0, LANES - a.shape[1])))


def _whole(g):
    return _cols(g, 0, N_DEV * g.shape[-1])


def _mixer_weights(full, i):
    row = lambda name: full[name][i].reshape(1, -1)
    conv = lambda name: _whole(full[name][:, i])
    w_in = full["w_in", i]
    return {
        "ada_mix_w": _whole(full["ada_mix_w", i]), "ada_mix_b": row("ada_mix_b"), "norm_mix_g": row("norm_mix_g"),
        "w_z": _cols(w_in, 0, OFF_Z), "w_xbc": _cols(w_in, OFF_Z, OFF_XBC),
        "w_dt": _pad_lanes(_cols(w_in, OFF_XBC, OFF_DT)), "w_conf": _cols(w_in, OFF_DT, OFF_CONF),
        "w_sc": _cols(w_in, OFF_CONF, OFF_SC), "w_gates": _cols(w_in, OFF_SC, N_IN),
        "b_gate": row("b_gate"),
        "ssd_conv_w": _pad_rows(conv("ssd_conv_w"), SUBLANES), "ssd_conv_b": row("ssd_conv_b"),
        "dt_bias": _pad_lanes(row("ssd_dt_bias")), "a_log": _pad_lanes(row("ssd_a_log")),
        "ssd_d": _pad_lanes(row("ssd_d")), "ssd_norm_g": row("ssd_norm_g"), "w_ssd_out": _rows(full["w_ssd_out", i]),
        "conf_conv_w": _pad_rows(conv("conf_conv_w"), CONF_HALO), "conf_conv_b": row("conf_conv_b"),
        "conf_ln_g": row("conf_ln_g"), "conf_ln_b": row("conf_ln_b"), "w_conf_out": _whole(full["w_conf_out", i]),
        "sc_conv_w": _pad_rows(conv("sc_conv_w"), SUBLANES), "w_sc_out": _whole(full["w_sc_out", i]),
        "w_o": _rows(full["w_o", i]),
    }


def _ffn_weights(full, i):
    row = lambda name: full[name][i].reshape(1, -1)
    return {
        "ada_ffn_w": _whole(full["ada_ffn_w", i]), "ada_ffn_b": row("ada_ffn_b"), "norm_ffn_g": row("norm_ffn_g"),
        "w_up": _whole(full["w_up", i]), "ffn_conv_w": _pad_rows(_whole(full["ffn_conv_w"][:, i]), SUBLANES),
        "ffn_conv_b": row("ffn_conv_b"), "w_down": _rows(full["w_down", i]),
    }


def _adaln(sc8, w, b, name):
    mod = _matmul(sc8, w, "nn", F32, name)[0:1, :] + b
    return mod[:, :D_MODEL], mod[:, D_MODEL:2 * D_MODEL], mod[:, 2 * D_MODEL:]


def _mixer_fwd(i, x, prev, sc8, wl):
    t = f"l{i}_"
    s = {}
    shift, scale, gate = _adaln(sc8, wl["ada_mix_w"], wl["ada_mix_b"], t + "ada_mix")
    if prev is None:
        s["x_in"] = x
        s["h"] = _prenorm_first(x, wl["norm_mix_g"], scale, shift, t + "norm_mix")
    else:
        s["x_in"], s["h"] = _prenorm_res(x, prev[0], prev[1], wl["norm_mix_g"], scale, shift, t + "norm_mix")
    s["scale_mix"], s["gate_mix"] = scale, gate
    h = s["h"]
    s["z"] = _matmul(h, wl["w_z"], "nn", F32, t + "in_z")
    s["xbc"] = _matmul(h, wl["w_xbc"], "nn", F32, t + "in_xbc")
    s["dt_raw"] = _matmul(h, wl["w_dt"], "nn", F32, t + "in_dt")
    s["conf"] = _matmul(h, wl["w_conf"], "nn", F32, t + "in_conf")
    s["sc"] = _matmul(h, wl["w_sc"], "nn", F32, t + "in_sc")
    s["gates"] = _matmul(h, wl["w_gates"], "nn", F32, t + "in_gates")
    s["pre"], s["dt"] = _ssd_pre(s["xbc"], s["dt_raw"], wl["ssd_conv_w"], wl["ssd_conv_b"], wl["dt_bias"],
                                 t + "ssd_pre")
    s["y"], s["hprev"] = _ssd_scan(s["pre"], s["dt"], wl["a_log"], wl["ssd_d"], t + "ssd_scan")
    s["ya_in"] = _ssd_post(s["y"], s["z"], wl["ssd_norm_g"], t + "ssd_post")
    s["yb_in"], s["uc"] = _conf_fwd(s["conf"], wl["conf_conv_w"], wl["conf_conv_b"], wl["conf_ln_g"],
                                    wl["conf_ln_b"], t + "conf")
    s["yc_in"] = _sc_fwd(s["sc"], wl["sc_conv_w"], t + "sconv")
    s["ya"] = _matmul(s["ya_in"], wl["w_ssd_out"], "nn", F32, t + "ssd_out")
    s["yb"] = _matmul(s["yb_in"], wl["w_conf_out"], "nn", F32, t + "conf_out")
    s["yc"] = _matmul(s["yc_in"], wl["w_sc_out"], "nn", F32, t + "sc_out")
    s["merged"] = _merge_fwd(s["gates"], s["ya"], s["yb"], s["yc"], wl["b_gate"], t + "merge")
    s["mix"] = _matmul(s["merged"], wl["w_o"], "nn", F32, t + "w_o")
    return s


def _ffn_fwd(i, s, sc8, wl):
    t = f"l{i}_"
    shift2, scale2, gate2 = _adaln(sc8, wl["ada_ffn_w"], wl["ada_ffn_b"], t + "ada_ffn")
    s["x_mid"], s["h2"] = _prenorm_res(s["x_in"], s["mix"], s["gate_mix"], wl["norm_ffn_g"], scale2, shift2,
                                       t + "norm_ffn")
    s["scale_ffn"], s["gate_ffn"] = scale2, gate2
    s["up"] = _matmul(s["h2"], wl["w_up"], "nn", F32, t + "w_up")
    s["a"] = _ffn_mid(s["up"], wl["ffn_conv_w"], wl["ffn_conv_b"], t + "ffn_mid")
    s["out"] = _matmul(s["a"], wl["w_down"], "nn", F32, t + "w_down")
    return s


def _layer_bwd(i, s, wl, sc8, dys_ffn, dx_after, dgate_ffn, prev, emit=None):
    t = f"l{i}_b_"
    g = {}
    da = _matmul(dys_ffn, wl["w_down"], "nt", F32, t + "d_a")
    g["w_down"] = _matmul(s["a"], dys_ffn, "tn", BF16, t + "dw_down")
    dug, duv, dfw, g["ffn_conv_b"] = _ffn_mid_bwd(da, s["up"], wl["ffn_conv_w"], wl["ffn_conv_b"], t + "ffn_mid")
    g["ffn_conv_w"] = dfw[:3]
    dh2 = _matmul_sum_nt([dug, duv], [wl["w_up"][:, :D_FF], wl["w_up"][:, D_FF:]], t + "d_h2")
    g["w_up"] = [_matmul(s["h2"], dug, "tn", BF16, t + "dw_up_g"), _matmul(s["h2"], duv, "tn", BF16, t + "dw_up_v")]
    dx_mid, dshift2, dscale2, g["norm_ffn_g"], dys_mix, dgate_mix = _norm_bwd(
        dh2, s["x_mid"], dx_after, wl["norm_ffn_g"], s["scale_ffn"], t + "norm_ffn", s["mix"], s["gate_mix"])
    dmod_ffn = jnp.concatenate([dshift2, dscale2, dgate_ffn], axis=1)
    g["ada_ffn_b"] = dmod_ffn
    g["ada_ffn_w"] = [_matmul(sc8, _pad_rows(dmod_ffn, SUBLANES), "tn", BF16, t + "dw_ada_ffn")]
    token = emit(i, "ffn", g) if emit is not None else None
    if token is not None:
        wl = {**wl, "b_gate": wl["b_gate"] + token}
    dmerged = _matmul(dys_mix, wl["w_o"], "nt", F32, t + "d_merged")
    g["w_o"] = _matmul(s["merged"], dys_mix, "tn", BF16, t + "dw_o")
    dya, dyb, dyc, dgates, g["b_gate"] = _merge_bwd(dmerged, s["gates"], s["ya"], s["yb"], s["yc"], wl["b_gate"],
                                                    t + "merge")
    dya_in = _matmul(dya, wl["w_ssd_out"], "nt", F32, t + "d_ya_in")
    g["w_ssd_out"] = _matmul(s["ya_in"], dya, "tn", BF16, t + "dw_ssd_out")
    dyb_in = _matmul(dyb, wl["w_conf_out"], "nt", F32, t + "d_yb_in")
    g["w_conf_out"] = [_matmul(s["yb_in"], dyb, "tn", BF16, t + "dw_conf_out")]
    dyc_in = _matmul(dyc, wl["w_sc_out"], "nt", F32, t + "d_yc_in")
    g["w_sc_out"] = [_matmul(s["yc_in"], dyc, "tn", BF16, t + "dw_sc_out")]
    dy, dz, g["ssd_norm_g"] = _ssd_post_bwd(dya_in, s["y"], s["z"], wl["ssd_norm_g"], t + "ssd_post")
    dpre, ddt, da_log, dd = _ssd_scan_bwd(s["pre"], s["dt"], s["hprev"], dy, wl["a_log"], wl["ssd_d"],
                                          t + "ssd_scan")
    g["ssd_a_log"], g["ssd_d"] = da_log[:, :SSD_HEADS], dd[:, :SSD_HEADS]
    dxbc, ddt_raw, dcw, g["ssd_conv_b"], ddtb = _ssd_pre_bwd(dpre, s["xbc"], ddt, s["dt_raw"], wl["ssd_conv_w"],
                                                             wl["dt_bias"], t + "ssd_pre")
    g["ssd_conv_w"], g["ssd_dt_bias"] = dcw[:4], ddtb[:, :SSD_HEADS]
    dconf, dccw, g["conf_conv_b"], g["conf_ln_g"], g["conf_ln_b"] = _conf_bwd(
        dyb_in, s["uc"], s["conf"], wl["conf_conv_w"], wl["conf_ln_g"], wl["conf_ln_b"], t + "conf")
    g["conf_conv_w"] = dccw[:CONF_KERNEL]
    dsc, dscw = _sc_bwd(dyc_in, s["sc"], wl["sc_conv_w"], t + "sconv")
    g["sc_conv_w"] = dscw[:3]
    segs = (("z", dz, "w_z"), ("xbc", dxbc, "w_xbc"), ("dt", ddt_raw, "w_dt"), ("conf", dconf, "w_conf"),
            ("sc", dsc, "w_sc"), ("gates", dgates, "w_gates"))
    dh = _matmul_sum_nt([dseg for _, dseg, _ in segs], [wl[wname] for _, _, wname in segs], t + "d_h")
    dw_segs = []
    for nm, dseg, wname in segs:
        dw = _matmul(s["h"], dseg, "tn", BF16, t + "dw_in_" + nm)
        dw_segs.append(dw[:, :SSD_HEADS] if nm == "dt" else dw)
    g["w_in"] = dw_segs
    if prev is None:
        dx_in, dshift, dscale, g["norm_mix_g"] = _norm_bwd(dh, s["x_in"], dx_mid, wl["norm_mix_g"], s["scale_mix"],
                                                          t + "norm_mix")
        back = None
    else:
        dx_in, dshift, dscale, g["norm_mix_g"], dys_prev, dgate_prev = _norm_bwd(
            dh, s["x_in"], dx_mid, wl["norm_mix_g"], s["scale_mix"], t + "norm_mix", prev[0], prev[1])
        back = (dys_prev, dgate_prev)
    dmod_mix = jnp.concatenate([dshift, dscale, dgate_mix], axis=1)
    g["ada_mix_b"] = dmod_mix
    g["ada_mix_w"] = [_matmul(sc8, _pad_rows(dmod_mix, SUBLANES), "tn", BF16, t + "dw_ada_mix")]
    return g, dx_in, back


def _device_step(x, c, target, full, fetch=None, emit=None):
    fetch = fetch or {}
    full = dict(full)
    sc8 = _pad_rows(c * (1.0 / (1.0 + jnp.exp(-c))), SUBLANES)
    wls, saved, prev, xcur = [], [], None, x
    for i in range(DEPTH):
        if (i, "mixer") in fetch:
            full.update(fetch[i, "mixer"](prev[0]))
        wl = _mixer_weights(full, i)
        s = _mixer_fwd(i, xcur, prev, sc8, wl)
        if (i, "ffn") in fetch:
            full.update(fetch[i, "ffn"](s["mix"]))
        wf = _ffn_weights(full, i)
        _ffn_fwd(i, s, sc8, wf)
        wls.append({**wl, **wf})
        saved.append(s)
        xcur, prev = s["x_mid"], (s["out"], s["gate_ffn"])
    gf = full["final_norm_g"].reshape(1, -1)
    last = saved[-1]
    loss, dx, dys, dgate, dgf = _final_loss(last["x_mid"], last["out"], last["gate_ffn"], gf, target, "final_loss")
    grads = [None] * DEPTH
    for i in reversed(range(DEPTH)):
        prev = None if i == 0 else (saved[i - 1]["out"], saved[i - 1]["gate_ffn"])
        grads[i], dx, back = _layer_bwd(i, saved[i], wls[i], sc8, dys, dx, dgate, prev, emit)
        token = emit(i, "mixer", grads[i]) if emit is not None else None
        if token is not None:
            wls[i - 1] = {**wls[i - 1], "ffn_conv_b": wls[i - 1]["ffn_conv_b"] + token}
        if back is not None:
            dys, dgate = back
    return loss[0, 0], dx, grads, dgf


def _step(x, c, target, weights, moments_m, moments_v):
    sharded_names = [n for n, _ in SHARDED]
    conv_names = [n for n, _ in CONV_WEIGHTS]
    shard = lambda key: weights[key[0]][key[1]].astype(BF16)
    first = [(n, 0) for n in MIXER_MATRICES]
    later = {(0, "ffn"): [(n, 0) for n in FFN_MATRICES], (1, "mixer"): [(n, 1) for n in MIXER_MATRICES + FFN_MATRICES]}
    gathered = _all_gather([shard(k) for k in first] + [weights[n] for n in conv_names], "gather_first")
    full = {n: weights[n] for n in REPLICATED}
    full.update(zip(first + conv_names, gathered))
    fetch, after = {}, gathered[0]
    for stage, keys in later.items():
        state, after = _gather_start([shard(k) for k in keys], after, f"gather_l{stage[0]}_{stage[1]}_start")
        fetch[stage] = functools.partial(
            lambda act, state, keys, nm: dict(zip(keys, _gather_wait(state, act, nm))),
            state=state, keys=keys, nm=f"gather_l{stage[0]}_{stage[1]}_wait")
    axis_of = dict(SHARDED)
    core = lax.axis_index("c").astype(jnp.int32).reshape(1)
    chip = (2 * lax.axis_index("x") + lax.axis_index("y")).astype(jnp.int32).reshape(1)
    ffn_names = list(FFN_MATRICES) + ["ffn_conv_w"]
    mixer_names = [n for n in sharded_names if n not in ffn_names]
    sums, received, pending = {}, {}, []

    def send(i, names, grads_i, last):
        keys = [(n, i) for n in names]
        parts = []
        for n in names:
            gw = grads_i[n]
            part = _row_shards(gw) if axis_of[n] == 1 else _col_shards(gw if isinstance(gw, list) else [gw],
                                                                       weights[n].shape[-1])
            parts.append(part.astype(BF16))
        tag = f"l{i}_{'ffn' if names is ffn_names else 'mixer'}"
        got = _sibling_swap(parts, "swap_grads_" + tag)
        pair = [_pair_add(p, g, core, f"pair_add_{n}_{i}") for n, p, g in zip(names, parts, got)]
        if last:
            sums.update(zip(keys, pair))
            received.update(zip(keys, _chip_exchange(pair, "exchange_grads_" + tag)))
            return None
        state, token = _chip_exchange_start(pair, "exchange_grads_" + tag + "_start")
        pending.append((keys, state, tag))
        return token[0:1, 0:1]

    def emit(i, kind, grads_i):
        if kind == "ffn" and i == 0:
            return send(0, ffn_names, grads_i, False)
        if kind == "mixer":
            return send(i, mixer_names if i == 0 else sharded_names, grads_i, i == 0)
        return None

    loss, grad_x, grads, dgf = _device_step(x[0], c + after[0:1, 0:1], target[0], full, fetch, emit)
    for keys, state, tag in pending:
        own, got = _chip_exchange_wait(state, grad_x, "exchange_grads_" + tag + "_wait")
        sums.update(zip(keys, own))
        received.update(zip(keys, got))
    big = {n: _chip_reduce_adamw([sums[n, i] for i in range(DEPTH)], [received[n, i] for i in range(DEPTH)],
                                 weights[n], moments_m[n], moments_v[n], chip, "adamw_" + n)
           for n in sharded_names}
    rep_grads = [dgf if n == "final_norm_g" else jnp.stack([grads[i][n].reshape(-1) for i in range(DEPTH)])
                 for n in REPLICATED]
    small_parts, = _all_gather([_pack_flat(rep_grads, LANES, SUBLANES, F32)], "gather_small_grads")
    pack_s = lambda d: _pack_flat([d[n] for n in REPLICATED], LANES, SUBLANES, F32)[None]
    small = _reduce_adamw(small_parts[:, None], pack_s(weights), pack_s(moments_m), pack_s(moments_v),
                          "adamw_replicated")
    small = [_unpack_flat(b, [weights[n].shape for n in REPLICATED]) for b in small]
    results = []
    for kind in range(4):
        by_name = {n: big[n][kind] for n in sharded_names}
        by_name.update(zip(REPLICATED, small[kind]))
        results.append([by_name[n] for n in WEIGHT_NAMES])
    loss = lax.psum(loss, ("x", "y", "c"))
    return (loss, grad_x[None], *results[0], *results[1], *results[2], *results[3])


def kernel(x, c, ada_mix_w, ada_mix_b, norm_mix_g, w_in, b_gate, ssd_conv_w, ssd_conv_b, ssd_dt_bias, ssd_a_log, ssd_d, ssd_norm_g, w_ssd_out, conf_conv_w, conf_conv_b, conf_ln_g, conf_ln_b, w_conf_out, sc_conv_w, w_sc_out, w_o, ada_ffn_w, ada_ffn_b, norm_ffn_g, w_up, ffn_conv_w, ffn_conv_b, w_down, final_norm_g, loss_target, m_ada_mix_w, m_ada_mix_b, m_norm_mix_g, m_w_in, m_b_gate, m_ssd_conv_w, m_ssd_conv_b, m_ssd_dt_bias, m_ssd_a_log, m_ssd_d, m_ssd_norm_g, m_w_ssd_out, m_conf_conv_w, m_conf_conv_b, m_conf_ln_g, m_conf_ln_b, m_w_conf_out, m_sc_conv_w, m_w_sc_out, m_w_o, m_ada_ffn_w, m_ada_ffn_b, m_norm_ffn_g, m_w_up, m_ffn_conv_w, m_ffn_conv_b, m_w_down, m_final_norm_g, v_ada_mix_w, v_ada_mix_b, v_norm_mix_g, v_w_in, v_b_gate, v_ssd_conv_w, v_ssd_conv_b, v_ssd_dt_bias, v_ssd_a_log, v_ssd_d, v_ssd_norm_g, v_w_ssd_out, v_conf_conv_w, v_conf_conv_b, v_conf_ln_g, v_conf_ln_b, v_w_conf_out, v_sc_conv_w, v_w_sc_out, v_w_o, v_ada_ffn_w, v_ada_ffn_b, v_norm_ffn_g, v_w_up, v_ffn_conv_w, v_ffn_conv_b, v_w_down, v_final_norm_g):
    given = dict(locals())
    weights = {n: given[n] for n in WEIGHT_NAMES}
    moments_m = {n: given["m_" + n] for n in WEIGHT_NAMES}
    moments_v = {n: given["v_" + n] for n in WEIGHT_NAMES}
    return _step(x, c, loss_target, weights, moments_m, moments_v)
```

```python
import functools

import jax
import jax.numpy as jnp
from jax import lax
from jax.experimental import pallas as pl
from jax.experimental.pallas import tpu as pltpu

F32 = jnp.float32
BF16 = jnp.bfloat16
MESH = pl.DeviceIdType.MESH

N_DEV = 8
DEPTH = 2
D_MODEL = 1024
SSD_HEADS = 16
SSD_HEAD_DIM = 64
SSD_INNER = 1024
SSD_STATE = 64
SSD_CHUNK = 128
SSD_XBC = 1280
CONF_WIDTH = 512
CONF_KERNEL = 31
SC_WIDTH = 512
D_FF = 2816
EPS = 1e-6
OFF_Z, OFF_XBC, OFF_DT, OFF_CONF, OFF_SC, N_IN = 1024, 2304, 2320, 3344, 4880, 7952

ADAM_LR, ADAM_B1, ADAM_B2, ADAM_EPS, ADAM_WD, ADAM_STEP = 0.001, 0.9, 0.999, 1e-08, 0.01, 10

LANES = 128
SUBLANES = 8
VMEM_LIMIT = 56 * 1024 * 1024
ROW_TILE = 256

NN = (((1,), (0,)), ((), ()))
NT = (((1,), (1,)), ((), ()))
TN = (((0,), (0,)), ((), ()))


def _params(n_axes):
    return pltpu.CompilerParams(dimension_semantics=("arbitrary",) * n_axes, vmem_limit_bytes=VMEM_LIMIT)


def _pc(body, **kw):
    return pl.pallas_call(body, **kw)


def _dot(a, b, dn=NN, precision=None):
    return lax.dot_general(a, b, dn, precision=precision, preferred_element_type=F32)


def _split3(x):
    hi = x.astype(BF16)
    r1 = x - hi.astype(F32)
    mid = r1.astype(BF16)
    return hi, mid, (r1 - mid.astype(F32)).astype(BF16)


def _dot_sel(x, sel):
    hi, mid, lo = _split3(x)
    return _dot(hi, sel) + _dot(mid, sel) + _dot(lo, sel)


def _sel_dot(sel, x):
    hi, mid, lo = _split3(x)
    return _dot(sel, hi) + _dot(sel, mid) + _dot(sel, lo)


def _sig(x):
    return 1.0 / (1.0 + jnp.exp(-x))


def _fold(v):
    r, c = v.shape
    return v.reshape(r // SUBLANES, SUBLANES, c).sum(axis=0)


def _tile(n_rows, target=ROW_TILE):
    return min(target, n_rows // 2)


STRIP_UNITS = 8
STRIP_ROW_TILE = 512


def _strip_units_per_trip(tl):
    return min(STRIP_UNITS, tl // SUBLANES)


def _bcast_row(ref, k, ls):
    return jnp.broadcast_to(ref[k:k + 1, ls], (SUBLANES, LANES))


def _unit_rows(t, u, nu):
    return pl.ds(pl.multiple_of((t * nu + u) * SUBLANES, SUBLANES), SUBLANES)


def _pair_rows(t, p, nu):
    return pl.ds(pl.multiple_of((t * nu + 2 * p) * SUBLANES, 2 * SUBLANES), 2 * SUBLANES)


def _strip_units(ref, ls, t, nu, nt, halo_prev=None, halo_next=None):
    units = [ref[_unit_rows(t, u, nu), ls] for u in range(nu)]
    if halo_prev is not None:
        before = pl.ds(pl.multiple_of(jnp.maximum(t * nu - 1, 0) * SUBLANES, SUBLANES), SUBLANES)
        units.insert(0, jnp.where(t > 0, ref[before, ls], halo_prev))
    if halo_next is not None:
        after = pl.ds(pl.multiple_of(jnp.minimum((t + 1) * nu, nt * nu - 1) * SUBLANES, SUBLANES), SUBLANES)
        units.append(jnp.where(t < nt - 1, ref[after, ls], halo_next))
    return units


def _row(tl, c, col=0):
    return pl.BlockSpec((tl, c), lambda i, col=col: (i, col))


def _prev(tl, hb, c, col=0):
    r = tl // hb
    return pl.BlockSpec((hb, c), lambda i, col=col: (jnp.maximum(i * r - 1, 0), col))


def _next(tl, hb, c, n_rows, col=0):
    r = tl // hb
    last = n_rows // hb - 1
    return pl.BlockSpec((hb, c), lambda i, col=col: (jnp.minimum((i + 1) * r, last), col))


def _const(shape):
    return pl.BlockSpec(shape, lambda i: (0,) * len(shape))


def _sds(shape, dtype=F32):
    return jax.ShapeDtypeStruct(shape, dtype)


MM_TILE = 1536
MM_TILE_ROWS = 2048
MM_FULL_K = 3072
MM_K_TILE = 1024


def _pick(dim, target):
    if dim <= target:
        return dim
    best = None
    for t in range(LANES, target + 1, LANES):
        if dim % t == 0:
            best = t
    assert best is not None, (dim, target)
    return best


def _matmul(a, b, mode, out_dtype, name):
    if mode == "nn":
        (m, k), (k2, n) = a.shape, b.shape
    elif mode == "nt":
        (m, k), (n, k2) = a.shape, b.shape
    else:
        (k, m), (k2, n) = a.shape, b.shape
    assert k == k2, (a.shape, b.shape, mode)
    tn = _pick(n, MM_TILE)
    tk = k if k <= MM_FULL_K else _pick(k, MM_K_TILE)
    tm = _pick(m, MM_TILE_ROWS if tn <= 1024 and tk <= 1024 else 1024)
    nk = k // tk
    dn = {"nn": NN, "nt": NT, "tn": TN}[mode]

    def body_one(a_ref, b_ref, o_ref):
        o_ref[...] = _dot(a_ref[...].astype(BF16), b_ref[...].astype(BF16), dn).astype(out_dtype)

    def body_acc(a_ref, b_ref, o_ref, acc):
        kk = pl.program_id(2)

        @pl.when(kk == 0)
        def _():
            acc[...] = jnp.zeros_like(acc)

        acc[...] += _dot(a_ref[...].astype(BF16), b_ref[...].astype(BF16), dn)

        @pl.when(kk == nk - 1)
        def _():
            o_ref[...] = acc[...].astype(out_dtype)

    a_spec = {"nn": pl.BlockSpec((tm, tk), lambda i, j, kk: (i, kk)),
              "nt": pl.BlockSpec((tm, tk), lambda i, j, kk: (i, kk)),
              "tn": pl.BlockSpec((tk, tm), lambda i, j, kk: (kk, i))}[mode]
    b_spec = {"nn": pl.BlockSpec((tk, tn), lambda i, j, kk: (kk, j)),
              "nt": pl.BlockSpec((tn, tk), lambda i, j, kk: (j, kk)),
              "tn": pl.BlockSpec((tk, tn), lambda i, j, kk: (kk, j))}[mode]
    o_spec = pl.BlockSpec((tm, tn), lambda i, j, kk: (i, j))
    return _pc(body_one if nk == 1 else body_acc, name=name, out_shape=_sds((m, n), out_dtype),
               grid=(m // tm, n // tn, nk), in_specs=[a_spec, b_spec], out_specs=o_spec,
               scratch_shapes=[] if nk == 1 else [pltpu.VMEM((tm, tn), F32)], compiler_params=_params(3))(a, b)


SUM_NT_TILE = 512


def _matmul_sum_nt(a_list, b_list, name):
    m, n = a_list[0].shape[0], b_list[0].shape[0]
    cnt = len(a_list)
    tm, tn = _pick(m, SUM_NT_TILE), _pick(n, SUM_NT_TILE)

    def body(*refs):
        a_refs, b_refs, o_ref = refs[:cnt], refs[cnt:2 * cnt], refs[2 * cnt]
        acc = _dot(a_refs[0][...].astype(BF16), b_refs[0][...].astype(BF16), NT)
        for t in range(1, cnt):
            acc = acc + _dot(a_refs[t][...].astype(BF16), b_refs[t][...].astype(BF16), NT)
        o_ref[...] = acc

    in_specs = [pl.BlockSpec((tm, a.shape[1]), lambda j, i: (i, 0)) for a in a_list]
    in_specs += [pl.BlockSpec((tn, b.shape[1]), lambda j, i: (j, 0)) for b in b_list]
    return _pc(body, name=name, out_shape=_sds((m, n)), grid=(n // tn, m // tm), in_specs=in_specs,
               out_specs=pl.BlockSpec((tm, tn), lambda j, i: (i, j)), compiler_params=_params(2))(*a_list, *b_list)


def _norm_mod(x, g, scale, shift):
    r = lax.rsqrt(jnp.mean(x * x, axis=-1, keepdims=True) + EPS)
    return ((x * r) * g) * (1.0 + scale) + shift


def _prenorm_first(x, g, scale, shift, name):
    n, d = x.shape
    tl = _tile(n)

    def body(x_ref, g_ref, sc_ref, sh_ref, h_ref):
        h_ref[...] = _norm_mod(x_ref[...], g_ref[...], sc_ref[...], sh_ref[...]).astype(BF16)

    return _pc(body, name=name, out_shape=_sds((n, d), BF16), grid=(n // tl,),
               in_specs=[_row(tl, d)] + [_const((1, d))] * 3, out_specs=_row(tl, d),
               compiler_params=_params(1))(x, g, scale, shift)


def _prenorm_res(x, y, gate, g, scale, shift, name):
    n, d = x.shape
    tl = _tile(n)

    def body(x_ref, y_ref, gate_ref, g_ref, sc_ref, sh_ref, xo_ref, h_ref):
        xn = x_ref[...] + gate_ref[...] * y_ref[...]
        xo_ref[...] = xn
        h_ref[...] = _norm_mod(xn, g_ref[...], sc_ref[...], sh_ref[...]).astype(BF16)

    return _pc(body, name=name, out_shape=(_sds((n, d)), _sds((n, d), BF16)), grid=(n // tl,),
               in_specs=[_row(tl, d), _row(tl, d)] + [_const((1, d))] * 4,
               out_specs=(_row(tl, d), _row(tl, d)), compiler_params=_params(1))(x, y, gate, g, scale, shift)


def _final_loss(x, y, gate, gf, target, name):
    n, d = x.shape
    tl = _tile(n)
    nb = n // tl

    def body(x_ref, y_ref, gate_ref, gf_ref, t_ref, loss_ref, dx_ref, dys_ref, dgate_ref, dgf_ref,
             acc_l, acc_gate, acc_gf):
        i = pl.program_id(0)

        @pl.when(i == 0)
        def _():
            acc_l[...] = jnp.zeros_like(acc_l)
            acc_gate[...] = jnp.zeros_like(acc_gate)
            acc_gf[...] = jnp.zeros_like(acc_gf)

        yv = y_ref[...]
        gate = gate_ref[...]
        gf = gf_ref[...]
        x2 = x_ref[...] + gate * yv
        r = lax.rsqrt(jnp.mean(x2 * x2, axis=-1, keepdims=True) + EPS)
        xn = x2 * r
        e = xn * gf - t_ref[...]
        acc_l[...] += _fold(e * e)
        dy = e * (1.0 / d)
        acc_gf[...] += _fold(dy * xn)
        dxn = dy * gf
        dx = r * (dxn - xn * jnp.mean(dxn * xn, axis=-1, keepdims=True))
        dx_ref[...] = dx
        dys_ref[...] = (dx * gate).astype(BF16)
        acc_gate[...] += _fold(dx * yv)

        @pl.when(i == nb - 1)
        def _():
            loss_ref[...] = jnp.full((SUBLANES, LANES), 0.5 / d, F32) * jnp.sum(acc_l[...])
            dgate_ref[...] = jnp.sum(acc_gate[...], axis=0, keepdims=True)
            dgf_ref[...] = jnp.sum(acc_gf[...], axis=0, keepdims=True)

    return _pc(body, name=name,
               out_shape=(_sds((SUBLANES, LANES)), _sds((n, d)), _sds((n, d), BF16), _sds((1, d)), _sds((1, d))),
               grid=(nb,),
               in_specs=[_row(tl, d), _row(tl, d), _const((1, d)), _const((1, d)), _row(tl, d)],
               out_specs=(_const((SUBLANES, LANES)), _row(tl, d), _row(tl, d), _const((1, d)), _const((1, d))),
               scratch_shapes=[pltpu.VMEM((SUBLANES, d), F32)] * 3,
               compiler_params=_params(1))(x, y, gate, gf, target)


def _norm_bwd(dh, x, dxo, g, scale, name, y_prev=None, gate_prev=None):
    n, d = x.shape
    tl = _tile(n)
    nb = n // tl
    has_prev = y_prev is not None

    def body(*refs):
        if has_prev:
            (dh_ref, x_ref, dxo_ref, g_ref, sc_ref, yp_ref, gp_ref,
             dx_ref, dsh_ref, dsc_ref, dg_ref, dys_ref, dgp_ref, acc_sh, acc_s, acc_gp) = refs
        else:
            (dh_ref, x_ref, dxo_ref, g_ref, sc_ref,
             dx_ref, dsh_ref, dsc_ref, dg_ref, acc_sh, acc_s) = refs
        i = pl.program_id(0)

        @pl.when(i == 0)
        def _():
            acc_sh[...] = jnp.zeros_like(acc_sh)
            acc_s[...] = jnp.zeros_like(acc_s)
            if has_prev:
                acc_gp[...] = jnp.zeros_like(acc_gp)

        x_ = x_ref[...]
        dh_ = dh_ref[...]
        g_ = g_ref[...]
        one_sc = 1.0 + sc_ref[...]
        r = lax.rsqrt(jnp.mean(x_ * x_, axis=-1, keepdims=True) + EPS)
        xn = x_ * r
        dxn = dh_ * (g_ * one_sc)
        dx = dxo_ref[...] + r * (dxn - xn * jnp.mean(dxn * xn, axis=-1, keepdims=True))
        dx_ref[...] = dx
        acc_sh[...] += _fold(dh_)
        acc_s[...] += _fold(dh_ * xn)
        if has_prev:
            dys_ref[...] = (dx * gp_ref[...]).astype(BF16)
            acc_gp[...] += _fold(dx * yp_ref[...])

        @pl.when(i == nb - 1)
        def _():
            s = jnp.sum(acc_s[...], axis=0, keepdims=True)
            dsh_ref[...] = jnp.sum(acc_sh[...], axis=0, keepdims=True)
            dsc_ref[...] = s * g_
            dg_ref[...] = s * one_sc
            if has_prev:
                dgp_ref[...] = jnp.sum(acc_gp[...], axis=0, keepdims=True)

    vec = _sds((1, d))
    in_specs = [_row(tl, d)] * 3 + [_const((1, d))] * 2
    out_shape = [_sds((n, d)), vec, vec, vec]
    out_specs = [_row(tl, d)] + [_const((1, d))] * 3
    scratch = [pltpu.VMEM((SUBLANES, d), F32)] * 2
    args = [dh, x, dxo, g, scale]
    if has_prev:
        in_specs += [_row(tl, d), _const((1, d))]
        out_shape += [_sds((n, d), BF16), vec]
        out_specs += [_row(tl, d), _const((1, d))]
        scratch += [pltpu.VMEM((SUBLANES, d), F32)]
        args += [y_prev, gate_prev]
    return _pc(body, name=name, out_shape=tuple(out_shape), grid=(nb,), in_specs=in_specs,
               out_specs=tuple(out_specs), scratch_shapes=scratch, compiler_params=_params(1))(*args)


CONV_HALO = 8
CONF_HALO = 32


def _ssd_pre(xbc, dt_raw, conv_w, conv_b, dt_bias, name):
    n, c = xbc.shape
    tl = _tile(n, STRIP_ROW_TILE)
    hb = CONV_HALO
    k_taps = 4

    nu = _strip_units_per_trip(tl)
    nt = tl // (nu * SUBLANES)

    def body(x_ref, xp_ref, dt_ref, w_ref, b_ref, dtb_ref, pre_ref, dts_ref):
        i = pl.program_id(0)
        row = lax.broadcasted_iota(jnp.int32, (SUBLANES, LANES), 0)
        for lc in range(c // LANES):
            ls = slice(lc * LANES, (lc + 1) * LANES)
            w = [_bcast_row(w_ref, k, ls) for k in range(k_taps)]
            b = _bcast_row(b_ref, 0, ls)
            xp0 = jnp.where(i > 0, xp_ref[:, ls], 0.0)

            def strip(t, carry):
                xs = _strip_units(x_ref, ls, t, nu, nt, halo_prev=xp0)
                for u in range(nu):
                    prev, cur = xs[u], xs[u + 1]
                    pre_ref[_unit_rows(t, u, nu), ls] = (
                        b + w[0] * _down(prev, cur, 3, row) + w[1] * _down(prev, cur, 2, row)
                        + w[2] * _down(prev, cur, 1, row) + w[3] * cur)
                return carry

            lax.fori_loop(0, nt, strip, 0)
        v = dt_ref[...] + dtb_ref[...]
        dts_ref[...] = jnp.maximum(v, 0.0) + jnp.log1p(jnp.exp(-jnp.abs(v)))

    return _pc(body, name=name, out_shape=(_sds((n, c)), _sds((n, LANES))), grid=(n // tl,),
               in_specs=[_row(tl, c), _prev(tl, hb, c), _row(tl, LANES), _const((SUBLANES, c)), _const((1, c)),
                         _const((1, LANES))],
               out_specs=(_row(tl, c), _row(tl, LANES)), compiler_params=_params(1))(
        xbc, xbc, dt_raw, conv_w, conv_b, dt_bias)


def _ssd_pre_bwd(dpre, xbc, ddt, dt_raw, conv_w, dt_bias, name):
    n, c = xbc.shape
    tl = _tile(n, STRIP_ROW_TILE)
    nb = n // tl
    hb = CONV_HALO
    k_taps = 4

    nu = _strip_units_per_trip(tl)
    nt = tl // (nu * SUBLANES)

    def body(dp_ref, dpn_ref, x_ref, xp_ref, ddt_ref, dt_ref, w_ref, dtb_ref,
             dx_ref, ddr_ref, dw_ref, db_ref, ddtb_ref, acc_w, acc_b, acc_dtb):
        i = pl.program_id(0)

        @pl.when(i == 0)
        def _():
            acc_w[...] = jnp.zeros_like(acc_w)
            acc_b[...] = jnp.zeros_like(acc_b)
            acc_dtb[...] = jnp.zeros_like(acc_dtb)

        row = lax.broadcasted_iota(jnp.int32, (SUBLANES, LANES), 0)
        zero = jnp.zeros((SUBLANES, LANES), F32)
        for lc in range(c // LANES):
            ls = slice(lc * LANES, (lc + 1) * LANES)
            w = [_bcast_row(w_ref, k, ls) for k in range(k_taps)]
            xp0 = jnp.where(i > 0, xp_ref[:, ls], 0.0)
            dpn0 = jnp.where(i < nb - 1, dpn_ref[:, ls], 0.0)

            def strip(t, carry):
                acc = list(carry)
                dps = _strip_units(dp_ref, ls, t, nu, nt, halo_next=dpn0)
                xs = _strip_units(x_ref, ls, t, nu, nt, halo_prev=xp0)
                dxs = []
                for u in range(nu):
                    d, dn = dps[u], dps[u + 1]
                    dxs.append(w[3] * d + w[2] * _up(d, dn, 1, row) + w[1] * _up(d, dn, 2, row)
                               + w[0] * _up(d, dn, 3, row))
                    prev, cur = xs[u], xs[u + 1]
                    acc[3] = acc[3] + d * cur
                    for k in range(3):
                        acc[k] = acc[k] + d * _down(prev, cur, 3 - k, row)
                    acc[4] = acc[4] + d
                for p in range(nu // 2):
                    dx_ref[_pair_rows(t, p, nu), ls] = jnp.concatenate(dxs[2 * p:2 * p + 2], axis=0).astype(BF16)
                return tuple(acc)

            res = lax.fori_loop(0, nt, strip, (zero,) * 5)
            for k in range(k_taps):
                acc_w[k, :, ls] += res[k]
            acc_b[:, ls] += res[4]
        ddr = ddt_ref[...] * _sig(dt_ref[...] + dtb_ref[...])
        ddr_ref[...] = ddr.astype(BF16)
        acc_dtb[...] += _fold(ddr)

        @pl.when(i == nb - 1)
        def _():
            dw_ref[...] = jnp.zeros_like(dw_ref)
            for k in range(k_taps):
                dw_ref[k:k + 1, :] = jnp.sum(acc_w[k], axis=0, keepdims=True)
            db_ref[...] = jnp.sum(acc_b[...], axis=0, keepdims=True)
            ddtb_ref[...] = jnp.sum(acc_dtb[...], axis=0, keepdims=True)

    return _pc(body, name=name,
               out_shape=(_sds((n, c), BF16), _sds((n, LANES), BF16), _sds((SUBLANES, c)), _sds((1, c)),
                          _sds((1, LANES))),
               grid=(nb,),
               in_specs=[_row(tl, c), _next(tl, hb, c, n), _row(tl, c), _prev(tl, hb, c), _row(tl, LANES),
                         _row(tl, LANES), _const((SUBLANES, c)), _const((1, LANES))],
               out_specs=(_row(tl, c), _row(tl, LANES), _const((SUBLANES, c)), _const((1, c)), _const((1, LANES))),
               scratch_shapes=[pltpu.VMEM((k_taps, SUBLANES, c), F32), pltpu.VMEM((SUBLANES, c), F32),
                               pltpu.VMEM((SUBLANES, LANES), F32)],
               compiler_params=_params(1))(dpre, dpre, xbc, xbc, ddt, dt_raw, conv_w, dt_bias)


def _expand_mat():
    r = lax.broadcasted_iota(jnp.int32, (LANES, SSD_INNER), 0)
    c = lax.broadcasted_iota(jnp.int32, (LANES, SSD_INNER), 1)
    return (jnp.right_shift(c, 6) == r).astype(BF16)


def _reduce_mat():
    r = lax.broadcasted_iota(jnp.int32, (SSD_INNER, LANES), 0)
    c = lax.broadcasted_iota(jnp.int32, (SSD_INNER, LANES), 1)
    return (jnp.right_shift(r, 6) == c).astype(BF16)


def _ssd_common(pre, dt, alog):
    q = SSD_CHUNK
    sg = _sig(pre)
    act = pre * sg
    lane = lax.broadcasted_iota(jnp.int32, (1, LANES), 1)
    a_neg = jnp.where(lane < SSD_HEADS, -jnp.exp(alog), 0.0)
    rr = lax.broadcasted_iota(jnp.int32, (q, q), 0)
    cc = lax.broadcasted_iota(jnp.int32, (q, q), 1)
    causal = rr >= cc
    cum = _sel_dot(causal.astype(BF16), dt * a_neg)
    e_mat = _expand_mat()
    dtx = _dot_sel(dt, e_mat)
    cumx = _dot_sel(cum, e_mat)
    return sg, act, a_neg, causal, cum, e_mat, dtx, cumx


def _ssd_scan(pre, dt, alog, dvec, name):
    n = pre.shape[0]
    q = SSD_CHUNK
    nc = n // q

    def body(pre_ref, dt_ref, alog_ref, d_ref, y_ref, hp_ref, state):
        i = pl.program_id(0)

        @pl.when(i == 0)
        def _():
            state[...] = jnp.zeros_like(state)

        dt_ = dt_ref[...]
        _, act, _, causal, cum, e_mat, dtx, cumx = _ssd_common(pre_ref[...], dt_, alog_ref[...])
        xs = act[:, :SSD_INNER]
        bm = act[:, SSD_INNER:SSD_INNER + LANES]
        cm = act[:, SSD_INNER + LANES:]
        cum_t = cum.T
        clx = cumx[q - 1:q, :]
        xc = xs * dtx
        xd = xc * jnp.exp(clx - cumx)
        doutx = jnp.exp(cumx)
        edec = jnp.exp(clx)
        dx_row = _dot_sel(jnp.broadcast_to(d_ref[...], (SUBLANES, LANES)), e_mat)[0:1, :]
        hp_ref[0] = state[...]
        bb = bm.astype(BF16)
        cb = cm.astype(BF16)
        lane = lax.broadcasted_iota(jnp.int32, (1, LANES), 1)
        row = lax.broadcasted_iota(jnp.int32, (LANES, 1), 0)
        cbs = []
        for g in range(2):
            cg = jnp.where(jnp.right_shift(lane, 6) == g, cm, 0.0).astype(BF16)
            cbs.append(_dot(cg, bb, NT))
        for j in range(SSD_HEADS // 2):
            sl = slice(j * LANES, (j + 1) * LANES)
            g = j // 4
            xcj = xc[:, sl].astype(BF16)
            halves = []
            for half in range(2):
                h = 2 * j + half
                seg = cum[:, h:h + 1] - cum_t[h:h + 1, :]
                w = cbs[g] * jnp.exp(jnp.where(causal, seg, -jnp.inf))
                halves.append(_dot(w.astype(BF16), xcj))
            y_diag = jnp.where(lane < SSD_HEAD_DIM, halves[0], halves[1])
            hj = state[:, sl]
            y_off = doutx[:, sl] * _dot(cb, hj.astype(BF16))
            y_ref[:, sl] = y_diag + y_off + xs[:, sl] * dx_row[:, sl]
            st = _dot(bb, xd[:, sl].astype(BF16), TN)
            state[:, sl] = hj * edec[:, sl] + jnp.where(jnp.right_shift(row, 6) == g, st, 0.0)

    return _pc(body, name=name, out_shape=(_sds((n, SSD_INNER)), _sds((nc, LANES, SSD_INNER))), grid=(nc,),
               in_specs=[_row(q, SSD_XBC), _row(q, LANES), _const((1, LANES)), _const((1, LANES))],
               out_specs=(_row(q, SSD_INNER), pl.BlockSpec((1, LANES, SSD_INNER), lambda i: (i, 0, 0))),
               scratch_shapes=[pltpu.VMEM((LANES, SSD_INNER), F32)], compiler_params=_params(1))(pre, dt, alog, dvec)


def _ssd_scan_bwd(pre, dt, hprev, dy, alog, dvec, name):
    n = pre.shape[0]
    q = SSD_CHUNK
    nc = n // q

    def body(pre_ref, dt_ref, hp_ref, dy_ref, alog_ref, d_ref, dpre_ref, ddt_ref, da_ref, dd_ref,
             d_state, dxc_s, dcx_s, dcl_s, acc_a, acc_d):
        i = pl.program_id(0)

        @pl.when(i == 0)
        def _():
            d_state[...] = jnp.zeros_like(d_state)
            acc_a[...] = jnp.zeros_like(acc_a)
            acc_d[...] = jnp.zeros_like(acc_d)

        pre_ = pre_ref[...]
        dt_ = dt_ref[...]
        sg, act, a_neg, causal, cum, e_mat, dtx, cumx = _ssd_common(pre_, dt_, alog_ref[...])
        r_mat = _reduce_mat()
        xs = act[:, :SSD_INNER]
        bm = act[:, SSD_INNER:SSD_INNER + LANES]
        cm = act[:, SSD_INNER + LANES:]
        cum_t = cum.T
        clx = cumx[q - 1:q, :]
        xc = xs * dtx
        dsx = jnp.exp(clx - cumx)
        doutx = jnp.exp(cumx)
        edec = jnp.exp(clx)
        dx_row = _dot_sel(jnp.broadcast_to(d_ref[...], (SUBLANES, LANES)), e_mat)[0:1, :]
        dy_ = dy_ref[...]
        acc_d[...] += _fold(dy_ * xs)
        bb = bm.astype(BF16)
        cb = cm.astype(BF16)
        lane = lax.broadcasted_iota(jnp.int32, (1, LANES), 1)
        row = lax.broadcasted_iota(jnp.int32, (LANES, 1), 0)
        d_c = jnp.zeros((q, LANES), F32)
        d_b = jnp.zeros((q, LANES), F32)

        for j in range(SSD_HEADS // 2):
            sl = slice(j * LANES, (j + 1) * LANES)
            g = j // 4
            hj = hp_ref[0, :, sl]
            hjb = hj.astype(BF16)
            dyj = dy_[:, sl]
            tj = _dot(cb, hjb)
            dtj = (doutx[:, sl] * dyj).astype(BF16)
            dcx = dyj * tj * doutx[:, sl]
            d_c = d_c + _dot(dtj, hjb, NT)
            dhn = d_state[:, sl]
            dhp = dhn * edec[:, sl] + jnp.where(jnp.right_shift(row, 6) == g, _dot(cb, dtj, TN), 0.0)
            dcl = jnp.sum(dhn * hj, axis=0, keepdims=True) * edec[:, sl]
            dsb = dhn.astype(BF16)
            dxd = _dot(bb, dsb)
            xcj = xc[:, sl]
            dsj = dsx[:, sl]
            d_b = d_b + _dot((xcj * dsj).astype(BF16), dsb, NT)
            dds = dxd * xcj * dsj
            d_state[:, sl] = dhp
            dxc_s[:, sl] = dxd * dsj
            dcx_s[:, sl] = dcx - dds
            dcl_s[:, sl] = jnp.broadcast_to(dcl + jnp.sum(dds, axis=0, keepdims=True), (SUBLANES, LANES))

        dcum_c = jnp.zeros((q, LANES), F32)
        dcum_t = jnp.zeros((LANES, q), F32)
        for g in range(2):
            gmask = jnp.right_shift(lane, 6) == g
            cg = jnp.where(gmask, cm, 0.0).astype(BF16)
            cbg = _dot(cg, bb, NT)
            d_cb = jnp.zeros((q, q), F32)
            for hh in range(SSD_HEADS // 2):
                h = g * (SSD_HEADS // 2) + hh
                j, half = h // 2, h % 2
                sl = slice(j * LANES, (j + 1) * LANES)
                hmask = jnp.right_shift(lane, 6) == half
                seg = cum[:, h:h + 1] - cum_t[h:h + 1, :]
                lm = jnp.exp(jnp.where(causal, seg, -jnp.inf))
                w = cbg * lm
                dyj = dy_[:, sl]
                dw = _dot(jnp.where(hmask, dyj, 0.0).astype(BF16), xc[:, sl].astype(BF16), NT)
                dxch = _dot(w.astype(BF16), dyj.astype(BF16), TN)
                dxc_s[:, sl] += jnp.where(hmask, dxch, 0.0)
                d_cb = d_cb + dw * lm
                m = dw * w
                dcum_c = dcum_c + jnp.sum(m, axis=1, keepdims=True) * (lane == h).astype(F32)
                dcum_t = dcum_t + (row == h).astype(F32) * jnp.sum(m, axis=0, keepdims=True)
            d_cbb = d_cb.astype(BF16)
            d_c = d_c + jnp.where(gmask, _dot(d_cbb, bb), 0.0)
            d_b = d_b + jnp.where(gmask, _dot(d_cbb, cb, TN), 0.0)

        dcl_row = _dot_sel(dcl_s[...], r_mat)[0:1, :]
        rowq = lax.broadcasted_iota(jnp.int32, (q, 1), 0)
        dcum = (dcum_c - dcum_t.T + _dot_sel(dcx_s[...], r_mat)
                + jnp.where(rowq == q - 1, dcl_row, 0.0))
        rr = lax.broadcasted_iota(jnp.int32, (q, q), 0)
        cc = lax.broadcasted_iota(jnp.int32, (q, q), 1)
        dadt = _sel_dot((rr <= cc).astype(BF16), dcum)
        dxc = dxc_s[...]
        ddt_ref[...] = dadt * a_neg + _dot_sel(dxc * xs, r_mat)
        acc_a[...] += _fold(dadt * dt_)
        dsilu = sg * (1.0 + pre_ * (1.0 - sg))
        dpre_ref[:, :SSD_INNER] = (dxc * dtx + dy_ * dx_row) * dsilu[:, :SSD_INNER]
        dpre_ref[:, SSD_INNER:SSD_INNER + LANES] = d_b * dsilu[:, SSD_INNER:SSD_INNER + LANES]
        dpre_ref[:, SSD_INNER + LANES:] = d_c * dsilu[:, SSD_INNER + LANES:]

        @pl.when(i == nc - 1)
        def _():
            da_ref[...] = jnp.sum(acc_a[...], axis=0, keepdims=True) * a_neg
            dd_ref[...] = jnp.sum(_dot_sel(acc_d[...], r_mat), axis=0, keepdims=True)

    rev = lambda i: (nc - 1 - i, 0)
    return _pc(body, name=name,
               out_shape=(_sds((n, SSD_XBC)), _sds((n, LANES)), _sds((1, LANES)), _sds((1, LANES))), grid=(nc,),
               in_specs=[pl.BlockSpec((q, SSD_XBC), rev), pl.BlockSpec((q, LANES), rev),
                         pl.BlockSpec((1, LANES, SSD_INNER), lambda i: (nc - 1 - i, 0, 0)),
                         pl.BlockSpec((q, SSD_INNER), rev), _const((1, LANES)), _const((1, LANES))],
               out_specs=(pl.BlockSpec((q, SSD_XBC), rev), pl.BlockSpec((q, LANES), rev), _const((1, LANES)),
                          _const((1, LANES))),
               scratch_shapes=[pltpu.VMEM((LANES, SSD_INNER), F32), pltpu.VMEM((q, SSD_INNER), F32),
                               pltpu.VMEM((q, SSD_INNER), F32), pltpu.VMEM((SUBLANES, SSD_INNER), F32),
                               pltpu.VMEM((SUBLANES, LANES), F32), pltpu.VMEM((SUBLANES, SSD_INNER), F32)],
               compiler_params=_params(1))(pre, dt, hprev, dy, alog, dvec)


def _group_norm_parts(v):
    half = SSD_INNER // 2
    r0 = lax.rsqrt(jnp.mean(v[:, :half] * v[:, :half], axis=-1, keepdims=True) + EPS)
    r1 = lax.rsqrt(jnp.mean(v[:, half:] * v[:, half:], axis=-1, keepdims=True) + EPS)
    lane = lax.broadcasted_iota(jnp.int32, (1, SSD_INNER), 1)
    return jnp.where(lane < half, r0, r1)


def _group_mean(v):
    half = SSD_INNER // 2
    m0 = jnp.mean(v[:, :half], axis=-1, keepdims=True)
    m1 = jnp.mean(v[:, half:], axis=-1, keepdims=True)
    lane = lax.broadcasted_iota(jnp.int32, (1, SSD_INNER), 1)
    return jnp.where(lane < half, m0, m1)


def _ssd_post(y, z, g, name):
    n, d = y.shape
    tl = _tile(n)

    def body(y_ref, z_ref, g_ref, o_ref):
        z_ = z_ref[...]
        v = y_ref[...] * (z_ * _sig(z_))
        o_ref[...] = ((v * _group_norm_parts(v)) * g_ref[...]).astype(BF16)

    return _pc(body, name=name, out_shape=_sds((n, d), BF16), grid=(n // tl,),
               in_specs=[_row(tl, d), _row(tl, d), _const((1, d))], out_specs=_row(tl, d),
               compiler_params=_params(1))(y, z, g)


def _ssd_post_bwd(dout, y, z, g, name):
    n, d = y.shape
    tl = _tile(n)
    nb = n // tl

    def body(do_ref, y_ref, z_ref, g_ref, dy_ref, dz_ref, dg_ref, acc_g):
        i = pl.program_id(0)

        @pl.when(i == 0)
        def _():
            acc_g[...] = jnp.zeros_like(acc_g)

        z_ = z_ref[...]
        y_ = y_ref[...]
        sz = _sig(z_)
        silu_z = z_ * sz
        v = y_ * silu_z
        rs = _group_norm_parts(v)
        nv = v * rs
        do_ = do_ref[...]
        acc_g[...] += _fold(do_ * nv)
        dn = do_ * g_ref[...]
        dv = rs * (dn - nv * _group_mean(dn * nv))
        dy_ref[...] = dv * silu_z
        dz_ref[...] = (dv * y_ * (sz * (1.0 + z_ * (1.0 - sz)))).astype(BF16)

        @pl.when(i == nb - 1)
        def _():
            dg_ref[...] = jnp.sum(acc_g[...], axis=0, keepdims=True)

    return _pc(body, name=name, out_shape=(_sds((n, d)), _sds((n, d), BF16), _sds((1, d))), grid=(nb,),
               in_specs=[_row(tl, d), _row(tl, d), _row(tl, d), _const((1, d))],
               out_specs=(_row(tl, d), _row(tl, d), _const((1, d))),
               scratch_shapes=[pltpu.VMEM((SUBLANES, d), F32)], compiler_params=_params(1))(dout, y, z, g)


def _layer_norm_parts(uc):
    mu = jnp.mean(uc, axis=-1, keepdims=True)
    xc = uc - mu
    rstd = lax.rsqrt(jnp.mean(xc * xc, axis=-1, keepdims=True) + EPS)
    return xc * rstd, rstd


def _rows_x8(w):
    return jnp.broadcast_to(w[:, None, :], (w.shape[0], SUBLANES, w.shape[1]))


def _glu_units(x_ref, ls, gl, t, nu, halo):
    nh = len(halo)
    units = []
    for h in range(nh):
        rows = pl.ds(pl.multiple_of(jnp.maximum(t * nu - nh + h, 0) * SUBLANES, SUBLANES), SUBLANES)
        units.append(jnp.where(t > 0, x_ref[rows, ls] * _sig(x_ref[rows, gl]), halo[h]))
    for u in range(nu):
        rows = _unit_rows(t, u, nu)
        units.append(x_ref[rows, ls] * _sig(x_ref[rows, gl]))
    return units


def _memo_rolls(units):
    memo = {}

    def rolls(key):
        if key not in memo:
            memo[key] = pltpu.roll(units[key[0]], key[1], 0)
        return memo[key]

    return rolls


def _window(units, e, s, rolls, row, up):
    a, b = divmod(s, SUBLANES)
    if b == 0:
        return units[e + a] if up else units[e - a]
    if up:
        sh = SUBLANES - b
        return jnp.where(row < sh, rolls((e + a, sh)), rolls((e + a + 1, sh)))
    return jnp.where(row < b, rolls((e - a - 1, b)), rolls((e - a, b)))


def _conf_fwd(conf_in, conv_w, conv_b, ln_g, ln_b, name):
    n = conf_in.shape[0]
    c = CONF_WIDTH
    tl = _tile(n, STRIP_ROW_TILE)
    hb = CONF_HALO
    k_taps = CONF_KERNEL

    nu = _strip_units_per_trip(tl)
    nt = tl // (nu * SUBLANES)
    nh = hb // SUBLANES

    def body(x_ref, xp_ref, w_ref, b_ref, g_ref, beta_ref, o_ref, uc_ref):
        i = pl.program_id(0)
        row = lax.broadcasted_iota(jnp.int32, (SUBLANES, LANES), 0)
        for lc in range(c // LANES):
            ls = slice(lc * LANES, (lc + 1) * LANES)
            gl = slice(c + lc * LANES, c + (lc + 1) * LANES)
            bias = _bcast_row(b_ref, 0, ls)
            halo = [jnp.where(i > 0, xp_ref[SUBLANES * h:SUBLANES * (h + 1), ls]
                              * _sig(xp_ref[SUBLANES * h:SUBLANES * (h + 1), gl]), 0.0) for h in range(nh)]

            def strip(t, carry):
                units = _glu_units(x_ref, ls, gl, t, nu, halo)
                rolls = _memo_rolls(units)
                for u in range(nu):
                    acc = bias
                    for k in range(k_taps):
                        acc = acc + w_ref[k, :, ls] * _window(units, u + nh, k_taps - 1 - k, rolls, row, up=False)
                    uc_ref[_unit_rows(t, u, nu), ls] = acc
                return carry

            lax.fori_loop(0, nt, strip, 0)
        nv, _ = _layer_norm_parts(uc_ref[...])
        v = nv * g_ref[...] + beta_ref[...]
        o_ref[...] = (v * _sig(v)).astype(BF16)

    return _pc(body, name=name, out_shape=(_sds((n, c), BF16), _sds((n, c))), grid=(n // tl,),
               in_specs=[_row(tl, 2 * c), _prev(tl, hb, 2 * c), _const((hb, SUBLANES, c)), _const((1, c)),
                         _const((1, c)), _const((1, c))],
               out_specs=(_row(tl, c), _row(tl, c)), compiler_params=_params(1))(
        conf_in, conf_in, _rows_x8(conv_w), conv_b, ln_g, ln_b)


def _conf_bwd(dout, uc, conf_in, conv_w, ln_g, ln_b, name):
    n = conf_in.shape[0]
    c = CONF_WIDTH
    tl = _tile(n, STRIP_ROW_TILE)
    nb = n // tl
    hb = CONF_HALO
    k_taps = CONF_KERNEL

    nu = _strip_units_per_trip(tl)
    nt = tl // (nu * SUBLANES)
    nh = hb // SUBLANES

    def body(do_ref, don_ref, uc_ref, ucn_ref, x_ref, xp_ref, w_ref, g_ref, beta_ref,
             dx_ref, dw_ref, db_ref, dg_ref, dbeta_ref, dbuf, acc_w, acc_b, acc_g, acc_beta):
        i = pl.program_id(0)

        @pl.when(i == 0)
        def _():
            acc_w[...] = jnp.zeros_like(acc_w)
            acc_b[...] = jnp.zeros_like(acc_b)
            acc_g[...] = jnp.zeros_like(acc_g)
            acc_beta[...] = jnp.zeros_like(acc_beta)

        g_ = g_ref[...]
        beta_ = beta_ref[...]

        def d_conv_out(do_, uc_):
            nv, rstd = _layer_norm_parts(uc_)
            v = nv * g_ + beta_
            sv = _sig(v)
            dv = do_ * (sv * (1.0 + v * (1.0 - sv)))
            dn = dv * g_
            duc = rstd * (dn - jnp.mean(dn, axis=-1, keepdims=True)
                          - nv * jnp.mean(dn * nv, axis=-1, keepdims=True))
            return duc, dv, nv

        duc, dv, nv = d_conv_out(do_ref[...], uc_ref[...])
        acc_g[...] += _fold(dv * nv)
        acc_beta[...] += _fold(dv)
        acc_b[...] += _fold(duc)
        dbuf[pl.ds(0, tl), :] = duc
        ducn, _, _ = d_conv_out(don_ref[...], ucn_ref[...])
        dbuf[pl.ds(tl, hb), :] = jnp.where(i < nb - 1, ducn, 0.0)

        row = lax.broadcasted_iota(jnp.int32, (SUBLANES, LANES), 0)
        for lc in range(c // LANES):
            ls = slice(lc * LANES, (lc + 1) * LANES)
            gl = slice(c + lc * LANES, c + (lc + 1) * LANES)
            halo = [jnp.where(i > 0, xp_ref[SUBLANES * h:SUBLANES * (h + 1), ls]
                              * _sig(xp_ref[SUBLANES * h:SUBLANES * (h + 1), gl]), 0.0) for h in range(nh)]

            def strip(t, carry):
                us = _glu_units(x_ref, ls, gl, t, nu, halo)
                ds = [dbuf[pl.ds(pl.multiple_of((t * nu + u) * SUBLANES, SUBLANES), SUBLANES), ls]
                      for u in range(nu + nh)]
                u_rolls, d_rolls = _memo_rolls(us), _memo_rolls(ds)
                for k in range(k_taps):
                    part = ds[0] * _window(us, nh, k_taps - 1 - k, u_rolls, row, up=False)
                    for u in range(1, nu):
                        part = part + ds[u] * _window(us, u + nh, k_taps - 1 - k, u_rolls, row, up=False)
                    acc_w[k, :, ls] += part
                dus = []
                for u in range(nu):
                    du = w_ref[0, :, ls] * _window(ds, u, k_taps - 1, d_rolls, row, up=True)
                    for k in range(1, k_taps):
                        du = du + w_ref[k, :, ls] * _window(ds, u, k_taps - 1 - k, d_rolls, row, up=True)
                    dus.append(du)
                for p in range(nu // 2):
                    rows = _pair_rows(t, p, nu)
                    du2 = jnp.concatenate(dus[2 * p:2 * p + 2], axis=0)
                    val, sgate = x_ref[rows, ls], _sig(x_ref[rows, gl])
                    dx_ref[rows, ls] = (du2 * sgate).astype(BF16)
                    dx_ref[rows, gl] = (du2 * val * sgate * (1.0 - sgate)).astype(BF16)
                return carry

            lax.fori_loop(0, nt, strip, 0)

        @pl.when(i == nb - 1)
        def _():
            dw_ref[...] = jnp.zeros_like(dw_ref)
            for k in range(k_taps):
                dw_ref[k:k + 1, :] = jnp.sum(acc_w[k], axis=0, keepdims=True)
            db_ref[...] = jnp.sum(acc_b[...], axis=0, keepdims=True)
            dg_ref[...] = jnp.sum(acc_g[...], axis=0, keepdims=True)
            dbeta_ref[...] = jnp.sum(acc_beta[...], axis=0, keepdims=True)

    vec = _sds((1, c))
    return _pc(body, name=name, out_shape=(_sds((n, 2 * c), BF16), _sds((hb, c)), vec, vec, vec), grid=(nb,),
               in_specs=[_row(tl, c), _next(tl, hb, c, n), _row(tl, c), _next(tl, hb, c, n), _row(tl, 2 * c),
                         _prev(tl, hb, 2 * c), _const((hb, SUBLANES, c)), _const((1, c)), _const((1, c))],
               out_specs=(_row(tl, 2 * c), _const((hb, c)), _const((1, c)), _const((1, c)), _const((1, c))),
               scratch_shapes=[pltpu.VMEM((tl + hb, c), F32),
                               pltpu.VMEM((k_taps, SUBLANES, c), F32), pltpu.VMEM((SUBLANES, c), F32),
                               pltpu.VMEM((SUBLANES, c), F32), pltpu.VMEM((SUBLANES, c), F32)],
               compiler_params=_params(1))(dout, dout, uc, uc, conf_in, conf_in, _rows_x8(conv_w), ln_g, ln_b)


def _sc_fwd(sc_in, conv_w, name):
    n = sc_in.shape[0]
    c = SC_WIDTH
    tl = _tile(n, STRIP_ROW_TILE)
    hb = CONV_HALO

    nu = _strip_units_per_trip(tl)
    nt = tl // (nu * SUBLANES)

    def body(x_ref, xp_ref, w_ref, o_ref):
        i = pl.program_id(0)
        row = lax.broadcasted_iota(jnp.int32, (SUBLANES, LANES), 0)
        for lc in range(c // LANES):
            ls = slice(lc * LANES, (lc + 1) * LANES)
            gc_ls = slice(c + lc * LANES, c + (lc + 1) * LANES)
            xv_ls = slice(2 * c + lc * LANES, 2 * c + (lc + 1) * LANES)
            w = [_bcast_row(w_ref, k, ls) for k in range(3)]
            gc0 = jnp.where(i > 0, xp_ref[:, gc_ls], 0.0)
            xv0 = xp_ref[:, xv_ls]

            def strip(t, carry):
                gcs = _strip_units(x_ref, gc_ls, t, nu, nt, halo_prev=gc0)
                xvs = _strip_units(x_ref, xv_ls, t, nu, nt, halo_prev=xv0)
                ps = [a * b for a, b in zip(gcs, xvs)]
                outs = []
                for u in range(nu):
                    prev, cur = ps[u], ps[u + 1]
                    cv = w[0] * _down(prev, cur, 2, row) + w[1] * _down(prev, cur, 1, row) + w[2] * cur
                    outs.append(x_ref[_unit_rows(t, u, nu), ls] * cv)
                for p in range(nu // 2):
                    o_ref[_pair_rows(t, p, nu), ls] = jnp.concatenate(outs[2 * p:2 * p + 2], axis=0).astype(BF16)
                return carry

            lax.fori_loop(0, nt, strip, 0)

    return _pc(body, name=name, out_shape=_sds((n, c), BF16), grid=(n // tl,),
               in_specs=[_row(tl, 3 * c), _prev(tl, hb, 3 * c), _const((SUBLANES, c))], out_specs=_row(tl, c),
               compiler_params=_params(1))(sc_in, sc_in, conv_w)


def _sc_bwd(dout, sc_in, conv_w, name):
    n = sc_in.shape[0]
    c = SC_WIDTH
    tl = _tile(n, STRIP_ROW_TILE)
    nb = n // tl
    hb = CONV_HALO

    nu = _strip_units_per_trip(tl)
    nt = tl // (nu * SUBLANES)

    def body(do_ref, don_ref, x_ref, xp_ref, xn_ref, w_ref, dx_ref, dw_ref, acc_w):
        i = pl.program_id(0)

        @pl.when(i == 0)
        def _():
            acc_w[...] = jnp.zeros_like(acc_w)

        row = lax.broadcasted_iota(jnp.int32, (SUBLANES, LANES), 0)
        zero = jnp.zeros((SUBLANES, LANES), F32)
        for lc in range(c // LANES):
            ls = slice(lc * LANES, (lc + 1) * LANES)
            gc_ls = slice(c + lc * LANES, c + (lc + 1) * LANES)
            xv_ls = slice(2 * c + lc * LANES, 2 * c + (lc + 1) * LANES)
            w = [_bcast_row(w_ref, k, ls) for k in range(3)]
            gc0 = jnp.where(i > 0, xp_ref[:, gc_ls], 0.0)
            xv0 = xp_ref[:, xv_ls]
            don0 = jnp.where(i < nb - 1, don_ref[:, ls], 0.0)
            gbn0 = xn_ref[:, ls]

            def strip(t, carry):
                acc = list(carry)
                dos = _strip_units(do_ref, ls, t, nu, nt, halo_next=don0)
                gbs = _strip_units(x_ref, ls, t, nu, nt, halo_next=gbn0)
                gcs = _strip_units(x_ref, gc_ls, t, nu, nt, halo_prev=gc0)
                xvs = _strip_units(x_ref, xv_ls, t, nu, nt, halo_prev=xv0)
                dcv = [a * b for a, b in zip(dos, gbs)]
                ps = [a * b for a, b in zip(gcs, xvs)]
                d_gb, d_gc, d_xv = [], [], []
                for u in range(nu):
                    prev, cur = ps[u], ps[u + 1]
                    p1, p2 = _down(prev, cur, 1, row), _down(prev, cur, 2, row)
                    d, dn = dcv[u], dcv[u + 1]
                    dp = w[2] * d + w[1] * _up(d, dn, 1, row) + w[0] * _up(d, dn, 2, row)
                    d_gb.append(dos[u] * (w[0] * p2 + w[1] * p1 + w[2] * cur))
                    d_gc.append(dp * xvs[u + 1])
                    d_xv.append(dp * gcs[u + 1])
                    acc[0] = acc[0] + d * p2
                    acc[1] = acc[1] + d * p1
                    acc[2] = acc[2] + d * cur
                for p in range(nu // 2):
                    rows = _pair_rows(t, p, nu)
                    for vals, lanes in ((d_gb, ls), (d_gc, gc_ls), (d_xv, xv_ls)):
                        dx_ref[rows, lanes] = jnp.concatenate(vals[2 * p:2 * p + 2], axis=0).astype(BF16)
                return tuple(acc)

            res = lax.fori_loop(0, nt, strip, (zero,) * 3)
            for k in range(3):
                acc_w[k, :, ls] += res[k]

        @pl.when(i == nb - 1)
        def _():
            dw_ref[...] = jnp.zeros_like(dw_ref)
            for k in range(3):
                dw_ref[k:k + 1, :] = jnp.sum(acc_w[k], axis=0, keepdims=True)

    return _pc(body, name=name, out_shape=(_sds((n, 3 * c), BF16), _sds((SUBLANES, c))), grid=(nb,),
               in_specs=[_row(tl, c), _next(tl, hb, c, n), _row(tl, 3 * c), _prev(tl, hb, 3 * c),
                         _next(tl, hb, c, n), _const((SUBLANES, c))],
               out_specs=(_row(tl, 3 * c), _const((SUBLANES, c))),
               scratch_shapes=[pltpu.VMEM((3, SUBLANES, c), F32)],
               compiler_params=_params(1))(dout, dout, sc_in, sc_in, sc_in, conv_w)


def _merge_fwd(gates, ya, yb, yc, b_gate, name):
    n, d = ya.shape
    tl = _tile(n)

    def body(gt_ref, ya_ref, yb_ref, yc_ref, b_ref, o_ref):
        gt = _sig(gt_ref[...] + b_ref[...])
        o_ref[...] = (gt[:, :d] * ya_ref[...] + gt[:, d:2 * d] * yb_ref[...] + gt[:, 2 * d:] * yc_ref[...]).astype(BF16)

    return _pc(body, name=name, out_shape=_sds((n, d), BF16), grid=(n // tl,),
               in_specs=[_row(tl, 3 * d), _row(tl, d), _row(tl, d), _row(tl, d), _const((1, 3 * d))],
               out_specs=_row(tl, d), compiler_params=_params(1))(gates, ya, yb, yc, b_gate)


def _merge_bwd(dm, gates, ya, yb, yc, b_gate, name):
    n, d = ya.shape
    tl = _tile(n)
    nb = n // tl

    def body(dm_ref, gt_ref, ya_ref, yb_ref, yc_ref, b_ref, dya_ref, dyb_ref, dyc_ref, dgt_ref, db_ref, acc):
        i = pl.program_id(0)

        @pl.when(i == 0)
        def _():
            acc[...] = jnp.zeros_like(acc)

        dm_ = dm_ref[...]
        gt = _sig(gt_ref[...] + b_ref[...])
        for idx, (y_ref, dy_ref) in enumerate(((ya_ref, dya_ref), (yb_ref, dyb_ref), (yc_ref, dyc_ref))):
            gk = gt[:, idx * d:(idx + 1) * d]
            dy_ref[...] = (dm_ * gk).astype(BF16)
            dpre = dm_ * y_ref[...] * gk * (1.0 - gk)
            dgt_ref[:, idx * d:(idx + 1) * d] = dpre.astype(BF16)
            acc[:, idx * d:(idx + 1) * d] += _fold(dpre)

        @pl.when(i == nb - 1)
        def _():
            db_ref[...] = jnp.sum(acc[...], axis=0, keepdims=True)

    bf = _sds((n, d), BF16)
    return _pc(body, name=name, out_shape=(bf, bf, bf, _sds((n, 3 * d), BF16), _sds((1, 3 * d))), grid=(nb,),
               in_specs=[_row(tl, d), _row(tl, 3 * d), _row(tl, d), _row(tl, d), _row(tl, d), _const((1, 3 * d))],
               out_specs=(_row(tl, d), _row(tl, d), _row(tl, d), _row(tl, 3 * d), _const((1, 3 * d))),
               scratch_shapes=[pltpu.VMEM((SUBLANES, 3 * d), F32)], compiler_params=_params(1))(
        dm, gates, ya, yb, yc, b_gate)


FFN_COLS = 1408
FFN_STRIP = 64
FFN_STRIP_BWD = 32


def _down(prev, cur, s, row):
    return jnp.where(row < s, pltpu.roll(prev, s, 0), pltpu.roll(cur, s, 0))


def _up(cur, nxt, s, row):
    return jnp.where(row < SUBLANES - s, pltpu.roll(cur, SUBLANES - s, 0), pltpu.roll(nxt, SUBLANES - s, 0))


def _ffn_mid(up, conv_w, conv_b, name):
    n = up.shape[0]
    tl = _tile(n, STRIP_ROW_TILE)
    hb = CONV_HALO
    tc = FFN_COLS
    ncb = D_FF // tc

    def spec(shape_rows, idx_fn, off):
        return pl.BlockSpec((shape_rows, tc), lambda j, i, off=off: (idx_fn(i), j + off))

    r = tl // hb
    cur = lambda i: i
    prv = lambda i: jnp.maximum(i * r - 1, 0)

    def body(g_ref, gp_ref, v_ref, vp_ref, wg_ref, wv_ref, bg_ref, bv_ref, o_ref):
        i = pl.program_id(1)
        row = lax.broadcasted_iota(jnp.int32, (SUBLANES, LANES), 0)
        full = lambda ref, k, ls: jnp.broadcast_to(ref[k:k + 1, ls], (SUBLANES, LANES))
        for lc in range(tc // LANES):
            ls = slice(lc * LANES, (lc + 1) * LANES)
            wg = [full(wg_ref, k, ls) for k in range(3)]
            wv = [full(wv_ref, k, ls) for k in range(3)]
            bg, bv = full(bg_ref, 0, ls), full(bv_ref, 0, ls)

            def conv(prev, x, w, b):
                return b + w[0] * _down(prev, x, 2, row) + w[1] * _down(prev, x, 1, row) + w[2] * x

            def strip(t, carry):
                gs, vs = [carry[0]], [carry[1]]
                for u in range(FFN_STRIP // SUBLANES):
                    rows = pl.ds(pl.multiple_of(t * FFN_STRIP + u * SUBLANES, SUBLANES), SUBLANES)
                    gs.append(g_ref[rows, ls])
                    vs.append(v_ref[rows, ls])
                outs = []
                for u in range(FFN_STRIP // SUBLANES):
                    ug = conv(gs[u], gs[u + 1], wg, bg)
                    outs.append(ug * _sig(ug) * conv(vs[u], vs[u + 1], wv, bv))
                for p in range(FFN_STRIP // 16):
                    rows = pl.ds(pl.multiple_of(t * FFN_STRIP + p * 16, 16), 16)
                    o_ref[rows, ls] = jnp.concatenate(outs[2 * p:2 * p + 2], axis=0).astype(BF16)
                return gs[-1], vs[-1]

            lax.fori_loop(0, tl // FFN_STRIP, strip,
                          (jnp.where(i > 0, gp_ref[:, ls], 0.0), jnp.where(i > 0, vp_ref[:, ls], 0.0)))

    wspec = lambda off: pl.BlockSpec((SUBLANES, tc), lambda j, i, off=off: (0, j + off))
    bspec = lambda off: pl.BlockSpec((1, tc), lambda j, i, off=off: (0, j + off))
    return _pc(body, name=name, out_shape=_sds((n, D_FF), BF16), grid=(ncb, n // tl),
               in_specs=[spec(tl, cur, 0), spec(hb, prv, 0), spec(tl, cur, ncb), spec(hb, prv, ncb),
                         wspec(0), wspec(ncb), bspec(0), bspec(ncb)],
               out_specs=pl.BlockSpec((tl, tc), lambda j, i: (i, j)), compiler_params=_params(2))(
        up, up, up, up, conv_w, conv_w, conv_b, conv_b)


def _ffn_mid_bwd(da, up, conv_w, conv_b, name):
    n = up.shape[0]
    tl = _tile(n, STRIP_ROW_TILE)
    nb = n // tl
    nt = tl // FFN_STRIP_BWD
    hb = CONV_HALO
    tc = FFN_COLS
    ncb = D_FF // tc
    r = tl // hb
    last = n // hb - 1
    cur = lambda i: i
    prv = lambda i: jnp.maximum(i * r - 1, 0)
    nxt = lambda i: jnp.minimum((i + 1) * r, last)

    def spec(shape_rows, idx_fn, off):
        return pl.BlockSpec((shape_rows, tc), lambda j, i, off=off: (idx_fn(i), j + off))

    def body(da_ref, dan_ref, g_ref, gp_ref, gn_ref, v_ref, vp_ref, vn_ref, wg_ref, wv_ref, bg_ref, bv_ref,
             dg_ref, dv_ref, dwg_ref, dwv_ref, dbg_ref, dbv_ref, acc_w, acc_b):
        i = pl.program_id(1)

        @pl.when(i == 0)
        def _():
            acc_w[...] = jnp.zeros_like(acc_w)
            acc_b[...] = jnp.zeros_like(acc_b)

        row = lax.broadcasted_iota(jnp.int32, (SUBLANES, LANES), 0)
        full = lambda ref, k, ls: jnp.broadcast_to(ref[k:k + 1, ls], (SUBLANES, LANES))
        zero = jnp.zeros((SUBLANES, LANES), F32)

        def d_conv_out(da_, ug, uv):
            s = _sig(ug)
            return da_ * uv * (s * (1.0 + ug * (1.0 - s))), da_ * (ug * s)

        for lc in range(tc // LANES):
            ls = slice(lc * LANES, (lc + 1) * LANES)
            wg = [full(wg_ref, k, ls) for k in range(3)]
            wv = [full(wv_ref, k, ls) for k in range(3)]
            bg, bv = full(bg_ref, 0, ls), full(bv_ref, 0, ls)

            def unit(prev_g, g, prev_v, v, da_):
                g1, g2 = _down(prev_g, g, 1, row), _down(prev_g, g, 2, row)
                v1, v2 = _down(prev_v, v, 1, row), _down(prev_v, v, 2, row)
                ug = bg + wg[0] * g2 + wg[1] * g1 + wg[2] * g
                uv = bv + wv[0] * v2 + wv[1] * v1 + wv[2] * v
                dug, duv = d_conv_out(da_, ug, uv)
                return dug, duv, (g2, g1, g), (v2, v1, v)

            def d_in(d, d_next, w):
                return w[2] * d + w[1] * _up(d, d_next, 1, row) + w[0] * _up(d, d_next, 2, row)

            tail = pl.ds(tl - SUBLANES, SUBLANES)
            dgn, dvn, _, _ = unit(g_ref[tail, ls], gn_ref[:, ls], v_ref[tail, ls], vn_ref[:, ls], dan_ref[:, ls])
            dgn = jnp.where(i < nb - 1, dgn, 0.0)
            dvn = jnp.where(i < nb - 1, dvn, 0.0)
            gp0 = jnp.where(i > 0, gp_ref[:, ls], 0.0)
            vp0 = jnp.where(i > 0, vp_ref[:, ls], 0.0)

            def strip(tt, carry):
                dgn, dvn = carry[0], carry[1]
                aw, ab = list(carry[2:8]), list(carry[8:10])
                t = nt - 1 - tt
                nu = FFN_STRIP_BWD // SUBLANES
                rm = pl.multiple_of(jnp.maximum(t * FFN_STRIP_BWD - SUBLANES, 0), SUBLANES)
                gs = [jnp.where(t > 0, g_ref[pl.ds(rm, SUBLANES), ls], gp0)]
                vs = [jnp.where(t > 0, v_ref[pl.ds(rm, SUBLANES), ls], vp0)]
                das = []
                for u in range(nu):
                    rows = pl.ds(pl.multiple_of(t * FFN_STRIP_BWD + u * SUBLANES, SUBLANES), SUBLANES)
                    gs.append(g_ref[rows, ls])
                    vs.append(v_ref[rows, ls])
                    das.append(da_ref[rows, ls])
                dgs, dvs = [None] * nu + [dgn], [None] * nu + [dvn]
                for u in reversed(range(nu)):
                    dgs[u], dvs[u], gsh, vsh = unit(gs[u], gs[u + 1], vs[u], vs[u + 1], das[u])
                    for k in range(3):
                        aw[k] = aw[k] + dgs[u] * gsh[k]
                        aw[3 + k] = aw[3 + k] + dvs[u] * vsh[k]
                    ab[0] = ab[0] + dgs[u]
                    ab[1] = ab[1] + dvs[u]
                for p in range(nu // 2):
                    rows = pl.ds(pl.multiple_of(t * FFN_STRIP_BWD + p * 16, 16), 16)
                    dg_ref[rows, ls] = jnp.concatenate([d_in(dgs[2 * p], dgs[2 * p + 1], wg),
                                                        d_in(dgs[2 * p + 1], dgs[2 * p + 2], wg)], axis=0).astype(BF16)
                    dv_ref[rows, ls] = jnp.concatenate([d_in(dvs[2 * p], dvs[2 * p + 1], wv),
                                                        d_in(dvs[2 * p + 1], dvs[2 * p + 2], wv)], axis=0).astype(BF16)
                return (dgs[0], dvs[0], *aw, *ab)

            res = lax.fori_loop(0, nt, strip, (dgn, dvn) + (zero,) * 8)
            for k in range(3):
                acc_w[0, k, :, ls] += res[2 + k]
                acc_w[1, k, :, ls] += res[5 + k]
            acc_b[0, :, ls] += res[8]
            acc_b[1, :, ls] += res[9]

        @pl.when(i == nb - 1)
        def _():
            for t, (dw_ref, db_ref) in enumerate(((dwg_ref, dbg_ref), (dwv_ref, dbv_ref))):
                dw_ref[...] = jnp.zeros_like(dw_ref)
                for k in range(3):
                    dw_ref[k:k + 1, :] = jnp.sum(acc_w[t, k], axis=0, keepdims=True)
                db_ref[...] = jnp.sum(acc_b[t], axis=0, keepdims=True)

    wspec = lambda off: pl.BlockSpec((SUBLANES, tc), lambda j, i, off=off: (0, j + off))
    bspec = lambda off: pl.BlockSpec((1, tc), lambda j, i, off=off: (0, j + off))
    ospec = lambda off: pl.BlockSpec((tl, tc), lambda j, i, off=off: (i, j + off))
    dg, dv, dwg, dwv, dbg, dbv = _pc(
        body, name=name,
        out_shape=(_sds((n, D_FF), BF16), _sds((n, D_FF), BF16), _sds((SUBLANES, D_FF)), _sds((SUBLANES, D_FF)),
                   _sds((1, D_FF)), _sds((1, D_FF))),
        grid=(ncb, nb),
        in_specs=[spec(tl, cur, 0), spec(hb, nxt, 0),
                  spec(tl, cur, 0), spec(hb, prv, 0), spec(hb, nxt, 0),
                  spec(tl, cur, ncb), spec(hb, prv, ncb), spec(hb, nxt, ncb),
                  wspec(0), wspec(ncb), bspec(0), bspec(ncb)],
        out_specs=(ospec(0), ospec(0), wspec(0), wspec(0), bspec(0), bspec(0)),
        scratch_shapes=[pltpu.VMEM((2, 3, SUBLANES, tc), F32), pltpu.VMEM((2, SUBLANES, tc), F32)],
        compiler_params=_params(2))(da, da, up, up, up, up, up, up, conv_w, conv_w, conv_b, conv_b)
    return dg, dv, jnp.concatenate([dwg, dwv], axis=1), jnp.concatenate([dbg, dbv], axis=1)


def _position():
    return lax.axis_index("x"), lax.axis_index("y"), lax.axis_index("c")


def _all_gather(locals_, name):
    n = len(locals_)

    def body(*refs):
        x_refs, out_refs = refs[:n], refs[n:2 * n]
        send_sems, recv_sems, local_sems = refs[2 * n:]
        x, y, cc = _position()
        me, sibling = (x, y, cc), (x, y, 1 - cc)
        chips = [(1 - x, y), (x, 1 - y), (1 - x, 1 - y)]

        def slot(a, px, py, pc):
            return out_refs[a].at[4 * px + 2 * py + pc]

        def copy(k, a, block, to, own=False):
            return pltpu.make_async_remote_copy(
                src_ref=x_refs[a] if own else slot(a, *block), dst_ref=slot(a, *block),
                send_sem=send_sems.at[k, a], recv_sem=recv_sems.at[k, a], device_id=to, device_id_type=MESH)

        mine = [pltpu.make_async_copy(x_refs[a], slot(a, *me), local_sems.at[a]) for a in range(n)]
        first = [copy(1 + j, a, me, (*chip, cc), own=True) for j, chip in enumerate(chips) for a in range(n)]
        first += [copy(0, a, me, sibling, own=True) for a in range(n)]
        for cp in mine + first:
            cp.start()
        passed = []
        for j, chip in enumerate(chips):
            for a in range(n):
                copy(1 + j, a, (*chip, cc), me).wait_recv()
                cp = copy(4 + j, a, (*chip, cc), sibling)
                cp.start()
                passed.append(cp)
        for a in range(n):
            copy(0, a, sibling, me).wait_recv()
        for j, chip in enumerate(chips):
            for a in range(n):
                copy(4 + j, a, (*chip, 1 - cc), me).wait_recv()
        for cp in first + passed:
            cp.wait_send()
        for cp in mine:
            cp.wait()

    hbm = pl.BlockSpec(memory_space=pl.ANY)
    return _pc(body, name=name, out_shape=[_sds((N_DEV,) + a.shape, a.dtype) for a in locals_],
               in_specs=[hbm] * n, out_specs=[hbm] * n,
               scratch_shapes=[pltpu.SemaphoreType.DMA((7, n)), pltpu.SemaphoreType.DMA((7, n)),
                               pltpu.SemaphoreType.DMA((n,))])(*locals_)


def _peers():
    x, y, cc = _position()
    others = []
    for fx, fy, fc in ((0, 0, 1), (1, 0, 0), (0, 1, 0), (1, 1, 0), (1, 0, 1), (0, 1, 1), (1, 1, 1)):
        p = (1 - x if fx else x, 1 - y if fy else y, 1 - cc if fc else cc)
        others.append((p, 4 * p[0] + 2 * p[1] + p[2]))
    return 4 * x + 2 * y + cc, others


def _gather_start(locals_, after, name):
    n = len(locals_)
    me, _ = _peers()
    lands = [lax.dynamic_update_slice(lax.empty((N_DEV,) + a.shape, a.dtype), a[None], (me,) + (0,) * a.ndim)
             for a in locals_]

    def body(*refs):
        x_refs, land_refs = refs[:n], refs[n:2 * n]
        send_sems, recv_sems, token = refs[2 * n + 1], refs[2 * n + 2], refs[-1]
        me_idx, others = _peers()
        for k, (peer, _) in enumerate(others):
            for a in range(n):
                pltpu.make_async_remote_copy(
                    src_ref=x_refs[a], dst_ref=land_refs[a].at[me_idx], send_sem=send_sems.at[k * n + a],
                    recv_sem=recv_sems.at[k * n + a], device_id=peer, device_id_type=MESH).start()
        token[...] = jnp.zeros_like(token)

    hbm = pl.BlockSpec(memory_space=pltpu.HBM)
    sem = pl.BlockSpec(memory_space=pltpu.SEMAPHORE)
    out = _pc(body, name=name,
              out_shape=(pltpu.SemaphoreType.DMA((7 * n,)), pltpu.SemaphoreType.DMA((7 * n,)),
                         *[pltpu.HBM(a.shape, a.dtype) for a in locals_], *[pltpu.HBM(l.shape, l.dtype) for l in lands],
                         _sds((SUBLANES, LANES))),
              in_specs=[hbm] * (2 * n) + [pl.BlockSpec(memory_space=pl.ANY)],
              out_specs=(sem, sem, *([hbm] * (2 * n)), pl.BlockSpec(memory_space=pltpu.VMEM)),
              input_output_aliases={i: 2 + i for i in range(2 * n)},
              compiler_params=pltpu.CompilerParams(has_side_effects=pltpu.SideEffectType.DATAFLOW_SIDE_EFFECTING))(
        *[pltpu.with_memory_space_constraint(a, pltpu.HBM) for a in locals_],
        *[pltpu.with_memory_space_constraint(l, pltpu.HBM) for l in lands], after)
    return (out[0], out[1], list(out[2:2 + n]), list(out[2 + n:2 + 2 * n])), out[-1]


def _gather_wait(state, after, name):
    send_sems, recv_sems, x_thru, land_thru = state
    n = len(x_thru)

    def body(*refs):
        x_refs, land_refs = refs[:n], refs[n:2 * n]
        send_sems, recv_sems = refs[2 * n], refs[2 * n + 1]
        _, others = _peers()
        for k, (peer, peer_idx) in enumerate(others):
            for a in range(n):
                cp = pltpu.make_async_remote_copy(
                    src_ref=x_refs[a], dst_ref=land_refs[a].at[peer_idx], send_sem=send_sems.at[k * n + a],
                    recv_sem=recv_sems.at[k * n + a], device_id=peer, device_id_type=MESH)
                cp.wait_send()
                cp.wait_recv()

    hbm = pl.BlockSpec(memory_space=pltpu.HBM)
    sem = pl.BlockSpec(memory_space=pltpu.SEMAPHORE)
    out = _pc(body, name=name, out_shape=tuple(pltpu.HBM(a.shape, a.dtype) for a in x_thru + land_thru),
              in_specs=[hbm] * (2 * n) + [sem, sem, pl.BlockSpec(memory_space=pl.ANY)], out_specs=tuple([hbm] * (2 * n)),
              input_output_aliases={i: i for i in range(2 * n)},
              compiler_params=pltpu.CompilerParams(has_side_effects=pltpu.SideEffectType.DATAFLOW_SIDE_EFFECTING))(
        *x_thru, *land_thru, send_sems, recv_sems, after)
    return list(out[n:])


N_CHIPS = 4


def _sibling_swap(parts, name):
    n = len(parts)

    def body(*refs):
        g_refs, got_refs = refs[:n], refs[n:2 * n]
        send_sems, recv_sems = refs[2 * n:]
        x, y, cc = _position()
        swaps = []
        for q in range(N_CHIPS):
            for a in range(n):
                swaps.append(pltpu.make_async_remote_copy(
                    src_ref=g_refs[a].at[2 * q + 1 - cc], dst_ref=got_refs[a].at[q], send_sem=send_sems.at[q, a],
                    recv_sem=recv_sems.at[q, a], device_id=(x, y, 1 - cc), device_id_type=MESH))
        for cp in swaps:
            cp.start()
        for cp in swaps:
            cp.wait_recv()
        for cp in swaps:
            cp.wait_send()

    hbm = pl.BlockSpec(memory_space=pl.ANY)
    return _pc(body, name=name, out_shape=[_sds((N_CHIPS,) + a.shape[1:], a.dtype) for a in parts],
               in_specs=[hbm] * n, out_specs=[hbm] * n,
               scratch_shapes=[pltpu.SemaphoreType.DMA((N_CHIPS, n)), pltpu.SemaphoreType.DMA((N_CHIPS, n))])(*parts)


def _pair_add(part, got, core, name):
    q, a, b = got.shape
    ta = _block_rows(a, b)

    def body(core_ref, k_ref, g_ref, o_ref):
        o_ref[...] = (k_ref[...].astype(F32) + g_ref[...].astype(F32)).astype(BF16)

    spec = pl.BlockSpec((None, ta, b), lambda c, i, core_ref: (c, i, 0))
    own = pl.BlockSpec((None, None, ta, b), lambda c, i, core_ref: (c, core_ref[0], i, 0))
    grid_spec = pltpu.PrefetchScalarGridSpec(num_scalar_prefetch=1, grid=(q, a // ta), in_specs=[own, spec],
                                             out_specs=spec)
    return _pc(body, name=name, out_shape=_sds(got.shape, BF16), grid_spec=grid_spec, compiler_params=_params(2))(
        core, part.reshape((N_CHIPS, 2) + part.shape[1:]), got)


def _chip_exchange(sums, name):
    n = len(sums)

    def body(*refs):
        g_refs, out_refs = refs[:n], refs[n:2 * n]
        send_sems, recv_sems = refs[2 * n:]
        me_q, others = _chip_peers()
        sends, recvs = [], []
        for k, (peer, peer_q) in enumerate(others):
            for a in range(n):
                sends.append(pltpu.make_async_remote_copy(
                    src_ref=g_refs[a].at[peer_q], dst_ref=out_refs[a].at[me_q], send_sem=send_sems.at[k, a],
                    recv_sem=recv_sems.at[k, a], device_id=peer, device_id_type=MESH))
                recvs.append(pltpu.make_async_remote_copy(
                    src_ref=g_refs[a].at[me_q], dst_ref=out_refs[a].at[peer_q], send_sem=send_sems.at[k, a],
                    recv_sem=recv_sems.at[k, a], device_id=peer, device_id_type=MESH))
        for cp in sends:
            cp.start()
        for cp in recvs:
            cp.wait_recv()
        for cp in sends:
            cp.wait_send()

    hbm = pl.BlockSpec(memory_space=pl.ANY)
    return _pc(body, name=name, out_shape=[_sds(a.shape, a.dtype) for a in sums],
               in_specs=[hbm] * n, out_specs=[hbm] * n,
               scratch_shapes=[pltpu.SemaphoreType.DMA((3, n)), pltpu.SemaphoreType.DMA((3, n))])(*sums)


def _chip_peers():
    x, y, cc = _position()
    others = []
    for fx, fy in ((1, 0), (0, 1), (1, 1)):
        px, py = (1 - x if fx else x), (1 - y if fy else y)
        others.append(((px, py, cc), 2 * px + py))
    return 2 * x + y, others


def _chip_exchange_start(sums, name):
    n = len(sums)
    lands = [lax.empty(a.shape, a.dtype) for a in sums]

    def body(*refs):
        g_refs, land_refs = refs[:n], refs[n:2 * n]
        send_sems, recv_sems, token = refs[2 * n], refs[2 * n + 1], refs[-1]
        me_q, others = _chip_peers()
        for k, (peer, peer_q) in enumerate(others):
            for a in range(n):
                pltpu.make_async_remote_copy(
                    src_ref=g_refs[a].at[peer_q], dst_ref=land_refs[a].at[me_q], send_sem=send_sems.at[k * n + a],
                    recv_sem=recv_sems.at[k * n + a], device_id=peer, device_id_type=MESH).start()
        token[...] = jnp.zeros_like(token)

    hbm = pl.BlockSpec(memory_space=pltpu.HBM)
    sem = pl.BlockSpec(memory_space=pltpu.SEMAPHORE)
    out = _pc(body, name=name,
              out_shape=(pltpu.SemaphoreType.DMA((3 * n,)), pltpu.SemaphoreType.DMA((3 * n,)),
                         *[pltpu.HBM(a.shape, a.dtype) for a in sums], *[pltpu.HBM(a.shape, a.dtype) for a in sums],
                         _sds((SUBLANES, LANES))),
              in_specs=[hbm] * (2 * n), out_specs=(sem, sem, *([hbm] * (2 * n)), pl.BlockSpec(memory_space=pltpu.VMEM)),
              input_output_aliases={i: 2 + i for i in range(2 * n)},
              compiler_params=pltpu.CompilerParams(has_side_effects=pltpu.SideEffectType.DATAFLOW_SIDE_EFFECTING))(
        *[pltpu.with_memory_space_constraint(a, pltpu.HBM) for a in sums],
        *[pltpu.with_memory_space_constraint(l, pltpu.HBM) for l in lands])
    return (out[0], out[1], list(out[2:2 + n]), list(out[2 + n:2 + 2 * n])), out[-1]


def _chip_exchange_wait(state, after, name):
    send_sems, recv_sems, g_thru, land_thru = state
    n = len(g_thru)

    def body(*refs):
        g_refs, land_refs = refs[:n], refs[n:2 * n]
        send_sems, recv_sems = refs[2 * n], refs[2 * n + 1]
        me_q, others = _chip_peers()
        for k, (peer, peer_q) in enumerate(others):
            for a in range(n):
                cp = pltpu.make_async_remote_copy(
                    src_ref=g_refs[a].at[me_q], dst_ref=land_refs[a].at[peer_q], send_sem=send_sems.at[k * n + a],
                    recv_sem=recv_sems.at[k * n + a], device_id=peer, device_id_type=MESH)
                cp.wait_send()
                cp.wait_recv()

    hbm = pl.BlockSpec(memory_space=pltpu.HBM)
    sem = pl.BlockSpec(memory_space=pltpu.SEMAPHORE)
    out = _pc(body, name=name, out_shape=tuple(pltpu.HBM(a.shape, a.dtype) for a in g_thru + land_thru),
              in_specs=[hbm] * (2 * n) + [sem, sem, pl.BlockSpec(memory_space=pl.ANY)], out_specs=tuple([hbm] * (2 * n)),
              input_output_aliases={i: i for i in range(2 * n)},
              compiler_params=pltpu.CompilerParams(has_side_effects=pltpu.SideEffectType.DATAFLOW_SIDE_EFFECTING))(
        *g_thru, *land_thru, send_sems, recv_sems, after)
    return list(out[:n]), list(out[n:])


def _block_rows(a, b):
    ta = a
    while ta * b > 256 * 1024 and ta % 32 == 0:
        ta //= 2
    return ta


def _reduce_adamw(parts, w, m, v, name):
    n_parts, s, a, b = parts.shape
    ta = _block_rows(a, b)

    def body(p_ref, w_ref, m_ref, v_ref, g_out, d_out, m_out, v_out):
        g = p_ref[0].astype(F32)
        for j in range(1, n_parts):
            g = g + p_ref[j].astype(F32)
        delta, m_new, v_new = _adamw(g, w_ref[...], m_ref[...], v_ref[...])
        g_out[...] = g
        d_out[...] = delta
        m_out[...] = m_new
        v_out[...] = v_new

    spec = pl.BlockSpec((None, ta, b), lambda l, i: (l, i, 0))
    return _pc(body, name=name, out_shape=(_sds((s, a, b)),) * 4, grid=(s, a // ta),
               in_specs=[pl.BlockSpec((n_parts, None, ta, b), lambda l, i: (0, l, i, 0)), spec, spec, spec],
               out_specs=(spec,) * 4, compiler_params=_params(2))(parts, w, m, v)


def _adamw(g, w, m, v):
    c1 = 1.0 - ADAM_B1 ** ADAM_STEP
    c2 = 1.0 - ADAM_B2 ** ADAM_STEP
    m_new = ADAM_B1 * m + (1.0 - ADAM_B1) * g
    v_new = ADAM_B2 * v + (1.0 - ADAM_B2) * (g * g)
    delta = -ADAM_LR * ((m_new / c1) / (jnp.sqrt(v_new / c2) + ADAM_EPS) + ADAM_WD * w)
    return delta, m_new, v_new


def _chip_reduce_adamw(own, recv, w, m, v, chip, name):
    s, a, b = w.shape
    ta = _block_rows(a, b)

    def body(chip_ref, *refs):
        p_refs, (w_ref, m_ref, v_ref), (g_out, d_out, m_out, v_out) = refs[:4 * s], refs[4 * s:4 * s + 3], refs[4 * s + 3:]
        layer = pl.program_id(0)
        g = None
        for l in range(s):
            gl = p_refs[4 * l][...].astype(F32)
            for j in range(1, N_CHIPS):
                gl = gl + p_refs[4 * l + j][...].astype(F32)
            g = gl if g is None else jnp.where(layer == l, gl, g)
        delta, m_new, v_new = _adamw(g, w_ref[...], m_ref[...], v_ref[...])
        g_out[...] = g
        d_out[...] = delta
        m_out[...] = m_new
        v_out[...] = v_new

    def part_spec(l, j):
        return pl.BlockSpec((None, ta, b), lambda layer, i, chip_ref, l=l, j=j: (
            (chip_ref[0] + j) % N_CHIPS, jnp.where(layer == l, i, 0), 0))

    spec = pl.BlockSpec((None, ta, b), lambda layer, i, chip_ref: (layer, i, 0))
    in_specs, args = [], []
    for l in range(s):
        for j in range(N_CHIPS):
            in_specs.append(part_spec(l, j))
            args.append(own[l] if j == 0 else recv[l])
    grid_spec = pltpu.PrefetchScalarGridSpec(num_scalar_prefetch=1, grid=(s, a // ta), in_specs=in_specs + [spec] * 3,
                                             out_specs=(spec,) * 4)
    return _pc(body, name=name, out_shape=(_sds((s, a, b)),) * 4, grid_spec=grid_spec, compiler_params=_params(2))(
        chip, *args, w, m, v)


MATRICES = (("ada_mix_w", 2), ("w_in", 2), ("w_ssd_out", 1), ("w_conf_out", 2), ("w_sc_out", 2), ("w_o", 1),
            ("ada_ffn_w", 2), ("w_up", 2), ("w_down", 1))
MIXER_MATRICES = ("ada_mix_w", "w_in", "w_ssd_out", "w_conf_out", "w_sc_out", "w_o")
FFN_MATRICES = ("ada_ffn_w", "w_up", "w_down")
CONV_WEIGHTS = (("ssd_conv_w", 2), ("conf_conv_w", 2), ("sc_conv_w", 2), ("ffn_conv_w", 2))
SHARDED = MATRICES + CONV_WEIGHTS
REPLICATED = ("ada_mix_b", "norm_mix_g", "b_gate", "ssd_conv_b", "ssd_dt_bias", "ssd_a_log", "ssd_d", "ssd_norm_g",
              "conf_conv_b", "conf_ln_g", "conf_ln_b", "ada_ffn_b", "norm_ffn_g", "ffn_conv_b", "final_norm_g")
WEIGHT_NAMES = ("ada_mix_w", "ada_mix_b", "norm_mix_g", "w_in", "b_gate", "ssd_conv_w", "ssd_conv_b", "ssd_dt_bias",
                "ssd_a_log", "ssd_d", "ssd_norm_g", "w_ssd_out", "conf_conv_w", "conf_conv_b", "conf_ln_g",
                "conf_ln_b", "w_conf_out", "sc_conv_w", "w_sc_out", "w_o", "ada_ffn_w", "ada_ffn_b", "norm_ffn_g",
                "w_up", "ffn_conv_w", "ffn_conv_b", "w_down", "final_norm_g")


def _pack_flat(arrays, cols, row_multiple, dtype):
    flat = jnp.concatenate([a.reshape(-1).astype(dtype) for a in arrays])
    rows = -(-flat.shape[0] // cols)
    rows = -(-rows // row_multiple) * row_multiple
    return jnp.pad(flat, (0, rows * cols - flat.shape[0])).reshape(rows, cols)


def _unpack_flat(flat2d, shapes):
    flat = flat2d.reshape(-1)
    out, off = [], 0
    for s in shapes:
        n = 1
        for d in s:
            n *= d
        out.append(flat[off:off + n].reshape(s))
        off += n
    return out


def _cols(g, lo, hi):
    b = g.shape[-1]
    pieces = []
    for k in range(N_DEV):
        a, e = max(lo, k * b), min(hi, (k + 1) * b)
        if a < e:
            pieces.append(g[k, :, a - k * b:e - k * b])
    return pieces[0] if len(pieces) == 1 else jnp.concatenate(pieces, axis=1)


def _rows(g):
    return g.reshape(N_DEV * g.shape[1], g.shape[2])


def _col_shards(segs, b):
    shards = []
    for k in range(N_DEV):
        lo, hi = k * b, (k + 1) * b
        pieces, off = [], 0
        for seg in segs:
            n = seg.shape[1]
            a, e = max(lo, off), min(hi, off + n)
            if a < e:
                pieces.append(seg[:, a - off:e - off])
            off += n
        shards.append(pieces[0] if len(pieces) == 1 else jnp.concatenate(pieces, axis=1))
    return jnp.stack(shards)


def _row_shards(full):
    return full.reshape(N_DEV, full.shape[0] // N_DEV, full.shape[1])


def _pad_rows(a, rows):
    return jnp.pad(a, ((0, rows - a.shape[0]), (0, 0)))


def _pad_lanes(a):
    return jnp.pad(a, ((0, 0), (0, LANES - a.shape[1])))


def _whole(g):
    return _cols(g, 0, N_DEV * g.shape[-1])


def _mixer_weights(full, i):
    row = lambda name: full[name][i].reshape(1, -1)
    conv = lambda name: _whole(full[name][:, i])
    w_in = full["w_in", i]
    return {
        "ada_mix_w": _whole(full["ada_mix_w", i]), "ada_mix_b": row("ada_mix_b"), "norm_mix_g": row("norm_mix_g"),
        "w_z": _cols(w_in, 0, OFF_Z), "w_xbc": _cols(w_in, OFF_Z, OFF_XBC),
        "w_dt": _pad_lanes(_cols(w_in, OFF_XBC, OFF_DT)), "w_conf": _cols(w_in, OFF_DT, OFF_CONF),
        "w_sc": _cols(w_in, OFF_CONF, OFF_SC), "w_gates": _cols(w_in, OFF_SC, N_IN),
        "b_gate": row("b_gate"),
        "ssd_conv_w": _pad_rows(conv("ssd_conv_w"), SUBLANES), "ssd_conv_b": row("ssd_conv_b"),
        "dt_bias": _pad_lanes(row("ssd_dt_bias")), "a_log": _pad_lanes(row("ssd_a_log")),
        "ssd_d": _pad_lanes(row("ssd_d")), "ssd_norm_g": row("ssd_norm_g"), "w_ssd_out": _rows(full["w_ssd_out", i]),
        "conf_conv_w": _pad_rows(conv("conf_conv_w"), CONF_HALO), "conf_conv_b": row("conf_conv_b"),
        "conf_ln_g": row("conf_ln_g"), "conf_ln_b": row("conf_ln_b"), "w_conf_out": _whole(full["w_conf_out", i]),
        "sc_conv_w": _pad_rows(conv("sc_conv_w"), SUBLANES), "w_sc_out": _whole(full["w_sc_out", i]),
        "w_o": _rows(full["w_o", i]),
    }


def _ffn_weights(full, i):
    row = lambda name: full[name][i].reshape(1, -1)
    return {
        "ada_ffn_w": _whole(full["ada_ffn_w", i]), "ada_ffn_b": row("ada_ffn_b"), "norm_ffn_g": row("norm_ffn_g"),
        "w_up": _whole(full["w_up", i]), "ffn_conv_w": _pad_rows(_whole(full["ffn_conv_w"][:, i]), SUBLANES),
        "ffn_conv_b": row("ffn_conv_b"), "w_down": _rows(full["w_down", i]),
    }


def _adaln(sc8, w, b, name):
    mod = _matmul(sc8, w, "nn", F32, name)[0:1, :] + b
    return mod[:, :D_MODEL], mod[:, D_MODEL:2 * D_MODEL], mod[:, 2 * D_MODEL:]


def _mixer_fwd(i, x, prev, sc8, wl):
    t = f"l{i}_"
    s = {}
    shift, scale, gate = _adaln(sc8, wl["ada_mix_w"], wl["ada_mix_b"], t + "ada_mix")
    if prev is None:
        s["x_in"] = x
        s["h"] = _prenorm_first(x, wl["norm_mix_g"], scale, shift, t + "norm_mix")
    else:
        s["x_in"], s["h"] = _prenorm_res(x, prev[0], prev[1], wl["norm_mix_g"], scale, shift, t + "norm_mix")
    s["scale_mix"], s["gate_mix"] = scale, gate
    h = s["h"]
    s["z"] = _matmul(h, wl["w_z"], "nn", F32, t + "in_z")
    s["xbc"] = _matmul(h, wl["w_xbc"], "nn", F32, t + "in_xbc")
    s["dt_raw"] = _matmul(h, wl["w_dt"], "nn", F32, t + "in_dt")
    s["conf"] = _matmul(h, wl["w_conf"], "nn", F32, t + "in_conf")
    s["sc"] = _matmul(h, wl["w_sc"], "nn", F32, t + "in_sc")
    s["gates"] = _matmul(h, wl["w_gates"], "nn", F32, t + "in_gates")
    s["pre"], s["dt"] = _ssd_pre(s["xbc"], s["dt_raw"], wl["ssd_conv_w"], wl["ssd_conv_b"], wl["dt_bias"],
                                 t + "ssd_pre")
    s["y"], s["hprev"] = _ssd_scan(s["pre"], s["dt"], wl["a_log"], wl["ssd_d"], t + "ssd_scan")
    s["ya_in"] = _ssd_post(s["y"], s["z"], wl["ssd_norm_g"], t + "ssd_post")
    s["yb_in"], s["uc"] = _conf_fwd(s["conf"], wl["conf_conv_w"], wl["conf_conv_b"], wl["conf_ln_g"],
                                    wl["conf_ln_b"], t + "conf")
    s["yc_in"] = _sc_fwd(s["sc"], wl["sc_conv_w"], t + "sconv")
    s["ya"] = _matmul(s["ya_in"], wl["w_ssd_out"], "nn", F32, t + "ssd_out")
    s["yb"] = _matmul(s["yb_in"], wl["w_conf_out"], "nn", F32, t + "conf_out")
    s["yc"] = _matmul(s["yc_in"], wl["w_sc_out"], "nn", F32, t + "sc_out")
    s["merged"] = _merge_fwd(s["gates"], s["ya"], s["yb"], s["yc"], wl["b_gate"], t + "merge")
    s["mix"] = _matmul(s["merged"], wl["w_o"], "nn", F32, t + "w_o")
    return s


def _ffn_fwd(i, s, sc8, wl):
    t = f"l{i}_"
    shift2, scale2, gate2 = _adaln(sc8, wl["ada_ffn_w"], wl["ada_ffn_b"], t + "ada_ffn")
    s["x_mid"], s["h2"] = _prenorm_res(s["x_in"], s["mix"], s["gate_mix"], wl["norm_ffn_g"], scale2, shift2,
                                       t + "norm_ffn")
    s["scale_ffn"], s["gate_ffn"] = scale2, gate2
    s["up"] = _matmul(s["h2"], wl["w_up"], "nn", F32, t + "w_up")
    s["a"] = _ffn_mid(s["up"], wl["ffn_conv_w"], wl["ffn_conv_b"], t + "ffn_mid")
    s["out"] = _matmul(s["a"], wl["w_down"], "nn", F32, t + "w_down")
    return s


def _layer_bwd(i, s, wl, sc8, dys_ffn, dx_after, dgate_ffn, prev, emit=None):
    t = f"l{i}_b_"
    g = {}
    da = _matmul(dys_ffn, wl["w_down"], "nt", F32, t + "d_a")
    g["w_down"] = _matmul(s["a"], dys_ffn, "tn", BF16, t + "dw_down")
    dug, duv, dfw, g["ffn_conv_b"] = _ffn_mid_bwd(da, s["up"], wl["ffn_conv_w"], wl["ffn_conv_b"], t + "ffn_mid")
    g["ffn_conv_w"] = dfw[:3]
    dh2 = _matmul_sum_nt([dug, duv], [wl["w_up"][:, :D_FF], wl["w_up"][:, D_FF:]], t + "d_h2")
    g["w_up"] = [_matmul(s["h2"], dug, "tn", BF16, t + "dw_up_g"), _matmul(s["h2"], duv, "tn", BF16, t + "dw_up_v")]
    dx_mid, dshift2, dscale2, g["norm_ffn_g"], dys_mix, dgate_mix = _norm_bwd(
        dh2, s["x_mid"], dx_after, wl["norm_ffn_g"], s["scale_ffn"], t + "norm_ffn", s["mix"], s["gate_mix"])
    dmod_ffn = jnp.concatenate([dshift2, dscale2, dgate_ffn], axis=1)
    g["ada_ffn_b"] = dmod_ffn
    g["ada_ffn_w"] = [_matmul(sc8, _pad_rows(dmod_ffn, SUBLANES), "tn", BF16, t + "dw_ada_ffn")]
    token = emit(i, "ffn", g) if emit is not None else None
    if token is not None:
        wl = {**wl, "b_gate": wl["b_gate"] + token}
    dmerged = _matmul(dys_mix, wl["w_o"], "nt", F32, t + "d_merged")
    g["w_o"] = _matmul(s["merged"], dys_mix, "tn", BF16, t + "dw_o")
    dya, dyb, dyc, dgates, g["b_gate"] = _merge_bwd(dmerged, s["gates"], s["ya"], s["yb"], s["yc"], wl["b_gate"],
                                                    t + "merge")
    dya_in = _matmul(dya, wl["w_ssd_out"], "nt", F32, t + "d_ya_in")
    g["w_ssd_out"] = _matmul(s["ya_in"], dya, "tn", BF16, t + "dw_ssd_out")
    dyb_in = _matmul(dyb, wl["w_conf_out"], "nt", F32, t + "d_yb_in")
    g["w_conf_out"] = [_matmul(s["yb_in"], dyb, "tn", BF16, t + "dw_conf_out")]
    dyc_in = _matmul(dyc, wl["w_sc_out"], "nt", F32, t + "d_yc_in")
    g["w_sc_out"] = [_matmul(s["yc_in"], dyc, "tn", BF16, t + "dw_sc_out")]
    dy, dz, g["ssd_norm_g"] = _ssd_post_bwd(dya_in, s["y"], s["z"], wl["ssd_norm_g"], t + "ssd_post")
    dpre, ddt, da_log, dd = _ssd_scan_bwd(s["pre"], s["dt"], s["hprev"], dy, wl["a_log"], wl["ssd_d"],
                                          t + "ssd_scan")
    g["ssd_a_log"], g["ssd_d"] = da_log[:, :SSD_HEADS], dd[:, :SSD_HEADS]
    dxbc, ddt_raw, dcw, g["ssd_conv_b"], ddtb = _ssd_pre_bwd(dpre, s["xbc"], ddt, s["dt_raw"], wl["ssd_conv_w"],
                                                             wl["dt_bias"], t + "ssd_pre")
    g["ssd_conv_w"], g["ssd_dt_bias"] = dcw[:4], ddtb[:, :SSD_HEADS]
    dconf, dccw, g["conf_conv_b"], g["conf_ln_g"], g["conf_ln_b"] = _conf_bwd(
        dyb_in, s["uc"], s["conf"], wl["conf_conv_w"], wl["conf_ln_g"], wl["conf_ln_b"], t + "conf")
    g["conf_conv_w"] = dccw[:CONF_KERNEL]
    dsc, dscw = _sc_bwd(dyc_in, s["sc"], wl["sc_conv_w"], t + "sconv")
    g["sc_conv_w"] = dscw[:3]
    segs = (("z", dz, "w_z"), ("xbc", dxbc, "w_xbc"), ("dt", ddt_raw, "w_dt"), ("conf", dconf, "w_conf"),
            ("sc", dsc, "w_sc"), ("gates", dgates, "w_gates"))
    dw_segs = []
    for nm, dseg, wname in segs:
        dw = _matmul(s["h"], dseg, "tn", BF16, t + "dw_in_" + nm)
        dw_segs.append(dw[:, :SSD_HEADS] if nm == "dt" else dw)
    g["w_in"] = dw_segs
    token = emit(i, "mixer_early", g) if emit is not None else None
    if token is not None:
        wl = {**wl, "w_dt": wl["w_dt"] + token.astype(BF16)}
    dh = _matmul_sum_nt([dseg for _, dseg, _ in segs], [wl[wname] for _, _, wname in segs], t + "d_h")
    if prev is None:
        dx_in, dshift, dscale, g["norm_mix_g"] = _norm_bwd(dh, s["x_in"], dx_mid, wl["norm_mix_g"], s["scale_mix"],
                                                          t + "norm_mix")
        back = None
    else:
        dx_in, dshift, dscale, g["norm_mix_g"], dys_prev, dgate_prev = _norm_bwd(
            dh, s["x_in"], dx_mid, wl["norm_mix_g"], s["scale_mix"], t + "norm_mix", prev[0], prev[1])
        back = (dys_prev, dgate_prev)
    dmod_mix = jnp.concatenate([dshift, dscale, dgate_mix], axis=1)
    g["ada_mix_b"] = dmod_mix
    g["ada_mix_w"] = [_matmul(sc8, _pad_rows(dmod_mix, SUBLANES), "tn", BF16, t + "dw_ada_mix")]
    return g, dx_in, back


def _device_step(x, c, target, full, fetch=None, emit=None):
    fetch = fetch or {}
    full = dict(full)
    sc8 = _pad_rows(c * (1.0 / (1.0 + jnp.exp(-c))), SUBLANES)
    wls, saved, prev, xcur = [], [], None, x
    for i in range(DEPTH):
        if (i, "mixer") in fetch:
            full.update(fetch[i, "mixer"](prev[0]))
        wl = _mixer_weights(full, i)
        s = _mixer_fwd(i, xcur, prev, sc8, wl)
        if (i, "ffn") in fetch:
            full.update(fetch[i, "ffn"](s["mix"]))
        wf = _ffn_weights(full, i)
        _ffn_fwd(i, s, sc8, wf)
        wls.append({**wl, **wf})
        saved.append(s)
        xcur, prev = s["x_mid"], (s["out"], s["gate_ffn"])
    gf = full["final_norm_g"].reshape(1, -1)
    last = saved[-1]
    loss, dx, dys, dgate, dgf = _final_loss(last["x_mid"], last["out"], last["gate_ffn"], gf, target, "final_loss")
    grads = [None] * DEPTH
    for i in reversed(range(DEPTH)):
        prev = None if i == 0 else (saved[i - 1]["out"], saved[i - 1]["gate_ffn"])
        grads[i], dx, back = _layer_bwd(i, saved[i], wls[i], sc8, dys, dx, dgate, prev, emit)
        token = emit(i, "mixer", grads[i]) if emit is not None else None
        if token is not None:
            wls[i - 1] = {**wls[i - 1], "ffn_conv_b": wls[i - 1]["ffn_conv_b"] + token}
        if back is not None:
            dys, dgate = back
    return loss[0, 0], dx, grads, dgf


def _step(x, c, target, weights, moments_m, moments_v):
    sharded_names = [n for n, _ in SHARDED]
    conv_names = [n for n, _ in CONV_WEIGHTS]
    shard = lambda key: weights[key[0]][key[1]].astype(BF16)
    first = [(n, 0) for n in MIXER_MATRICES]
    later = {(0, "ffn"): [(n, 0) for n in FFN_MATRICES], (1, "mixer"): [(n, 1) for n in MIXER_MATRICES + FFN_MATRICES]}
    gathered = _all_gather([shard(k) for k in first] + [weights[n] for n in conv_names], "gather_first")
    full = {n: weights[n] for n in REPLICATED}
    full.update(zip(first + conv_names, gathered))
    fetch, after = {}, gathered[0]
    for stage, keys in later.items():
        state, after = _gather_start([shard(k) for k in keys], after, f"gather_l{stage[0]}_{stage[1]}_start")
        fetch[stage] = functools.partial(
            lambda act, state, keys, nm: dict(zip(keys, _gather_wait(state, act, nm))),
            state=state, keys=keys, nm=f"gather_l{stage[0]}_{stage[1]}_wait")
    axis_of = dict(SHARDED)
    core = lax.axis_index("c").astype(jnp.int32).reshape(1)
    chip = (2 * lax.axis_index("x") + lax.axis_index("y")).astype(jnp.int32).reshape(1)
    ffn_names = list(FFN_MATRICES) + ["ffn_conv_w"]
    early_names = [n for n in sharded_names if n not in ffn_names and n != "ada_mix_w"]
    sums, received, pending = {}, {}, []

    def send(i, names, grads_i, last, tag):
        keys = [(n, i) for n in names]
        parts = []
        for n in names:
            gw = grads_i[n]
            part = _row_shards(gw) if axis_of[n] == 1 else _col_shards(gw if isinstance(gw, list) else [gw],
                                                                       weights[n].shape[-1])
            parts.append(part.astype(BF16))
        got = _sibling_swap(parts, "swap_grads_" + tag)
        pair = [_pair_add(p, g, core, f"pair_add_{n}_{i}") for n, p, g in zip(names, parts, got)]
        if last:
            sums.update(zip(keys, pair))
            received.update(zip(keys, _chip_exchange(pair, "exchange_grads_" + tag)))
            return None
        state, token = _chip_exchange_start(pair, "exchange_grads_" + tag + "_start")
        pending.append((keys, state, tag))
        return token[0:1, 0:1]

    def emit(i, kind, grads_i):
        if i == 0 and kind == "ffn":
            return send(0, ffn_names, grads_i, False, "l0_ffn")
        if i == 0 and kind == "mixer_early":
            return send(0, early_names, grads_i, False, "l0_mixer")
        if i == 0 and kind == "mixer":
            return send(0, ["ada_mix_w"], grads_i, True, "l0_ada")
        if kind == "mixer":
            return send(i, sharded_names, grads_i, False, f"l{i}_all")
        return None

    loss, grad_x, grads, dgf = _device_step(x[0], c + after[0:1, 0:1], target[0], full, fetch, emit)
    for keys, state, tag in pending:
        own, got = _chip_exchange_wait(state, grad_x, "exchange_grads_" + tag + "_wait")
        sums.update(zip(keys, own))
        received.update(zip(keys, got))
    big = {n: _chip_reduce_adamw([sums[n, i] for i in range(DEPTH)], [received[n, i] for i in range(DEPTH)],
                                 weights[n], moments_m[n], moments_v[n], chip, "adamw_" + n)
           for n in sharded_names}
    rep_grads = [dgf if n == "final_norm_g" else jnp.stack([grads[i][n].reshape(-1) for i in range(DEPTH)])
                 for n in REPLICATED]
    small_parts, = _all_gather([_pack_flat(rep_grads, LANES, SUBLANES, F32)], "gather_small_grads")
    pack_s = lambda d: _pack_flat([d[n] for n in REPLICATED], LANES, SUBLANES, F32)[None]
    small = _reduce_adamw(small_parts[:, None], pack_s(weights), pack_s(moments_m), pack_s(moments_v),
                          "adamw_replicated")
    small = [_unpack_flat(b, [weights[n].shape for n in REPLICATED]) for b in small]
    results = []
    for kind in range(4):
        by_name = {n: big[n][kind] for n in sharded_names}
        by_name.update(zip(REPLICATED, small[kind]))
        results.append([by_name[n] for n in WEIGHT_NAMES])
    loss = lax.psum(loss, ("x", "y", "c"))
    return (loss, grad_x[None], *results[0], *results[1], *results[2], *results[3])


def kernel(x, c, ada_mix_w, ada_mix_b, norm_mix_g, w_in, b_gate, ssd_conv_w, ssd_conv_b, ssd_dt_bias, ssd_a_log, ssd_d, ssd_norm_g, w_ssd_out, conf_conv_w, conf_conv_b, conf_ln_g, conf_ln_b, w_conf_out, sc_conv_w, w_sc_out, w_o, ada_ffn_w, ada_ffn_b, norm_ffn_g, w_up, ffn_conv_w, ffn_conv_b, w_down, final_norm_g, loss_target, m_ada_mix_w, m_ada_mix_b, m_norm_mix_g, m_w_in, m_b_gate, m_ssd_conv_w, m_ssd_conv_b, m_ssd_dt_bias, m_ssd_a_log, m_ssd_d, m_ssd_norm_g, m_w_ssd_out, m_conf_conv_w, m_conf_conv_b, m_conf_ln_g, m_conf_ln_b, m_w_conf_out, m_sc_conv_w, m_w_sc_out, m_w_o, m_ada_ffn_w, m_ada_ffn_b, m_norm_ffn_g, m_w_up, m_ffn_conv_w, m_ffn_conv_b, m_w_down, m_final_norm_g, v_ada_mix_w, v_ada_mix_b, v_norm_mix_g, v_w_in, v_b_gate, v_ssd_conv_w, v_ssd_conv_b, v_ssd_dt_bias, v_ssd_a_log, v_ssd_d, v_ssd_norm_g, v_w_ssd_out, v_conf_conv_w, v_conf_conv_b, v_conf_ln_g, v_conf_ln_b, v_w_conf_out, v_sc_conv_w, v_w_sc_out, v_w_o, v_ada_ffn_w, v_ada_ffn_b, v_norm_ffn_g, v_w_up, v_ffn_conv_w, v_ffn_conv_b, v_w_down, v_final_norm_g):
    given = dict(locals())
    weights = {n: given[n] for n in WEIGHT_NAMES}
    moments_m = {n: given["m_" + n] for n in WEIGHT_NAMES}
    moments_v = {n: given["v_" + n] for n in WEIGHT_NAMES}
    return _step(x, c, loss_target, weights, moments_m, moments_v)
```

```python
import functools

import jax
import jax.numpy as jnp
from jax import lax
from jax.experimental import pallas as pl
from jax.experimental.pallas import tpu as pltpu

F32 = jnp.float32
BF16 = jnp.bfloat16
MESH = pl.DeviceIdType.MESH

N_DEV = 8
DEPTH = 2
D_MODEL = 1024
SSD_HEADS = 16
SSD_HEAD_DIM = 64
SSD_INNER = 1024
SSD_STATE = 64
SSD_CHUNK = 128
SSD_XBC = 1280
CONF_WIDTH = 512
CONF_KERNEL = 31
SC_WIDTH = 512
D_FF = 2816
EPS = 1e-6
OFF_Z, OFF_XBC, OFF_DT, OFF_CONF, OFF_SC, N_IN = 1024, 2304, 2320, 3344, 4880, 7952

ADAM_LR, ADAM_B1, ADAM_B2, ADAM_EPS, ADAM_WD, ADAM_STEP = 0.001, 0.9, 0.999, 1e-08, 0.01, 10

LANES = 128
SUBLANES = 8
VMEM_LIMIT = 56 * 1024 * 1024
ROW_TILE = 256
NORM_ROW_TILE = 512

NN = (((1,), (0,)), ((), ()))
NT = (((1,), (1,)), ((), ()))
TN = (((0,), (0,)), ((), ()))


def _params(n_axes):
    return pltpu.CompilerParams(dimension_semantics=("arbitrary",) * n_axes, vmem_limit_bytes=VMEM_LIMIT)


def _pc(body, **kw):
    return pl.pallas_call(body, **kw)


def _dot(a, b, dn=NN, precision=None):
    return lax.dot_general(a, b, dn, precision=precision, preferred_element_type=F32)


def _split3(x):
    hi = x.astype(BF16)
    r1 = x - hi.astype(F32)
    mid = r1.astype(BF16)
    return hi, mid, (r1 - mid.astype(F32)).astype(BF16)


def _dot_sel(x, sel):
    hi, mid, lo = _split3(x)
    return _dot(hi, sel) + _dot(mid, sel) + _dot(lo, sel)


def _sel_dot(sel, x):
    hi, mid, lo = _split3(x)
    return _dot(sel, hi) + _dot(sel, mid) + _dot(sel, lo)


def _sig(x):
    return 1.0 / (1.0 + jnp.exp(-x))


def _fold(v):
    r, c = v.shape
    return v.reshape(r // SUBLANES, SUBLANES, c).sum(axis=0)


def _tile(n_rows, target=ROW_TILE):
    return min(target, n_rows // 2)


STRIP_UNITS = 8
STRIP_ROW_TILE = 512


def _strip_units_per_trip(tl):
    return min(STRIP_UNITS, tl // SUBLANES)


def _bcast_row(ref, k, ls):
    return jnp.broadcast_to(ref[k:k + 1, ls], (SUBLANES, LANES))


def _unit_rows(t, u, nu):
    return pl.ds(pl.multiple_of((t * nu + u) * SUBLANES, SUBLANES), SUBLANES)


def _pair_rows(t, p, nu):
    return pl.ds(pl.multiple_of((t * nu + 2 * p) * SUBLANES, 2 * SUBLANES), 2 * SUBLANES)


def _strip_units(ref, ls, t, nu, nt, halo_prev=None, halo_next=None):
    units = [ref[_unit_rows(t, u, nu), ls] for u in range(nu)]
    if halo_prev is not None:
        before = pl.ds(pl.multiple_of(jnp.maximum(t * nu - 1, 0) * SUBLANES, SUBLANES), SUBLANES)
        units.insert(0, jnp.where(t > 0, ref[before, ls], halo_prev))
    if halo_next is not None:
        after = pl.ds(pl.multiple_of(jnp.minimum((t + 1) * nu, nt * nu - 1) * SUBLANES, SUBLANES), SUBLANES)
        units.append(jnp.where(t < nt - 1, ref[after, ls], halo_next))
    return units


def _row(tl, c, col=0):
    return pl.BlockSpec((tl, c), lambda i, col=col: (i, col))


def _prev(tl, hb, c, col=0):
    r = tl // hb
    return pl.BlockSpec((hb, c), lambda i, col=col: (jnp.maximum(i * r - 1, 0), col))


def _next(tl, hb, c, n_rows, col=0):
    r = tl // hb
    last = n_rows // hb - 1
    return pl.BlockSpec((hb, c), lambda i, col=col: (jnp.minimum((i + 1) * r, last), col))


def _const(shape):
    return pl.BlockSpec(shape, lambda i: (0,) * len(shape))


def _sds(shape, dtype=F32):
    return jax.ShapeDtypeStruct(shape, dtype)


MM_TILE = 1536
MM_TILE_ROWS = 2048
MM_FULL_K = 3072
MM_K_TILE = 1024


def _pick(dim, target):
    if dim <= target:
        return dim
    best = None
    for t in range(LANES, target + 1, LANES):
        if dim % t == 0:
            best = t
    assert best is not None, (dim, target)
    return best


def _matmul(a, b, mode, out_dtype, name):
    if mode == "nn":
        (m, k), (k2, n) = a.shape, b.shape
    elif mode == "nt":
        (m, k), (n, k2) = a.shape, b.shape
    else:
        (k, m), (k2, n) = a.shape, b.shape
    assert k == k2, (a.shape, b.shape, mode)
    tn = _pick(n, MM_TILE)
    tk = k if k <= MM_FULL_K else _pick(k, MM_K_TILE)
    tm = _pick(m, MM_TILE_ROWS if tn <= 1024 and tk <= 1024 else 1024)
    nk = k // tk
    dn = {"nn": NN, "nt": NT, "tn": TN}[mode]

    def body_one(a_ref, b_ref, o_ref):
        o_ref[...] = _dot(a_ref[...].astype(BF16), b_ref[...].astype(BF16), dn).astype(out_dtype)

    def body_acc(a_ref, b_ref, o_ref, acc):
        kk = pl.program_id(2)

        @pl.when(kk == 0)
        def _():
            acc[...] = jnp.zeros_like(acc)

        acc[...] += _dot(a_ref[...].astype(BF16), b_ref[...].astype(BF16), dn)

        @pl.when(kk == nk - 1)
        def _():
            o_ref[...] = acc[...].astype(out_dtype)

    a_spec = {"nn": pl.BlockSpec((tm, tk), lambda i, j, kk: (i, kk)),
              "nt": pl.BlockSpec((tm, tk), lambda i, j, kk: (i, kk)),
              "tn": pl.BlockSpec((tk, tm), lambda i, j, kk: (kk, i))}[mode]
    b_spec = {"nn": pl.BlockSpec((tk, tn), lambda i, j, kk: (kk, j)),
              "nt": pl.BlockSpec((tn, tk), lambda i, j, kk: (j, kk)),
              "tn": pl.BlockSpec((tk, tn), lambda i, j, kk: (kk, j))}[mode]
    o_spec = pl.BlockSpec((tm, tn), lambda i, j, kk: (i, j))
    return _pc(body_one if nk == 1 else body_acc, name=name, out_shape=_sds((m, n), out_dtype),
               grid=(m // tm, n // tn, nk), in_specs=[a_spec, b_spec], out_specs=o_spec,
               scratch_shapes=[] if nk == 1 else [pltpu.VMEM((tm, tn), F32)], compiler_params=_params(3))(a, b)


SUM_NT_TILE = 512


def _matmul_sum_nt(a_list, b_list, name):
    m, n = a_list[0].shape[0], b_list[0].shape[0]
    cnt = len(a_list)
    tm, tn = _pick(m, SUM_NT_TILE), _pick(n, SUM_NT_TILE)

    def body(*refs):
        a_refs, b_refs, o_ref = refs[:cnt], refs[cnt:2 * cnt], refs[2 * cnt]
        acc = _dot(a_refs[0][...].astype(BF16), b_refs[0][...].astype(BF16), NT)
        for t in range(1, cnt):
            acc = acc + _dot(a_refs[t][...].astype(BF16), b_refs[t][...].astype(BF16), NT)
        o_ref[...] = acc

    in_specs = [pl.BlockSpec((tm, a.shape[1]), lambda j, i: (i, 0)) for a in a_list]
    in_specs += [pl.BlockSpec((tn, b.shape[1]), lambda j, i: (j, 0)) for b in b_list]
    return _pc(body, name=name, out_shape=_sds((m, n)), grid=(n // tn, m // tm), in_specs=in_specs,
               out_specs=pl.BlockSpec((tm, tn), lambda j, i: (i, j)), compiler_params=_params(2))(*a_list, *b_list)


def _norm_mod(x, g, scale, shift):
    r = lax.rsqrt(jnp.mean(x * x, axis=-1, keepdims=True) + EPS)
    return ((x * r) * g) * (1.0 + scale) + shift


def _prenorm_first(x, g, scale, shift, name):
    n, d = x.shape
    tl = _tile(n, NORM_ROW_TILE)

    def body(x_ref, g_ref, sc_ref, sh_ref, h_ref):
        h_ref[...] = _norm_mod(x_ref[...], g_ref[...], sc_ref[...], sh_ref[...]).astype(BF16)

    return _pc(body, name=name, out_shape=_sds((n, d), BF16), grid=(n // tl,),
               in_specs=[_row(tl, d)] + [_const((1, d))] * 3, out_specs=_row(tl, d),
               compiler_params=_params(1))(x, g, scale, shift)


def _prenorm_res(x, y, gate, g, scale, shift, name):
    n, d = x.shape
    tl = _tile(n, NORM_ROW_TILE)

    def body(x_ref, y_ref, gate_ref, g_ref, sc_ref, sh_ref, xo_ref, h_ref):
        xn = x_ref[...] + gate_ref[...] * y_ref[...]
        xo_ref[...] = xn
        h_ref[...] = _norm_mod(xn, g_ref[...], sc_ref[...], sh_ref[...]).astype(BF16)

    return _pc(body, name=name, out_shape=(_sds((n, d)), _sds((n, d), BF16)), grid=(n // tl,),
               in_specs=[_row(tl, d), _row(tl, d)] + [_const((1, d))] * 4,
               out_specs=(_row(tl, d), _row(tl, d)), compiler_params=_params(1))(x, y, gate, g, scale, shift)


def _final_loss(x, y, gate, gf, target, name):
    n, d = x.shape
    tl = _tile(n, NORM_ROW_TILE)
    nb = n // tl

    def body(x_ref, y_ref, gate_ref, gf_ref, t_ref, loss_ref, dx_ref, dys_ref, dgate_ref, dgf_ref,
             acc_l, acc_gate, acc_gf):
        i = pl.program_id(0)

        @pl.when(i == 0)
        def _():
            acc_l[...] = jnp.zeros_like(acc_l)
            acc_gate[...] = jnp.zeros_like(acc_gate)
            acc_gf[...] = jnp.zeros_like(acc_gf)

        yv = y_ref[...]
        gate = gate_ref[...]
        gf = gf_ref[...]
        x2 = x_ref[...] + gate * yv
        r = lax.rsqrt(jnp.mean(x2 * x2, axis=-1, keepdims=True) + EPS)
        xn = x2 * r
        e = xn * gf - t_ref[...]
        acc_l[...] += _fold(e * e)
        dy = e * (1.0 / d)
        acc_gf[...] += _fold(dy * xn)
        dxn = dy * gf
        dx = r * (dxn - xn * jnp.mean(dxn * xn, axis=-1, keepdims=True))
        dx_ref[...] = dx
        dys_ref[...] = (dx * gate).astype(BF16)
        acc_gate[...] += _fold(dx * yv)

        @pl.when(i == nb - 1)
        def _():
            loss_ref[...] = jnp.full((SUBLANES, LANES), 0.5 / d, F32) * jnp.sum(acc_l[...])
            dgate_ref[...] = jnp.sum(acc_gate[...], axis=0, keepdims=True)
            dgf_ref[...] = jnp.sum(acc_gf[...], axis=0, keepdims=True)

    return _pc(body, name=name,
               out_shape=(_sds((SUBLANES, LANES)), _sds((n, d)), _sds((n, d), BF16), _sds((1, d)), _sds((1, d))),
               grid=(nb,),
               in_specs=[_row(tl, d), _row(tl, d), _const((1, d)), _const((1, d)), _row(tl, d)],
               out_specs=(_const((SUBLANES, LANES)), _row(tl, d), _row(tl, d), _const((1, d)), _const((1, d))),
               scratch_shapes=[pltpu.VMEM((SUBLANES, d), F32)] * 3,
               compiler_params=_params(1))(x, y, gate, gf, target)


def _norm_bwd(dh, x, dxo, g, scale, name, y_prev=None, gate_prev=None):
    n, d = x.shape
    tl = _tile(n, NORM_ROW_TILE)
    nb = n // tl
    has_prev = y_prev is not None

    def body(*refs):
        if has_prev:
            (dh_ref, x_ref, dxo_ref, g_ref, sc_ref, yp_ref, gp_ref,
             dx_ref, dsh_ref, dsc_ref, dg_ref, dys_ref, dgp_ref, acc_sh, acc_s, acc_gp) = refs
        else:
            (dh_ref, x_ref, dxo_ref, g_ref, sc_ref,
             dx_ref, dsh_ref, dsc_ref, dg_ref, acc_sh, acc_s) = refs
        i = pl.program_id(0)

        @pl.when(i == 0)
        def _():
            acc_sh[...] = jnp.zeros_like(acc_sh)
            acc_s[...] = jnp.zeros_like(acc_s)
            if has_prev:
                acc_gp[...] = jnp.zeros_like(acc_gp)

        x_ = x_ref[...]
        dh_ = dh_ref[...]
        g_ = g_ref[...]
        one_sc = 1.0 + sc_ref[...]
        r = lax.rsqrt(jnp.mean(x_ * x_, axis=-1, keepdims=True) + EPS)
        xn = x_ * r
        dxn = dh_ * (g_ * one_sc)
        dx = dxo_ref[...] + r * (dxn - xn * jnp.mean(dxn * xn, axis=-1, keepdims=True))
        dx_ref[...] = dx
        acc_sh[...] += _fold(dh_)
        acc_s[...] += _fold(dh_ * xn)
        if has_prev:
            dys_ref[...] = (dx * gp_ref[...]).astype(BF16)
            acc_gp[...] += _fold(dx * yp_ref[...])

        @pl.when(i == nb - 1)
        def _():
            s = jnp.sum(acc_s[...], axis=0, keepdims=True)
            dsh_ref[...] = jnp.sum(acc_sh[...], axis=0, keepdims=True)
            dsc_ref[...] = s * g_
            dg_ref[...] = s * one_sc
            if has_prev:
                dgp_ref[...] = jnp.sum(acc_gp[...], axis=0, keepdims=True)

    vec = _sds((1, d))
    in_specs = [_row(tl, d)] * 3 + [_const((1, d))] * 2
    out_shape = [_sds((n, d)), vec, vec, vec]
    out_specs = [_row(tl, d)] + [_const((1, d))] * 3
    scratch = [pltpu.VMEM((SUBLANES, d), F32)] * 2
    args = [dh, x, dxo, g, scale]
    if has_prev:
        in_specs += [_row(tl, d), _const((1, d))]
        out_shape += [_sds((n, d), BF16), vec]
        out_specs += [_row(tl, d), _const((1, d))]
        scratch += [pltpu.VMEM((SUBLANES, d), F32)]
        args += [y_prev, gate_prev]
    return _pc(body, name=name, out_shape=tuple(out_shape), grid=(nb,), in_specs=in_specs,
               out_specs=tuple(out_specs), scratch_shapes=scratch, compiler_params=_params(1))(*args)


CONV_HALO = 8
CONF_HALO = 32


def _ssd_pre(xbc, dt_raw, conv_w, conv_b, dt_bias, name):
    n, c = xbc.shape
    tl = _tile(n, STRIP_ROW_TILE)
    hb = CONV_HALO
    k_taps = 4

    nu = _strip_units_per_trip(tl)
    nt = tl // (nu * SUBLANES)

    def body(x_ref, xp_ref, dt_ref, w_ref, b_ref, dtb_ref, pre_ref, dts_ref):
        i = pl.program_id(0)
        row = lax.broadcasted_iota(jnp.int32, (SUBLANES, LANES), 0)
        for lc in range(c // LANES):
            ls = slice(lc * LANES, (lc + 1) * LANES)
            w = [_bcast_row(w_ref, k, ls) for k in range(k_taps)]
            b = _bcast_row(b_ref, 0, ls)
            xp0 = jnp.where(i > 0, xp_ref[:, ls], 0.0)

            def strip(t, carry):
                xs = _strip_units(x_ref, ls, t, nu, nt, halo_prev=xp0)
                for u in range(nu):
                    prev, cur = xs[u], xs[u + 1]
                    pre_ref[_unit_rows(t, u, nu), ls] = (
                        b + w[0] * _down(prev, cur, 3, row) + w[1] * _down(prev, cur, 2, row)
                        + w[2] * _down(prev, cur, 1, row) + w[3] * cur)
                return carry

            lax.fori_loop(0, nt, strip, 0)
        v = dt_ref[...] + dtb_ref[...]
        dts_ref[...] = jnp.maximum(v, 0.0) + jnp.log1p(jnp.exp(-jnp.abs(v)))

    return _pc(body, name=name, out_shape=(_sds((n, c)), _sds((n, LANES))), grid=(n // tl,),
               in_specs=[_row(tl, c), _prev(tl, hb, c), _row(tl, LANES), _const((SUBLANES, c)), _const((1, c)),
                         _const((1, LANES))],
               out_specs=(_row(tl, c), _row(tl, LANES)), compiler_params=_params(1))(
        xbc, xbc, dt_raw, conv_w, conv_b, dt_bias)


def _ssd_pre_bwd(dpre, xbc, ddt, dt_raw, conv_w, dt_bias, name):
    n, c = xbc.shape
    tl = _tile(n, STRIP_ROW_TILE)
    nb = n // tl
    hb = CONV_HALO
    k_taps = 4

    nu = _strip_units_per_trip(tl)
    nt = tl // (nu * SUBLANES)

    def body(dp_ref, dpn_ref, x_ref, xp_ref, ddt_ref, dt_ref, w_ref, dtb_ref,
             dx_ref, ddr_ref, dw_ref, db_ref, ddtb_ref, acc_w, acc_b, acc_dtb):
        i = pl.program_id(0)

        @pl.when(i == 0)
        def _():
            acc_w[...] = jnp.zeros_like(acc_w)
            acc_b[...] = jnp.zeros_like(acc_b)
            acc_dtb[...] = jnp.zeros_like(acc_dtb)

        row = lax.broadcasted_iota(jnp.int32, (SUBLANES, LANES), 0)
        zero = jnp.zeros((SUBLANES, LANES), F32)
        for lc in range(c // LANES):
            ls = slice(lc * LANES, (lc + 1) * LANES)
            w = [_bcast_row(w_ref, k, ls) for k in range(k_taps)]
            xp0 = jnp.where(i > 0, xp_ref[:, ls], 0.0)
            dpn0 = jnp.where(i < nb - 1, dpn_ref[:, ls], 0.0)

            def strip(t, carry):
                acc = list(carry)
                dps = _strip_units(dp_ref, ls, t, nu, nt, halo_next=dpn0)
                xs = _strip_units(x_ref, ls, t, nu, nt, halo_prev=xp0)
                dxs = []
                for u in range(nu):
                    d, dn = dps[u], dps[u + 1]
                    dxs.append(w[3] * d + w[2] * _up(d, dn, 1, row) + w[1] * _up(d, dn, 2, row)
                               + w[0] * _up(d, dn, 3, row))
                    prev, cur = xs[u], xs[u + 1]
                    acc[3] = acc[3] + d * cur
                    for k in range(3):
                        acc[k] = acc[k] + d * _down(prev, cur, 3 - k, row)
                    acc[4] = acc[4] + d
                for p in range(nu // 2):
                    dx_ref[_pair_rows(t, p, nu), ls] = jnp.concatenate(dxs[2 * p:2 * p + 2], axis=0).astype(BF16)
                return tuple(acc)

            res = lax.fori_loop(0, nt, strip, (zero,) * 5)
            for k in range(k_taps):
                acc_w[k, :, ls] += res[k]
            acc_b[:, ls] += res[4]
        ddr = ddt_ref[...] * _sig(dt_ref[...] + dtb_ref[...])
        ddr_ref[...] = ddr.astype(BF16)
        acc_dtb[...] += _fold(ddr)

        @pl.when(i == nb - 1)
        def _():
            dw_ref[...] = jnp.zeros_like(dw_ref)
            for k in range(k_taps):
                dw_ref[k:k + 1, :] = jnp.sum(acc_w[k], axis=0, keepdims=True)
            db_ref[...] = jnp.sum(acc_b[...], axis=0, keepdims=True)
            ddtb_ref[...] = jnp.sum(acc_dtb[...], axis=0, keepdims=True)

    return _pc(body, name=name,
               out_shape=(_sds((n, c), BF16), _sds((n, LANES), BF16), _sds((SUBLANES, c)), _sds((1, c)),
                          _sds((1, LANES))),
               grid=(nb,),
               in_specs=[_row(tl, c), _next(tl, hb, c, n), _row(tl, c), _prev(tl, hb, c), _row(tl, LANES),
                         _row(tl, LANES), _const((SUBLANES, c)), _const((1, LANES))],
               out_specs=(_row(tl, c), _row(tl, LANES), _const((SUBLANES, c)), _const((1, c)), _const((1, LANES))),
               scratch_shapes=[pltpu.VMEM((k_taps, SUBLANES, c), F32), pltpu.VMEM((SUBLANES, c), F32),
                               pltpu.VMEM((SUBLANES, LANES), F32)],
               compiler_params=_params(1))(dpre, dpre, xbc, xbc, ddt, dt_raw, conv_w, dt_bias)


def _expand_mat():
    r = lax.broadcasted_iota(jnp.int32, (LANES, SSD_INNER), 0)
    c = lax.broadcasted_iota(jnp.int32, (LANES, SSD_INNER), 1)
    return (jnp.right_shift(c, 6) == r).astype(BF16)


def _reduce_mat():
    r = lax.broadcasted_iota(jnp.int32, (SSD_INNER, LANES), 0)
    c = lax.broadcasted_iota(jnp.int32, (SSD_INNER, LANES), 1)
    return (jnp.right_shift(r, 6) == c).astype(BF16)


def _ssd_common(pre, dt, alog):
    q = SSD_CHUNK
    sg = _sig(pre)
    act = pre * sg
    lane = lax.broadcasted_iota(jnp.int32, (1, LANES), 1)
    a_neg = jnp.where(lane < SSD_HEADS, -jnp.exp(alog), 0.0)
    rr = lax.broadcasted_iota(jnp.int32, (q, q), 0)
    cc = lax.broadcasted_iota(jnp.int32, (q, q), 1)
    causal = rr >= cc
    cum = _sel_dot(causal.astype(BF16), dt * a_neg)
    e_mat = _expand_mat()
    dtx = _dot_sel(dt, e_mat)
    cumx = _dot_sel(cum, e_mat)
    return sg, act, a_neg, causal, cum, e_mat, dtx, cumx


def _ssd_scan(pre, dt, alog, dvec, name):
    n = pre.shape[0]
    q = SSD_CHUNK
    nc = n // q

    def body(pre_ref, dt_ref, alog_ref, d_ref, y_ref, hp_ref, state):
        i = pl.program_id(0)

        @pl.when(i == 0)
        def _():
            state[...] = jnp.zeros_like(state)

        dt_ = dt_ref[...]
        _, act, _, causal, cum, e_mat, dtx, cumx = _ssd_common(pre_ref[...], dt_, alog_ref[...])
        xs = act[:, :SSD_INNER]
        bm = act[:, SSD_INNER:SSD_INNER + LANES]
        cm = act[:, SSD_INNER + LANES:]
        cum_t = cum.T
        clx = cumx[q - 1:q, :]
        xc = xs * dtx
        xd = xc * jnp.exp(clx - cumx)
        doutx = jnp.exp(cumx)
        edec = jnp.exp(clx)
        dx_row = _dot_sel(jnp.broadcast_to(d_ref[...], (SUBLANES, LANES)), e_mat)[0:1, :]
        hp_ref[0] = state[...]
        bb = bm.astype(BF16)
        cb = cm.astype(BF16)
        lane = lax.broadcasted_iota(jnp.int32, (1, LANES), 1)
        row = lax.broadcasted_iota(jnp.int32, (LANES, 1), 0)
        cbs = []
        for g in range(2):
            cg = jnp.where(jnp.right_shift(lane, 6) == g, cm, 0.0).astype(BF16)
            cbs.append(_dot(cg, bb, NT))
        for j in range(SSD_HEADS // 2):
            sl = slice(j * LANES, (j + 1) * LANES)
            g = j // 4
            xcj = xc[:, sl].astype(BF16)
            halves = []
            for half in range(2):
                h = 2 * j + half
                seg = cum[:, h:h + 1] - cum_t[h:h + 1, :]
                w = cbs[g] * jnp.exp(jnp.where(causal, seg, -jnp.inf))
                halves.append(_dot(w.astype(BF16), xcj))
            y_diag = jnp.where(lane < SSD_HEAD_DIM, halves[0], halves[1])
            hj = state[:, sl]
            y_off = doutx[:, sl] * _dot(cb, hj.astype(BF16))
            y_ref[:, sl] = y_diag + y_off + xs[:, sl] * dx_row[:, sl]
            st = _dot(bb, xd[:, sl].astype(BF16), TN)
            state[:, sl] = hj * edec[:, sl] + jnp.where(jnp.right_shift(row, 6) == g, st, 0.0)

    return _pc(body, name=name, out_shape=(_sds((n, SSD_INNER)), _sds((nc, LANES, SSD_INNER))), grid=(nc,),
               in_specs=[_row(q, SSD_XBC), _row(q, LANES), _const((1, LANES)), _const((1, LANES))],
               out_specs=(_row(q, SSD_INNER), pl.BlockSpec((1, LANES, SSD_INNER), lambda i: (i, 0, 0))),
               scratch_shapes=[pltpu.VMEM((LANES, SSD_INNER), F32)], compiler_params=_params(1))(pre, dt, alog, dvec)


def _ssd_scan_bwd(pre, dt, hprev, dy, alog, dvec, name):
    n = pre.shape[0]
    q = SSD_CHUNK
    nc = n // q

    def body(pre_ref, dt_ref, hp_ref, dy_ref, alog_ref, d_ref, dpre_ref, ddt_ref, da_ref, dd_ref,
             d_state, dxc_s, dcx_s, dcl_s, acc_a, acc_d):
        i = pl.program_id(0)

        @pl.when(i == 0)
        def _():
            d_state[...] = jnp.zeros_like(d_state)
            acc_a[...] = jnp.zeros_like(acc_a)
            acc_d[...] = jnp.zeros_like(acc_d)

        pre_ = pre_ref[...]
        dt_ = dt_ref[...]
        sg, act, a_neg, causal, cum, e_mat, dtx, cumx = _ssd_common(pre_, dt_, alog_ref[...])
        r_mat = _reduce_mat()
        xs = act[:, :SSD_INNER]
        bm = act[:, SSD_INNER:SSD_INNER + LANES]
        cm = act[:, SSD_INNER + LANES:]
        cum_t = cum.T
        clx = cumx[q - 1:q, :]
        xc = xs * dtx
        dsx = jnp.exp(clx - cumx)
        doutx = jnp.exp(cumx)
        edec = jnp.exp(clx)
        dx_row = _dot_sel(jnp.broadcast_to(d_ref[...], (SUBLANES, LANES)), e_mat)[0:1, :]
        dy_ = dy_ref[...]
        acc_d[...] += _fold(dy_ * xs)
        bb = bm.astype(BF16)
        cb = cm.astype(BF16)
        lane = lax.broadcasted_iota(jnp.int32, (1, LANES), 1)
        row = lax.broadcasted_iota(jnp.int32, (LANES, 1), 0)
        d_c = jnp.zeros((q, LANES), F32)
        d_b = jnp.zeros((q, LANES), F32)

        for j in range(SSD_HEADS // 2):
            sl = slice(j * LANES, (j + 1) * LANES)
            g = j // 4
            hj = hp_ref[0, :, sl]
            hjb = hj.astype(BF16)
            dyj = dy_[:, sl]
            tj = _dot(cb, hjb)
            dtj = (doutx[:, sl] * dyj).astype(BF16)
            dcx = dyj * tj * doutx[:, sl]
            d_c = d_c + _dot(dtj, hjb, NT)
            dhn = d_state[:, sl]
            dhp = dhn * edec[:, sl] + jnp.where(jnp.right_shift(row, 6) == g, _dot(cb, dtj, TN), 0.0)
            dcl = jnp.sum(dhn * hj, axis=0, keepdims=True) * edec[:, sl]
            dsb = dhn.astype(BF16)
            dxd = _dot(bb, dsb)
            xcj = xc[:, sl]
            dsj = dsx[:, sl]
            d_b = d_b + _dot((xcj * dsj).astype(BF16), dsb, NT)
            dds = dxd * xcj * dsj
            d_state[:, sl] = dhp
            dxc_s[:, sl] = dxd * dsj
            dcx_s[:, sl] = dcx - dds
            dcl_s[:, sl] = jnp.broadcast_to(dcl + jnp.sum(dds, axis=0, keepdims=True), (SUBLANES, LANES))

        dcum_c = jnp.zeros((q, LANES), F32)
        dcum_t = jnp.zeros((LANES, q), F32)
        for g in range(2):
            gmask = jnp.right_shift(lane, 6) == g
            cg = jnp.where(gmask, cm, 0.0).astype(BF16)
            cbg = _dot(cg, bb, NT)
            d_cb = jnp.zeros((q, q), F32)
            for hh in range(SSD_HEADS // 2):
                h = g * (SSD_HEADS // 2) + hh
                j, half = h // 2, h % 2
                sl = slice(j * LANES, (j + 1) * LANES)
                hmask = jnp.right_shift(lane, 6) == half
                seg = cum[:, h:h + 1] - cum_t[h:h + 1, :]
                lm = jnp.exp(jnp.where(causal, seg, -jnp.inf))
                w = cbg * lm
                dyj = dy_[:, sl]
                dw = _dot(jnp.where(hmask, dyj, 0.0).astype(BF16), xc[:, sl].astype(BF16), NT)
                dxch = _dot(w.astype(BF16), dyj.astype(BF16), TN)
                dxc_s[:, sl] += jnp.where(hmask, dxch, 0.0)
                d_cb = d_cb + dw * lm
                m = dw * w
                dcum_c = dcum_c + jnp.sum(m, axis=1, keepdims=True) * (lane == h).astype(F32)
                dcum_t = dcum_t + (row == h).astype(F32) * jnp.sum(m, axis=0, keepdims=True)
            d_cbb = d_cb.astype(BF16)
            d_c = d_c + jnp.where(gmask, _dot(d_cbb, bb), 0.0)
            d_b = d_b + jnp.where(gmask, _dot(d_cbb, cb, TN), 0.0)

        dcl_row = _dot_sel(dcl_s[...], r_mat)[0:1, :]
        rowq = lax.broadcasted_iota(jnp.int32, (q, 1), 0)
        dcum = (dcum_c - dcum_t.T + _dot_sel(dcx_s[...], r_mat)
                + jnp.where(rowq == q - 1, dcl_row, 0.0))
        rr = lax.broadcasted_iota(jnp.int32, (q, q), 0)
        cc = lax.broadcasted_iota(jnp.int32, (q, q), 1)
        dadt = _sel_dot((rr <= cc).astype(BF16), dcum)
        dxc = dxc_s[...]
        ddt_ref[...] = dadt * a_neg + _dot_sel(dxc * xs, r_mat)
        acc_a[...] += _fold(dadt * dt_)
        dsilu = sg * (1.0 + pre_ * (1.0 - sg))
        dpre_ref[:, :SSD_INNER] = (dxc * dtx + dy_ * dx_row) * dsilu[:, :SSD_INNER]
        dpre_ref[:, SSD_INNER:SSD_INNER + LANES] = d_b * dsilu[:, SSD_INNER:SSD_INNER + LANES]
        dpre_ref[:, SSD_INNER + LANES:] = d_c * dsilu[:, SSD_INNER + LANES:]

        @pl.when(i == nc - 1)
        def _():
            da_ref[...] = jnp.sum(acc_a[...], axis=0, keepdims=True) * a_neg
            dd_ref[...] = jnp.sum(_dot_sel(acc_d[...], r_mat), axis=0, keepdims=True)

    rev = lambda i: (nc - 1 - i, 0)
    return _pc(body, name=name,
               out_shape=(_sds((n, SSD_XBC)), _sds((n, LANES)), _sds((1, LANES)), _sds((1, LANES))), grid=(nc,),
               in_specs=[pl.BlockSpec((q, SSD_XBC), rev), pl.BlockSpec((q, LANES), rev),
                         pl.BlockSpec((1, LANES, SSD_INNER), lambda i: (nc - 1 - i, 0, 0)),
                         pl.BlockSpec((q, SSD_INNER), rev), _const((1, LANES)), _const((1, LANES))],
               out_specs=(pl.BlockSpec((q, SSD_XBC), rev), pl.BlockSpec((q, LANES), rev), _const((1, LANES)),
                          _const((1, LANES))),
               scratch_shapes=[pltpu.VMEM((LANES, SSD_INNER), F32), pltpu.VMEM((q, SSD_INNER), F32),
                               pltpu.VMEM((q, SSD_INNER), F32), pltpu.VMEM((SUBLANES, SSD_INNER), F32),
                               pltpu.VMEM((SUBLANES, LANES), F32), pltpu.VMEM((SUBLANES, SSD_INNER), F32)],
               compiler_params=_params(1))(pre, dt, hprev, dy, alog, dvec)


def _group_norm_parts(v):
    half = SSD_INNER // 2
    r0 = lax.rsqrt(jnp.mean(v[:, :half] * v[:, :half], axis=-1, keepdims=True) + EPS)
    r1 = lax.rsqrt(jnp.mean(v[:, half:] * v[:, half:], axis=-1, keepdims=True) + EPS)
    lane = lax.broadcasted_iota(jnp.int32, (1, SSD_INNER), 1)
    return jnp.where(lane < half, r0, r1)


def _group_mean(v):
    half = SSD_INNER // 2
    m0 = jnp.mean(v[:, :half], axis=-1, keepdims=True)
    m1 = jnp.mean(v[:, half:], axis=-1, keepdims=True)
    lane = lax.broadcasted_iota(jnp.int32, (1, SSD_INNER), 1)
    return jnp.where(lane < half, m0, m1)


def _ssd_post(y, z, g, name):
    n, d = y.shape
    tl = _tile(n, NORM_ROW_TILE)

    def body(y_ref, z_ref, g_ref, o_ref):
        z_ = z_ref[...]
        v = y_ref[...] * (z_ * _sig(z_))
        o_ref[...] = ((v * _group_norm_parts(v)) * g_ref[...]).astype(BF16)

    return _pc(body, name=name, out_shape=_sds((n, d), BF16), grid=(n // tl,),
               in_specs=[_row(tl, d), _row(tl, d), _const((1, d))], out_specs=_row(tl, d),
               compiler_params=_params(1))(y, z, g)


def _ssd_post_bwd(dout, y, z, g, name):
    n, d = y.shape
    tl = _tile(n, NORM_ROW_TILE)
    nb = n // tl

    def body(do_ref, y_ref, z_ref, g_ref, dy_ref, dz_ref, dg_ref, acc_g):
        i = pl.program_id(0)

        @pl.when(i == 0)
        def _():
            acc_g[...] = jnp.zeros_like(acc_g)

        z_ = z_ref[...]
        y_ = y_ref[...]
        sz = _sig(z_)
        silu_z = z_ * sz
        v = y_ * silu_z
        rs = _group_norm_parts(v)
        nv = v * rs
        do_ = do_ref[...]
        acc_g[...] += _fold(do_ * nv)
        dn = do_ * g_ref[...]
        dv = rs * (dn - nv * _group_mean(dn * nv))
        dy_ref[...] = dv * silu_z
        dz_ref[...] = (dv * y_ * (sz * (1.0 + z_ * (1.0 - sz)))).astype(BF16)

        @pl.when(i == nb - 1)
        def _():
            dg_ref[...] = jnp.sum(acc_g[...], axis=0, keepdims=True)

    return _pc(body, name=name, out_shape=(_sds((n, d)), _sds((n, d), BF16), _sds((1, d))), grid=(nb,),
               in_specs=[_row(tl, d), _row(tl, d), _row(tl, d), _const((1, d))],
               out_specs=(_row(tl, d), _row(tl, d), _const((1, d))),
               scratch_shapes=[pltpu.VMEM((SUBLANES, d), F32)], compiler_params=_params(1))(dout, y, z, g)


def _layer_norm_parts(uc):
    mu = jnp.mean(uc, axis=-1, keepdims=True)
    xc = uc - mu
    rstd = lax.rsqrt(jnp.mean(xc * xc, axis=-1, keepdims=True) + EPS)
    return xc * rstd, rstd


def _rows_x8(w):
    return jnp.broadcast_to(w[:, None, :], (w.shape[0], SUBLANES, w.shape[1]))


def _glu_units(x_ref, ls, gl, t, nu, halo):
    nh = len(halo)
    units = []
    for h in range(nh):
        rows = pl.ds(pl.multiple_of(jnp.maximum(t * nu - nh + h, 0) * SUBLANES, SUBLANES), SUBLANES)
        units.append(jnp.where(t > 0, x_ref[rows, ls] * _sig(x_ref[rows, gl]), halo[h]))
    for u in range(nu):
        rows = _unit_rows(t, u, nu)
        units.append(x_ref[rows, ls] * _sig(x_ref[rows, gl]))
    return units


def _memo_rolls(units):
    memo = {}

    def rolls(key):
        if key not in memo:
            memo[key] = pltpu.roll(units[key[0]], key[1], 0)
        return memo[key]

    return rolls


def _window(units, e, s, rolls, row, up):
    a, b = divmod(s, SUBLANES)
    if b == 0:
        return units[e + a] if up else units[e - a]
    if up:
        sh = SUBLANES - b
        return jnp.where(row < sh, rolls((e + a, sh)), rolls((e + a + 1, sh)))
    return jnp.where(row < b, rolls((e - a - 1, b)), rolls((e - a, b)))


def _conf_fwd(conf_in, conv_w, conv_b, ln_g, ln_b, name):
    n = conf_in.shape[0]
    c = CONF_WIDTH
    tl = _tile(n, STRIP_ROW_TILE)
    hb = CONF_HALO
    k_taps = CONF_KERNEL

    nu = _strip_units_per_trip(tl)
    nt = tl // (nu * SUBLANES)
    nh = hb // SUBLANES

    def body(x_ref, xp_ref, w_ref, b_ref, g_ref, beta_ref, o_ref, uc_ref):
        i = pl.program_id(0)
        row = lax.broadcasted_iota(jnp.int32, (SUBLANES, LANES), 0)
        for lc in range(c // LANES):
            ls = slice(lc * LANES, (lc + 1) * LANES)
            gl = slice(c + lc * LANES, c + (lc + 1) * LANES)
            bias = _bcast_row(b_ref, 0, ls)
            halo = [jnp.where(i > 0, xp_ref[SUBLANES * h:SUBLANES * (h + 1), ls]
                              * _sig(xp_ref[SUBLANES * h:SUBLANES * (h + 1), gl]), 0.0) for h in range(nh)]

            def strip(t, carry):
                units = _glu_units(x_ref, ls, gl, t, nu, halo)
                rolls = _memo_rolls(units)
                for u in range(nu):
                    acc = bias
                    for k in range(k_taps):
                        acc = acc + w_ref[k, :, ls] * _window(units, u + nh, k_taps - 1 - k, rolls, row, up=False)
                    uc_ref[_unit_rows(t, u, nu), ls] = acc
                return carry

            lax.fori_loop(0, nt, strip, 0)
        nv, _ = _layer_norm_parts(uc_ref[...])
        v = nv * g_ref[...] + beta_ref[...]
        o_ref[...] = (v * _sig(v)).astype(BF16)

    return _pc(body, name=name, out_shape=(_sds((n, c), BF16), _sds((n, c))), grid=(n // tl,),
               in_specs=[_row(tl, 2 * c), _prev(tl, hb, 2 * c), _const((hb, SUBLANES, c)), _const((1, c)),
                         _const((1, c)), _const((1, c))],
               out_specs=(_row(tl, c), _row(tl, c)), compiler_params=_params(1))(
        conf_in, conf_in, _rows_x8(conv_w), conv_b, ln_g, ln_b)


def _conf_bwd(dout, uc, conf_in, conv_w, ln_g, ln_b, name):
    n = conf_in.shape[0]
    c = CONF_WIDTH
    tl = _tile(n, STRIP_ROW_TILE)
    nb = n // tl
    hb = CONF_HALO
    k_taps = CONF_KERNEL

    nu = _strip_units_per_trip(tl)
    nt = tl // (nu * SUBLANES)
    nh = hb // SUBLANES

    def body(do_ref, don_ref, uc_ref, ucn_ref, x_ref, xp_ref, w_ref, g_ref, beta_ref,
             dx_ref, dw_ref, db_ref, dg_ref, dbeta_ref, dbuf, acc_w, acc_b, acc_g, acc_beta):
        i = pl.program_id(0)

        @pl.when(i == 0)
        def _():
            acc_w[...] = jnp.zeros_like(acc_w)
            acc_b[...] = jnp.zeros_like(acc_b)
            acc_g[...] = jnp.zeros_like(acc_g)
            acc_beta[...] = jnp.zeros_like(acc_beta)

        g_ = g_ref[...]
        beta_ = beta_ref[...]

        def d_conv_out(do_, uc_):
            nv, rstd = _layer_norm_parts(uc_)
            v = nv * g_ + beta_
            sv = _sig(v)
            dv = do_ * (sv * (1.0 + v * (1.0 - sv)))
            dn = dv * g_
            duc = rstd * (dn - jnp.mean(dn, axis=-1, keepdims=True)
                          - nv * jnp.mean(dn * nv, axis=-1, keepdims=True))
            return duc, dv, nv

        duc, dv, nv = d_conv_out(do_ref[...], uc_ref[...])
        acc_g[...] += _fold(dv * nv)
        acc_beta[...] += _fold(dv)
        acc_b[...] += _fold(duc)
        dbuf[pl.ds(0, tl), :] = duc
        ducn, _, _ = d_conv_out(don_ref[...], ucn_ref[...])
        dbuf[pl.ds(tl, hb), :] = jnp.where(i < nb - 1, ducn, 0.0)

        row = lax.broadcasted_iota(jnp.int32, (SUBLANES, LANES), 0)
        for lc in range(c // LANES):
            ls = slice(lc * LANES, (lc + 1) * LANES)
            gl = slice(c + lc * LANES, c + (lc + 1) * LANES)
            halo = [jnp.where(i > 0, xp_ref[SUBLANES * h:SUBLANES * (h + 1), ls]
                              * _sig(xp_ref[SUBLANES * h:SUBLANES * (h + 1), gl]), 0.0) for h in range(nh)]

            def strip(t, carry):
                us = _glu_units(x_ref, ls, gl, t, nu, halo)
                ds = [dbuf[pl.ds(pl.multiple_of((t * nu + u) * SUBLANES, SUBLANES), SUBLANES), ls]
                      for u in range(nu + nh)]
                u_rolls, d_rolls = _memo_rolls(us), _memo_rolls(ds)
                for k in range(k_taps):
                    part = ds[0] * _window(us, nh, k_taps - 1 - k, u_rolls, row, up=False)
                    for u in range(1, nu):
                        part = part + ds[u] * _window(us, u + nh, k_taps - 1 - k, u_rolls, row, up=False)
                    acc_w[k, :, ls] += part
                dus = []
                for u in range(nu):
                    du = w_ref[0, :, ls] * _window(ds, u, k_taps - 1, d_rolls, row, up=True)
                    for k in range(1, k_taps):
                        du = du + w_ref[k, :, ls] * _window(ds, u, k_taps - 1 - k, d_rolls, row, up=True)
                    dus.append(du)
                for p in range(nu // 2):
                    rows = _pair_rows(t, p, nu)
                    du2 = jnp.concatenate(dus[2 * p:2 * p + 2], axis=0)
                    val, sgate = x_ref[rows, ls], _sig(x_ref[rows, gl])
                    dx_ref[rows, ls] = (du2 * sgate).astype(BF16)
                    dx_ref[rows, gl] = (du2 * val * sgate * (1.0 - sgate)).astype(BF16)
                return carry

            lax.fori_loop(0, nt, strip, 0)

        @pl.when(i == nb - 1)
        def _():
            dw_ref[...] = jnp.zeros_like(dw_ref)
            for k in range(k_taps):
                dw_ref[k:k + 1, :] = jnp.sum(acc_w[k], axis=0, keepdims=True)
            db_ref[...] = jnp.sum(acc_b[...], axis=0, keepdims=True)
            dg_ref[...] = jnp.sum(acc_g[...], axis=0, keepdims=True)
            dbeta_ref[...] = jnp.sum(acc_beta[...], axis=0, keepdims=True)

    vec = _sds((1, c))
    return _pc(body, name=name, out_shape=(_sds((n, 2 * c), BF16), _sds((hb, c)), vec, vec, vec), grid=(nb,),
               in_specs=[_row(tl, c), _next(tl, hb, c, n), _row(tl, c), _next(tl, hb, c, n), _row(tl, 2 * c),
                         _prev(tl, hb, 2 * c), _const((hb, SUBLANES, c)), _const((1, c)), _const((1, c))],
               out_specs=(_row(tl, 2 * c), _const((hb, c)), _const((1, c)), _const((1, c)), _const((1, c))),
               scratch_shapes=[pltpu.VMEM((tl + hb, c), F32),
                               pltpu.VMEM((k_taps, SUBLANES, c), F32), pltpu.VMEM((SUBLANES, c), F32),
                               pltpu.VMEM((SUBLANES, c), F32), pltpu.VMEM((SUBLANES, c), F32)],
               compiler_params=_params(1))(dout, dout, uc, uc, conf_in, conf_in, _rows_x8(conv_w), ln_g, ln_b)


def _sc_fwd(sc_in, conv_w, name):
    n = sc_in.shape[0]
    c = SC_WIDTH
    tl = _tile(n, STRIP_ROW_TILE)
    hb = CONV_HALO

    nu = _strip_units_per_trip(tl)
    nt = tl // (nu * SUBLANES)

    def body(x_ref, xp_ref, w_ref, o_ref):
        i = pl.program_id(0)
        row = lax.broadcasted_iota(jnp.int32, (SUBLANES, LANES), 0)
        for lc in range(c // LANES):
            ls = slice(lc * LANES, (lc + 1) * LANES)
            gc_ls = slice(c + lc * LANES, c + (lc + 1) * LANES)
            xv_ls = slice(2 * c + lc * LANES, 2 * c + (lc + 1) * LANES)
            w = [_bcast_row(w_ref, k, ls) for k in range(3)]
            gc0 = jnp.where(i > 0, xp_ref[:, gc_ls], 0.0)
            xv0 = xp_ref[:, xv_ls]

            def strip(t, carry):
                gcs = _strip_units(x_ref, gc_ls, t, nu, nt, halo_prev=gc0)
                xvs = _strip_units(x_ref, xv_ls, t, nu, nt, halo_prev=xv0)
                ps = [a * b for a, b in zip(gcs, xvs)]
                outs = []
                for u in range(nu):
                    prev, cur = ps[u], ps[u + 1]
                    cv = w[0] * _down(prev, cur, 2, row) + w[1] * _down(prev, cur, 1, row) + w[2] * cur
                    outs.append(x_ref[_unit_rows(t, u, nu), ls] * cv)
                for p in range(nu // 2):
                    o_ref[_pair_rows(t, p, nu), ls] = jnp.concatenate(outs[2 * p:2 * p + 2], axis=0).astype(BF16)
                return carry

            lax.fori_loop(0, nt, strip, 0)

    return _pc(body, name=name, out_shape=_sds((n, c), BF16), grid=(n // tl,),
               in_specs=[_row(tl, 3 * c), _prev(tl, hb, 3 * c), _const((SUBLANES, c))], out_specs=_row(tl, c),
               compiler_params=_params(1))(sc_in, sc_in, conv_w)


def _sc_bwd(dout, sc_in, conv_w, name):
    n = sc_in.shape[0]
    c = SC_WIDTH
    tl = _tile(n, STRIP_ROW_TILE)
    nb = n // tl
    hb = CONV_HALO

    nu = _strip_units_per_trip(tl)
    nt = tl // (nu * SUBLANES)

    def body(do_ref, don_ref, x_ref, xp_ref, xn_ref, w_ref, dx_ref, dw_ref, acc_w):
        i = pl.program_id(0)

        @pl.when(i == 0)
        def _():
            acc_w[...] = jnp.zeros_like(acc_w)

        row = lax.broadcasted_iota(jnp.int32, (SUBLANES, LANES), 0)
        zero = jnp.zeros((SUBLANES, LANES), F32)
        for lc in range(c // LANES):
            ls = slice(lc * LANES, (lc + 1) * LANES)
            gc_ls = slice(c + lc * LANES, c + (lc + 1) * LANES)
            xv_ls = slice(2 * c + lc * LANES, 2 * c + (lc + 1) * LANES)
            w = [_bcast_row(w_ref, k, ls) for k in range(3)]
            gc0 = jnp.where(i > 0, xp_ref[:, gc_ls], 0.0)
            xv0 = xp_ref[:, xv_ls]
            don0 = jnp.where(i < nb - 1, don_ref[:, ls], 0.0)
            gbn0 = xn_ref[:, ls]

            def strip(t, carry):
                acc = list(carry)
                dos = _strip_units(do_ref, ls, t, nu, nt, halo_next=don0)
                gbs = _strip_units(x_ref, ls, t, nu, nt, halo_next=gbn0)
                gcs = _strip_units(x_ref, gc_ls, t, nu, nt, halo_prev=gc0)
                xvs = _strip_units(x_ref, xv_ls, t, nu, nt, halo_prev=xv0)
                dcv = [a * b for a, b in zip(dos, gbs)]
                ps = [a * b for a, b in zip(gcs, xvs)]
                d_gb, d_gc, d_xv = [], [], []
                for u in range(nu):
                    prev, cur = ps[u], ps[u + 1]
                    p1, p2 = _down(prev, cur, 1, row), _down(prev, cur, 2, row)
                    d, dn = dcv[u], dcv[u + 1]
                    dp = w[2] * d + w[1] * _up(d, dn, 1, row) + w[0] * _up(d, dn, 2, row)
                    d_gb.append(dos[u] * (w[0] * p2 + w[1] * p1 + w[2] * cur))
                    d_gc.append(dp * xvs[u + 1])
                    d_xv.append(dp * gcs[u + 1])
                    acc[0] = acc[0] + d * p2
                    acc[1] = acc[1] + d * p1
                    acc[2] = acc[2] + d * cur
                for p in range(nu // 2):
                    rows = _pair_rows(t, p, nu)
                    for vals, lanes in ((d_gb, ls), (d_gc, gc_ls), (d_xv, xv_ls)):
                        dx_ref[rows, lanes] = jnp.concatenate(vals[2 * p:2 * p + 2], axis=0).astype(BF16)
                return tuple(acc)

            res = lax.fori_loop(0, nt, strip, (zero,) * 3)
            for k in range(3):
                acc_w[k, :, ls] += res[k]

        @pl.when(i == nb - 1)
        def _():
            dw_ref[...] = jnp.zeros_like(dw_ref)
            for k in range(3):
                dw_ref[k:k + 1, :] = jnp.sum(acc_w[k], axis=0, keepdims=True)

    return _pc(body, name=name, out_shape=(_sds((n, 3 * c), BF16), _sds((SUBLANES, c))), grid=(nb,),
               in_specs=[_row(tl, c), _next(tl, hb, c, n), _row(tl, 3 * c), _prev(tl, hb, 3 * c),
                         _next(tl, hb, c, n), _const((SUBLANES, c))],
               out_specs=(_row(tl, 3 * c), _const((SUBLANES, c))),
               scratch_shapes=[pltpu.VMEM((3, SUBLANES, c), F32)],
               compiler_params=_params(1))(dout, dout, sc_in, sc_in, sc_in, conv_w)


def _merge_fwd(gates, ya, yb, yc, b_gate, name):
    n, d = ya.shape
    tl = _tile(n)

    def body(gt_ref, ya_ref, yb_ref, yc_ref, b_ref, o_ref):
        gt = _sig(gt_ref[...] + b_ref[...])
        o_ref[...] = (gt[:, :d] * ya_ref[...] + gt[:, d:2 * d] * yb_ref[...] + gt[:, 2 * d:] * yc_ref[...]).astype(BF16)

    return _pc(body, name=name, out_shape=_sds((n, d), BF16), grid=(n // tl,),
               in_specs=[_row(tl, 3 * d), _row(tl, d), _row(tl, d), _row(tl, d), _const((1, 3 * d))],
               out_specs=_row(tl, d), compiler_params=_params(1))(gates, ya, yb, yc, b_gate)


def _merge_bwd(dm, gates, ya, yb, yc, b_gate, name):
    n, d = ya.shape
    tl = _tile(n)
    nb = n // tl

    def body(dm_ref, gt_ref, ya_ref, yb_ref, yc_ref, b_ref, dya_ref, dyb_ref, dyc_ref, dgt_ref, db_ref, acc):
        i = pl.program_id(0)

        @pl.when(i == 0)
        def _():
            acc[...] = jnp.zeros_like(acc)

        dm_ = dm_ref[...]
        gt = _sig(gt_ref[...] + b_ref[...])
        for idx, (y_ref, dy_ref) in enumerate(((ya_ref, dya_ref), (yb_ref, dyb_ref), (yc_ref, dyc_ref))):
            gk = gt[:, idx * d:(idx + 1) * d]
            dy_ref[...] = (dm_ * gk).astype(BF16)
            dpre = dm_ * y_ref[...] * gk * (1.0 - gk)
            dgt_ref[:, idx * d:(idx + 1) * d] = dpre.astype(BF16)
            acc[:, idx * d:(idx + 1) * d] += _fold(dpre)

        @pl.when(i == nb - 1)
        def _():
            db_ref[...] = jnp.sum(acc[...], axis=0, keepdims=True)

    bf = _sds((n, d), BF16)
    return _pc(body, name=name, out_shape=(bf, bf, bf, _sds((n, 3 * d), BF16), _sds((1, 3 * d))), grid=(nb,),
               in_specs=[_row(tl, d), _row(tl, 3 * d), _row(tl, d), _row(tl, d), _row(tl, d), _const((1, 3 * d))],
               out_specs=(_row(tl, d), _row(tl, d), _row(tl, d), _row(tl, 3 * d), _const((1, 3 * d))),
               scratch_shapes=[pltpu.VMEM((SUBLANES, 3 * d), F32)], compiler_params=_params(1))(
        dm, gates, ya, yb, yc, b_gate)


FFN_COLS = 1408
FFN_STRIP = 64
FFN_STRIP_BWD = 32


def _down(prev, cur, s, row):
    return jnp.where(row < s, pltpu.roll(prev, s, 0), pltpu.roll(cur, s, 0))


def _up(cur, nxt, s, row):
    return jnp.where(row < SUBLANES - s, pltpu.roll(cur, SUBLANES - s, 0), pltpu.roll(nxt, SUBLANES - s, 0))


def _ffn_mid(up, conv_w, conv_b, name):
    n = up.shape[0]
    tl = _tile(n, STRIP_ROW_TILE)
    hb = CONV_HALO
    tc = FFN_COLS
    ncb = D_FF // tc

    def spec(shape_rows, idx_fn, off):
        return pl.BlockSpec((shape_rows, tc), lambda j, i, off=off: (idx_fn(i), j + off))

    r = tl // hb
    cur = lambda i: i
    prv = lambda i: jnp.maximum(i * r - 1, 0)

    def body(g_ref, gp_ref, v_ref, vp_ref, wg_ref, wv_ref, bg_ref, bv_ref, o_ref):
        i = pl.program_id(1)
        row = lax.broadcasted_iota(jnp.int32, (SUBLANES, LANES), 0)
        full = lambda ref, k, ls: jnp.broadcast_to(ref[k:k + 1, ls], (SUBLANES, LANES))
        for lc in range(tc // LANES):
            ls = slice(lc * LANES, (lc + 1) * LANES)
            wg = [full(wg_ref, k, ls) for k in range(3)]
            wv = [full(wv_ref, k, ls) for k in range(3)]
            bg, bv = full(bg_ref, 0, ls), full(bv_ref, 0, ls)

            def conv(prev, x, w, b):
                return b + w[0] * _down(prev, x, 2, row) + w[1] * _down(prev, x, 1, row) + w[2] * x

            def strip(t, carry):
                gs, vs = [carry[0]], [carry[1]]
                for u in range(FFN_STRIP // SUBLANES):
                    rows = pl.ds(pl.multiple_of(t * FFN_STRIP + u * SUBLANES, SUBLANES), SUBLANES)
                    gs.append(g_ref[rows, ls])
                    vs.append(v_ref[rows, ls])
                outs = []
                for u in range(FFN_STRIP // SUBLANES):
                    ug = conv(gs[u], gs[u + 1], wg, bg)
                    outs.append(ug * _sig(ug) * conv(vs[u], vs[u + 1], wv, bv))
                for p in range(FFN_STRIP // 16):
                    rows = pl.ds(pl.multiple_of(t * FFN_STRIP + p * 16, 16), 16)
                    o_ref[rows, ls] = jnp.concatenate(outs[2 * p:2 * p + 2], axis=0).astype(BF16)
                return gs[-1], vs[-1]

            lax.fori_loop(0, tl // FFN_STRIP, strip,
                          (jnp.where(i > 0, gp_ref[:, ls], 0.0), jnp.where(i > 0, vp_ref[:, ls], 0.0)))

    wspec = lambda off: pl.BlockSpec((SUBLANES, tc), lambda j, i, off=off: (0, j + off))
    bspec = lambda off: pl.BlockSpec((1, tc), lambda j, i, off=off: (0, j + off))
    return _pc(body, name=name, out_shape=_sds((n, D_FF), BF16), grid=(ncb, n // tl),
               in_specs=[spec(tl, cur, 0), spec(hb, prv, 0), spec(tl, cur, ncb), spec(hb, prv, ncb),
                         wspec(0), wspec(ncb), bspec(0), bspec(ncb)],
               out_specs=pl.BlockSpec((tl, tc), lambda j, i: (i, j)), compiler_params=_params(2))(
        up, up, up, up, conv_w, conv_w, conv_b, conv_b)


def _ffn_mid_bwd(da, up, conv_w, conv_b, name):
    n = up.shape[0]
    tl = _tile(n, STRIP_ROW_TILE)
    nb = n // tl
    nt = tl // FFN_STRIP_BWD
    hb = CONV_HALO
    tc = FFN_COLS
    ncb = D_FF // tc
    r = tl // hb
    last = n // hb - 1
    cur = lambda i: i
    prv = lambda i: jnp.maximum(i * r - 1, 0)
    nxt = lambda i: jnp.minimum((i + 1) * r, last)

    def spec(shape_rows, idx_fn, off):
        return pl.BlockSpec((shape_rows, tc), lambda j, i, off=off: (idx_fn(i), j + off))

    def body(da_ref, dan_ref, g_ref, gp_ref, gn_ref, v_ref, vp_ref, vn_ref, wg_ref, wv_ref, bg_ref, bv_ref,
             dg_ref, dv_ref, dwg_ref, dwv_ref, dbg_ref, dbv_ref, acc_w, acc_b):
        i = pl.program_id(1)

        @pl.when(i == 0)
        def _():
            acc_w[...] = jnp.zeros_like(acc_w)
            acc_b[...] = jnp.zeros_like(acc_b)

        row = lax.broadcasted_iota(jnp.int32, (SUBLANES, LANES), 0)
        full = lambda ref, k, ls: jnp.broadcast_to(ref[k:k + 1, ls], (SUBLANES, LANES))
        zero = jnp.zeros((SUBLANES, LANES), F32)

        def d_conv_out(da_, ug, uv):
            s = _sig(ug)
            return da_ * uv * (s * (1.0 + ug * (1.0 - s))), da_ * (ug * s)

        for lc in range(tc // LANES):
            ls = slice(lc * LANES, (lc + 1) * LANES)
            wg = [full(wg_ref, k, ls) for k in range(3)]
            wv = [full(wv_ref, k, ls) for k in range(3)]
            bg, bv = full(bg_ref, 0, ls), full(bv_ref, 0, ls)

            def unit(prev_g, g, prev_v, v, da_):
                g1, g2 = _down(prev_g, g, 1, row), _down(prev_g, g, 2, row)
                v1, v2 = _down(prev_v, v, 1, row), _down(prev_v, v, 2, row)
                ug = bg + wg[0] * g2 + wg[1] * g1 + wg[2] * g
                uv = bv + wv[0] * v2 + wv[1] * v1 + wv[2] * v
                dug, duv = d_conv_out(da_, ug, uv)
                return dug, duv, (g2, g1, g), (v2, v1, v)

            def d_in(d, d_next, w):
                return w[2] * d + w[1] * _up(d, d_next, 1, row) + w[0] * _up(d, d_next, 2, row)

            tail = pl.ds(tl - SUBLANES, SUBLANES)
            dgn, dvn, _, _ = unit(g_ref[tail, ls], gn_ref[:, ls], v_ref[tail, ls], vn_ref[:, ls], dan_ref[:, ls])
            dgn = jnp.where(i < nb - 1, dgn, 0.0)
            dvn = jnp.where(i < nb - 1, dvn, 0.0)
            gp0 = jnp.where(i > 0, gp_ref[:, ls], 0.0)
            vp0 = jnp.where(i > 0, vp_ref[:, ls], 0.0)

            def strip(tt, carry):
                dgn, dvn = carry[0], carry[1]
                aw, ab = list(carry[2:8]), list(carry[8:10])
                t = nt - 1 - tt
                nu = FFN_STRIP_BWD // SUBLANES
                rm = pl.multiple_of(jnp.maximum(t * FFN_STRIP_BWD - SUBLANES, 0), SUBLANES)
                gs = [jnp.where(t > 0, g_ref[pl.ds(rm, SUBLANES), ls], gp0)]
                vs = [jnp.where(t > 0, v_ref[pl.ds(rm, SUBLANES), ls], vp0)]
                das = []
                for u in range(nu):
                    rows = pl.ds(pl.multiple_of(t * FFN_STRIP_BWD + u * SUBLANES, SUBLANES), SUBLANES)
                    gs.append(g_ref[rows, ls])
                    vs.append(v_ref[rows, ls])
                    das.append(da_ref[rows, ls])
                dgs, dvs = [None] * nu + [dgn], [None] * nu + [dvn]
                for u in reversed(range(nu)):
                    dgs[u], dvs[u], gsh, vsh = unit(gs[u], gs[u + 1], vs[u], vs[u + 1], das[u])
                    for k in range(3):
                        aw[k] = aw[k] + dgs[u] * gsh[k]
                        aw[3 + k] = aw[3 + k] + dvs[u] * vsh[k]
                    ab[0] = ab[0] + dgs[u]
                    ab[1] = ab[1] + dvs[u]
                for p in range(nu // 2):
                    rows = pl.ds(pl.multiple_of(t * FFN_STRIP_BWD + p * 16, 16), 16)
                    dg_ref[rows, ls] = jnp.concatenate([d_in(dgs[2 * p], dgs[2 * p + 1], wg),
                                                        d_in(dgs[2 * p + 1], dgs[2 * p + 2], wg)], axis=0).astype(BF16)
                    dv_ref[rows, ls] = jnp.concatenate([d_in(dvs[2 * p], dvs[2 * p + 1], wv),
                                                        d_in(dvs[2 * p + 1], dvs[2 * p + 2], wv)], axis=0).astype(BF16)
                return (dgs[0], dvs[0], *aw, *ab)

            res = lax.fori_loop(0, nt, strip, (dgn, dvn) + (zero,) * 8)
            for k in range(3):
                acc_w[0, k, :, ls] += res[2 + k]
                acc_w[1, k, :, ls] += res[5 + k]
            acc_b[0, :, ls] += res[8]
            acc_b[1, :, ls] += res[9]

        @pl.when(i == nb - 1)
        def _():
            for t, (dw_ref, db_ref) in enumerate(((dwg_ref, dbg_ref), (dwv_ref, dbv_ref))):
                dw_ref[...] = jnp.zeros_like(dw_ref)
                for k in range(3):
                    dw_ref[k:k + 1, :] = jnp.sum(acc_w[t, k], axis=0, keepdims=True)
                db_ref[...] = jnp.sum(acc_b[t], axis=0, keepdims=True)

    wspec = lambda off: pl.BlockSpec((SUBLANES, tc), lambda j, i, off=off: (0, j + off))
    bspec = lambda off: pl.BlockSpec((1, tc), lambda j, i, off=off: (0, j + off))
    ospec = lambda off: pl.BlockSpec((tl, tc), lambda j, i, off=off: (i, j + off))
    dg, dv, dwg, dwv, dbg, dbv = _pc(
        body, name=name,
        out_shape=(_sds((n, D_FF), BF16), _sds((n, D_FF), BF16), _sds((SUBLANES, D_FF)), _sds((SUBLANES, D_FF)),
                   _sds((1, D_FF)), _sds((1, D_FF))),
        grid=(ncb, nb),
        in_specs=[spec(tl, cur, 0), spec(hb, nxt, 0),
                  spec(tl, cur, 0), spec(hb, prv, 0), spec(hb, nxt, 0),
                  spec(tl, cur, ncb), spec(hb, prv, ncb), spec(hb, nxt, ncb),
                  wspec(0), wspec(ncb), bspec(0), bspec(ncb)],
        out_specs=(ospec(0), ospec(0), wspec(0), wspec(0), bspec(0), bspec(0)),
        scratch_shapes=[pltpu.VMEM((2, 3, SUBLANES, tc), F32), pltpu.VMEM((2, SUBLANES, tc), F32)],
        compiler_params=_params(2))(da, da, up, up, up, up, up, up, conv_w, conv_w, conv_b, conv_b)
    return dg, dv, jnp.concatenate([dwg, dwv], axis=1), jnp.concatenate([dbg, dbv], axis=1)


def _position():
    return lax.axis_index("x"), lax.axis_index("y"), lax.axis_index("c")


def _all_gather(locals_, name):
    n = len(locals_)

    def body(*refs):
        x_refs, out_refs = refs[:n], refs[n:2 * n]
        send_sems, recv_sems, local_sems = refs[2 * n:]
        x, y, cc = _position()
        me, sibling = (x, y, cc), (x, y, 1 - cc)
        chips = [(1 - x, y), (x, 1 - y), (1 - x, 1 - y)]

        def slot(a, px, py, pc):
            return out_refs[a].at[4 * px + 2 * py + pc]

        def copy(k, a, block, to, own=False):
            return pltpu.make_async_remote_copy(
                src_ref=x_refs[a] if own else slot(a, *block), dst_ref=slot(a, *block),
                send_sem=send_sems.at[k, a], recv_sem=recv_sems.at[k, a], device_id=to, device_id_type=MESH)

        mine = [pltpu.make_async_copy(x_refs[a], slot(a, *me), local_sems.at[a]) for a in range(n)]
        first = [copy(1 + j, a, me, (*chip, cc), own=True) for j, chip in enumerate(chips) for a in range(n)]
        first += [copy(0, a, me, sibling, own=True) for a in range(n)]
        for cp in mine + first:
            cp.start()
        passed = []
        for j, chip in enumerate(chips):
            for a in range(n):
                copy(1 + j, a, (*chip, cc), me).wait_recv()
                cp = copy(4 + j, a, (*chip, cc), sibling)
                cp.start()
                passed.append(cp)
        for a in range(n):
            copy(0, a, sibling, me).wait_recv()
        for j, chip in enumerate(chips):
            for a in range(n):
                copy(4 + j, a, (*chip, 1 - cc), me).wait_recv()
        for cp in first + passed:
            cp.wait_send()
        for cp in mine:
            cp.wait()

    hbm = pl.BlockSpec(memory_space=pl.ANY)
    return _pc(body, name=name, out_shape=[_sds((N_DEV,) + a.shape, a.dtype) for a in locals_],
               in_specs=[hbm] * n, out_specs=[hbm] * n,
               scratch_shapes=[pltpu.SemaphoreType.DMA((7, n)), pltpu.SemaphoreType.DMA((7, n)),
                               pltpu.SemaphoreType.DMA((n,))])(*locals_)


def _peers():
    x, y, cc = _position()
    others = []
    for fx, fy, fc in ((0, 0, 1), (1, 0, 0), (0, 1, 0), (1, 1, 0), (1, 0, 1), (0, 1, 1), (1, 1, 1)):
        p = (1 - x if fx else x, 1 - y if fy else y, 1 - cc if fc else cc)
        others.append((p, 4 * p[0] + 2 * p[1] + p[2]))
    return 4 * x + 2 * y + cc, others


def _gather_start(locals_, after, name):
    n = len(locals_)
    me, _ = _peers()
    lands = [lax.dynamic_update_slice(lax.empty((N_DEV,) + a.shape, a.dtype), a[None], (me,) + (0,) * a.ndim)
             for a in locals_]

    def body(*refs):
        x_refs, land_refs = refs[:n], refs[n:2 * n]
        send_sems, recv_sems, token = refs[2 * n + 1], refs[2 * n + 2], refs[-1]
        me_idx, others = _peers()
        for k, (peer, _) in enumerate(others):
            for a in range(n):
                pltpu.make_async_remote_copy(
                    src_ref=x_refs[a], dst_ref=land_refs[a].at[me_idx], send_sem=send_sems.at[k * n + a],
                    recv_sem=recv_sems.at[k * n + a], device_id=peer, device_id_type=MESH).start()
        token[...] = jnp.zeros_like(token)

    hbm = pl.BlockSpec(memory_space=pltpu.HBM)
    sem = pl.BlockSpec(memory_space=pltpu.SEMAPHORE)
    out = _pc(body, name=name,
              out_shape=(pltpu.SemaphoreType.DMA((7 * n,)), pltpu.SemaphoreType.DMA((7 * n,)),
                         *[pltpu.HBM(a.shape, a.dtype) for a in locals_], *[pltpu.HBM(l.shape, l.dtype) for l in lands],
                         _sds((SUBLANES, LANES))),
              in_specs=[hbm] * (2 * n) + [pl.BlockSpec(memory_space=pl.ANY)],
              out_specs=(sem, sem, *([hbm] * (2 * n)), pl.BlockSpec(memory_space=pltpu.VMEM)),
              input_output_aliases={i: 2 + i for i in range(2 * n)},
              compiler_params=pltpu.CompilerParams(has_side_effects=pltpu.SideEffectType.DATAFLOW_SIDE_EFFECTING))(
        *[pltpu.with_memory_space_constraint(a, pltpu.HBM) for a in locals_],
        *[pltpu.with_memory_space_constraint(l, pltpu.HBM) for l in lands], after)
    return (out[0], out[1], list(out[2:2 + n]), list(out[2 + n:2 + 2 * n])), out[-1]


def _gather_wait(state, after, name):
    send_sems, recv_sems, x_thru, land_thru = state
    n = len(x_thru)

    def body(*refs):
        x_refs, land_refs = refs[:n], refs[n:2 * n]
        send_sems, recv_sems = refs[2 * n], refs[2 * n + 1]
        _, others = _peers()
        for k, (peer, peer_idx) in enumerate(others):
            for a in range(n):
                cp = pltpu.make_async_remote_copy(
                    src_ref=x_refs[a], dst_ref=land_refs[a].at[peer_idx], send_sem=send_sems.at[k * n + a],
                    recv_sem=recv_sems.at[k * n + a], device_id=peer, device_id_type=MESH)
                cp.wait_send()
                cp.wait_recv()

    hbm = pl.BlockSpec(memory_space=pltpu.HBM)
    sem = pl.BlockSpec(memory_space=pltpu.SEMAPHORE)
    out = _pc(body, name=name, out_shape=tuple(pltpu.HBM(a.shape, a.dtype) for a in x_thru + land_thru),
              in_specs=[hbm] * (2 * n) + [sem, sem, pl.BlockSpec(memory_space=pl.ANY)], out_specs=tuple([hbm] * (2 * n)),
              input_output_aliases={i: i for i in range(2 * n)},
              compiler_params=pltpu.CompilerParams(has_side_effects=pltpu.SideEffectType.DATAFLOW_SIDE_EFFECTING))(
        *x_thru, *land_thru, send_sems, recv_sems, after)
    return list(out[n:])


N_CHIPS = 4


def _sibling_swap(parts, name):
    n = len(parts)

    def body(*refs):
        g_refs, got_refs = refs[:n], refs[n:2 * n]
        send_sems, recv_sems = refs[2 * n:]
        x, y, cc = _position()
        swaps = []
        for q in range(N_CHIPS):
            for a in range(n):
                swaps.append(pltpu.make_async_remote_copy(
                    src_ref=g_refs[a].at[2 * q + 1 - cc], dst_ref=got_refs[a].at[q], send_sem=send_sems.at[q, a],
                    recv_sem=recv_sems.at[q, a], device_id=(x, y, 1 - cc), device_id_type=MESH))
        for cp in swaps:
            cp.start()
        for cp in swaps:
            cp.wait_recv()
        for cp in swaps:
            cp.wait_send()

    hbm = pl.BlockSpec(memory_space=pl.ANY)
    return _pc(body, name=name, out_shape=[_sds((N_CHIPS,) + a.shape[1:], a.dtype) for a in parts],
               in_specs=[hbm] * n, out_specs=[hbm] * n,
               scratch_shapes=[pltpu.SemaphoreType.DMA((N_CHIPS, n)), pltpu.SemaphoreType.DMA((N_CHIPS, n))])(*parts)


def _pair_add(part, got, core, name):
    q, a, b = got.shape
    ta = _block_rows(a, b)

    def body(core_ref, k_ref, g_ref, o_ref):
        o_ref[...] = (k_ref[...].astype(F32) + g_ref[...].astype(F32)).astype(BF16)

    spec = pl.BlockSpec((None, ta, b), lambda c, i, core_ref: (c, i, 0))
    own = pl.BlockSpec((None, None, ta, b), lambda c, i, core_ref: (c, core_ref[0], i, 0))
    grid_spec = pltpu.PrefetchScalarGridSpec(num_scalar_prefetch=1, grid=(q, a // ta), in_specs=[own, spec],
                                             out_specs=spec)
    return _pc(body, name=name, out_shape=_sds(got.shape, BF16), grid_spec=grid_spec, compiler_params=_params(2))(
        core, part.reshape((N_CHIPS, 2) + part.shape[1:]), got)


def _chip_exchange(sums, name):
    n = len(sums)

    def body(*refs):
        g_refs, out_refs = refs[:n], refs[n:2 * n]
        send_sems, recv_sems = refs[2 * n:]
        me_q, others = _chip_peers()
        sends, recvs = [], []
        for k, (peer, peer_q) in enumerate(others):
            for a in range(n):
                sends.append(pltpu.make_async_remote_copy(
                    src_ref=g_refs[a].at[peer_q], dst_ref=out_refs[a].at[me_q], send_sem=send_sems.at[k, a],
                    recv_sem=recv_sems.at[k, a], device_id=peer, device_id_type=MESH))
                recvs.append(pltpu.make_async_remote_copy(
                    src_ref=g_refs[a].at[me_q], dst_ref=out_refs[a].at[peer_q], send_sem=send_sems.at[k, a],
                    recv_sem=recv_sems.at[k, a], device_id=peer, device_id_type=MESH))
        for cp in sends:
            cp.start()
        for cp in recvs:
            cp.wait_recv()
        for cp in sends:
            cp.wait_send()

    hbm = pl.BlockSpec(memory_space=pl.ANY)
    return _pc(body, name=name, out_shape=[_sds(a.shape, a.dtype) for a in sums],
               in_specs=[hbm] * n, out_specs=[hbm] * n,
               scratch_shapes=[pltpu.SemaphoreType.DMA((3, n)), pltpu.SemaphoreType.DMA((3, n))])(*sums)


def _chip_peers():
    x, y, cc = _position()
    others = []
    for fx, fy in ((1, 0), (0, 1), (1, 1)):
        px, py = (1 - x if fx else x), (1 - y if fy else y)
        others.append(((px, py, cc), 2 * px + py))
    return 2 * x + y, others


def _chip_exchange_start(sums, name):
    n = len(sums)
    lands = [lax.empty(a.shape, a.dtype) for a in sums]

    def body(*refs):
        g_refs, land_refs = refs[:n], refs[n:2 * n]
        send_sems, recv_sems, token = refs[2 * n], refs[2 * n + 1], refs[-1]
        me_q, others = _chip_peers()
        for k, (peer, peer_q) in enumerate(others):
            for a in range(n):
                pltpu.make_async_remote_copy(
                    src_ref=g_refs[a].at[peer_q], dst_ref=land_refs[a].at[me_q], send_sem=send_sems.at[k * n + a],
                    recv_sem=recv_sems.at[k * n + a], device_id=peer, device_id_type=MESH).start()
        token[...] = jnp.zeros_like(token)

    hbm = pl.BlockSpec(memory_space=pltpu.HBM)
    sem = pl.BlockSpec(memory_space=pltpu.SEMAPHORE)
    out = _pc(body, name=name,
              out_shape=(pltpu.SemaphoreType.DMA((3 * n,)), pltpu.SemaphoreType.DMA((3 * n,)),
                         *[pltpu.HBM(a.shape, a.dtype) for a in sums], *[pltpu.HBM(a.shape, a.dtype) for a in sums],
                         _sds((SUBLANES, LANES))),
              in_specs=[hbm] * (2 * n), out_specs=(sem, sem, *([hbm] * (2 * n)), pl.BlockSpec(memory_space=pltpu.VMEM)),
              input_output_aliases={i: 2 + i for i in range(2 * n)},
              compiler_params=pltpu.CompilerParams(has_side_effects=pltpu.SideEffectType.DATAFLOW_SIDE_EFFECTING))(
        *[pltpu.with_memory_space_constraint(a, pltpu.HBM) for a in sums],
        *[pltpu.with_memory_space_constraint(l, pltpu.HBM) for l in lands])
    return (out[0], out[1], list(out[2:2 + n]), list(out[2 + n:2 + 2 * n])), out[-1]


def _chip_exchange_wait(state, after, name):
    send_sems, recv_sems, g_thru, land_thru = state
    n = len(g_thru)

    def body(*refs):
        g_refs, land_refs = refs[:n], refs[n:2 * n]
        send_sems, recv_sems = refs[2 * n], refs[2 * n + 1]
        me_q, others = _chip_peers()
        for k, (peer, peer_q) in enumerate(others):
            for a in range(n):
                cp = pltpu.make_async_remote_copy(
                    src_ref=g_refs[a].at[me_q], dst_ref=land_refs[a].at[peer_q], send_sem=send_sems.at[k * n + a],
                    recv_sem=recv_sems.at[k * n + a], device_id=peer, device_id_type=MESH)
                cp.wait_send()
                cp.wait_recv()

    hbm = pl.BlockSpec(memory_space=pltpu.HBM)
    sem = pl.BlockSpec(memory_space=pltpu.SEMAPHORE)
    out = _pc(body, name=name, out_shape=tuple(pltpu.HBM(a.shape, a.dtype) for a in g_thru + land_thru),
              in_specs=[hbm] * (2 * n) + [sem, sem, pl.BlockSpec(memory_space=pl.ANY)], out_specs=tuple([hbm] * (2 * n)),
              input_output_aliases={i: i for i in range(2 * n)},
              compiler_params=pltpu.CompilerParams(has_side_effects=pltpu.SideEffectType.DATAFLOW_SIDE_EFFECTING))(
        *g_thru, *land_thru, send_sems, recv_sems, after)
    return list(out[:n]), list(out[n:])


def _block_rows(a, b):
    ta = a
    while ta * b > 256 * 1024 and ta % 32 == 0:
        ta //= 2
    return ta


def _reduce_adamw(parts, w, m, v, name):
    n_parts, s, a, b = parts.shape
    ta = _block_rows(a, b)

    def body(p_ref, w_ref, m_ref, v_ref, g_out, d_out, m_out, v_out):
        g = p_ref[0].astype(F32)
        for j in range(1, n_parts):
            g = g + p_ref[j].astype(F32)
        delta, m_new, v_new = _adamw(g, w_ref[...], m_ref[...], v_ref[...])
        g_out[...] = g
        d_out[...] = delta
        m_out[...] = m_new
        v_out[...] = v_new

    spec = pl.BlockSpec((None, ta, b), lambda l, i: (l, i, 0))
    return _pc(body, name=name, out_shape=(_sds((s, a, b)),) * 4, grid=(s, a // ta),
               in_specs=[pl.BlockSpec((n_parts, None, ta, b), lambda l, i: (0, l, i, 0)), spec, spec, spec],
               out_specs=(spec,) * 4, compiler_params=_params(2))(parts, w, m, v)


def _adamw(g, w, m, v):
    c1 = 1.0 - ADAM_B1 ** ADAM_STEP
    c2 = 1.0 - ADAM_B2 ** ADAM_STEP
    m_new = ADAM_B1 * m + (1.0 - ADAM_B1) * g
    v_new = ADAM_B2 * v + (1.0 - ADAM_B2) * (g * g)
    delta = -ADAM_LR * ((m_new / c1) / (jnp.sqrt(v_new / c2) + ADAM_EPS) + ADAM_WD * w)
    return delta, m_new, v_new


def _chip_reduce_adamw(own, recv, w, m, v, chip, name):
    s, a, b = w.shape
    ta = _block_rows(a, b)

    def body(chip_ref, *refs):
        p_refs, (w_ref, m_ref, v_ref), (g_out, d_out, m_out, v_out) = refs[:4 * s], refs[4 * s:4 * s + 3], refs[4 * s + 3:]
        layer = pl.program_id(0)
        g = None
        for l in range(s):
            gl = p_refs[4 * l][...].astype(F32)
            for j in range(1, N_CHIPS):
                gl = gl + p_refs[4 * l + j][...].astype(F32)
            g = gl if g is None else jnp.where(layer == l, gl, g)
        delta, m_new, v_new = _adamw(g, w_ref[...], m_ref[...], v_ref[...])
        g_out[...] = g
        d_out[...] = delta
        m_out[...] = m_new
        v_out[...] = v_new

    def part_spec(l, j):
        return pl.BlockSpec((None, ta, b), lambda layer, i, chip_ref, l=l, j=j: (
            (chip_ref[0] + j) % N_CHIPS, jnp.where(layer == l, i, 0), 0))

    spec = pl.BlockSpec((None, ta, b), lambda layer, i, chip_ref: (layer, i, 0))
    in_specs, args = [], []
    for l in range(s):
        for j in range(N_CHIPS):
            in_specs.append(part_spec(l, j))
            args.append(own[l] if j == 0 else recv[l])
    grid_spec = pltpu.PrefetchScalarGridSpec(num_scalar_prefetch=1, grid=(s, a // ta), in_specs=in_specs + [spec] * 3,
                                             out_specs=(spec,) * 4)
    return _pc(body, name=name, out_shape=(_sds((s, a, b)),) * 4, grid_spec=grid_spec, compiler_params=_params(2))(
        chip, *args, w, m, v)


MATRICES = (("ada_mix_w", 2), ("w_in", 2), ("w_ssd_out", 1), ("w_conf_out", 2), ("w_sc_out", 2), ("w_o", 1),
            ("ada_ffn_w", 2), ("w_up", 2), ("w_down", 1))
MIXER_MATRICES = ("ada_mix_w", "w_in", "w_ssd_out", "w_conf_out", "w_sc_out", "w_o")
FFN_MATRICES = ("ada_ffn_w", "w_up", "w_down")
CONV_WEIGHTS = (("ssd_conv_w", 2), ("conf_conv_w", 2), ("sc_conv_w", 2), ("ffn_conv_w", 2))
SHARDED = MATRICES + CONV_WEIGHTS
REPLICATED = ("ada_mix_b", "norm_mix_g", "b_gate", "ssd_conv_b", "ssd_dt_bias", "ssd_a_log", "ssd_d", "ssd_norm_g",
              "conf_conv_b", "conf_ln_g", "conf_ln_b", "ada_ffn_b", "norm_ffn_g", "ffn_conv_b", "final_norm_g")
WEIGHT_NAMES = ("ada_mix_w", "ada_mix_b", "norm_mix_g", "w_in", "b_gate", "ssd_conv_w", "ssd_conv_b", "ssd_dt_bias",
                "ssd_a_log", "ssd_d", "ssd_norm_g", "w_ssd_out", "conf_conv_w", "conf_conv_b", "conf_ln_g",
                "conf_ln_b", "w_conf_out", "sc_conv_w", "w_sc_out", "w_o", "ada_ffn_w", "ada_ffn_b", "norm_ffn_g",
                "w_up", "ffn_conv_w", "ffn_conv_b", "w_down", "final_norm_g")


def _pack_flat(arrays, cols, row_multiple, dtype):
    flat = jnp.concatenate([a.reshape(-1).astype(dtype) for a in arrays])
    rows = -(-flat.shape[0] // cols)
    rows = -(-rows // row_multiple) * row_multiple
    return jnp.pad(flat, (0, rows * cols - flat.shape[0])).reshape(rows, cols)


def _unpack_flat(flat2d, shapes):
    flat = flat2d.reshape(-1)
    out, off = [], 0
    for s in shapes:
        n = 1
        for d in s:
            n *= d
        out.append(flat[off:off + n].reshape(s))
        off += n
    return out


def _cols(g, lo, hi):
    b = g.shape[-1]
    pieces = []
    for k in range(N_DEV):
        a, e = max(lo, k * b), min(hi, (k + 1) * b)
        if a < e:
            pieces.append(g[k, :, a - k * b:e - k * b])
    return pieces[0] if len(pieces) == 1 else jnp.concatenate(pieces, axis=1)


def _rows(g):
    return g.reshape(N_DEV * g.shape[1], g.shape[2])


def _col_shards(segs, b):
    shards = []
    for k in range(N_DEV):
        lo, hi = k * b, (k + 1) * b
        pieces, off = [], 0
        for seg in segs:
            n = seg.shape[1]
            a, e = max(lo, off), min(hi, off + n)
            if a < e:
                pieces.append(seg[:, a - off:e - off])
            off += n
        shards.append(pieces[0] if len(pieces) == 1 else jnp.concatenate(pieces, axis=1))
    return jnp.stack(shards)


def _row_shards(full):
    return full.reshape(N_DEV, full.shape[0] // N_DEV, full.shape[1])


def _pad_rows(a, rows):
    return jnp.pad(a, ((0, rows - a.shape[0]), (0, 0)))


def _pad_lanes(a):
    return jnp.pad(a, ((0, 0), (0, LANES - a.shape[1])))


def _whole(g):
    return _cols(g, 0, N_DEV * g.shape[-1])


def _mixer_weights(full, i):
    row = lambda name: full[name][i].reshape(1, -1)
    conv = lambda name: _whole(full[name][:, i])
    w_in = full["w_in", i]
    return {
        "ada_mix_w": _whole(full["ada_mix_w", i]), "ada_mix_b": row("ada_mix_b"), "norm_mix_g": row("norm_mix_g"),
        "w_z": _cols(w_in, 0, OFF_Z), "w_xbc": _cols(w_in, OFF_Z, OFF_XBC),
        "w_dt": _pad_lanes(_cols(w_in, OFF_XBC, OFF_DT)), "w_conf": _cols(w_in, OFF_DT, OFF_CONF),
        "w_sc": _cols(w_in, OFF_CONF, OFF_SC), "w_gates": _cols(w_in, OFF_SC, N_IN),
        "b_gate": row("b_gate"),
        "ssd_conv_w": _pad_rows(conv("ssd_conv_w"), SUBLANES), "ssd_conv_b": row("ssd_conv_b"),
        "dt_bias": _pad_lanes(row("ssd_dt_bias")), "a_log": _pad_lanes(row("ssd_a_log")),
        "ssd_d": _pad_lanes(row("ssd_d")), "ssd_norm_g": row("ssd_norm_g"), "w_ssd_out": _rows(full["w_ssd_out", i]),
        "conf_conv_w": _pad_rows(conv("conf_conv_w"), CONF_HALO), "conf_conv_b": row("conf_conv_b"),
        "conf_ln_g": row("conf_ln_g"), "conf_ln_b": row("conf_ln_b"), "w_conf_out": _whole(full["w_conf_out", i]),
        "sc_conv_w": _pad_rows(conv("sc_conv_w"), SUBLANES), "w_sc_out": _whole(full["w_sc_out", i]),
        "w_o": _rows(full["w_o", i]),
    }


def _ffn_weights(full, i):
    row = lambda name: full[name][i].reshape(1, -1)
    return {
        "ada_ffn_w": _whole(full["ada_ffn_w", i]), "ada_ffn_b": row("ada_ffn_b"), "norm_ffn_g": row("norm_ffn_g"),
        "w_up": _whole(full["w_up", i]), "ffn_conv_w": _pad_rows(_whole(full["ffn_conv_w"][:, i]), SUBLANES),
        "ffn_conv_b": row("ffn_conv_b"), "w_down": _rows(full["w_down", i]),
    }


def _adaln(sc8, w, b, name):
    mod = _matmul(sc8, w, "nn", F32, name)[0:1, :] + b
    return mod[:, :D_MODEL], mod[:, D_MODEL:2 * D_MODEL], mod[:, 2 * D_MODEL:]


def _mixer_fwd(i, x, prev, sc8, wl):
    t = f"l{i}_"
    s = {}
    shift, scale, gate = _adaln(sc8, wl["ada_mix_w"], wl["ada_mix_b"], t + "ada_mix")
    if prev is None:
        s["x_in"] = x
        s["h"] = _prenorm_first(x, wl["norm_mix_g"], scale, shift, t + "norm_mix")
    else:
        s["x_in"], s["h"] = _prenorm_res(x, prev[0], prev[1], wl["norm_mix_g"], scale, shift, t + "norm_mix")
    s["scale_mix"], s["gate_mix"] = scale, gate
    h = s["h"]
    s["z"] = _matmul(h, wl["w_z"], "nn", F32, t + "in_z")
    s["xbc"] = _matmul(h, wl["w_xbc"], "nn", F32, t + "in_xbc")
    s["dt_raw"] = _matmul(h, wl["w_dt"], "nn", F32, t + "in_dt")
    s["conf"] = _matmul(h, wl["w_conf"], "nn", F32, t + "in_conf")
    s["sc"] = _matmul(h, wl["w_sc"], "nn", F32, t + "in_sc")
    s["gates"] = _matmul(h, wl["w_gates"], "nn", F32, t + "in_gates")
    s["pre"], s["dt"] = _ssd_pre(s["xbc"], s["dt_raw"], wl["ssd_conv_w"], wl["ssd_conv_b"], wl["dt_bias"],
                                 t + "ssd_pre")
    s["y"], s["hprev"] = _ssd_scan(s["pre"], s["dt"], wl["a_log"], wl["ssd_d"], t + "ssd_scan")
    s["ya_in"] = _ssd_post(s["y"], s["z"], wl["ssd_norm_g"], t + "ssd_post")
    s["yb_in"], s["uc"] = _conf_fwd(s["conf"], wl["conf_conv_w"], wl["conf_conv_b"], wl["conf_ln_g"],
                                    wl["conf_ln_b"], t + "conf")
    s["yc_in"] = _sc_fwd(s["sc"], wl["sc_conv_w"], t + "sconv")
    s["ya"] = _matmul(s["ya_in"], wl["w_ssd_out"], "nn", F32, t + "ssd_out")
    s["yb"] = _matmul(s["yb_in"], wl["w_conf_out"], "nn", F32, t + "conf_out")
    s["yc"] = _matmul(s["yc_in"], wl["w_sc_out"], "nn", F32, t + "sc_out")
    s["merged"] = _merge_fwd(s["gates"], s["ya"], s["yb"], s["yc"], wl["b_gate"], t + "merge")
    s["mix"] = _matmul(s["merged"], wl["w_o"], "nn", F32, t + "w_o")
    return s


def _ffn_fwd(i, s, sc8, wl):
    t = f"l{i}_"
    shift2, scale2, gate2 = _adaln(sc8, wl["ada_ffn_w"], wl["ada_ffn_b"], t + "ada_ffn")
    s["x_mid"], s["h2"] = _prenorm_res(s["x_in"], s["mix"], s["gate_mix"], wl["norm_ffn_g"], scale2, shift2,
                                       t + "norm_ffn")
    s["scale_ffn"], s["gate_ffn"] = scale2, gate2
    s["up"] = _matmul(s["h2"], wl["w_up"], "nn", F32, t + "w_up")
    s["a"] = _ffn_mid(s["up"], wl["ffn_conv_w"], wl["ffn_conv_b"], t + "ffn_mid")
    s["out"] = _matmul(s["a"], wl["w_down"], "nn", F32, t + "w_down")
    return s


def _layer_bwd(i, s, wl, sc8, dys_ffn, dx_after, dgate_ffn, prev, emit=None):
    t = f"l{i}_b_"
    g = {}
    da = _matmul(dys_ffn, wl["w_down"], "nt", F32, t + "d_a")
    g["w_down"] = _matmul(s["a"], dys_ffn, "tn", BF16, t + "dw_down")
    dug, duv, dfw, g["ffn_conv_b"] = _ffn_mid_bwd(da, s["up"], wl["ffn_conv_w"], wl["ffn_conv_b"], t + "ffn_mid")
    g["ffn_conv_w"] = dfw[:3]
    dh2 = _matmul_sum_nt([dug, duv], [wl["w_up"][:, :D_FF], wl["w_up"][:, D_FF:]], t + "d_h2")
    g["w_up"] = [_matmul(s["h2"], dug, "tn", BF16, t + "dw_up_g"), _matmul(s["h2"], duv, "tn", BF16, t + "dw_up_v")]
    dx_mid, dshift2, dscale2, g["norm_ffn_g"], dys_mix, dgate_mix = _norm_bwd(
        dh2, s["x_mid"], dx_after, wl["norm_ffn_g"], s["scale_ffn"], t + "norm_ffn", s["mix"], s["gate_mix"])
    dmod_ffn = jnp.concatenate([dshift2, dscale2, dgate_ffn], axis=1)
    g["ada_ffn_b"] = dmod_ffn
    g["ada_ffn_w"] = [_matmul(sc8, _pad_rows(dmod_ffn, SUBLANES), "tn", BF16, t + "dw_ada_ffn")]
    token = emit(i, "ffn", g) if emit is not None else None
    if token is not None:
        wl = {**wl, "b_gate": wl["b_gate"] + token}
    dmerged = _matmul(dys_mix, wl["w_o"], "nt", F32, t + "d_merged")
    g["w_o"] = _matmul(s["merged"], dys_mix, "tn", BF16, t + "dw_o")
    dya, dyb, dyc, dgates, g["b_gate"] = _merge_bwd(dmerged, s["gates"], s["ya"], s["yb"], s["yc"], wl["b_gate"],
                                                    t + "merge")
    dya_in = _matmul(dya, wl["w_ssd_out"], "nt", F32, t + "d_ya_in")
    g["w_ssd_out"] = _matmul(s["ya_in"], dya, "tn", BF16, t + "dw_ssd_out")
    dyb_in = _matmul(dyb, wl["w_conf_out"], "nt", F32, t + "d_yb_in")
    g["w_conf_out"] = [_matmul(s["yb_in"], dyb, "tn", BF16, t + "dw_conf_out")]
    dyc_in = _matmul(dyc, wl["w_sc_out"], "nt", F32, t + "d_yc_in")
    g["w_sc_out"] = [_matmul(s["yc_in"], dyc, "tn", BF16, t + "dw_sc_out")]
    dy, dz, g["ssd_norm_g"] = _ssd_post_bwd(dya_in, s["y"], s["z"], wl["ssd_norm_g"], t + "ssd_post")
    dpre, ddt, da_log, dd = _ssd_scan_bwd(s["pre"], s["dt"], s["hprev"], dy, wl["a_log"], wl["ssd_d"],
                                          t + "ssd_scan")
    g["ssd_a_log"], g["ssd_d"] = da_log[:, :SSD_HEADS], dd[:, :SSD_HEADS]
    dxbc, ddt_raw, dcw, g["ssd_conv_b"], ddtb = _ssd_pre_bwd(dpre, s["xbc"], ddt, s["dt_raw"], wl["ssd_conv_w"],
                                                             wl["dt_bias"], t + "ssd_pre")
    g["ssd_conv_w"], g["ssd_dt_bias"] = dcw[:4], ddtb[:, :SSD_HEADS]
    dconf, dccw, g["conf_conv_b"], g["conf_ln_g"], g["conf_ln_b"] = _conf_bwd(
        dyb_in, s["uc"], s["conf"], wl["conf_conv_w"], wl["conf_ln_g"], wl["conf_ln_b"], t + "conf")
    g["conf_conv_w"] = dccw[:CONF_KERNEL]
    dsc, dscw = _sc_bwd(dyc_in, s["sc"], wl["sc_conv_w"], t + "sconv")
    g["sc_conv_w"] = dscw[:3]
    segs = (("z", dz, "w_z"), ("xbc", dxbc, "w_xbc"), ("dt", ddt_raw, "w_dt"), ("conf", dconf, "w_conf"),
            ("sc", dsc, "w_sc"), ("gates", dgates, "w_gates"))
    dw_segs = []
    for nm, dseg, wname in segs:
        dw = _matmul(s["h"], dseg, "tn", BF16, t + "dw_in_" + nm)
        dw_segs.append(dw[:, :SSD_HEADS] if nm == "dt" else dw)
    g["w_in"] = dw_segs
    token = emit(i, "mixer_early", g) if emit is not None else None
    if token is not None:
        wl = {**wl, "w_dt": wl["w_dt"] + token.astype(BF16)}
    dh = _matmul_sum_nt([dseg for _, dseg, _ in segs], [wl[wname] for _, _, wname in segs], t + "d_h")
    if prev is None:
        dx_in, dshift, dscale, g["norm_mix_g"] = _norm_bwd(dh, s["x_in"], dx_mid, wl["norm_mix_g"], s["scale_mix"],
                                                          t + "norm_mix")
        back = None
    else:
        dx_in, dshift, dscale, g["norm_mix_g"], dys_prev, dgate_prev = _norm_bwd(
            dh, s["x_in"], dx_mid, wl["norm_mix_g"], s["scale_mix"], t + "norm_mix", prev[0], prev[1])
        back = (dys_prev, dgate_prev)
    dmod_mix = jnp.concatenate([dshift, dscale, dgate_mix], axis=1)
    g["ada_mix_b"] = dmod_mix
    g["ada_mix_w"] = [_matmul(sc8, _pad_rows(dmod_mix, SUBLANES), "tn", BF16, t + "dw_ada_mix")]
    return g, dx_in, back


def _device_step(x, c, target, full, fetch=None, emit=None):
    fetch = fetch or {}
    full = dict(full)
    sc8 = _pad_rows(c * (1.0 / (1.0 + jnp.exp(-c))), SUBLANES)
    wls, saved, prev, xcur = [], [], None, x
    for i in range(DEPTH):
        if (i, "mixer") in fetch:
            full.update(fetch[i, "mixer"](prev[0]))
        wl = _mixer_weights(full, i)
        s = _mixer_fwd(i, xcur, prev, sc8, wl)
        if (i, "ffn") in fetch:
            full.update(fetch[i, "ffn"](s["mix"]))
        wf = _ffn_weights(full, i)
        _ffn_fwd(i, s, sc8, wf)
        wls.append({**wl, **wf})
        saved.append(s)
        xcur, prev = s["x_mid"], (s["out"], s["gate_ffn"])
    gf = full["final_norm_g"].reshape(1, -1)
    last = saved[-1]
    loss, dx, dys, dgate, dgf = _final_loss(last["x_mid"], last["out"], last["gate_ffn"], gf, target, "final_loss")
    grads = [None] * DEPTH
    for i in reversed(range(DEPTH)):
        prev = None if i == 0 else (saved[i - 1]["out"], saved[i - 1]["gate_ffn"])
        grads[i], dx, back = _layer_bwd(i, saved[i], wls[i], sc8, dys, dx, dgate, prev, emit)
        token = emit(i, "mixer", grads[i]) if emit is not None else None
        if token is not None:
            wls[i - 1] = {**wls[i - 1], "ffn_conv_b": wls[i - 1]["ffn_conv_b"] + token}
        if back is not None:
            dys, dgate = back
    return loss[0, 0], dx, grads, dgf


def _step(x, c, target, weights, moments_m, moments_v):
    sharded_names = [n for n, _ in SHARDED]
    conv_names = [n for n, _ in CONV_WEIGHTS]
    shard = lambda key: weights[key[0]][key[1]].astype(BF16)
    first = [(n, 0) for n in MIXER_MATRICES]
    later = {(0, "ffn"): [(n, 0) for n in FFN_MATRICES], (1, "mixer"): [(n, 1) for n in MIXER_MATRICES + FFN_MATRICES]}
    gathered = _all_gather([shard(k) for k in first] + [weights[n] for n in conv_names], "gather_first")
    full = {n: weights[n] for n in REPLICATED}
    full.update(zip(first + conv_names, gathered))
    fetch, after = {}, gathered[0]
    for stage, keys in later.items():
        state, after = _gather_start([shard(k) for k in keys], after, f"gather_l{stage[0]}_{stage[1]}_start")
        fetch[stage] = functools.partial(
            lambda act, state, keys, nm: dict(zip(keys, _gather_wait(state, act, nm))),
            state=state, keys=keys, nm=f"gather_l{stage[0]}_{stage[1]}_wait")
    axis_of = dict(SHARDED)
    core = lax.axis_index("c").astype(jnp.int32).reshape(1)
    chip = (2 * lax.axis_index("x") + lax.axis_index("y")).astype(jnp.int32).reshape(1)
    ffn_names = list(FFN_MATRICES) + ["ffn_conv_w"]
    early_names = [n for n in sharded_names if n not in ffn_names and n != "ada_mix_w"]
    sums, received, pending = {}, {}, []

    def send(i, names, grads_i, last, tag):
        keys = [(n, i) for n in names]
        parts = []
        for n in names:
            gw = grads_i[n]
            part = _row_shards(gw) if axis_of[n] == 1 else _col_shards(gw if isinstance(gw, list) else [gw],
                                                                       weights[n].shape[-1])
            parts.append(part.astype(BF16))
        got = _sibling_swap(parts, "swap_grads_" + tag)
        pair = [_pair_add(p, g, core, f"pair_add_{n}_{i}") for n, p, g in zip(names, parts, got)]
        if last:
            sums.update(zip(keys, pair))
            received.update(zip(keys, _chip_exchange(pair, "exchange_grads_" + tag)))
            return None
        state, token = _chip_exchange_start(pair, "exchange_grads_" + tag + "_start")
        pending.append((keys, state, tag))
        return token[0:1, 0:1]

    def emit(i, kind, grads_i):
        if i == 0 and kind == "ffn":
            return send(0, ffn_names, grads_i, False, "l0_ffn")
        if i == 0 and kind == "mixer_early":
            return send(0, early_names, grads_i, False, "l0_mixer")
        if i == 0 and kind == "mixer":
            return send(0, ["ada_mix_w"], grads_i, True, "l0_ada")
        if kind == "mixer":
            return send(i, sharded_names, grads_i, False, f"l{i}_all")
        return None

    loss, grad_x, grads, dgf = _device_step(x[0], c + after[0:1, 0:1], target[0], full, fetch, emit)
    for keys, state, tag in pending:
        own, got = _chip_exchange_wait(state, grad_x, "exchange_grads_" + tag + "_wait")
        sums.update(zip(keys, own))
        received.update(zip(keys, got))
    big = {n: _chip_reduce_adamw([sums[n, i] for i in range(DEPTH)], [received[n, i] for i in range(DEPTH)],
                                 weights[n], moments_m[n], moments_v[n], chip, "adamw_" + n)
           for n in sharded_names}
    rep_grads = [dgf if n == "final_norm_g" else jnp.stack([grads[i][n].reshape(-1) for i in range(DEPTH)])
                 for n in REPLICATED]
    small_parts, = _all_gather([_pack_flat(rep_grads, LANES, SUBLANES, F32)], "gather_small_grads")
    pack_s = lambda d: _pack_flat([d[n] for n in REPLICATED], LANES, SUBLANES, F32)[None]
    small = _reduce_adamw(small_parts[:, None], pack_s(weights), pack_s(moments_m), pack_s(moments_v),
                          "adamw_replicated")
    small = [_unpack_flat(b, [weights[n].shape for n in REPLICATED]) for b in small]
    results = []
    for kind in range(4):
        by_name = {n: big[n][kind] for n in sharded_names}
        by_name.update(zip(REPLICATED, small[kind]))
        results.append([by_name[n] for n in WEIGHT_NAMES])
    loss = lax.psum(loss, ("x", "y", "c"))
    return (loss, grad_x[None], *results[0], *results[1], *results[2], *results[3])


def kernel(x, c, ada_mix_w, ada_mix_b, norm_mix_g, w_in, b_gate, ssd_conv_w, ssd_conv_b, ssd_dt_bias, ssd_a_log, ssd_d, ssd_norm_g, w_ssd_out, conf_conv_w, conf_conv_b, conf_ln_g, conf_ln_b, w_conf_out, sc_conv_w, w_sc_out, w_o, ada_ffn_w, ada_ffn_b, norm_ffn_g, w_up, ffn_conv_w, ffn_conv_b, w_down, final_norm_g, loss_target, m_ada_mix_w, m_ada_mix_b, m_norm_mix_g, m_w_in, m_b_gate, m_ssd_conv_w, m_ssd_conv_b, m_ssd_dt_bias, m_ssd_a_log, m_ssd_d, m_ssd_norm_g, m_w_ssd_out, m_conf_conv_w, m_conf_conv_b, m_conf_ln_g, m_conf_ln_b, m_w_conf_out, m_sc_conv_w, m_w_sc_out, m_w_o, m_ada_ffn_w, m_ada_ffn_b, m_norm_ffn_g, m_w_up, m_ffn_conv_w, m_ffn_conv_b, m_w_down, m_final_norm_g, v_ada_mix_w, v_ada_mix_b, v_norm_mix_g, v_w_in, v_b_gate, v_ssd_conv_w, v_ssd_conv_b, v_ssd_dt_bias, v_ssd_a_log, v_ssd_d, v_ssd_norm_g, v_w_ssd_out, v_conf_conv_w, v_conf_conv_b, v_conf_ln_g, v_conf_ln_b, v_w_conf_out, v_sc_conv_w, v_w_sc_out, v_w_o, v_ada_ffn_w, v_ada_ffn_b, v_norm_ffn_g, v_w_up, v_ffn_conv_w, v_ffn_conv_b, v_w_down, v_final_norm_g):
    given = dict(locals())
    weights = {n: given[n] for n in WEIGHT_NAMES}
    moments_m = {n: given["m_" + n] for n in WEIGHT_NAMES}
    moments_v = {n: given["v_" + n] for n in WEIGHT_NAMES}
    return _step(x, c, loss_target, weights, moments_m, moments_v)
```

```python
import functools

import jax
import jax.numpy as jnp
from jax import lax
from jax.experimental import pallas as pl
from jax.experimental.pallas import tpu as pltpu

F32 = jnp.float32
BF16 = jnp.bfloat16
MESH = pl.DeviceIdType.MESH

N_DEV = 8
DEPTH = 2
D_MODEL = 1024
SSD_HEADS = 16
SSD_HEAD_DIM = 64
SSD_INNER = 1024
SSD_STATE = 64
SSD_CHUNK = 128
SSD_XBC = 1280
CONF_WIDTH = 512
CONF_KERNEL = 31
SC_WIDTH = 512
D_FF = 2816
EPS = 1e-6
OFF_Z, OFF_XBC, OFF_DT, OFF_CONF, OFF_SC, N_IN = 1024, 2304, 2320, 3344, 4880, 7952

ADAM_LR, ADAM_B1, ADAM_B2, ADAM_EPS, ADAM_WD, ADAM_STEP = 0.001, 0.9, 0.999, 1e-08, 0.01, 10

LANES = 128
SUBLANES = 8
VMEM_LIMIT = 56 * 1024 * 1024
ROW_TILE = 256
NORM_ROW_TILE = 512

NN = (((1,), (0,)), ((), ()))
NT = (((1,), (1,)), ((), ()))
TN = (((0,), (0,)), ((), ()))


def _params(n_axes):
    return pltpu.CompilerParams(dimension_semantics=("arbitrary",) * n_axes, vmem_limit_bytes=VMEM_LIMIT)


def _pc(body, **kw):
    return pl.pallas_call(body, **kw)


def _dot(a, b, dn=NN, precision=None):
    return lax.dot_general(a, b, dn, precision=precision, preferred_element_type=F32)


def _split3(x):
    hi = x.astype(BF16)
    r1 = x - hi.astype(F32)
    mid = r1.astype(BF16)
    return hi, mid, (r1 - mid.astype(F32)).astype(BF16)


def _dot_sel(x, sel):
    hi, mid, lo = _split3(x)
    return _dot(hi, sel) + _dot(mid, sel) + _dot(lo, sel)


def _sel_dot(sel, x):
    hi, mid, lo = _split3(x)
    return _dot(sel, hi) + _dot(sel, mid) + _dot(sel, lo)


def _sig(x):
    return 1.0 / (1.0 + jnp.exp(-x))


def _fold(v):
    r, c = v.shape
    return v.reshape(r // SUBLANES, SUBLANES, c).sum(axis=0)


def _tile(n_rows, target=ROW_TILE):
    return min(target, n_rows // 2)


STRIP_UNITS = 8
STRIP_ROW_TILE = 512


def _strip_units_per_trip(tl):
    return min(STRIP_UNITS, tl // SUBLANES)


def _bcast_row(ref, k, ls):
    return jnp.broadcast_to(ref[k:k + 1, ls], (SUBLANES, LANES))


def _unit_rows(t, u, nu):
    return pl.ds(pl.multiple_of((t * nu + u) * SUBLANES, SUBLANES), SUBLANES)


def _pair_rows(t, p, nu):
    return pl.ds(pl.multiple_of((t * nu + 2 * p) * SUBLANES, 2 * SUBLANES), 2 * SUBLANES)


def _strip_units(ref, ls, t, nu, nt, halo_prev=None, halo_next=None):
    units = [ref[_unit_rows(t, u, nu), ls] for u in range(nu)]
    if halo_prev is not None:
        before = pl.ds(pl.multiple_of(jnp.maximum(t * nu - 1, 0) * SUBLANES, SUBLANES), SUBLANES)
        units.insert(0, jnp.where(t > 0, ref[before, ls], halo_prev))
    if halo_next is not None:
        after = pl.ds(pl.multiple_of(jnp.minimum((t + 1) * nu, nt * nu - 1) * SUBLANES, SUBLANES), SUBLANES)
        units.append(jnp.where(t < nt - 1, ref[after, ls], halo_next))
    return units


def _row(tl, c, col=0):
    return pl.BlockSpec((tl, c), lambda i, col=col: (i, col))


def _prev(tl, hb, c, col=0):
    r = tl // hb
    return pl.BlockSpec((hb, c), lambda i, col=col: (jnp.maximum(i * r - 1, 0), col))


def _next(tl, hb, c, n_rows, col=0):
    r = tl // hb
    last = n_rows // hb - 1
    return pl.BlockSpec((hb, c), lambda i, col=col: (jnp.minimum((i + 1) * r, last), col))


def _const(shape):
    return pl.BlockSpec(shape, lambda i: (0,) * len(shape))


def _sds(shape, dtype=F32):
    return jax.ShapeDtypeStruct(shape, dtype)


MM_TILE = 1536
MM_TILE_ROWS = 2048
MM_FULL_K = 3072
MM_K_TILE = 1024


def _pick(dim, target):
    if dim <= target:
        return dim
    best = None
    for t in range(LANES, target + 1, LANES):
        if dim % t == 0:
            best = t
    assert best is not None, (dim, target)
    return best


def _matmul(a, b, mode, out_dtype, name):
    if mode == "nn":
        (m, k), (k2, n) = a.shape, b.shape
    elif mode == "nt":
        (m, k), (n, k2) = a.shape, b.shape
    else:
        (k, m), (k2, n) = a.shape, b.shape
    assert k == k2, (a.shape, b.shape, mode)
    tn = _pick(n, MM_TILE)
    tk = k if k <= MM_FULL_K else _pick(k, MM_K_TILE)
    tm = _pick(m, MM_TILE_ROWS if tn <= 1024 and tk <= 1024 else 1024)
    nk = k // tk
    dn = {"nn": NN, "nt": NT, "tn": TN}[mode]

    def body_one(a_ref, b_ref, o_ref):
        o_ref[...] = _dot(a_ref[...].astype(BF16), b_ref[...].astype(BF16), dn).astype(out_dtype)

    def body_acc(a_ref, b_ref, o_ref, acc):
        kk = pl.program_id(2)

        @pl.when(kk == 0)
        def _():
            acc[...] = jnp.zeros_like(acc)

        acc[...] += _dot(a_ref[...].astype(BF16), b_ref[...].astype(BF16), dn)

        @pl.when(kk == nk - 1)
        def _():
            o_ref[...] = acc[...].astype(out_dtype)

    a_spec = {"nn": pl.BlockSpec((tm, tk), lambda i, j, kk: (i, kk)),
              "nt": pl.BlockSpec((tm, tk), lambda i, j, kk: (i, kk)),
              "tn": pl.BlockSpec((tk, tm), lambda i, j, kk: (kk, i))}[mode]
    b_spec = {"nn": pl.BlockSpec((tk, tn), lambda i, j, kk: (kk, j)),
              "nt": pl.BlockSpec((tn, tk), lambda i, j, kk: (j, kk)),
              "tn": pl.BlockSpec((tk, tn), lambda i, j, kk: (kk, j))}[mode]
    o_spec = pl.BlockSpec((tm, tn), lambda i, j, kk: (i, j))
    return _pc(body_one if nk == 1 else body_acc, name=name, out_shape=_sds((m, n), out_dtype),
               grid=(m // tm, n // tn, nk), in_specs=[a_spec, b_spec], out_specs=o_spec,
               scratch_shapes=[] if nk == 1 else [pltpu.VMEM((tm, tn), F32)], compiler_params=_params(3))(a, b)


SUM_NT_TILE = 512


def _matmul_sum_nt(a_list, b_list, name):
    m, n = a_list[0].shape[0], b_list[0].shape[0]
    cnt = len(a_list)
    tm, tn = _pick(m, SUM_NT_TILE), _pick(n, SUM_NT_TILE)

    def body(*refs):
        a_refs, b_refs, o_ref = refs[:cnt], refs[cnt:2 * cnt], refs[2 * cnt]
        acc = _dot(a_refs[0][...].astype(BF16), b_refs[0][...].astype(BF16), NT)
        for t in range(1, cnt):
            acc = acc + _dot(a_refs[t][...].astype(BF16), b_refs[t][...].astype(BF16), NT)
        o_ref[...] = acc

    in_specs = [pl.BlockSpec((tm, a.shape[1]), lambda j, i: (i, 0)) for a in a_list]
    in_specs += [pl.BlockSpec((tn, b.shape[1]), lambda j, i: (j, 0)) for b in b_list]
    return _pc(body, name=name, out_shape=_sds((m, n)), grid=(n // tn, m // tm), in_specs=in_specs,
               out_specs=pl.BlockSpec((tm, tn), lambda j, i: (i, j)), compiler_params=_params(2))(*a_list, *b_list)


def _norm_mod(x, g, scale, shift):
    r = lax.rsqrt(jnp.mean(x * x, axis=-1, keepdims=True) + EPS)
    return ((x * r) * g) * (1.0 + scale) + shift


def _prenorm_first(x, g, scale, shift, name):
    n, d = x.shape
    tl = _tile(n, NORM_ROW_TILE)

    def body(x_ref, g_ref, sc_ref, sh_ref, h_ref):
        h_ref[...] = _norm_mod(x_ref[...], g_ref[...], sc_ref[...], sh_ref[...]).astype(BF16)

    return _pc(body, name=name, out_shape=_sds((n, d), BF16), grid=(n // tl,),
               in_specs=[_row(tl, d)] + [_const((1, d))] * 3, out_specs=_row(tl, d),
               compiler_params=_params(1))(x, g, scale, shift)


def _prenorm_res(x, y, gate, g, scale, shift, name):
    n, d = x.shape
    tl = _tile(n, NORM_ROW_TILE)

    def body(x_ref, y_ref, gate_ref, g_ref, sc_ref, sh_ref, xo_ref, h_ref):
        xn = x_ref[...] + gate_ref[...] * y_ref[...]
        xo_ref[...] = xn
        h_ref[...] = _norm_mod(xn, g_ref[...], sc_ref[...], sh_ref[...]).astype(BF16)

    return _pc(body, name=name, out_shape=(_sds((n, d)), _sds((n, d), BF16)), grid=(n // tl,),
               in_specs=[_row(tl, d), _row(tl, d)] + [_const((1, d))] * 4,
               out_specs=(_row(tl, d), _row(tl, d)), compiler_params=_params(1))(x, y, gate, g, scale, shift)


def _final_loss(x, y, gate, gf, target, name):
    n, d = x.shape
    tl = _tile(n, NORM_ROW_TILE)
    nb = n // tl

    def body(x_ref, y_ref, gate_ref, gf_ref, t_ref, loss_ref, dx_ref, dys_ref, dgate_ref, dgf_ref,
             acc_l, acc_gate, acc_gf):
        i = pl.program_id(0)

        @pl.when(i == 0)
        def _():
            acc_l[...] = jnp.zeros_like(acc_l)
            acc_gate[...] = jnp.zeros_like(acc_gate)
            acc_gf[...] = jnp.zeros_like(acc_gf)

        yv = y_ref[...]
        gate = gate_ref[...]
        gf = gf_ref[...]
        x2 = x_ref[...] + gate * yv
        r = lax.rsqrt(jnp.mean(x2 * x2, axis=-1, keepdims=True) + EPS)
        xn = x2 * r
        e = xn * gf - t_ref[...]
        acc_l[...] += _fold(e * e)
        dy = e * (1.0 / d)
        acc_gf[...] += _fold(dy * xn)
        dxn = dy * gf
        dx = r * (dxn - xn * jnp.mean(dxn * xn, axis=-1, keepdims=True))
        dx_ref[...] = dx
        dys_ref[...] = (dx * gate).astype(BF16)
        acc_gate[...] += _fold(dx * yv)

        @pl.when(i == nb - 1)
        def _():
            loss_ref[...] = jnp.full((SUBLANES, LANES), 0.5 / d, F32) * jnp.sum(acc_l[...])
            dgate_ref[...] = jnp.sum(acc_gate[...], axis=0, keepdims=True)
            dgf_ref[...] = jnp.sum(acc_gf[...], axis=0, keepdims=True)

    return _pc(body, name=name,
               out_shape=(_sds((SUBLANES, LANES)), _sds((n, d)), _sds((n, d), BF16), _sds((1, d)), _sds((1, d))),
               grid=(nb,),
               in_specs=[_row(tl, d), _row(tl, d), _const((1, d)), _const((1, d)), _row(tl, d)],
               out_specs=(_const((SUBLANES, LANES)), _row(tl, d), _row(tl, d), _const((1, d)), _const((1, d))),
               scratch_shapes=[pltpu.VMEM((SUBLANES, d), F32)] * 3,
               compiler_params=_params(1))(x, y, gate, gf, target)


def _norm_bwd(dh, x, dxo, g, scale, name, y_prev=None, gate_prev=None):
    n, d = x.shape
    tl = _tile(n, NORM_ROW_TILE)
    nb = n // tl
    has_prev = y_prev is not None

    def body(*refs):
        if has_prev:
            (dh_ref, x_ref, dxo_ref, g_ref, sc_ref, yp_ref, gp_ref,
             dx_ref, dsh_ref, dsc_ref, dg_ref, dys_ref, dgp_ref, acc_sh, acc_s, acc_gp) = refs
        else:
            (dh_ref, x_ref, dxo_ref, g_ref, sc_ref,
             dx_ref, dsh_ref, dsc_ref, dg_ref, acc_sh, acc_s) = refs
        i = pl.program_id(0)

        @pl.when(i == 0)
        def _():
            acc_sh[...] = jnp.zeros_like(acc_sh)
            acc_s[...] = jnp.zeros_like(acc_s)
            if has_prev:
                acc_gp[...] = jnp.zeros_like(acc_gp)

        x_ = x_ref[...]
        dh_ = dh_ref[...]
        g_ = g_ref[...]
        one_sc = 1.0 + sc_ref[...]
        r = lax.rsqrt(jnp.mean(x_ * x_, axis=-1, keepdims=True) + EPS)
        xn = x_ * r
        dxn = dh_ * (g_ * one_sc)
        dx = dxo_ref[...] + r * (dxn - xn * jnp.mean(dxn * xn, axis=-1, keepdims=True))
        dx_ref[...] = dx
        acc_sh[...] += _fold(dh_)
        acc_s[...] += _fold(dh_ * xn)
        if has_prev:
            dys_ref[...] = (dx * gp_ref[...]).astype(BF16)
            acc_gp[...] += _fold(dx * yp_ref[...])

        @pl.when(i == nb - 1)
        def _():
            s = jnp.sum(acc_s[...], axis=0, keepdims=True)
            dsh_ref[...] = jnp.sum(acc_sh[...], axis=0, keepdims=True)
            dsc_ref[...] = s * g_
            dg_ref[...] = s * one_sc
            if has_prev:
                dgp_ref[...] = jnp.sum(acc_gp[...], axis=0, keepdims=True)

    vec = _sds((1, d))
    in_specs = [_row(tl, d)] * 3 + [_const((1, d))] * 2
    out_shape = [_sds((n, d)), vec, vec, vec]
    out_specs = [_row(tl, d)] + [_const((1, d))] * 3
    scratch = [pltpu.VMEM((SUBLANES, d), F32)] * 2
    args = [dh, x, dxo, g, scale]
    if has_prev:
        in_specs += [_row(tl, d), _const((1, d))]
        out_shape += [_sds((n, d), BF16), vec]
        out_specs += [_row(tl, d), _const((1, d))]
        scratch += [pltpu.VMEM((SUBLANES, d), F32)]
        args += [y_prev, gate_prev]
    return _pc(body, name=name, out_shape=tuple(out_shape), grid=(nb,), in_specs=in_specs,
               out_specs=tuple(out_specs), scratch_shapes=scratch, compiler_params=_params(1))(*args)


CONV_HALO = 8
CONF_HALO = 32


def _ssd_pre(xbc, dt_raw, conv_w, conv_b, dt_bias, name):
    n, c = xbc.shape
    tl = _tile(n, STRIP_ROW_TILE)
    hb = CONV_HALO
    k_taps = 4

    nu = _strip_units_per_trip(tl)
    nt = tl // (nu * SUBLANES)

    def body(x_ref, xp_ref, dt_ref, w_ref, b_ref, dtb_ref, pre_ref, dts_ref):
        i = pl.program_id(0)
        row = lax.broadcasted_iota(jnp.int32, (SUBLANES, LANES), 0)
        for lc in range(c // LANES):
            ls = slice(lc * LANES, (lc + 1) * LANES)
            w = [_bcast_row(w_ref, k, ls) for k in range(k_taps)]
            b = _bcast_row(b_ref, 0, ls)
            xp0 = jnp.where(i > 0, xp_ref[:, ls], 0.0)

            def strip(t, carry):
                xs = _strip_units(x_ref, ls, t, nu, nt, halo_prev=xp0)
                for u in range(nu):
                    prev, cur = xs[u], xs[u + 1]
                    pre_ref[_unit_rows(t, u, nu), ls] = (
                        b + w[0] * _down(prev, cur, 3, row) + w[1] * _down(prev, cur, 2, row)
                        + w[2] * _down(prev, cur, 1, row) + w[3] * cur)
                return carry

            lax.fori_loop(0, nt, strip, 0)
        v = dt_ref[...] + dtb_ref[...]
        dts_ref[...] = jnp.maximum(v, 0.0) + jnp.log1p(jnp.exp(-jnp.abs(v)))

    return _pc(body, name=name, out_shape=(_sds((n, c)), _sds((n, LANES))), grid=(n // tl,),
               in_specs=[_row(tl, c), _prev(tl, hb, c), _row(tl, LANES), _const((SUBLANES, c)), _const((1, c)),
                         _const((1, LANES))],
               out_specs=(_row(tl, c), _row(tl, LANES)), compiler_params=_params(1))(
        xbc, xbc, dt_raw, conv_w, conv_b, dt_bias)


def _ssd_pre_bwd(dpre, xbc, ddt, dt_raw, conv_w, dt_bias, name):
    n, c = xbc.shape
    tl = _tile(n, STRIP_ROW_TILE)
    nb = n // tl
    hb = CONV_HALO
    k_taps = 4

    nu = _strip_units_per_trip(tl)
    nt = tl // (nu * SUBLANES)

    def body(dp_ref, dpn_ref, x_ref, xp_ref, ddt_ref, dt_ref, w_ref, dtb_ref,
             dx_ref, ddr_ref, dw_ref, db_ref, ddtb_ref, acc_w, acc_b, acc_dtb):
        i = pl.program_id(0)

        @pl.when(i == 0)
        def _():
            acc_w[...] = jnp.zeros_like(acc_w)
            acc_b[...] = jnp.zeros_like(acc_b)
            acc_dtb[...] = jnp.zeros_like(acc_dtb)

        row = lax.broadcasted_iota(jnp.int32, (SUBLANES, LANES), 0)
        zero = jnp.zeros((SUBLANES, LANES), F32)
        for lc in range(c // LANES):
            ls = slice(lc * LANES, (lc + 1) * LANES)
            w = [_bcast_row(w_ref, k, ls) for k in range(k_taps)]
            xp0 = jnp.where(i > 0, xp_ref[:, ls], 0.0)
            dpn0 = jnp.where(i < nb - 1, dpn_ref[:, ls], 0.0)

            def strip(t, carry):
                acc = list(carry)
                dps = _strip_units(dp_ref, ls, t, nu, nt, halo_next=dpn0)
                xs = _strip_units(x_ref, ls, t, nu, nt, halo_prev=xp0)
                dxs = []
                for u in range(nu):
                    d, dn = dps[u], dps[u + 1]
                    dxs.append(w[3] * d + w[2] * _up(d, dn, 1, row) + w[1] * _up(d, dn, 2, row)
                               + w[0] * _up(d, dn, 3, row))
                    prev, cur = xs[u], xs[u + 1]
                    acc[3] = acc[3] + d * cur
                    for k in range(3):
                        acc[k] = acc[k] + d * _down(prev, cur, 3 - k, row)
                    acc[4] = acc[4] + d
                for p in range(nu // 2):
                    dx_ref[_pair_rows(t, p, nu), ls] = jnp.concatenate(dxs[2 * p:2 * p + 2], axis=0).astype(BF16)
                return tuple(acc)

            res = lax.fori_loop(0, nt, strip, (zero,) * 5)
            for k in range(k_taps):
                acc_w[k, :, ls] += res[k]
            acc_b[:, ls] += res[4]
        ddr = ddt_ref[...] * _sig(dt_ref[...] + dtb_ref[...])
        ddr_ref[...] = ddr.astype(BF16)
        acc_dtb[...] += _fold(ddr)

        @pl.when(i == nb - 1)
        def _():
            dw_ref[...] = jnp.zeros_like(dw_ref)
            for k in range(k_taps):
                dw_ref[k:k + 1, :] = jnp.sum(acc_w[k], axis=0, keepdims=True)
            db_ref[...] = jnp.sum(acc_b[...], axis=0, keepdims=True)
            ddtb_ref[...] = jnp.sum(acc_dtb[...], axis=0, keepdims=True)

    return _pc(body, name=name,
               out_shape=(_sds((n, c), BF16), _sds((n, LANES), BF16), _sds((SUBLANES, c)), _sds((1, c)),
                          _sds((1, LANES))),
               grid=(nb,),
               in_specs=[_row(tl, c), _next(tl, hb, c, n), _row(tl, c), _prev(tl, hb, c), _row(tl, LANES),
                         _row(tl, LANES), _const((SUBLANES, c)), _const((1, LANES))],
               out_specs=(_row(tl, c), _row(tl, LANES), _const((SUBLANES, c)), _const((1, c)), _const((1, LANES))),
               scratch_shapes=[pltpu.VMEM((k_taps, SUBLANES, c), F32), pltpu.VMEM((SUBLANES, c), F32),
                               pltpu.VMEM((SUBLANES, LANES), F32)],
               compiler_params=_params(1))(dpre, dpre, xbc, xbc, ddt, dt_raw, conv_w, dt_bias)


def _expand_mat():
    r = lax.broadcasted_iota(jnp.int32, (LANES, SSD_INNER), 0)
    c = lax.broadcasted_iota(jnp.int32, (LANES, SSD_INNER), 1)
    return (jnp.right_shift(c, 6) == r).astype(BF16)


def _reduce_mat():
    r = lax.broadcasted_iota(jnp.int32, (SSD_INNER, LANES), 0)
    c = lax.broadcasted_iota(jnp.int32, (SSD_INNER, LANES), 1)
    return (jnp.right_shift(r, 6) == c).astype(BF16)


def _ssd_common(pre, dt, alog):
    q = SSD_CHUNK
    sg = _sig(pre)
    act = pre * sg
    lane = lax.broadcasted_iota(jnp.int32, (1, LANES), 1)
    a_neg = jnp.where(lane < SSD_HEADS, -jnp.exp(alog), 0.0)
    rr = lax.broadcasted_iota(jnp.int32, (q, q), 0)
    cc = lax.broadcasted_iota(jnp.int32, (q, q), 1)
    causal = rr >= cc
    cum = _sel_dot(causal.astype(BF16), dt * a_neg)
    e_mat = _expand_mat()
    dtx = _dot_sel(dt, e_mat)
    cumx = _dot_sel(cum, e_mat)
    return sg, act, a_neg, causal, cum, e_mat, dtx, cumx


def _ssd_scan(pre, dt, alog, dvec, name):
    n = pre.shape[0]
    q = SSD_CHUNK
    nc = n // q

    def body(pre_ref, dt_ref, alog_ref, d_ref, y_ref, hp_ref, state):
        i = pl.program_id(0)

        @pl.when(i == 0)
        def _():
            state[...] = jnp.zeros_like(state)

        dt_ = dt_ref[...]
        _, act, _, causal, cum, e_mat, dtx, cumx = _ssd_common(pre_ref[...], dt_, alog_ref[...])
        xs = act[:, :SSD_INNER]
        bm = act[:, SSD_INNER:SSD_INNER + LANES]
        cm = act[:, SSD_INNER + LANES:]
        cum_t = cum.T
        clx = cumx[q - 1:q, :]
        xc = xs * dtx
        xd = xc * jnp.exp(clx - cumx)
        doutx = jnp.exp(cumx)
        edec = jnp.exp(clx)
        dx_row = _dot_sel(jnp.broadcast_to(d_ref[...], (SUBLANES, LANES)), e_mat)[0:1, :]
        hp_ref[0] = state[...]
        bb = bm.astype(BF16)
        cb = cm.astype(BF16)
        lane = lax.broadcasted_iota(jnp.int32, (1, LANES), 1)
        row = lax.broadcasted_iota(jnp.int32, (LANES, 1), 0)
        cbs = []
        for g in range(2):
            cg = jnp.where(jnp.right_shift(lane, 6) == g, cm, 0.0).astype(BF16)
            cbs.append(_dot(cg, bb, NT))
        for j in range(SSD_HEADS // 2):
            sl = slice(j * LANES, (j + 1) * LANES)
            g = j // 4
            xcj = xc[:, sl].astype(BF16)
            halves = []
            for half in range(2):
                h = 2 * j + half
                seg = cum[:, h:h + 1] - cum_t[h:h + 1, :]
                w = cbs[g] * jnp.exp(jnp.where(causal, seg, -jnp.inf))
                halves.append(_dot(w.astype(BF16), xcj))
            y_diag = jnp.where(lane < SSD_HEAD_DIM, halves[0], halves[1])
            hj = state[:, sl]
            y_off = doutx[:, sl] * _dot(cb, hj.astype(BF16))
            y_ref[:, sl] = y_diag + y_off + xs[:, sl] * dx_row[:, sl]
            st = _dot(bb, xd[:, sl].astype(BF16), TN)
            state[:, sl] = hj * edec[:, sl] + jnp.where(jnp.right_shift(row, 6) == g, st, 0.0)

    return _pc(body, name=name, out_shape=(_sds((n, SSD_INNER)), _sds((nc, LANES, SSD_INNER))), grid=(nc,),
               in_specs=[_row(q, SSD_XBC), _row(q, LANES), _const((1, LANES)), _const((1, LANES))],
               out_specs=(_row(q, SSD_INNER), pl.BlockSpec((1, LANES, SSD_INNER), lambda i: (i, 0, 0))),
               scratch_shapes=[pltpu.VMEM((LANES, SSD_INNER), F32)], compiler_params=_params(1))(pre, dt, alog, dvec)


def _ssd_scan_bwd(pre, dt, hprev, dy, alog, dvec, name):
    n = pre.shape[0]
    q = SSD_CHUNK
    nc = n // q

    def body(pre_ref, dt_ref, hp_ref, dy_ref, alog_ref, d_ref, dpre_ref, ddt_ref, da_ref, dd_ref,
             d_state, dxc_s, dcx_s, dcl_s, acc_a, acc_d):
        i = pl.program_id(0)

        @pl.when(i == 0)
        def _():
            d_state[...] = jnp.zeros_like(d_state)
            acc_a[...] = jnp.zeros_like(acc_a)
            acc_d[...] = jnp.zeros_like(acc_d)

        pre_ = pre_ref[...]
        dt_ = dt_ref[...]
        sg, act, a_neg, causal, cum, e_mat, dtx, cumx = _ssd_common(pre_, dt_, alog_ref[...])
        r_mat = _reduce_mat()
        xs = act[:, :SSD_INNER]
        bm = act[:, SSD_INNER:SSD_INNER + LANES]
        cm = act[:, SSD_INNER + LANES:]
        cum_t = cum.T
        clx = cumx[q - 1:q, :]
        xc = xs * dtx
        dsx = jnp.exp(clx - cumx)
        doutx = jnp.exp(cumx)
        edec = jnp.exp(clx)
        dx_row = _dot_sel(jnp.broadcast_to(d_ref[...], (SUBLANES, LANES)), e_mat)[0:1, :]
        dy_ = dy_ref[...]
        acc_d[...] += _fold(dy_ * xs)
        bb = bm.astype(BF16)
        cb = cm.astype(BF16)
        lane = lax.broadcasted_iota(jnp.int32, (1, LANES), 1)
        row = lax.broadcasted_iota(jnp.int32, (LANES, 1), 0)
        d_c = jnp.zeros((q, LANES), F32)
        d_b = jnp.zeros((q, LANES), F32)

        for j in range(SSD_HEADS // 2):
            sl = slice(j * LANES, (j + 1) * LANES)
            g = j // 4
            hj = hp_ref[0, :, sl]
            hjb = hj.astype(BF16)
            dyj = dy_[:, sl]
            tj = _dot(cb, hjb)
            dtj = (doutx[:, sl] * dyj).astype(BF16)
            dcx = dyj * tj * doutx[:, sl]
            d_c = d_c + _dot(dtj, hjb, NT)
            dhn = d_state[:, sl]
            dhp = dhn * edec[:, sl] + jnp.where(jnp.right_shift(row, 6) == g, _dot(cb, dtj, TN), 0.0)
            dcl = jnp.sum(dhn * hj, axis=0, keepdims=True) * edec[:, sl]
            dsb = dhn.astype(BF16)
            dxd = _dot(bb, dsb)
            xcj = xc[:, sl]
            dsj = dsx[:, sl]
            d_b = d_b + _dot((xcj * dsj).astype(BF16), dsb, NT)
            dds = dxd * xcj * dsj
            d_state[:, sl] = dhp
            dxc_s[:, sl] = dxd * dsj
            dcx_s[:, sl] = dcx - dds
            dcl_s[:, sl] = jnp.broadcast_to(dcl + jnp.sum(dds, axis=0, keepdims=True), (SUBLANES, LANES))

        dcum_c = jnp.zeros((q, LANES), F32)
        dcum_t = jnp.zeros((LANES, q), F32)
        for g in range(2):
            gmask = jnp.right_shift(lane, 6) == g
            cg = jnp.where(gmask, cm, 0.0).astype(BF16)
            cbg = _dot(cg, bb, NT)
            d_cb = jnp.zeros((q, q), F32)
            for hh in range(SSD_HEADS // 2):
                h = g * (SSD_HEADS // 2) + hh
                j, half = h // 2, h % 2
                sl = slice(j * LANES, (j + 1) * LANES)
                hmask = jnp.right_shift(lane, 6) == half
                seg = cum[:, h:h + 1] - cum_t[h:h + 1, :]
                lm = jnp.exp(jnp.where(causal, seg, -jnp.inf))
                w = cbg * lm
                dyj = dy_[:, sl]
                dw = _dot(jnp.where(hmask, dyj, 0.0).astype(BF16), xc[:, sl].astype(BF16), NT)
                dxch = _dot(w.astype(BF16), dyj.astype(BF16), TN)
                dxc_s[:, sl] += jnp.where(hmask, dxch, 0.0)
                d_cb = d_cb + dw * lm
                m = dw * w
                dcum_c = dcum_c + jnp.sum(m, axis=1, keepdims=True) * (lane == h).astype(F32)
                dcum_t = dcum_t + (row == h).astype(F32) * jnp.sum(m, axis=0, keepdims=True)
            d_cbb = d_cb.astype(BF16)
            d_c = d_c + jnp.where(gmask, _dot(d_cbb, bb), 0.0)
            d_b = d_b + jnp.where(gmask, _dot(d_cbb, cb, TN), 0.0)

        dcl_row = _dot_sel(dcl_s[...], r_mat)[0:1, :]
        rowq = lax.broadcasted_iota(jnp.int32, (q, 1), 0)
        dcum = (dcum_c - dcum_t.T + _dot_sel(dcx_s[...], r_mat)
                + jnp.where(rowq == q - 1, dcl_row, 0.0))
        rr = lax.broadcasted_iota(jnp.int32, (q, q), 0)
        cc = lax.broadcasted_iota(jnp.int32, (q, q), 1)
        dadt = _sel_dot((rr <= cc).astype(BF16), dcum)
        dxc = dxc_s[...]
        ddt_ref[...] = dadt * a_neg + _dot_sel(dxc * xs, r_mat)
        acc_a[...] += _fold(dadt * dt_)
        dsilu = sg * (1.0 + pre_ * (1.0 - sg))
        dpre_ref[:, :SSD_INNER] = (dxc * dtx + dy_ * dx_row) * dsilu[:, :SSD_INNER]
        dpre_ref[:, SSD_INNER:SSD_INNER + LANES] = d_b * dsilu[:, SSD_INNER:SSD_INNER + LANES]
        dpre_ref[:, SSD_INNER + LANES:] = d_c * dsilu[:, SSD_INNER + LANES:]

        @pl.when(i == nc - 1)
        def _():
            da_ref[...] = jnp.sum(acc_a[...], axis=0, keepdims=True) * a_neg
            dd_ref[...] = jnp.sum(_dot_sel(acc_d[...], r_mat), axis=0, keepdims=True)

    rev = lambda i: (nc - 1 - i, 0)
    return _pc(body, name=name,
               out_shape=(_sds((n, SSD_XBC)), _sds((n, LANES)), _sds((1, LANES)), _sds((1, LANES))), grid=(nc,),
               in_specs=[pl.BlockSpec((q, SSD_XBC), rev), pl.BlockSpec((q, LANES), rev),
                         pl.BlockSpec((1, LANES, SSD_INNER), lambda i: (nc - 1 - i, 0, 0)),
                         pl.BlockSpec((q, SSD_INNER), rev), _const((1, LANES)), _const((1, LANES))],
               out_specs=(pl.BlockSpec((q, SSD_XBC), rev), pl.BlockSpec((q, LANES), rev), _const((1, LANES)),
                          _const((1, LANES))),
               scratch_shapes=[pltpu.VMEM((LANES, SSD_INNER), F32), pltpu.VMEM((q, SSD_INNER), F32),
                               pltpu.VMEM((q, SSD_INNER), F32), pltpu.VMEM((SUBLANES, SSD_INNER), F32),
                               pltpu.VMEM((SUBLANES, LANES), F32), pltpu.VMEM((SUBLANES, SSD_INNER), F32)],
               compiler_params=_params(1))(pre, dt, hprev, dy, alog, dvec)


def _group_norm_parts(v):
    half = SSD_INNER // 2
    r0 = lax.rsqrt(jnp.mean(v[:, :half] * v[:, :half], axis=-1, keepdims=True) + EPS)
    r1 = lax.rsqrt(jnp.mean(v[:, half:] * v[:, half:], axis=-1, keepdims=True) + EPS)
    lane = lax.broadcasted_iota(jnp.int32, (1, SSD_INNER), 1)
    return jnp.where(lane < half, r0, r1)


def _group_mean(v):
    half = SSD_INNER // 2
    m0 = jnp.mean(v[:, :half], axis=-1, keepdims=True)
    m1 = jnp.mean(v[:, half:], axis=-1, keepdims=True)
    lane = lax.broadcasted_iota(jnp.int32, (1, SSD_INNER), 1)
    return jnp.where(lane < half, m0, m1)


def _ssd_post(y, z, g, name):
    n, d = y.shape
    tl = _tile(n, NORM_ROW_TILE)

    def body(y_ref, z_ref, g_ref, o_ref):
        z_ = z_ref[...]
        v = y_ref[...] * (z_ * _sig(z_))
        o_ref[...] = ((v * _group_norm_parts(v)) * g_ref[...]).astype(BF16)

    return _pc(body, name=name, out_shape=_sds((n, d), BF16), grid=(n // tl,),
               in_specs=[_row(tl, d), _row(tl, d), _const((1, d))], out_specs=_row(tl, d),
               compiler_params=_params(1))(y, z, g)


def _ssd_post_bwd(dout, y, z, g, name):
    n, d = y.shape
    tl = _tile(n, NORM_ROW_TILE)
    nb = n // tl

    def body(do_ref, y_ref, z_ref, g_ref, dy_ref, dz_ref, dg_ref, acc_g):
        i = pl.program_id(0)

        @pl.when(i == 0)
        def _():
            acc_g[...] = jnp.zeros_like(acc_g)

        z_ = z_ref[...]
        y_ = y_ref[...]
        sz = _sig(z_)
        silu_z = z_ * sz
        v = y_ * silu_z
        rs = _group_norm_parts(v)
        nv = v * rs
        do_ = do_ref[...]
        acc_g[...] += _fold(do_ * nv)
        dn = do_ * g_ref[...]
        dv = rs * (dn - nv * _group_mean(dn * nv))
        dy_ref[...] = dv * silu_z
        dz_ref[...] = (dv * y_ * (sz * (1.0 + z_ * (1.0 - sz)))).astype(BF16)

        @pl.when(i == nb - 1)
        def _():
            dg_ref[...] = jnp.sum(acc_g[...], axis=0, keepdims=True)

    return _pc(body, name=name, out_shape=(_sds((n, d)), _sds((n, d), BF16), _sds((1, d))), grid=(nb,),
               in_specs=[_row(tl, d), _row(tl, d), _row(tl, d), _const((1, d))],
               out_specs=(_row(tl, d), _row(tl, d), _const((1, d))),
               scratch_shapes=[pltpu.VMEM((SUBLANES, d), F32)], compiler_params=_params(1))(dout, y, z, g)


def _layer_norm_parts(uc):
    mu = jnp.mean(uc, axis=-1, keepdims=True)
    xc = uc - mu
    rstd = lax.rsqrt(jnp.mean(xc * xc, axis=-1, keepdims=True) + EPS)
    return xc * rstd, rstd


def _rows_x8(w):
    return jnp.broadcast_to(w[:, None, :], (w.shape[0], SUBLANES, w.shape[1]))


def _glu_units(x_ref, ls, gl, t, nu, halo):
    nh = len(halo)
    units = []
    for h in range(nh):
        rows = pl.ds(pl.multiple_of(jnp.maximum(t * nu - nh + h, 0) * SUBLANES, SUBLANES), SUBLANES)
        units.append(jnp.where(t > 0, x_ref[rows, ls] * _sig(x_ref[rows, gl]), halo[h]))
    for u in range(nu):
        rows = _unit_rows(t, u, nu)
        units.append(x_ref[rows, ls] * _sig(x_ref[rows, gl]))
    return units


def _memo_rolls(units):
    memo = {}

    def rolls(key):
        if key not in memo:
            memo[key] = pltpu.roll(units[key[0]], key[1], 0)
        return memo[key]

    return rolls


def _window(units, e, s, rolls, row, up):
    a, b = divmod(s, SUBLANES)
    if b == 0:
        return units[e + a] if up else units[e - a]
    if up:
        sh = SUBLANES - b
        return jnp.where(row < sh, rolls((e + a, sh)), rolls((e + a + 1, sh)))
    return jnp.where(row < b, rolls((e - a - 1, b)), rolls((e - a, b)))


def _conf_fwd(conf_in, conv_w, conv_b, ln_g, ln_b, name):
    n = conf_in.shape[0]
    c = CONF_WIDTH
    tl = _tile(n, STRIP_ROW_TILE)
    hb = CONF_HALO
    k_taps = CONF_KERNEL

    nu = _strip_units_per_trip(tl)
    nt = tl // (nu * SUBLANES)
    nh = hb // SUBLANES

    def body(x_ref, xp_ref, w_ref, b_ref, g_ref, beta_ref, o_ref, uc_ref):
        i = pl.program_id(0)
        row = lax.broadcasted_iota(jnp.int32, (SUBLANES, LANES), 0)
        for lc in range(c // LANES):
            ls = slice(lc * LANES, (lc + 1) * LANES)
            gl = slice(c + lc * LANES, c + (lc + 1) * LANES)
            bias = _bcast_row(b_ref, 0, ls)
            halo = [jnp.where(i > 0, xp_ref[SUBLANES * h:SUBLANES * (h + 1), ls]
                              * _sig(xp_ref[SUBLANES * h:SUBLANES * (h + 1), gl]), 0.0) for h in range(nh)]

            def strip(t, carry):
                units = _glu_units(x_ref, ls, gl, t, nu, halo)
                rolls = _memo_rolls(units)
                for u in range(nu):
                    acc = bias
                    for k in range(k_taps):
                        acc = acc + w_ref[k, :, ls] * _window(units, u + nh, k_taps - 1 - k, rolls, row, up=False)
                    uc_ref[_unit_rows(t, u, nu), ls] = acc
                return carry

            lax.fori_loop(0, nt, strip, 0)
        nv, _ = _layer_norm_parts(uc_ref[...])
        v = nv * g_ref[...] + beta_ref[...]
        o_ref[...] = (v * _sig(v)).astype(BF16)

    return _pc(body, name=name, out_shape=(_sds((n, c), BF16), _sds((n, c))), grid=(n // tl,),
               in_specs=[_row(tl, 2 * c), _prev(tl, hb, 2 * c), _const((hb, SUBLANES, c)), _const((1, c)),
                         _const((1, c)), _const((1, c))],
               out_specs=(_row(tl, c), _row(tl, c)), compiler_params=_params(1))(
        conf_in, conf_in, _rows_x8(conv_w), conv_b, ln_g, ln_b)


def _conf_bwd(dout, uc, conf_in, conv_w, ln_g, ln_b, name):
    n = conf_in.shape[0]
    c = CONF_WIDTH
    tl = _tile(n, STRIP_ROW_TILE)
    nb = n // tl
    hb = CONF_HALO
    k_taps = CONF_KERNEL

    nu = _strip_units_per_trip(tl)
    nt = tl // (nu * SUBLANES)
    nh = hb // SUBLANES

    def body(do_ref, don_ref, uc_ref, ucn_ref, x_ref, xp_ref, w_ref, g_ref, beta_ref,
             dx_ref, dw_ref, db_ref, dg_ref, dbeta_ref, dbuf, acc_w, acc_b, acc_g, acc_beta):
        i = pl.program_id(0)

        @pl.when(i == 0)
        def _():
            acc_w[...] = jnp.zeros_like(acc_w)
            acc_b[...] = jnp.zeros_like(acc_b)
            acc_g[...] = jnp.zeros_like(acc_g)
            acc_beta[...] = jnp.zeros_like(acc_beta)

        g_ = g_ref[...]
        beta_ = beta_ref[...]

        def d_conv_out(do_, uc_):
            nv, rstd = _layer_norm_parts(uc_)
            v = nv * g_ + beta_
            sv = _sig(v)
            dv = do_ * (sv * (1.0 + v * (1.0 - sv)))
            dn = dv * g_
            duc = rstd * (dn - jnp.mean(dn, axis=-1, keepdims=True)
                          - nv * jnp.mean(dn * nv, axis=-1, keepdims=True))
            return duc, dv, nv

        duc, dv, nv = d_conv_out(do_ref[...], uc_ref[...])
        acc_g[...] += _fold(dv * nv)
        acc_beta[...] += _fold(dv)
        acc_b[...] += _fold(duc)
        dbuf[pl.ds(0, tl), :] = duc
        ducn, _, _ = d_conv_out(don_ref[...], ucn_ref[...])
        dbuf[pl.ds(tl, hb), :] = jnp.where(i < nb - 1, ducn, 0.0)

        row = lax.broadcasted_iota(jnp.int32, (SUBLANES, LANES), 0)
        for lc in range(c // LANES):
            ls = slice(lc * LANES, (lc + 1) * LANES)
            gl = slice(c + lc * LANES, c + (lc + 1) * LANES)
            halo = [jnp.where(i > 0, xp_ref[SUBLANES * h:SUBLANES * (h + 1), ls]
                              * _sig(xp_ref[SUBLANES * h:SUBLANES * (h + 1), gl]), 0.0) for h in range(nh)]

            def strip(t, carry):
                us = _glu_units(x_ref, ls, gl, t, nu, halo)
                ds = [dbuf[pl.ds(pl.multiple_of((t * nu + u) * SUBLANES, SUBLANES), SUBLANES), ls]
                      for u in range(nu + nh)]
                u_rolls, d_rolls = _memo_rolls(us), _memo_rolls(ds)
                for k in range(k_taps):
                    part = ds[0] * _window(us, nh, k_taps - 1 - k, u_rolls, row, up=False)
                    for u in range(1, nu):
                        part = part + ds[u] * _window(us, u + nh, k_taps - 1 - k, u_rolls, row, up=False)
                    acc_w[k, :, ls] += part
                dus = []
                for u in range(nu):
                    du = w_ref[0, :, ls] * _window(ds, u, k_taps - 1, d_rolls, row, up=True)
                    for k in range(1, k_taps):
                        du = du + w_ref[k, :, ls] * _window(ds, u, k_taps - 1 - k, d_rolls, row, up=True)
                    dus.append(du)
                for p in range(nu // 2):
                    rows = _pair_rows(t, p, nu)
                    du2 = jnp.concatenate(dus[2 * p:2 * p + 2], axis=0)
                    val, sgate = x_ref[rows, ls], _sig(x_ref[rows, gl])
                    dx_ref[rows, ls] = (du2 * sgate).astype(BF16)
                    dx_ref[rows, gl] = (du2 * val * sgate * (1.0 - sgate)).astype(BF16)
                return carry

            lax.fori_loop(0, nt, strip, 0)

        @pl.when(i == nb - 1)
        def _():
            dw_ref[...] = jnp.zeros_like(dw_ref)
            for k in range(k_taps):
                dw_ref[k:k + 1, :] = jnp.sum(acc_w[k], axis=0, keepdims=True)
            db_ref[...] = jnp.sum(acc_b[...], axis=0, keepdims=True)
            dg_ref[...] = jnp.sum(acc_g[...], axis=0, keepdims=True)
            dbeta_ref[...] = jnp.sum(acc_beta[...], axis=0, keepdims=True)

    vec = _sds((1, c))
    return _pc(body, name=name, out_shape=(_sds((n, 2 * c), BF16), _sds((hb, c)), vec, vec, vec), grid=(nb,),
               in_specs=[_row(tl, c), _next(tl, hb, c, n), _row(tl, c), _next(tl, hb, c, n), _row(tl, 2 * c),
                         _prev(tl, hb, 2 * c), _const((hb, SUBLANES, c)), _const((1, c)), _const((1, c))],
               out_specs=(_row(tl, 2 * c), _const((hb, c)), _const((1, c)), _const((1, c)), _const((1, c))),
               scratch_shapes=[pltpu.VMEM((tl + hb, c), F32),
                               pltpu.VMEM((k_taps, SUBLANES, c), F32), pltpu.VMEM((SUBLANES, c), F32),
                               pltpu.VMEM((SUBLANES, c), F32), pltpu.VMEM((SUBLANES, c), F32)],
               compiler_params=_params(1))(dout, dout, uc, uc, conf_in, conf_in, _rows_x8(conv_w), ln_g, ln_b)


def _sc_fwd(sc_in, conv_w, name):
    n = sc_in.shape[0]
    c = SC_WIDTH
    tl = _tile(n, STRIP_ROW_TILE)
    hb = CONV_HALO

    nu = _strip_units_per_trip(tl)
    nt = tl // (nu * SUBLANES)

    def body(x_ref, xp_ref, w_ref, o_ref):
        i = pl.program_id(0)
        row = lax.broadcasted_iota(jnp.int32, (SUBLANES, LANES), 0)
        for lc in range(c // LANES):
            ls = slice(lc * LANES, (lc + 1) * LANES)
            gc_ls = slice(c + lc * LANES, c + (lc + 1) * LANES)
            xv_ls = slice(2 * c + lc * LANES, 2 * c + (lc + 1) * LANES)
            w = [_bcast_row(w_ref, k, ls) for k in range(3)]
            gc0 = jnp.where(i > 0, xp_ref[:, gc_ls], 0.0)
            xv0 = xp_ref[:, xv_ls]

            def strip(t, carry):
                gcs = _strip_units(x_ref, gc_ls, t, nu, nt, halo_prev=gc0)
                xvs = _strip_units(x_ref, xv_ls, t, nu, nt, halo_prev=xv0)
                ps = [a * b for a, b in zip(gcs, xvs)]
                outs = []
                for u in range(nu):
                    prev, cur = ps[u], ps[u + 1]
                    cv = w[0] * _down(prev, cur, 2, row) + w[1] * _down(prev, cur, 1, row) + w[2] * cur
                    outs.append(x_ref[_unit_rows(t, u, nu), ls] * cv)
                for p in range(nu // 2):
                    o_ref[_pair_rows(t, p, nu), ls] = jnp.concatenate(outs[2 * p:2 * p + 2], axis=0).astype(BF16)
                return carry

            lax.fori_loop(0, nt, strip, 0)

    return _pc(body, name=name, out_shape=_sds((n, c), BF16), grid=(n // tl,),
               in_specs=[_row(tl, 3 * c), _prev(tl, hb, 3 * c), _const((SUBLANES, c))], out_specs=_row(tl, c),
               compiler_params=_params(1))(sc_in, sc_in, conv_w)


def _sc_bwd(dout, sc_in, conv_w, name):
    n = sc_in.shape[0]
    c = SC_WIDTH
    tl = _tile(n, STRIP_ROW_TILE)
    nb = n // tl
    hb = CONV_HALO

    nu = _strip_units_per_trip(tl)
    nt = tl // (nu * SUBLANES)

    def body(do_ref, don_ref, x_ref, xp_ref, xn_ref, w_ref, dx_ref, dw_ref, acc_w):
        i = pl.program_id(0)

        @pl.when(i == 0)
        def _():
            acc_w[...] = jnp.zeros_like(acc_w)

        row = lax.broadcasted_iota(jnp.int32, (SUBLANES, LANES), 0)
        zero = jnp.zeros((SUBLANES, LANES), F32)
        for lc in range(c // LANES):
            ls = slice(lc * LANES, (lc + 1) * LANES)
            gc_ls = slice(c + lc * LANES, c + (lc + 1) * LANES)
            xv_ls = slice(2 * c + lc * LANES, 2 * c + (lc + 1) * LANES)
            w = [_bcast_row(w_ref, k, ls) for k in range(3)]
            gc0 = jnp.where(i > 0, xp_ref[:, gc_ls], 0.0)
            xv0 = xp_ref[:, xv_ls]
            don0 = jnp.where(i < nb - 1, don_ref[:, ls], 0.0)
            gbn0 = xn_ref[:, ls]

            def strip(t, carry):
                acc = list(carry)
                dos = _strip_units(do_ref, ls, t, nu, nt, halo_next=don0)
                gbs = _strip_units(x_ref, ls, t, nu, nt, halo_next=gbn0)
                gcs = _strip_units(x_ref, gc_ls, t, nu, nt, halo_prev=gc0)
                xvs = _strip_units(x_ref, xv_ls, t, nu, nt, halo_prev=xv0)
                dcv = [a * b for a, b in zip(dos, gbs)]
                ps = [a * b for a, b in zip(gcs, xvs)]
                d_gb, d_gc, d_xv = [], [], []
                for u in range(nu):
                    prev, cur = ps[u], ps[u + 1]
                    p1, p2 = _down(prev, cur, 1, row), _down(prev, cur, 2, row)
                    d, dn = dcv[u], dcv[u + 1]
                    dp = w[2] * d + w[1] * _up(d, dn, 1, row) + w[0] * _up(d, dn, 2, row)
                    d_gb.append(dos[u] * (w[0] * p2 + w[1] * p1 + w[2] * cur))
                    d_gc.append(dp * xvs[u + 1])
                    d_xv.append(dp * gcs[u + 1])
                    acc[0] = acc[0] + d * p2
                    acc[1] = acc[1] + d * p1
                    acc[2] = acc[2] + d * cur
                for p in range(nu // 2):
                    rows = _pair_rows(t, p, nu)
                    for vals, lanes in ((d_gb, ls), (d_gc, gc_ls), (d_xv, xv_ls)):
                        dx_ref[rows, lanes] = jnp.concatenate(vals[2 * p:2 * p + 2], axis=0).astype(BF16)
                return tuple(acc)

            res = lax.fori_loop(0, nt, strip, (zero,) * 3)
            for k in range(3):
                acc_w[k, :, ls] += res[k]

        @pl.when(i == nb - 1)
        def _():
            dw_ref[...] = jnp.zeros_like(dw_ref)
            for k in range(3):
                dw_ref[k:k + 1, :] = jnp.sum(acc_w[k], axis=0, keepdims=True)

    return _pc(body, name=name, out_shape=(_sds((n, 3 * c), BF16), _sds((SUBLANES, c))), grid=(nb,),
               in_specs=[_row(tl, c), _next(tl, hb, c, n), _row(tl, 3 * c), _prev(tl, hb, 3 * c),
                         _next(tl, hb, c, n), _const((SUBLANES, c))],
               out_specs=(_row(tl, 3 * c), _const((SUBLANES, c))),
               scratch_shapes=[pltpu.VMEM((3, SUBLANES, c), F32)],
               compiler_params=_params(1))(dout, dout, sc_in, sc_in, sc_in, conv_w)


def _merge_fwd(gates, ya, yb, yc, b_gate, name):
    n, d = ya.shape
    tl = _tile(n, NORM_ROW_TILE)

    def body(gt_ref, ya_ref, yb_ref, yc_ref, b_ref, o_ref):
        gt = _sig(gt_ref[...] + b_ref[...])
        o_ref[...] = (gt[:, :d] * ya_ref[...] + gt[:, d:2 * d] * yb_ref[...] + gt[:, 2 * d:] * yc_ref[...]).astype(BF16)

    return _pc(body, name=name, out_shape=_sds((n, d), BF16), grid=(n // tl,),
               in_specs=[_row(tl, 3 * d), _row(tl, d), _row(tl, d), _row(tl, d), _const((1, 3 * d))],
               out_specs=_row(tl, d), compiler_params=_params(1))(gates, ya, yb, yc, b_gate)


def _merge_bwd(dm, gates, ya, yb, yc, b_gate, name):
    n, d = ya.shape
    tl = _tile(n, NORM_ROW_TILE)
    nb = n // tl

    def body(dm_ref, gt_ref, ya_ref, yb_ref, yc_ref, b_ref, dya_ref, dyb_ref, dyc_ref, dgt_ref, db_ref, acc):
        i = pl.program_id(0)

        @pl.when(i == 0)
        def _():
            acc[...] = jnp.zeros_like(acc)

        dm_ = dm_ref[...]
        gt = _sig(gt_ref[...] + b_ref[...])
        for idx, (y_ref, dy_ref) in enumerate(((ya_ref, dya_ref), (yb_ref, dyb_ref), (yc_ref, dyc_ref))):
            gk = gt[:, idx * d:(idx + 1) * d]
            dy_ref[...] = (dm_ * gk).astype(BF16)
            dpre = dm_ * y_ref[...] * gk * (1.0 - gk)
            dgt_ref[:, idx * d:(idx + 1) * d] = dpre.astype(BF16)
            acc[:, idx * d:(idx + 1) * d] += _fold(dpre)

        @pl.when(i == nb - 1)
        def _():
            db_ref[...] = jnp.sum(acc[...], axis=0, keepdims=True)

    bf = _sds((n, d), BF16)
    return _pc(body, name=name, out_shape=(bf, bf, bf, _sds((n, 3 * d), BF16), _sds((1, 3 * d))), grid=(nb,),
               in_specs=[_row(tl, d), _row(tl, 3 * d), _row(tl, d), _row(tl, d), _row(tl, d), _const((1, 3 * d))],
               out_specs=(_row(tl, d), _row(tl, d), _row(tl, d), _row(tl, 3 * d), _const((1, 3 * d))),
               scratch_shapes=[pltpu.VMEM((SUBLANES, 3 * d), F32)], compiler_params=_params(1))(
        dm, gates, ya, yb, yc, b_gate)


FFN_COLS = 1408
FFN_STRIP = 64
FFN_STRIP_BWD = 32


def _down(prev, cur, s, row):
    return jnp.where(row < s, pltpu.roll(prev, s, 0), pltpu.roll(cur, s, 0))


def _up(cur, nxt, s, row):
    return jnp.where(row < SUBLANES - s, pltpu.roll(cur, SUBLANES - s, 0), pltpu.roll(nxt, SUBLANES - s, 0))


def _ffn_mid(up, conv_w, conv_b, name):
    n = up.shape[0]
    tl = _tile(n, STRIP_ROW_TILE)
    hb = CONV_HALO
    tc = FFN_COLS
    ncb = D_FF // tc

    def spec(shape_rows, idx_fn, off):
        return pl.BlockSpec((shape_rows, tc), lambda j, i, off=off: (idx_fn(i), j + off))

    r = tl // hb
    cur = lambda i: i
    prv = lambda i: jnp.maximum(i * r - 1, 0)

    def body(g_ref, gp_ref, v_ref, vp_ref, wg_ref, wv_ref, bg_ref, bv_ref, o_ref):
        i = pl.program_id(1)
        row = lax.broadcasted_iota(jnp.int32, (SUBLANES, LANES), 0)
        full = lambda ref, k, ls: jnp.broadcast_to(ref[k:k + 1, ls], (SUBLANES, LANES))
        for lc in range(tc // LANES):
            ls = slice(lc * LANES, (lc + 1) * LANES)
            wg = [full(wg_ref, k, ls) for k in range(3)]
            wv = [full(wv_ref, k, ls) for k in range(3)]
            bg, bv = full(bg_ref, 0, ls), full(bv_ref, 0, ls)

            def conv(prev, x, w, b):
                return b + w[0] * _down(prev, x, 2, row) + w[1] * _down(prev, x, 1, row) + w[2] * x

            def strip(t, carry):
                gs, vs = [carry[0]], [carry[1]]
                for u in range(FFN_STRIP // SUBLANES):
                    rows = pl.ds(pl.multiple_of(t * FFN_STRIP + u * SUBLANES, SUBLANES), SUBLANES)
                    gs.append(g_ref[rows, ls])
                    vs.append(v_ref[rows, ls])
                outs = []
                for u in range(FFN_STRIP // SUBLANES):
                    ug = conv(gs[u], gs[u + 1], wg, bg)
                    outs.append(ug * _sig(ug) * conv(vs[u], vs[u + 1], wv, bv))
                for p in range(FFN_STRIP // 16):
                    rows = pl.ds(pl.multiple_of(t * FFN_STRIP + p * 16, 16), 16)
                    o_ref[rows, ls] = jnp.concatenate(outs[2 * p:2 * p + 2], axis=0).astype(BF16)
                return gs[-1], vs[-1]

            lax.fori_loop(0, tl // FFN_STRIP, strip,
                          (jnp.where(i > 0, gp_ref[:, ls], 0.0), jnp.where(i > 0, vp_ref[:, ls], 0.0)))

    wspec = lambda off: pl.BlockSpec((SUBLANES, tc), lambda j, i, off=off: (0, j + off))
    bspec = lambda off: pl.BlockSpec((1, tc), lambda j, i, off=off: (0, j + off))
    return _pc(body, name=name, out_shape=_sds((n, D_FF), BF16), grid=(ncb, n // tl),
               in_specs=[spec(tl, cur, 0), spec(hb, prv, 0), spec(tl, cur, ncb), spec(hb, prv, ncb),
                         wspec(0), wspec(ncb), bspec(0), bspec(ncb)],
               out_specs=pl.BlockSpec((tl, tc), lambda j, i: (i, j)), compiler_params=_params(2))(
        up, up, up, up, conv_w, conv_w, conv_b, conv_b)


def _ffn_mid_bwd(da, up, conv_w, conv_b, name):
    n = up.shape[0]
    tl = _tile(n, STRIP_ROW_TILE)
    nb = n // tl
    nt = tl // FFN_STRIP_BWD
    hb = CONV_HALO
    tc = FFN_COLS
    ncb = D_FF // tc
    r = tl // hb
    last = n // hb - 1
    cur = lambda i: i
    prv = lambda i: jnp.maximum(i * r - 1, 0)
    nxt = lambda i: jnp.minimum((i + 1) * r, last)

    def spec(shape_rows, idx_fn, off):
        return pl.BlockSpec((shape_rows, tc), lambda j, i, off=off: (idx_fn(i), j + off))

    def body(da_ref, dan_ref, g_ref, gp_ref, gn_ref, v_ref, vp_ref, vn_ref, wg_ref, wv_ref, bg_ref, bv_ref,
             dg_ref, dv_ref, dwg_ref, dwv_ref, dbg_ref, dbv_ref, acc_w, acc_b):
        i = pl.program_id(1)

        @pl.when(i == 0)
        def _():
            acc_w[...] = jnp.zeros_like(acc_w)
            acc_b[...] = jnp.zeros_like(acc_b)

        row = lax.broadcasted_iota(jnp.int32, (SUBLANES, LANES), 0)
        full = lambda ref, k, ls: jnp.broadcast_to(ref[k:k + 1, ls], (SUBLANES, LANES))
        zero = jnp.zeros((SUBLANES, LANES), F32)

        def d_conv_out(da_, ug, uv):
            s = _sig(ug)
            return da_ * uv * (s * (1.0 + ug * (1.0 - s))), da_ * (ug * s)

        for lc in range(tc // LANES):
            ls = slice(lc * LANES, (lc + 1) * LANES)
            wg = [full(wg_ref, k, ls) for k in range(3)]
            wv = [full(wv_ref, k, ls) for k in range(3)]
            bg, bv = full(bg_ref, 0, ls), full(bv_ref, 0, ls)

            def unit(prev_g, g, prev_v, v, da_):
                g1, g2 = _down(prev_g, g, 1, row), _down(prev_g, g, 2, row)
                v1, v2 = _down(prev_v, v, 1, row), _down(prev_v, v, 2, row)
                ug = bg + wg[0] * g2 + wg[1] * g1 + wg[2] * g
                uv = bv + wv[0] * v2 + wv[1] * v1 + wv[2] * v
                dug, duv = d_conv_out(da_, ug, uv)
                return dug, duv, (g2, g1, g), (v2, v1, v)

            def d_in(d, d_next, w):
                return w[2] * d + w[1] * _up(d, d_next, 1, row) + w[0] * _up(d, d_next, 2, row)

            tail = pl.ds(tl - SUBLANES, SUBLANES)
            dgn, dvn, _, _ = unit(g_ref[tail, ls], gn_ref[:, ls], v_ref[tail, ls], vn_ref[:, ls], dan_ref[:, ls])
            dgn = jnp.where(i < nb - 1, dgn, 0.0)
            dvn = jnp.where(i < nb - 1, dvn, 0.0)
            gp0 = jnp.where(i > 0, gp_ref[:, ls], 0.0)
            vp0 = jnp.where(i > 0, vp_ref[:, ls], 0.0)

            def strip(tt, carry):
                dgn, dvn = carry[0], carry[1]
                aw, ab = list(carry[2:8]), list(carry[8:10])
                t = nt - 1 - tt
                nu = FFN_STRIP_BWD // SUBLANES
                rm = pl.multiple_of(jnp.maximum(t * FFN_STRIP_BWD - SUBLANES, 0), SUBLANES)
                gs = [jnp.where(t > 0, g_ref[pl.ds(rm, SUBLANES), ls], gp0)]
                vs = [jnp.where(t > 0, v_ref[pl.ds(rm, SUBLANES), ls], vp0)]
                das = []
                for u in range(nu):
                    rows = pl.ds(pl.multiple_of(t * FFN_STRIP_BWD + u * SUBLANES, SUBLANES), SUBLANES)
                    gs.append(g_ref[rows, ls])
                    vs.append(v_ref[rows, ls])
                    das.append(da_ref[rows, ls])
                dgs, dvs = [None] * nu + [dgn], [None] * nu + [dvn]
                for u in reversed(range(nu)):
                    dgs[u], dvs[u], gsh, vsh = unit(gs[u], gs[u + 1], vs[u], vs[u + 1], das[u])
                    for k in range(3):
                        aw[k] = aw[k] + dgs[u] * gsh[k]
                        aw[3 + k] = aw[3 + k] + dvs[u] * vsh[k]
                    ab[0] = ab[0] + dgs[u]
                    ab[1] = ab[1] + dvs[u]
                for p in range(nu // 2):
                    rows = pl.ds(pl.multiple_of(t * FFN_STRIP_BWD + p * 16, 16), 16)
                    dg_ref[rows, ls] = jnp.concatenate([d_in(dgs[2 * p], dgs[2 * p + 1], wg),
                                                        d_in(dgs[2 * p + 1], dgs[2 * p + 2], wg)], axis=0).astype(BF16)
                    dv_ref[rows, ls] = jnp.concatenate([d_in(dvs[2 * p], dvs[2 * p + 1], wv),
                                                        d_in(dvs[2 * p + 1], dvs[2 * p + 2], wv)], axis=0).astype(BF16)
                return (dgs[0], dvs[0], *aw, *ab)

            res = lax.fori_loop(0, nt, strip, (dgn, dvn) + (zero,) * 8)
            for k in range(3):
                acc_w[0, k, :, ls] += res[2 + k]
                acc_w[1, k, :, ls] += res[5 + k]
            acc_b[0, :, ls] += res[8]
            acc_b[1, :, ls] += res[9]

        @pl.when(i == nb - 1)
        def _():
            for t, (dw_ref, db_ref) in enumerate(((dwg_ref, dbg_ref), (dwv_ref, dbv_ref))):
                dw_ref[...] = jnp.zeros_like(dw_ref)
                for k in range(3):
                    dw_ref[k:k + 1, :] = jnp.sum(acc_w[t, k], axis=0, keepdims=True)
                db_ref[...] = jnp.sum(acc_b[t], axis=0, keepdims=True)

    wspec = lambda off: pl.BlockSpec((SUBLANES, tc), lambda j, i, off=off: (0, j + off))
    bspec = lambda off: pl.BlockSpec((1, tc), lambda j, i, off=off: (0, j + off))
    ospec = lambda off: pl.BlockSpec((tl, tc), lambda j, i, off=off: (i, j + off))
    dg, dv, dwg, dwv, dbg, dbv = _pc(
        body, name=name,
        out_shape=(_sds((n, D_FF), BF16), _sds((n, D_FF), BF16), _sds((SUBLANES, D_FF)), _sds((SUBLANES, D_FF)),
                   _sds((1, D_FF)), _sds((1, D_FF))),
        grid=(ncb, nb),
        in_specs=[spec(tl, cur, 0), spec(hb, nxt, 0),
                  spec(tl, cur, 0), spec(hb, prv, 0), spec(hb, nxt, 0),
                  spec(tl, cur, ncb), spec(hb, prv, ncb), spec(hb, nxt, ncb),
                  wspec(0), wspec(ncb), bspec(0), bspec(ncb)],
        out_specs=(ospec(0), ospec(0), wspec(0), wspec(0), bspec(0), bspec(0)),
        scratch_shapes=[pltpu.VMEM((2, 3, SUBLANES, tc), F32), pltpu.VMEM((2, SUBLANES, tc), F32)],
        compiler_params=_params(2))(da, da, up, up, up, up, up, up, conv_w, conv_w, conv_b, conv_b)
    return dg, dv, jnp.concatenate([dwg, dwv], axis=1), jnp.concatenate([dbg, dbv], axis=1)


def _position():
    return lax.axis_index("x"), lax.axis_index("y"), lax.axis_index("c")


def _all_gather(locals_, name):
    n = len(locals_)

    def body(*refs):
        x_refs, out_refs = refs[:n], refs[n:2 * n]
        send_sems, recv_sems, local_sems = refs[2 * n:]
        x, y, cc = _position()
        me, sibling = (x, y, cc), (x, y, 1 - cc)
        chips = [(1 - x, y), (x, 1 - y), (1 - x, 1 - y)]

        def slot(a, px, py, pc):
            return out_refs[a].at[4 * px + 2 * py + pc]

        def copy(k, a, block, to, own=False):
            return pltpu.make_async_remote_copy(
                src_ref=x_refs[a] if own else slot(a, *block), dst_ref=slot(a, *block),
                send_sem=send_sems.at[k, a], recv_sem=recv_sems.at[k, a], device_id=to, device_id_type=MESH)

        mine = [pltpu.make_async_copy(x_refs[a], slot(a, *me), local_sems.at[a]) for a in range(n)]
        first = [copy(1 + j, a, me, (*chip, cc), own=True) for j, chip in enumerate(chips) for a in range(n)]
        first += [copy(0, a, me, sibling, own=True) for a in range(n)]
        for cp in mine + first:
            cp.start()
        passed = []
        for j, chip in enumerate(chips):
            for a in range(n):
                copy(1 + j, a, (*chip, cc), me).wait_recv()
                cp = copy(4 + j, a, (*chip, cc), sibling)
                cp.start()
                passed.append(cp)
        for a in range(n):
            copy(0, a, sibling, me).wait_recv()
        for j, chip in enumerate(chips):
            for a in range(n):
                copy(4 + j, a, (*chip, 1 - cc), me).wait_recv()
        for cp in first + passed:
            cp.wait_send()
        for cp in mine:
            cp.wait()

    hbm = pl.BlockSpec(memory_space=pl.ANY)
    return _pc(body, name=name, out_shape=[_sds((N_DEV,) + a.shape, a.dtype) for a in locals_],
               in_specs=[hbm] * n, out_specs=[hbm] * n,
               scratch_shapes=[pltpu.SemaphoreType.DMA((7, n)), pltpu.SemaphoreType.DMA((7, n)),
                               pltpu.SemaphoreType.DMA((n,))])(*locals_)


def _peers():
    x, y, cc = _position()
    others = []
    for fx, fy, fc in ((0, 0, 1), (1, 0, 0), (0, 1, 0), (1, 1, 0), (1, 0, 1), (0, 1, 1), (1, 1, 1)):
        p = (1 - x if fx else x, 1 - y if fy else y, 1 - cc if fc else cc)
        others.append((p, 4 * p[0] + 2 * p[1] + p[2]))
    return 4 * x + 2 * y + cc, others


def _gather_start(locals_, after, name):
    n = len(locals_)
    me, _ = _peers()
    lands = [lax.dynamic_update_slice(lax.empty((N_DEV,) + a.shape, a.dtype), a[None], (me,) + (0,) * a.ndim)
             for a in locals_]

    def body(*refs):
        x_refs, land_refs = refs[:n], refs[n:2 * n]
        send_sems, recv_sems, token = refs[2 * n + 1], refs[2 * n + 2], refs[-1]
        me_idx, others = _peers()
        for k, (peer, _) in enumerate(others):
            for a in range(n):
                pltpu.make_async_remote_copy(
                    src_ref=x_refs[a], dst_ref=land_refs[a].at[me_idx], send_sem=send_sems.at[k * n + a],
                    recv_sem=recv_sems.at[k * n + a], device_id=peer, device_id_type=MESH).start()
        token[...] = jnp.zeros_like(token)

    hbm = pl.BlockSpec(memory_space=pltpu.HBM)
    sem = pl.BlockSpec(memory_space=pltpu.SEMAPHORE)
    out = _pc(body, name=name,
              out_shape=(pltpu.SemaphoreType.DMA((7 * n,)), pltpu.SemaphoreType.DMA((7 * n,)),
                         *[pltpu.HBM(a.shape, a.dtype) for a in locals_], *[pltpu.HBM(l.shape, l.dtype) for l in lands],
                         _sds((SUBLANES, LANES))),
              in_specs=[hbm] * (2 * n) + [pl.BlockSpec(memory_space=pl.ANY)],
              out_specs=(sem, sem, *([hbm] * (2 * n)), pl.BlockSpec(memory_space=pltpu.VMEM)),
              input_output_aliases={i: 2 + i for i in range(2 * n)},
              compiler_params=pltpu.CompilerParams(has_side_effects=pltpu.SideEffectType.DATAFLOW_SIDE_EFFECTING))(
        *[pltpu.with_memory_space_constraint(a, pltpu.HBM) for a in locals_],
        *[pltpu.with_memory_space_constraint(l, pltpu.HBM) for l in lands], after)
    return (out[0], out[1], list(out[2:2 + n]), list(out[2 + n:2 + 2 * n])), out[-1]


def _gather_wait(state, after, name):
    send_sems, recv_sems, x_thru, land_thru = state
    n = len(x_thru)

    def body(*refs):
        x_refs, land_refs = refs[:n], refs[n:2 * n]
        send_sems, recv_sems = refs[2 * n], refs[2 * n + 1]
        _, others = _peers()
        for k, (peer, peer_idx) in enumerate(others):
            for a in range(n):
                cp = pltpu.make_async_remote_copy(
                    src_ref=x_refs[a], dst_ref=land_refs[a].at[peer_idx], send_sem=send_sems.at[k * n + a],
                    recv_sem=recv_sems.at[k * n + a], device_id=peer, device_id_type=MESH)
                cp.wait_send()
                cp.wait_recv()

    hbm = pl.BlockSpec(memory_space=pltpu.HBM)
    sem = pl.BlockSpec(memory_space=pltpu.SEMAPHORE)
    out = _pc(body, name=name, out_shape=tuple(pltpu.HBM(a.shape, a.dtype) for a in x_thru + land_thru),
              in_specs=[hbm] * (2 * n) + [sem, sem, pl.BlockSpec(memory_space=pl.ANY)], out_specs=tuple([hbm] * (2 * n)),
              input_output_aliases={i: i for i in range(2 * n)},
              compiler_params=pltpu.CompilerParams(has_side_effects=pltpu.SideEffectType.DATAFLOW_SIDE_EFFECTING))(
        *x_thru, *land_thru, send_sems, recv_sems, after)
    return list(out[n:])


N_CHIPS = 4


def _sibling_swap(parts, name):
    n = len(parts)

    def body(*refs):
        g_refs, got_refs = refs[:n], refs[n:2 * n]
        send_sems, recv_sems = refs[2 * n:]
        x, y, cc = _position()
        swaps = []
        for q in range(N_CHIPS):
            for a in range(n):
                swaps.append(pltpu.make_async_remote_copy(
                    src_ref=g_refs[a].at[2 * q + 1 - cc], dst_ref=got_refs[a].at[q], send_sem=send_sems.at[q, a],
                    recv_sem=recv_sems.at[q, a], device_id=(x, y, 1 - cc), device_id_type=MESH))
        for cp in swaps:
            cp.start()
        for cp in swaps:
            cp.wait_recv()
        for cp in swaps:
            cp.wait_send()

    hbm = pl.BlockSpec(memory_space=pl.ANY)
    return _pc(body, name=name, out_shape=[_sds((N_CHIPS,) + a.shape[1:], a.dtype) for a in parts],
               in_specs=[hbm] * n, out_specs=[hbm] * n,
               scratch_shapes=[pltpu.SemaphoreType.DMA((N_CHIPS, n)), pltpu.SemaphoreType.DMA((N_CHIPS, n))])(*parts)


def _pair_add(part, got, core, name):
    q, a, b = got.shape
    ta = _block_rows(a, b)

    def body(core_ref, k_ref, g_ref, o_ref):
        o_ref[...] = (k_ref[...].astype(F32) + g_ref[...].astype(F32)).astype(BF16)

    spec = pl.BlockSpec((None, ta, b), lambda c, i, core_ref: (c, i, 0))
    own = pl.BlockSpec((None, None, ta, b), lambda c, i, core_ref: (c, core_ref[0], i, 0))
    grid_spec = pltpu.PrefetchScalarGridSpec(num_scalar_prefetch=1, grid=(q, a // ta), in_specs=[own, spec],
                                             out_specs=spec)
    return _pc(body, name=name, out_shape=_sds(got.shape, BF16), grid_spec=grid_spec, compiler_params=_params(2))(
        core, part.reshape((N_CHIPS, 2) + part.shape[1:]), got)


def _chip_exchange(sums, name):
    n = len(sums)

    def body(*refs):
        g_refs, out_refs = refs[:n], refs[n:2 * n]
        send_sems, recv_sems = refs[2 * n:]
        me_q, others = _chip_peers()
        sends, recvs = [], []
        for k, (peer, peer_q) in enumerate(others):
            for a in range(n):
                sends.append(pltpu.make_async_remote_copy(
                    src_ref=g_refs[a].at[peer_q], dst_ref=out_refs[a].at[me_q], send_sem=send_sems.at[k, a],
                    recv_sem=recv_sems.at[k, a], device_id=peer, device_id_type=MESH))
                recvs.append(pltpu.make_async_remote_copy(
                    src_ref=g_refs[a].at[me_q], dst_ref=out_refs[a].at[peer_q], send_sem=send_sems.at[k, a],
                    recv_sem=recv_sems.at[k, a], device_id=peer, device_id_type=MESH))
        for cp in sends:
            cp.start()
        for cp in recvs:
            cp.wait_recv()
        for cp in sends:
            cp.wait_send()

    hbm = pl.BlockSpec(memory_space=pl.ANY)
    return _pc(body, name=name, out_shape=[_sds(a.shape, a.dtype) for a in sums],
               in_specs=[hbm] * n, out_specs=[hbm] * n,
               scratch_shapes=[pltpu.SemaphoreType.DMA((3, n)), pltpu.SemaphoreType.DMA((3, n))])(*sums)


def _chip_peers():
    x, y, cc = _position()
    others = []
    for fx, fy in ((1, 0), (0, 1), (1, 1)):
        px, py = (1 - x if fx else x), (1 - y if fy else y)
        others.append(((px, py, cc), 2 * px + py))
    return 2 * x + y, others


def _chip_exchange_start(sums, name):
    n = len(sums)
    lands = [lax.empty(a.shape, a.dtype) for a in sums]

    def body(*refs):
        g_refs, land_refs = refs[:n], refs[n:2 * n]
        send_sems, recv_sems, token = refs[2 * n], refs[2 * n + 1], refs[-1]
        me_q, others = _chip_peers()
        for k, (peer, peer_q) in enumerate(others):
            for a in range(n):
                pltpu.make_async_remote_copy(
                    src_ref=g_refs[a].at[peer_q], dst_ref=land_refs[a].at[me_q], send_sem=send_sems.at[k * n + a],
                    recv_sem=recv_sems.at[k * n + a], device_id=peer, device_id_type=MESH).start()
        token[...] = jnp.zeros_like(token)

    hbm = pl.BlockSpec(memory_space=pltpu.HBM)
    sem = pl.BlockSpec(memory_space=pltpu.SEMAPHORE)
    out = _pc(body, name=name,
              out_shape=(pltpu.SemaphoreType.DMA((3 * n,)), pltpu.SemaphoreType.DMA((3 * n,)),
                         *[pltpu.HBM(a.shape, a.dtype) for a in sums], *[pltpu.HBM(a.shape, a.dtype) for a in sums],
                         _sds((SUBLANES, LANES))),
              in_specs=[hbm] * (2 * n), out_specs=(sem, sem, *([hbm] * (2 * n)), pl.BlockSpec(memory_space=pltpu.VMEM)),
              input_output_aliases={i: 2 + i for i in range(2 * n)},
              compiler_params=pltpu.CompilerParams(has_side_effects=pltpu.SideEffectType.DATAFLOW_SIDE_EFFECTING))(
        *[pltpu.with_memory_space_constraint(a, pltpu.HBM) for a in sums],
        *[pltpu.with_memory_space_constraint(l, pltpu.HBM) for l in lands])
    return (out[0], out[1], list(out[2:2 + n]), list(out[2 + n:2 + 2 * n])), out[-1]


def _chip_exchange_wait(state, after, name):
    send_sems, recv_sems, g_thru, land_thru = state
    n = len(g_thru)

    def body(*refs):
        g_refs, land_refs = refs[:n], refs[n:2 * n]
        send_sems, recv_sems = refs[2 * n], refs[2 * n + 1]
        me_q, others = _chip_peers()
        for k, (peer, peer_q) in enumerate(others):
            for a in range(n):
                cp = pltpu.make_async_remote_copy(
                    src_ref=g_refs[a].at[me_q], dst_ref=land_refs[a].at[peer_q], send_sem=send_sems.at[k * n + a],
                    recv_sem=recv_sems.at[k * n + a], device_id=peer, device_id_type=MESH)
                cp.wait_send()
                cp.wait_recv()

    hbm = pl.BlockSpec(memory_space=pltpu.HBM)
    sem = pl.BlockSpec(memory_space=pltpu.SEMAPHORE)
    out = _pc(body, name=name, out_shape=tuple(pltpu.HBM(a.shape, a.dtype) for a in g_thru + land_thru),
              in_specs=[hbm] * (2 * n) + [sem, sem, pl.BlockSpec(memory_space=pl.ANY)], out_specs=tuple([hbm] * (2 * n)),
              input_output_aliases={i: i for i in range(2 * n)},
              compiler_params=pltpu.CompilerParams(has_side_effects=pltpu.SideEffectType.DATAFLOW_SIDE_EFFECTING))(
        *g_thru, *land_thru, send_sems, recv_sems, after)
    return list(out[:n]), list(out[n:])


def _block_rows(a, b):
    ta = a
    while ta * b > 256 * 1024 and ta % 32 == 0:
        ta //= 2
    return ta


def _reduce_adamw(parts, w, m, v, name):
    n_parts, s, a, b = parts.shape
    ta = _block_rows(a, b)

    def body(p_ref, w_ref, m_ref, v_ref, g_out, d_out, m_out, v_out):
        g = p_ref[0].astype(F32)
        for j in range(1, n_parts):
            g = g + p_ref[j].astype(F32)
        delta, m_new, v_new = _adamw(g, w_ref[...], m_ref[...], v_ref[...])
        g_out[...] = g
        d_out[...] = delta
        m_out[...] = m_new
        v_out[...] = v_new

    spec = pl.BlockSpec((None, ta, b), lambda l, i: (l, i, 0))
    return _pc(body, name=name, out_shape=(_sds((s, a, b)),) * 4, grid=(s, a // ta),
               in_specs=[pl.BlockSpec((n_parts, None, ta, b), lambda l, i: (0, l, i, 0)), spec, spec, spec],
               out_specs=(spec,) * 4, compiler_params=_params(2))(parts, w, m, v)


def _adamw(g, w, m, v):
    c1 = 1.0 - ADAM_B1 ** ADAM_STEP
    c2 = 1.0 - ADAM_B2 ** ADAM_STEP
    m_new = ADAM_B1 * m + (1.0 - ADAM_B1) * g
    v_new = ADAM_B2 * v + (1.0 - ADAM_B2) * (g * g)
    delta = -ADAM_LR * ((m_new / c1) / (jnp.sqrt(v_new / c2) + ADAM_EPS) + ADAM_WD * w)
    return delta, m_new, v_new


def _chip_reduce_adamw(own, recv, w, m, v, chip, name):
    s, a, b = w.shape
    ta = _block_rows(a, b)

    def body(chip_ref, *refs):
        p_refs, (w_ref, m_ref, v_ref), (g_out, d_out, m_out, v_out) = refs[:4 * s], refs[4 * s:4 * s + 3], refs[4 * s + 3:]
        layer = pl.program_id(0)
        g = None
        for l in range(s):
            gl = p_refs[4 * l][...].astype(F32)
            for j in range(1, N_CHIPS):
                gl = gl + p_refs[4 * l + j][...].astype(F32)
            g = gl if g is None else jnp.where(layer == l, gl, g)
        delta, m_new, v_new = _adamw(g, w_ref[...], m_ref[...], v_ref[...])
        g_out[...] = g
        d_out[...] = delta
        m_out[...] = m_new
        v_out[...] = v_new

    def part_spec(l, j):
        return pl.BlockSpec((None, ta, b), lambda layer, i, chip_ref, l=l, j=j: (
            (chip_ref[0] + j) % N_CHIPS, jnp.where(layer == l, i, 0), 0))

    spec = pl.BlockSpec((None, ta, b), lambda layer, i, chip_ref: (layer, i, 0))
    in_specs, args = [], []
    for l in range(s):
        for j in range(N_CHIPS):
            in_specs.append(part_spec(l, j))
            args.append(own[l] if j == 0 else recv[l])
    grid_spec = pltpu.PrefetchScalarGridSpec(num_scalar_prefetch=1, grid=(s, a // ta), in_specs=in_specs + [spec] * 3,
                                             out_specs=(spec,) * 4)
    return _pc(body, name=name, out_shape=(_sds((s, a, b)),) * 4, grid_spec=grid_spec, compiler_params=_params(2))(
        chip, *args, w, m, v)


MATRICES = (("ada_mix_w", 2), ("w_in", 2), ("w_ssd_out", 1), ("w_conf_out", 2), ("w_sc_out", 2), ("w_o", 1),
            ("ada_ffn_w", 2), ("w_up", 2), ("w_down", 1))
MIXER_MATRICES = ("ada_mix_w", "w_in", "w_ssd_out", "w_conf_out", "w_sc_out", "w_o")
FFN_MATRICES = ("ada_ffn_w", "w_up", "w_down")
CONV_WEIGHTS = (("ssd_conv_w", 2), ("conf_conv_w", 2), ("sc_conv_w", 2), ("ffn_conv_w", 2))
SHARDED = MATRICES + CONV_WEIGHTS
REPLICATED = ("ada_mix_b", "norm_mix_g", "b_gate", "ssd_conv_b", "ssd_dt_bias", "ssd_a_log", "ssd_d", "ssd_norm_g",
              "conf_conv_b", "conf_ln_g", "conf_ln_b", "ada_ffn_b", "norm_ffn_g", "ffn_conv_b", "final_norm_g")
WEIGHT_NAMES = ("ada_mix_w", "ada_mix_b", "norm_mix_g", "w_in", "b_gate", "ssd_conv_w", "ssd_conv_b", "ssd_dt_bias",
                "ssd_a_log", "ssd_d", "ssd_norm_g", "w_ssd_out", "conf_conv_w", "conf_conv_b", "conf_ln_g",
                "conf_ln_b", "w_conf_out", "sc_conv_w", "w_sc_out", "w_o", "ada_ffn_w", "ada_ffn_b", "norm_ffn_g",
                "w_up", "ffn_conv_w", "ffn_conv_b", "w_down", "final_norm_g")


def _pack_flat(arrays, cols, row_multiple, dtype):
    flat = jnp.concatenate([a.reshape(-1).astype(dtype) for a in arrays])
    rows = -(-flat.shape[0] // cols)
    rows = -(-rows // row_multiple) * row_multiple
    return jnp.pad(flat, (0, rows * cols - flat.shape[0])).reshape(rows, cols)


def _unpack_flat(flat2d, shapes):
    flat = flat2d.reshape(-1)
    out, off = [], 0
    for s in shapes:
        n = 1
        for d in s:
            n *= d
        out.append(flat[off:off + n].reshape(s))
        off += n
    return out


def _cols(g, lo, hi):
    b = g.shape[-1]
    pieces = []
    for k in range(N_DEV):
        a, e = max(lo, k * b), min(hi, (k + 1) * b)
        if a < e:
            pieces.append(g[k, :, a - k * b:e - k * b])
    return pieces[0] if len(pieces) == 1 else jnp.concatenate(pieces, axis=1)


def _rows(g):
    return g.reshape(N_DEV * g.shape[1], g.shape[2])


def _col_shards(segs, b):
    shards = []
    for k in range(N_DEV):
        lo, hi = k * b, (k + 1) * b
        pieces, off = [], 0
        for seg in segs:
            n = seg.shape[1]
            a, e = max(lo, off), min(hi, off + n)
            if a < e:
                pieces.append(seg[:, a - off:e - off])
            off += n
        shards.append(pieces[0] if len(pieces) == 1 else jnp.concatenate(pieces, axis=1))
    return jnp.stack(shards)


def _row_shards(full):
    return full.reshape(N_DEV, full.shape[0] // N_DEV, full.shape[1])


def _pad_rows(a, rows):
    return jnp.pad(a, ((0, rows - a.shape[0]), (0, 0)))


def _pad_lanes(a):
    return jnp.pad(a, ((0, 0), (0, LANES - a.shape[1])))


def _whole(g):
    return _cols(g, 0, N_DEV * g.shape[-1])


def _mixer_weights(full, i):
    row = lambda name: full[name][i].reshape(1, -1)
    conv = lambda name: _whole(full[name][:, i])
    w_in = full["w_in", i]
    return {
        "ada_mix_w": _whole(full["ada_mix_w", i]), "ada_mix_b": row("ada_mix_b"), "norm_mix_g": row("norm_mix_g"),
        "w_z": _cols(w_in, 0, OFF_Z), "w_xbc": _cols(w_in, OFF_Z, OFF_XBC),
        "w_dt": _pad_lanes(_cols(w_in, OFF_XBC, OFF_DT)), "w_conf": _cols(w_in, OFF_DT, OFF_CONF),
        "w_sc": _cols(w_in, OFF_CONF, OFF_SC), "w_gates": _cols(w_in, OFF_SC, N_IN),
        "b_gate": row("b_gate"),
        "ssd_conv_w": _pad_rows(conv("ssd_conv_w"), SUBLANES), "ssd_conv_b": row("ssd_conv_b"),
        "dt_bias": _pad_lanes(row("ssd_dt_bias")), "a_log": _pad_lanes(row("ssd_a_log")),
        "ssd_d": _pad_lanes(row("ssd_d")), "ssd_norm_g": row("ssd_norm_g"), "w_ssd_out": _rows(full["w_ssd_out", i]),
        "conf_conv_w": _pad_rows(conv("conf_conv_w"), CONF_HALO), "conf_conv_b": row("conf_conv_b"),
        "conf_ln_g": row("conf_ln_g"), "conf_ln_b": row("conf_ln_b"), "w_conf_out": _whole(full["w_conf_out", i]),
        "sc_conv_w": _pad_rows(conv("sc_conv_w"), SUBLANES), "w_sc_out": _whole(full["w_sc_out", i]),
        "w_o": _rows(full["w_o", i]),
    }


def _ffn_weights(full, i):
    row = lambda name: full[name][i].reshape(1, -1)
    return {
        "ada_ffn_w": _whole(full["ada_ffn_w", i]), "ada_ffn_b": row("ada_ffn_b"), "norm_ffn_g": row("norm_ffn_g"),
        "w_up": _whole(full["w_up", i]), "ffn_conv_w": _pad_rows(_whole(full["ffn_conv_w"][:, i]), SUBLANES),
        "ffn_conv_b": row("ffn_conv_b"), "w_down": _rows(full["w_down", i]),
    }


def _adaln(sc8, w, b, name):
    mod = _matmul(sc8, w, "nn", F32, name)[0:1, :] + b
    return mod[:, :D_MODEL], mod[:, D_MODEL:2 * D_MODEL], mod[:, 2 * D_MODEL:]


def _mixer_fwd(i, x, prev, sc8, wl):
    t = f"l{i}_"
    s = {}
    shift, scale, gate = _adaln(sc8, wl["ada_mix_w"], wl["ada_mix_b"], t + "ada_mix")
    if prev is None:
        s["x_in"] = x
        s["h"] = _prenorm_first(x, wl["norm_mix_g"], scale, shift, t + "norm_mix")
    else:
        s["x_in"], s["h"] = _prenorm_res(x, prev[0], prev[1], wl["norm_mix_g"], scale, shift, t + "norm_mix")
    s["scale_mix"], s["gate_mix"] = scale, gate
    h = s["h"]
    s["z"] = _matmul(h, wl["w_z"], "nn", F32, t + "in_z")
    s["xbc"] = _matmul(h, wl["w_xbc"], "nn", F32, t + "in_xbc")
    s["dt_raw"] = _matmul(h, wl["w_dt"], "nn", F32, t + "in_dt")
    s["conf"] = _matmul(h, wl["w_conf"], "nn", F32, t + "in_conf")
    s["sc"] = _matmul(h, wl["w_sc"], "nn", F32, t + "in_sc")
    s["gates"] = _matmul(h, wl["w_gates"], "nn", F32, t + "in_gates")
    s["pre"], s["dt"] = _ssd_pre(s["xbc"], s["dt_raw"], wl["ssd_conv_w"], wl["ssd_conv_b"], wl["dt_bias"],
                                 t + "ssd_pre")
    s["y"], s["hprev"] = _ssd_scan(s["pre"], s["dt"], wl["a_log"], wl["ssd_d"], t + "ssd_scan")
    s["ya_in"] = _ssd_post(s["y"], s["z"], wl["ssd_norm_g"], t + "ssd_post")
    s["yb_in"], s["uc"] = _conf_fwd(s["conf"], wl["conf_conv_w"], wl["conf_conv_b"], wl["conf_ln_g"],
                                    wl["conf_ln_b"], t + "conf")
    s["yc_in"] = _sc_fwd(s["sc"], wl["sc_conv_w"], t + "sconv")
    s["ya"] = _matmul(s["ya_in"], wl["w_ssd_out"], "nn", F32, t + "ssd_out")
    s["yb"] = _matmul(s["yb_in"], wl["w_conf_out"], "nn", F32, t + "conf_out")
    s["yc"] = _matmul(s["yc_in"], wl["w_sc_out"], "nn", F32, t + "sc_out")
    s["merged"] = _merge_fwd(s["gates"], s["ya"], s["yb"], s["yc"], wl["b_gate"], t + "merge")
    s["mix"] = _matmul(s["merged"], wl["w_o"], "nn", F32, t + "w_o")
    return s


def _ffn_fwd(i, s, sc8, wl):
    t = f"l{i}_"
    shift2, scale2, gate2 = _adaln(sc8, wl["ada_ffn_w"], wl["ada_ffn_b"], t + "ada_ffn")
    s["x_mid"], s["h2"] = _prenorm_res(s["x_in"], s["mix"], s["gate_mix"], wl["norm_ffn_g"], scale2, shift2,
                                       t + "norm_ffn")
    s["scale_ffn"], s["gate_ffn"] = scale2, gate2
    s["up"] = _matmul(s["h2"], wl["w_up"], "nn", F32, t + "w_up")
    s["a"] = _ffn_mid(s["up"], wl["ffn_conv_w"], wl["ffn_conv_b"], t + "ffn_mid")
    s["out"] = _matmul(s["a"], wl["w_down"], "nn", F32, t + "w_down")
    return s


def _layer_bwd(i, s, wl, sc8, dys_ffn, dx_after, dgate_ffn, prev, emit=None):
    t = f"l{i}_b_"
    g = {}
    da = _matmul(dys_ffn, wl["w_down"], "nt", F32, t + "d_a")
    g["w_down"] = _matmul(s["a"], dys_ffn, "tn", BF16, t + "dw_down")
    dug, duv, dfw, g["ffn_conv_b"] = _ffn_mid_bwd(da, s["up"], wl["ffn_conv_w"], wl["ffn_conv_b"], t + "ffn_mid")
    g["ffn_conv_w"] = dfw[:3]
    dh2 = _matmul_sum_nt([dug, duv], [wl["w_up"][:, :D_FF], wl["w_up"][:, D_FF:]], t + "d_h2")
    g["w_up"] = [_matmul(s["h2"], dug, "tn", BF16, t + "dw_up_g"), _matmul(s["h2"], duv, "tn", BF16, t + "dw_up_v")]
    dx_mid, dshift2, dscale2, g["norm_ffn_g"], dys_mix, dgate_mix = _norm_bwd(
        dh2, s["x_mid"], dx_after, wl["norm_ffn_g"], s["scale_ffn"], t + "norm_ffn", s["mix"], s["gate_mix"])
    dmod_ffn = jnp.concatenate([dshift2, dscale2, dgate_ffn], axis=1)
    g["ada_ffn_b"] = dmod_ffn
    g["ada_ffn_w"] = [_matmul(sc8, _pad_rows(dmod_ffn, SUBLANES), "tn", BF16, t + "dw_ada_ffn")]
    token = emit(i, "ffn", g) if emit is not None else None
    if token is not None:
        wl = {**wl, "b_gate": wl["b_gate"] + token}
    dmerged = _matmul(dys_mix, wl["w_o"], "nt", F32, t + "d_merged")
    g["w_o"] = _matmul(s["merged"], dys_mix, "tn", BF16, t + "dw_o")
    dya, dyb, dyc, dgates, g["b_gate"] = _merge_bwd(dmerged, s["gates"], s["ya"], s["yb"], s["yc"], wl["b_gate"],
                                                    t + "merge")
    dya_in = _matmul(dya, wl["w_ssd_out"], "nt", F32, t + "d_ya_in")
    g["w_ssd_out"] = _matmul(s["ya_in"], dya, "tn", BF16, t + "dw_ssd_out")
    dyb_in = _matmul(dyb, wl["w_conf_out"], "nt", F32, t + "d_yb_in")
    g["w_conf_out"] = [_matmul(s["yb_in"], dyb, "tn", BF16, t + "dw_conf_out")]
    dyc_in = _matmul(dyc, wl["w_sc_out"], "nt", F32, t + "d_yc_in")
    g["w_sc_out"] = [_matmul(s["yc_in"], dyc, "tn", BF16, t + "dw_sc_out")]
    dy, dz, g["ssd_norm_g"] = _ssd_post_bwd(dya_in, s["y"], s["z"], wl["ssd_norm_g"], t + "ssd_post")
    dpre, ddt, da_log, dd = _ssd_scan_bwd(s["pre"], s["dt"], s["hprev"], dy, wl["a_log"], wl["ssd_d"],
                                          t + "ssd_scan")
    g["ssd_a_log"], g["ssd_d"] = da_log[:, :SSD_HEADS], dd[:, :SSD_HEADS]
    dxbc, ddt_raw, dcw, g["ssd_conv_b"], ddtb = _ssd_pre_bwd(dpre, s["xbc"], ddt, s["dt_raw"], wl["ssd_conv_w"],
                                                             wl["dt_bias"], t + "ssd_pre")
    g["ssd_conv_w"], g["ssd_dt_bias"] = dcw[:4], ddtb[:, :SSD_HEADS]
    dconf, dccw, g["conf_conv_b"], g["conf_ln_g"], g["conf_ln_b"] = _conf_bwd(
        dyb_in, s["uc"], s["conf"], wl["conf_conv_w"], wl["conf_ln_g"], wl["conf_ln_b"], t + "conf")
    g["conf_conv_w"] = dccw[:CONF_KERNEL]
    dsc, dscw = _sc_bwd(dyc_in, s["sc"], wl["sc_conv_w"], t + "sconv")
    g["sc_conv_w"] = dscw[:3]
    segs = (("z", dz, "w_z"), ("xbc", dxbc, "w_xbc"), ("dt", ddt_raw, "w_dt"), ("conf", dconf, "w_conf"),
            ("sc", dsc, "w_sc"), ("gates", dgates, "w_gates"))
    dw_segs = []
    for nm, dseg, wname in segs:
        dw = _matmul(s["h"], dseg, "tn", BF16, t + "dw_in_" + nm)
        dw_segs.append(dw[:, :SSD_HEADS] if nm == "dt" else dw)
    g["w_in"] = dw_segs
    token = emit(i, "mixer_early", g) if emit is not None else None
    if token is not None:
        wl = {**wl, "w_dt": wl["w_dt"] + token.astype(BF16)}
    dh = _matmul_sum_nt([dseg for _, dseg, _ in segs], [wl[wname] for _, _, wname in segs], t + "d_h")
    if prev is None:
        dx_in, dshift, dscale, g["norm_mix_g"] = _norm_bwd(dh, s["x_in"], dx_mid, wl["norm_mix_g"], s["scale_mix"],
                                                          t + "norm_mix")
        back = None
    else:
        dx_in, dshift, dscale, g["norm_mix_g"], dys_prev, dgate_prev = _norm_bwd(
            dh, s["x_in"], dx_mid, wl["norm_mix_g"], s["scale_mix"], t + "norm_mix", prev[0], prev[1])
        back = (dys_prev, dgate_prev)
    dmod_mix = jnp.concatenate([dshift, dscale, dgate_mix], axis=1)
    g["ada_mix_b"] = dmod_mix
    g["ada_mix_w"] = [_matmul(sc8, _pad_rows(dmod_mix, SUBLANES), "tn", BF16, t + "dw_ada_mix")]
    return g, dx_in, back


def _device_step(x, c, target, full, fetch=None, emit=None):
    fetch = fetch or {}
    full = dict(full)
    sc8 = _pad_rows(c * (1.0 / (1.0 + jnp.exp(-c))), SUBLANES)
    wls, saved, prev, xcur = [], [], None, x
    for i in range(DEPTH):
        if (i, "mixer") in fetch:
            full.update(fetch[i, "mixer"](prev[0]))
        wl = _mixer_weights(full, i)
        s = _mixer_fwd(i, xcur, prev, sc8, wl)
        if (i, "ffn") in fetch:
            full.update(fetch[i, "ffn"](s["mix"]))
        wf = _ffn_weights(full, i)
        _ffn_fwd(i, s, sc8, wf)
        wls.append({**wl, **wf})
        saved.append(s)
        xcur, prev = s["x_mid"], (s["out"], s["gate_ffn"])
    gf = full["final_norm_g"].reshape(1, -1)
    last = saved[-1]
    loss, dx, dys, dgate, dgf = _final_loss(last["x_mid"], last["out"], last["gate_ffn"], gf, target, "final_loss")
    grads = [None] * DEPTH
    for i in reversed(range(DEPTH)):
        prev = None if i == 0 else (saved[i - 1]["out"], saved[i - 1]["gate_ffn"])
        grads[i], dx, back = _layer_bwd(i, saved[i], wls[i], sc8, dys, dx, dgate, prev, emit)
        token = emit(i, "mixer", grads[i]) if emit is not None else None
        if token is not None:
            wls[i - 1] = {**wls[i - 1], "ffn_conv_b": wls[i - 1]["ffn_conv_b"] + token}
        if back is not None:
            dys, dgate = back
    return loss[0, 0], dx, grads, dgf


def _step(x, c, target, weights, moments_m, moments_v):
    sharded_names = [n for n, _ in SHARDED]
    conv_names = [n for n, _ in CONV_WEIGHTS]
    shard = lambda key: weights[key[0]][key[1]].astype(BF16)
    first = [(n, 0) for n in MIXER_MATRICES]
    later = {(0, "ffn"): [(n, 0) for n in FFN_MATRICES], (1, "mixer"): [(n, 1) for n in MIXER_MATRICES + FFN_MATRICES]}
    gathered = _all_gather([shard(k) for k in first] + [weights[n] for n in conv_names], "gather_first")
    full = {n: weights[n] for n in REPLICATED}
    full.update(zip(first + conv_names, gathered))
    fetch, after = {}, gathered[0]
    for stage, keys in later.items():
        state, after = _gather_start([shard(k) for k in keys], after, f"gather_l{stage[0]}_{stage[1]}_start")
        fetch[stage] = functools.partial(
            lambda act, state, keys, nm: dict(zip(keys, _gather_wait(state, act, nm))),
            state=state, keys=keys, nm=f"gather_l{stage[0]}_{stage[1]}_wait")
    axis_of = dict(SHARDED)
    core = lax.axis_index("c").astype(jnp.int32).reshape(1)
    chip = (2 * lax.axis_index("x") + lax.axis_index("y")).astype(jnp.int32).reshape(1)
    ffn_names = list(FFN_MATRICES) + ["ffn_conv_w"]
    early_names = [n for n in sharded_names if n not in ffn_names and n != "ada_mix_w"]
    sums, received, pending = {}, {}, []

    def send(i, names, grads_i, last, tag):
        keys = [(n, i) for n in names]
        parts = []
        for n in names:
            gw = grads_i[n]
            part = _row_shards(gw) if axis_of[n] == 1 else _col_shards(gw if isinstance(gw, list) else [gw],
                                                                       weights[n].shape[-1])
            parts.append(part.astype(BF16))
        got = _sibling_swap(parts, "swap_grads_" + tag)
        pair = [_pair_add(p, g, core, f"pair_add_{n}_{i}") for n, p, g in zip(names, parts, got)]
        if last:
            sums.update(zip(keys, pair))
            received.update(zip(keys, _chip_exchange(pair, "exchange_grads_" + tag)))
            return None
        state, token = _chip_exchange_start(pair, "exchange_grads_" + tag + "_start")
        pending.append((keys, state, tag))
        return token[0:1, 0:1]

    def emit(i, kind, grads_i):
        if i == 0 and kind == "ffn":
            return send(0, ffn_names, grads_i, False, "l0_ffn")
        if i == 0 and kind == "mixer_early":
            return send(0, early_names, grads_i, False, "l0_mixer")
        if i == 0 and kind == "mixer":
            return send(0, ["ada_mix_w"], grads_i, True, "l0_ada")
        if kind == "mixer":
            return send(i, sharded_names, grads_i, False, f"l{i}_all")
        return None

    loss, grad_x, grads, dgf = _device_step(x[0], c + after[0:1, 0:1], target[0], full, fetch, emit)
    for keys, state, tag in pending:
        own, got = _chip_exchange_wait(state, grad_x, "exchange_grads_" + tag + "_wait")
        sums.update(zip(keys, own))
        received.update(zip(keys, got))
    big = {n: _chip_reduce_adamw([sums[n, i] for i in range(DEPTH)], [received[n, i] for i in range(DEPTH)],
                                 weights[n], moments_m[n], moments_v[n], chip, "adamw_" + n)
           for n in sharded_names}
    rep_grads = [dgf if n == "final_norm_g" else jnp.stack([grads[i][n].reshape(-1) for i in range(DEPTH)])
                 for n in REPLICATED]
    small_parts, = _all_gather([_pack_flat(rep_grads, LANES, SUBLANES, F32)], "gather_small_grads")
    pack_s = lambda d: _pack_flat([d[n] for n in REPLICATED], LANES, SUBLANES, F32)[None]
    small = _reduce_adamw(small_parts[:, None], pack_s(weights), pack_s(moments_m), pack_s(moments_v),
                          "adamw_replicated")
    small = [_unpack_flat(b, [weights[n].shape for n in REPLICATED]) for b in small]
    results = []
    for kind in range(4):
        by_name = {n: big[n][kind] for n in sharded_names}
        by_name.update(zip(REPLICATED, small[kind]))
        results.append([by_name[n] for n in WEIGHT_NAMES])
    loss = lax.psum(loss, ("x", "y", "c"))
    return (loss, grad_x[None], *results[0], *results[1], *results[2], *results[3])


def kernel(x, c, ada_mix_w, ada_mix_b, norm_mix_g, w_in, b_gate, ssd_conv_w, ssd_conv_b, ssd_dt_bias, ssd_a_log, ssd_d, ssd_norm_g, w_ssd_out, conf_conv_w, conf_conv_b, conf_ln_g, conf_ln_b, w_conf_out, sc_conv_w, w_sc_out, w_o, ada_ffn_w, ada_ffn_b, norm_ffn_g, w_up, ffn_conv_w, ffn_conv_b, w_down, final_norm_g, loss_target, m_ada_mix_w, m_ada_mix_b, m_norm_mix_g, m_w_in, m_b_gate, m_ssd_conv_w, m_ssd_conv_b, m_ssd_dt_bias, m_ssd_a_log, m_ssd_d, m_ssd_norm_g, m_w_ssd_out, m_conf_conv_w, m_conf_conv_b, m_conf_ln_g, m_conf_ln_b, m_w_conf_out, m_sc_conv_w, m_w_sc_out, m_w_o, m_ada_ffn_w, m_ada_ffn_b, m_norm_ffn_g, m_w_up, m_ffn_conv_w, m_ffn_conv_b, m_w_down, m_final_norm_g, v_ada_mix_w, v_ada_mix_b, v_norm_mix_g, v_w_in, v_b_gate, v_ssd_conv_w, v_ssd_conv_b, v_ssd_dt_bias, v_ssd_a_log, v_ssd_d, v_ssd_norm_g, v_w_ssd_out, v_conf_conv_w, v_conf_conv_b, v_conf_ln_g, v_conf_ln_b, v_w_conf_out, v_sc_conv_w, v_w_sc_out, v_w_o, v_ada_ffn_w, v_ada_ffn_b, v_norm_ffn_g, v_w_up, v_ffn_conv_w, v_ffn_conv_b, v_w_down, v_final_norm_g):
    given = dict(locals())
    weights = {n: given[n] for n in WEIGHT_NAMES}
    moments_m = {n: given["m_" + n] for n in WEIGHT_NAMES}
    moments_v = {n: given["v_" + n] for n in WEIGHT_NAMES}
    return _step(x, c, loss_target, weights, moments_m, moments_v)
```

```python
import functools

import jax
import jax.numpy as jnp
from jax import lax
from jax.experimental import pallas as pl
from jax.experimental.pallas import tpu as pltpu

F32 = jnp.float32
BF16 = jnp.bfloat16
MESH = pl.DeviceIdType.MESH

N_DEV = 8
DEPTH = 2
D_MODEL = 1024
SSD_HEADS = 16
SSD_HEAD_DIM = 64
SSD_INNER = 1024
SSD_STATE = 64
SSD_CHUNK = 128
SSD_XBC = 1280
CONF_WIDTH = 512
CONF_KERNEL = 31
SC_WIDTH = 512
D_FF = 2816
EPS = 1e-6
OFF_Z, OFF_XBC, OFF_DT, OFF_CONF, OFF_SC, N_IN = 1024, 2304, 2320, 3344, 4880, 7952

ADAM_LR, ADAM_B1, ADAM_B2, ADAM_EPS, ADAM_WD, ADAM_STEP = 0.001, 0.9, 0.999, 1e-08, 0.01, 10

LANES = 128
SUBLANES = 8
VMEM_LIMIT = 56 * 1024 * 1024
ROW_TILE = 256
NORM_ROW_TILE = 512

NN = (((1,), (0,)), ((), ()))
NT = (((1,), (1,)), ((), ()))
TN = (((0,), (0,)), ((), ()))


def _params(n_axes):
    return pltpu.CompilerParams(dimension_semantics=("arbitrary",) * n_axes, vmem_limit_bytes=VMEM_LIMIT)


def _pc(body, **kw):
    return pl.pallas_call(body, **kw)


def _dot(a, b, dn=NN, precision=None):
    return lax.dot_general(a, b, dn, precision=precision, preferred_element_type=F32)


def _split3(x):
    hi = x.astype(BF16)
    r1 = x - hi.astype(F32)
    mid = r1.astype(BF16)
    return hi, mid, (r1 - mid.astype(F32)).astype(BF16)


def _dot_sel(x, sel):
    hi, mid, lo = _split3(x)
    return _dot(hi, sel) + _dot(mid, sel) + _dot(lo, sel)


def _sel_dot(sel, x):
    hi, mid, lo = _split3(x)
    return _dot(sel, hi) + _dot(sel, mid) + _dot(sel, lo)


def _sig(x):
    return 1.0 / (1.0 + jnp.exp(-x))


def _fold(v):
    r, c = v.shape
    return v.reshape(r // SUBLANES, SUBLANES, c).sum(axis=0)


def _tile(n_rows, target=ROW_TILE):
    return min(target, n_rows // 2)


STRIP_UNITS = 8
STRIP_ROW_TILE = 512


def _strip_units_per_trip(tl):
    return min(STRIP_UNITS, tl // SUBLANES)


def _bcast_row(ref, k, ls):
    return jnp.broadcast_to(ref[k:k + 1, ls], (SUBLANES, LANES))


def _unit_rows(t, u, nu):
    return pl.ds(pl.multiple_of((t * nu + u) * SUBLANES, SUBLANES), SUBLANES)


def _pair_rows(t, p, nu):
    return pl.ds(pl.multiple_of((t * nu + 2 * p) * SUBLANES, 2 * SUBLANES), 2 * SUBLANES)


def _strip_units(ref, ls, t, nu, nt, halo_prev=None, halo_next=None):
    units = [ref[_unit_rows(t, u, nu), ls] for u in range(nu)]
    if halo_prev is not None:
        before = pl.ds(pl.multiple_of(jnp.maximum(t * nu - 1, 0) * SUBLANES, SUBLANES), SUBLANES)
        units.insert(0, jnp.where(t > 0, ref[before, ls], halo_prev))
    if halo_next is not None:
        after = pl.ds(pl.multiple_of(jnp.minimum((t + 1) * nu, nt * nu - 1) * SUBLANES, SUBLANES), SUBLANES)
        units.append(jnp.where(t < nt - 1, ref[after, ls], halo_next))
    return units


def _row(tl, c, col=0):
    return pl.BlockSpec((tl, c), lambda i, col=col: (i, col))


def _prev(tl, hb, c, col=0):
    r = tl // hb
    return pl.BlockSpec((hb, c), lambda i, col=col: (jnp.maximum(i * r - 1, 0), col))


def _next(tl, hb, c, n_rows, col=0):
    r = tl // hb
    last = n_rows // hb - 1
    return pl.BlockSpec((hb, c), lambda i, col=col: (jnp.minimum((i + 1) * r, last), col))


def _const(shape):
    return pl.BlockSpec(shape, lambda i: (0,) * len(shape))


def _sds(shape, dtype=F32):
    return jax.ShapeDtypeStruct(shape, dtype)


MM_TILE = 1536
MM_TILE_ROWS = 2048
MM_FULL_K = 3072
MM_K_TILE = 1024


def _pick(dim, target):
    if dim <= target:
        return dim
    best = None
    for t in range(LANES, target + 1, LANES):
        if dim % t == 0:
            best = t
    assert best is not None, (dim, target)
    return best


def _matmul(a, b, mode, out_dtype, name):
    if mode == "nn":
        (m, k), (k2, n) = a.shape, b.shape
    elif mode == "nt":
        (m, k), (n, k2) = a.shape, b.shape
    else:
        (k, m), (k2, n) = a.shape, b.shape
    assert k == k2, (a.shape, b.shape, mode)
    tn = _pick(n, MM_TILE)
    tk = k if k <= MM_FULL_K else _pick(k, MM_K_TILE)
    tm = _pick(m, MM_TILE_ROWS if tn <= 1024 and tk <= 1024 else 1024)
    nk = k // tk
    dn = {"nn": NN, "nt": NT, "tn": TN}[mode]

    def body_one(a_ref, b_ref, o_ref):
        o_ref[...] = _dot(a_ref[...].astype(BF16), b_ref[...].astype(BF16), dn).astype(out_dtype)

    def body_acc(a_ref, b_ref, o_ref, acc):
        kk = pl.program_id(2)

        @pl.when(kk == 0)
        def _():
            acc[...] = jnp.zeros_like(acc)

        acc[...] += _dot(a_ref[...].astype(BF16), b_ref[...].astype(BF16), dn)

        @pl.when(kk == nk - 1)
        def _():
            o_ref[...] = acc[...].astype(out_dtype)

    a_spec = {"nn": pl.BlockSpec((tm, tk), lambda i, j, kk: (i, kk)),
              "nt": pl.BlockSpec((tm, tk), lambda i, j, kk: (i, kk)),
              "tn": pl.BlockSpec((tk, tm), lambda i, j, kk: (kk, i))}[mode]
    b_spec = {"nn": pl.BlockSpec((tk, tn), lambda i, j, kk: (kk, j)),
              "nt": pl.BlockSpec((tn, tk), lambda i, j, kk: (j, kk)),
              "tn": pl.BlockSpec((tk, tn), lambda i, j, kk: (kk, j))}[mode]
    o_spec = pl.BlockSpec((tm, tn), lambda i, j, kk: (i, j))
    return _pc(body_one if nk == 1 else body_acc, name=name, out_shape=_sds((m, n), out_dtype),
               grid=(m // tm, n // tn, nk), in_specs=[a_spec, b_spec], out_specs=o_spec,
               scratch_shapes=[] if nk == 1 else [pltpu.VMEM((tm, tn), F32)], compiler_params=_params(3))(a, b)


SUM_NT_TILE = 512


def _matmul_sum_nt(a_list, b_list, name):
    m, n = a_list[0].shape[0], b_list[0].shape[0]
    cnt = len(a_list)
    tm, tn = _pick(m, SUM_NT_TILE), _pick(n, SUM_NT_TILE)

    def body(*refs):
        a_refs, b_refs, o_ref = refs[:cnt], refs[cnt:2 * cnt], refs[2 * cnt]
        acc = _dot(a_refs[0][...].astype(BF16), b_refs[0][...].astype(BF16), NT)
        for t in range(1, cnt):
            acc = acc + _dot(a_refs[t][...].astype(BF16), b_refs[t][...].astype(BF16), NT)
        o_ref[...] = acc

    in_specs = [pl.BlockSpec((tm, a.shape[1]), lambda j, i: (i, 0)) for a in a_list]
    in_specs += [pl.BlockSpec((tn, b.shape[1]), lambda j, i: (j, 0)) for b in b_list]
    return _pc(body, name=name, out_shape=_sds((m, n)), grid=(n // tn, m // tm), in_specs=in_specs,
               out_specs=pl.BlockSpec((tm, tn), lambda j, i: (i, j)), compiler_params=_params(2))(*a_list, *b_list)


def _norm_mod(x, g, scale, shift):
    r = lax.rsqrt(jnp.mean(x * x, axis=-1, keepdims=True) + EPS)
    return ((x * r) * g) * (1.0 + scale) + shift


def _prenorm_first(x, g, scale, shift, name):
    n, d = x.shape
    tl = _tile(n, NORM_ROW_TILE)

    def body(x_ref, g_ref, sc_ref, sh_ref, h_ref):
        h_ref[...] = _norm_mod(x_ref[...], g_ref[...], sc_ref[...], sh_ref[...]).astype(BF16)

    return _pc(body, name=name, out_shape=_sds((n, d), BF16), grid=(n // tl,),
               in_specs=[_row(tl, d)] + [_const((1, d))] * 3, out_specs=_row(tl, d),
               compiler_params=_params(1))(x, g, scale, shift)


def _prenorm_res(x, y, gate, g, scale, shift, name):
    n, d = x.shape
    tl = _tile(n, NORM_ROW_TILE)

    def body(x_ref, y_ref, gate_ref, g_ref, sc_ref, sh_ref, xo_ref, h_ref):
        xn = x_ref[...] + gate_ref[...] * y_ref[...]
        xo_ref[...] = xn
        h_ref[...] = _norm_mod(xn, g_ref[...], sc_ref[...], sh_ref[...]).astype(BF16)

    return _pc(body, name=name, out_shape=(_sds((n, d)), _sds((n, d), BF16)), grid=(n // tl,),
               in_specs=[_row(tl, d), _row(tl, d)] + [_const((1, d))] * 4,
               out_specs=(_row(tl, d), _row(tl, d)), compiler_params=_params(1))(x, y, gate, g, scale, shift)


def _final_loss(x, y, gate, gf, target, name):
    n, d = x.shape
    tl = _tile(n, NORM_ROW_TILE)
    nb = n // tl

    def body(x_ref, y_ref, gate_ref, gf_ref, t_ref, loss_ref, dx_ref, dys_ref, dgate_ref, dgf_ref,
             acc_l, acc_gate, acc_gf):
        i = pl.program_id(0)

        @pl.when(i == 0)
        def _():
            acc_l[...] = jnp.zeros_like(acc_l)
            acc_gate[...] = jnp.zeros_like(acc_gate)
            acc_gf[...] = jnp.zeros_like(acc_gf)

        yv = y_ref[...]
        gate = gate_ref[...]
        gf = gf_ref[...]
        x2 = x_ref[...] + gate * yv
        r = lax.rsqrt(jnp.mean(x2 * x2, axis=-1, keepdims=True) + EPS)
        xn = x2 * r
        e = xn * gf - t_ref[...]
        acc_l[...] += _fold(e * e)
        dy = e * (1.0 / d)
        acc_gf[...] += _fold(dy * xn)
        dxn = dy * gf
        dx = r * (dxn - xn * jnp.mean(dxn * xn, axis=-1, keepdims=True))
        dx_ref[...] = dx
        dys_ref[...] = (dx * gate).astype(BF16)
        acc_gate[...] += _fold(dx * yv)

        @pl.when(i == nb - 1)
        def _():
            loss_ref[...] = jnp.full((SUBLANES, LANES), 0.5 / d, F32) * jnp.sum(acc_l[...])
            dgate_ref[...] = jnp.sum(acc_gate[...], axis=0, keepdims=True)
            dgf_ref[...] = jnp.sum(acc_gf[...], axis=0, keepdims=True)

    return _pc(body, name=name,
               out_shape=(_sds((SUBLANES, LANES)), _sds((n, d)), _sds((n, d), BF16), _sds((1, d)), _sds((1, d))),
               grid=(nb,),
               in_specs=[_row(tl, d), _row(tl, d), _const((1, d)), _const((1, d)), _row(tl, d)],
               out_specs=(_const((SUBLANES, LANES)), _row(tl, d), _row(tl, d), _const((1, d)), _const((1, d))),
               scratch_shapes=[pltpu.VMEM((SUBLANES, d), F32)] * 3,
               compiler_params=_params(1))(x, y, gate, gf, target)


def _norm_bwd(dh, x, dxo, g, scale, name, y_prev=None, gate_prev=None):
    n, d = x.shape
    tl = _tile(n, NORM_ROW_TILE)
    nb = n // tl
    has_prev = y_prev is not None

    def body(*refs):
        if has_prev:
            (dh_ref, x_ref, dxo_ref, g_ref, sc_ref, yp_ref, gp_ref,
             dx_ref, dsh_ref, dsc_ref, dg_ref, dys_ref, dgp_ref, acc_sh, acc_s, acc_gp) = refs
        else:
            (dh_ref, x_ref, dxo_ref, g_ref, sc_ref,
             dx_ref, dsh_ref, dsc_ref, dg_ref, acc_sh, acc_s) = refs
        i = pl.program_id(0)

        @pl.when(i == 0)
        def _():
            acc_sh[...] = jnp.zeros_like(acc_sh)
            acc_s[...] = jnp.zeros_like(acc_s)
            if has_prev:
                acc_gp[...] = jnp.zeros_like(acc_gp)

        x_ = x_ref[...]
        dh_ = dh_ref[...]
        g_ = g_ref[...]
        one_sc = 1.0 + sc_ref[...]
        r = lax.rsqrt(jnp.mean(x_ * x_, axis=-1, keepdims=True) + EPS)
        xn = x_ * r
        dxn = dh_ * (g_ * one_sc)
        dx = dxo_ref[...] + r * (dxn - xn * jnp.mean(dxn * xn, axis=-1, keepdims=True))
        dx_ref[...] = dx
        acc_sh[...] += _fold(dh_)
        acc_s[...] += _fold(dh_ * xn)
        if has_prev:
            dys_ref[...] = (dx * gp_ref[...]).astype(BF16)
            acc_gp[...] += _fold(dx * yp_ref[...])

        @pl.when(i == nb - 1)
        def _():
            s = jnp.sum(acc_s[...], axis=0, keepdims=True)
            dsh_ref[...] = jnp.sum(acc_sh[...], axis=0, keepdims=True)
            dsc_ref[...] = s * g_
            dg_ref[...] = s * one_sc
            if has_prev:
                dgp_ref[...] = jnp.sum(acc_gp[...], axis=0, keepdims=True)

    vec = _sds((1, d))
    in_specs = [_row(tl, d)] * 3 + [_const((1, d))] * 2
    out_shape = [_sds((n, d)), vec, vec, vec]
    out_specs = [_row(tl, d)] + [_const((1, d))] * 3
    scratch = [pltpu.VMEM((SUBLANES, d), F32)] * 2
    args = [dh, x, dxo, g, scale]
    if has_prev:
        in_specs += [_row(tl, d), _const((1, d))]
        out_shape += [_sds((n, d), BF16), vec]
        out_specs += [_row(tl, d), _const((1, d))]
        scratch += [pltpu.VMEM((SUBLANES, d), F32)]
        args += [y_prev, gate_prev]
    return _pc(body, name=name, out_shape=tuple(out_shape), grid=(nb,), in_specs=in_specs,
               out_specs=tuple(out_specs), scratch_shapes=scratch, compiler_params=_params(1))(*args)


CONV_HALO = 8
CONF_HALO = 32


def _ssd_pre(xbc, dt_raw, conv_w, conv_b, dt_bias, name):
    n, c = xbc.shape
    tl = _tile(n, STRIP_ROW_TILE)
    hb = CONV_HALO
    k_taps = 4

    nu = _strip_units_per_trip(tl)
    nt = tl // (nu * SUBLANES)

    def body(x_ref, xp_ref, dt_ref, w_ref, b_ref, dtb_ref, pre_ref, dts_ref):
        i = pl.program_id(0)
        row = lax.broadcasted_iota(jnp.int32, (SUBLANES, LANES), 0)
        for lc in range(c // LANES):
            ls = slice(lc * LANES, (lc + 1) * LANES)
            w = [_bcast_row(w_ref, k, ls) for k in range(k_taps)]
            b = _bcast_row(b_ref, 0, ls)
            xp0 = jnp.where(i > 0, xp_ref[:, ls], 0.0)

            def strip(t, carry):
                xs = _strip_units(x_ref, ls, t, nu, nt, halo_prev=xp0)
                for u in range(nu):
                    prev, cur = xs[u], xs[u + 1]
                    pre_ref[_unit_rows(t, u, nu), ls] = (
                        b + w[0] * _down(prev, cur, 3, row) + w[1] * _down(prev, cur, 2, row)
                        + w[2] * _down(prev, cur, 1, row) + w[3] * cur)
                return carry

            lax.fori_loop(0, nt, strip, 0)
        v = dt_ref[...] + dtb_ref[...]
        dts_ref[...] = jnp.maximum(v, 0.0) + jnp.log1p(jnp.exp(-jnp.abs(v)))

    return _pc(body, name=name, out_shape=(_sds((n, c)), _sds((n, LANES))), grid=(n // tl,),
               in_specs=[_row(tl, c), _prev(tl, hb, c), _row(tl, LANES), _const((SUBLANES, c)), _const((1, c)),
                         _const((1, LANES))],
               out_specs=(_row(tl, c), _row(tl, LANES)), compiler_params=_params(1))(
        xbc, xbc, dt_raw, conv_w, conv_b, dt_bias)


def _ssd_pre_bwd(dpre, xbc, ddt, dt_raw, conv_w, dt_bias, name):
    n, c = xbc.shape
    tl = _tile(n, STRIP_ROW_TILE)
    nb = n // tl
    hb = CONV_HALO
    k_taps = 4

    nu = _strip_units_per_trip(tl)
    nt = tl // (nu * SUBLANES)

    def body(dp_ref, dpn_ref, x_ref, xp_ref, ddt_ref, dt_ref, w_ref, dtb_ref,
             dx_ref, ddr_ref, dw_ref, db_ref, ddtb_ref, acc_w, acc_b, acc_dtb):
        i = pl.program_id(0)

        @pl.when(i == 0)
        def _():
            acc_w[...] = jnp.zeros_like(acc_w)
            acc_b[...] = jnp.zeros_like(acc_b)
            acc_dtb[...] = jnp.zeros_like(acc_dtb)

        row = lax.broadcasted_iota(jnp.int32, (SUBLANES, LANES), 0)
        zero = jnp.zeros((SUBLANES, LANES), F32)
        for lc in range(c // LANES):
            ls = slice(lc * LANES, (lc + 1) * LANES)
            w = [_bcast_row(w_ref, k, ls) for k in range(k_taps)]
            xp0 = jnp.where(i > 0, xp_ref[:, ls], 0.0)
            dpn0 = jnp.where(i < nb - 1, dpn_ref[:, ls], 0.0)

            def strip(t, carry):
                acc = list(carry)
                dps = _strip_units(dp_ref, ls, t, nu, nt, halo_next=dpn0)
                xs = _strip_units(x_ref, ls, t, nu, nt, halo_prev=xp0)
                dxs = []
                for u in range(nu):
                    d, dn = dps[u], dps[u + 1]
                    dxs.append(w[3] * d + w[2] * _up(d, dn, 1, row) + w[1] * _up(d, dn, 2, row)
                               + w[0] * _up(d, dn, 3, row))
                    prev, cur = xs[u], xs[u + 1]
                    acc[3] = acc[3] + d * cur
                    for k in range(3):
                        acc[k] = acc[k] + d * _down(prev, cur, 3 - k, row)
                    acc[4] = acc[4] + d
                for p in range(nu // 2):
                    dx_ref[_pair_rows(t, p, nu), ls] = jnp.concatenate(dxs[2 * p:2 * p + 2], axis=0).astype(BF16)
                return tuple(acc)

            res = lax.fori_loop(0, nt, strip, (zero,) * 5)
            for k in range(k_taps):
                acc_w[k, :, ls] += res[k]
            acc_b[:, ls] += res[4]
        ddr = ddt_ref[...] * _sig(dt_ref[...] + dtb_ref[...])
        ddr_ref[...] = ddr.astype(BF16)
        acc_dtb[...] += _fold(ddr)

        @pl.when(i == nb - 1)
        def _():
            dw_ref[...] = jnp.zeros_like(dw_ref)
            for k in range(k_taps):
                dw_ref[k:k + 1, :] = jnp.sum(acc_w[k], axis=0, keepdims=True)
            db_ref[...] = jnp.sum(acc_b[...], axis=0, keepdims=True)
            ddtb_ref[...] = jnp.sum(acc_dtb[...], axis=0, keepdims=True)

    return _pc(body, name=name,
               out_shape=(_sds((n, c), BF16), _sds((n, LANES), BF16), _sds((SUBLANES, c)), _sds((1, c)),
                          _sds((1, LANES))),
               grid=(nb,),
               in_specs=[_row(tl, c), _next(tl, hb, c, n), _row(tl, c), _prev(tl, hb, c), _row(tl, LANES),
                         _row(tl, LANES), _const((SUBLANES, c)), _const((1, LANES))],
               out_specs=(_row(tl, c), _row(tl, LANES), _const((SUBLANES, c)), _const((1, c)), _const((1, LANES))),
               scratch_shapes=[pltpu.VMEM((k_taps, SUBLANES, c), F32), pltpu.VMEM((SUBLANES, c), F32),
                               pltpu.VMEM((SUBLANES, LANES), F32)],
               compiler_params=_params(1))(dpre, dpre, xbc, xbc, ddt, dt_raw, conv_w, dt_bias)


def _expand_mat():
    r = lax.broadcasted_iota(jnp.int32, (LANES, SSD_INNER), 0)
    c = lax.broadcasted_iota(jnp.int32, (LANES, SSD_INNER), 1)
    return (jnp.right_shift(c, 6) == r).astype(BF16)


def _reduce_mat():
    r = lax.broadcasted_iota(jnp.int32, (SSD_INNER, LANES), 0)
    c = lax.broadcasted_iota(jnp.int32, (SSD_INNER, LANES), 1)
    return (jnp.right_shift(r, 6) == c).astype(BF16)


def _ssd_common(pre, dt, alog):
    q = SSD_CHUNK
    sg = _sig(pre)
    act = pre * sg
    lane = lax.broadcasted_iota(jnp.int32, (1, LANES), 1)
    a_neg = jnp.where(lane < SSD_HEADS, -jnp.exp(alog), 0.0)
    rr = lax.broadcasted_iota(jnp.int32, (q, q), 0)
    cc = lax.broadcasted_iota(jnp.int32, (q, q), 1)
    causal = rr >= cc
    cum = _sel_dot(causal.astype(BF16), dt * a_neg)
    e_mat = _expand_mat()
    dtx = _dot_sel(dt, e_mat)
    cumx = _dot_sel(cum, e_mat)
    return sg, act, a_neg, causal, cum, e_mat, dtx, cumx


def _ssd_scan(pre, dt, alog, dvec, name):
    n = pre.shape[0]
    q = SSD_CHUNK
    nc = n // q

    def body(pre_ref, dt_ref, alog_ref, d_ref, y_ref, hp_ref, state):
        i = pl.program_id(0)

        @pl.when(i == 0)
        def _():
            state[...] = jnp.zeros_like(state)

        dt_ = dt_ref[...]
        _, act, _, causal, cum, e_mat, dtx, cumx = _ssd_common(pre_ref[...], dt_, alog_ref[...])
        xs = act[:, :SSD_INNER]
        bm = act[:, SSD_INNER:SSD_INNER + LANES]
        cm = act[:, SSD_INNER + LANES:]
        cum_t = cum.T
        clx = cumx[q - 1:q, :]
        xc = xs * dtx
        xd = xc * jnp.exp(clx - cumx)
        doutx = jnp.exp(cumx)
        edec = jnp.exp(clx)
        dx_row = _dot_sel(jnp.broadcast_to(d_ref[...], (SUBLANES, LANES)), e_mat)[0:1, :]
        hp_ref[0] = state[...]
        bb = bm.astype(BF16)
        cb = cm.astype(BF16)
        lane = lax.broadcasted_iota(jnp.int32, (1, LANES), 1)
        row = lax.broadcasted_iota(jnp.int32, (LANES, 1), 0)
        cbs = []
        for g in range(2):
            cg = jnp.where(jnp.right_shift(lane, 6) == g, cm, 0.0).astype(BF16)
            cbs.append(_dot(cg, bb, NT))
        for j in range(SSD_HEADS // 2):
            sl = slice(j * LANES, (j + 1) * LANES)
            g = j // 4
            xcj = xc[:, sl].astype(BF16)
            halves = []
            for half in range(2):
                h = 2 * j + half
                seg = cum[:, h:h + 1] - cum_t[h:h + 1, :]
                w = cbs[g] * jnp.exp(jnp.where(causal, seg, -jnp.inf))
                halves.append(_dot(w.astype(BF16), xcj))
            y_diag = jnp.where(lane < SSD_HEAD_DIM, halves[0], halves[1])
            hj = state[:, sl]
            y_off = doutx[:, sl] * _dot(cb, hj.astype(BF16))
            y_ref[:, sl] = y_diag + y_off + xs[:, sl] * dx_row[:, sl]
            st = _dot(bb, xd[:, sl].astype(BF16), TN)
            state[:, sl] = hj * edec[:, sl] + jnp.where(jnp.right_shift(row, 6) == g, st, 0.0)

    return _pc(body, name=name, out_shape=(_sds((n, SSD_INNER)), _sds((nc, LANES, SSD_INNER))), grid=(nc,),
               in_specs=[_row(q, SSD_XBC), _row(q, LANES), _const((1, LANES)), _const((1, LANES))],
               out_specs=(_row(q, SSD_INNER), pl.BlockSpec((1, LANES, SSD_INNER), lambda i: (i, 0, 0))),
               scratch_shapes=[pltpu.VMEM((LANES, SSD_INNER), F32)], compiler_params=_params(1))(pre, dt, alog, dvec)


def _ssd_scan_bwd(pre, dt, hprev, dy, alog, dvec, name):
    n = pre.shape[0]
    q = SSD_CHUNK
    nc = n // q

    def body(pre_ref, dt_ref, hp_ref, dy_ref, alog_ref, d_ref, dpre_ref, ddt_ref, da_ref, dd_ref,
             d_state, dxc_s, dcx_s, dcl_s, acc_a, acc_d):
        i = pl.program_id(0)

        @pl.when(i == 0)
        def _():
            d_state[...] = jnp.zeros_like(d_state)
            acc_a[...] = jnp.zeros_like(acc_a)
            acc_d[...] = jnp.zeros_like(acc_d)

        pre_ = pre_ref[...]
        dt_ = dt_ref[...]
        sg, act, a_neg, causal, cum, e_mat, dtx, cumx = _ssd_common(pre_, dt_, alog_ref[...])
        r_mat = _reduce_mat()
        xs = act[:, :SSD_INNER]
        bm = act[:, SSD_INNER:SSD_INNER + LANES]
        cm = act[:, SSD_INNER + LANES:]
        cum_t = cum.T
        clx = cumx[q - 1:q, :]
        xc = xs * dtx
        dsx = jnp.exp(clx - cumx)
        doutx = jnp.exp(cumx)
        edec = jnp.exp(clx)
        dx_row = _dot_sel(jnp.broadcast_to(d_ref[...], (SUBLANES, LANES)), e_mat)[0:1, :]
        dy_ = dy_ref[...]
        acc_d[...] += _fold(dy_ * xs)
        bb = bm.astype(BF16)
        cb = cm.astype(BF16)
        lane = lax.broadcasted_iota(jnp.int32, (1, LANES), 1)
        row = lax.broadcasted_iota(jnp.int32, (LANES, 1), 0)
        d_c = jnp.zeros((q, LANES), F32)
        d_b = jnp.zeros((q, LANES), F32)

        for j in range(SSD_HEADS // 2):
            sl = slice(j * LANES, (j + 1) * LANES)
            g = j // 4
            hj = hp_ref[0, :, sl]
            hjb = hj.astype(BF16)
            dyj = dy_[:, sl]
            tj = _dot(cb, hjb)
            dtj = (doutx[:, sl] * dyj).astype(BF16)
            dcx = dyj * tj * doutx[:, sl]
            d_c = d_c + _dot(dtj, hjb, NT)
            dhn = d_state[:, sl]
            dhp = dhn * edec[:, sl] + jnp.where(jnp.right_shift(row, 6) == g, _dot(cb, dtj, TN), 0.0)
            dcl = jnp.sum(dhn * hj, axis=0, keepdims=True) * edec[:, sl]
            dsb = dhn.astype(BF16)
            dxd = _dot(bb, dsb)
            xcj = xc[:, sl]
            dsj = dsx[:, sl]
            d_b = d_b + _dot((xcj * dsj).astype(BF16), dsb, NT)
            dds = dxd * xcj * dsj
            d_state[:, sl] = dhp
            dxc_s[:, sl] = dxd * dsj
            dcx_s[:, sl] = dcx - dds
            dcl_s[:, sl] = jnp.broadcast_to(dcl + jnp.sum(dds, axis=0, keepdims=True), (SUBLANES, LANES))

        dcum_c = jnp.zeros((q, LANES), F32)
        dcum_t = jnp.zeros((LANES, q), F32)
        for g in range(2):
            gmask = jnp.right_shift(lane, 6) == g
            cg = jnp.where(gmask, cm, 0.0).astype(BF16)
            cbg = _dot(cg, bb, NT)
            d_cb = jnp.zeros((q, q), F32)
            for hh in range(SSD_HEADS // 2):
                h = g * (SSD_HEADS // 2) + hh
                j, half = h // 2, h % 2
                sl = slice(j * LANES, (j + 1) * LANES)
                hmask = jnp.right_shift(lane, 6) == half
                seg = cum[:, h:h + 1] - cum_t[h:h + 1, :]
                lm = jnp.exp(jnp.where(causal, seg, -jnp.inf))
                w = cbg * lm
                dyj = dy_[:, sl]
                dw = _dot(jnp.where(hmask, dyj, 0.0).astype(BF16), xc[:, sl].astype(BF16), NT)
                dxch = _dot(w.astype(BF16), dyj.astype(BF16), TN)
                dxc_s[:, sl] += jnp.where(hmask, dxch, 0.0)
                d_cb = d_cb + dw * lm
                m = dw * w
                dcum_c = dcum_c + jnp.sum(m, axis=1, keepdims=True) * (lane == h).astype(F32)
                dcum_t = dcum_t + (row == h).astype(F32) * jnp.sum(m, axis=0, keepdims=True)
            d_cbb = d_cb.astype(BF16)
            d_c = d_c + jnp.where(gmask, _dot(d_cbb, bb), 0.0)
            d_b = d_b + jnp.where(gmask, _dot(d_cbb, cb, TN), 0.0)

        dcl_row = _dot_sel(dcl_s[...], r_mat)[0:1, :]
        rowq = lax.broadcasted_iota(jnp.int32, (q, 1), 0)
        dcum = (dcum_c - dcum_t.T + _dot_sel(dcx_s[...], r_mat)
                + jnp.where(rowq == q - 1, dcl_row, 0.0))
        rr = lax.broadcasted_iota(jnp.int32, (q, q), 0)
        cc = lax.broadcasted_iota(jnp.int32, (q, q), 1)
        dadt = _sel_dot((rr <= cc).astype(BF16), dcum)
        dxc = dxc_s[...]
        ddt_ref[...] = dadt * a_neg + _dot_sel(dxc * xs, r_mat)
        acc_a[...] += _fold(dadt * dt_)
        dsilu = sg * (1.0 + pre_ * (1.0 - sg))
        dpre_ref[:, :SSD_INNER] = (dxc * dtx + dy_ * dx_row) * dsilu[:, :SSD_INNER]
        dpre_ref[:, SSD_INNER:SSD_INNER + LANES] = d_b * dsilu[:, SSD_INNER:SSD_INNER + LANES]
        dpre_ref[:, SSD_INNER + LANES:] = d_c * dsilu[:, SSD_INNER + LANES:]

        @pl.when(i == nc - 1)
        def _():
            da_ref[...] = jnp.sum(acc_a[...], axis=0, keepdims=True) * a_neg
            dd_ref[...] = jnp.sum(_dot_sel(acc_d[...], r_mat), axis=0, keepdims=True)

    rev = lambda i: (nc - 1 - i, 0)
    return _pc(body, name=name,
               out_shape=(_sds((n, SSD_XBC)), _sds((n, LANES)), _sds((1, LANES)), _sds((1, LANES))), grid=(nc,),
               in_specs=[pl.BlockSpec((q, SSD_XBC), rev), pl.BlockSpec((q, LANES), rev),
                         pl.BlockSpec((1, LANES, SSD_INNER), lambda i: (nc - 1 - i, 0, 0)),
                         pl.BlockSpec((q, SSD_INNER), rev), _const((1, LANES)), _const((1, LANES))],
               out_specs=(pl.BlockSpec((q, SSD_XBC), rev), pl.BlockSpec((q, LANES), rev), _const((1, LANES)),
                          _const((1, LANES))),
               scratch_shapes=[pltpu.VMEM((LANES, SSD_INNER), F32), pltpu.VMEM((q, SSD_INNER), F32),
                               pltpu.VMEM((q, SSD_INNER), F32), pltpu.VMEM((SUBLANES, SSD_INNER), F32),
                               pltpu.VMEM((SUBLANES, LANES), F32), pltpu.VMEM((SUBLANES, SSD_INNER), F32)],
               compiler_params=_params(1))(pre, dt, hprev, dy, alog, dvec)


def _group_norm_parts(v):
    half = SSD_INNER // 2
    r0 = lax.rsqrt(jnp.mean(v[:, :half] * v[:, :half], axis=-1, keepdims=True) + EPS)
    r1 = lax.rsqrt(jnp.mean(v[:, half:] * v[:, half:], axis=-1, keepdims=True) + EPS)
    lane = lax.broadcasted_iota(jnp.int32, (1, SSD_INNER), 1)
    return jnp.where(lane < half, r0, r1)


def _group_mean(v):
    half = SSD_INNER // 2
    m0 = jnp.mean(v[:, :half], axis=-1, keepdims=True)
    m1 = jnp.mean(v[:, half:], axis=-1, keepdims=True)
    lane = lax.broadcasted_iota(jnp.int32, (1, SSD_INNER), 1)
    return jnp.where(lane < half, m0, m1)


def _ssd_post(y, z, g, name):
    n, d = y.shape
    tl = _tile(n, NORM_ROW_TILE)

    def body(y_ref, z_ref, g_ref, o_ref):
        z_ = z_ref[...]
        v = y_ref[...] * (z_ * _sig(z_))
        o_ref[...] = ((v * _group_norm_parts(v)) * g_ref[...]).astype(BF16)

    return _pc(body, name=name, out_shape=_sds((n, d), BF16), grid=(n // tl,),
               in_specs=[_row(tl, d), _row(tl, d), _const((1, d))], out_specs=_row(tl, d),
               compiler_params=_params(1))(y, z, g)


def _ssd_post_bwd(dout, y, z, g, name):
    n, d = y.shape
    tl = _tile(n, NORM_ROW_TILE)
    nb = n // tl

    def body(do_ref, y_ref, z_ref, g_ref, dy_ref, dz_ref, dg_ref, acc_g):
        i = pl.program_id(0)

        @pl.when(i == 0)
        def _():
            acc_g[...] = jnp.zeros_like(acc_g)

        z_ = z_ref[...]
        y_ = y_ref[...]
        sz = _sig(z_)
        silu_z = z_ * sz
        v = y_ * silu_z
        rs = _group_norm_parts(v)
        nv = v * rs
        do_ = do_ref[...]
        acc_g[...] += _fold(do_ * nv)
        dn = do_ * g_ref[...]
        dv = rs * (dn - nv * _group_mean(dn * nv))
        dy_ref[...] = dv * silu_z
        dz_ref[...] = (dv * y_ * (sz * (1.0 + z_ * (1.0 - sz)))).astype(BF16)

        @pl.when(i == nb - 1)
        def _():
            dg_ref[...] = jnp.sum(acc_g[...], axis=0, keepdims=True)

    return _pc(body, name=name, out_shape=(_sds((n, d)), _sds((n, d), BF16), _sds((1, d))), grid=(nb,),
               in_specs=[_row(tl, d), _row(tl, d), _row(tl, d), _const((1, d))],
               out_specs=(_row(tl, d), _row(tl, d), _const((1, d))),
               scratch_shapes=[pltpu.VMEM((SUBLANES, d), F32)], compiler_params=_params(1))(dout, y, z, g)


def _layer_norm_parts(uc):
    mu = jnp.mean(uc, axis=-1, keepdims=True)
    xc = uc - mu
    rstd = lax.rsqrt(jnp.mean(xc * xc, axis=-1, keepdims=True) + EPS)
    return xc * rstd, rstd


def _rows_x8(w):
    return jnp.broadcast_to(w[:, None, :], (w.shape[0], SUBLANES, w.shape[1]))


def _glu_units(x_ref, ls, gl, t, nu, halo):
    nh = len(halo)
    units = []
    for h in range(nh):
        rows = pl.ds(pl.multiple_of(jnp.maximum(t * nu - nh + h, 0) * SUBLANES, SUBLANES), SUBLANES)
        units.append(jnp.where(t > 0, x_ref[rows, ls] * _sig(x_ref[rows, gl]), halo[h]))
    for u in range(nu):
        rows = _unit_rows(t, u, nu)
        units.append(x_ref[rows, ls] * _sig(x_ref[rows, gl]))
    return units


def _memo_rolls(units):
    memo = {}

    def rolls(key):
        if key not in memo:
            memo[key] = pltpu.roll(units[key[0]], key[1], 0)
        return memo[key]

    return rolls


def _window(units, e, s, rolls, row, up):
    a, b = divmod(s, SUBLANES)
    if b == 0:
        return units[e + a] if up else units[e - a]
    if up:
        sh = SUBLANES - b
        return jnp.where(row < sh, rolls((e + a, sh)), rolls((e + a + 1, sh)))
    return jnp.where(row < b, rolls((e - a - 1, b)), rolls((e - a, b)))


def _conf_fwd(conf_in, conv_w, conv_b, ln_g, ln_b, name):
    n = conf_in.shape[0]
    c = CONF_WIDTH
    tl = _tile(n, STRIP_ROW_TILE)
    hb = CONF_HALO
    k_taps = CONF_KERNEL

    nu = _strip_units_per_trip(tl)
    nt = tl // (nu * SUBLANES)
    nh = hb // SUBLANES

    def body(x_ref, xp_ref, w_ref, b_ref, g_ref, beta_ref, o_ref, uc_ref):
        i = pl.program_id(0)
        row = lax.broadcasted_iota(jnp.int32, (SUBLANES, LANES), 0)
        for lc in range(c // LANES):
            ls = slice(lc * LANES, (lc + 1) * LANES)
            gl = slice(c + lc * LANES, c + (lc + 1) * LANES)
            bias = _bcast_row(b_ref, 0, ls)
            halo = [jnp.where(i > 0, xp_ref[SUBLANES * h:SUBLANES * (h + 1), ls]
                              * _sig(xp_ref[SUBLANES * h:SUBLANES * (h + 1), gl]), 0.0) for h in range(nh)]

            def strip(t, carry):
                units = _glu_units(x_ref, ls, gl, t, nu, halo)
                rolls = _memo_rolls(units)
                for u in range(nu):
                    acc = bias
                    for k in range(k_taps):
                        acc = acc + w_ref[k, :, ls] * _window(units, u + nh, k_taps - 1 - k, rolls, row, up=False)
                    uc_ref[_unit_rows(t, u, nu), ls] = acc
                return carry

            lax.fori_loop(0, nt, strip, 0)
        nv, _ = _layer_norm_parts(uc_ref[...])
        v = nv * g_ref[...] + beta_ref[...]
        o_ref[...] = (v * _sig(v)).astype(BF16)

    return _pc(body, name=name, out_shape=(_sds((n, c), BF16), _sds((n, c))), grid=(n // tl,),
               in_specs=[_row(tl, 2 * c), _prev(tl, hb, 2 * c), _const((hb, SUBLANES, c)), _const((1, c)),
                         _const((1, c)), _const((1, c))],
               out_specs=(_row(tl, c), _row(tl, c)), compiler_params=_params(1))(
        conf_in, conf_in, _rows_x8(conv_w), conv_b, ln_g, ln_b)


def _conf_bwd(dout, uc, conf_in, conv_w, ln_g, ln_b, name):
    n = conf_in.shape[0]
    c = CONF_WIDTH
    tl = _tile(n, STRIP_ROW_TILE)
    nb = n // tl
    hb = CONF_HALO
    k_taps = CONF_KERNEL

    nu = _strip_units_per_trip(tl)
    nt = tl // (nu * SUBLANES)
    nh = hb // SUBLANES

    def body(do_ref, don_ref, uc_ref, ucn_ref, x_ref, xp_ref, w_ref, g_ref, beta_ref,
             dx_ref, dw_ref, db_ref, dg_ref, dbeta_ref, dbuf, acc_w, acc_b, acc_g, acc_beta):
        i = pl.program_id(0)

        @pl.when(i == 0)
        def _():
            acc_w[...] = jnp.zeros_like(acc_w)
            acc_b[...] = jnp.zeros_like(acc_b)
            acc_g[...] = jnp.zeros_like(acc_g)
            acc_beta[...] = jnp.zeros_like(acc_beta)

        g_ = g_ref[...]
        beta_ = beta_ref[...]

        def d_conv_out(do_, uc_):
            nv, rstd = _layer_norm_parts(uc_)
            v = nv * g_ + beta_
            sv = _sig(v)
            dv = do_ * (sv * (1.0 + v * (1.0 - sv)))
            dn = dv * g_
            duc = rstd * (dn - jnp.mean(dn, axis=-1, keepdims=True)
                          - nv * jnp.mean(dn * nv, axis=-1, keepdims=True))
            return duc, dv, nv

        duc, dv, nv = d_conv_out(do_ref[...], uc_ref[...])
        acc_g[...] += _fold(dv * nv)
        acc_beta[...] += _fold(dv)
        acc_b[...] += _fold(duc)
        dbuf[pl.ds(0, tl), :] = duc
        ducn, _, _ = d_conv_out(don_ref[...], ucn_ref[...])
        dbuf[pl.ds(tl, hb), :] = jnp.where(i < nb - 1, ducn, 0.0)

        row = lax.broadcasted_iota(jnp.int32, (SUBLANES, LANES), 0)
        for lc in range(c // LANES):
            ls = slice(lc * LANES, (lc + 1) * LANES)
            gl = slice(c + lc * LANES, c + (lc + 1) * LANES)
            halo = [jnp.where(i > 0, xp_ref[SUBLANES * h:SUBLANES * (h + 1), ls]
                              * _sig(xp_ref[SUBLANES * h:SUBLANES * (h + 1), gl]), 0.0) for h in range(nh)]

            def strip(t, carry):
                us = _glu_units(x_ref, ls, gl, t, nu, halo)
                ds = [dbuf[pl.ds(pl.multiple_of((t * nu + u) * SUBLANES, SUBLANES), SUBLANES), ls]
                      for u in range(nu + nh)]
                u_rolls, d_rolls = _memo_rolls(us), _memo_rolls(ds)
                for k in range(k_taps):
                    part = ds[0] * _window(us, nh, k_taps - 1 - k, u_rolls, row, up=False)
                    for u in range(1, nu):
                        part = part + ds[u] * _window(us, u + nh, k_taps - 1 - k, u_rolls, row, up=False)
                    acc_w[k, :, ls] += part
                dus = []
                for u in range(nu):
                    du = w_ref[0, :, ls] * _window(ds, u, k_taps - 1, d_rolls, row, up=True)
                    for k in range(1, k_taps):
                        du = du + w_ref[k, :, ls] * _window(ds, u, k_taps - 1 - k, d_rolls, row, up=True)
                    dus.append(du)
                for p in range(nu // 2):
                    rows = _pair_rows(t, p, nu)
                    du2 = jnp.concatenate(dus[2 * p:2 * p + 2], axis=0)
                    val, sgate = x_ref[rows, ls], _sig(x_ref[rows, gl])
                    dx_ref[rows, ls] = (du2 * sgate).astype(BF16)
                    dx_ref[rows, gl] = (du2 * val * sgate * (1.0 - sgate)).astype(BF16)
                return carry

            lax.fori_loop(0, nt, strip, 0)

        @pl.when(i == nb - 1)
        def _():
            dw_ref[...] = jnp.zeros_like(dw_ref)
            for k in range(k_taps):
                dw_ref[k:k + 1, :] = jnp.sum(acc_w[k], axis=0, keepdims=True)
            db_ref[...] = jnp.sum(acc_b[...], axis=0, keepdims=True)
            dg_ref[...] = jnp.sum(acc_g[...], axis=0, keepdims=True)
            dbeta_ref[...] = jnp.sum(acc_beta[...], axis=0, keepdims=True)

    vec = _sds((1, c))
    return _pc(body, name=name, out_shape=(_sds((n, 2 * c), BF16), _sds((hb, c)), vec, vec, vec), grid=(nb,),
               in_specs=[_row(tl, c), _next(tl, hb, c, n), _row(tl, c), _next(tl, hb, c, n), _row(tl, 2 * c),
                         _prev(tl, hb, 2 * c), _const((hb, SUBLANES, c)), _const((1, c)), _const((1, c))],
               out_specs=(_row(tl, 2 * c), _const((hb, c)), _const((1, c)), _const((1, c)), _const((1, c))),
               scratch_shapes=[pltpu.VMEM((tl + hb, c), F32),
                               pltpu.VMEM((k_taps, SUBLANES, c), F32), pltpu.VMEM((SUBLANES, c), F32),
                               pltpu.VMEM((SUBLANES, c), F32), pltpu.VMEM((SUBLANES, c), F32)],
               compiler_params=_params(1))(dout, dout, uc, uc, conf_in, conf_in, _rows_x8(conv_w), ln_g, ln_b)


def _sc_fwd(sc_in, conv_w, name):
    n = sc_in.shape[0]
    c = SC_WIDTH
    tl = _tile(n, STRIP_ROW_TILE)
    hb = CONV_HALO

    nu = _strip_units_per_trip(tl)
    nt = tl // (nu * SUBLANES)

    def body(x_ref, xp_ref, w_ref, o_ref):
        i = pl.program_id(0)
        row = lax.broadcasted_iota(jnp.int32, (SUBLANES, LANES), 0)
        for lc in range(c // LANES):
            ls = slice(lc * LANES, (lc + 1) * LANES)
            gc_ls = slice(c + lc * LANES, c + (lc + 1) * LANES)
            xv_ls = slice(2 * c + lc * LANES, 2 * c + (lc + 1) * LANES)
            w = [_bcast_row(w_ref, k, ls) for k in range(3)]
            gc0 = jnp.where(i > 0, xp_ref[:, gc_ls], 0.0)
            xv0 = xp_ref[:, xv_ls]

            def strip(t, carry):
                gcs = _strip_units(x_ref, gc_ls, t, nu, nt, halo_prev=gc0)
                xvs = _strip_units(x_ref, xv_ls, t, nu, nt, halo_prev=xv0)
                ps = [a * b for a, b in zip(gcs, xvs)]
                outs = []
                for u in range(nu):
                    prev, cur = ps[u], ps[u + 1]
                    cv = w[0] * _down(prev, cur, 2, row) + w[1] * _down(prev, cur, 1, row) + w[2] * cur
                    outs.append(x_ref[_unit_rows(t, u, nu), ls] * cv)
                for p in range(nu // 2):
                    o_ref[_pair_rows(t, p, nu), ls] = jnp.concatenate(outs[2 * p:2 * p + 2], axis=0).astype(BF16)
                return carry

            lax.fori_loop(0, nt, strip, 0)

    return _pc(body, name=name, out_shape=_sds((n, c), BF16), grid=(n // tl,),
               in_specs=[_row(tl, 3 * c), _prev(tl, hb, 3 * c), _const((SUBLANES, c))], out_specs=_row(tl, c),
               compiler_params=_params(1))(sc_in, sc_in, conv_w)


def _sc_bwd(dout, sc_in, conv_w, name):
    n = sc_in.shape[0]
    c = SC_WIDTH
    tl = _tile(n, STRIP_ROW_TILE)
    nb = n // tl
    hb = CONV_HALO

    nu = _strip_units_per_trip(tl)
    nt = tl // (nu * SUBLANES)

    def body(do_ref, don_ref, x_ref, xp_ref, xn_ref, w_ref, dx_ref, dw_ref, acc_w):
        i = pl.program_id(0)

        @pl.when(i == 0)
        def _():
            acc_w[...] = jnp.zeros_like(acc_w)

        row = lax.broadcasted_iota(jnp.int32, (SUBLANES, LANES), 0)
        zero = jnp.zeros((SUBLANES, LANES), F32)
        for lc in range(c // LANES):
            ls = slice(lc * LANES, (lc + 1) * LANES)
            gc_ls = slice(c + lc * LANES, c + (lc + 1) * LANES)
            xv_ls = slice(2 * c + lc * LANES, 2 * c + (lc + 1) * LANES)
            w = [_bcast_row(w_ref, k, ls) for k in range(3)]
            gc0 = jnp.where(i > 0, xp_ref[:, gc_ls], 0.0)
            xv0 = xp_ref[:, xv_ls]
            don0 = jnp.where(i < nb - 1, don_ref[:, ls], 0.0)
            gbn0 = xn_ref[:, ls]

            def strip(t, carry):
                acc = list(carry)
                dos = _strip_units(do_ref, ls, t, nu, nt, halo_next=don0)
                gbs = _strip_units(x_ref, ls, t, nu, nt, halo_next=gbn0)
                gcs = _strip_units(x_ref, gc_ls, t, nu, nt, halo_prev=gc0)
                xvs = _strip_units(x_ref, xv_ls, t, nu, nt, halo_prev=xv0)
                dcv = [a * b for a, b in zip(dos, gbs)]
                ps = [a * b for a, b in zip(gcs, xvs)]
                d_gb, d_gc, d_xv = [], [], []
                for u in range(nu):
                    prev, cur = ps[u], ps[u + 1]
                    p1, p2 = _down(prev, cur, 1, row), _down(prev, cur, 2, row)
                    d, dn = dcv[u], dcv[u + 1]
                    dp = w[2] * d + w[1] * _up(d, dn, 1, row) + w[0] * _up(d, dn, 2, row)
                    d_gb.append(dos[u] * (w[0] * p2 + w[1] * p1 + w[2] * cur))
                    d_gc.append(dp * xvs[u + 1])
                    d_xv.append(dp * gcs[u + 1])
                    acc[0] = acc[0] + d * p2
                    acc[1] = acc[1] + d * p1
                    acc[2] = acc[2] + d * cur
                for p in range(nu // 2):
                    rows = _pair_rows(t, p, nu)
                    for vals, lanes in ((d_gb, ls), (d_gc, gc_ls), (d_xv, xv_ls)):
                        dx_ref[rows, lanes] = jnp.concatenate(vals[2 * p:2 * p + 2], axis=0).astype(BF16)
                return tuple(acc)

            res = lax.fori_loop(0, nt, strip, (zero,) * 3)
            for k in range(3):
                acc_w[k, :, ls] += res[k]

        @pl.when(i == nb - 1)
        def _():
            dw_ref[...] = jnp.zeros_like(dw_ref)
            for k in range(3):
                dw_ref[k:k + 1, :] = jnp.sum(acc_w[k], axis=0, keepdims=True)

    return _pc(body, name=name, out_shape=(_sds((n, 3 * c), BF16), _sds((SUBLANES, c))), grid=(nb,),
               in_specs=[_row(tl, c), _next(tl, hb, c, n), _row(tl, 3 * c), _prev(tl, hb, 3 * c),
                         _next(tl, hb, c, n), _const((SUBLANES, c))],
               out_specs=(_row(tl, 3 * c), _const((SUBLANES, c))),
               scratch_shapes=[pltpu.VMEM((3, SUBLANES, c), F32)],
               compiler_params=_params(1))(dout, dout, sc_in, sc_in, sc_in, conv_w)


def _merge_fwd(gates, ya, yb, yc, b_gate, name):
    n, d = ya.shape
    tl = _tile(n, NORM_ROW_TILE)

    def body(gt_ref, ya_ref, yb_ref, yc_ref, b_ref, o_ref):
        gt = _sig(gt_ref[...] + b_ref[...])
        o_ref[...] = (gt[:, :d] * ya_ref[...] + gt[:, d:2 * d] * yb_ref[...] + gt[:, 2 * d:] * yc_ref[...]).astype(BF16)

    return _pc(body, name=name, out_shape=_sds((n, d), BF16), grid=(n // tl,),
               in_specs=[_row(tl, 3 * d), _row(tl, d), _row(tl, d), _row(tl, d), _const((1, 3 * d))],
               out_specs=_row(tl, d), compiler_params=_params(1))(gates, ya, yb, yc, b_gate)


def _merge_bwd(dm, gates, ya, yb, yc, b_gate, name):
    n, d = ya.shape
    tl = _tile(n, NORM_ROW_TILE)
    nb = n // tl

    def body(dm_ref, gt_ref, ya_ref, yb_ref, yc_ref, b_ref, dya_ref, dyb_ref, dyc_ref, dgt_ref, db_ref, acc):
        i = pl.program_id(0)

        @pl.when(i == 0)
        def _():
            acc[...] = jnp.zeros_like(acc)

        dm_ = dm_ref[...]
        gt = _sig(gt_ref[...] + b_ref[...])
        for idx, (y_ref, dy_ref) in enumerate(((ya_ref, dya_ref), (yb_ref, dyb_ref), (yc_ref, dyc_ref))):
            gk = gt[:, idx * d:(idx + 1) * d]
            dy_ref[...] = (dm_ * gk).astype(BF16)
            dpre = dm_ * y_ref[...] * gk * (1.0 - gk)
            dgt_ref[:, idx * d:(idx + 1) * d] = dpre.astype(BF16)
            acc[:, idx * d:(idx + 1) * d] += _fold(dpre)

        @pl.when(i == nb - 1)
        def _():
            db_ref[...] = jnp.sum(acc[...], axis=0, keepdims=True)

    bf = _sds((n, d), BF16)
    return _pc(body, name=name, out_shape=(bf, bf, bf, _sds((n, 3 * d), BF16), _sds((1, 3 * d))), grid=(nb,),
               in_specs=[_row(tl, d), _row(tl, 3 * d), _row(tl, d), _row(tl, d), _row(tl, d), _const((1, 3 * d))],
               out_specs=(_row(tl, d), _row(tl, d), _row(tl, d), _row(tl, 3 * d), _const((1, 3 * d))),
               scratch_shapes=[pltpu.VMEM((SUBLANES, 3 * d), F32)], compiler_params=_params(1))(
        dm, gates, ya, yb, yc, b_gate)


FFN_COLS = 1408
FFN_STRIP = 64
FFN_STRIP_BWD = 32


def _down(prev, cur, s, row):
    return jnp.where(row < s, pltpu.roll(prev, s, 0), pltpu.roll(cur, s, 0))


def _up(cur, nxt, s, row):
    return jnp.where(row < SUBLANES - s, pltpu.roll(cur, SUBLANES - s, 0), pltpu.roll(nxt, SUBLANES - s, 0))


def _ffn_mid(up, conv_w, conv_b, name):
    n = up.shape[0]
    tl = _tile(n, STRIP_ROW_TILE)
    hb = CONV_HALO
    tc = FFN_COLS
    ncb = D_FF // tc

    def spec(shape_rows, idx_fn, off):
        return pl.BlockSpec((shape_rows, tc), lambda j, i, off=off: (idx_fn(i), j + off))

    r = tl // hb
    cur = lambda i: i
    prv = lambda i: jnp.maximum(i * r - 1, 0)

    def body(g_ref, gp_ref, v_ref, vp_ref, wg_ref, wv_ref, bg_ref, bv_ref, o_ref):
        i = pl.program_id(1)
        row = lax.broadcasted_iota(jnp.int32, (SUBLANES, LANES), 0)
        full = lambda ref, k, ls: jnp.broadcast_to(ref[k:k + 1, ls], (SUBLANES, LANES))
        for lc in range(tc // LANES):
            ls = slice(lc * LANES, (lc + 1) * LANES)
            wg = [full(wg_ref, k, ls) for k in range(3)]
            wv = [full(wv_ref, k, ls) for k in range(3)]
            bg, bv = full(bg_ref, 0, ls), full(bv_ref, 0, ls)

            def conv(prev, x, w, b):
                return b + w[0] * _down(prev, x, 2, row) + w[1] * _down(prev, x, 1, row) + w[2] * x

            def strip(t, carry):
                gs, vs = [carry[0]], [carry[1]]
                for u in range(FFN_STRIP // SUBLANES):
                    rows = pl.ds(pl.multiple_of(t * FFN_STRIP + u * SUBLANES, SUBLANES), SUBLANES)
                    gs.append(g_ref[rows, ls])
                    vs.append(v_ref[rows, ls])
                outs = []
                for u in range(FFN_STRIP // SUBLANES):
                    ug = conv(gs[u], gs[u + 1], wg, bg)
                    outs.append(ug * _sig(ug) * conv(vs[u], vs[u + 1], wv, bv))
                for p in range(FFN_STRIP // 16):
                    rows = pl.ds(pl.multiple_of(t * FFN_STRIP + p * 16, 16), 16)
                    o_ref[rows, ls] = jnp.concatenate(outs[2 * p:2 * p + 2], axis=0).astype(BF16)
                return gs[-1], vs[-1]

            lax.fori_loop(0, tl // FFN_STRIP, strip,
                          (jnp.where(i > 0, gp_ref[:, ls], 0.0), jnp.where(i > 0, vp_ref[:, ls], 0.0)))

    wspec = lambda off: pl.BlockSpec((SUBLANES, tc), lambda j, i, off=off: (0, j + off))
    bspec = lambda off: pl.BlockSpec((1, tc), lambda j, i, off=off: (0, j + off))
    return _pc(body, name=name, out_shape=_sds((n, D_FF), BF16), grid=(ncb, n // tl),
               in_specs=[spec(tl, cur, 0), spec(hb, prv, 0), spec(tl, cur, ncb), spec(hb, prv, ncb),
                         wspec(0), wspec(ncb), bspec(0), bspec(ncb)],
               out_specs=pl.BlockSpec((tl, tc), lambda j, i: (i, j)), compiler_params=_params(2))(
        up, up, up, up, conv_w, conv_w, conv_b, conv_b)


def _ffn_mid_bwd(da, up, conv_w, conv_b, name):
    n = up.shape[0]
    tl = _tile(n, STRIP_ROW_TILE)
    nb = n // tl
    nt = tl // FFN_STRIP_BWD
    hb = CONV_HALO
    tc = FFN_COLS
    ncb = D_FF // tc
    r = tl // hb
    last = n // hb - 1
    cur = lambda i: i
    prv = lambda i: jnp.maximum(i * r - 1, 0)
    nxt = lambda i: jnp.minimum((i + 1) * r, last)

    def spec(shape_rows, idx_fn, off):
        return pl.BlockSpec((shape_rows, tc), lambda j, i, off=off: (idx_fn(i), j + off))

    def body(da_ref, dan_ref, g_ref, gp_ref, gn_ref, v_ref, vp_ref, vn_ref, wg_ref, wv_ref, bg_ref, bv_ref,
             dg_ref, dv_ref, dwg_ref, dwv_ref, dbg_ref, dbv_ref, acc_w, acc_b):
        i = pl.program_id(1)

        @pl.when(i == 0)
        def _():
            acc_w[...] = jnp.zeros_like(acc_w)
            acc_b[...] = jnp.zeros_like(acc_b)

        row = lax.broadcasted_iota(jnp.int32, (SUBLANES, LANES), 0)
        full = lambda ref, k, ls: jnp.broadcast_to(ref[k:k + 1, ls], (SUBLANES, LANES))
        zero = jnp.zeros((SUBLANES, LANES), F32)

        def d_conv_out(da_, ug, uv):
            s = _sig(ug)
            return da_ * uv * (s * (1.0 + ug * (1.0 - s))), da_ * (ug * s)

        for lc in range(tc // LANES):
            ls = slice(lc * LANES, (lc + 1) * LANES)
            wg = [full(wg_ref, k, ls) for k in range(3)]
            wv = [full(wv_ref, k, ls) for k in range(3)]
            bg, bv = full(bg_ref, 0, ls), full(bv_ref, 0, ls)

            def unit(prev_g, g, prev_v, v, da_):
                g1, g2 = _down(prev_g, g, 1, row), _down(prev_g, g, 2, row)
                v1, v2 = _down(prev_v, v, 1, row), _down(prev_v, v, 2, row)
                ug = bg + wg[0] * g2 + wg[1] * g1 + wg[2] * g
                uv = bv + wv[0] * v2 + wv[1] * v1 + wv[2] * v
                dug, duv = d_conv_out(da_, ug, uv)
                return dug, duv, (g2, g1, g), (v2, v1, v)

            def d_in(d, d_next, w):
                return w[2] * d + w[1] * _up(d, d_next, 1, row) + w[0] * _up(d, d_next, 2, row)

            tail = pl.ds(tl - SUBLANES, SUBLANES)
            dgn, dvn, _, _ = unit(g_ref[tail, ls], gn_ref[:, ls], v_ref[tail, ls], vn_ref[:, ls], dan_ref[:, ls])
            dgn = jnp.where(i < nb - 1, dgn, 0.0)
            dvn = jnp.where(i < nb - 1, dvn, 0.0)
            gp0 = jnp.where(i > 0, gp_ref[:, ls], 0.0)
            vp0 = jnp.where(i > 0, vp_ref[:, ls], 0.0)

            def strip(tt, carry):
                dgn, dvn = carry[0], carry[1]
                aw, ab = list(carry[2:8]), list(carry[8:10])
                t = nt - 1 - tt
                nu = FFN_STRIP_BWD // SUBLANES
                rm = pl.multiple_of(jnp.maximum(t * FFN_STRIP_BWD - SUBLANES, 0), SUBLANES)
                gs = [jnp.where(t > 0, g_ref[pl.ds(rm, SUBLANES), ls], gp0)]
                vs = [jnp.where(t > 0, v_ref[pl.ds(rm, SUBLANES), ls], vp0)]
                das = []
                for u in range(nu):
                    rows = pl.ds(pl.multiple_of(t * FFN_STRIP_BWD + u * SUBLANES, SUBLANES), SUBLANES)
                    gs.append(g_ref[rows, ls])
                    vs.append(v_ref[rows, ls])
                    das.append(da_ref[rows, ls])
                dgs, dvs = [None] * nu + [dgn], [None] * nu + [dvn]
                for u in reversed(range(nu)):
                    dgs[u], dvs[u], gsh, vsh = unit(gs[u], gs[u + 1], vs[u], vs[u + 1], das[u])
                    for k in range(3):
                        aw[k] = aw[k] + dgs[u] * gsh[k]
                        aw[3 + k] = aw[3 + k] + dvs[u] * vsh[k]
                    ab[0] = ab[0] + dgs[u]
                    ab[1] = ab[1] + dvs[u]
                for p in range(nu // 2):
                    rows = pl.ds(pl.multiple_of(t * FFN_STRIP_BWD + p * 16, 16), 16)
                    dg_ref[rows, ls] = jnp.concatenate([d_in(dgs[2 * p], dgs[2 * p + 1], wg),
                                                        d_in(dgs[2 * p + 1], dgs[2 * p + 2], wg)], axis=0).astype(BF16)
                    dv_ref[rows, ls] = jnp.concatenate([d_in(dvs[2 * p], dvs[2 * p + 1], wv),
                                                        d_in(dvs[2 * p + 1], dvs[2 * p + 2], wv)], axis=0).astype(BF16)
                return (dgs[0], dvs[0], *aw, *ab)

            res = lax.fori_loop(0, nt, strip, (dgn, dvn) + (zero,) * 8)
            for k in range(3):
                acc_w[0, k, :, ls] += res[2 + k]
                acc_w[1, k, :, ls] += res[5 + k]
            acc_b[0, :, ls] += res[8]
            acc_b[1, :, ls] += res[9]

        @pl.when(i == nb - 1)
        def _():
            for t, (dw_ref, db_ref) in enumerate(((dwg_ref, dbg_ref), (dwv_ref, dbv_ref))):
                dw_ref[...] = jnp.zeros_like(dw_ref)
                for k in range(3):
                    dw_ref[k:k + 1, :] = jnp.sum(acc_w[t, k], axis=0, keepdims=True)
                db_ref[...] = jnp.sum(acc_b[t], axis=0, keepdims=True)

    wspec = lambda off: pl.BlockSpec((SUBLANES, tc), lambda j, i, off=off: (0, j + off))
    bspec = lambda off: pl.BlockSpec((1, tc), lambda j, i, off=off: (0, j + off))
    ospec = lambda off: pl.BlockSpec((tl, tc), lambda j, i, off=off: (i, j + off))
    dg, dv, dwg, dwv, dbg, dbv = _pc(
        body, name=name,
        out_shape=(_sds((n, D_FF), BF16), _sds((n, D_FF), BF16), _sds((SUBLANES, D_FF)), _sds((SUBLANES, D_FF)),
                   _sds((1, D_FF)), _sds((1, D_FF))),
        grid=(ncb, nb),
        in_specs=[spec(tl, cur, 0), spec(hb, nxt, 0),
                  spec(tl, cur, 0), spec(hb, prv, 0), spec(hb, nxt, 0),
                  spec(tl, cur, ncb), spec(hb, prv, ncb), spec(hb, nxt, ncb),
                  wspec(0), wspec(ncb), bspec(0), bspec(ncb)],
        out_specs=(ospec(0), ospec(0), wspec(0), wspec(0), bspec(0), bspec(0)),
        scratch_shapes=[pltpu.VMEM((2, 3, SUBLANES, tc), F32), pltpu.VMEM((2, SUBLANES, tc), F32)],
        compiler_params=_params(2))(da, da, up, up, up, up, up, up, conv_w, conv_w, conv_b, conv_b)
    return dg, dv, jnp.concatenate([dwg, dwv], axis=1), jnp.concatenate([dbg, dbv], axis=1)


def _position():
    return lax.axis_index("x"), lax.axis_index("y"), lax.axis_index("c")


def _all_gather(locals_, name):
    n = len(locals_)

    def body(*refs):
        x_refs, out_refs = refs[:n], refs[n:2 * n]
        send_sems, recv_sems, local_sems = refs[2 * n:]
        x, y, cc = _position()
        me, sibling = (x, y, cc), (x, y, 1 - cc)
        chips = [(1 - x, y), (x, 1 - y), (1 - x, 1 - y)]

        def slot(a, px, py, pc):
            return out_refs[a].at[4 * px + 2 * py + pc]

        def copy(k, a, block, to, own=False):
            return pltpu.make_async_remote_copy(
                src_ref=x_refs[a] if own else slot(a, *block), dst_ref=slot(a, *block),
                send_sem=send_sems.at[k, a], recv_sem=recv_sems.at[k, a], device_id=to, device_id_type=MESH)

        mine = [pltpu.make_async_copy(x_refs[a], slot(a, *me), local_sems.at[a]) for a in range(n)]
        first = [copy(1 + j, a, me, (*chip, cc), own=True) for j, chip in enumerate(chips) for a in range(n)]
        first += [copy(0, a, me, sibling, own=True) for a in range(n)]
        for cp in mine + first:
            cp.start()
        passed = []
        for j, chip in enumerate(chips):
            for a in range(n):
                copy(1 + j, a, (*chip, cc), me).wait_recv()
                cp = copy(4 + j, a, (*chip, cc), sibling)
                cp.start()
                passed.append(cp)
        for a in range(n):
            copy(0, a, sibling, me).wait_recv()
        for j, chip in enumerate(chips):
            for a in range(n):
                copy(4 + j, a, (*chip, 1 - cc), me).wait_recv()
        for cp in first + passed:
            cp.wait_send()
        for cp in mine:
            cp.wait()

    hbm = pl.BlockSpec(memory_space=pl.ANY)
    return _pc(body, name=name, out_shape=[_sds((N_DEV,) + a.shape, a.dtype) for a in locals_],
               in_specs=[hbm] * n, out_specs=[hbm] * n,
               scratch_shapes=[pltpu.SemaphoreType.DMA((7, n)), pltpu.SemaphoreType.DMA((7, n)),
                               pltpu.SemaphoreType.DMA((n,))])(*locals_)


def _peers():
    x, y, cc = _position()
    others = []
    for fx, fy, fc in ((0, 0, 1), (1, 0, 0), (0, 1, 0), (1, 1, 0), (1, 0, 1), (0, 1, 1), (1, 1, 1)):
        p = (1 - x if fx else x, 1 - y if fy else y, 1 - cc if fc else cc)
        others.append((p, 4 * p[0] + 2 * p[1] + p[2]))
    return 4 * x + 2 * y + cc, others


def _gather_start(locals_, after, name):
    n = len(locals_)
    me, _ = _peers()
    lands = [lax.dynamic_update_slice(lax.empty((N_DEV,) + a.shape, a.dtype), a[None], (me,) + (0,) * a.ndim)
             for a in locals_]

    def body(*refs):
        x_refs, land_refs = refs[:n], refs[n:2 * n]
        send_sems, recv_sems, token = refs[2 * n + 1], refs[2 * n + 2], refs[-1]
        me_idx, others = _peers()
        for k, (peer, _) in enumerate(others):
            for a in range(n):
                pltpu.make_async_remote_copy(
                    src_ref=x_refs[a], dst_ref=land_refs[a].at[me_idx], send_sem=send_sems.at[k * n + a],
                    recv_sem=recv_sems.at[k * n + a], device_id=peer, device_id_type=MESH).start()
        token[...] = jnp.zeros_like(token)

    hbm = pl.BlockSpec(memory_space=pltpu.HBM)
    sem = pl.BlockSpec(memory_space=pltpu.SEMAPHORE)
    out = _pc(body, name=name,
              out_shape=(pltpu.SemaphoreType.DMA((7 * n,)), pltpu.SemaphoreType.DMA((7 * n,)),
                         *[pltpu.HBM(a.shape, a.dtype) for a in locals_], *[pltpu.HBM(l.shape, l.dtype) for l in lands],
                         _sds((SUBLANES, LANES))),
              in_specs=[hbm] * (2 * n) + [pl.BlockSpec(memory_space=pl.ANY)],
              out_specs=(sem, sem, *([hbm] * (2 * n)), pl.BlockSpec(memory_space=pltpu.VMEM)),
              input_output_aliases={i: 2 + i for i in range(2 * n)},
              compiler_params=pltpu.CompilerParams(has_side_effects=pltpu.SideEffectType.DATAFLOW_SIDE_EFFECTING))(
        *[pltpu.with_memory_space_constraint(a, pltpu.HBM) for a in locals_],
        *[pltpu.with_memory_space_constraint(l, pltpu.HBM) for l in lands], after)
    return (out[0], out[1], list(out[2:2 + n]), list(out[2 + n:2 + 2 * n])), out[-1]


def _gather_wait(state, after, name):
    send_sems, recv_sems, x_thru, land_thru = state
    n = len(x_thru)

    def body(*refs):
        x_refs, land_refs = refs[:n], refs[n:2 * n]
        send_sems, recv_sems = refs[2 * n], refs[2 * n + 1]
        _, others = _peers()
        for k, (peer, peer_idx) in enumerate(others):
            for a in range(n):
                cp = pltpu.make_async_remote_copy(
                    src_ref=x_refs[a], dst_ref=land_refs[a].at[peer_idx], send_sem=send_sems.at[k * n + a],
                    recv_sem=recv_sems.at[k * n + a], device_id=peer, device_id_type=MESH)
                cp.wait_send()
                cp.wait_recv()

    hbm = pl.BlockSpec(memory_space=pltpu.HBM)
    sem = pl.BlockSpec(memory_space=pltpu.SEMAPHORE)
    out = _pc(body, name=name, out_shape=tuple(pltpu.HBM(a.shape, a.dtype) for a in x_thru + land_thru),
              in_specs=[hbm] * (2 * n) + [sem, sem, pl.BlockSpec(memory_space=pl.ANY)], out_specs=tuple([hbm] * (2 * n)),
              input_output_aliases={i: i for i in range(2 * n)},
              compiler_params=pltpu.CompilerParams(has_side_effects=pltpu.SideEffectType.DATAFLOW_SIDE_EFFECTING))(
        *x_thru, *land_thru, send_sems, recv_sems, after)
    return list(out[n:])


N_CHIPS = 4


def _sibling_swap(parts, name):
    n = len(parts)

    def body(*refs):
        g_refs, got_refs = refs[:n], refs[n:2 * n]
        send_sems, recv_sems = refs[2 * n:]
        x, y, cc = _position()
        swaps = []
        for q in range(N_CHIPS):
            for a in range(n):
                swaps.append(pltpu.make_async_remote_copy(
                    src_ref=g_refs[a].at[2 * q + 1 - cc], dst_ref=got_refs[a].at[q], send_sem=send_sems.at[q, a],
                    recv_sem=recv_sems.at[q, a], device_id=(x, y, 1 - cc), device_id_type=MESH))
        for cp in swaps:
            cp.start()
        for cp in swaps:
            cp.wait_recv()
        for cp in swaps:
            cp.wait_send()

    hbm = pl.BlockSpec(memory_space=pl.ANY)
    return _pc(body, name=name, out_shape=[_sds((N_CHIPS,) + a.shape[1:], a.dtype) for a in parts],
               in_specs=[hbm] * n, out_specs=[hbm] * n,
               scratch_shapes=[pltpu.SemaphoreType.DMA((N_CHIPS, n)), pltpu.SemaphoreType.DMA((N_CHIPS, n))])(*parts)


def _pair_add(part, got, core, name):
    q, a, b = got.shape
    ta = _block_rows(a, b)

    def body(core_ref, k_ref, g_ref, o_ref):
        o_ref[...] = (k_ref[...].astype(F32) + g_ref[...].astype(F32)).astype(BF16)

    spec = pl.BlockSpec((None, ta, b), lambda c, i, core_ref: (c, i, 0))
    own = pl.BlockSpec((None, None, ta, b), lambda c, i, core_ref: (c, core_ref[0], i, 0))
    grid_spec = pltpu.PrefetchScalarGridSpec(num_scalar_prefetch=1, grid=(q, a // ta), in_specs=[own, spec],
                                             out_specs=spec)
    return _pc(body, name=name, out_shape=_sds(got.shape, BF16), grid_spec=grid_spec, compiler_params=_params(2))(
        core, part.reshape((N_CHIPS, 2) + part.shape[1:]), got)


def _chip_exchange(sums, name):
    n = len(sums)

    def body(*refs):
        g_refs, out_refs = refs[:n], refs[n:2 * n]
        send_sems, recv_sems = refs[2 * n:]
        me_q, others = _chip_peers()
        sends, recvs = [], []
        for k, (peer, peer_q) in enumerate(others):
            for a in range(n):
                sends.append(pltpu.make_async_remote_copy(
                    src_ref=g_refs[a].at[peer_q], dst_ref=out_refs[a].at[me_q], send_sem=send_sems.at[k, a],
                    recv_sem=recv_sems.at[k, a], device_id=peer, device_id_type=MESH))
                recvs.append(pltpu.make_async_remote_copy(
                    src_ref=g_refs[a].at[me_q], dst_ref=out_refs[a].at[peer_q], send_sem=send_sems.at[k, a],
                    recv_sem=recv_sems.at[k, a], device_id=peer, device_id_type=MESH))
        for cp in sends:
            cp.start()
        for cp in recvs:
            cp.wait_recv()
        for cp in sends:
            cp.wait_send()

    hbm = pl.BlockSpec(memory_space=pl.ANY)
    return _pc(body, name=name, out_shape=[_sds(a.shape, a.dtype) for a in sums],
               in_specs=[hbm] * n, out_specs=[hbm] * n,
               scratch_shapes=[pltpu.SemaphoreType.DMA((3, n)), pltpu.SemaphoreType.DMA((3, n))])(*sums)


def _chip_peers():
    x, y, cc = _position()
    others = []
    for fx, fy in ((1, 0), (0, 1), (1, 1)):
        px, py = (1 - x if fx else x), (1 - y if fy else y)
        others.append(((px, py, cc), 2 * px + py))
    return 2 * x + y, others


def _chip_exchange_start(sums, name):
    n = len(sums)
    lands = [lax.empty(a.shape, a.dtype) for a in sums]

    def body(*refs):
        g_refs, land_refs = refs[:n], refs[n:2 * n]
        send_sems, recv_sems, token = refs[2 * n], refs[2 * n + 1], refs[-1]
        me_q, others = _chip_peers()
        for k, (peer, peer_q) in enumerate(others):
            for a in range(n):
                pltpu.make_async_remote_copy(
                    src_ref=g_refs[a].at[peer_q], dst_ref=land_refs[a].at[me_q], send_sem=send_sems.at[k * n + a],
                    recv_sem=recv_sems.at[k * n + a], device_id=peer, device_id_type=MESH).start()
        token[...] = jnp.zeros_like(token)

    hbm = pl.BlockSpec(memory_space=pltpu.HBM)
    sem = pl.BlockSpec(memory_space=pltpu.SEMAPHORE)
    out = _pc(body, name=name,
              out_shape=(pltpu.SemaphoreType.DMA((3 * n,)), pltpu.SemaphoreType.DMA((3 * n,)),
                         *[pltpu.HBM(a.shape, a.dtype) for a in sums], *[pltpu.HBM(a.shape, a.dtype) for a in sums],
                         _sds((SUBLANES, LANES))),
              in_specs=[hbm] * (2 * n), out_specs=(sem, sem, *([hbm] * (2 * n)), pl.BlockSpec(memory_space=pltpu.VMEM)),
              input_output_aliases={i: 2 + i for i in range(2 * n)},
              compiler_params=pltpu.CompilerParams(has_side_effects=pltpu.SideEffectType.DATAFLOW_SIDE_EFFECTING))(
        *[pltpu.with_memory_space_constraint(a, pltpu.HBM) for a in sums],
        *[pltpu.with_memory_space_constraint(l, pltpu.HBM) for l in lands])
    return (out[0], out[1], list(out[2:2 + n]), list(out[2 + n:2 + 2 * n])), out[-1]


def _chip_exchange_wait(state, after, name):
    send_sems, recv_sems, g_thru, land_thru = state
    n = len(g_thru)

    def body(*refs):
        g_refs, land_refs = refs[:n], refs[n:2 * n]
        send_sems, recv_sems = refs[2 * n], refs[2 * n + 1]
        me_q, others = _chip_peers()
        for k, (peer, peer_q) in enumerate(others):
            for a in range(n):
                cp = pltpu.make_async_remote_copy(
                    src_ref=g_refs[a].at[me_q], dst_ref=land_refs[a].at[peer_q], send_sem=send_sems.at[k * n + a],
                    recv_sem=recv_sems.at[k * n + a], device_id=peer, device_id_type=MESH)
                cp.wait_send()
                cp.wait_recv()

    hbm = pl.BlockSpec(memory_space=pltpu.HBM)
    sem = pl.BlockSpec(memory_space=pltpu.SEMAPHORE)
    out = _pc(body, name=name, out_shape=tuple(pltpu.HBM(a.shape, a.dtype) for a in g_thru + land_thru),
              in_specs=[hbm] * (2 * n) + [sem, sem, pl.BlockSpec(memory_space=pl.ANY)], out_specs=tuple([hbm] * (2 * n)),
              input_output_aliases={i: i for i in range(2 * n)},
              compiler_params=pltpu.CompilerParams(has_side_effects=pltpu.SideEffectType.DATAFLOW_SIDE_EFFECTING))(
        *g_thru, *land_thru, send_sems, recv_sems, after)
    return list(out[:n]), list(out[n:])


def _block_rows(a, b):
    ta = a
    while ta * b > 256 * 1024 and ta % 32 == 0:
        ta //= 2
    return ta


def _reduce_adamw(parts, w, m, v, name):
    n_parts, s, a, b = parts.shape
    ta = _block_rows(a, b)

    def body(p_ref, w_ref, m_ref, v_ref, g_out, d_out, m_out, v_out):
        g = p_ref[0].astype(F32)
        for j in range(1, n_parts):
            g = g + p_ref[j].astype(F32)
        delta, m_new, v_new = _adamw(g, w_ref[...], m_ref[...], v_ref[...])
        g_out[...] = g
        d_out[...] = delta
        m_out[...] = m_new
        v_out[...] = v_new

    spec = pl.BlockSpec((None, ta, b), lambda l, i: (l, i, 0))
    return _pc(body, name=name, out_shape=(_sds((s, a, b)),) * 4, grid=(s, a // ta),
               in_specs=[pl.BlockSpec((n_parts, None, ta, b), lambda l, i: (0, l, i, 0)), spec, spec, spec],
               out_specs=(spec,) * 4, compiler_params=_params(2))(parts, w, m, v)


def _adamw(g, w, m, v):
    c1 = 1.0 - ADAM_B1 ** ADAM_STEP
    c2 = 1.0 - ADAM_B2 ** ADAM_STEP
    m_new = ADAM_B1 * m + (1.0 - ADAM_B1) * g
    v_new = ADAM_B2 * v + (1.0 - ADAM_B2) * (g * g)
    delta = -ADAM_LR * ((m_new / c1) / (jnp.sqrt(v_new / c2) + ADAM_EPS) + ADAM_WD * w)
    return delta, m_new, v_new


def _chip_reduce_adamw(own, recv, w, m, v, chip, name):
    s, a, b = w.shape
    ta = _block_rows(a, b)

    def body(chip_ref, *refs):
        p_refs, (w_ref, m_ref, v_ref), (g_out, d_out, m_out, v_out) = refs[:4 * s], refs[4 * s:4 * s + 3], refs[4 * s + 3:]
        layer = pl.program_id(0)
        g = None
        for l in range(s):
            gl = p_refs[4 * l][...].astype(F32)
            for j in range(1, N_CHIPS):
                gl = gl + p_refs[4 * l + j][...].astype(F32)
            g = gl if g is None else jnp.where(layer == l, gl, g)
        delta, m_new, v_new = _adamw(g, w_ref[...], m_ref[...], v_ref[...])
        g_out[...] = g
        d_out[...] = delta
        m_out[...] = m_new
        v_out[...] = v_new

    def part_spec(l, j):
        return pl.BlockSpec((None, ta, b), lambda layer, i, chip_ref, l=l, j=j: (
            (chip_ref[0] + j) % N_CHIPS, jnp.where(layer == l, i, 0), 0))

    spec = pl.BlockSpec((None, ta, b), lambda layer, i, chip_ref: (layer, i, 0))
    in_specs, args = [], []
    for l in range(s):
        for j in range(N_CHIPS):
            in_specs.append(part_spec(l, j))
            args.append(own[l] if j == 0 else recv[l])
    grid_spec = pltpu.PrefetchScalarGridSpec(num_scalar_prefetch=1, grid=(s, a // ta), in_specs=in_specs + [spec] * 3,
                                             out_specs=(spec,) * 4)
    return _pc(body, name=name, out_shape=(_sds((s, a, b)),) * 4, grid_spec=grid_spec, compiler_params=_params(2))(
        chip, *args, w, m, v)


MATRICES = (("ada_mix_w", 2), ("w_in", 2), ("w_ssd_out", 1), ("w_conf_out", 2), ("w_sc_out", 2), ("w_o", 1),
            ("ada_ffn_w", 2), ("w_up", 2), ("w_down", 1))
MIXER_MATRICES = ("ada_mix_w", "w_in", "w_ssd_out", "w_conf_out", "w_sc_out", "w_o")
FFN_MATRICES = ("ada_ffn_w", "w_up", "w_down")
CONV_WEIGHTS = (("ssd_conv_w", 2), ("conf_conv_w", 2), ("sc_conv_w", 2), ("ffn_conv_w", 2))
SHARDED = MATRICES + CONV_WEIGHTS
REPLICATED = ("ada_mix_b", "norm_mix_g", "b_gate", "ssd_conv_b", "ssd_dt_bias", "ssd_a_log", "ssd_d", "ssd_norm_g",
              "conf_conv_b", "conf_ln_g", "conf_ln_b", "ada_ffn_b", "norm_ffn_g", "ffn_conv_b", "final_norm_g")
WEIGHT_NAMES = ("ada_mix_w", "ada_mix_b", "norm_mix_g", "w_in", "b_gate", "ssd_conv_w", "ssd_conv_b", "ssd_dt_bias",
                "ssd_a_log", "ssd_d", "ssd_norm_g", "w_ssd_out", "conf_conv_w", "conf_conv_b", "conf_ln_g",
                "conf_ln_b", "w_conf_out", "sc_conv_w", "w_sc_out", "w_o", "ada_ffn_w", "ada_ffn_b", "norm_ffn_g",
                "w_up", "ffn_conv_w", "ffn_conv_b", "w_down", "final_norm_g")


def _pack_flat(arrays, cols, row_multiple, dtype):
    flat = jnp.concatenate([a.reshape(-1).astype(dtype) for a in arrays])
    rows = -(-flat.shape[0] // cols)
    rows = -(-rows // row_multiple) * row_multiple
    return jnp.pad(flat, (0, rows * cols - flat.shape[0])).reshape(rows, cols)


def _unpack_flat(flat2d, shapes):
    flat = flat2d.reshape(-1)
    out, off = [], 0
    for s in shapes:
        n = 1
        for d in s:
            n *= d
        out.append(flat[off:off + n].reshape(s))
        off += n
    return out


def _cols(g, lo, hi):
    b = g.shape[-1]
    pieces = []
    for k in range(N_DEV):
        a, e = max(lo, k * b), min(hi, (k + 1) * b)
        if a < e:
            pieces.append(g[k, :, a - k * b:e - k * b])
    return pieces[0] if len(pieces) == 1 else jnp.concatenate(pieces, axis=1)


def _rows(g):
    return g.reshape(N_DEV * g.shape[1], g.shape[2])


def _col_shards(segs, b):
    shards = []
    for k in range(N_DEV):
        lo, hi = k * b, (k + 1) * b
        pieces, off = [], 0
        for seg in segs:
            n = seg.shape[1]
            a, e = max(lo, off), min(hi, off + n)
            if a < e:
                pieces.append(seg[:, a - off:e - off])
            off += n
        shards.append(pieces[0] if len(pieces) == 1 else jnp.concatenate(pieces, axis=1))
    return jnp.stack(shards)


def _row_shards(full):
    return full.reshape(N_DEV, full.shape[0] // N_DEV, full.shape[1])


def _pad_rows(a, rows):
    return jnp.pad(a, ((0, rows - a.shape[0]), (0, 0)))


def _pad_lanes(a):
    return jnp.pad(a, ((0, 0), (0, LANES - a.shape[1])))


def _whole(g):
    return _cols(g, 0, N_DEV * g.shape[-1])


def _mixer_weights(full, i):
    row = lambda name: full[name][i].reshape(1, -1)
    conv = lambda name: _whole(full[name][:, i])
    w_in = full["w_in", i]
    return {
        "ada_mix_w": _whole(full["ada_mix_w", i]), "ada_mix_b": row("ada_mix_b"), "norm_mix_g": row("norm_mix_g"),
        "w_z": _cols(w_in, 0, OFF_Z), "w_xbc": _cols(w_in, OFF_Z, OFF_XBC),
        "w_dt": _pad_lanes(_cols(w_in, OFF_XBC, OFF_DT)), "w_conf": _cols(w_in, OFF_DT, OFF_CONF),
        "w_sc": _cols(w_in, OFF_CONF, OFF_SC), "w_gates": _cols(w_in, OFF_SC, N_IN),
        "b_gate": row("b_gate"),
        "ssd_conv_w": _pad_rows(conv("ssd_conv_w"), SUBLANES), "ssd_conv_b": row("ssd_conv_b"),
        "dt_bias": _pad_lanes(row("ssd_dt_bias")), "a_log": _pad_lanes(row("ssd_a_log")),
        "ssd_d": _pad_lanes(row("ssd_d")), "ssd_norm_g": row("ssd_norm_g"),
        "conf_conv_w": _pad_rows(conv("conf_conv_w"), CONF_HALO), "conf_conv_b": row("conf_conv_b"),
        "conf_ln_g": row("conf_ln_g"), "conf_ln_b": row("conf_ln_b"),
        "sc_conv_w": _pad_rows(conv("sc_conv_w"), SUBLANES),
    }


def _mixer_out_weights(full, i):
    return {"w_ssd_out": _rows(full["w_ssd_out", i]), "w_conf_out": _whole(full["w_conf_out", i]),
            "w_sc_out": _whole(full["w_sc_out", i]), "w_o": _rows(full["w_o", i])}


def _ffn_weights(full, i):
    row = lambda name: full[name][i].reshape(1, -1)
    return {
        "ada_ffn_w": _whole(full["ada_ffn_w", i]), "ada_ffn_b": row("ada_ffn_b"), "norm_ffn_g": row("norm_ffn_g"),
        "w_up": _whole(full["w_up", i]), "ffn_conv_w": _pad_rows(_whole(full["ffn_conv_w"][:, i]), SUBLANES),
        "ffn_conv_b": row("ffn_conv_b"), "w_down": _rows(full["w_down", i]),
    }


def _adaln(sc8, w, b, name):
    mod = _matmul(sc8, w, "nn", F32, name)[0:1, :] + b
    return mod[:, :D_MODEL], mod[:, D_MODEL:2 * D_MODEL], mod[:, 2 * D_MODEL:]


def _mixer_fwd(i, x, prev, sc8, wl, out_weights):
    t = f"l{i}_"
    s = {}
    shift, scale, gate = _adaln(sc8, wl["ada_mix_w"], wl["ada_mix_b"], t + "ada_mix")
    if prev is None:
        s["x_in"] = x
        s["h"] = _prenorm_first(x, wl["norm_mix_g"], scale, shift, t + "norm_mix")
    else:
        s["x_in"], s["h"] = _prenorm_res(x, prev[0], prev[1], wl["norm_mix_g"], scale, shift, t + "norm_mix")
    s["scale_mix"], s["gate_mix"] = scale, gate
    h = s["h"]
    s["z"] = _matmul(h, wl["w_z"], "nn", F32, t + "in_z")
    s["xbc"] = _matmul(h, wl["w_xbc"], "nn", F32, t + "in_xbc")
    s["dt_raw"] = _matmul(h, wl["w_dt"], "nn", F32, t + "in_dt")
    s["conf"] = _matmul(h, wl["w_conf"], "nn", F32, t + "in_conf")
    s["sc"] = _matmul(h, wl["w_sc"], "nn", F32, t + "in_sc")
    s["gates"] = _matmul(h, wl["w_gates"], "nn", F32, t + "in_gates")
    s["pre"], s["dt"] = _ssd_pre(s["xbc"], s["dt_raw"], wl["ssd_conv_w"], wl["ssd_conv_b"], wl["dt_bias"],
                                 t + "ssd_pre")
    s["y"], s["hprev"] = _ssd_scan(s["pre"], s["dt"], wl["a_log"], wl["ssd_d"], t + "ssd_scan")
    s["ya_in"] = _ssd_post(s["y"], s["z"], wl["ssd_norm_g"], t + "ssd_post")
    s["yb_in"], s["uc"] = _conf_fwd(s["conf"], wl["conf_conv_w"], wl["conf_conv_b"], wl["conf_ln_g"],
                                    wl["conf_ln_b"], t + "conf")
    s["yc_in"] = _sc_fwd(s["sc"], wl["sc_conv_w"], t + "sconv")
    wl = {**wl, **out_weights(s["yc_in"])}
    s["ya"] = _matmul(s["ya_in"], wl["w_ssd_out"], "nn", F32, t + "ssd_out")
    s["yb"] = _matmul(s["yb_in"], wl["w_conf_out"], "nn", F32, t + "conf_out")
    s["yc"] = _matmul(s["yc_in"], wl["w_sc_out"], "nn", F32, t + "sc_out")
    s["merged"] = _merge_fwd(s["gates"], s["ya"], s["yb"], s["yc"], wl["b_gate"], t + "merge")
    s["mix"] = _matmul(s["merged"], wl["w_o"], "nn", F32, t + "w_o")
    return s, wl


def _ffn_fwd(i, s, sc8, wl):
    t = f"l{i}_"
    shift2, scale2, gate2 = _adaln(sc8, wl["ada_ffn_w"], wl["ada_ffn_b"], t + "ada_ffn")
    s["x_mid"], s["h2"] = _prenorm_res(s["x_in"], s["mix"], s["gate_mix"], wl["norm_ffn_g"], scale2, shift2,
                                       t + "norm_ffn")
    s["scale_ffn"], s["gate_ffn"] = scale2, gate2
    s["up"] = _matmul(s["h2"], wl["w_up"], "nn", F32, t + "w_up")
    s["a"] = _ffn_mid(s["up"], wl["ffn_conv_w"], wl["ffn_conv_b"], t + "ffn_mid")
    s["out"] = _matmul(s["a"], wl["w_down"], "nn", F32, t + "w_down")
    return s


def _layer_bwd(i, s, wl, sc8, dys_ffn, dx_after, dgate_ffn, prev, emit=None):
    t = f"l{i}_b_"
    g = {}
    da = _matmul(dys_ffn, wl["w_down"], "nt", F32, t + "d_a")
    g["w_down"] = _matmul(s["a"], dys_ffn, "tn", BF16, t + "dw_down")
    dug, duv, dfw, g["ffn_conv_b"] = _ffn_mid_bwd(da, s["up"], wl["ffn_conv_w"], wl["ffn_conv_b"], t + "ffn_mid")
    g["ffn_conv_w"] = dfw[:3]
    dh2 = _matmul_sum_nt([dug, duv], [wl["w_up"][:, :D_FF], wl["w_up"][:, D_FF:]], t + "d_h2")
    g["w_up"] = [_matmul(s["h2"], dug, "tn", BF16, t + "dw_up_g"), _matmul(s["h2"], duv, "tn", BF16, t + "dw_up_v")]
    dx_mid, dshift2, dscale2, g["norm_ffn_g"], dys_mix, dgate_mix = _norm_bwd(
        dh2, s["x_mid"], dx_after, wl["norm_ffn_g"], s["scale_ffn"], t + "norm_ffn", s["mix"], s["gate_mix"])
    dmod_ffn = jnp.concatenate([dshift2, dscale2, dgate_ffn], axis=1)
    g["ada_ffn_b"] = dmod_ffn
    g["ada_ffn_w"] = [_matmul(sc8, _pad_rows(dmod_ffn, SUBLANES), "tn", BF16, t + "dw_ada_ffn")]
    token = emit(i, "ffn", g) if emit is not None else None
    if token is not None:
        wl = {**wl, "b_gate": wl["b_gate"] + token}
    dmerged = _matmul(dys_mix, wl["w_o"], "nt", F32, t + "d_merged")
    g["w_o"] = _matmul(s["merged"], dys_mix, "tn", BF16, t + "dw_o")
    dya, dyb, dyc, dgates, g["b_gate"] = _merge_bwd(dmerged, s["gates"], s["ya"], s["yb"], s["yc"], wl["b_gate"],
                                                    t + "merge")
    dya_in = _matmul(dya, wl["w_ssd_out"], "nt", F32, t + "d_ya_in")
    g["w_ssd_out"] = _matmul(s["ya_in"], dya, "tn", BF16, t + "dw_ssd_out")
    dyb_in = _matmul(dyb, wl["w_conf_out"], "nt", F32, t + "d_yb_in")
    g["w_conf_out"] = [_matmul(s["yb_in"], dyb, "tn", BF16, t + "dw_conf_out")]
    dyc_in = _matmul(dyc, wl["w_sc_out"], "nt", F32, t + "d_yc_in")
    g["w_sc_out"] = [_matmul(s["yc_in"], dyc, "tn", BF16, t + "dw_sc_out")]
    dy, dz, g["ssd_norm_g"] = _ssd_post_bwd(dya_in, s["y"], s["z"], wl["ssd_norm_g"], t + "ssd_post")
    dpre, ddt, da_log, dd = _ssd_scan_bwd(s["pre"], s["dt"], s["hprev"], dy, wl["a_log"], wl["ssd_d"],
                                          t + "ssd_scan")
    g["ssd_a_log"], g["ssd_d"] = da_log[:, :SSD_HEADS], dd[:, :SSD_HEADS]
    dxbc, ddt_raw, dcw, g["ssd_conv_b"], ddtb = _ssd_pre_bwd(dpre, s["xbc"], ddt, s["dt_raw"], wl["ssd_conv_w"],
                                                             wl["dt_bias"], t + "ssd_pre")
    g["ssd_conv_w"], g["ssd_dt_bias"] = dcw[:4], ddtb[:, :SSD_HEADS]
    dconf, dccw, g["conf_conv_b"], g["conf_ln_g"], g["conf_ln_b"] = _conf_bwd(
        dyb_in, s["uc"], s["conf"], wl["conf_conv_w"], wl["conf_ln_g"], wl["conf_ln_b"], t + "conf")
    g["conf_conv_w"] = dccw[:CONF_KERNEL]
    dsc, dscw = _sc_bwd(dyc_in, s["sc"], wl["sc_conv_w"], t + "sconv")
    g["sc_conv_w"] = dscw[:3]
    segs = (("z", dz, "w_z"), ("xbc", dxbc, "w_xbc"), ("dt", ddt_raw, "w_dt"), ("conf", dconf, "w_conf"),
            ("sc", dsc, "w_sc"), ("gates", dgates, "w_gates"))
    dw_segs = []
    for nm, dseg, wname in segs:
        dw = _matmul(s["h"], dseg, "tn", BF16, t + "dw_in_" + nm)
        dw_segs.append(dw[:, :SSD_HEADS] if nm == "dt" else dw)
    g["w_in"] = dw_segs
    token = emit(i, "mixer_early", g) if emit is not None else None
    if token is not None:
        wl = {**wl, "w_dt": wl["w_dt"] + token.astype(BF16)}
    dh = _matmul_sum_nt([dseg for _, dseg, _ in segs], [wl[wname] for _, _, wname in segs], t + "d_h")
    if prev is None:
        dx_in, dshift, dscale, g["norm_mix_g"] = _norm_bwd(dh, s["x_in"], dx_mid, wl["norm_mix_g"], s["scale_mix"],
                                                          t + "norm_mix")
        back = None
    else:
        dx_in, dshift, dscale, g["norm_mix_g"], dys_prev, dgate_prev = _norm_bwd(
            dh, s["x_in"], dx_mid, wl["norm_mix_g"], s["scale_mix"], t + "norm_mix", prev[0], prev[1])
        back = (dys_prev, dgate_prev)
    dmod_mix = jnp.concatenate([dshift, dscale, dgate_mix], axis=1)
    g["ada_mix_b"] = dmod_mix
    g["ada_mix_w"] = [_matmul(sc8, _pad_rows(dmod_mix, SUBLANES), "tn", BF16, t + "dw_ada_mix")]
    return g, dx_in, back


def _device_step(x, c, target, full, fetch=None, emit=None):
    fetch = fetch or {}
    full = dict(full)
    sc8 = _pad_rows(c * (1.0 / (1.0 + jnp.exp(-c))), SUBLANES)
    wls, saved, prev, xcur = [], [], None, x
    for i in range(DEPTH):
        if (i, "mixer") in fetch:
            full.update(fetch[i, "mixer"](prev[0]))
        def out_weights(act, i=i):
            if (i, "out") in fetch:
                full.update(fetch[i, "out"](act))
            return _mixer_out_weights(full, i)

        s, wl = _mixer_fwd(i, xcur, prev, sc8, _mixer_weights(full, i), out_weights)
        if (i, "ffn") in fetch:
            full.update(fetch[i, "ffn"](s["mix"]))
        wf = _ffn_weights(full, i)
        _ffn_fwd(i, s, sc8, wf)
        wls.append({**wl, **wf})
        saved.append(s)
        xcur, prev = s["x_mid"], (s["out"], s["gate_ffn"])
    gf = full["final_norm_g"].reshape(1, -1)
    last = saved[-1]
    loss, dx, dys, dgate, dgf = _final_loss(last["x_mid"], last["out"], last["gate_ffn"], gf, target, "final_loss")
    grads = [None] * DEPTH
    for i in reversed(range(DEPTH)):
        prev = None if i == 0 else (saved[i - 1]["out"], saved[i - 1]["gate_ffn"])
        grads[i], dx, back = _layer_bwd(i, saved[i], wls[i], sc8, dys, dx, dgate, prev, emit)
        token = emit(i, "mixer", grads[i]) if emit is not None else None
        if token is not None:
            wls[i - 1] = {**wls[i - 1], "ffn_conv_b": wls[i - 1]["ffn_conv_b"] + token}
        if back is not None:
            dys, dgate = back
    return loss[0, 0], dx, grads, dgf


def _step(x, c, target, weights, moments_m, moments_v):
    sharded_names = [n for n, _ in SHARDED]
    conv_names = [n for n, _ in CONV_WEIGHTS]
    shard = lambda key: weights[key[0]][key[1]].astype(BF16)
    first = [("ada_mix_w", 0), ("w_in", 0)]
    later = {(0, "out"): [(n, 0) for n in MIXER_MATRICES if (n, 0) not in first],
             (0, "ffn"): [(n, 0) for n in FFN_MATRICES], (1, "mixer"): [(n, 1) for n in MIXER_MATRICES + FFN_MATRICES]}
    gathered = _all_gather([shard(k) for k in first] + [weights[n] for n in conv_names], "gather_first")
    full = {n: weights[n] for n in REPLICATED}
    full.update(zip(first + conv_names, gathered))
    fetch, after = {}, gathered[0]
    for stage, keys in later.items():
        state, after = _gather_start([shard(k) for k in keys], after, f"gather_l{stage[0]}_{stage[1]}_start")
        fetch[stage] = functools.partial(
            lambda act, state, keys, nm: dict(zip(keys, _gather_wait(state, act, nm))),
            state=state, keys=keys, nm=f"gather_l{stage[0]}_{stage[1]}_wait")
    axis_of = dict(SHARDED)
    core = lax.axis_index("c").astype(jnp.int32).reshape(1)
    chip = (2 * lax.axis_index("x") + lax.axis_index("y")).astype(jnp.int32).reshape(1)
    ffn_names = list(FFN_MATRICES) + ["ffn_conv_w"]
    early_names = [n for n in sharded_names if n not in ffn_names and n != "ada_mix_w"]
    sums, received, pending = {}, {}, []

    def send(i, names, grads_i, last, tag):
        keys = [(n, i) for n in names]
        parts = []
        for n in names:
            gw = grads_i[n]
            part = _row_shards(gw) if axis_of[n] == 1 else _col_shards(gw if isinstance(gw, list) else [gw],
                                                                       weights[n].shape[-1])
            parts.append(part.astype(BF16))
        got = _sibling_swap(parts, "swap_grads_" + tag)
        pair = [_pair_add(p, g, core, f"pair_add_{n}_{i}") for n, p, g in zip(names, parts, got)]
        if last:
            sums.update(zip(keys, pair))
            received.update(zip(keys, _chip_exchange(pair, "exchange_grads_" + tag)))
            return None
        state, token = _chip_exchange_start(pair, "exchange_grads_" + tag + "_start")
        pending.append((keys, state, tag))
        return token[0:1, 0:1]

    def emit(i, kind, grads_i):
        if i == 0 and kind == "ffn":
            return send(0, ffn_names, grads_i, False, "l0_ffn")
        if i == 0 and kind == "mixer_early":
            return send(0, early_names, grads_i, False, "l0_mixer")
        if i == 0 and kind == "mixer":
            return send(0, ["ada_mix_w"], grads_i, True, "l0_ada")
        if kind == "mixer":
            return send(i, sharded_names, grads_i, False, f"l{i}_all")
        return None

    loss, grad_x, grads, dgf = _device_step(x[0], c + after[0:1, 0:1], target[0], full, fetch, emit)
    for keys, state, tag in pending:
        own, got = _chip_exchange_wait(state, grad_x, "exchange_grads_" + tag + "_wait")
        sums.update(zip(keys, own))
        received.update(zip(keys, got))
    big = {n: _chip_reduce_adamw([sums[n, i] for i in range(DEPTH)], [received[n, i] for i in range(DEPTH)],
                                 weights[n], moments_m[n], moments_v[n], chip, "adamw_" + n)
           for n in sharded_names}
    rep_grads = [dgf if n == "final_norm_g" else jnp.stack([grads[i][n].reshape(-1) for i in range(DEPTH)])
                 for n in REPLICATED]
    small_parts, = _all_gather([_pack_flat(rep_grads, LANES, SUBLANES, F32)], "gather_small_grads")
    pack_s = lambda d: _pack_flat([d[n] for n in REPLICATED], LANES, SUBLANES, F32)[None]
    small = _reduce_adamw(small_parts[:, None], pack_s(weights), pack_s(moments_m), pack_s(moments_v),
                          "adamw_replicated")
    small = [_unpack_flat(b, [weights[n].shape for n in REPLICATED]) for b in small]
    results = []
    for kind in range(4):
        by_name = {n: big[n][kind] for n in sharded_names}
        by_name.update(zip(REPLICATED, small[kind]))
        results.append([by_name[n] for n in WEIGHT_NAMES])
    loss = lax.psum(loss, ("x", "y", "c"))
    return (loss, grad_x[None], *results[0], *results[1], *results[2], *results[3])


def kernel(x, c, ada_mix_w, ada_mix_b, norm_mix_g, w_in, b_gate, ssd_conv_w, ssd_conv_b, ssd_dt_bias, ssd_a_log, ssd_d, ssd_norm_g, w_ssd_out, conf_conv_w, conf_conv_b, conf_ln_g, conf_ln_b, w_conf_out, sc_conv_w, w_sc_out, w_o, ada_ffn_w, ada_ffn_b, norm_ffn_g, w_up, ffn_conv_w, ffn_conv_b, w_down, final_norm_g, loss_target, m_ada_mix_w, m_ada_mix_b, m_norm_mix_g, m_w_in, m_b_gate, m_ssd_conv_w, m_ssd_conv_b, m_ssd_dt_bias, m_ssd_a_log, m_ssd_d, m_ssd_norm_g, m_w_ssd_out, m_conf_conv_w, m_conf_conv_b, m_conf_ln_g, m_conf_ln_b, m_w_conf_out, m_sc_conv_w, m_w_sc_out, m_w_o, m_ada_ffn_w, m_ada_ffn_b, m_norm_ffn_g, m_w_up, m_ffn_conv_w, m_ffn_conv_b, m_w_down, m_final_norm_g, v_ada_mix_w, v_ada_mix_b, v_norm_mix_g, v_w_in, v_b_gate, v_ssd_conv_w, v_ssd_conv_b, v_ssd_dt_bias, v_ssd_a_log, v_ssd_d, v_ssd_norm_g, v_w_ssd_out, v_conf_conv_w, v_conf_conv_b, v_conf_ln_g, v_conf_ln_b, v_w_conf_out, v_sc_conv_w, v_w_sc_out, v_w_o, v_ada_ffn_w, v_ada_ffn_b, v_norm_ffn_g, v_w_up, v_ffn_conv_w, v_ffn_conv_b, v_w_down, v_final_norm_g):
    given = dict(locals())
    weights = {n: given[n] for n in WEIGHT_NAMES}
    moments_m = {n: given["m_" + n] for n in WEIGHT_NAMES}
    moments_v = {n: given["v_" + n] for n in WEIGHT_NAMES}
    return _step(x, c, loss_target, weights, moments_m, moments_v)
```
